```python
import math
import jax
import jax.numpy as jnp
from jax import lax
import numpy as np

D_MODEL = 1024
BATCH = 32
SEQ = 2048
DEPTH = 2

D_MIX = D_MODEL
D_GROUP = D_MIX // 4

HY_PROJ = 3 * D_GROUP
HY_SHORT = 3
HY_POS_DIM = 33
HY_FILT_HIDDEN = 64
HY_FAST_DECAY_PCT = 0.3
HY_SLOW_DECAY_PCT = 1.5
HY_DECAY_TARGET = 1e-2

M_HEADS = 4
M_HEADDIM = D_GROUP // M_HEADS
M_STATE = 64
M_GROUPS = 2
M_CONV = 5
M_CHUNK = 128
M_XBC = D_GROUP + 2 * M_GROUPS * M_STATE
M_PROJ = D_GROUP + M_XBC + 2 * M_HEADS

A_HEADS = 4
A_HEADDIM = D_GROUP // A_HEADS
A_PATTERNS = ((128, 1), (512, 4), (2048, 16))
A_PROJ = 3 * D_GROUP
N_BUCKETS = 32
MAX_DISTANCE = 1024

H_HEADS = 4
H_EXPAND = D_GROUP // H_HEADS
H_HEADDIM = D_GROUP // H_HEADS
H_CHUNK = 32
H_PROJ = 5 * D_GROUP

N_IN = HY_PROJ + M_PROJ + A_PROJ + H_PROJ

N_EXPERTS = 16
EC_CAPACITY = 2
D_FF_EXPERT = 1024

EPS = 1e-6
F32 = jnp.float32

kernel_name = "hybrid_parallel_heads_encoder"


def rmsnorm(x, w):
    xf = x.astype(F32)
    r = lax.rsqrt(jnp.mean(xf * xf, axis=-1, keepdims=True) + EPS)
    return (xf * r).astype(x.dtype) * w


def centered_dwconv(u, w):
    k = w.shape[0]
    c = u.shape[-1]
    return lax.conv_general_dilated(
        u, w[:, None, :].astype(u.dtype), window_strides=(1,),
        padding=((k // 2, k // 2),), dimension_numbers=("NWC", "WIO", "NWC"),
        feature_group_count=c)


def hyena_filters(L, w1, b1, w2, b2, freq, w3):
    t = jnp.linspace(0.0, 1.0, L, dtype=F32)[:, None]
    bands = (HY_POS_DIM - 1) // 2
    ang_pos = 2.0 * math.pi * jnp.arange(L, dtype=F32) / L
    f = jnp.linspace(1e-4, bands - 1, bands, dtype=F32)
    ang = ang_pos[:, None] * f[None, :]
    z = jnp.concatenate([t, jnp.cos(ang), -jnp.sin(ang)], axis=-1)
    freq = freq.astype(F32)
    h = jnp.sin(freq * (z @ w1.astype(F32) + b1.astype(F32)))
    h = jnp.sin(freq * (h @ w2.astype(F32) + b2.astype(F32)))
    h = h @ w3.astype(F32)
    max_decay = math.log(HY_DECAY_TARGET) / HY_FAST_DECAY_PCT
    min_decay = math.log(HY_DECAY_TARGET) / HY_SLOW_DECAY_PCT
    deltas = jnp.abs(jnp.linspace(min_decay, max_decay, D_GROUP, dtype=F32))
    decay = jnp.exp(-t * deltas[None, :])
    h = h.reshape(L, 2, D_GROUP) * decay[:, None, :]
    return h[:, 0], h[:, 1]


def bidir_fftconv(u, h_fwd, h_bwd, bias):
    L = u.shape[1]
    k = jnp.concatenate([h_fwd, jnp.zeros_like(h_fwd[:1]), h_bwd[:0:-1]], axis=0)
    kf = jnp.fft.rfft(k, n=2 * L, axis=0)
    uf = jnp.fft.rfft(u, n=2 * L, axis=1)
    y = jnp.fft.irfft(uf * kf[None], n=2 * L, axis=1)[:, :L]
    return y + u * bias.astype(F32)


def hyena_mixer(p, conv_w, w1, b1, w2, b2, freq, w3, fbias):
    L = p.shape[1]
    u = centered_dwconv(p, conv_w).astype(F32)
    x0, x1, v = jnp.split(u, 3, axis=-1)
    hf, hb = hyena_filters(L, w1, b1, w2, b2, freq, w3)
    y = x0 * bidir_fftconv(v * x1, hf, hb, fbias)
    return y.astype(p.dtype)


def ssd_chunked(x, dt, A, Bm, Cm):
    b, L, h, pd = x.shape
    n = Bm.shape[-1]
    nc = L // M_CHUNK
    xd = (x * dt[..., None]).reshape(b, nc, M_CHUNK, h, pd)
    Bc = Bm.reshape(b, nc, M_CHUNK, h, n)
    Cc = Cm.reshape(b, nc, M_CHUNK, h, n)
    a_cum = jnp.cumsum((dt * A).reshape(b, nc, M_CHUNK, h), axis=2)
    seg = a_cum[:, :, :, None, :] - a_cum[:, :, None, :, :]
    causal = jnp.tril(jnp.ones((M_CHUNK, M_CHUNK), bool))[None, None, :, :, None]
    Lmat = jnp.exp(jnp.where(causal, seg, -jnp.inf))
    cb = jnp.einsum("bclhn,bcshn->bclsh", Cc, Bc) * Lmat
    y_diag = jnp.einsum("bclsh,bcshp->bclhp", cb, xd)
    decay_to_end = jnp.exp(a_cum[:, :, -1:, :] - a_cum)
    chunk_states = jnp.einsum("bclhn,bclhp->bchpn", Bc, xd * decay_to_end[..., None])
    chunk_decay = jnp.exp(a_cum[:, :, -1, :])

    def step(S, inp):
        st, dec = inp
        return S * dec[..., None, None] + st, S

    S0 = jnp.zeros((b, h, pd, n), F32)
    _, prev = lax.scan(step, S0, (jnp.moveaxis(chunk_states, 1, 0), jnp.moveaxis(chunk_decay, 1, 0)))
    prev = jnp.moveaxis(prev, 0, 1)
    y_off = jnp.einsum("bclhn,bchpn->bclhp", Cc * jnp.exp(a_cum)[..., None], prev)
    return (y_diag + y_off).reshape(b, L, h, pd)


def mamba2_mixer(p, conv_w, conv_b, dt_bias, A_log, Dskip, norm_w):
    b, L, _ = p.shape
    z, xbc, dt_raw = jnp.split(p, [D_GROUP, D_GROUP + M_XBC], axis=-1)
    xbc = jax.nn.silu(centered_dwconv(xbc, conv_w) + conv_b).astype(F32)
    xs, Bm, Cm = jnp.split(xbc, [D_GROUP, D_GROUP + M_GROUPS * M_STATE], axis=-1)
    xs = xs.reshape(b, L, M_HEADS, M_HEADDIM)
    rep = M_HEADS // M_GROUPS
    Bm = jnp.repeat(Bm.reshape(b, L, M_GROUPS, M_STATE), rep, axis=2)
    Cm = jnp.repeat(Cm.reshape(b, L, M_GROUPS, M_STATE), rep, axis=2)
    dt = jax.nn.softplus(dt_raw.astype(F32).reshape(b, L, 2, M_HEADS) + dt_bias.astype(F32))
    A = -jnp.exp(A_log.astype(F32))
    fl = lambda t: jnp.flip(t, axis=1)
    y_f = ssd_chunked(xs, dt[:, :, 0], A[0], Bm, Cm)
    y_b = fl(ssd_chunked(fl(xs), fl(dt[:, :, 1]), A[1], fl(Bm), fl(Cm)))
    y = y_f + y_b + xs * Dskip.astype(F32)[:, None]
    y = y.reshape(b, L, D_GROUP) * jax.nn.silu(z.astype(F32))
    return rmsnorm(y, norm_w.astype(F32)).astype(p.dtype)


def t5_bucket(rel):
    nb = N_BUCKETS // 2
    max_exact = nb // 2
    ret = (rel > 0).astype(jnp.int32) * nb
    n = jnp.abs(rel)
    nf = jnp.maximum(n, 1).astype(F32)
    large = max_exact + (jnp.log(nf / max_exact) / math.log(MAX_DISTANCE / max_exact)
                         * (nb - max_exact)).astype(jnp.int32)
    large = jnp.minimum(large, nb - 1)
    return ret + jnp.where(n < max_exact, n, large)


def band_attention_parts(q, k, v, rel_bias, dil, band):
    b, h, L, dh = q.shape
    n = L // dil
    nb = -(-n // band)
    npad = nb * band

    def to_sub(t):
        t = t.reshape(b, h, n, dil, dh).transpose(0, 1, 3, 2, 4)
        return jnp.pad(t, ((0, 0), (0, 0), (0, 0), (0, npad - n), (0, 0)))

    def kv_band(t):
        tp = jnp.pad(to_sub(t), ((0, 0), (0, 0), (0, 0), (band, band), (0, 0)))
        tp = tp.reshape(b, h, dil, nb + 2, band, dh)
        return jnp.concatenate([tp[:, :, :, :-2], tp[:, :, :, 1:-1], tp[:, :, :, 2:]], axis=4)

    qs = to_sub(q).reshape(b, h, dil, nb, band, dh)
    kb = kv_band(k)
    vb = kv_band(v)
    rel = jnp.arange(3 * band)[None, :] - band - jnp.arange(band)[:, None]
    bias = jnp.transpose(rel_bias.astype(F32)[t5_bucket(rel * dil)], (2, 0, 1))
    kidx = jnp.arange(nb)[:, None, None] * band + band + rel[None] - band + jnp.arange(band)[None, :, None]
    valid = (jnp.abs(rel)[None] <= band) & (kidx >= 0) & (kidx < n)
    logits = jnp.einsum("bhrnqd,bhrnkd->bhrnqk", qs, kb) + bias[None, :, None, None]
    logits = jnp.where(valid, logits, -jnp.inf)
    m = jnp.max(logits, axis=-1)
    m = jnp.where(jnp.isfinite(m), m, 0.0)
    pr = jnp.exp(logits - m[..., None])
    den = jnp.sum(pr, axis=-1)
    num = jnp.einsum("bhrnqk,bhrnkd->bhrnqd", pr, vb)

    def from_sub(t):
        t = t.reshape(b, h, dil, npad, *t.shape[5:])[:, :, :, :n]
        return jnp.moveaxis(t, 2, 3).reshape(b, h, L, *t.shape[4:])

    return from_sub(num), from_sub(den), from_sub(m)


def dilated_attention_mixer(p, rel_bias):
    b, L, _ = p.shape
    q, k, v = jnp.split(p.astype(F32), 3, axis=-1)
    heads = lambda t: t.reshape(b, L, A_HEADS, A_HEADDIM).transpose(0, 2, 1, 3)
    q, k, v = heads(q) * (A_HEADDIM ** -0.5), heads(k), heads(v)
    parts = [band_attention_parts(q, k, v, rel_bias, d, w // (2 * d)) for (w, d) in A_PATTERNS]
    m_all = jnp.max(jnp.stack([pt[2] for pt in parts]), axis=0)
    num = 0.0
    den = 0.0
    for n_i, d_i, m_i in parts:
        w_i = jnp.exp(m_i - m_all)
        num = num + w_i[..., None] * n_i
        den = den + w_i * d_i
    o = num / den[..., None]
    return o.transpose(0, 2, 1, 3).reshape(b, L, D_GROUP).astype(p.dtype)


def gla_chunked(q, k, v, g):
    b, L, h, e = q.shape
    pd = v.shape[-1]
    nc = L // H_CHUNK
    rs = lambda t: t.reshape(b, nc, H_CHUNK, h, t.shape[-1])
    q, k, v, g = rs(q), rs(k), rs(v), rs(g)
    gc = jnp.cumsum(g, axis=2)
    ref = gc[:, :, H_CHUNK // 2 - 1:H_CHUNK // 2]
    scores = jnp.einsum("bclhe,bcshe->bchls", q * jnp.exp(gc - ref), k * jnp.exp(ref - gc))
    mask = jnp.tril(jnp.ones((H_CHUNK, H_CHUNK), bool))
    scores = jnp.where(mask, scores, 0.0)
    o_intra = jnp.einsum("bchls,bcshp->bclhp", scores, v)
    g_last = gc[:, :, -1:]
    U = jnp.einsum("bclhe,bclhp->bchep", k * jnp.exp(g_last - gc), v)
    a = jnp.exp(g_last[:, :, 0])

    def step(S, inp):
        u_c, a_c = inp
        return a_c[..., None] * S + u_c, S

    S0 = jnp.zeros((b, h, e, pd), F32)
    _, prev = lax.scan(step, S0, (jnp.moveaxis(U, 1, 0), jnp.moveaxis(a, 1, 0)))
    prev = jnp.moveaxis(prev, 0, 1)
    o_inter = jnp.einsum("bclhe,bchep->bclhp", q * jnp.exp(gc), prev)
    return (o_intra + o_inter).reshape(b, L, h, pd)


def hgrn2_mixer(p, lb, norm_w):
    b, L, _ = p.shape
    q, f_fwd, f_bwd, i, g = jnp.split(p.astype(F32), 5, axis=-1)
    hs = lambda t, d: t.reshape(b, L, H_HEADS, d)
    q = hs(jax.nn.silu(q), H_EXPAND)
    i = hs(i, H_HEADDIM)
    lb = lb.reshape(2, H_HEADS, H_EXPAND)

    def gates(fpre, lbd):
        fpre = hs(fpre, H_EXPAND)
        f = lbd + (1.0 - lbd) * jax.nn.sigmoid(fpre)
        return jnp.log(f), (1.0 - lbd) * jax.nn.sigmoid(-fpre)

    gf, kf = gates(f_fwd, lb[0])
    gb, kb = gates(f_bwd, lb[1])
    fl = lambda t: jnp.flip(t, axis=1)
    o = gla_chunked(q, kf, i, gf) + fl(gla_chunked(fl(q), fl(kb), fl(i), fl(gb)))
    o = rmsnorm(o, norm_w.astype(F32)) * jax.nn.silu(hs(g, H_HEADDIM))
    return o.reshape(b, L, D_GROUP).astype(p.dtype)


def expert_choice_moe(x, w_router, w_gate, w_up, w_down):
    b, L, d = x.shape
    cap = EC_CAPACITY * L // N_EXPERTS
    aff = jax.nn.softmax((x @ w_router).astype(F32), axis=-1)
    gate, idx = lax.top_k(jnp.swapaxes(aff, 1, 2), cap)
    xe = jax.vmap(lambda xb, ib: xb[ib])(x, idx)
    hdn = jax.nn.silu(jnp.einsum("becd,edf->becf", xe, w_gate)) * jnp.einsum("becd,edf->becf", xe, w_up)
    ye = jnp.einsum("becf,efd->becd", hdn, w_down) * gate[..., None].astype(x.dtype)
    return jax.vmap(lambda ib, yb: jnp.zeros((L, d), yb.dtype).at[ib.reshape(-1)].add(yb.reshape(-1, d)))(idx, ye)


def setup_inputs(seed: int = 0) -> dict:
    key = jax.random.key(seed)
    ks = jax.random.split(key, 27)
    nrm = lambda k, shape, std: std * jax.random.normal(k, shape, F32)
    gain = lambda k, shape: 1.0 + 0.05 * jax.random.normal(k, shape, F32)
    dt0 = jnp.exp(jax.random.uniform(ks[15], (DEPTH, 2, M_HEADS), F32, math.log(1e-3), math.log(1e-1)))
    return {
        "x": jax.random.normal(ks[0], (BATCH, SEQ, D_MODEL), F32),
        "w_in": nrm(ks[1], (DEPTH, D_MODEL, N_IN), D_MODEL ** -0.5),
        "w_out": nrm(ks[2], (DEPTH, D_MIX, D_MODEL), D_MIX ** -0.5),
        "norm_mix_w": gain(ks[3], (DEPTH, D_MODEL)),
        "norm_ffn_w": gain(ks[4], (DEPTH, D_MODEL)),
        "hy_conv_w": nrm(ks[5], (DEPTH, HY_SHORT, HY_PROJ), HY_SHORT ** -0.5),
        "hy_pos_w1": nrm(ks[6], (DEPTH, HY_POS_DIM, HY_FILT_HIDDEN), HY_POS_DIM ** -0.5),
        "hy_pos_b1": nrm(ks[7], (DEPTH, HY_FILT_HIDDEN), 0.1),
        "hy_pos_w2": nrm(ks[8], (DEPTH, HY_FILT_HIDDEN, HY_FILT_HIDDEN), HY_FILT_HIDDEN ** -0.5),
        "hy_pos_b2": nrm(ks[9], (DEPTH, HY_FILT_HIDDEN), 0.1),
        "hy_sin_freq": gain(ks[10], (DEPTH, HY_FILT_HIDDEN)),
        "hy_pos_w3": nrm(ks[11], (DEPTH, HY_FILT_HIDDEN, 2 * D_GROUP), 0.1 * HY_FILT_HIDDEN ** -0.5),
        "hy_filt_bias": nrm(ks[12], (DEPTH, D_GROUP), 1.0),
        "m_conv_w": nrm(ks[13], (DEPTH, M_CONV, M_XBC), M_CONV ** -0.5),
        "m_conv_b": nrm(ks[14], (DEPTH, M_XBC), 0.02),
        "m_dt_bias": dt0 + jnp.log(-jnp.expm1(-dt0)),
        "m_A_log": jnp.log(jax.random.uniform(ks[16], (DEPTH, 2, M_HEADS), F32, 1.0, 16.0)),
        "m_D": gain(ks[17], (DEPTH, M_HEADS)),
        "m_norm_w": gain(ks[18], (DEPTH, D_GROUP)),
        "rel_bias": nrm(ks[19], (N_BUCKETS, A_HEADS), 0.5),
        "hg_lb": nrm(ks[20], (DEPTH, 2, D_GROUP), 1.0),
        "hg_norm_w": gain(ks[21], (DEPTH, H_HEADDIM)),
        "router_w": nrm(ks[22], (DEPTH, D_MODEL, N_EXPERTS), D_MODEL ** -0.5),
        "moe_w_gate": nrm(ks[23], (DEPTH, N_EXPERTS, D_MODEL, D_FF_EXPERT), D_MODEL ** -0.5),
        "moe_w_up": nrm(ks[24], (DEPTH, N_EXPERTS, D_MODEL, D_FF_EXPERT), D_MODEL ** -0.5),
        "moe_w_down": nrm(ks[25], (DEPTH, N_EXPERTS, D_FF_EXPERT, D_MODEL), D_FF_EXPERT ** -0.5),
        "final_norm_w": gain(ks[26], (D_MODEL,)),
    }


def reference(x, w_in, w_out, norm_mix_w, norm_ffn_w, hy_conv_w, hy_pos_w1, hy_pos_b1, hy_pos_w2,
              hy_pos_b2, hy_sin_freq, hy_pos_w3, hy_filt_bias, m_conv_w, m_conv_b, m_dt_bias, m_A_log,
              m_D, m_norm_w, rel_bias, hg_lb, hg_norm_w, router_w, moe_w_gate, moe_w_up, moe_w_down,
              final_norm_w):
    sm = jax.nn.softmax(hg_lb.astype(F32), axis=0)
    lower_bounds = jnp.cumsum(sm, axis=0) - sm[:1]
    splits = [HY_PROJ, HY_PROJ + M_PROJ, HY_PROJ + M_PROJ + A_PROJ]
    for l in range(DEPTH):
        hn = rmsnorm(x, norm_mix_w[l])
        proj = hn @ w_in[l]
        pa, pb, pc, pd = jnp.split(proj, splits, axis=-1)
        ya = hyena_mixer(pa, hy_conv_w[l], hy_pos_w1[l], hy_pos_b1[l], hy_pos_w2[l], hy_pos_b2[l],
                         hy_sin_freq[l], hy_pos_w3[l], hy_filt_bias[l])
        yb = mamba2_mixer(pb, m_conv_w[l], m_conv_b[l], m_dt_bias[l], m_A_log[l], m_D[l], m_norm_w[l])
        yc = dilated_attention_mixer(pc, rel_bias)
        yd = hgrn2_mixer(pd, lower_bounds[l], hg_norm_w[l])
        mix = jnp.concatenate([ya, yb, yc, yd], axis=-1).astype(x.dtype)
        x = x + mix @ w_out[l]
        x = x + expert_choice_moe(rmsnorm(x, norm_ffn_w[l]), router_w[l], moe_w_gate[l],
                                  moe_w_up[l], moe_w_down[l])
    return rmsnorm(x, final_norm_w)
```

```python
import functools
import math

import numpy as np
import jax
import jax.numpy as jnp
from jax import lax
from jax.experimental import pallas as pl
from jax.experimental.pallas import tpu as pltpu

F32 = jnp.float32
BF16 = jnp.bfloat16
I32 = jnp.int32

D_MODEL = 1024
SEQ = 2048
DEPTH = 2
DG = 256
N_HEADS = 4
HD = 64
HY_POS_DIM = 33
HY_HID = 64
M_CONV = 5
M_STATE = 64
M_CHUNK = 128
H_CHUNK = 32
A_BAND = 64
A_DILS = (1, 4, 16)
N_BUCKETS = 32
MAX_DISTANCE = 1024
N_EXPERTS = 16
CAP = 2 * SEQ // N_EXPERTS
D_FF = 1024
EPS = 1e-6
NFFT = 2 * SEQ

V7X_LANES = 128
V7X_SUBLANES = 8
V7X_VMEM_LIMIT_BYTES = 56 * 1024 * 1024

NEG_BIG = -1e30

_NT = (((1,), (1,)), ((), ()))
_TN = (((0,), (0,)), ((), ()))


def _params(*sem):
    return pltpu.CompilerParams(dimension_semantics=sem, vmem_limit_bytes=V7X_VMEM_LIMIT_BYTES)


def _dot(a, b):
    return jnp.dot(a, b, preferred_element_type=F32)


def _dot_hi(a, b):
    return jnp.dot(a, b, preferred_element_type=F32, precision=lax.Precision.HIGHEST)


def _dot01(t_bf16, x):
    x1 = x.astype(BF16)
    r1 = x - x1.astype(F32)
    x2 = r1.astype(BF16)
    x3 = (r1 - x2.astype(F32)).astype(BF16)
    return _dot(t_bf16, x1) + _dot(t_bf16, x2) + _dot(t_bf16, x3)


def _dot01_rhs(x, t_bf16):
    x1 = x.astype(BF16)
    r1 = x - x1.astype(F32)
    x2 = r1.astype(BF16)
    x3 = (r1 - x2.astype(F32)).astype(BF16)
    return _dot(x1, t_bf16) + _dot(x2, t_bf16) + _dot(x3, t_bf16)


def _silu(x):
    return x * jax.nn.sigmoid(x)


def _softplus(x):
    return jnp.maximum(x, 0.0) + jnp.log(1.0 + jnp.exp(-jnp.abs(x)))


def _rms(x):
    return x * lax.rsqrt(jnp.mean(x * x, axis=-1, keepdims=True) + EPS)


TM_PROJ = 512
_HY0, _MZ0, _MX0, _AT0, _HG0, _PEND = 0, 768, 1024, 1536, 2304, 3584


def _inproj_body(has_b, *refs):
    if has_b:
        xa_ref, xb_ref, nw_ref, w_ref, wdt_ref, hy_ref, mz_ref, mx_ref, at_ref, hg_ref, dtc_ref = refs
        x = xa_ref[...] + xb_ref[...]
    else:
        xa_ref, nw_ref, w_ref, wdt_ref, hy_ref, mz_ref, mx_ref, at_ref, hg_ref, dtc_ref = refs
        x = xa_ref[...]
    hn = (_rms(x) * nw_ref[...]).astype(BF16)
    hy_ref[...] = _dot(hn, w_ref[:, _HY0:_MZ0]).astype(BF16)
    mz_ref[...] = _dot(hn, w_ref[:, _MZ0:_MX0]).astype(BF16)
    mx_ref[...] = _dot(hn, w_ref[:, _MX0:_AT0]).astype(BF16)
    at_ref[...] = _dot(hn, w_ref[:, _AT0:_HG0]).astype(BF16)
    hg_ref[...] = _dot(hn, w_ref[:, _HG0:_PEND]).astype(BF16)
    dt_rows = lax.dot_general(wdt_ref[...], hn, _NT, preferred_element_type=F32)
    for j in range(TM_PROJ // M_CHUNK):
        dtc_ref[j] = dt_rows[:, j * M_CHUNK:(j + 1) * M_CHUNK]


def in_projection(xa, xb, norm_w, w_main, w_dt_rows):
    t = xa.shape[0]
    tm = TM_PROJ
    has_b = xb is not None
    row = lambda w: pl.BlockSpec((tm, w), lambda i: (i, 0))
    full = lambda a: pl.BlockSpec(a.shape, lambda i: (0,) * a.ndim)
    ins = [xa] + ([xb] if has_b else []) + [norm_w, w_main, w_dt_rows]
    in_specs = [row(D_MODEL)] * (2 if has_b else 1) + [full(norm_w), full(w_main), full(w_dt_rows)]
    widths = (768, 256, 512, 768, 1280)
    out_shape = [jax.ShapeDtypeStruct((t, w), BF16) for w in widths]
    out_shape.append(jax.ShapeDtypeStruct((t // M_CHUNK, 8, M_CHUNK), F32))
    out_specs = [row(w) for w in widths] + [pl.BlockSpec((tm // M_CHUNK, 8, M_CHUNK), lambda i: (i, 0, 0))]
    return pl.pallas_call(
        functools.partial(_inproj_body, has_b),
        grid=(t // tm,), in_specs=in_specs, out_specs=out_specs, out_shape=out_shape,
        compiler_params=_params("parallel"), name="in_projection",
    )(*ins)


TM_OUT = 512


def _outproj_body(has_b, *refs):
    if has_b:
        xa_ref, xb_ref, ya_ref, yb_ref, yc_ref, yd_ref, w_ref, nw_ref, xo_ref, xn_ref = refs
        x = xa_ref[...] + xb_ref[...]
    else:
        xa_ref, ya_ref, yb_ref, yc_ref, yd_ref, w_ref, nw_ref, xo_ref, xn_ref = refs
        x = xa_ref[...]
    acc = x + _dot(ya_ref[...], w_ref[0]) + _dot(yb_ref[...], w_ref[1])
    acc = acc + _dot(yc_ref[...], w_ref[2]) + _dot(yd_ref[...], w_ref[3])
    xo_ref[...] = acc
    xn_ref[...] = (_rms(acc) * nw_ref[...]).astype(BF16)


def out_projection(xa, xb, ya, yb, yc, yd, w_out4, norm_w):
    t = xa.shape[0]
    tm = TM_OUT
    has_b = xb is not None
    row = lambda w: pl.BlockSpec((tm, w), lambda i: (i, 0))
    full = lambda a: pl.BlockSpec(a.shape, lambda i: (0,) * a.ndim)
    ins = [xa] + ([xb] if has_b else []) + [ya, yb, yc, yd, w_out4, norm_w]
    in_specs = [row(D_MODEL)] * (2 if has_b else 1) + [row(DG)] * 4 + [full(w_out4), full(norm_w)]
    return pl.pallas_call(
        functools.partial(_outproj_body, has_b),
        grid=(t // tm,), in_specs=in_specs,
        out_specs=[row(D_MODEL), row(D_MODEL)],
        out_shape=[jax.ShapeDtypeStruct((t, D_MODEL), F32), jax.ShapeDtypeStruct((t, D_MODEL), BF16)],
        compiler_params=_params("parallel"), name="out_projection",
    )(*ins)


def _final_norm_body(xa_ref, xb_ref, nw_ref, o_ref):
    o_ref[...] = _rms(xa_ref[...] + xb_ref[...]) * nw_ref[...]


def final_norm(xa, xb, norm_w):
    t = xa.shape[0]
    tm = 1024
    row = pl.BlockSpec((tm, D_MODEL), lambda i: (i, 0))
    return pl.pallas_call(
        _final_norm_body, grid=(t // tm,),
        in_specs=[row, row, pl.BlockSpec(norm_w.shape, lambda i: (0, 0))],
        out_specs=row, out_shape=jax.ShapeDtypeStruct((t, D_MODEL), F32),
        compiler_params=_params("parallel"), name="final_norm",
    )(xa, xb, norm_w)


def _prefix_excl_lanes(mask_f32, u_ref):
    e = mask_f32.shape[0]
    off = jnp.zeros((e, 1), F32)
    parts = []
    for k in range(SEQ // V7X_LANES):
        tile = mask_f32[:, k * V7X_LANES:(k + 1) * V7X_LANES]
        parts.append(_dot(tile.astype(BF16), u_ref[...]) + off)
        off = off + jnp.sum(tile, axis=1, keepdims=True)
    return jnp.concatenate(parts, axis=1)


def _router_body(xo_ref, nw_ref, rwh_ref, rwl_ref, u_ref, rank_ref, gate_ref):
    xn = _rms(xo_ref[0]) * nw_ref[...]
    xh = xn.astype(BF16)
    xl = (xn - xh.astype(F32)).astype(BF16)
    nt = lambda w, a: lax.dot_general(w, a, _NT, preferred_element_type=F32)
    logits = nt(rwh_ref[...], xh) + nt(rwh_ref[...], xl) + nt(rwl_ref[...], xh)
    mx = jnp.max(logits, axis=0, keepdims=True)
    ex = jnp.exp(logits - mx)
    aff = ex / jnp.sum(ex, axis=0, keepdims=True)
    bits = pltpu.bitcast(aff, I32)

    def search(i, thr):
        cand = thr | jnp.left_shift(jnp.int32(1), 30 - i)
        cnt = jnp.sum((bits >= cand).astype(I32), axis=1, keepdims=True)
        return jnp.where(cnt >= CAP, cand, thr)

    thr = lax.fori_loop(0, 31, search, jnp.zeros((N_EXPERTS, 1), I32))
    gt = (bits > thr).astype(F32)
    eq = (bits == thr).astype(F32)
    need = CAP - jnp.sum(gt, axis=1, keepdims=True)
    tie_rank = _prefix_excl_lanes(eq, u_ref)
    sel = gt + eq * (tie_rank < need).astype(F32)
    rank = _prefix_excl_lanes(sel, u_ref)
    rank_ref[0] = jnp.where(sel > 0.0, rank, -1.0)
    gate_ref[0] = aff


def router(xo3, norm_w, rw_hi, rw_lo, u128):
    b = xo3.shape[0]
    out = jax.ShapeDtypeStruct((b, N_EXPERTS, SEQ), F32)
    full = lambda a: pl.BlockSpec(a.shape, lambda i: (0,) * a.ndim)
    return pl.pallas_call(
        _router_body, grid=(b,),
        in_specs=[pl.BlockSpec((1, SEQ, D_MODEL), lambda i: (i, 0, 0)), full(norm_w), full(rw_hi), full(rw_lo),
                  full(u128)],
        out_specs=[pl.BlockSpec((1, N_EXPERTS, SEQ), lambda i: (i, 0, 0))] * 2,
        out_shape=[out, out],
        compiler_params=_params("parallel"), name="router",
    )(xo3, norm_w, rw_hi, rw_lo, u128)


MOE_SCATTER_ROWS = 512


def _moe_body(xn_ref, rank_ref, gate_ref, wg_ref, wu_ref, wd_ref, o_ref):
    e = pl.program_id(1)
    rank_row = rank_ref[0, 0]
    slot = lax.broadcasted_iota(I32, (CAP, SEQ), 0).astype(F32)
    hit = rank_row == slot
    onehot = jnp.where(hit, 1.0, 0.0).astype(BF16)
    gate_c = jnp.sum(jnp.where(hit, gate_ref[0, 0], 0.0), axis=1, keepdims=True)
    xe = _dot(onehot, xn_ref[0]).astype(BF16)
    hid = (_silu(_dot(xe, wg_ref[0])) * _dot(xe, wu_ref[0])).astype(BF16)
    ye = (_dot(hid, wd_ref[0]) * gate_c).astype(BF16)
    for r in range(SEQ // MOE_SCATTER_ROWS):
        rows = slice(r * MOE_SCATTER_ROWS, (r + 1) * MOE_SCATTER_ROWS)
        contrib = lax.dot_general(onehot[:, rows], ye, _TN, preferred_element_type=F32)

        @pl.when(e == 0)
        def _():
            o_ref[0, rows, :] = contrib

        @pl.when(e > 0)
        def _():
            o_ref[0, rows, :] += contrib


def moe_ffn(xn3, rank4, gate4, w_gate, w_up, w_down):
    b = xn3.shape[0]
    sel_spec = pl.BlockSpec((1, 1, 1, SEQ), lambda i, e: (i, e, 0, 0))
    w_spec = lambda a: pl.BlockSpec((1,) + a.shape[1:], lambda i, e: (e, 0, 0))
    return pl.pallas_call(
        _moe_body, grid=(b, N_EXPERTS),
        in_specs=[pl.BlockSpec((1, SEQ, D_MODEL), lambda i, e: (i, 0, 0)), sel_spec, sel_spec,
                  w_spec(w_gate), w_spec(w_up), w_spec(w_down)],
        out_specs=pl.BlockSpec((1, SEQ, D_MODEL), lambda i, e: (i, 0, 0)),
        out_shape=jax.ShapeDtypeStruct((b, SEQ, D_MODEL), F32),
        compiler_params=_params("parallel", "arbitrary"), name="moe_ffn",
    )(xn3, rank4, gate4, w_gate, w_up, w_down)


HY_KB = 256
HY_ROWS = 256


def _hy_filter_body(z_ref, w1_ref, b1_ref, w2_ref, b2_ref, fr_ref, w3_ref, dec_ref, c_ref, s_ref,
                    kr_ref, ki_ref, kny_ref, a_s, d_s):
    @pl.when(pl.program_id(0) == 0)
    def _():
        def rows(c, kny):
            r0 = pl.multiple_of(c * HY_ROWS, HY_ROWS)
            fr = fr_ref[...]
            h = jnp.sin(fr * (_dot_hi(z_ref[pl.ds(r0, HY_ROWS), :], w1_ref[...]) + b1_ref[...]))
            h = jnp.sin(fr * (_dot_hi(h, w2_ref[...]) + b2_ref[...]))
            h = _dot_hi(h, w3_ref[...])
            dec = dec_ref[pl.ds(r0, HY_ROWS), :]
            pos = r0 + lax.broadcasted_iota(I32, (HY_ROWS, DG), 0)
            hf = h[:, :DG] * dec
            hb = jnp.where(pos == 0, 0.0, h[:, DG:] * dec)
            a = hf + hb
            a_s[pl.ds(r0, HY_ROWS), :] = a
            d_s[pl.ds(r0, HY_ROWS), :] = hf - hb
            sgn = (1 - 2 * (pos & 1)).astype(F32)
            return kny + jnp.sum(a * sgn, axis=0, keepdims=True)

        kny = lax.fori_loop(0, SEQ // HY_ROWS, rows, jnp.zeros((1, DG), F32))
        kny_ref[...] = jnp.broadcast_to(kny, kny_ref.shape)

    kr_ref[...] = _dot_hi(c_ref[...], a_s[...])
    ki_ref[...] = _dot_hi(s_ref[...], d_s[...])


def hyena_filter_spectrum(zpos, w1, b1, w2, b2, freq, w3, decay, cos_f32, sin_f32):
    full = lambda a: pl.BlockSpec(a.shape, lambda k: (0,) * a.ndim)
    kblk = pl.BlockSpec((HY_KB, SEQ), lambda k: (k, 0))
    oblk = pl.BlockSpec((HY_KB, DG), lambda k: (k, 0))
    return pl.pallas_call(
        _hy_filter_body, grid=(SEQ // HY_KB,),
        in_specs=[full(zpos), full(w1), full(b1), full(w2), full(b2), full(freq), full(w3), full(decay), kblk, kblk],
        out_specs=[oblk, oblk, pl.BlockSpec((V7X_SUBLANES, DG), lambda k: (0, 0))],
        out_shape=[jax.ShapeDtypeStruct((SEQ, DG), F32), jax.ShapeDtypeStruct((SEQ, DG), F32),
                   jax.ShapeDtypeStruct((V7X_SUBLANES, DG), F32)],
        scratch_shapes=[pltpu.VMEM((SEQ, DG), F32), pltpu.VMEM((SEQ, DG), F32)],
        compiler_params=_params("arbitrary"), name="hyena_filter",
    )(zpos, w1, b1, w2, b2, freq, w3, decay, cos_f32, sin_f32)


CONV_ROWS = 128
CONV_HALO = 8


def _dwconv_rows(pad_ref, w_ref, r0, lanes, k):
    n = CONV_ROWS + 2 * CONV_HALO
    win = pad_ref[pl.ds(r0, n), lanes]
    acc = None
    for j in range(k):
        sh = (k // 2 - j) % n
        rolled = win if sh == 0 else pltpu.roll(win, sh, 0)
        term = rolled[CONV_HALO:CONV_HALO + CONV_ROWS] * w_ref[j:j + 1, lanes]
        acc = term if acc is None else acc + term
    return acc


def _fill_padded(pad_ref, src_ref, width):
    zeros = jnp.zeros((CONV_HALO, width), F32)
    pad_ref[pl.ds(0, CONV_HALO), :] = zeros
    pad_ref[pl.ds(SEQ + CONV_HALO, CONV_HALO), :] = zeros

    def fill(c, _):
        r0 = pl.multiple_of(c * CONV_ROWS, CONV_ROWS)
        pad_ref[pl.ds(r0 + CONV_HALO, CONV_ROWS), :] = src_ref[0, pl.ds(r0, CONV_ROWS), :].astype(F32)
        return 0

    lax.fori_loop(0, SEQ // CONV_ROWS, fill, 0)


def _hy_prep_body(p_ref, w_ref, z_ref, x0_ref, pad):
    _fill_padded(pad, p_ref, 3 * DG)

    def rows(c, _):
        r0 = pl.multiple_of(c * CONV_ROWS, CONV_ROWS)
        x0 = _dwconv_rows(pad, w_ref, r0, slice(0, DG), 3)
        x1 = _dwconv_rows(pad, w_ref, r0, slice(DG, 2 * DG), 3)
        v = _dwconv_rows(pad, w_ref, r0, slice(2 * DG, 3 * DG), 3)
        x0_ref[0, pl.ds(r0, CONV_ROWS), :] = x0.astype(BF16)
        z_ref[0, pl.ds(r0, CONV_ROWS), :] = (v * x1).astype(BF16)
        return 0

    lax.fori_loop(0, SEQ // CONV_ROWS, rows, 0)


def hyena_prep(p3, conv_w):
    b = p3.shape[0]
    blk = pl.BlockSpec((1, SEQ, DG), lambda i: (i, 0, 0))
    out = jax.ShapeDtypeStruct((b, SEQ, DG), BF16)
    return pl.pallas_call(
        _hy_prep_body, grid=(b,),
        in_specs=[pl.BlockSpec((1, SEQ, 3 * DG), lambda i: (i, 0, 0)), pl.BlockSpec(conv_w.shape, lambda i: (0, 0))],
        out_specs=[blk, blk], out_shape=[out, out],
        scratch_shapes=[pltpu.VMEM((SEQ + 2 * CONV_HALO, 3 * DG), F32)],
        compiler_params=_params("parallel"), name="hyena_prep",
    )(p3, conv_w)


HY_G = 2
HY_FB = 512


def _hy_conv_body(z_ref, x0_ref, cr_ref, sr_ref, cc_ref, sc_ref, kr_ref, ki_ref, kny_ref, fb_ref, o_ref, acc):
    kb = pl.program_id(1)
    krow = kb * HY_FB + lax.broadcasted_iota(I32, (HY_FB, 1), 0)
    wk = jnp.where(krow == 0, 1.0 / NFFT, 2.0 / NFFT)
    kr = kr_ref[...]
    ki = ki_ref[...]
    for g in range(HY_G):
        z = z_ref[g]
        zr = _dot(cr_ref[...], z)
        zi = _dot(sr_ref[...], z)
        yr = ((zr * kr - zi * ki) * wk).astype(BF16)
        yi = ((zr * ki + zi * kr) * wk).astype(BF16)
        part = _dot(cc_ref[...], yr) + _dot(sc_ref[...], yi)

        @pl.when(kb == 0)
        def _():
            acc[g] = part

        @pl.when(kb > 0)
        def _():
            acc[g] += part

    @pl.when(kb == pl.num_programs(1) - 1)
    def _():
        sgn = (1 - 2 * (lax.broadcasted_iota(I32, (SEQ, DG), 0) & 1)).astype(F32)
        for g in range(HY_G):
            zf = z_ref[g].astype(F32)
            zny = jnp.sum(zf * sgn, axis=0, keepdims=True)
            conv = acc[g] + (zny * kny_ref[0:1, :] * (1.0 / NFFT)) * sgn
            o_ref[g] = (x0_ref[g].astype(F32) * (conv + zf * fb_ref[...])).astype(BF16)


def hyena_conv(z3, x03, cos_bf, sin_bf, kr, ki, kny, fbias):
    b = z3.shape[0]
    seq_blk = pl.BlockSpec((HY_G, SEQ, DG), lambda i, k: (i, 0, 0))
    rows = pl.BlockSpec((HY_FB, SEQ), lambda i, k: (k, 0))
    cols = pl.BlockSpec((SEQ, HY_FB), lambda i, k: (0, k))
    kblk = pl.BlockSpec((HY_FB, DG), lambda i, k: (k, 0))
    return pl.pallas_call(
        _hy_conv_body, grid=(b // HY_G, SEQ // HY_FB),
        in_specs=[seq_blk, seq_blk, rows, rows, cols, cols, kblk, kblk,
                  pl.BlockSpec(kny.shape, lambda i, k: (0, 0)), pl.BlockSpec(fbias.shape, lambda i, k: (0, 0))],
        out_specs=seq_blk, out_shape=jax.ShapeDtypeStruct((b, SEQ, DG), BF16),
        scratch_shapes=[pltpu.VMEM((HY_G, SEQ, DG), F32)],
        compiler_params=_params("parallel", "arbitrary"), name="hyena_conv",
    )(z3, x03, cos_bf, sin_bf, cos_bf, sin_bf, kr, ki, kny, fbias)


N_MCH = SEQ // M_CHUNK
MQ = M_CHUNK


def _head_lane_vec(rows8, base):
    lane_head = lax.broadcasted_iota(I32, (1, DG), 1) // HD
    out = jnp.zeros((1, DG), F32)
    for h in range(N_HEADS):
        out = jnp.where(lane_head == h, rows8[base + h:base + h + 1, :], out)
    return out


def _mamba_body(z_ref, xbc_ref, dtc_ref, cw_ref, cb_ref, dtb_ref, a_ref, dsk_ref, nw_ref, tri_ref, bd_ref,
                o_ref, pad, xs_s, b_s, c_s, y_s, col_s):
    _fill_padded(pad, xbc_ref, 2 * DG)

    def conv_rows(c, _):
        r0 = pl.multiple_of(c * CONV_ROWS, CONV_ROWS)
        for g in range(4):
            lanes = slice(g * V7X_LANES, (g + 1) * V7X_LANES)
            u = _silu(_dwconv_rows(pad, cw_ref, r0, lanes, M_CONV) + cb_ref[:, lanes])
            if g < 2:
                xs_s[pl.ds(r0, CONV_ROWS), lanes] = u
            elif g == 2:
                b_s[pl.ds(r0, CONV_ROWS), :] = u.astype(BF16)
            else:
                c_s[pl.ds(r0, CONV_ROWS), :] = u.astype(BF16)
        return 0

    lax.fori_loop(0, SEQ // CONV_ROWS, conv_rows, 0)

    li = lax.broadcasted_iota(I32, (MQ, MQ), 0)
    si = lax.broadcasted_iota(I32, (MQ, MQ), 1)
    lower = si <= li
    upper = si >= li
    bdmask = bd_ref[...]

    def chunk_common(c):
        dt = _softplus(dtc_ref[0, c] + dtb_ref[...])
        a = dt * a_ref[...]
        cum = _dot01_rhs(a, tri_ref[...])
        tot = cum[:, MQ - 1:MQ]
        suf = tot - cum + a
        row_dir = lax.broadcasted_iota(I32, (8, MQ), 0) // N_HEADS
        seg = jnp.where(row_dir == 0, cum, suf)
        wgt = jnp.exp(tot - seg) * dt
        carry = jnp.exp(seg)
        stack = jnp.concatenate([seg, wgt, carry, dt], axis=0)
        return dt, seg, tot, stack.T

    def pass_fwd(c, s_f):
        r0 = pl.multiple_of(c * MQ, MQ)
        dt, seg, tot, cols = chunk_common(c)
        col_s[c] = cols
        x = xs_s[pl.ds(r0, MQ), :]
        xb = x.astype(BF16)
        bm = b_s[pl.ds(r0, MQ), :]
        cm = c_s[pl.ds(r0, MQ), :]
        bmf = bm.astype(F32)
        cmf = cm.astype(F32)
        ydiag = []
        for h in range(N_HEADS):
            g = h // 2
            cb = lax.dot_general(cm[:, g * M_STATE:(g + 1) * M_STATE], bm[:, g * M_STATE:(g + 1) * M_STATE],
                                 _NT, preferred_element_type=F32)
            lf = jnp.where(lower, jnp.exp(jnp.minimum(cols[:, h:h + 1] - seg[h:h + 1, :], 0.0)), 0.0)
            lb = jnp.where(upper, jnp.exp(jnp.minimum(cols[:, 4 + h:5 + h] - seg[4 + h:5 + h, :], 0.0)), 0.0)
            m = cb * (lf * dt[h:h + 1, :] + lb * dt[4 + h:5 + h, :])
            ydiag.append(_dot(m.astype(BF16), xb[:, h * HD:(h + 1) * HD]))
        y = jnp.concatenate(ydiag, axis=1)
        cw = jnp.concatenate([cmf[:, (h // 2) * M_STATE:(h // 2 + 1) * M_STATE] * cols[:, 16 + h:17 + h]
                              for h in range(N_HEADS)], axis=1)
        y = y + _dot(cw.astype(BF16), s_f.astype(BF16))
        y_s[pl.ds(r0, MQ), :] = y
        bw = jnp.concatenate([bmf[:, (h // 2) * M_STATE:(h // 2 + 1) * M_STATE] * cols[:, 8 + h:9 + h]
                              for h in range(N_HEADS)], axis=1)
        upd = lax.dot_general(bw.astype(BF16), xb, _TN, preferred_element_type=F32)
        decay = _head_lane_vec(jnp.exp(tot), 0)
        return s_f * decay + upd * bdmask

    lax.fori_loop(0, N_MCH, pass_fwd, jnp.zeros((DG, DG), F32))

    def pass_bwd(i, s_b):
        c = N_MCH - 1 - i
        r0 = pl.multiple_of(c * MQ, MQ)
        cols = col_s[c]
        x = xs_s[pl.ds(r0, MQ), :]
        xb = x.astype(BF16)
        bmf = b_s[pl.ds(r0, MQ), :].astype(F32)
        cmf = c_s[pl.ds(r0, MQ), :].astype(F32)
        cw = jnp.concatenate([cmf[:, (h // 2) * M_STATE:(h // 2 + 1) * M_STATE] * cols[:, 20 + h:21 + h]
                              for h in range(N_HEADS)], axis=1)
        y_s[pl.ds(r0, MQ), :] += _dot(cw.astype(BF16), s_b.astype(BF16))
        bw = jnp.concatenate([bmf[:, (h // 2) * M_STATE:(h // 2 + 1) * M_STATE] * cols[:, 12 + h:13 + h]
                              for h in range(N_HEADS)], axis=1)
        upd = lax.dot_general(bw.astype(BF16), xb, _TN, preferred_element_type=F32)
        tot_rows = jnp.concatenate([cols[0:1, 20 + h:21 + h] for h in range(N_HEADS)], axis=0)
        lane_head = lax.broadcasted_iota(I32, (1, DG), 1) // HD
        decay = jnp.zeros((1, DG), F32)
        for h in range(N_HEADS):
            decay = jnp.where(lane_head == h, tot_rows[h:h + 1, :], decay)
        return s_b * decay + upd * bdmask

    lax.fori_loop(0, N_MCH, pass_bwd, jnp.zeros((DG, DG), F32))

    def finish(c, _):
        r0 = pl.multiple_of(c * CONV_ROWS, CONV_ROWS)
        y = y_s[pl.ds(r0, CONV_ROWS), :] + xs_s[pl.ds(r0, CONV_ROWS), :] * dsk_ref[...]
        y = y * _silu(z_ref[0, pl.ds(r0, CONV_ROWS), :].astype(F32))
        o_ref[0, pl.ds(r0, CONV_ROWS), :] = (_rms(y) * nw_ref[...]).astype(BF16)
        return 0

    lax.fori_loop(0, SEQ // CONV_ROWS, finish, 0)


def mamba2(z3, xbc3, dtc4, conv_w, conv_b, dt_bias_col, a_col, dskip_lanes, norm_w, tri_incl, bdmask):
    b = z3.shape[0]
    full = lambda a: pl.BlockSpec(a.shape, lambda i: (0,) * a.ndim)
    return pl.pallas_call(
        _mamba_body, grid=(b,),
        in_specs=[pl.BlockSpec((1, SEQ, DG), lambda i: (i, 0, 0)),
                  pl.BlockSpec((1, SEQ, 2 * DG), lambda i: (i, 0, 0)),
                  pl.BlockSpec((1, N_MCH, 8, MQ), lambda i: (i, 0, 0, 0)),
                  full(conv_w), full(conv_b), full(dt_bias_col), full(a_col), full(dskip_lanes), full(norm_w),
                  full(tri_incl), full(bdmask)],
        out_specs=pl.BlockSpec((1, SEQ, DG), lambda i: (i, 0, 0)),
        out_shape=jax.ShapeDtypeStruct((b, SEQ, DG), BF16),
        scratch_shapes=[pltpu.VMEM((SEQ + 2 * CONV_HALO, 2 * DG), F32),
                        pltpu.VMEM((SEQ, DG), F32),
                        pltpu.VMEM((SEQ, 2 * M_STATE), BF16),
                        pltpu.VMEM((SEQ, 2 * M_STATE), BF16),
                        pltpu.VMEM((SEQ, DG), F32),
                        pltpu.VMEM((N_MCH, MQ, 32), F32)],
        compiler_params=_params("parallel"), name="mamba2",
    )(z3, xbc3, dtc4, conv_w, conv_b, dt_bias_col, a_col, dskip_lanes, norm_w, tri_incl, bdmask)


A_TQ = 128


def _band_attn_body(n, w, nvar, qkv_ref, bias_ref, o_ref, lse_ref):
    nblk = n // A_TQ

    def block(i, _):
        q0 = pl.multiple_of(i * A_TQ, A_TQ)
        if nvar == 1:
            k0 = 0
            var = 0
        else:
            k0 = pl.multiple_of(jnp.clip(q0 - A_BAND, 0, n - w), A_BAND)
            var = jnp.where(i == 0, 0, jnp.where(i == nblk - 1, 2, 1))
        for h in range(N_HEADS):
            q = qkv_ref[0, pl.ds(q0, A_TQ), h * HD:(h + 1) * HD]
            k = qkv_ref[0, pl.ds(k0, w), DG + h * HD:DG + (h + 1) * HD]
            v = qkv_ref[0, pl.ds(k0, w), 2 * DG + h * HD:2 * DG + (h + 1) * HD]
            s = lax.dot_general(q, k, _NT, preferred_element_type=F32) * (HD ** -0.5) + bias_ref[h, var]
            m = jnp.max(s, axis=1, keepdims=True)
            p = jnp.exp(s - m)
            den = jnp.sum(p, axis=1, keepdims=True)
            o = _dot(p.astype(BF16), v) / den
            o_ref[0, pl.ds(q0, A_TQ), h * HD:(h + 1) * HD] = o
            lse_ref[0, pl.ds(q0, A_TQ), h * HD:(h + 1) * HD] = jnp.broadcast_to(m + jnp.log(den), (A_TQ, HD))
        return 0

    lax.fori_loop(0, nblk, block, 0)


def band_attention(qkv_view, bias, dil):
    b = qkv_view.shape[0]
    n = SEQ // dil
    _, nvar, _, w = bias.shape
    out = jax.ShapeDtypeStruct((b, n, dil * DG), F32)
    oblk = pl.BlockSpec((1, n, DG), lambda i, r: (i, 0, r))
    return pl.pallas_call(
        functools.partial(_band_attn_body, n, w, nvar), grid=(b, dil),
        in_specs=[pl.BlockSpec((1, n, 3 * DG), lambda i, r: (i, 0, r)),
                  pl.BlockSpec(bias.shape, lambda i, r: (0, 0, 0, 0))],
        out_specs=[oblk, oblk], out_shape=[out, out],
        compiler_params=_params("parallel", "parallel"), name=f"band_attention_d{dil}",
    )(qkv_view, bias)


def _attn_combine_body(o1, l1, o2, l2, o3, l3, y_ref):
    a1, a2, a3 = l1[...], l2[...], l3[...]
    m = jnp.maximum(jnp.maximum(a1, a2), a3)
    w1, w2, w3 = jnp.exp(a1 - m), jnp.exp(a2 - m), jnp.exp(a3 - m)
    num = w1 * o1[...] + w2 * o2[...] + w3 * o3[...]
    y_ref[...] = (num / (w1 + w2 + w3)).astype(BF16)


def attention_combine(parts):
    t = parts[0].shape[0]
    tm = 1024
    blk = pl.BlockSpec((tm, DG), lambda i: (i, 0))
    return pl.pallas_call(
        _attn_combine_body, grid=(t // tm,), in_specs=[blk] * 6, out_specs=blk,
        out_shape=jax.ShapeDtypeStruct((t, DG), BF16),
        compiler_params=_params("parallel"), name="attention_combine",
    )(*parts)


H_BLK = 256
H_CPB = H_BLK // H_CHUNK
N_HBLK = SEQ // H_BLK
N_HCH = SEQ // H_CHUNK


def _chunk_bcast(x, row):
    c = x.shape[1]
    x3 = x.reshape(H_CPB, H_CHUNK, c)
    return jnp.broadcast_to(x3[:, row:row + 1, :], (H_CPB, H_CHUNK, c)).reshape(H_BLK, c)


def _hgrn_body(p_ref, lb_ref, nw_ref, tin_ref, bd_ref, o_ref, qd_s, ut_s, oi_s, dec_s):
    li = lax.broadcasted_iota(I32, (H_BLK, H_BLK), 0)
    si = lax.broadcasted_iota(I32, (H_BLK, H_BLK), 1)
    same = (li // H_CHUNK) == (si // H_CHUNK)
    mask_f = same & (si <= li)
    mask_b = same & (si >= li)
    bdmask = bd_ref[...]

    def block(bi, _):
        r0 = pl.multiple_of(bi * H_BLK, H_BLK)
        rows = pl.ds(r0, H_BLK)
        q = _silu(p_ref[0, rows, 0:DG].astype(F32))
        v = p_ref[0, rows, 3 * DG:4 * DG]
        oi = jnp.zeros((H_BLK, DG), F32)
        scores = [None] * N_HEADS
        for d in range(2):
            fpre = p_ref[0, rows, (1 + d) * DG:(2 + d) * DG].astype(F32)
            lb = lb_ref[d:d + 1, :]
            sg = jax.nn.sigmoid(fpre)
            g = jnp.log(lb + (1.0 - lb) * sg)
            k = (1.0 - lb) * (1.0 - sg)
            gi = _dot01(tin_ref[...], g)
            glast = _chunk_bcast(gi, H_CHUNK - 1)
            if d == 0:
                gc = gi
                gref = _chunk_bcast(gi, H_CHUNK // 2 - 1)
                msk = mask_f
            else:
                gc = glast - gi + g
                gref = _chunk_bcast(gc, H_CHUNK // 2)
                msk = mask_b
            qe = (q * jnp.exp(gc - gref)).astype(BF16)
            ke = (k * jnp.exp(gref - gc)).astype(BF16)
            for h in range(N_HEADS):
                hs = slice(h * HD, (h + 1) * HD)
                sc = jnp.where(msk, lax.dot_general(qe[:, hs], ke[:, hs], _NT, preferred_element_type=F32), 0.0)
                scores[h] = sc if d == 0 else scores[h] + sc
            qd_s[d, rows, :] = (q * jnp.exp(gc)).astype(BF16)
            kd = (k * jnp.exp(glast - gc)).astype(BF16)
            for j in range(H_CPB):
                cr = slice(j * H_CHUNK, (j + 1) * H_CHUNK)
                ut = lax.dot_general(v[cr, :], kd[cr, :], _TN, preferred_element_type=F32)
                ut_s[d, bi * H_CPB + j] = (ut * bdmask).astype(BF16)
                dec_s[d, bi * H_CPB + j] = jnp.broadcast_to(jnp.exp(glast[j * H_CHUNK:j * H_CHUNK + 1, :]),
                                                             (V7X_SUBLANES, DG))
        oi = jnp.concatenate([_dot(scores[h].astype(BF16), v[:, h * HD:(h + 1) * HD]) for h in range(N_HEADS)],
                             axis=1)
        oi_s[rows, :] = oi
        return 0

    lax.fori_loop(0, N_HBLK, block, 0)

    def scan_dir(d):
        def step(i, st):
            c = i if d == 0 else N_HCH - 1 - i
            r0 = pl.multiple_of(c * H_CHUNK, H_CHUNK)
            qd = qd_s[d, pl.ds(r0, H_CHUNK), :]
            inter = lax.dot_general(qd, st.astype(BF16), _NT, preferred_element_type=F32)
            oi_s[pl.ds(r0, H_CHUNK), :] += inter
            return st * dec_s[d, c][0:1, :] + ut_s[d, c].astype(F32)

        lax.fori_loop(0, N_HCH, step, jnp.zeros((DG, DG), F32))

    scan_dir(0)
    scan_dir(1)

    def finish(c, _):
        r0 = pl.multiple_of(c * CONV_ROWS, CONV_ROWS)
        rows = pl.ds(r0, CONV_ROWS)
        o = oi_s[rows, :]
        gate = _silu(p_ref[0, rows, 4 * DG:5 * DG].astype(F32))
        outs = [_rms(o[:, h * HD:(h + 1) * HD]) for h in range(N_HEADS)]
        o_ref[0, rows, :] = (jnp.concatenate(outs, axis=1) * nw_ref[...] * gate).astype(BF16)
        return 0

    lax.fori_loop(0, SEQ // CONV_ROWS, finish, 0)


def hgrn2(p3, lb2, norm_w_lanes, tri_in_chunk, bdmask):
    b = p3.shape[0]
    full = lambda a: pl.BlockSpec(a.shape, lambda i: (0,) * a.ndim)
    return pl.pallas_call(
        _hgrn_body, grid=(b,),
        in_specs=[pl.BlockSpec((1, SEQ, 5 * DG), lambda i: (i, 0, 0)), full(lb2), full(norm_w_lanes),
                  full(tri_in_chunk), full(bdmask)],
        out_specs=pl.BlockSpec((1, SEQ, DG), lambda i: (i, 0, 0)),
        out_shape=jax.ShapeDtypeStruct((b, SEQ, DG), BF16),
        scratch_shapes=[pltpu.VMEM((2, SEQ, DG), BF16),
                        pltpu.VMEM((2, N_HCH, DG, DG), BF16),
                        pltpu.VMEM((SEQ, DG), F32),
                        pltpu.VMEM((2, N_HCH, V7X_SUBLANES, DG), F32)],
        compiler_params=_params("parallel"), name="hgrn2",
    )(p3, lb2, norm_w_lanes, tri_in_chunk, bdmask)


@functools.lru_cache(maxsize=None)
def _tables():
    t = {}
    k = np.arange(SEQ, dtype=np.int64)
    ang = 2.0 * np.pi * ((k[:, None] * k[None, :]) % NFFT).astype(np.float64) / NFFT
    t["cos"] = np.cos(ang).astype(np.float32)
    t["sin"] = np.sin(ang).astype(np.float32)
    tt = np.linspace(0.0, 1.0, SEQ, dtype=np.float32)[:, None]
    bands = (HY_POS_DIM - 1) // 2
    ang_pos = (2.0 * math.pi * np.arange(SEQ, dtype=np.float32) / SEQ).astype(np.float32)
    f = np.linspace(1e-4, bands - 1, bands, dtype=np.float32)
    a2 = (ang_pos[:, None] * f[None, :]).astype(np.float32)
    z = np.concatenate([tt, np.cos(a2), -np.sin(a2)], axis=-1).astype(np.float32)
    zp = np.zeros((SEQ, V7X_LANES), np.float32)
    zp[:, :HY_POS_DIM] = z
    t["zpos"] = zp
    max_decay = math.log(1e-2) / 0.3
    min_decay = math.log(1e-2) / 1.5
    deltas = np.abs(np.linspace(min_decay, max_decay, DG, dtype=np.float32))
    t["decay"] = np.exp(-tt * deltas[None, :]).astype(np.float32)
    i128 = np.arange(V7X_LANES)
    t["u128"] = (i128[:, None] < i128[None, :]).astype(np.float32)
    im = np.arange(M_CHUNK)
    t["tri_incl"] = (im[:, None] <= im[None, :]).astype(np.float32)
    ib = np.arange(H_BLK)
    t["tri_in_chunk"] = ((ib[:, None] // H_CHUNK == ib[None, :] // H_CHUNK)
                         & (ib[None, :] <= ib[:, None])).astype(np.float32)
    idg = np.arange(DG)
    t["bdmask"] = (idg[:, None] // HD == idg[None, :] // HD).astype(np.float32)
    def bucket(rel):
        nb = N_BUCKETS // 2
        max_exact = nb // 2
        ret = (rel > 0).astype(np.int64) * nb
        n = np.abs(rel)
        nf = np.maximum(n, 1).astype(np.float64)
        large = max_exact + (np.log(nf / max_exact) / math.log(MAX_DISTANCE / max_exact)
                             * (nb - max_exact)).astype(np.int64)
        large = np.minimum(large, nb - 1)
        return ret + np.where(n < max_exact, n, large)

    for dil in A_DILS:
        n = SEQ // dil
        w = min(n, A_TQ + 2 * A_BAND)
        starts = [0] if n == w else [0, -A_BAND, -(w - A_TQ)]
        qi = np.arange(A_TQ)[:, None]
        kj = np.arange(w)[None, :]
        ids = []
        for s0 in starts:
            rel = kj + s0 - qi
            ids.append(np.where(np.abs(rel) <= A_BAND, bucket(rel * dil), -1))
        t[f"bucket{dil}"] = np.stack(ids).astype(np.int32)
    return t


def _attn_bias(rel_bias, bucket_ids):
    ids = jnp.asarray(bucket_ids)
    vals = jnp.take(rel_bias.astype(F32), jnp.maximum(ids, 0), axis=0)
    vals = jnp.where((ids >= 0)[..., None], vals, NEG_BIG)
    return jnp.transpose(vals, (3, 0, 1, 2))


def kernel(x, w_in, w_out, norm_mix_w, norm_ffn_w, hy_conv_w, hy_pos_w1, hy_pos_b1, hy_pos_w2, hy_pos_b2,
           hy_sin_freq, hy_pos_w3, hy_filt_bias, m_conv_w, m_conv_b, m_dt_bias, m_A_log, m_D, m_norm_w, rel_bias,
           hg_lb, hg_norm_w, router_w, moe_w_gate, moe_w_up, moe_w_down, final_norm_w):
    b = x.shape[0]
    t = b * SEQ
    tb = _tables()
    cos_f32 = jnp.asarray(tb["cos"])
    sin_f32 = jnp.asarray(tb["sin"])
    cos_bf = cos_f32.astype(BF16)
    sin_bf = sin_f32.astype(BF16)
    u128 = jnp.asarray(tb["u128"]).astype(BF16)
    tri_incl = jnp.asarray(tb["tri_incl"]).astype(BF16)
    tri_in_chunk = jnp.asarray(tb["tri_in_chunk"]).astype(BF16)
    bdmask = jnp.asarray(tb["bdmask"])
    attn_bias = [_attn_bias(rel_bias, tb[f"bucket{d}"]) for d in A_DILS]

    sm = jax.nn.softmax(hg_lb.astype(F32), axis=0)
    lower_bounds = jnp.cumsum(sm, axis=0) - sm[:1]

    xa = x.reshape(t, D_MODEL)
    xb = None
    for l in range(DEPTH):
        wl = w_in[l]
        w_main = jnp.concatenate([wl[:, 0:768], wl[:, 768:1024], wl[:, 1024:1536], wl[:, 1544:2312],
                                  wl[:, 2312:3592]], axis=1).astype(BF16)
        w_dt_rows = wl[:, 1536:1544].T.astype(BF16)
        hy, mz, mx, at, hg, dtc = in_projection(xa, xb, norm_mix_w[l][None, :], w_main, w_dt_rows)

        w1p = jnp.zeros((V7X_LANES, HY_HID), F32).at[:HY_POS_DIM].set(hy_pos_w1[l])
        kr, ki, kny = hyena_filter_spectrum(
            jnp.asarray(tb["zpos"]), w1p, hy_pos_b1[l][None, :], hy_pos_w2[l], hy_pos_b2[l][None, :],
            hy_sin_freq[l][None, :], hy_pos_w3[l], jnp.asarray(tb["decay"]), cos_f32, sin_f32)
        z3, x03 = hyena_prep(hy.reshape(b, SEQ, 3 * DG), hy_conv_w[l])
        ya = hyena_conv(z3, x03, cos_bf, sin_bf, kr, ki, kny, hy_filt_bias[l][None, :]).reshape(t, DG)

        a_col = (-jnp.exp(m_A_log[l].astype(F32))).reshape(8, 1)
        yb = mamba2(mz.reshape(b, SEQ, DG), mx.reshape(b, SEQ, 2 * DG), dtc.reshape(b, N_MCH, 8, MQ),
                    m_conv_w[l], m_conv_b[l][None, :], m_dt_bias[l].reshape(8, 1), a_col,
                    jnp.repeat(m_D[l].astype(F32), HD)[None, :], m_norm_w[l][None, :], tri_incl, bdmask).reshape(t, DG)

        parts = []
        for dil, bias in zip(A_DILS, attn_bias):
            o, lse = band_attention(at.reshape(b, SEQ // dil, dil * 3 * DG), bias, dil)
            parts += [o.reshape(t, DG), lse.reshape(t, DG)]
        yc = attention_combine(parts)

        lbl = lower_bounds[l]
        yd = hgrn2(hg.reshape(b, SEQ, 5 * DG), lbl, jnp.tile(hg_norm_w[l], N_HEADS)[None, :],
                   tri_in_chunk, bdmask).reshape(t, DG)

        xo, xn = out_projection(xa, xb, ya, yb, yc, yd, w_out[l].reshape(4, DG, D_MODEL).astype(BF16),
                                norm_ffn_w[l][None, :])
        xn3 = xn.reshape(b, SEQ, D_MODEL)
        rw_rows = router_w[l].T.astype(F32)
        rw_hi = rw_rows.astype(BF16)
        rw_lo = (rw_rows - rw_hi.astype(F32)).astype(BF16)
        rank, gate = router(xo.reshape(b, SEQ, D_MODEL), norm_ffn_w[l][None, :], rw_hi, rw_lo, u128)
        moe = moe_ffn(xn3, rank.reshape(b, N_EXPERTS, 1, SEQ), gate.reshape(b, N_EXPERTS, 1, SEQ),
                      moe_w_gate[l].astype(BF16), moe_w_up[l].astype(BF16), moe_w_down[l].astype(BF16))
        xa, xb = xo, moe.reshape(t, D_MODEL)
    return final_norm(xa, xb, final_norm_w[None, :]).reshape(b, SEQ, D_MODEL)
```

```python
import functools
import math

import numpy as np
import jax
import jax.numpy as jnp
from jax import lax
from jax.experimental import pallas as pl
from jax.experimental.pallas import tpu as pltpu

F32 = jnp.float32
BF16 = jnp.bfloat16
I32 = jnp.int32

D_MODEL = 1024
SEQ = 2048
DEPTH = 2
DG = 256
N_HEADS = 4
HD = 64
HY_POS_DIM = 33
HY_HID = 64
M_CONV = 5
M_STATE = 64
M_CHUNK = 128
H_CHUNK = 32
A_BAND = 64
A_DILS = (1, 4, 16)
N_BUCKETS = 32
MAX_DISTANCE = 1024
N_EXPERTS = 16
CAP = 2 * SEQ // N_EXPERTS
D_FF = 1024
EPS = 1e-6
NFFT = 2 * SEQ

V7X_LANES = 128
V7X_SUBLANES = 8
V7X_VMEM_LIMIT_BYTES = 56 * 1024 * 1024

NEG_BIG = -1e30

_NT = (((1,), (1,)), ((), ()))
_TN = (((0,), (0,)), ((), ()))


def _params(*sem):
    return pltpu.CompilerParams(dimension_semantics=sem, vmem_limit_bytes=V7X_VMEM_LIMIT_BYTES)


def _dot(a, b):
    return jnp.dot(a, b, preferred_element_type=F32)


def _dot_hi(a, b):
    return jnp.dot(a, b, preferred_element_type=F32, precision=lax.Precision.HIGHEST)


def _dot01(t_bf16, x):
    x1 = x.astype(BF16)
    r1 = x - x1.astype(F32)
    x2 = r1.astype(BF16)
    x3 = (r1 - x2.astype(F32)).astype(BF16)
    return _dot(t_bf16, x1) + _dot(t_bf16, x2) + _dot(t_bf16, x3)


def _dot01_rhs(x, t_bf16):
    x1 = x.astype(BF16)
    r1 = x - x1.astype(F32)
    x2 = r1.astype(BF16)
    x3 = (r1 - x2.astype(F32)).astype(BF16)
    return _dot(x1, t_bf16) + _dot(x2, t_bf16) + _dot(x3, t_bf16)


def _silu(x):
    return x * jax.nn.sigmoid(x)


def _softplus(x):
    return jnp.maximum(x, 0.0) + jnp.log(1.0 + jnp.exp(-jnp.abs(x)))


def _rms(x):
    return x * lax.rsqrt(jnp.mean(x * x, axis=-1, keepdims=True) + EPS)


TM_PROJ = 512
_HY0, _MZ0, _MX0, _AT0, _HG0, _PEND = 0, 768, 1024, 1536, 2304, 3584


def _inproj_body(has_b, *refs):
    if has_b:
        xa_ref, xb_ref, nw_ref, w_ref, wdt_ref, hy_ref, mz_ref, mx_ref, at_ref, hg_ref, dtc_ref = refs
        x = xa_ref[...] + xb_ref[...]
    else:
        xa_ref, nw_ref, w_ref, wdt_ref, hy_ref, mz_ref, mx_ref, at_ref, hg_ref, dtc_ref = refs
        x = xa_ref[...]
    hn = (_rms(x) * nw_ref[...]).astype(BF16)
    hy_ref[...] = _dot(hn, w_ref[:, _HY0:_MZ0]).astype(BF16)
    mz_ref[...] = _dot(hn, w_ref[:, _MZ0:_MX0]).astype(BF16)
    mx_ref[...] = _dot(hn, w_ref[:, _MX0:_AT0]).astype(BF16)
    at_ref[...] = _dot(hn, w_ref[:, _AT0:_HG0]).astype(BF16)
    hg_ref[...] = _dot(hn, w_ref[:, _HG0:_PEND]).astype(BF16)
    dt_rows = lax.dot_general(wdt_ref[...], hn, _NT, preferred_element_type=F32)
    for j in range(TM_PROJ // M_CHUNK):
        dtc_ref[j] = dt_rows[:, j * M_CHUNK:(j + 1) * M_CHUNK]


def in_projection(xa, xb, norm_w, w_main, w_dt_rows):
    t = xa.shape[0]
    tm = TM_PROJ
    has_b = xb is not None
    row = lambda w: pl.BlockSpec((tm, w), lambda i: (i, 0))
    full = lambda a: pl.BlockSpec(a.shape, lambda i: (0,) * a.ndim)
    ins = [xa] + ([xb] if has_b else []) + [norm_w, w_main, w_dt_rows]
    in_specs = [row(D_MODEL)] * (2 if has_b else 1) + [full(norm_w), full(w_main), full(w_dt_rows)]
    widths = (768, 256, 512, 768, 1280)
    out_shape = [jax.ShapeDtypeStruct((t, w), BF16) for w in widths]
    out_shape.append(jax.ShapeDtypeStruct((t // M_CHUNK, 8, M_CHUNK), F32))
    out_specs = [row(w) for w in widths] + [pl.BlockSpec((tm // M_CHUNK, 8, M_CHUNK), lambda i: (i, 0, 0))]
    return pl.pallas_call(
        functools.partial(_inproj_body, has_b),
        grid=(t // tm,), in_specs=in_specs, out_specs=out_specs, out_shape=out_shape,
        compiler_params=_params("parallel"), name="in_projection",
    )(*ins)


TM_OUT = 512


def _outproj_body(has_b, *refs):
    if has_b:
        xa_ref, xb_ref, ya_ref, yb_ref, yc_ref, yd_ref, w_ref, nw_ref, xo_ref, xn_ref = refs
        x = xa_ref[...] + xb_ref[...]
    else:
        xa_ref, ya_ref, yb_ref, yc_ref, yd_ref, w_ref, nw_ref, xo_ref, xn_ref = refs
        x = xa_ref[...]
    acc = x + _dot(ya_ref[...], w_ref[0]) + _dot(yb_ref[...], w_ref[1])
    acc = acc + _dot(yc_ref[...], w_ref[2]) + _dot(yd_ref[...], w_ref[3])
    xo_ref[...] = acc
    xn_ref[...] = (_rms(acc) * nw_ref[...]).astype(BF16)


def out_projection(xa, xb, ya, yb, yc, yd, w_out4, norm_w):
    t = xa.shape[0]
    tm = TM_OUT
    has_b = xb is not None
    row = lambda w: pl.BlockSpec((tm, w), lambda i: (i, 0))
    full = lambda a: pl.BlockSpec(a.shape, lambda i: (0,) * a.ndim)
    ins = [xa] + ([xb] if has_b else []) + [ya, yb, yc, yd, w_out4, norm_w]
    in_specs = [row(D_MODEL)] * (2 if has_b else 1) + [row(DG)] * 4 + [full(w_out4), full(norm_w)]
    return pl.pallas_call(
        functools.partial(_outproj_body, has_b),
        grid=(t // tm,), in_specs=in_specs,
        out_specs=[row(D_MODEL), row(D_MODEL)],
        out_shape=[jax.ShapeDtypeStruct((t, D_MODEL), F32), jax.ShapeDtypeStruct((t, D_MODEL), BF16)],
        compiler_params=_params("parallel"), name="out_projection",
    )(*ins)


def _final_norm_body(xa_ref, xb_ref, nw_ref, o_ref):
    o_ref[...] = _rms(xa_ref[...] + xb_ref[...]) * nw_ref[...]


def final_norm(xa, xb, norm_w):
    t = xa.shape[0]
    tm = 1024
    row = pl.BlockSpec((tm, D_MODEL), lambda i: (i, 0))
    return pl.pallas_call(
        _final_norm_body, grid=(t // tm,),
        in_specs=[row, row, pl.BlockSpec(norm_w.shape, lambda i: (0, 0))],
        out_specs=row, out_shape=jax.ShapeDtypeStruct((t, D_MODEL), F32),
        compiler_params=_params("parallel"), name="final_norm",
    )(xa, xb, norm_w)


def _prefix_excl_lanes(mask_f32, u_ref):
    e = mask_f32.shape[0]
    off = jnp.zeros((e, 1), F32)
    parts = []
    for k in range(SEQ // V7X_LANES):
        tile = mask_f32[:, k * V7X_LANES:(k + 1) * V7X_LANES]
        parts.append(_dot(tile.astype(BF16), u_ref[...]) + off)
        off = off + jnp.sum(tile, axis=1, keepdims=True)
    return jnp.concatenate(parts, axis=1)


def _router_body(xo_ref, nw_ref, rwh_ref, rwl_ref, u_ref, rank_ref, gate_ref):
    xn = _rms(xo_ref[0]) * nw_ref[...]
    xh = xn.astype(BF16)
    xl = (xn - xh.astype(F32)).astype(BF16)
    nt = lambda w, a: lax.dot_general(w, a, _NT, preferred_element_type=F32)
    logits = nt(rwh_ref[...], xh) + nt(rwh_ref[...], xl) + nt(rwl_ref[...], xh)
    mx = jnp.max(logits, axis=0, keepdims=True)
    ex = jnp.exp(logits - mx)
    aff = ex / jnp.sum(ex, axis=0, keepdims=True)
    bits = pltpu.bitcast(aff, I32)

    def search(i, thr):
        cand = thr | jnp.left_shift(jnp.int32(1), 30 - i)
        cnt = jnp.sum((bits >= cand).astype(I32), axis=1, keepdims=True)
        return jnp.where(cnt >= CAP, cand, thr)

    thr = lax.fori_loop(0, 31, search, jnp.zeros((N_EXPERTS, 1), I32))
    gt = (bits > thr).astype(F32)
    eq = (bits == thr).astype(F32)
    need = CAP - jnp.sum(gt, axis=1, keepdims=True)
    tie_rank = _prefix_excl_lanes(eq, u_ref)
    sel = gt + eq * (tie_rank < need).astype(F32)
    rank = _prefix_excl_lanes(sel, u_ref)
    rank_ref[0] = jnp.where(sel > 0.0, rank, -1.0)
    gate_ref[0] = aff


def router(xo3, norm_w, rw_hi, rw_lo, u128):
    b = xo3.shape[0]
    out = jax.ShapeDtypeStruct((b, N_EXPERTS, SEQ), F32)
    full = lambda a: pl.BlockSpec(a.shape, lambda i: (0,) * a.ndim)
    return pl.pallas_call(
        _router_body, grid=(b,),
        in_specs=[pl.BlockSpec((1, SEQ, D_MODEL), lambda i: (i, 0, 0)), full(norm_w), full(rw_hi), full(rw_lo),
                  full(u128)],
        out_specs=[pl.BlockSpec((1, N_EXPERTS, SEQ), lambda i: (i, 0, 0))] * 2,
        out_shape=[out, out],
        compiler_params=_params("parallel"), name="router",
    )(xo3, norm_w, rw_hi, rw_lo, u128)


MOE_SCATTER_ROWS = 512


def _moe_body(xn_ref, rank_ref, gate_ref, wg_ref, wu_ref, wd_ref, o_ref):
    e = pl.program_id(1)
    rank_row = rank_ref[0, 0]
    slot = lax.broadcasted_iota(I32, (CAP, SEQ), 0).astype(F32)
    hit = rank_row == slot
    onehot = jnp.where(hit, 1.0, 0.0).astype(BF16)
    gate_c = jnp.sum(jnp.where(hit, gate_ref[0, 0], 0.0), axis=1, keepdims=True)
    xe = _dot(onehot, xn_ref[0]).astype(BF16)
    hid = (_silu(_dot(xe, wg_ref[0])) * _dot(xe, wu_ref[0])).astype(BF16)
    ye = (_dot(hid, wd_ref[0]) * gate_c).astype(BF16)
    for r in range(SEQ // MOE_SCATTER_ROWS):
        rows = slice(r * MOE_SCATTER_ROWS, (r + 1) * MOE_SCATTER_ROWS)
        contrib = lax.dot_general(onehot[:, rows], ye, _TN, preferred_element_type=F32)

        @pl.when(e == 0)
        def _():
            o_ref[0, rows, :] = contrib

        @pl.when(e > 0)
        def _():
            o_ref[0, rows, :] += contrib


def moe_ffn(xn3, rank4, gate4, w_gate, w_up, w_down):
    b = xn3.shape[0]
    sel_spec = pl.BlockSpec((1, 1, 1, SEQ), lambda i, e: (i, e, 0, 0))
    w_spec = lambda a: pl.BlockSpec((1,) + a.shape[1:], lambda i, e: (e, 0, 0))
    return pl.pallas_call(
        _moe_body, grid=(b, N_EXPERTS),
        in_specs=[pl.BlockSpec((1, SEQ, D_MODEL), lambda i, e: (i, 0, 0)), sel_spec, sel_spec,
                  w_spec(w_gate), w_spec(w_up), w_spec(w_down)],
        out_specs=pl.BlockSpec((1, SEQ, D_MODEL), lambda i, e: (i, 0, 0)),
        out_shape=jax.ShapeDtypeStruct((b, SEQ, D_MODEL), F32),
        compiler_params=_params("parallel", "arbitrary"), name="moe_ffn",
    )(xn3, rank4, gate4, w_gate, w_up, w_down)


HY_KB = 256
HY_ROWS = 256


def _hy_filter_body(z_ref, w1_ref, b1_ref, w2_ref, b2_ref, fr_ref, w3_ref, dec_ref, c_ref, s_ref,
                    kr_ref, ki_ref, kny_ref, a_s, d_s):
    @pl.when(pl.program_id(0) == 0)
    def _():
        def rows(c, kny):
            r0 = pl.multiple_of(c * HY_ROWS, HY_ROWS)
            fr = fr_ref[...]
            h = jnp.sin(fr * (_dot_hi(z_ref[pl.ds(r0, HY_ROWS), :], w1_ref[...]) + b1_ref[...]))
            h = jnp.sin(fr * (_dot_hi(h, w2_ref[...]) + b2_ref[...]))
            h = _dot_hi(h, w3_ref[...])
            dec = dec_ref[pl.ds(r0, HY_ROWS), :]
            pos = r0 + lax.broadcasted_iota(I32, (HY_ROWS, DG), 0)
            hf = h[:, :DG] * dec
            hb = jnp.where(pos == 0, 0.0, h[:, DG:] * dec)
            a = hf + hb
            a_s[pl.ds(r0, HY_ROWS), :] = a
            d_s[pl.ds(r0, HY_ROWS), :] = hf - hb
            sgn = (1 - 2 * (pos & 1)).astype(F32)
            return kny + jnp.sum(a * sgn, axis=0, keepdims=True)

        kny = lax.fori_loop(0, SEQ // HY_ROWS, rows, jnp.zeros((1, DG), F32))
        kny_ref[...] = jnp.broadcast_to(kny, kny_ref.shape)

    kr_ref[...] = _dot_hi(c_ref[...], a_s[...])
    ki_ref[...] = _dot_hi(s_ref[...], d_s[...])


def hyena_filter_spectrum(zpos, w1, b1, w2, b2, freq, w3, decay, cos_f32, sin_f32):
    full = lambda a: pl.BlockSpec(a.shape, lambda k: (0,) * a.ndim)
    kblk = pl.BlockSpec((HY_KB, SEQ), lambda k: (k, 0))
    oblk = pl.BlockSpec((HY_KB, DG), lambda k: (k, 0))
    return pl.pallas_call(
        _hy_filter_body, grid=(SEQ // HY_KB,),
        in_specs=[full(zpos), full(w1), full(b1), full(w2), full(b2), full(freq), full(w3), full(decay), kblk, kblk],
        out_specs=[oblk, oblk, pl.BlockSpec((V7X_SUBLANES, DG), lambda k: (0, 0))],
        out_shape=[jax.ShapeDtypeStruct((SEQ, DG), F32), jax.ShapeDtypeStruct((SEQ, DG), F32),
                   jax.ShapeDtypeStruct((V7X_SUBLANES, DG), F32)],
        scratch_shapes=[pltpu.VMEM((SEQ, DG), F32), pltpu.VMEM((SEQ, DG), F32)],
        compiler_params=_params("arbitrary"), name="hyena_filter",
    )(zpos, w1, b1, w2, b2, freq, w3, decay, cos_f32, sin_f32)


CONV_ROWS = 128
CONV_HALO = 8


def _dwconv_rows(pad_ref, w_ref, r0, lanes, k):
    n = CONV_ROWS + 2 * CONV_HALO
    win = pad_ref[pl.ds(r0, n), lanes]
    acc = None
    for j in range(k):
        sh = (k // 2 - j) % n
        rolled = win if sh == 0 else pltpu.roll(win, sh, 0)
        term = rolled[CONV_HALO:CONV_HALO + CONV_ROWS] * w_ref[j:j + 1, lanes]
        acc = term if acc is None else acc + term
    return acc


def _fill_padded(pad_ref, src_ref, width):
    zeros = jnp.zeros((CONV_HALO, width), F32)
    pad_ref[pl.ds(0, CONV_HALO), :] = zeros
    pad_ref[pl.ds(SEQ + CONV_HALO, CONV_HALO), :] = zeros

    def fill(c, _):
        r0 = pl.multiple_of(c * CONV_ROWS, CONV_ROWS)
        pad_ref[pl.ds(r0 + CONV_HALO, CONV_ROWS), :] = src_ref[0, pl.ds(r0, CONV_ROWS), :].astype(F32)
        return 0

    lax.fori_loop(0, SEQ // CONV_ROWS, fill, 0)


def _hy_prep_body(p_ref, w_ref, z_ref, x0_ref, pad):
    _fill_padded(pad, p_ref, 3 * DG)

    def rows(c, _):
        r0 = pl.multiple_of(c * CONV_ROWS, CONV_ROWS)
        x0 = _dwconv_rows(pad, w_ref, r0, slice(0, DG), 3)
        x1 = _dwconv_rows(pad, w_ref, r0, slice(DG, 2 * DG), 3)
        v = _dwconv_rows(pad, w_ref, r0, slice(2 * DG, 3 * DG), 3)
        x0_ref[0, pl.ds(r0, CONV_ROWS), :] = x0.astype(BF16)
        z_ref[0, pl.ds(r0, CONV_ROWS), :] = (v * x1).astype(BF16)
        return 0

    lax.fori_loop(0, SEQ // CONV_ROWS, rows, 0)


def hyena_prep(p3, conv_w):
    b = p3.shape[0]
    blk = pl.BlockSpec((1, SEQ, DG), lambda i: (i, 0, 0))
    out = jax.ShapeDtypeStruct((b, SEQ, DG), BF16)
    return pl.pallas_call(
        _hy_prep_body, grid=(b,),
        in_specs=[pl.BlockSpec((1, SEQ, 3 * DG), lambda i: (i, 0, 0)), pl.BlockSpec(conv_w.shape, lambda i: (0, 0))],
        out_specs=[blk, blk], out_shape=[out, out],
        scratch_shapes=[pltpu.VMEM((SEQ + 2 * CONV_HALO, 3 * DG), F32)],
        compiler_params=_params("parallel"), name="hyena_prep",
    )(p3, conv_w)


HY_G = 2
HY_FB = 512


def _hy_conv_body(z_ref, x0_ref, cr_ref, sr_ref, cc_ref, sc_ref, kr_ref, ki_ref, kny_ref, fb_ref, o_ref, acc):
    kb = pl.program_id(1)
    krow = kb * HY_FB + lax.broadcasted_iota(I32, (HY_FB, 1), 0)
    wk = jnp.where(krow == 0, 1.0 / NFFT, 2.0 / NFFT)
    kr = kr_ref[...]
    ki = ki_ref[...]
    for g in range(HY_G):
        z = z_ref[g]
        zr = _dot(cr_ref[...], z)
        zi = _dot(sr_ref[...], z)
        yr = ((zr * kr - zi * ki) * wk).astype(BF16)
        yi = ((zr * ki + zi * kr) * wk).astype(BF16)
        part = _dot(cc_ref[...], yr) + _dot(sc_ref[...], yi)

        @pl.when(kb == 0)
        def _():
            acc[g] = part

        @pl.when(kb > 0)
        def _():
            acc[g] += part

    @pl.when(kb == pl.num_programs(1) - 1)
    def _():
        sgn = (1 - 2 * (lax.broadcasted_iota(I32, (SEQ, DG), 0) & 1)).astype(F32)
        for g in range(HY_G):
            zf = z_ref[g].astype(F32)
            zny = jnp.sum(zf * sgn, axis=0, keepdims=True)
            conv = acc[g] + (zny * kny_ref[0:1, :] * (1.0 / NFFT)) * sgn
            o_ref[g] = (x0_ref[g].astype(F32) * (conv + zf * fb_ref[...])).astype(BF16)


def hyena_conv(z3, x03, cos_bf, sin_bf, kr, ki, kny, fbias):
    b = z3.shape[0]
    seq_blk = pl.BlockSpec((HY_G, SEQ, DG), lambda i, k: (i, 0, 0))
    rows = pl.BlockSpec((HY_FB, SEQ), lambda i, k: (k, 0))
    cols = pl.BlockSpec((SEQ, HY_FB), lambda i, k: (0, k))
    kblk = pl.BlockSpec((HY_FB, DG), lambda i, k: (k, 0))
    return pl.pallas_call(
        _hy_conv_body, grid=(b // HY_G, SEQ // HY_FB),
        in_specs=[seq_blk, seq_blk, rows, rows, cols, cols, kblk, kblk,
                  pl.BlockSpec(kny.shape, lambda i, k: (0, 0)), pl.BlockSpec(fbias.shape, lambda i, k: (0, 0))],
        out_specs=seq_blk, out_shape=jax.ShapeDtypeStruct((b, SEQ, DG), BF16),
        scratch_shapes=[pltpu.VMEM((HY_G, SEQ, DG), F32)],
        compiler_params=_params("parallel", "arbitrary"), name="hyena_conv",
    )(z3, x03, cos_bf, sin_bf, cos_bf, sin_bf, kr, ki, kny, fbias)


N_MCH = SEQ // M_CHUNK
MQ = M_CHUNK


def _head_lane_vec(rows8, base):
    lane_head = lax.broadcasted_iota(I32, (1, DG), 1) // HD
    out = jnp.zeros((1, DG), F32)
    for h in range(N_HEADS):
        out = jnp.where(lane_head == h, rows8[base + h:base + h + 1, :], out)
    return out


def _mamba_body(z_ref, xbc_ref, dtc_ref, cw_ref, cb_ref, dtb_ref, a_ref, dsk_ref, nw_ref, tri_ref, bd_ref,
                o_ref, pad, xs_s, b_s, c_s, y_s, col_s):
    _fill_padded(pad, xbc_ref, 2 * DG)

    def conv_rows(c, _):
        r0 = pl.multiple_of(c * CONV_ROWS, CONV_ROWS)
        for g in range(4):
            lanes = slice(g * V7X_LANES, (g + 1) * V7X_LANES)
            u = _silu(_dwconv_rows(pad, cw_ref, r0, lanes, M_CONV) + cb_ref[:, lanes])
            if g < 2:
                xs_s[pl.ds(r0, CONV_ROWS), lanes] = u
            elif g == 2:
                b_s[pl.ds(r0, CONV_ROWS), :] = u.astype(BF16)
            else:
                c_s[pl.ds(r0, CONV_ROWS), :] = u.astype(BF16)
        return 0

    lax.fori_loop(0, SEQ // CONV_ROWS, conv_rows, 0)

    li = lax.broadcasted_iota(I32, (MQ, MQ), 0)
    si = lax.broadcasted_iota(I32, (MQ, MQ), 1)
    lower = si <= li
    upper = si >= li
    bdmask = bd_ref[...]

    def chunk_common(c):
        dt = _softplus(dtc_ref[0, c] + dtb_ref[...])
        a = dt * a_ref[...]
        cum = _dot01_rhs(a, tri_ref[...])
        tot = cum[:, MQ - 1:MQ]
        suf = tot - cum + a
        row_dir = lax.broadcasted_iota(I32, (8, MQ), 0) // N_HEADS
        seg = jnp.where(row_dir == 0, cum, suf)
        wgt = jnp.exp(tot - seg) * dt
        carry = jnp.exp(seg)
        stack = jnp.concatenate([seg, wgt, carry, dt], axis=0)
        return dt, seg, tot, stack.T

    def pass_fwd(c, s_f):
        r0 = pl.multiple_of(c * MQ, MQ)
        dt, seg, tot, cols = chunk_common(c)
        col_s[c] = cols
        x = xs_s[pl.ds(r0, MQ), :]
        xb = x.astype(BF16)
        bm = b_s[pl.ds(r0, MQ), :]
        cm = c_s[pl.ds(r0, MQ), :]
        bmf = bm.astype(F32)
        cmf = cm.astype(F32)
        ydiag = []
        for h in range(N_HEADS):
            g = h // 2
            cb = lax.dot_general(cm[:, g * M_STATE:(g + 1) * M_STATE], bm[:, g * M_STATE:(g + 1) * M_STATE],
                                 _NT, preferred_element_type=F32)
            lf = jnp.where(lower, jnp.exp(jnp.minimum(cols[:, h:h + 1] - seg[h:h + 1, :], 0.0)), 0.0)
            lb = jnp.where(upper, jnp.exp(jnp.minimum(cols[:, 4 + h:5 + h] - seg[4 + h:5 + h, :], 0.0)), 0.0)
            m = cb * (lf * dt[h:h + 1, :] + lb * dt[4 + h:5 + h, :])
            ydiag.append(_dot(m.astype(BF16), xb[:, h * HD:(h + 1) * HD]))
        y = jnp.concatenate(ydiag, axis=1)
        cw = jnp.concatenate([cmf[:, (h // 2) * M_STATE:(h // 2 + 1) * M_STATE] * cols[:, 16 + h:17 + h]
                              for h in range(N_HEADS)], axis=1)
        y = y + _dot(cw.astype(BF16), s_f.astype(BF16))
        y_s[pl.ds(r0, MQ), :] = y
        bw = jnp.concatenate([bmf[:, (h // 2) * M_STATE:(h // 2 + 1) * M_STATE] * cols[:, 8 + h:9 + h]
                              for h in range(N_HEADS)], axis=1)
        upd = lax.dot_general(bw.astype(BF16), xb, _TN, preferred_element_type=F32)
        decay = _head_lane_vec(jnp.exp(tot), 0)
        return s_f * decay + upd * bdmask

    lax.fori_loop(0, N_MCH, pass_fwd, jnp.zeros((DG, DG), F32))

    def pass_bwd(i, s_b):
        c = N_MCH - 1 - i
        r0 = pl.multiple_of(c * MQ, MQ)
        cols = col_s[c]
        x = xs_s[pl.ds(r0, MQ), :]
        xb = x.astype(BF16)
        bmf = b_s[pl.ds(r0, MQ), :].astype(F32)
        cmf = c_s[pl.ds(r0, MQ), :].astype(F32)
        cw = jnp.concatenate([cmf[:, (h // 2) * M_STATE:(h // 2 + 1) * M_STATE] * cols[:, 20 + h:21 + h]
                              for h in range(N_HEADS)], axis=1)
        y_s[pl.ds(r0, MQ), :] += _dot(cw.astype(BF16), s_b.astype(BF16))
        bw = jnp.concatenate([bmf[:, (h // 2) * M_STATE:(h // 2 + 1) * M_STATE] * cols[:, 12 + h:13 + h]
                              for h in range(N_HEADS)], axis=1)
        upd = lax.dot_general(bw.astype(BF16), xb, _TN, preferred_element_type=F32)
        tot_rows = jnp.concatenate([cols[0:1, 20 + h:21 + h] for h in range(N_HEADS)], axis=0)
        lane_head = lax.broadcasted_iota(I32, (1, DG), 1) // HD
        decay = jnp.zeros((1, DG), F32)
        for h in range(N_HEADS):
            decay = jnp.where(lane_head == h, tot_rows[h:h + 1, :], decay)
        return s_b * decay + upd * bdmask

    lax.fori_loop(0, N_MCH, pass_bwd, jnp.zeros((DG, DG), F32))

    def finish(c, _):
        r0 = pl.multiple_of(c * CONV_ROWS, CONV_ROWS)
        y = y_s[pl.ds(r0, CONV_ROWS), :] + xs_s[pl.ds(r0, CONV_ROWS), :] * dsk_ref[...]
        y = y * _silu(z_ref[0, pl.ds(r0, CONV_ROWS), :].astype(F32))
        o_ref[0, pl.ds(r0, CONV_ROWS), :] = (_rms(y) * nw_ref[...]).astype(BF16)
        return 0

    lax.fori_loop(0, SEQ // CONV_ROWS, finish, 0)


def mamba2(z3, xbc3, dtc4, conv_w, conv_b, dt_bias_col, a_col, dskip_lanes, norm_w, tri_incl, bdmask):
    b = z3.shape[0]
    full = lambda a: pl.BlockSpec(a.shape, lambda i: (0,) * a.ndim)
    return pl.pallas_call(
        _mamba_body, grid=(b,),
        in_specs=[pl.BlockSpec((1, SEQ, DG), lambda i: (i, 0, 0)),
                  pl.BlockSpec((1, SEQ, 2 * DG), lambda i: (i, 0, 0)),
                  pl.BlockSpec((1, N_MCH, 8, MQ), lambda i: (i, 0, 0, 0)),
                  full(conv_w), full(conv_b), full(dt_bias_col), full(a_col), full(dskip_lanes), full(norm_w),
                  full(tri_incl), full(bdmask)],
        out_specs=pl.BlockSpec((1, SEQ, DG), lambda i: (i, 0, 0)),
        out_shape=jax.ShapeDtypeStruct((b, SEQ, DG), BF16),
        scratch_shapes=[pltpu.VMEM((SEQ + 2 * CONV_HALO, 2 * DG), F32),
                        pltpu.VMEM((SEQ, DG), F32),
                        pltpu.VMEM((SEQ, 2 * M_STATE), BF16),
                        pltpu.VMEM((SEQ, 2 * M_STATE), BF16),
                        pltpu.VMEM((SEQ, DG), F32),
                        pltpu.VMEM((N_MCH, MQ, 32), F32)],
        compiler_params=_params("parallel"), name="mamba2",
    )(z3, xbc3, dtc4, conv_w, conv_b, dt_bias_col, a_col, dskip_lanes, norm_w, tri_incl, bdmask)


A_TQ = 128
A_ROWS = 256


def _attn_bias_body(ids_ref, rb_ref, o_ref):
    ids = ids_ref[0]
    for h in range(N_HEADS):
        acc = jnp.full(ids.shape, NEG_BIG, F32)
        for bkt in range(N_BUCKETS):
            acc = jnp.where(ids == bkt, rb_ref[bkt, h], acc)
        o_ref[h, 0] = acc


def attention_bias_table(bucket_ids, rel_bias):
    nvar, tq, w = bucket_ids.shape
    return pl.pallas_call(
        _attn_bias_body, grid=(nvar,),
        in_specs=[pl.BlockSpec((1, tq, w), lambda v: (v, 0, 0)),
                  pl.BlockSpec(memory_space=pltpu.SMEM)],
        out_specs=pl.BlockSpec((N_HEADS, 1, tq, w), lambda v: (0, v, 0, 0)),
        out_shape=jax.ShapeDtypeStruct((N_HEADS, nvar, tq, w), F32),
        compiler_params=_params("parallel"), name="attention_bias_table",
    )(bucket_ids, rel_bias)


A_SLABS = 3 * DG // V7X_LANES
A_QBLOCKS = SEQ // A_TQ


def _attn_body(at_ref, b1_ref, b4_ref, b16_ref, o_ref, qkv_s, acc_o, acc_l):
    def fill(c, _):
        r0 = pl.multiple_of(c * A_ROWS, A_ROWS)
        for s in range(A_SLABS):
            qkv_s[s, pl.ds(r0, A_ROWS), :] = at_ref[0, pl.ds(r0, A_ROWS), s * V7X_LANES:(s + 1) * V7X_LANES].astype(F32)
        return 0

    lax.fori_loop(0, SEQ // A_ROWS, fill, 0)
    first_head = lax.broadcasted_iota(I32, (A_TQ, V7X_LANES), 1) < HD

    def run_pattern(dil, bias_ref):
        n = SEQ // dil
        nblk = n // A_TQ
        nvar = bias_ref.shape[1]
        w = bias_ref.shape[3]

        def rows(start, size):
            return pl.ds(start, size) if dil == 1 else pl.ds(start, size, stride=dil)

        def block(it, _):
            r = it // nblk
            i = it - r * nblk
            q0 = i * A_TQ
            if nvar == 1:
                k0 = 0
                var = 0
            else:
                k0 = jnp.clip(q0 - A_BAND, 0, n - w)
                var = jnp.where(i == 0, 0, jnp.where(i == nblk - 1, 2, 1))
            qrows = rows(r + dil * q0, A_TQ)
            krows = rows(r + dil * k0, w)
            for hp in range(2):
                q2 = qkv_s[hp, qrows, :]
                k2 = qkv_s[2 + hp, krows, :].astype(BF16)
                v2 = qkv_s[4 + hp, krows, :].astype(BF16)
                outs, lses = [], []
                for hh in range(2):
                    keep = first_head if hh == 0 else jnp.logical_not(first_head)
                    qm = jnp.where(keep, q2, 0.0).astype(BF16)
                    s = lax.dot_general(qm, k2, _NT, preferred_element_type=F32) * (HD ** -0.5)
                    s = s + bias_ref[2 * hp + hh, var]
                    m = jnp.max(s, axis=1, keepdims=True)
                    p = jnp.exp(s - m)
                    den = jnp.sum(p, axis=1, keepdims=True)
                    outs.append(_dot(p.astype(BF16), v2) / den)
                    lses.append(m + jnp.log(den))
                o_new = jnp.where(first_head, outs[0], outs[1])
                l_new = jnp.where(first_head, lses[0], lses[1])
                if dil == A_DILS[0]:
                    acc_o[hp, qrows, :] = o_new
                    acc_l[hp, qrows, :] = l_new
                else:
                    o_old = acc_o[hp, qrows, :]
                    l_old = acc_l[hp, qrows, :]
                    mx = jnp.maximum(l_old, l_new)
                    w_old = jnp.exp(l_old - mx)
                    w_new = jnp.exp(l_new - mx)
                    tot = w_old + w_new
                    acc_o[hp, qrows, :] = (w_old * o_old + w_new * o_new) / tot
                    acc_l[hp, qrows, :] = mx + jnp.log(tot)
            return 0

        lax.fori_loop(0, A_QBLOCKS, block, 0)

    run_pattern(A_DILS[0], b1_ref)
    run_pattern(A_DILS[1], b4_ref)
    run_pattern(A_DILS[2], b16_ref)

    def finish(c, _):
        r0 = pl.multiple_of(c * A_ROWS, A_ROWS)
        for hp in range(2):
            o_ref[0, pl.ds(r0, A_ROWS), hp * V7X_LANES:(hp + 1) * V7X_LANES] = acc_o[hp, pl.ds(r0, A_ROWS), :].astype(BF16)
        return 0

    lax.fori_loop(0, SEQ // A_ROWS, finish, 0)


def dilated_attention(at3, bias1, bias4, bias16):
    b = at3.shape[0]
    full = lambda a: pl.BlockSpec(a.shape, lambda i: (0,) * a.ndim)
    return pl.pallas_call(
        _attn_body, grid=(b,),
        in_specs=[pl.BlockSpec((1, SEQ, 3 * DG), lambda i: (i, 0, 0)), full(bias1), full(bias4), full(bias16)],
        out_specs=pl.BlockSpec((1, SEQ, DG), lambda i: (i, 0, 0)),
        out_shape=jax.ShapeDtypeStruct((b, SEQ, DG), BF16),
        scratch_shapes=[pltpu.VMEM((A_SLABS, SEQ, V7X_LANES), F32),
                        pltpu.VMEM((2, SEQ, V7X_LANES), F32),
                        pltpu.VMEM((2, SEQ, V7X_LANES), F32)],
        compiler_params=_params("parallel"), name="dilated_attention",
    )(at3, bias1, bias4, bias16)


H_BLK = 256
H_CPB = H_BLK // H_CHUNK
N_HBLK = SEQ // H_BLK
N_HCH = SEQ // H_CHUNK


def _chunk_bcast(x, row):
    c = x.shape[1]
    x3 = x.reshape(H_CPB, H_CHUNK, c)
    return jnp.broadcast_to(x3[:, row:row + 1, :], (H_CPB, H_CHUNK, c)).reshape(H_BLK, c)


def _hgrn_body(p_ref, lb_ref, nw_ref, tin_ref, bd_ref, o_ref, qd_s, ut_s, oi_s, dec_s):
    li = lax.broadcasted_iota(I32, (H_BLK, H_BLK), 0)
    si = lax.broadcasted_iota(I32, (H_BLK, H_BLK), 1)
    same = (li // H_CHUNK) == (si // H_CHUNK)
    mask_f = same & (si <= li)
    mask_b = same & (si >= li)
    bdmask = bd_ref[...]

    def block(bi, _):
        r0 = pl.multiple_of(bi * H_BLK, H_BLK)
        rows = pl.ds(r0, H_BLK)
        q = _silu(p_ref[0, rows, 0:DG].astype(F32))
        v = p_ref[0, rows, 3 * DG:4 * DG]
        oi = jnp.zeros((H_BLK, DG), F32)
        scores = [None] * N_HEADS
        for d in range(2):
            fpre = p_ref[0, rows, (1 + d) * DG:(2 + d) * DG].astype(F32)
            lb = lb_ref[d:d + 1, :]
            sg = jax.nn.sigmoid(fpre)
            g = jnp.log(lb + (1.0 - lb) * sg)
            k = (1.0 - lb) * (1.0 - sg)
            gi = _dot01(tin_ref[...], g)
            glast = _chunk_bcast(gi, H_CHUNK - 1)
            if d == 0:
                gc = gi
                gref = _chunk_bcast(gi, H_CHUNK // 2 - 1)
                msk = mask_f
            else:
                gc = glast - gi + g
                gref = _chunk_bcast(gc, H_CHUNK // 2)
                msk = mask_b
            qe = (q * jnp.exp(gc - gref)).astype(BF16)
            ke = (k * jnp.exp(gref - gc)).astype(BF16)
            for h in range(N_HEADS):
                hs = slice(h * HD, (h + 1) * HD)
                sc = jnp.where(msk, lax.dot_general(qe[:, hs], ke[:, hs], _NT, preferred_element_type=F32), 0.0)
                scores[h] = sc if d == 0 else scores[h] + sc
            qd_s[d, rows, :] = (q * jnp.exp(gc)).astype(BF16)
            kd = (k * jnp.exp(glast - gc)).astype(BF16)
            for j in range(H_CPB):
                cr = slice(j * H_CHUNK, (j + 1) * H_CHUNK)
                ut = lax.dot_general(v[cr, :], kd[cr, :], _TN, preferred_element_type=F32)
                ut_s[d, bi * H_CPB + j] = (ut * bdmask).astype(BF16)
                dec_s[d, bi * H_CPB + j] = jnp.broadcast_to(jnp.exp(glast[j * H_CHUNK:j * H_CHUNK + 1, :]),
                                                             (V7X_SUBLANES, DG))
        oi = jnp.concatenate([_dot(scores[h].astype(BF16), v[:, h * HD:(h + 1) * HD]) for h in range(N_HEADS)],
                             axis=1)
        oi_s[rows, :] = oi
        return 0

    lax.fori_loop(0, N_HBLK, block, 0)

    def scan_dir(d):
        def step(i, st):
            c = i if d == 0 else N_HCH - 1 - i
            r0 = pl.multiple_of(c * H_CHUNK, H_CHUNK)
            qd = qd_s[d, pl.ds(r0, H_CHUNK), :]
            inter = lax.dot_general(qd, st.astype(BF16), _NT, preferred_element_type=F32)
            oi_s[pl.ds(r0, H_CHUNK), :] += inter
            return st * dec_s[d, c][0:1, :] + ut_s[d, c].astype(F32)

        lax.fori_loop(0, N_HCH, step, jnp.zeros((DG, DG), F32))

    scan_dir(0)
    scan_dir(1)

    def finish(c, _):
        r0 = pl.multiple_of(c * CONV_ROWS, CONV_ROWS)
        rows = pl.ds(r0, CONV_ROWS)
        o = oi_s[rows, :]
        gate = _silu(p_ref[0, rows, 4 * DG:5 * DG].astype(F32))
        outs = [_rms(o[:, h * HD:(h + 1) * HD]) for h in range(N_HEADS)]
        o_ref[0, rows, :] = (jnp.concatenate(outs, axis=1) * nw_ref[...] * gate).astype(BF16)
        return 0

    lax.fori_loop(0, SEQ // CONV_ROWS, finish, 0)


def hgrn2(p3, lb2, norm_w_lanes, tri_in_chunk, bdmask):
    b = p3.shape[0]
    full = lambda a: pl.BlockSpec(a.shape, lambda i: (0,) * a.ndim)
    return pl.pallas_call(
        _hgrn_body, grid=(b,),
        in_specs=[pl.BlockSpec((1, SEQ, 5 * DG), lambda i: (i, 0, 0)), full(lb2), full(norm_w_lanes),
                  full(tri_in_chunk), full(bdmask)],
        out_specs=pl.BlockSpec((1, SEQ, DG), lambda i: (i, 0, 0)),
        out_shape=jax.ShapeDtypeStruct((b, SEQ, DG), BF16),
        scratch_shapes=[pltpu.VMEM((2, SEQ, DG), BF16),
                        pltpu.VMEM((2, N_HCH, DG, DG), BF16),
                        pltpu.VMEM((SEQ, DG), F32),
                        pltpu.VMEM((2, N_HCH, V7X_SUBLANES, DG), F32)],
        compiler_params=_params("parallel"), name="hgrn2",
    )(p3, lb2, norm_w_lanes, tri_in_chunk, bdmask)


@functools.lru_cache(maxsize=None)
def _tables():
    t = {}
    k = np.arange(SEQ, dtype=np.int64)
    ang = 2.0 * np.pi * ((k[:, None] * k[None, :]) % NFFT).astype(np.float64) / NFFT
    t["cos"] = np.cos(ang).astype(np.float32)
    t["sin"] = np.sin(ang).astype(np.float32)
    tt = np.linspace(0.0, 1.0, SEQ, dtype=np.float32)[:, None]
    bands = (HY_POS_DIM - 1) // 2
    ang_pos = (2.0 * math.pi * np.arange(SEQ, dtype=np.float32) / SEQ).astype(np.float32)
    f = np.linspace(1e-4, bands - 1, bands, dtype=np.float32)
    a2 = (ang_pos[:, None] * f[None, :]).astype(np.float32)
    z = np.concatenate([tt, np.cos(a2), -np.sin(a2)], axis=-1).astype(np.float32)
    zp = np.zeros((SEQ, V7X_LANES), np.float32)
    zp[:, :HY_POS_DIM] = z
    t["zpos"] = zp
    max_decay = math.log(1e-2) / 0.3
    min_decay = math.log(1e-2) / 1.5
    deltas = np.abs(np.linspace(min_decay, max_decay, DG, dtype=np.float32))
    t["decay"] = np.exp(-tt * deltas[None, :]).astype(np.float32)
    i128 = np.arange(V7X_LANES)
    t["u128"] = (i128[:, None] < i128[None, :]).astype(np.float32)
    im = np.arange(M_CHUNK)
    t["tri_incl"] = (im[:, None] <= im[None, :]).astype(np.float32)
    ib = np.arange(H_BLK)
    t["tri_in_chunk"] = ((ib[:, None] // H_CHUNK == ib[None, :] // H_CHUNK)
                         & (ib[None, :] <= ib[:, None])).astype(np.float32)
    idg = np.arange(DG)
    t["bdmask"] = (idg[:, None] // HD == idg[None, :] // HD).astype(np.float32)
    def bucket(rel):
        nb = N_BUCKETS // 2
        max_exact = nb // 2
        ret = (rel > 0).astype(np.int64) * nb
        n = np.abs(rel)
        nf = np.maximum(n, 1).astype(np.float64)
        large = max_exact + (np.log(nf / max_exact) / math.log(MAX_DISTANCE / max_exact)
                             * (nb - max_exact)).astype(np.int64)
        large = np.minimum(large, nb - 1)
        return ret + np.where(n < max_exact, n, large)

    for dil in A_DILS:
        n = SEQ // dil
        w = min(n, A_TQ + 2 * A_BAND)
        starts = [0] if n == w else [0, -A_BAND, -(w - A_TQ)]
        qi = np.arange(A_TQ)[:, None]
        kj = np.arange(w)[None, :]
        ids = []
        for s0 in starts:
            rel = kj + s0 - qi
            ids.append(np.where(np.abs(rel) <= A_BAND, bucket(rel * dil), -1))
        t[f"bucket{dil}"] = np.stack(ids).astype(np.int32)
    return t


def kernel(x, w_in, w_out, norm_mix_w, norm_ffn_w, hy_conv_w, hy_pos_w1, hy_pos_b1, hy_pos_w2, hy_pos_b2,
           hy_sin_freq, hy_pos_w3, hy_filt_bias, m_conv_w, m_conv_b, m_dt_bias, m_A_log, m_D, m_norm_w, rel_bias,
           hg_lb, hg_norm_w, router_w, moe_w_gate, moe_w_up, moe_w_down, final_norm_w):
    b = x.shape[0]
    t = b * SEQ
    tb = _tables()
    cos_f32 = jnp.asarray(tb["cos"])
    sin_f32 = jnp.asarray(tb["sin"])
    cos_bf = cos_f32.astype(BF16)
    sin_bf = sin_f32.astype(BF16)
    u128 = jnp.asarray(tb["u128"]).astype(BF16)
    tri_incl = jnp.asarray(tb["tri_incl"]).astype(BF16)
    tri_in_chunk = jnp.asarray(tb["tri_in_chunk"]).astype(BF16)
    bdmask = jnp.asarray(tb["bdmask"])
    attn_bias = [attention_bias_table(jnp.asarray(tb[f"bucket{d}"]), rel_bias.astype(F32)) for d in A_DILS]

    sm = jax.nn.softmax(hg_lb.astype(F32), axis=0)
    lower_bounds = jnp.cumsum(sm, axis=0) - sm[:1]

    xa = x.reshape(t, D_MODEL)
    xb = None
    for l in range(DEPTH):
        wl = w_in[l]
        w_main = jnp.concatenate([wl[:, 0:768], wl[:, 768:1024], wl[:, 1024:1536], wl[:, 1544:2312],
                                  wl[:, 2312:3592]], axis=1).astype(BF16)
        w_dt_rows = wl[:, 1536:1544].T.astype(BF16)
        hy, mz, mx, at, hg, dtc = in_projection(xa, xb, norm_mix_w[l][None, :], w_main, w_dt_rows)

        w1p = jnp.zeros((V7X_LANES, HY_HID), F32).at[:HY_POS_DIM].set(hy_pos_w1[l])
        kr, ki, kny = hyena_filter_spectrum(
            jnp.asarray(tb["zpos"]), w1p, hy_pos_b1[l][None, :], hy_pos_w2[l], hy_pos_b2[l][None, :],
            hy_sin_freq[l][None, :], hy_pos_w3[l], jnp.asarray(tb["decay"]), cos_f32, sin_f32)
        z3, x03 = hyena_prep(hy.reshape(b, SEQ, 3 * DG), hy_conv_w[l])
        ya = hyena_conv(z3, x03, cos_bf, sin_bf, kr, ki, kny, hy_filt_bias[l][None, :]).reshape(t, DG)

        a_col = (-jnp.exp(m_A_log[l].astype(F32))).reshape(8, 1)
        yb = mamba2(mz.reshape(b, SEQ, DG), mx.reshape(b, SEQ, 2 * DG), dtc.reshape(b, N_MCH, 8, MQ),
                    m_conv_w[l], m_conv_b[l][None, :], m_dt_bias[l].reshape(8, 1), a_col,
                    jnp.repeat(m_D[l].astype(F32), HD)[None, :], m_norm_w[l][None, :], tri_incl, bdmask).reshape(t, DG)

        yc = dilated_attention(at.reshape(b, SEQ, 3 * DG), *attn_bias).reshape(t, DG)

        lbl = lower_bounds[l]
        yd = hgrn2(hg.reshape(b, SEQ, 5 * DG), lbl, jnp.tile(hg_norm_w[l], N_HEADS)[None, :],
                   tri_in_chunk, bdmask).reshape(t, DG)

        xo, xn = out_projection(xa, xb, ya, yb, yc, yd, w_out[l].reshape(4, DG, D_MODEL).astype(BF16),
                                norm_ffn_w[l][None, :])
        xn3 = xn.reshape(b, SEQ, D_MODEL)
        rw_rows = router_w[l].T.astype(F32)
        rw_hi = rw_rows.astype(BF16)
        rw_lo = (rw_rows - rw_hi.astype(F32)).astype(BF16)
        rank, gate = router(xo.reshape(b, SEQ, D_MODEL), norm_ffn_w[l][None, :], rw_hi, rw_lo, u128)
        moe = moe_ffn(xn3, rank.reshape(b, N_EXPERTS, 1, SEQ), gate.reshape(b, N_EXPERTS, 1, SEQ),
                      moe_w_gate[l].astype(BF16), moe_w_up[l].astype(BF16), moe_w_down[l].astype(BF16))
        xa, xb = xo, moe.reshape(t, D_MODEL)
    return final_norm(xa, xb, final_norm_w[None, :]).reshape(b, SEQ, D_MODEL)
```

```python
import functools
import math

import numpy as np
import jax
import jax.numpy as jnp
from jax import lax
from jax.experimental import pallas as pl
from jax.experimental.pallas import tpu as pltpu

F32 = jnp.float32
BF16 = jnp.bfloat16
I32 = jnp.int32

D_MODEL = 1024
SEQ = 2048
DEPTH = 2
DG = 256
N_HEADS = 4
HD = 64
HY_POS_DIM = 33
HY_HID = 64
M_CONV = 5
M_STATE = 64
M_CHUNK = 128
H_CHUNK = 32
A_BAND = 64
A_DILS = (1, 4, 16)
N_BUCKETS = 32
MAX_DISTANCE = 1024
N_EXPERTS = 16
CAP = 2 * SEQ // N_EXPERTS
D_FF = 1024
EPS = 1e-6
NFFT = 2 * SEQ

V7X_LANES = 128
V7X_SUBLANES = 8
V7X_VMEM_LIMIT_BYTES = 56 * 1024 * 1024

NEG_BIG = -1e30

_NT = (((1,), (1,)), ((), ()))
_TN = (((0,), (0,)), ((), ()))


def _params(*sem):
    return pltpu.CompilerParams(dimension_semantics=sem, vmem_limit_bytes=V7X_VMEM_LIMIT_BYTES)


def _dot(a, b):
    return jnp.dot(a, b, preferred_element_type=F32)


def _dot_hi(a, b):
    return jnp.dot(a, b, preferred_element_type=F32, precision=lax.Precision.HIGHEST)


def _dot01(t_bf16, x):
    x1 = x.astype(BF16)
    r1 = x - x1.astype(F32)
    x2 = r1.astype(BF16)
    x3 = (r1 - x2.astype(F32)).astype(BF16)
    return _dot(t_bf16, x1) + _dot(t_bf16, x2) + _dot(t_bf16, x3)


def _dot01_rhs(x, t_bf16):
    x1 = x.astype(BF16)
    r1 = x - x1.astype(F32)
    x2 = r1.astype(BF16)
    x3 = (r1 - x2.astype(F32)).astype(BF16)
    return _dot(x1, t_bf16) + _dot(x2, t_bf16) + _dot(x3, t_bf16)


def _silu(x):
    return x * jax.nn.sigmoid(x)


def _softplus(x):
    return jnp.maximum(x, 0.0) + jnp.log(1.0 + jnp.exp(-jnp.abs(x)))


def _rms(x):
    return x * lax.rsqrt(jnp.mean(x * x, axis=-1, keepdims=True) + EPS)


TM_PROJ = 512
_HY0, _MZ0, _MX0, _AT0, _HG0, _PEND = 0, 768, 1024, 1536, 2304, 3584


def _inproj_body(has_b, *refs):
    if has_b:
        xa_ref, xb_ref, nw_ref, w_ref, wdt_ref, hy_ref, mz_ref, mx_ref, at_ref, hg_ref, dtc_ref = refs
        x = xa_ref[...] + xb_ref[...]
    else:
        xa_ref, nw_ref, w_ref, wdt_ref, hy_ref, mz_ref, mx_ref, at_ref, hg_ref, dtc_ref = refs
        x = xa_ref[...]
    hn = (_rms(x) * nw_ref[...]).astype(BF16)
    hy_ref[...] = _dot(hn, w_ref[:, _HY0:_MZ0]).astype(BF16)
    mz_ref[...] = _dot(hn, w_ref[:, _MZ0:_MX0]).astype(BF16)
    mx_ref[...] = _dot(hn, w_ref[:, _MX0:_AT0]).astype(BF16)
    at_ref[...] = _dot(hn, w_ref[:, _AT0:_HG0]).astype(BF16)
    hg_ref[...] = _dot(hn, w_ref[:, _HG0:_PEND]).astype(BF16)
    dt_rows = lax.dot_general(wdt_ref[...], hn, _NT, preferred_element_type=F32)
    for j in range(TM_PROJ // M_CHUNK):
        dtc_ref[j] = dt_rows[:, j * M_CHUNK:(j + 1) * M_CHUNK]


def in_projection(xa, xb, norm_w, w_main, w_dt_rows):
    t = xa.shape[0]
    tm = TM_PROJ
    has_b = xb is not None
    row = lambda w: pl.BlockSpec((tm, w), lambda i: (i, 0))
    full = lambda a: pl.BlockSpec(a.shape, lambda i: (0,) * a.ndim)
    ins = [xa] + ([xb] if has_b else []) + [norm_w, w_main, w_dt_rows]
    in_specs = [row(D_MODEL)] * (2 if has_b else 1) + [full(norm_w), full(w_main), full(w_dt_rows)]
    widths = (768, 256, 512, 768, 1280)
    out_shape = [jax.ShapeDtypeStruct((t, w), BF16) for w in widths]
    out_shape.append(jax.ShapeDtypeStruct((t // M_CHUNK, 8, M_CHUNK), F32))
    out_specs = [row(w) for w in widths] + [pl.BlockSpec((tm // M_CHUNK, 8, M_CHUNK), lambda i: (i, 0, 0))]
    return pl.pallas_call(
        functools.partial(_inproj_body, has_b),
        grid=(t // tm,), in_specs=in_specs, out_specs=out_specs, out_shape=out_shape,
        compiler_params=_params("parallel"), name="in_projection",
    )(*ins)


TM_OUT = 512


def _outproj_body(has_b, *refs):
    if has_b:
        xa_ref, xb_ref, ya_ref, yb_ref, yc_ref, yd_ref, w_ref, nw_ref, xo_ref, xn_ref = refs
        x = xa_ref[...] + xb_ref[...]
    else:
        xa_ref, ya_ref, yb_ref, yc_ref, yd_ref, w_ref, nw_ref, xo_ref, xn_ref = refs
        x = xa_ref[...]
    acc = x + _dot(ya_ref[...], w_ref[0]) + _dot(yb_ref[...], w_ref[1])
    acc = acc + _dot(yc_ref[...], w_ref[2]) + _dot(yd_ref[...], w_ref[3])
    xo_ref[...] = acc
    xn_ref[...] = (_rms(acc) * nw_ref[...]).astype(BF16)


def out_projection(xa, xb, ya, yb, yc, yd, w_out4, norm_w):
    t = xa.shape[0]
    tm = TM_OUT
    has_b = xb is not None
    row = lambda w: pl.BlockSpec((tm, w), lambda i: (i, 0))
    full = lambda a: pl.BlockSpec(a.shape, lambda i: (0,) * a.ndim)
    ins = [xa] + ([xb] if has_b else []) + [ya, yb, yc, yd, w_out4, norm_w]
    in_specs = [row(D_MODEL)] * (2 if has_b else 1) + [row(DG)] * 4 + [full(w_out4), full(norm_w)]
    return pl.pallas_call(
        functools.partial(_outproj_body, has_b),
        grid=(t // tm,), in_specs=in_specs,
        out_specs=[row(D_MODEL), row(D_MODEL)],
        out_shape=[jax.ShapeDtypeStruct((t, D_MODEL), F32), jax.ShapeDtypeStruct((t, D_MODEL), BF16)],
        compiler_params=_params("parallel"), name="out_projection",
    )(*ins)


def _final_norm_body(xa_ref, xb_ref, nw_ref, o_ref):
    o_ref[...] = _rms(xa_ref[...] + xb_ref[...]) * nw_ref[...]


def final_norm(xa, xb, norm_w):
    t = xa.shape[0]
    tm = 1024
    row = pl.BlockSpec((tm, D_MODEL), lambda i: (i, 0))
    return pl.pallas_call(
        _final_norm_body, grid=(t // tm,),
        in_specs=[row, row, pl.BlockSpec(norm_w.shape, lambda i: (0, 0))],
        out_specs=row, out_shape=jax.ShapeDtypeStruct((t, D_MODEL), F32),
        compiler_params=_params("parallel"), name="final_norm",
    )(xa, xb, norm_w)


def _prefix_excl_lanes(mask_f32, u_ref):
    e = mask_f32.shape[0]
    off = jnp.zeros((e, 1), F32)
    parts, bounds = [], [off]
    for k in range(SEQ // V7X_LANES):
        tile = mask_f32[:, k * V7X_LANES:(k + 1) * V7X_LANES]
        parts.append(_dot(tile.astype(BF16), u_ref[...]) + off)
        off = off + jnp.sum(tile, axis=1, keepdims=True)
        bounds.append(off)
    return jnp.concatenate(parts, axis=1), bounds


def _router_body(xo_ref, nw_ref, rwh_ref, rwl_ref, u_ref, rank_ref, gate_ref, seg_ref):
    xn = _rms(xo_ref[0]) * nw_ref[...]
    xh = xn.astype(BF16)
    xl = (xn - xh.astype(F32)).astype(BF16)
    nt = lambda w, a: lax.dot_general(w, a, _NT, preferred_element_type=F32)
    logits = nt(rwh_ref[...], xh) + nt(rwh_ref[...], xl) + nt(rwl_ref[...], xh)
    mx = jnp.max(logits, axis=0, keepdims=True)
    ex = jnp.exp(logits - mx)
    aff = ex / jnp.sum(ex, axis=0, keepdims=True)
    bits = pltpu.bitcast(aff, I32)

    def search(i, thr):
        cand = thr | jnp.left_shift(jnp.int32(1), 30 - i)
        cnt = jnp.sum((bits >= cand).astype(I32), axis=1, keepdims=True)
        return jnp.where(cnt >= CAP, cand, thr)

    thr = lax.fori_loop(0, 31, search, jnp.zeros((N_EXPERTS, 1), I32))
    gt = (bits > thr).astype(F32)
    eq = (bits == thr).astype(F32)
    need = CAP - jnp.sum(gt, axis=1, keepdims=True)
    tie_rank, _ = _prefix_excl_lanes(eq, u_ref)
    sel = gt + eq * (tie_rank < need).astype(F32)
    rank, bounds = _prefix_excl_lanes(sel, u_ref)
    rank_ref[0] = jnp.where(sel > 0.0, rank, -1.0)
    gate_ref[0] = aff
    lane = lax.broadcasted_iota(I32, (N_EXPERTS, V7X_LANES), 1)
    seg = jnp.zeros((N_EXPERTS, V7X_LANES), F32)
    for sgm in range(N_MOE_SEG + 1):
        seg = jnp.where(lane == sgm, bounds[sgm * (MOE_SEG // V7X_LANES)], seg)
    seg_ref[0] = seg.astype(I32)


def router(xo3, norm_w, rw_hi, rw_lo, u128):
    b = xo3.shape[0]
    out = jax.ShapeDtypeStruct((b, N_EXPERTS, SEQ), F32)
    full = lambda a: pl.BlockSpec(a.shape, lambda i: (0,) * a.ndim)
    return pl.pallas_call(
        _router_body, grid=(b,),
        in_specs=[pl.BlockSpec((1, SEQ, D_MODEL), lambda i: (i, 0, 0)), full(norm_w), full(rw_hi), full(rw_lo),
                  full(u128)],
        out_specs=[pl.BlockSpec((1, N_EXPERTS, SEQ), lambda i: (i, 0, 0))] * 2
                  + [pl.BlockSpec((1, N_EXPERTS, V7X_LANES), lambda i: (i, 0, 0))],
        out_shape=[out, out, jax.ShapeDtypeStruct((b, N_EXPERTS, V7X_LANES), I32)],
        compiler_params=_params("parallel"), name="router",
    )(xo3, norm_w, rw_hi, rw_lo, u128)


MOE_SEG = 256
N_MOE_SEG = SEQ // MOE_SEG
MOE_TILE = 64
MOE_ALIGN = 16
MOE_GROUP = 4
MOE_SEG_STRIDE = 16
CAP_PAD = CAP + MOE_TILE


def _moe_body(cs_ref, xn_ref, rank_ref, gate_ref, wg_ref, wu_ref, wd_ref, o_ref, xy_s):
    b = pl.program_id(0)
    e = pl.program_id(1)

    def seg_plan(s):
        starts, rounds = [], jnp.int32(0)
        for ex in range(N_EXPERTS):
            base = (b * N_EXPERTS + ex) * MOE_SEG_STRIDE
            first = (cs_ref[base + s] // MOE_ALIGN) * MOE_ALIGN
            span = cs_ref[base + s + 1] - first
            starts.append(first)
            rounds = jnp.maximum(rounds, (span + MOE_TILE - 1) // MOE_TILE)
        return starts, rounds

    def tile_base(start, r):
        return pl.multiple_of(jnp.minimum(start + r * MOE_TILE, CAP), MOE_ALIGN)

    def onehot_group(s, bases, grp, weights):
        lanes = pl.ds(pl.multiple_of(s * MOE_SEG, MOE_SEG), MOE_SEG)
        j = lax.broadcasted_iota(I32, (MOE_TILE, MOE_SEG), 0)
        rows = []
        for ex in grp:
            slot = (bases[ex] + j).astype(F32)
            hit = rank_ref[0, ex:ex + 1, lanes] == slot
            val = gate_ref[0, ex:ex + 1, lanes] if weights else 1.0
            rows.append(jnp.where(hit, val, 0.0).astype(BF16))
        return jnp.concatenate(rows, axis=0)

    groups = [list(range(g * MOE_GROUP, (g + 1) * MOE_GROUP)) for g in range(N_EXPERTS // MOE_GROUP)]

    @pl.when(e == 0)
    def _():
        def zero(ex, _):
            xy_s[ex] = jnp.zeros((CAP_PAD, D_MODEL), BF16)
            return 0

        lax.fori_loop(0, N_EXPERTS, zero, 0)

        def seg_gather(s, _):
            starts, rounds = seg_plan(s)
            xn_seg = xn_ref[0, pl.ds(pl.multiple_of(s * MOE_SEG, MOE_SEG), MOE_SEG), :]

            def one_round(r, _):
                bases = [tile_base(st, r) for st in starts]
                for grp in groups:
                    got = _dot(onehot_group(s, bases, grp, False), xn_seg)
                    for k, ex in enumerate(grp):
                        rows = pl.ds(bases[ex], MOE_TILE)
                        old = xy_s[ex, rows, :].astype(F32)
                        xy_s[ex, rows, :] = (old + got[k * MOE_TILE:(k + 1) * MOE_TILE]).astype(BF16)
                return 0

            lax.fori_loop(0, rounds, one_round, 0)
            return 0

        lax.fori_loop(0, N_MOE_SEG, seg_gather, 0)

    xe = xy_s[e, 0:CAP, :]
    hid = (_silu(_dot(xe, wg_ref[0])) * _dot(xe, wu_ref[0])).astype(BF16)
    xy_s[e, 0:CAP, :] = _dot(hid, wd_ref[0]).astype(BF16)

    @pl.when(e == N_EXPERTS - 1)
    def _():
        def seg_scatter(s, _):
            starts, rounds = seg_plan(s)
            tok = pl.ds(pl.multiple_of(s * MOE_SEG, MOE_SEG), MOE_SEG)
            o_ref[0, tok, :] = jnp.zeros((MOE_SEG, D_MODEL), F32)

            def one_round(r, _):
                bases = [tile_base(st, r) for st in starts]
                for grp in groups:
                    ye = jnp.concatenate([xy_s[ex, pl.ds(bases[ex], MOE_TILE), :] for ex in grp], axis=0)
                    o_ref[0, tok, :] += lax.dot_general(onehot_group(s, bases, grp, True), ye, _TN,
                                                        preferred_element_type=F32)
                return 0

            lax.fori_loop(0, rounds, one_round, 0)
            return 0

        lax.fori_loop(0, N_MOE_SEG, seg_scatter, 0)


def moe_ffn(seg_counts_flat, xn3, rank3, gate3, w_gate, w_up, w_down):
    b = xn3.shape[0]
    sel_spec = pl.BlockSpec((1, N_EXPERTS, SEQ), lambda i, e, cs: (i, 0, 0))
    w_spec = lambda a: pl.BlockSpec((1,) + a.shape[1:], lambda i, e, cs: (e, 0, 0))
    grid_spec = pltpu.PrefetchScalarGridSpec(
        num_scalar_prefetch=1, grid=(b, N_EXPERTS),
        in_specs=[pl.BlockSpec((1, SEQ, D_MODEL), lambda i, e, cs: (i, 0, 0)), sel_spec, sel_spec,
                  w_spec(w_gate), w_spec(w_up), w_spec(w_down)],
        out_specs=pl.BlockSpec((1, SEQ, D_MODEL), lambda i, e, cs: (i, 0, 0)),
        scratch_shapes=[pltpu.VMEM((N_EXPERTS, CAP_PAD, D_MODEL), BF16)])
    return pl.pallas_call(
        _moe_body, grid_spec=grid_spec,
        out_shape=jax.ShapeDtypeStruct((b, SEQ, D_MODEL), F32),
        compiler_params=_params("parallel", "arbitrary"), name="moe_ffn",
    )(seg_counts_flat, xn3, rank3, gate3, w_gate, w_up, w_down)


HY_KB = 256
HY_ROWS = 256


def _hy_filter_body(z_ref, w1_ref, b1_ref, w2_ref, b2_ref, fr_ref, w3_ref, dec_ref, c_ref, s_ref,
                    kr_ref, ki_ref, kny_ref, a_s, d_s):
    @pl.when(pl.program_id(0) == 0)
    def _():
        def rows(c, kny):
            r0 = pl.multiple_of(c * HY_ROWS, HY_ROWS)
            fr = fr_ref[...]
            h = jnp.sin(fr * (_dot_hi(z_ref[pl.ds(r0, HY_ROWS), :], w1_ref[...]) + b1_ref[...]))
            h = jnp.sin(fr * (_dot_hi(h, w2_ref[...]) + b2_ref[...]))
            h = _dot_hi(h, w3_ref[...])
            dec = dec_ref[pl.ds(r0, HY_ROWS), :]
            pos = r0 + lax.broadcasted_iota(I32, (HY_ROWS, DG), 0)
            hf = h[:, :DG] * dec
            hb = jnp.where(pos == 0, 0.0, h[:, DG:] * dec)
            a = hf + hb
            a_s[pl.ds(r0, HY_ROWS), :] = a
            d_s[pl.ds(r0, HY_ROWS), :] = hf - hb
            sgn = (1 - 2 * (pos & 1)).astype(F32)
            return kny + jnp.sum(a * sgn, axis=0, keepdims=True)

        kny = lax.fori_loop(0, SEQ // HY_ROWS, rows, jnp.zeros((1, DG), F32))
        kny_ref[...] = jnp.broadcast_to(kny, kny_ref.shape)

    kr_ref[...] = _dot_hi(c_ref[...], a_s[...])
    ki_ref[...] = _dot_hi(s_ref[...], d_s[...])


def hyena_filter_spectrum(zpos, w1, b1, w2, b2, freq, w3, decay, cos_f32, sin_f32):
    full = lambda a: pl.BlockSpec(a.shape, lambda k: (0,) * a.ndim)
    kblk = pl.BlockSpec((HY_KB, SEQ), lambda k: (k, 0))
    oblk = pl.BlockSpec((HY_KB, DG), lambda k: (k, 0))
    return pl.pallas_call(
        _hy_filter_body, grid=(SEQ // HY_KB,),
        in_specs=[full(zpos), full(w1), full(b1), full(w2), full(b2), full(freq), full(w3), full(decay), kblk, kblk],
        out_specs=[oblk, oblk, pl.BlockSpec((V7X_SUBLANES, DG), lambda k: (0, 0))],
        out_shape=[jax.ShapeDtypeStruct((SEQ, DG), F32), jax.ShapeDtypeStruct((SEQ, DG), F32),
                   jax.ShapeDtypeStruct((V7X_SUBLANES, DG), F32)],
        scratch_shapes=[pltpu.VMEM((SEQ, DG), F32), pltpu.VMEM((SEQ, DG), F32)],
        compiler_params=_params("arbitrary"), name="hyena_filter",
    )(zpos, w1, b1, w2, b2, freq, w3, decay, cos_f32, sin_f32)


CONV_ROWS = 128
CONV_HALO = 8


def _dwconv_rows(pad_ref, w_ref, r0, lanes, k):
    n = CONV_ROWS + 2 * CONV_HALO
    win = pad_ref[pl.ds(r0, n), lanes]
    acc = None
    for j in range(k):
        sh = (k // 2 - j) % n
        rolled = win if sh == 0 else pltpu.roll(win, sh, 0)
        term = rolled[CONV_HALO:CONV_HALO + CONV_ROWS] * w_ref[j:j + 1, lanes]
        acc = term if acc is None else acc + term
    return acc


def _fill_padded(pad_ref, src_ref, width):
    zeros = jnp.zeros((CONV_HALO, width), F32)
    pad_ref[pl.ds(0, CONV_HALO), :] = zeros
    pad_ref[pl.ds(SEQ + CONV_HALO, CONV_HALO), :] = zeros

    def fill(c, _):
        r0 = pl.multiple_of(c * CONV_ROWS, CONV_ROWS)
        pad_ref[pl.ds(r0 + CONV_HALO, CONV_ROWS), :] = src_ref[0, pl.ds(r0, CONV_ROWS), :].astype(F32)
        return 0

    lax.fori_loop(0, SEQ // CONV_ROWS, fill, 0)


def _hy_prep_body(p_ref, w_ref, z_ref, x0_ref, pad):
    _fill_padded(pad, p_ref, 3 * DG)

    def rows(c, _):
        r0 = pl.multiple_of(c * CONV_ROWS, CONV_ROWS)
        x0 = _dwconv_rows(pad, w_ref, r0, slice(0, DG), 3)
        x1 = _dwconv_rows(pad, w_ref, r0, slice(DG, 2 * DG), 3)
        v = _dwconv_rows(pad, w_ref, r0, slice(2 * DG, 3 * DG), 3)
        x0_ref[0, pl.ds(r0, CONV_ROWS), :] = x0.astype(BF16)
        z_ref[0, pl.ds(r0, CONV_ROWS), :] = (v * x1).astype(BF16)
        return 0

    lax.fori_loop(0, SEQ // CONV_ROWS, rows, 0)


def hyena_prep(p3, conv_w):
    b = p3.shape[0]
    blk = pl.BlockSpec((1, SEQ, DG), lambda i: (i, 0, 0))
    out = jax.ShapeDtypeStruct((b, SEQ, DG), BF16)
    return pl.pallas_call(
        _hy_prep_body, grid=(b,),
        in_specs=[pl.BlockSpec((1, SEQ, 3 * DG), lambda i: (i, 0, 0)), pl.BlockSpec(conv_w.shape, lambda i: (0, 0))],
        out_specs=[blk, blk], out_shape=[out, out],
        scratch_shapes=[pltpu.VMEM((SEQ + 2 * CONV_HALO, 3 * DG), F32)],
        compiler_params=_params("parallel"), name="hyena_prep",
    )(p3, conv_w)


HY_G = 2
HY_FB = 512


def _hy_conv_body(z_ref, x0_ref, cr_ref, sr_ref, cc_ref, sc_ref, kr_ref, ki_ref, kny_ref, fb_ref, o_ref, acc):
    kb = pl.program_id(1)
    krow = kb * HY_FB + lax.broadcasted_iota(I32, (HY_FB, 1), 0)
    wk = jnp.where(krow == 0, 1.0 / NFFT, 2.0 / NFFT)
    kr = kr_ref[...]
    ki = ki_ref[...]
    for g in range(HY_G):
        z = z_ref[g]
        zr = _dot(cr_ref[...], z)
        zi = _dot(sr_ref[...], z)
        yr = ((zr * kr - zi * ki) * wk).astype(BF16)
        yi = ((zr * ki + zi * kr) * wk).astype(BF16)
        part = _dot(cc_ref[...], yr) + _dot(sc_ref[...], yi)

        @pl.when(kb == 0)
        def _():
            acc[g] = part

        @pl.when(kb > 0)
        def _():
            acc[g] += part

    @pl.when(kb == pl.num_programs(1) - 1)
    def _():
        sgn = (1 - 2 * (lax.broadcasted_iota(I32, (SEQ, DG), 0) & 1)).astype(F32)
        for g in range(HY_G):
            zf = z_ref[g].astype(F32)
            zny = jnp.sum(zf * sgn, axis=0, keepdims=True)
            conv = acc[g] + (zny * kny_ref[0:1, :] * (1.0 / NFFT)) * sgn
            o_ref[g] = (x0_ref[g].astype(F32) * (conv + zf * fb_ref[...])).astype(BF16)


def hyena_conv(z3, x03, cos_bf, sin_bf, kr, ki, kny, fbias):
    b = z3.shape[0]
    seq_blk = pl.BlockSpec((HY_G, SEQ, DG), lambda i, k: (i, 0, 0))
    rows = pl.BlockSpec((HY_FB, SEQ), lambda i, k: (k, 0))
    cols = pl.BlockSpec((SEQ, HY_FB), lambda i, k: (0, k))
    kblk = pl.BlockSpec((HY_FB, DG), lambda i, k: (k, 0))
    return pl.pallas_call(
        _hy_conv_body, grid=(b // HY_G, SEQ // HY_FB),
        in_specs=[seq_blk, seq_blk, rows, rows, cols, cols, kblk, kblk,
                  pl.BlockSpec(kny.shape, lambda i, k: (0, 0)), pl.BlockSpec(fbias.shape, lambda i, k: (0, 0))],
        out_specs=seq_blk, out_shape=jax.ShapeDtypeStruct((b, SEQ, DG), BF16),
        scratch_shapes=[pltpu.VMEM((HY_G, SEQ, DG), F32)],
        compiler_params=_params("parallel", "arbitrary"), name="hyena_conv",
    )(z3, x03, cos_bf, sin_bf, cos_bf, sin_bf, kr, ki, kny, fbias)


N_MCH = SEQ // M_CHUNK
MQ = M_CHUNK


def _head_lane_vec(rows8, base):
    lane_head = lax.broadcasted_iota(I32, (1, DG), 1) // HD
    out = jnp.zeros((1, DG), F32)
    for h in range(N_HEADS):
        out = jnp.where(lane_head == h, rows8[base + h:base + h + 1, :], out)
    return out


def _mamba_body(z_ref, xbc_ref, dtc_ref, cw_ref, cb_ref, dtb_ref, a_ref, dsk_ref, nw_ref, tri_ref, bd_ref,
                o_ref, pad, xs_s, b_s, c_s, y_s, col_s):
    _fill_padded(pad, xbc_ref, 2 * DG)

    def conv_rows(c, _):
        r0 = pl.multiple_of(c * CONV_ROWS, CONV_ROWS)
        for g in range(4):
            lanes = slice(g * V7X_LANES, (g + 1) * V7X_LANES)
            u = _silu(_dwconv_rows(pad, cw_ref, r0, lanes, M_CONV) + cb_ref[:, lanes])
            if g < 2:
                xs_s[pl.ds(r0, CONV_ROWS), lanes] = u
            elif g == 2:
                b_s[pl.ds(r0, CONV_ROWS), :] = u.astype(BF16)
            else:
                c_s[pl.ds(r0, CONV_ROWS), :] = u.astype(BF16)
        return 0

    lax.fori_loop(0, SEQ // CONV_ROWS, conv_rows, 0)

    li = lax.broadcasted_iota(I32, (MQ, MQ), 0)
    si = lax.broadcasted_iota(I32, (MQ, MQ), 1)
    lower = si <= li
    upper = si >= li
    bdmask = bd_ref[...]

    def chunk_common(c):
        dt = _softplus(dtc_ref[0, c] + dtb_ref[...])
        a = dt * a_ref[...]
        cum = _dot01_rhs(a, tri_ref[...])
        tot = cum[:, MQ - 1:MQ]
        suf = tot - cum + a
        row_dir = lax.broadcasted_iota(I32, (8, MQ), 0) // N_HEADS
        seg = jnp.where(row_dir == 0, cum, suf)
        wgt = jnp.exp(tot - seg) * dt
        carry = jnp.exp(seg)
        stack = jnp.concatenate([seg, wgt, carry, dt], axis=0)
        return dt, seg, tot, stack.T

    def pass_fwd(c, s_f):
        r0 = pl.multiple_of(c * MQ, MQ)
        dt, seg, tot, cols = chunk_common(c)
        col_s[c] = cols
        x = xs_s[pl.ds(r0, MQ), :]
        xb = x.astype(BF16)
        bm = b_s[pl.ds(r0, MQ), :]
        cm = c_s[pl.ds(r0, MQ), :]
        bmf = bm.astype(F32)
        cmf = cm.astype(F32)
        ydiag = []
        for h in range(N_HEADS):
            g = h // 2
            cb = lax.dot_general(cm[:, g * M_STATE:(g + 1) * M_STATE], bm[:, g * M_STATE:(g + 1) * M_STATE],
                                 _NT, preferred_element_type=F32)
            lf = jnp.where(lower, jnp.exp(jnp.minimum(cols[:, h:h + 1] - seg[h:h + 1, :], 0.0)), 0.0)
            lb = jnp.where(upper, jnp.exp(jnp.minimum(cols[:, 4 + h:5 + h] - seg[4 + h:5 + h, :], 0.0)), 0.0)
            m = cb * (lf * dt[h:h + 1, :] + lb * dt[4 + h:5 + h, :])
            ydiag.append(_dot(m.astype(BF16), xb[:, h * HD:(h + 1) * HD]))
        y = jnp.concatenate(ydiag, axis=1)
        cw = jnp.concatenate([cmf[:, (h // 2) * M_STATE:(h // 2 + 1) * M_STATE] * cols[:, 16 + h:17 + h]
                              for h in range(N_HEADS)], axis=1)
        y = y + _dot(cw.astype(BF16), s_f.astype(BF16))
        y_s[pl.ds(r0, MQ), :] = y
        bw = jnp.concatenate([bmf[:, (h // 2) * M_STATE:(h // 2 + 1) * M_STATE] * cols[:, 8 + h:9 + h]
                              for h in range(N_HEADS)], axis=1)
        upd = lax.dot_general(bw.astype(BF16), xb, _TN, preferred_element_type=F32)
        decay = _head_lane_vec(jnp.exp(tot), 0)
        return s_f * decay + upd * bdmask

    lax.fori_loop(0, N_MCH, pass_fwd, jnp.zeros((DG, DG), F32))

    def pass_bwd(i, s_b):
        c = N_MCH - 1 - i
        r0 = pl.multiple_of(c * MQ, MQ)
        cols = col_s[c]
        x = xs_s[pl.ds(r0, MQ), :]
        xb = x.astype(BF16)
        bmf = b_s[pl.ds(r0, MQ), :].astype(F32)
        cmf = c_s[pl.ds(r0, MQ), :].astype(F32)
        cw = jnp.concatenate([cmf[:, (h // 2) * M_STATE:(h // 2 + 1) * M_STATE] * cols[:, 20 + h:21 + h]
                              for h in range(N_HEADS)], axis=1)
        y_s[pl.ds(r0, MQ), :] += _dot(cw.astype(BF16), s_b.astype(BF16))
        bw = jnp.concatenate([bmf[:, (h // 2) * M_STATE:(h // 2 + 1) * M_STATE] * cols[:, 12 + h:13 + h]
                              for h in range(N_HEADS)], axis=1)
        upd = lax.dot_general(bw.astype(BF16), xb, _TN, preferred_element_type=F32)
        tot_rows = jnp.concatenate([cols[0:1, 20 + h:21 + h] for h in range(N_HEADS)], axis=0)
        lane_head = lax.broadcasted_iota(I32, (1, DG), 1) // HD
        decay = jnp.zeros((1, DG), F32)
        for h in range(N_HEADS):
            decay = jnp.where(lane_head == h, tot_rows[h:h + 1, :], decay)
        return s_b * decay + upd * bdmask

    lax.fori_loop(0, N_MCH, pass_bwd, jnp.zeros((DG, DG), F32))

    def finish(c, _):
        r0 = pl.multiple_of(c * CONV_ROWS, CONV_ROWS)
        y = y_s[pl.ds(r0, CONV_ROWS), :] + xs_s[pl.ds(r0, CONV_ROWS), :] * dsk_ref[...]
        y = y * _silu(z_ref[0, pl.ds(r0, CONV_ROWS), :].astype(F32))
        o_ref[0, pl.ds(r0, CONV_ROWS), :] = (_rms(y) * nw_ref[...]).astype(BF16)
        return 0

    lax.fori_loop(0, SEQ // CONV_ROWS, finish, 0)


def mamba2(z3, xbc3, dtc4, conv_w, conv_b, dt_bias_col, a_col, dskip_lanes, norm_w, tri_incl, bdmask):
    b = z3.shape[0]
    full = lambda a: pl.BlockSpec(a.shape, lambda i: (0,) * a.ndim)
    return pl.pallas_call(
        _mamba_body, grid=(b,),
        in_specs=[pl.BlockSpec((1, SEQ, DG), lambda i: (i, 0, 0)),
                  pl.BlockSpec((1, SEQ, 2 * DG), lambda i: (i, 0, 0)),
                  pl.BlockSpec((1, N_MCH, 8, MQ), lambda i: (i, 0, 0, 0)),
                  full(conv_w), full(conv_b), full(dt_bias_col), full(a_col), full(dskip_lanes), full(norm_w),
                  full(tri_incl), full(bdmask)],
        out_specs=pl.BlockSpec((1, SEQ, DG), lambda i: (i, 0, 0)),
        out_shape=jax.ShapeDtypeStruct((b, SEQ, DG), BF16),
        scratch_shapes=[pltpu.VMEM((SEQ + 2 * CONV_HALO, 2 * DG), F32),
                        pltpu.VMEM((SEQ, DG), F32),
                        pltpu.VMEM((SEQ, 2 * M_STATE), BF16),
                        pltpu.VMEM((SEQ, 2 * M_STATE), BF16),
                        pltpu.VMEM((SEQ, DG), F32),
                        pltpu.VMEM((N_MCH, MQ, 32), F32)],
        compiler_params=_params("parallel"), name="mamba2",
    )(z3, xbc3, dtc4, conv_w, conv_b, dt_bias_col, a_col, dskip_lanes, norm_w, tri_incl, bdmask)


A_TQ = 128
A_ROWS = 256


def _attn_bias_body(ids_ref, rb_ref, o_ref):
    ids = ids_ref[0]
    for h in range(N_HEADS):
        acc = jnp.full(ids.shape, NEG_BIG, F32)
        for bkt in range(N_BUCKETS):
            acc = jnp.where(ids == bkt, rb_ref[bkt, h], acc)
        o_ref[h, 0] = acc


def attention_bias_table(bucket_ids, rel_bias):
    nvar, tq, w = bucket_ids.shape
    return pl.pallas_call(
        _attn_bias_body, grid=(nvar,),
        in_specs=[pl.BlockSpec((1, tq, w), lambda v: (v, 0, 0)),
                  pl.BlockSpec(memory_space=pltpu.SMEM)],
        out_specs=pl.BlockSpec((N_HEADS, 1, tq, w), lambda v: (0, v, 0, 0)),
        out_shape=jax.ShapeDtypeStruct((N_HEADS, nvar, tq, w), F32),
        compiler_params=_params("parallel"), name="attention_bias_table",
    )(bucket_ids, rel_bias)


A_SLABS = 3 * DG // V7X_LANES
A_QBLOCKS = SEQ // A_TQ


def _attn_body(at_ref, b1_ref, b4_ref, b16_ref, o_ref, qkv_s, acc_o, acc_l):
    def fill(c, _):
        r0 = pl.multiple_of(c * A_ROWS, A_ROWS)
        for s in range(A_SLABS):
            qkv_s[s, pl.ds(r0, A_ROWS), :] = at_ref[0, pl.ds(r0, A_ROWS), s * V7X_LANES:(s + 1) * V7X_LANES].astype(F32)
        return 0

    lax.fori_loop(0, SEQ // A_ROWS, fill, 0)
    first_head = lax.broadcasted_iota(I32, (A_TQ, V7X_LANES), 1) < HD

    def run_pattern(dil, bias_ref):
        n = SEQ // dil
        nblk = n // A_TQ
        nvar = bias_ref.shape[1]
        w = bias_ref.shape[3]

        def rows(start, size):
            return pl.ds(start, size) if dil == 1 else pl.ds(start, size, stride=dil)

        def block(it, _):
            r = it // nblk
            i = it - r * nblk
            q0 = i * A_TQ
            if nvar == 1:
                k0 = 0
                var = 0
            else:
                k0 = jnp.clip(q0 - A_BAND, 0, n - w)
                var = jnp.where(i == 0, 0, jnp.where(i == nblk - 1, 2, 1))
            qrows = rows(r + dil * q0, A_TQ)
            krows = rows(r + dil * k0, w)
            for hp in range(2):
                q2 = qkv_s[hp, qrows, :]
                k2 = qkv_s[2 + hp, krows, :].astype(BF16)
                v2 = qkv_s[4 + hp, krows, :].astype(BF16)
                outs, lses = [], []
                for hh in range(2):
                    keep = first_head if hh == 0 else jnp.logical_not(first_head)
                    qm = jnp.where(keep, q2, 0.0).astype(BF16)
                    s = lax.dot_general(qm, k2, _NT, preferred_element_type=F32) * (HD ** -0.5)
                    s = s + bias_ref[2 * hp + hh, var]
                    m = jnp.max(s, axis=1, keepdims=True)
                    p = jnp.exp(s - m)
                    den = jnp.sum(p, axis=1, keepdims=True)
                    outs.append(_dot(p.astype(BF16), v2) / den)
                    lses.append(m + jnp.log(den))
                o_new = jnp.where(first_head, outs[0], outs[1])
                l_new = jnp.where(first_head, lses[0], lses[1])
                if dil == A_DILS[0]:
                    acc_o[hp, qrows, :] = o_new
                    acc_l[hp, qrows, :] = l_new
                else:
                    o_old = acc_o[hp, qrows, :]
                    l_old = acc_l[hp, qrows, :]
                    mx = jnp.maximum(l_old, l_new)
                    w_old = jnp.exp(l_old - mx)
                    w_new = jnp.exp(l_new - mx)
                    tot = w_old + w_new
                    acc_o[hp, qrows, :] = (w_old * o_old + w_new * o_new) / tot
                    acc_l[hp, qrows, :] = mx + jnp.log(tot)
            return 0

        lax.fori_loop(0, A_QBLOCKS, block, 0)

    run_pattern(A_DILS[0], b1_ref)
    run_pattern(A_DILS[1], b4_ref)
    run_pattern(A_DILS[2], b16_ref)

    def finish(c, _):
        r0 = pl.multiple_of(c * A_ROWS, A_ROWS)
        for hp in range(2):
            o_ref[0, pl.ds(r0, A_ROWS), hp * V7X_LANES:(hp + 1) * V7X_LANES] = acc_o[hp, pl.ds(r0, A_ROWS), :].astype(BF16)
        return 0

    lax.fori_loop(0, SEQ // A_ROWS, finish, 0)


def dilated_attention(at3, bias1, bias4, bias16):
    b = at3.shape[0]
    full = lambda a: pl.BlockSpec(a.shape, lambda i: (0,) * a.ndim)
    return pl.pallas_call(
        _attn_body, grid=(b,),
        in_specs=[pl.BlockSpec((1, SEQ, 3 * DG), lambda i: (i, 0, 0)), full(bias1), full(bias4), full(bias16)],
        out_specs=pl.BlockSpec((1, SEQ, DG), lambda i: (i, 0, 0)),
        out_shape=jax.ShapeDtypeStruct((b, SEQ, DG), BF16),
        scratch_shapes=[pltpu.VMEM((A_SLABS, SEQ, V7X_LANES), F32),
                        pltpu.VMEM((2, SEQ, V7X_LANES), F32),
                        pltpu.VMEM((2, SEQ, V7X_LANES), F32)],
        compiler_params=_params("parallel"), name="dilated_attention",
    )(at3, bias1, bias4, bias16)


H_BLK = 256
H_CPB = H_BLK // H_CHUNK
N_HBLK = SEQ // H_BLK
N_HCH = SEQ // H_CHUNK


def _chunk_bcast(x, row):
    c = x.shape[1]
    x3 = x.reshape(H_CPB, H_CHUNK, c)
    return jnp.broadcast_to(x3[:, row:row + 1, :], (H_CPB, H_CHUNK, c)).reshape(H_BLK, c)


def _hgrn_body(p_ref, lb_ref, nw_ref, tin_ref, bd_ref, o_ref, qd_s, ut_s, oi_s, dec_s):
    li = lax.broadcasted_iota(I32, (H_BLK, H_BLK), 0)
    si = lax.broadcasted_iota(I32, (H_BLK, H_BLK), 1)
    same = (li // H_CHUNK) == (si // H_CHUNK)
    mask_f = same & (si <= li)
    mask_b = same & (si >= li)
    bdmask = bd_ref[...]

    def block(bi, _):
        r0 = pl.multiple_of(bi * H_BLK, H_BLK)
        rows = pl.ds(r0, H_BLK)
        q = _silu(p_ref[0, rows, 0:DG].astype(F32))
        v = p_ref[0, rows, 3 * DG:4 * DG]
        oi = jnp.zeros((H_BLK, DG), F32)
        scores = [None] * N_HEADS
        for d in range(2):
            fpre = p_ref[0, rows, (1 + d) * DG:(2 + d) * DG].astype(F32)
            lb = lb_ref[d:d + 1, :]
            sg = jax.nn.sigmoid(fpre)
            g = jnp.log(lb + (1.0 - lb) * sg)
            k = (1.0 - lb) * (1.0 - sg)
            gi = _dot01(tin_ref[...], g)
            glast = _chunk_bcast(gi, H_CHUNK - 1)
            if d == 0:
                gc = gi
                gref = _chunk_bcast(gi, H_CHUNK // 2 - 1)
                msk = mask_f
            else:
                gc = glast - gi + g
                gref = _chunk_bcast(gc, H_CHUNK // 2)
                msk = mask_b
            qe = (q * jnp.exp(gc - gref)).astype(BF16)
            ke = (k * jnp.exp(gref - gc)).astype(BF16)
            for h in range(N_HEADS):
                hs = slice(h * HD, (h + 1) * HD)
                sc = jnp.where(msk, lax.dot_general(qe[:, hs], ke[:, hs], _NT, preferred_element_type=F32), 0.0)
                scores[h] = sc if d == 0 else scores[h] + sc
            qd_s[d, rows, :] = (q * jnp.exp(gc)).astype(BF16)
            kd = (k * jnp.exp(glast - gc)).astype(BF16)
            for j in range(H_CPB):
                cr = slice(j * H_CHUNK, (j + 1) * H_CHUNK)
                ut = lax.dot_general(v[cr, :], kd[cr, :], _TN, preferred_element_type=F32)
                ut_s[d, bi * H_CPB + j] = (ut * bdmask).astype(BF16)
                dec_s[d, bi * H_CPB + j] = jnp.broadcast_to(jnp.exp(glast[j * H_CHUNK:j * H_CHUNK + 1, :]),
                                                             (V7X_SUBLANES, DG))
        oi = jnp.concatenate([_dot(scores[h].astype(BF16), v[:, h * HD:(h + 1) * HD]) for h in range(N_HEADS)],
                             axis=1)
        oi_s[rows, :] = oi
        return 0

    lax.fori_loop(0, N_HBLK, block, 0)

    def scan_dir(d):
        def step(i, st):
            c = i if d == 0 else N_HCH - 1 - i
            r0 = pl.multiple_of(c * H_CHUNK, H_CHUNK)
            qd = qd_s[d, pl.ds(r0, H_CHUNK), :]
            inter = lax.dot_general(qd, st.astype(BF16), _NT, preferred_element_type=F32)
            oi_s[pl.ds(r0, H_CHUNK), :] += inter
            return st * dec_s[d, c][0:1, :] + ut_s[d, c].astype(F32)

        lax.fori_loop(0, N_HCH, step, jnp.zeros((DG, DG), F32))

    scan_dir(0)
    scan_dir(1)

    def finish(c, _):
        r0 = pl.multiple_of(c * CONV_ROWS, CONV_ROWS)
        rows = pl.ds(r0, CONV_ROWS)
        o = oi_s[rows, :]
        gate = _silu(p_ref[0, rows, 4 * DG:5 * DG].astype(F32))
        outs = [_rms(o[:, h * HD:(h + 1) * HD]) for h in range(N_HEADS)]
        o_ref[0, rows, :] = (jnp.concatenate(outs, axis=1) * nw_ref[...] * gate).astype(BF16)
        return 0

    lax.fori_loop(0, SEQ // CONV_ROWS, finish, 0)


def hgrn2(p3, lb2, norm_w_lanes, tri_in_chunk, bdmask):
    b = p3.shape[0]
    full = lambda a: pl.BlockSpec(a.shape, lambda i: (0,) * a.ndim)
    return pl.pallas_call(
        _hgrn_body, grid=(b,),
        in_specs=[pl.BlockSpec((1, SEQ, 5 * DG), lambda i: (i, 0, 0)), full(lb2), full(norm_w_lanes),
                  full(tri_in_chunk), full(bdmask)],
        out_specs=pl.BlockSpec((1, SEQ, DG), lambda i: (i, 0, 0)),
        out_shape=jax.ShapeDtypeStruct((b, SEQ, DG), BF16),
        scratch_shapes=[pltpu.VMEM((2, SEQ, DG), BF16),
                        pltpu.VMEM((2, N_HCH, DG, DG), BF16),
                        pltpu.VMEM((SEQ, DG), F32),
                        pltpu.VMEM((2, N_HCH, V7X_SUBLANES, DG), F32)],
        compiler_params=_params("parallel"), name="hgrn2",
    )(p3, lb2, norm_w_lanes, tri_in_chunk, bdmask)


@functools.lru_cache(maxsize=None)
def _tables():
    t = {}
    k = np.arange(SEQ, dtype=np.int64)
    ang = 2.0 * np.pi * ((k[:, None] * k[None, :]) % NFFT).astype(np.float64) / NFFT
    t["cos"] = np.cos(ang).astype(np.float32)
    t["sin"] = np.sin(ang).astype(np.float32)
    tt = np.linspace(0.0, 1.0, SEQ, dtype=np.float32)[:, None]
    bands = (HY_POS_DIM - 1) // 2
    ang_pos = (2.0 * math.pi * np.arange(SEQ, dtype=np.float32) / SEQ).astype(np.float32)
    f = np.linspace(1e-4, bands - 1, bands, dtype=np.float32)
    a2 = (ang_pos[:, None] * f[None, :]).astype(np.float32)
    z = np.concatenate([tt, np.cos(a2), -np.sin(a2)], axis=-1).astype(np.float32)
    zp = np.zeros((SEQ, V7X_LANES), np.float32)
    zp[:, :HY_POS_DIM] = z
    t["zpos"] = zp
    max_decay = math.log(1e-2) / 0.3
    min_decay = math.log(1e-2) / 1.5
    deltas = np.abs(np.linspace(min_decay, max_decay, DG, dtype=np.float32))
    t["decay"] = np.exp(-tt * deltas[None, :]).astype(np.float32)
    i128 = np.arange(V7X_LANES)
    t["u128"] = (i128[:, None] < i128[None, :]).astype(np.float32)
    im = np.arange(M_CHUNK)
    t["tri_incl"] = (im[:, None] <= im[None, :]).astype(np.float32)
    ib = np.arange(H_BLK)
    t["tri_in_chunk"] = ((ib[:, None] // H_CHUNK == ib[None, :] // H_CHUNK)
                         & (ib[None, :] <= ib[:, None])).astype(np.float32)
    idg = np.arange(DG)
    t["bdmask"] = (idg[:, None] // HD == idg[None, :] // HD).astype(np.float32)
    def bucket(rel):
        nb = N_BUCKETS // 2
        max_exact = nb // 2
        ret = (rel > 0).astype(np.int64) * nb
        n = np.abs(rel)
        nf = np.maximum(n, 1).astype(np.float64)
        large = max_exact + (np.log(nf / max_exact) / math.log(MAX_DISTANCE / max_exact)
                             * (nb - max_exact)).astype(np.int64)
        large = np.minimum(large, nb - 1)
        return ret + np.where(n < max_exact, n, large)

    for dil in A_DILS:
        n = SEQ // dil
        w = min(n, A_TQ + 2 * A_BAND)
        starts = [0] if n == w else [0, -A_BAND, -(w - A_TQ)]
        qi = np.arange(A_TQ)[:, None]
        kj = np.arange(w)[None, :]
        ids = []
        for s0 in starts:
            rel = kj + s0 - qi
            ids.append(np.where(np.abs(rel) <= A_BAND, bucket(rel * dil), -1))
        t[f"bucket{dil}"] = np.stack(ids).astype(np.int32)
    return t


def kernel(x, w_in, w_out, norm_mix_w, norm_ffn_w, hy_conv_w, hy_pos_w1, hy_pos_b1, hy_pos_w2, hy_pos_b2,
           hy_sin_freq, hy_pos_w3, hy_filt_bias, m_conv_w, m_conv_b, m_dt_bias, m_A_log, m_D, m_norm_w, rel_bias,
           hg_lb, hg_norm_w, router_w, moe_w_gate, moe_w_up, moe_w_down, final_norm_w):
    b = x.shape[0]
    t = b * SEQ
    tb = _tables()
    cos_f32 = jnp.asarray(tb["cos"])
    sin_f32 = jnp.asarray(tb["sin"])
    cos_bf = cos_f32.astype(BF16)
    sin_bf = sin_f32.astype(BF16)
    u128 = jnp.asarray(tb["u128"]).astype(BF16)
    tri_incl = jnp.asarray(tb["tri_incl"]).astype(BF16)
    tri_in_chunk = jnp.asarray(tb["tri_in_chunk"]).astype(BF16)
    bdmask = jnp.asarray(tb["bdmask"])
    attn_bias = [attention_bias_table(jnp.asarray(tb[f"bucket{d}"]), rel_bias.astype(F32)) for d in A_DILS]

    sm = jax.nn.softmax(hg_lb.astype(F32), axis=0)
    lower_bounds = jnp.cumsum(sm, axis=0) - sm[:1]

    xa = x.reshape(t, D_MODEL)
    xb = None
    for l in range(DEPTH):
        wl = w_in[l]
        w_main = jnp.concatenate([wl[:, 0:768], wl[:, 768:1024], wl[:, 1024:1536], wl[:, 1544:2312],
                                  wl[:, 2312:3592]], axis=1).astype(BF16)
        w_dt_rows = wl[:, 1536:1544].T.astype(BF16)
        hy, mz, mx, at, hg, dtc = in_projection(xa, xb, norm_mix_w[l][None, :], w_main, w_dt_rows)

        w1p = jnp.zeros((V7X_LANES, HY_HID), F32).at[:HY_POS_DIM].set(hy_pos_w1[l])
        kr, ki, kny = hyena_filter_spectrum(
            jnp.asarray(tb["zpos"]), w1p, hy_pos_b1[l][None, :], hy_pos_w2[l], hy_pos_b2[l][None, :],
            hy_sin_freq[l][None, :], hy_pos_w3[l], jnp.asarray(tb["decay"]), cos_f32, sin_f32)
        z3, x03 = hyena_prep(hy.reshape(b, SEQ, 3 * DG), hy_conv_w[l])
        ya = hyena_conv(z3, x03, cos_bf, sin_bf, kr, ki, kny, hy_filt_bias[l][None, :]).reshape(t, DG)

        a_col = (-jnp.exp(m_A_log[l].astype(F32))).reshape(8, 1)
        yb = mamba2(mz.reshape(b, SEQ, DG), mx.reshape(b, SEQ, 2 * DG), dtc.reshape(b, N_MCH, 8, MQ),
                    m_conv_w[l], m_conv_b[l][None, :], m_dt_bias[l].reshape(8, 1), a_col,
                    jnp.repeat(m_D[l].astype(F32), HD)[None, :], m_norm_w[l][None, :], tri_incl, bdmask).reshape(t, DG)

        yc = dilated_attention(at.reshape(b, SEQ, 3 * DG), *attn_bias).reshape(t, DG)

        lbl = lower_bounds[l]
        yd = hgrn2(hg.reshape(b, SEQ, 5 * DG), lbl, jnp.tile(hg_norm_w[l], N_HEADS)[None, :],
                   tri_in_chunk, bdmask).reshape(t, DG)

        xo, xn = out_projection(xa, xb, ya, yb, yc, yd, w_out[l].reshape(4, DG, D_MODEL).astype(BF16),
                                norm_ffn_w[l][None, :])
        xn3 = xn.reshape(b, SEQ, D_MODEL)
        rw_rows = router_w[l].T.astype(F32)
        rw_hi = rw_rows.astype(BF16)
        rw_lo = (rw_rows - rw_hi.astype(F32)).astype(BF16)
        rank, gate, seg = router(xo.reshape(b, SEQ, D_MODEL), norm_ffn_w[l][None, :], rw_hi, rw_lo, u128)
        moe = moe_ffn(seg[:, :, :MOE_SEG_STRIDE].reshape(-1), xn3, rank, gate,
                      moe_w_gate[l].astype(BF16), moe_w_up[l].astype(BF16), moe_w_down[l].astype(BF16))
        xa, xb = xo, moe.reshape(t, D_MODEL)
    return final_norm(xa, xb, final_norm_w[None, :]).reshape(b, SEQ, D_MODEL)
```

```python
import functools
import math

import numpy as np
import jax
import jax.numpy as jnp
from jax import lax
from jax.experimental import pallas as pl
from jax.experimental.pallas import tpu as pltpu

F32 = jnp.float32
BF16 = jnp.bfloat16
I32 = jnp.int32

D_MODEL = 1024
SEQ = 2048
DEPTH = 2
DG = 256
N_HEADS = 4
HD = 64
HY_POS_DIM = 33
HY_HID = 64
M_CONV = 5
M_STATE = 64
M_CHUNK = 128
H_CHUNK = 32
A_BAND = 64
A_DILS = (1, 4, 16)
N_BUCKETS = 32
MAX_DISTANCE = 1024
N_EXPERTS = 16
CAP = 2 * SEQ // N_EXPERTS
D_FF = 1024
EPS = 1e-6
NFFT = 2 * SEQ

V7X_LANES = 128
V7X_SUBLANES = 8
V7X_VMEM_LIMIT_BYTES = 56 * 1024 * 1024

NEG_BIG = -1e30

_NT = (((1,), (1,)), ((), ()))
_TN = (((0,), (0,)), ((), ()))


def _params(*sem):
    return pltpu.CompilerParams(dimension_semantics=sem, vmem_limit_bytes=V7X_VMEM_LIMIT_BYTES)


def _dot(a, b):
    return jnp.dot(a, b, preferred_element_type=F32)


def _dot_hi(a, b):
    return jnp.dot(a, b, preferred_element_type=F32, precision=lax.Precision.HIGHEST)


def _dot01(t_bf16, x):
    x1 = x.astype(BF16)
    r1 = x - x1.astype(F32)
    x2 = r1.astype(BF16)
    x3 = (r1 - x2.astype(F32)).astype(BF16)
    return _dot(t_bf16, x1) + _dot(t_bf16, x2) + _dot(t_bf16, x3)


def _dot01_rhs(x, t_bf16):
    x1 = x.astype(BF16)
    r1 = x - x1.astype(F32)
    x2 = r1.astype(BF16)
    x3 = (r1 - x2.astype(F32)).astype(BF16)
    return _dot(x1, t_bf16) + _dot(x2, t_bf16) + _dot(x3, t_bf16)


def _silu(x):
    return x * jax.nn.sigmoid(x)


def _softplus(x):
    return jnp.maximum(x, 0.0) + jnp.log(1.0 + jnp.exp(-jnp.abs(x)))


def _rms(x):
    return x * lax.rsqrt(jnp.mean(x * x, axis=-1, keepdims=True) + EPS)


TM_PROJ = 512
_HY0, _MZ0, _MX0, _AT0, _HG0, _PEND = 0, 768, 1024, 1536, 2304, 3584


def _inproj_body(has_b, *refs):
    if has_b:
        xa_ref, xb_ref, nw_ref, w_ref, wdt_ref, hy_ref, mz_ref, mx_ref, at_ref, hg_ref, dtc_ref = refs
        x = xa_ref[...] + xb_ref[...]
    else:
        xa_ref, nw_ref, w_ref, wdt_ref, hy_ref, mz_ref, mx_ref, at_ref, hg_ref, dtc_ref = refs
        x = xa_ref[...]
    hn = (_rms(x) * nw_ref[...]).astype(BF16)
    hy_ref[...] = _dot(hn, w_ref[:, _HY0:_MZ0]).astype(BF16)
    mz_ref[...] = _dot(hn, w_ref[:, _MZ0:_MX0]).astype(BF16)
    mx_ref[...] = _dot(hn, w_ref[:, _MX0:_AT0]).astype(BF16)
    at_ref[...] = _dot(hn, w_ref[:, _AT0:_HG0]).astype(BF16)
    hg_ref[...] = _dot(hn, w_ref[:, _HG0:_PEND]).astype(BF16)
    dt_rows = lax.dot_general(wdt_ref[...], hn, _NT, preferred_element_type=F32)
    for j in range(TM_PROJ // M_CHUNK):
        dtc_ref[j] = dt_rows[:, j * M_CHUNK:(j + 1) * M_CHUNK]


def in_projection(xa, xb, norm_w, w_main, w_dt_rows):
    t = xa.shape[0]
    tm = TM_PROJ
    has_b = xb is not None
    row = lambda w: pl.BlockSpec((tm, w), lambda i: (i, 0))
    full = lambda a: pl.BlockSpec(a.shape, lambda i: (0,) * a.ndim)
    ins = [xa] + ([xb] if has_b else []) + [norm_w, w_main, w_dt_rows]
    in_specs = [row(D_MODEL)] * (2 if has_b else 1) + [full(norm_w), full(w_main), full(w_dt_rows)]
    widths = (768, 256, 512, 768, 1280)
    out_shape = [jax.ShapeDtypeStruct((t, w), BF16) for w in widths]
    out_shape.append(jax.ShapeDtypeStruct((t // M_CHUNK, 8, M_CHUNK), F32))
    out_specs = [row(w) for w in widths] + [pl.BlockSpec((tm // M_CHUNK, 8, M_CHUNK), lambda i: (i, 0, 0))]
    return pl.pallas_call(
        functools.partial(_inproj_body, has_b),
        grid=(t // tm,), in_specs=in_specs, out_specs=out_specs, out_shape=out_shape,
        compiler_params=_params("parallel"), name="in_projection",
    )(*ins)


TM_OUT = 512


def _outproj_body(has_b, *refs):
    if has_b:
        xa_ref, xb_ref, ya_ref, yb_ref, yc_ref, yd_ref, w_ref, nw_ref, xo_ref, xn_ref = refs
        x = xa_ref[...] + xb_ref[...]
    else:
        xa_ref, ya_ref, yb_ref, yc_ref, yd_ref, w_ref, nw_ref, xo_ref, xn_ref = refs
        x = xa_ref[...]
    acc = x + _dot(ya_ref[...], w_ref[0]) + _dot(yb_ref[...], w_ref[1])
    acc = acc + _dot(yc_ref[...], w_ref[2]) + _dot(yd_ref[...], w_ref[3])
    xo_ref[...] = acc
    xn_ref[...] = (_rms(acc) * nw_ref[...]).astype(BF16)


def out_projection(xa, xb, ya, yb, yc, yd, w_out4, norm_w):
    t = xa.shape[0]
    tm = TM_OUT
    has_b = xb is not None
    row = lambda w: pl.BlockSpec((tm, w), lambda i: (i, 0))
    full = lambda a: pl.BlockSpec(a.shape, lambda i: (0,) * a.ndim)
    ins = [xa] + ([xb] if has_b else []) + [ya, yb, yc, yd, w_out4, norm_w]
    in_specs = [row(D_MODEL)] * (2 if has_b else 1) + [row(DG)] * 4 + [full(w_out4), full(norm_w)]
    return pl.pallas_call(
        functools.partial(_outproj_body, has_b),
        grid=(t // tm,), in_specs=in_specs,
        out_specs=[row(D_MODEL), row(D_MODEL)],
        out_shape=[jax.ShapeDtypeStruct((t, D_MODEL), F32), jax.ShapeDtypeStruct((t, D_MODEL), BF16)],
        compiler_params=_params("parallel"), name="out_projection",
    )(*ins)


def _final_norm_body(xa_ref, xb_ref, nw_ref, o_ref):
    o_ref[...] = _rms(xa_ref[...] + xb_ref[...]) * nw_ref[...]


def final_norm(xa, xb, norm_w):
    t = xa.shape[0]
    tm = 1024
    row = pl.BlockSpec((tm, D_MODEL), lambda i: (i, 0))
    return pl.pallas_call(
        _final_norm_body, grid=(t // tm,),
        in_specs=[row, row, pl.BlockSpec(norm_w.shape, lambda i: (0, 0))],
        out_specs=row, out_shape=jax.ShapeDtypeStruct((t, D_MODEL), F32),
        compiler_params=_params("parallel"), name="final_norm",
    )(xa, xb, norm_w)


def _prefix_excl_lanes(mask_f32, u_ref):
    e = mask_f32.shape[0]
    off = jnp.zeros((e, 1), F32)
    parts, bounds = [], [off]
    for k in range(SEQ // V7X_LANES):
        tile = mask_f32[:, k * V7X_LANES:(k + 1) * V7X_LANES]
        parts.append(_dot(tile.astype(BF16), u_ref[...]) + off)
        off = off + jnp.sum(tile, axis=1, keepdims=True)
        bounds.append(off)
    return jnp.concatenate(parts, axis=1), bounds


def _router_body(xo_ref, nw_ref, rwh_ref, rwl_ref, u_ref, rank_ref, gate_ref, seg_ref):
    xn = _rms(xo_ref[0]) * nw_ref[...]
    xh = xn.astype(BF16)
    xl = (xn - xh.astype(F32)).astype(BF16)
    nt = lambda w, a: lax.dot_general(w, a, _NT, preferred_element_type=F32)
    logits = nt(rwh_ref[...], xh) + nt(rwh_ref[...], xl) + nt(rwl_ref[...], xh)
    mx = jnp.max(logits, axis=0, keepdims=True)
    ex = jnp.exp(logits - mx)
    aff = ex / jnp.sum(ex, axis=0, keepdims=True)
    bits = pltpu.bitcast(aff, I32)

    def search(i, thr):
        cand = thr | jnp.left_shift(jnp.int32(1), 30 - i)
        cnt = jnp.sum((bits >= cand).astype(I32), axis=1, keepdims=True)
        return jnp.where(cnt >= CAP, cand, thr)

    thr = lax.fori_loop(0, 31, search, jnp.zeros((N_EXPERTS, 1), I32))
    gt = (bits > thr).astype(F32)
    eq = (bits == thr).astype(F32)
    need = CAP - jnp.sum(gt, axis=1, keepdims=True)
    tie_rank, _ = _prefix_excl_lanes(eq, u_ref)
    sel = gt + eq * (tie_rank < need).astype(F32)
    rank, bounds = _prefix_excl_lanes(sel, u_ref)
    rank_ref[0] = jnp.where(sel > 0.0, rank, -1.0)
    gate_ref[0] = aff
    lane = lax.broadcasted_iota(I32, (N_EXPERTS, V7X_LANES), 1)
    seg = jnp.zeros((N_EXPERTS, V7X_LANES), F32)
    for sgm in range(N_MOE_SEG + 1):
        seg = jnp.where(lane == sgm, bounds[sgm * (MOE_SEG // V7X_LANES)], seg)
    seg_ref[0] = seg.astype(I32)


def router(xo3, norm_w, rw_hi, rw_lo, u128):
    b = xo3.shape[0]
    out = jax.ShapeDtypeStruct((b, N_EXPERTS, SEQ), F32)
    full = lambda a: pl.BlockSpec(a.shape, lambda i: (0,) * a.ndim)
    return pl.pallas_call(
        _router_body, grid=(b,),
        in_specs=[pl.BlockSpec((1, SEQ, D_MODEL), lambda i: (i, 0, 0)), full(norm_w), full(rw_hi), full(rw_lo),
                  full(u128)],
        out_specs=[pl.BlockSpec((1, N_EXPERTS, SEQ), lambda i: (i, 0, 0))] * 2
                  + [pl.BlockSpec((1, N_EXPERTS, V7X_LANES), lambda i: (i, 0, 0))],
        out_shape=[out, out, jax.ShapeDtypeStruct((b, N_EXPERTS, V7X_LANES), I32)],
        compiler_params=_params("parallel"), name="router",
    )(xo3, norm_w, rw_hi, rw_lo, u128)


MOE_SEG = 256
N_MOE_SEG = SEQ // MOE_SEG
MOE_TILE = 64
MOE_ALIGN = 16
MOE_GROUP = 4
MOE_SEG_STRIDE = 16
CAP_PAD = CAP + MOE_TILE


def _moe_body(cs_ref, xn_ref, rank_ref, gate_ref, wg_ref, wu_ref, wd_ref, o_ref, xy_s):
    b = pl.program_id(0)
    e = pl.program_id(1)

    def seg_plan(s):
        starts, rounds = [], jnp.int32(0)
        for ex in range(N_EXPERTS):
            base = (b * N_EXPERTS + ex) * MOE_SEG_STRIDE
            first = (cs_ref[base + s] // MOE_ALIGN) * MOE_ALIGN
            span = cs_ref[base + s + 1] - first
            starts.append(first)
            rounds = jnp.maximum(rounds, (span + MOE_TILE - 1) // MOE_TILE)
        return starts, rounds

    def tile_base(start, r):
        return pl.multiple_of(jnp.minimum(start + r * MOE_TILE, CAP), MOE_ALIGN)

    def onehot_group(s, bases, grp, weights):
        lanes = pl.ds(pl.multiple_of(s * MOE_SEG, MOE_SEG), MOE_SEG)
        j = lax.broadcasted_iota(I32, (MOE_TILE, MOE_SEG), 0)
        rows = []
        for ex in grp:
            slot = (bases[ex] + j).astype(F32)
            hit = rank_ref[0, ex:ex + 1, lanes] == slot
            val = gate_ref[0, ex:ex + 1, lanes] if weights else 1.0
            rows.append(jnp.where(hit, val, 0.0).astype(BF16))
        return jnp.concatenate(rows, axis=0)

    groups = [list(range(g * MOE_GROUP, (g + 1) * MOE_GROUP)) for g in range(N_EXPERTS // MOE_GROUP)]

    @pl.when(e == 0)
    def _():
        def zero(ex, _):
            xy_s[ex] = jnp.zeros((CAP_PAD, D_MODEL), BF16)
            return 0

        lax.fori_loop(0, N_EXPERTS, zero, 0)

        def seg_gather(s, _):
            starts, rounds = seg_plan(s)
            xn_seg = xn_ref[0, pl.ds(pl.multiple_of(s * MOE_SEG, MOE_SEG), MOE_SEG), :]

            def one_round(r, _):
                bases = [tile_base(st, r) for st in starts]
                for grp in groups:
                    got = _dot(onehot_group(s, bases, grp, False), xn_seg)
                    for k, ex in enumerate(grp):
                        rows = pl.ds(bases[ex], MOE_TILE)
                        old = xy_s[ex, rows, :].astype(F32)
                        xy_s[ex, rows, :] = (old + got[k * MOE_TILE:(k + 1) * MOE_TILE]).astype(BF16)
                return 0

            lax.fori_loop(0, rounds, one_round, 0)
            return 0

        lax.fori_loop(0, N_MOE_SEG, seg_gather, 0)

    xe = xy_s[e, 0:CAP, :]
    hid = (_silu(_dot(xe, wg_ref[0])) * _dot(xe, wu_ref[0])).astype(BF16)
    xy_s[e, 0:CAP, :] = _dot(hid, wd_ref[0]).astype(BF16)

    @pl.when(e == N_EXPERTS - 1)
    def _():
        def seg_scatter(s, _):
            starts, rounds = seg_plan(s)
            tok = pl.ds(pl.multiple_of(s * MOE_SEG, MOE_SEG), MOE_SEG)
            o_ref[0, tok, :] = jnp.zeros((MOE_SEG, D_MODEL), F32)

            def one_round(r, _):
                bases = [tile_base(st, r) for st in starts]
                for grp in groups:
                    ye = jnp.concatenate([xy_s[ex, pl.ds(bases[ex], MOE_TILE), :] for ex in grp], axis=0)
                    o_ref[0, tok, :] += lax.dot_general(onehot_group(s, bases, grp, True), ye, _TN,
                                                        preferred_element_type=F32)
                return 0

            lax.fori_loop(0, rounds, one_round, 0)
            return 0

        lax.fori_loop(0, N_MOE_SEG, seg_scatter, 0)


def moe_ffn(seg_counts_flat, xn3, rank3, gate3, w_gate, w_up, w_down):
    b = xn3.shape[0]
    sel_spec = pl.BlockSpec((1, N_EXPERTS, SEQ), lambda i, e, cs: (i, 0, 0))
    w_spec = lambda a: pl.BlockSpec((1,) + a.shape[1:], lambda i, e, cs: (e, 0, 0))
    grid_spec = pltpu.PrefetchScalarGridSpec(
        num_scalar_prefetch=1, grid=(b, N_EXPERTS),
        in_specs=[pl.BlockSpec((1, SEQ, D_MODEL), lambda i, e, cs: (i, 0, 0)), sel_spec, sel_spec,
                  w_spec(w_gate), w_spec(w_up), w_spec(w_down)],
        out_specs=pl.BlockSpec((1, SEQ, D_MODEL), lambda i, e, cs: (i, 0, 0)),
        scratch_shapes=[pltpu.VMEM((N_EXPERTS, CAP_PAD, D_MODEL), BF16)])
    return pl.pallas_call(
        _moe_body, grid_spec=grid_spec,
        out_shape=jax.ShapeDtypeStruct((b, SEQ, D_MODEL), F32),
        compiler_params=_params("parallel", "arbitrary"), name="moe_ffn",
    )(seg_counts_flat, xn3, rank3, gate3, w_gate, w_up, w_down)


HY_KB = 256
HY_ROWS = 256


def _hy_filter_body(z_ref, w1_ref, b1_ref, w2_ref, b2_ref, fr_ref, w3_ref, dec_ref, c_ref, s_ref,
                    kr_ref, ki_ref, kny_ref, a_s, d_s):
    @pl.when(pl.program_id(0) == 0)
    def _():
        def rows(c, kny):
            r0 = pl.multiple_of(c * HY_ROWS, HY_ROWS)
            fr = fr_ref[...]
            h = jnp.sin(fr * (_dot_hi(z_ref[pl.ds(r0, HY_ROWS), :], w1_ref[...]) + b1_ref[...]))
            h = jnp.sin(fr * (_dot_hi(h, w2_ref[...]) + b2_ref[...]))
            h = _dot_hi(h, w3_ref[...])
            dec = dec_ref[pl.ds(r0, HY_ROWS), :]
            pos = r0 + lax.broadcasted_iota(I32, (HY_ROWS, DG), 0)
            hf = h[:, :DG] * dec
            hb = jnp.where(pos == 0, 0.0, h[:, DG:] * dec)
            a = hf + hb
            a_s[pl.ds(r0, HY_ROWS), :] = a
            d_s[pl.ds(r0, HY_ROWS), :] = hf - hb
            sgn = (1 - 2 * (pos & 1)).astype(F32)
            return kny + jnp.sum(a * sgn, axis=0, keepdims=True)

        kny = lax.fori_loop(0, SEQ // HY_ROWS, rows, jnp.zeros((1, DG), F32))
        kny_ref[...] = jnp.broadcast_to(kny, kny_ref.shape)

    kr_ref[...] = _dot_hi(c_ref[...], a_s[...])
    ki_ref[...] = _dot_hi(s_ref[...], d_s[...])


def hyena_filter_spectrum(zpos, w1, b1, w2, b2, freq, w3, decay, cos_f32, sin_f32):
    full = lambda a: pl.BlockSpec(a.shape, lambda k: (0,) * a.ndim)
    kblk = pl.BlockSpec((HY_KB, SEQ), lambda k: (k, 0))
    oblk = pl.BlockSpec((HY_KB, DG), lambda k: (k, 0))
    return pl.pallas_call(
        _hy_filter_body, grid=(SEQ // HY_KB,),
        in_specs=[full(zpos), full(w1), full(b1), full(w2), full(b2), full(freq), full(w3), full(decay), kblk, kblk],
        out_specs=[oblk, oblk, pl.BlockSpec((V7X_SUBLANES, DG), lambda k: (0, 0))],
        out_shape=[jax.ShapeDtypeStruct((SEQ, DG), F32), jax.ShapeDtypeStruct((SEQ, DG), F32),
                   jax.ShapeDtypeStruct((V7X_SUBLANES, DG), F32)],
        scratch_shapes=[pltpu.VMEM((SEQ, DG), F32), pltpu.VMEM((SEQ, DG), F32)],
        compiler_params=_params("arbitrary"), name="hyena_filter",
    )(zpos, w1, b1, w2, b2, freq, w3, decay, cos_f32, sin_f32)


CONV_ROWS = 128
CONV_HALO = 8


def _dwconv_rows(pad_ref, w_ref, r0, lanes, k):
    n = CONV_ROWS + 2 * CONV_HALO
    win = pad_ref[pl.ds(r0, n), lanes]
    acc = None
    for j in range(k):
        sh = (k // 2 - j) % n
        rolled = win if sh == 0 else pltpu.roll(win, sh, 0)
        term = rolled[CONV_HALO:CONV_HALO + CONV_ROWS] * w_ref[j:j + 1, lanes]
        acc = term if acc is None else acc + term
    return acc


def _fill_padded(pad_ref, src_ref, width):
    zeros = jnp.zeros((CONV_HALO, width), F32)
    pad_ref[pl.ds(0, CONV_HALO), :] = zeros
    pad_ref[pl.ds(SEQ + CONV_HALO, CONV_HALO), :] = zeros

    def fill(c, _):
        r0 = pl.multiple_of(c * CONV_ROWS, CONV_ROWS)
        pad_ref[pl.ds(r0 + CONV_HALO, CONV_ROWS), :] = src_ref[0, pl.ds(r0, CONV_ROWS), :].astype(F32)
        return 0

    lax.fori_loop(0, SEQ // CONV_ROWS, fill, 0)


def _hy_prep_body(p_ref, w_ref, z_ref, x0_ref, pad):
    _fill_padded(pad, p_ref, 3 * DG)

    def rows(c, _):
        r0 = pl.multiple_of(c * CONV_ROWS, CONV_ROWS)
        x0 = _dwconv_rows(pad, w_ref, r0, slice(0, DG), 3)
        x1 = _dwconv_rows(pad, w_ref, r0, slice(DG, 2 * DG), 3)
        v = _dwconv_rows(pad, w_ref, r0, slice(2 * DG, 3 * DG), 3)
        x0_ref[0, pl.ds(r0, CONV_ROWS), :] = x0.astype(BF16)
        z_ref[0, pl.ds(r0, CONV_ROWS), :] = (v * x1).astype(BF16)
        return 0

    lax.fori_loop(0, SEQ // CONV_ROWS, rows, 0)


def hyena_prep(p3, conv_w):
    b = p3.shape[0]
    blk = pl.BlockSpec((1, SEQ, DG), lambda i: (i, 0, 0))
    out = jax.ShapeDtypeStruct((b, SEQ, DG), BF16)
    return pl.pallas_call(
        _hy_prep_body, grid=(b,),
        in_specs=[pl.BlockSpec((1, SEQ, 3 * DG), lambda i: (i, 0, 0)), pl.BlockSpec(conv_w.shape, lambda i: (0, 0))],
        out_specs=[blk, blk], out_shape=[out, out],
        scratch_shapes=[pltpu.VMEM((SEQ + 2 * CONV_HALO, 3 * DG), F32)],
        compiler_params=_params("parallel"), name="hyena_prep",
    )(p3, conv_w)


HY_G = 2
HY_FB = 512


def _hy_conv_body(z_ref, x0_ref, cr_ref, sr_ref, cc_ref, sc_ref, kr_ref, ki_ref, kny_ref, fb_ref, o_ref, acc):
    kb = pl.program_id(1)
    krow = kb * HY_FB + lax.broadcasted_iota(I32, (HY_FB, 1), 0)
    wk = jnp.where(krow == 0, 1.0 / NFFT, 2.0 / NFFT)
    kr = kr_ref[...]
    ki = ki_ref[...]
    for g in range(HY_G):
        z = z_ref[g]
        zr = _dot(cr_ref[...], z)
        zi = _dot(sr_ref[...], z)
        yr = ((zr * kr - zi * ki) * wk).astype(BF16)
        yi = ((zr * ki + zi * kr) * wk).astype(BF16)
        part = _dot(cc_ref[...], yr) + _dot(sc_ref[...], yi)

        @pl.when(kb == 0)
        def _():
            acc[g] = part

        @pl.when(kb > 0)
        def _():
            acc[g] += part

    @pl.when(kb == pl.num_programs(1) - 1)
    def _():
        sgn = (1 - 2 * (lax.broadcasted_iota(I32, (SEQ, DG), 0) & 1)).astype(F32)
        for g in range(HY_G):
            zf = z_ref[g].astype(F32)
            zny = jnp.sum(zf * sgn, axis=0, keepdims=True)
            conv = acc[g] + (zny * kny_ref[0:1, :] * (1.0 / NFFT)) * sgn
            o_ref[g] = (x0_ref[g].astype(F32) * (conv + zf * fb_ref[...])).astype(BF16)


def hyena_conv(z3, x03, cos_bf, sin_bf, kr, ki, kny, fbias):
    b = z3.shape[0]
    seq_blk = pl.BlockSpec((HY_G, SEQ, DG), lambda i, k: (i, 0, 0))
    rows = pl.BlockSpec((HY_FB, SEQ), lambda i, k: (k, 0))
    cols = pl.BlockSpec((SEQ, HY_FB), lambda i, k: (0, k))
    kblk = pl.BlockSpec((HY_FB, DG), lambda i, k: (k, 0))
    return pl.pallas_call(
        _hy_conv_body, grid=(b // HY_G, SEQ // HY_FB),
        in_specs=[seq_blk, seq_blk, rows, rows, cols, cols, kblk, kblk,
                  pl.BlockSpec(kny.shape, lambda i, k: (0, 0)), pl.BlockSpec(fbias.shape, lambda i, k: (0, 0))],
        out_specs=seq_blk, out_shape=jax.ShapeDtypeStruct((b, SEQ, DG), BF16),
        scratch_shapes=[pltpu.VMEM((HY_G, SEQ, DG), F32)],
        compiler_params=_params("parallel", "arbitrary"), name="hyena_conv",
    )(z3, x03, cos_bf, sin_bf, cos_bf, sin_bf, kr, ki, kny, fbias)


N_MCH = SEQ // M_CHUNK
MQ = M_CHUNK


def _head_lane_vec(rows8, base):
    lane_head = lax.broadcasted_iota(I32, (1, DG), 1) // HD
    out = jnp.zeros((1, DG), F32)
    for h in range(N_HEADS):
        out = jnp.where(lane_head == h, rows8[base + h:base + h + 1, :], out)
    return out


def _mamba_body(z_ref, xbc_ref, dtc_ref, cw_ref, cb_ref, dtb_ref, a_ref, dsk_ref, nw_ref, tri_ref, bd_ref,
                o_ref, pad, xs_s, b_s, c_s, y_s, col_s):
    _fill_padded(pad, xbc_ref, 2 * DG)

    def conv_rows(c, _):
        r0 = pl.multiple_of(c * CONV_ROWS, CONV_ROWS)
        for g in range(4):
            lanes = slice(g * V7X_LANES, (g + 1) * V7X_LANES)
            u = _silu(_dwconv_rows(pad, cw_ref, r0, lanes, M_CONV) + cb_ref[:, lanes])
            if g < 2:
                xs_s[pl.ds(r0, CONV_ROWS), lanes] = u
            elif g == 2:
                b_s[pl.ds(r0, CONV_ROWS), :] = u.astype(BF16)
            else:
                c_s[pl.ds(r0, CONV_ROWS), :] = u.astype(BF16)
        return 0

    lax.fori_loop(0, SEQ // CONV_ROWS, conv_rows, 0)

    li = lax.broadcasted_iota(I32, (MQ, MQ), 0)
    si = lax.broadcasted_iota(I32, (MQ, MQ), 1)
    lower = si <= li
    upper = si >= li
    bdmask = bd_ref[...]

    def chunk_common(c):
        dt = _softplus(dtc_ref[0, c] + dtb_ref[...])
        a = dt * a_ref[...]
        cum = _dot01_rhs(a, tri_ref[...])
        tot = cum[:, MQ - 1:MQ]
        suf = tot - cum + a
        row_dir = lax.broadcasted_iota(I32, (8, MQ), 0) // N_HEADS
        seg = jnp.where(row_dir == 0, cum, suf)
        wgt = jnp.exp(tot - seg) * dt
        carry = jnp.exp(seg)
        stack = jnp.concatenate([seg, wgt, carry, dt], axis=0)
        return dt, seg, tot, stack.T

    def pass_fwd(c, s_f):
        r0 = pl.multiple_of(c * MQ, MQ)
        dt, seg, tot, cols = chunk_common(c)
        col_s[c] = cols
        x = xs_s[pl.ds(r0, MQ), :]
        xb = x.astype(BF16)
        bm = b_s[pl.ds(r0, MQ), :]
        cm = c_s[pl.ds(r0, MQ), :]
        bmf = bm.astype(F32)
        cmf = cm.astype(F32)
        ydiag = []
        for h in range(N_HEADS):
            g = h // 2
            cb = lax.dot_general(cm[:, g * M_STATE:(g + 1) * M_STATE], bm[:, g * M_STATE:(g + 1) * M_STATE],
                                 _NT, preferred_element_type=F32)
            lf = jnp.where(lower, jnp.exp(jnp.minimum(cols[:, h:h + 1] - seg[h:h + 1, :], 0.0)), 0.0)
            lb = jnp.where(upper, jnp.exp(jnp.minimum(cols[:, 4 + h:5 + h] - seg[4 + h:5 + h, :], 0.0)), 0.0)
            m = cb * (lf * dt[h:h + 1, :] + lb * dt[4 + h:5 + h, :])
            ydiag.append(_dot(m.astype(BF16), xb[:, h * HD:(h + 1) * HD]))
        y = jnp.concatenate(ydiag, axis=1)
        cw = jnp.concatenate([cmf[:, (h // 2) * M_STATE:(h // 2 + 1) * M_STATE] * cols[:, 16 + h:17 + h]
                              for h in range(N_HEADS)], axis=1)
        y = y + _dot(cw.astype(BF16), s_f.astype(BF16))
        y_s[pl.ds(r0, MQ), :] = y
        bw = jnp.concatenate([bmf[:, (h // 2) * M_STATE:(h // 2 + 1) * M_STATE] * cols[:, 8 + h:9 + h]
                              for h in range(N_HEADS)], axis=1)
        upd = lax.dot_general(bw.astype(BF16), xb, _TN, preferred_element_type=F32)
        decay = _head_lane_vec(jnp.exp(tot), 0)
        return s_f * decay + upd * bdmask

    lax.fori_loop(0, N_MCH, pass_fwd, jnp.zeros((DG, DG), F32))

    def pass_bwd(i, s_b):
        c = N_MCH - 1 - i
        r0 = pl.multiple_of(c * MQ, MQ)
        cols = col_s[c]
        x = xs_s[pl.ds(r0, MQ), :]
        xb = x.astype(BF16)
        bmf = b_s[pl.ds(r0, MQ), :].astype(F32)
        cmf = c_s[pl.ds(r0, MQ), :].astype(F32)
        cw = jnp.concatenate([cmf[:, (h // 2) * M_STATE:(h // 2 + 1) * M_STATE] * cols[:, 20 + h:21 + h]
                              for h in range(N_HEADS)], axis=1)
        y_s[pl.ds(r0, MQ), :] += _dot(cw.astype(BF16), s_b.astype(BF16))
        bw = jnp.concatenate([bmf[:, (h // 2) * M_STATE:(h // 2 + 1) * M_STATE] * cols[:, 12 + h:13 + h]
                              for h in range(N_HEADS)], axis=1)
        upd = lax.dot_general(bw.astype(BF16), xb, _TN, preferred_element_type=F32)
        tot_rows = jnp.concatenate([cols[0:1, 20 + h:21 + h] for h in range(N_HEADS)], axis=0)
        lane_head = lax.broadcasted_iota(I32, (1, DG), 1) // HD
        decay = jnp.zeros((1, DG), F32)
        for h in range(N_HEADS):
            decay = jnp.where(lane_head == h, tot_rows[h:h + 1, :], decay)
        return s_b * decay + upd * bdmask

    lax.fori_loop(0, N_MCH, pass_bwd, jnp.zeros((DG, DG), F32))

    def finish(c, _):
        r0 = pl.multiple_of(c * CONV_ROWS, CONV_ROWS)
        y = y_s[pl.ds(r0, CONV_ROWS), :] + xs_s[pl.ds(r0, CONV_ROWS), :] * dsk_ref[...]
        y = y * _silu(z_ref[0, pl.ds(r0, CONV_ROWS), :].astype(F32))
        o_ref[0, pl.ds(r0, CONV_ROWS), :] = (_rms(y) * nw_ref[...]).astype(BF16)
        return 0

    lax.fori_loop(0, SEQ // CONV_ROWS, finish, 0)


def mamba2(z3, xbc3, dtc4, conv_w, conv_b, dt_bias_col, a_col, dskip_lanes, norm_w, tri_incl, bdmask):
    b = z3.shape[0]
    full = lambda a: pl.BlockSpec(a.shape, lambda i: (0,) * a.ndim)
    return pl.pallas_call(
        _mamba_body, grid=(b,),
        in_specs=[pl.BlockSpec((1, SEQ, DG), lambda i: (i, 0, 0)),
                  pl.BlockSpec((1, SEQ, 2 * DG), lambda i: (i, 0, 0)),
                  pl.BlockSpec((1, N_MCH, 8, MQ), lambda i: (i, 0, 0, 0)),
                  full(conv_w), full(conv_b), full(dt_bias_col), full(a_col), full(dskip_lanes), full(norm_w),
                  full(tri_incl), full(bdmask)],
        out_specs=pl.BlockSpec((1, SEQ, DG), lambda i: (i, 0, 0)),
        out_shape=jax.ShapeDtypeStruct((b, SEQ, DG), BF16),
        scratch_shapes=[pltpu.VMEM((SEQ + 2 * CONV_HALO, 2 * DG), F32),
                        pltpu.VMEM((SEQ, DG), F32),
                        pltpu.VMEM((SEQ, 2 * M_STATE), BF16),
                        pltpu.VMEM((SEQ, 2 * M_STATE), BF16),
                        pltpu.VMEM((SEQ, DG), F32),
                        pltpu.VMEM((N_MCH, MQ, 32), F32)],
        compiler_params=_params("parallel"), name="mamba2",
    )(z3, xbc3, dtc4, conv_w, conv_b, dt_bias_col, a_col, dskip_lanes, norm_w, tri_incl, bdmask)


A_TQ = 128
A_ROWS = 256


def _attn_bias_body(ids_ref, rb_ref, o_ref):
    ids = ids_ref[0]
    for h in range(N_HEADS):
        acc = jnp.full(ids.shape, NEG_BIG, F32)
        for bkt in range(N_BUCKETS):
            acc = jnp.where(ids == bkt, rb_ref[bkt, h], acc)
        o_ref[h, 0] = acc


def attention_bias_table(bucket_ids, rel_bias):
    nvar, tq, w = bucket_ids.shape
    return pl.pallas_call(
        _attn_bias_body, grid=(nvar,),
        in_specs=[pl.BlockSpec((1, tq, w), lambda v: (v, 0, 0)),
                  pl.BlockSpec(memory_space=pltpu.SMEM)],
        out_specs=pl.BlockSpec((N_HEADS, 1, tq, w), lambda v: (0, v, 0, 0)),
        out_shape=jax.ShapeDtypeStruct((N_HEADS, nvar, tq, w), F32),
        compiler_params=_params("parallel"), name="attention_bias_table",
    )(bucket_ids, rel_bias)


A_SLABS = 3 * DG // V7X_LANES
A_QBLOCKS = SEQ // A_TQ


def _attn_body(at_ref, b1_ref, b4_ref, b16_ref, o_ref, qkv_s, part_o, part_l):
    def fill(c, _):
        r0 = pl.multiple_of(c * A_ROWS, A_ROWS)
        for s in range(A_SLABS):
            qkv_s[s, pl.ds(r0, A_ROWS), :] = at_ref[0, pl.ds(r0, A_ROWS), s * V7X_LANES:(s + 1) * V7X_LANES].astype(F32)
        return 0

    lax.fori_loop(0, SEQ // A_ROWS, fill, 0)
    first_head = lax.broadcasted_iota(I32, (A_TQ, V7X_LANES), 1) < HD

    def run_pattern(pat, dil, bias_ref):
        n = SEQ // dil
        nblk = n // A_TQ
        nvar = bias_ref.shape[1]
        w = bias_ref.shape[3]

        def rows(start, size):
            return pl.ds(start, size) if dil == 1 else pl.ds(start, size, stride=dil)

        def block(it, _):
            r = it // nblk
            i = it - r * nblk
            q0 = i * A_TQ
            if nvar == 1:
                k0 = 0
                var = 0
            else:
                k0 = jnp.clip(q0 - A_BAND, 0, n - w)
                var = jnp.where(i == 0, 0, jnp.where(i == nblk - 1, 2, 1))
            qrows = rows(r + dil * q0, A_TQ)
            krows = rows(r + dil * k0, w)
            for hp in range(2):
                q2 = qkv_s[hp, qrows, :]
                k2 = qkv_s[2 + hp, krows, :].astype(BF16)
                v2 = qkv_s[4 + hp, krows, :].astype(BF16)
                outs, lses = [], []
                for hh in range(2):
                    keep = first_head if hh == 0 else jnp.logical_not(first_head)
                    qm = jnp.where(keep, q2, 0.0).astype(BF16)
                    s = lax.dot_general(qm, k2, _NT, preferred_element_type=F32) * (HD ** -0.5)
                    s = s + bias_ref[2 * hp + hh, var]
                    m = jnp.max(s, axis=1, keepdims=True)
                    p = jnp.exp(s - m)
                    den = jnp.sum(p, axis=1, keepdims=True)
                    outs.append(_dot(p.astype(BF16), v2) / den)
                    lses.append(m + jnp.log(den))
                o_new = jnp.where(first_head, outs[0], outs[1])
                l_new = jnp.where(first_head, lses[0], lses[1])
                part_o[pat, hp, qrows, :] = o_new
                part_l[pat, hp, qrows, :] = l_new
            return 0

        lax.fori_loop(0, A_QBLOCKS, block, 0, unroll=2)

    for pat, (dil, bias_ref) in enumerate(zip(A_DILS, (b1_ref, b4_ref, b16_ref))):
        run_pattern(pat, dil, bias_ref)

    def finish(c, _):
        rows = pl.ds(pl.multiple_of(c * A_TQ, A_TQ), A_TQ)
        for hp in range(2):
            ls = [part_l[pat, hp, rows, :] for pat in range(len(A_DILS))]
            mx = jnp.maximum(jnp.maximum(ls[0], ls[1]), ls[2])
            ws = [jnp.exp(l - mx) for l in ls]
            num = ws[0] * part_o[0, hp, rows, :] + ws[1] * part_o[1, hp, rows, :] + ws[2] * part_o[2, hp, rows, :]
            o_ref[0, rows, hp * V7X_LANES:(hp + 1) * V7X_LANES] = (num / (ws[0] + ws[1] + ws[2])).astype(BF16)
        return 0

    lax.fori_loop(0, SEQ // A_TQ, finish, 0)


def dilated_attention(at3, bias1, bias4, bias16):
    b = at3.shape[0]
    full = lambda a: pl.BlockSpec(a.shape, lambda i: (0,) * a.ndim)
    return pl.pallas_call(
        _attn_body, grid=(b,),
        in_specs=[pl.BlockSpec((1, SEQ, 3 * DG), lambda i: (i, 0, 0)), full(bias1), full(bias4), full(bias16)],
        out_specs=pl.BlockSpec((1, SEQ, DG), lambda i: (i, 0, 0)),
        out_shape=jax.ShapeDtypeStruct((b, SEQ, DG), BF16),
        scratch_shapes=[pltpu.VMEM((A_SLABS, SEQ, V7X_LANES), F32),
                        pltpu.VMEM((len(A_DILS), 2, SEQ, V7X_LANES), F32),
                        pltpu.VMEM((len(A_DILS), 2, SEQ, V7X_LANES), F32)],
        compiler_params=_params("parallel"), name="dilated_attention",
    )(at3, bias1, bias4, bias16)


H_BLK = 256
H_CPB = H_BLK // H_CHUNK
N_HBLK = SEQ // H_BLK
N_HCH = SEQ // H_CHUNK


def _chunk_bcast(x, row):
    c = x.shape[1]
    x3 = x.reshape(H_CPB, H_CHUNK, c)
    return jnp.broadcast_to(x3[:, row:row + 1, :], (H_CPB, H_CHUNK, c)).reshape(H_BLK, c)


def _hgrn_body(p_ref, lb_ref, nw_ref, tin_ref, bd_ref, o_ref, qd_s, ut_s, oi_s, dec_s, oe_s, st_s):
    li = lax.broadcasted_iota(I32, (H_BLK, H_BLK), 0)
    si = lax.broadcasted_iota(I32, (H_BLK, H_BLK), 1)
    same = (li // H_CHUNK) == (si // H_CHUNK)
    mask_f = same & (si <= li)
    mask_b = same & (si >= li)
    bdmask = bd_ref[...]

    def block(bi, _):
        r0 = pl.multiple_of(bi * H_BLK, H_BLK)
        rows = pl.ds(r0, H_BLK)
        q = _silu(p_ref[0, rows, 0:DG].astype(F32))
        v = p_ref[0, rows, 3 * DG:4 * DG]
        oi = jnp.zeros((H_BLK, DG), F32)
        scores = [None] * N_HEADS
        for d in range(2):
            fpre = p_ref[0, rows, (1 + d) * DG:(2 + d) * DG].astype(F32)
            lb = lb_ref[d:d + 1, :]
            sg = jax.nn.sigmoid(fpre)
            g = jnp.log(lb + (1.0 - lb) * sg)
            k = (1.0 - lb) * (1.0 - sg)
            gi = _dot01(tin_ref[...], g)
            glast = _chunk_bcast(gi, H_CHUNK - 1)
            if d == 0:
                gc = gi
                gref = _chunk_bcast(gi, H_CHUNK // 2 - 1)
                msk = mask_f
            else:
                gc = glast - gi + g
                gref = _chunk_bcast(gc, H_CHUNK // 2)
                msk = mask_b
            qe = (q * jnp.exp(gc - gref)).astype(BF16)
            ke = (k * jnp.exp(gref - gc)).astype(BF16)
            for h in range(N_HEADS):
                hs = slice(h * HD, (h + 1) * HD)
                sc = jnp.where(msk, lax.dot_general(qe[:, hs], ke[:, hs], _NT, preferred_element_type=F32), 0.0)
                scores[h] = sc if d == 0 else scores[h] + sc
            qd_s[d, rows, :] = (q * jnp.exp(gc)).astype(BF16)
            kd = (k * jnp.exp(glast - gc)).astype(BF16)
            for j in range(H_CPB):
                cr = slice(j * H_CHUNK, (j + 1) * H_CHUNK)
                ut = lax.dot_general(v[cr, :], kd[cr, :], _TN, preferred_element_type=F32)
                ut_s[d, bi * H_CPB + j] = (ut * bdmask).astype(BF16)
                dec_s[d, bi * H_CPB + j] = jnp.broadcast_to(jnp.exp(glast[j * H_CHUNK:j * H_CHUNK + 1, :]),
                                                             (V7X_SUBLANES, DG))
        oi = jnp.concatenate([_dot(scores[h].astype(BF16), v[:, h * HD:(h + 1) * HD]) for h in range(N_HEADS)],
                             axis=1)
        oi_s[rows, :] = oi
        return 0

    lax.fori_loop(0, N_HBLK, block, 0)

    st_s[...] = jnp.zeros(st_s.shape, F32)

    def step(i, _):
        for d in range(2):
            c = i if d == 0 else N_HCH - 1 - i
            rows = pl.ds(pl.multiple_of(c * H_CHUNK, H_CHUNK), H_CHUNK)
            st = st_s[d]
            oe_s[d, rows, :] = lax.dot_general(qd_s[d, rows, :], st.astype(BF16), _NT,
                                               preferred_element_type=F32)
            st_s[d] = st * dec_s[d, c][0:1, :] + ut_s[d, c].astype(F32)
        return 0

    lax.fori_loop(0, N_HCH, step, 0)

    def finish(c, _):
        r0 = pl.multiple_of(c * CONV_ROWS, CONV_ROWS)
        rows = pl.ds(r0, CONV_ROWS)
        o = oi_s[rows, :] + oe_s[0, rows, :] + oe_s[1, rows, :]
        gate = _silu(p_ref[0, rows, 4 * DG:5 * DG].astype(F32))
        outs = [_rms(o[:, h * HD:(h + 1) * HD]) for h in range(N_HEADS)]
        o_ref[0, rows, :] = (jnp.concatenate(outs, axis=1) * nw_ref[...] * gate).astype(BF16)
        return 0

    lax.fori_loop(0, SEQ // CONV_ROWS, finish, 0)


def hgrn2(p3, lb2, norm_w_lanes, tri_in_chunk, bdmask):
    b = p3.shape[0]
    full = lambda a: pl.BlockSpec(a.shape, lambda i: (0,) * a.ndim)
    return pl.pallas_call(
        _hgrn_body, grid=(b,),
        in_specs=[pl.BlockSpec((1, SEQ, 5 * DG), lambda i: (i, 0, 0)), full(lb2), full(norm_w_lanes),
                  full(tri_in_chunk), full(bdmask)],
        out_specs=pl.BlockSpec((1, SEQ, DG), lambda i: (i, 0, 0)),
        out_shape=jax.ShapeDtypeStruct((b, SEQ, DG), BF16),
        scratch_shapes=[pltpu.VMEM((2, SEQ, DG), BF16),
                        pltpu.VMEM((2, N_HCH, DG, DG), BF16),
                        pltpu.VMEM((SEQ, DG), F32),
                        pltpu.VMEM((2, N_HCH, V7X_SUBLANES, DG), F32),
                        pltpu.VMEM((2, SEQ, DG), F32),
                        pltpu.VMEM((2, DG, DG), F32)],
        compiler_params=_params("parallel"), name="hgrn2",
    )(p3, lb2, norm_w_lanes, tri_in_chunk, bdmask)


@functools.lru_cache(maxsize=None)
def _tables():
    t = {}
    k = np.arange(SEQ, dtype=np.int64)
    ang = 2.0 * np.pi * ((k[:, None] * k[None, :]) % NFFT).astype(np.float64) / NFFT
    t["cos"] = np.cos(ang).astype(np.float32)
    t["sin"] = np.sin(ang).astype(np.float32)
    tt = np.linspace(0.0, 1.0, SEQ, dtype=np.float32)[:, None]
    bands = (HY_POS_DIM - 1) // 2
    ang_pos = (2.0 * math.pi * np.arange(SEQ, dtype=np.float32) / SEQ).astype(np.float32)
    f = np.linspace(1e-4, bands - 1, bands, dtype=np.float32)
    a2 = (ang_pos[:, None] * f[None, :]).astype(np.float32)
    z = np.concatenate([tt, np.cos(a2), -np.sin(a2)], axis=-1).astype(np.float32)
    zp = np.zeros((SEQ, V7X_LANES), np.float32)
    zp[:, :HY_POS_DIM] = z
    t["zpos"] = zp
    max_decay = math.log(1e-2) / 0.3
    min_decay = math.log(1e-2) / 1.5
    deltas = np.abs(np.linspace(min_decay, max_decay, DG, dtype=np.float32))
    t["decay"] = np.exp(-tt * deltas[None, :]).astype(np.float32)
    i128 = np.arange(V7X_LANES)
    t["u128"] = (i128[:, None] < i128[None, :]).astype(np.float32)
    im = np.arange(M_CHUNK)
    t["tri_incl"] = (im[:, None] <= im[None, :]).astype(np.float32)
    ib = np.arange(H_BLK)
    t["tri_in_chunk"] = ((ib[:, None] // H_CHUNK == ib[None, :] // H_CHUNK)
                         & (ib[None, :] <= ib[:, None])).astype(np.float32)
    idg = np.arange(DG)
    t["bdmask"] = (idg[:, None] // HD == idg[None, :] // HD).astype(np.float32)
    def bucket(rel):
        nb = N_BUCKETS // 2
        max_exact = nb // 2
        ret = (rel > 0).astype(np.int64) * nb
        n = np.abs(rel)
        nf = np.maximum(n, 1).astype(np.float64)
        large = max_exact + (np.log(nf / max_exact) / math.log(MAX_DISTANCE / max_exact)
                             * (nb - max_exact)).astype(np.int64)
        large = np.minimum(large, nb - 1)
        return ret + np.where(n < max_exact, n, large)

    for dil in A_DILS:
        n = SEQ // dil
        w = min(n, A_TQ + 2 * A_BAND)
        starts = [0] if n == w else [0, -A_BAND, -(w - A_TQ)]
        qi = np.arange(A_TQ)[:, None]
        kj = np.arange(w)[None, :]
        ids = []
        for s0 in starts:
            rel = kj + s0 - qi
            ids.append(np.where(np.abs(rel) <= A_BAND, bucket(rel * dil), -1))
        t[f"bucket{dil}"] = np.stack(ids).astype(np.int32)
    return t


def kernel(x, w_in, w_out, norm_mix_w, norm_ffn_w, hy_conv_w, hy_pos_w1, hy_pos_b1, hy_pos_w2, hy_pos_b2,
           hy_sin_freq, hy_pos_w3, hy_filt_bias, m_conv_w, m_conv_b, m_dt_bias, m_A_log, m_D, m_norm_w, rel_bias,
           hg_lb, hg_norm_w, router_w, moe_w_gate, moe_w_up, moe_w_down, final_norm_w):
    b = x.shape[0]
    t = b * SEQ
    tb = _tables()
    cos_f32 = jnp.asarray(tb["cos"])
    sin_f32 = jnp.asarray(tb["sin"])
    cos_bf = cos_f32.astype(BF16)
    sin_bf = sin_f32.astype(BF16)
    u128 = jnp.asarray(tb["u128"]).astype(BF16)
    tri_incl = jnp.asarray(tb["tri_incl"]).astype(BF16)
    tri_in_chunk = jnp.asarray(tb["tri_in_chunk"]).astype(BF16)
    bdmask = jnp.asarray(tb["bdmask"])
    attn_bias = [attention_bias_table(jnp.asarray(tb[f"bucket{d}"]), rel_bias.astype(F32)) for d in A_DILS]

    sm = jax.nn.softmax(hg_lb.astype(F32), axis=0)
    lower_bounds = jnp.cumsum(sm, axis=0) - sm[:1]

    xa = x.reshape(t, D_MODEL)
    xb = None
    for l in range(DEPTH):
        wl = w_in[l]
        w_main = jnp.concatenate([wl[:, 0:768], wl[:, 768:1024], wl[:, 1024:1536], wl[:, 1544:2312],
                                  wl[:, 2312:3592]], axis=1).astype(BF16)
        w_dt_rows = wl[:, 1536:1544].T.astype(BF16)
        hy, mz, mx, at, hg, dtc = in_projection(xa, xb, norm_mix_w[l][None, :], w_main, w_dt_rows)

        w1p = jnp.zeros((V7X_LANES, HY_HID), F32).at[:HY_POS_DIM].set(hy_pos_w1[l])
        kr, ki, kny = hyena_filter_spectrum(
            jnp.asarray(tb["zpos"]), w1p, hy_pos_b1[l][None, :], hy_pos_w2[l], hy_pos_b2[l][None, :],
            hy_sin_freq[l][None, :], hy_pos_w3[l], jnp.asarray(tb["decay"]), cos_f32, sin_f32)
        z3, x03 = hyena_prep(hy.reshape(b, SEQ, 3 * DG), hy_conv_w[l])
        ya = hyena_conv(z3, x03, cos_bf, sin_bf, kr, ki, kny, hy_filt_bias[l][None, :]).reshape(t, DG)

        a_col = (-jnp.exp(m_A_log[l].astype(F32))).reshape(8, 1)
        yb = mamba2(mz.reshape(b, SEQ, DG), mx.reshape(b, SEQ, 2 * DG), dtc.reshape(b, N_MCH, 8, MQ),
                    m_conv_w[l], m_conv_b[l][None, :], m_dt_bias[l].reshape(8, 1), a_col,
                    jnp.repeat(m_D[l].astype(F32), HD)[None, :], m_norm_w[l][None, :], tri_incl, bdmask).reshape(t, DG)

        yc = dilated_attention(at.reshape(b, SEQ, 3 * DG), *attn_bias).reshape(t, DG)

        lbl = lower_bounds[l]
        yd = hgrn2(hg.reshape(b, SEQ, 5 * DG), lbl, jnp.tile(hg_norm_w[l], N_HEADS)[None, :],
                   tri_in_chunk, bdmask).reshape(t, DG)

        xo, xn = out_projection(xa, xb, ya, yb, yc, yd, w_out[l].reshape(4, DG, D_MODEL).astype(BF16),
                                norm_ffn_w[l][None, :])
        xn3 = xn.reshape(b, SEQ, D_MODEL)
        rw_rows = router_w[l].T.astype(F32)
        rw_hi = rw_rows.astype(BF16)
        rw_lo = (rw_rows - rw_hi.astype(F32)).astype(BF16)
        rank, gate, seg = router(xo.reshape(b, SEQ, D_MODEL), norm_ffn_w[l][None, :], rw_hi, rw_lo, u128)
        moe = moe_ffn(seg[:, :, :MOE_SEG_STRIDE].reshape(-1), xn3, rank, gate,
                      moe_w_gate[l].astype(BF16), moe_w_up[l].astype(BF16), moe_w_down[l].astype(BF16))
        xa, xb = xo, moe.reshape(t, D_MODEL)
    return final_norm(xa, xb, final_norm_w[None, :]).reshape(b, SEQ, D_MODEL)
```

```python
import functools
import math

import numpy as np
import jax
import jax.numpy as jnp
from jax import lax
from jax.experimental import pallas as pl
from jax.experimental.pallas import tpu as pltpu

F32 = jnp.float32
BF16 = jnp.bfloat16
I32 = jnp.int32

D_MODEL = 1024
SEQ = 2048
DEPTH = 2
DG = 256
N_HEADS = 4
HD = 64
HY_POS_DIM = 33
HY_HID = 64
M_CONV = 5
M_STATE = 64
M_CHUNK = 128
H_CHUNK = 32
A_BAND = 64
A_DILS = (1, 4, 16)
N_BUCKETS = 32
MAX_DISTANCE = 1024
N_EXPERTS = 16
CAP = 2 * SEQ // N_EXPERTS
D_FF = 1024
EPS = 1e-6
NFFT = 2 * SEQ

V7X_LANES = 128
V7X_SUBLANES = 8
V7X_VMEM_LIMIT_BYTES = 56 * 1024 * 1024

NEG_BIG = -1e30

_NT = (((1,), (1,)), ((), ()))
_TN = (((0,), (0,)), ((), ()))


def _params(*sem):
    return pltpu.CompilerParams(dimension_semantics=sem, vmem_limit_bytes=V7X_VMEM_LIMIT_BYTES)


def _dot(a, b):
    return jnp.dot(a, b, preferred_element_type=F32)


def _dot_hi(a, b):
    return jnp.dot(a, b, preferred_element_type=F32, precision=lax.Precision.HIGHEST)


def _dot01(t_bf16, x):
    x1 = x.astype(BF16)
    r1 = x - x1.astype(F32)
    x2 = r1.astype(BF16)
    x3 = (r1 - x2.astype(F32)).astype(BF16)
    return _dot(t_bf16, x1) + _dot(t_bf16, x2) + _dot(t_bf16, x3)


def _dot01_rhs(x, t_bf16):
    x1 = x.astype(BF16)
    r1 = x - x1.astype(F32)
    x2 = r1.astype(BF16)
    x3 = (r1 - x2.astype(F32)).astype(BF16)
    return _dot(x1, t_bf16) + _dot(x2, t_bf16) + _dot(x3, t_bf16)


def _silu(x):
    return x * jax.nn.sigmoid(x)


def _softplus(x):
    return jnp.maximum(x, 0.0) + jnp.log(1.0 + jnp.exp(-jnp.abs(x)))


def _rms(x):
    return x * lax.rsqrt(jnp.mean(x * x, axis=-1, keepdims=True) + EPS)


TM_PROJ = 512
_HY0, _MZ0, _MX0, _AT0, _HG0, _PEND = 0, 768, 1024, 1536, 2304, 3584


def _inproj_body(x_ref, nw_ref, w_ref, wdt_ref, hy_ref, mz_ref, mx_ref, at_ref, hg_ref, dtc_ref):
    x = x_ref[...]
    hn = (_rms(x) * nw_ref[...]).astype(BF16)
    hy_ref[...] = _dot(hn, w_ref[:, _HY0:_MZ0]).astype(BF16)
    mz_ref[...] = _dot(hn, w_ref[:, _MZ0:_MX0]).astype(BF16)
    mx_ref[...] = _dot(hn, w_ref[:, _MX0:_AT0]).astype(BF16)
    at_ref[...] = _dot(hn, w_ref[:, _AT0:_HG0]).astype(BF16)
    hg_ref[...] = _dot(hn, w_ref[:, _HG0:_PEND]).astype(BF16)
    dt_rows = lax.dot_general(wdt_ref[...], hn, _NT, preferred_element_type=F32)
    for j in range(TM_PROJ // M_CHUNK):
        dtc_ref[j] = dt_rows[:, j * M_CHUNK:(j + 1) * M_CHUNK]


def in_projection(x, norm_w, w_main, w_dt_rows):
    t = x.shape[0]
    tm = TM_PROJ
    row = lambda w: pl.BlockSpec((tm, w), lambda i: (i, 0))
    full = lambda a: pl.BlockSpec(a.shape, lambda i: (0,) * a.ndim)
    in_specs = [row(D_MODEL), full(norm_w), full(w_main), full(w_dt_rows)]
    widths = (768, 256, 512, 768, 1280)
    out_shape = [jax.ShapeDtypeStruct((t, w), BF16) for w in widths]
    out_shape.append(jax.ShapeDtypeStruct((t // M_CHUNK, 8, M_CHUNK), F32))
    out_specs = [row(w) for w in widths] + [pl.BlockSpec((tm // M_CHUNK, 8, M_CHUNK), lambda i: (i, 0, 0))]
    return pl.pallas_call(
        _inproj_body, grid=(t // tm,), in_specs=in_specs, out_specs=out_specs, out_shape=out_shape,
        compiler_params=_params("parallel"), name="in_projection",
    )(x, norm_w, w_main, w_dt_rows)


TM_OUT = 512


def _outproj_body(x_ref, ya_ref, yb_ref, yc_ref, yd_ref, w_ref, nw_ref, xo_ref, xn_ref):
    x = x_ref[...]
    acc = x + _dot(ya_ref[...], w_ref[0]) + _dot(yb_ref[...], w_ref[1])
    acc = acc + _dot(yc_ref[...], w_ref[2]) + _dot(yd_ref[...], w_ref[3])
    xo_ref[...] = acc
    xn_ref[...] = (_rms(acc) * nw_ref[...]).astype(BF16)


def out_projection(x, ya, yb, yc, yd, w_out4, norm_w):
    t = x.shape[0]
    tm = TM_OUT
    row = lambda w: pl.BlockSpec((tm, w), lambda i: (i, 0))
    full = lambda a: pl.BlockSpec(a.shape, lambda i: (0,) * a.ndim)
    in_specs = [row(D_MODEL)] + [row(DG)] * 4 + [full(w_out4), full(norm_w)]
    return pl.pallas_call(
        _outproj_body, grid=(t // tm,), in_specs=in_specs,
        out_specs=[row(D_MODEL), row(D_MODEL)],
        out_shape=[jax.ShapeDtypeStruct((t, D_MODEL), F32), jax.ShapeDtypeStruct((t, D_MODEL), BF16)],
        compiler_params=_params("parallel"), name="out_projection",
    )(x, ya, yb, yc, yd, w_out4, norm_w)


def _prefix_excl_lanes(mask_f32, u_ref):
    e = mask_f32.shape[0]
    off = jnp.zeros((e, 1), F32)
    parts, bounds = [], [off]
    for k in range(SEQ // V7X_LANES):
        tile = mask_f32[:, k * V7X_LANES:(k + 1) * V7X_LANES]
        parts.append(_dot(tile.astype(BF16), u_ref[...]) + off)
        off = off + jnp.sum(tile, axis=1, keepdims=True)
        bounds.append(off)
    return jnp.concatenate(parts, axis=1), bounds


def _router_body(xo_ref, nw_ref, rwh_ref, rwl_ref, u_ref, rank_ref, gate_ref, seg_ref):
    xn = _rms(xo_ref[0]) * nw_ref[...]
    xh = xn.astype(BF16)
    xl = (xn - xh.astype(F32)).astype(BF16)
    nt = lambda w, a: lax.dot_general(w, a, _NT, preferred_element_type=F32)
    logits = nt(rwh_ref[...], xh) + nt(rwh_ref[...], xl) + nt(rwl_ref[...], xh)
    mx = jnp.max(logits, axis=0, keepdims=True)
    ex = jnp.exp(logits - mx)
    aff = ex / jnp.sum(ex, axis=0, keepdims=True)
    bits = pltpu.bitcast(aff, I32)

    def search(i, thr):
        cand = thr | jnp.left_shift(jnp.int32(1), 30 - i)
        cnt = jnp.sum((bits >= cand).astype(I32), axis=1, keepdims=True)
        return jnp.where(cnt >= CAP, cand, thr)

    thr = lax.fori_loop(0, 31, search, jnp.zeros((N_EXPERTS, 1), I32))
    gt = (bits > thr).astype(F32)
    eq = (bits == thr).astype(F32)
    need = CAP - jnp.sum(gt, axis=1, keepdims=True)
    tie_rank, _ = _prefix_excl_lanes(eq, u_ref)
    sel = gt + eq * (tie_rank < need).astype(F32)
    rank, bounds = _prefix_excl_lanes(sel, u_ref)
    rank_ref[0] = jnp.where(sel > 0.0, rank, -1.0)
    gate_ref[0] = aff
    lane = lax.broadcasted_iota(I32, (N_EXPERTS, V7X_LANES), 1)
    seg = jnp.zeros((N_EXPERTS, V7X_LANES), F32)
    for sgm in range(N_MOE_SEG + 1):
        seg = jnp.where(lane == sgm, bounds[sgm * (MOE_SEG // V7X_LANES)], seg)
    seg_ref[0] = seg.astype(I32)


def router(xo3, norm_w, rw_hi, rw_lo, u128):
    b = xo3.shape[0]
    out = jax.ShapeDtypeStruct((b, N_EXPERTS, SEQ), F32)
    full = lambda a: pl.BlockSpec(a.shape, lambda i: (0,) * a.ndim)
    return pl.pallas_call(
        _router_body, grid=(b,),
        in_specs=[pl.BlockSpec((1, SEQ, D_MODEL), lambda i: (i, 0, 0)), full(norm_w), full(rw_hi), full(rw_lo),
                  full(u128)],
        out_specs=[pl.BlockSpec((1, N_EXPERTS, SEQ), lambda i: (i, 0, 0))] * 2
                  + [pl.BlockSpec((1, N_EXPERTS, V7X_LANES), lambda i: (i, 0, 0))],
        out_shape=[out, out, jax.ShapeDtypeStruct((b, N_EXPERTS, V7X_LANES), I32)],
        compiler_params=_params("parallel"), name="router",
    )(xo3, norm_w, rw_hi, rw_lo, u128)


MOE_SEG = 256
N_MOE_SEG = SEQ // MOE_SEG
MOE_TILE = 64
MOE_ALIGN = 16
MOE_GROUP = 4
MOE_SEG_STRIDE = 16
MOE_FFN_SEQS = 4


def _moe_seg_plan(cs_ref, b, s):
    starts, rounds = [], jnp.int32(0)
    for ex in range(N_EXPERTS):
        base = (b * N_EXPERTS + ex) * MOE_SEG_STRIDE
        first = (cs_ref[base + s] // MOE_ALIGN) * MOE_ALIGN
        span = cs_ref[base + s + 1] - first
        starts.append(first)
        rounds = jnp.maximum(rounds, (span + MOE_TILE - 1) // MOE_TILE)
    return starts, rounds


def _moe_tile_bases(starts, r):
    own = [st + r * MOE_TILE for st in starts]
    return [pl.multiple_of(jnp.minimum(o, CAP - MOE_TILE), MOE_ALIGN) for o in own], own


def _moe_onehot_group(rank_ref, gate_ref, s, bases, own, grp):
    lanes = pl.ds(pl.multiple_of(s * MOE_SEG, MOE_SEG), MOE_SEG)
    j = lax.broadcasted_iota(I32, (MOE_TILE, MOE_SEG), 0)
    rows = []
    for ex in grp:
        slot = bases[ex] + j
        hit = (rank_ref[0, ex:ex + 1, lanes] == slot.astype(F32)) & (slot >= own[ex])
        val = 1.0 if gate_ref is None else gate_ref[0, ex:ex + 1, lanes]
        rows.append(jnp.where(hit, val, 0.0).astype(BF16))
    return jnp.concatenate(rows, axis=0)


_MOE_GROUPS = [list(range(g * MOE_GROUP, (g + 1) * MOE_GROUP)) for g in range(N_EXPERTS // MOE_GROUP)]


def _moe_gather_body(cs_ref, xn_ref, rank_ref, xe_ref):
    b = pl.program_id(0)

    def zero(ex, _):
        xe_ref[0, ex] = jnp.zeros((CAP, D_MODEL), BF16)
        return 0

    lax.fori_loop(0, N_EXPERTS, zero, 0)

    def seg_gather(s, _):
        starts, rounds = _moe_seg_plan(cs_ref, b, s)
        xn_seg = xn_ref[0, pl.ds(pl.multiple_of(s * MOE_SEG, MOE_SEG), MOE_SEG), :]

        def one_round(r, _):
            bases, own = _moe_tile_bases(starts, r)
            for grp in _MOE_GROUPS:
                got = _dot(_moe_onehot_group(rank_ref, None, s, bases, own, grp), xn_seg)
                for k, ex in enumerate(grp):
                    rows = pl.ds(bases[ex], MOE_TILE)
                    old = xe_ref[0, ex, rows, :].astype(F32)
                    xe_ref[0, ex, rows, :] = (old + got[k * MOE_TILE:(k + 1) * MOE_TILE]).astype(BF16)
            return 0

        lax.fori_loop(0, rounds, one_round, 0)
        return 0

    lax.fori_loop(0, N_MOE_SEG, seg_gather, 0)


def moe_gather(seg_counts_flat, xn3, rank3):
    b = xn3.shape[0]
    grid_spec = pltpu.PrefetchScalarGridSpec(
        num_scalar_prefetch=1, grid=(b,),
        in_specs=[pl.BlockSpec((1, SEQ, D_MODEL), lambda i, cs: (i, 0, 0)),
                  pl.BlockSpec((1, N_EXPERTS, SEQ), lambda i, cs: (i, 0, 0))],
        out_specs=pl.BlockSpec((1, N_EXPERTS, CAP, D_MODEL), lambda i, cs: (i, 0, 0, 0)))
    return pl.pallas_call(
        _moe_gather_body, grid_spec=grid_spec,
        out_shape=jax.ShapeDtypeStruct((b, N_EXPERTS, CAP, D_MODEL), BF16),
        compiler_params=_params("parallel"), name="moe_gather",
    )(seg_counts_flat, xn3, rank3)


def _moe_experts_body(xe_ref, wg_ref, wu_ref, wd_ref, ye_ref):
    xe = xe_ref[...].reshape(MOE_FFN_SEQS * CAP, D_MODEL)
    hid = (_silu(_dot(xe, wg_ref[0])) * _dot(xe, wu_ref[0])).astype(BF16)
    ye_ref[...] = _dot(hid, wd_ref[0]).astype(BF16).reshape(MOE_FFN_SEQS, 1, CAP, D_MODEL)


def moe_experts(xe4, w_gate, w_up, w_down):
    b = xe4.shape[0]
    blk = pl.BlockSpec((MOE_FFN_SEQS, 1, CAP, D_MODEL), lambda e, g: (g, e, 0, 0))
    w_spec = lambda a: pl.BlockSpec((1,) + a.shape[1:], lambda e, g: (e, 0, 0))
    return pl.pallas_call(
        _moe_experts_body, grid=(N_EXPERTS, b // MOE_FFN_SEQS),
        in_specs=[blk, w_spec(w_gate), w_spec(w_up), w_spec(w_down)],
        out_specs=blk, out_shape=jax.ShapeDtypeStruct(xe4.shape, BF16),
        compiler_params=_params("parallel", "parallel"), name="moe_experts",
    )(xe4, w_gate, w_up, w_down)


MOE_SCATTER_SEGS = 4


def _moe_scatter_body(final, cs_ref, ye_ref, rank_ref, gate_ref, xo_ref, nw_ref, o_ref):
    b = pl.program_id(0)
    half = pl.program_id(1)

    def seg_scatter(k, _):
        s = half * MOE_SCATTER_SEGS + k
        starts, rounds = _moe_seg_plan(cs_ref, b, s)
        tok = pl.ds(pl.multiple_of(k * MOE_SEG, MOE_SEG), MOE_SEG)
        o_ref[0, tok, :] = xo_ref[0, tok, :]

        def one_round(r, _):
            bases, own = _moe_tile_bases(starts, r)
            for grp in _MOE_GROUPS:
                ye = jnp.concatenate([ye_ref[0, ex, pl.ds(bases[ex], MOE_TILE), :] for ex in grp], axis=0)
                o_ref[0, tok, :] += lax.dot_general(_moe_onehot_group(rank_ref, gate_ref, s, bases, own, grp), ye,
                                                    _TN, preferred_element_type=F32)
            return 0

        lax.fori_loop(0, rounds, one_round, 0)
        if final:
            o_ref[0, tok, :] = _rms(o_ref[0, tok, :]) * nw_ref[...]
        return 0

    lax.fori_loop(0, MOE_SCATTER_SEGS, seg_scatter, 0)


def moe_scatter(seg_counts_flat, ye4, rank3, gate3, xo3, final_norm_w, final):
    b = ye4.shape[0]
    rows = MOE_SCATTER_SEGS * MOE_SEG
    sel_spec = pl.BlockSpec((1, N_EXPERTS, SEQ), lambda i, j, cs: (i, 0, 0))
    tok_spec = pl.BlockSpec((1, rows, D_MODEL), lambda i, j, cs: (i, j, 0))
    grid_spec = pltpu.PrefetchScalarGridSpec(
        num_scalar_prefetch=1, grid=(b, SEQ // rows),
        in_specs=[pl.BlockSpec((1, N_EXPERTS, CAP, D_MODEL), lambda i, j, cs: (i, 0, 0, 0)), sel_spec, sel_spec,
                  tok_spec, pl.BlockSpec(final_norm_w.shape, lambda i, j, cs: (0, 0))],
        out_specs=tok_spec)
    return pl.pallas_call(
        functools.partial(_moe_scatter_body, final), grid_spec=grid_spec,
        out_shape=jax.ShapeDtypeStruct((b, SEQ, D_MODEL), F32),
        compiler_params=_params("parallel", "arbitrary"), name="moe_scatter",
    )(seg_counts_flat, ye4, rank3, gate3, xo3, final_norm_w)


HY_KB = 256
HY_ROWS = 256


def _hy_filter_body(z_ref, w1_ref, b1_ref, w2_ref, b2_ref, fr_ref, w3_ref, dec_ref, c_ref, s_ref,
                    kr_ref, ki_ref, kny_ref, a_s, d_s):
    @pl.when(pl.program_id(0) == 0)
    def _():
        def rows(c, kny):
            r0 = pl.multiple_of(c * HY_ROWS, HY_ROWS)
            fr = fr_ref[...]
            h = jnp.sin(fr * (_dot_hi(z_ref[pl.ds(r0, HY_ROWS), :], w1_ref[...]) + b1_ref[...]))
            h = jnp.sin(fr * (_dot_hi(h, w2_ref[...]) + b2_ref[...]))
            h = _dot_hi(h, w3_ref[...])
            dec = dec_ref[pl.ds(r0, HY_ROWS), :]
            pos = r0 + lax.broadcasted_iota(I32, (HY_ROWS, DG), 0)
            hf = h[:, :DG] * dec
            hb = jnp.where(pos == 0, 0.0, h[:, DG:] * dec)
            a = hf + hb
            a_s[pl.ds(r0, HY_ROWS), :] = a
            d_s[pl.ds(r0, HY_ROWS), :] = hf - hb
            sgn = (1 - 2 * (pos & 1)).astype(F32)
            return kny + jnp.sum(a * sgn, axis=0, keepdims=True)

        kny = lax.fori_loop(0, SEQ // HY_ROWS, rows, jnp.zeros((1, DG), F32))
        kny_ref[...] = jnp.broadcast_to(kny, kny_ref.shape)

    kr_ref[...] = _dot_hi(c_ref[...], a_s[...])
    ki_ref[...] = _dot_hi(s_ref[...], d_s[...])


def hyena_filter_spectrum(zpos, w1, b1, w2, b2, freq, w3, decay, cos_f32, sin_f32):
    full = lambda a: pl.BlockSpec(a.shape, lambda k: (0,) * a.ndim)
    kblk = pl.BlockSpec((HY_KB, SEQ), lambda k: (k, 0))
    oblk = pl.BlockSpec((HY_KB, DG), lambda k: (k, 0))
    return pl.pallas_call(
        _hy_filter_body, grid=(SEQ // HY_KB,),
        in_specs=[full(zpos), full(w1), full(b1), full(w2), full(b2), full(freq), full(w3), full(decay), kblk, kblk],
        out_specs=[oblk, oblk, pl.BlockSpec((V7X_SUBLANES, DG), lambda k: (0, 0))],
        out_shape=[jax.ShapeDtypeStruct((SEQ, DG), F32), jax.ShapeDtypeStruct((SEQ, DG), F32),
                   jax.ShapeDtypeStruct((V7X_SUBLANES, DG), F32)],
        scratch_shapes=[pltpu.VMEM((SEQ, DG), F32), pltpu.VMEM((SEQ, DG), F32)],
        compiler_params=_params("arbitrary"), name="hyena_filter",
    )(zpos, w1, b1, w2, b2, freq, w3, decay, cos_f32, sin_f32)


CONV_ROWS = 128
CONV_HALO = 8


def _dwconv_rows(pad_ref, w_ref, r0, lanes, k):
    n = CONV_ROWS + 2 * CONV_HALO
    win = pad_ref[pl.ds(r0, n), lanes]
    acc = None
    for j in range(k):
        sh = (k // 2 - j) % n
        rolled = win if sh == 0 else pltpu.roll(win, sh, 0)
        term = rolled[CONV_HALO:CONV_HALO + CONV_ROWS] * w_ref[j:j + 1, lanes]
        acc = term if acc is None else acc + term
    return acc


def _fill_padded(pad_ref, src_ref, width):
    zeros = jnp.zeros((CONV_HALO, width), F32)
    pad_ref[pl.ds(0, CONV_HALO), :] = zeros
    pad_ref[pl.ds(SEQ + CONV_HALO, CONV_HALO), :] = zeros

    def fill(c, _):
        r0 = pl.multiple_of(c * CONV_ROWS, CONV_ROWS)
        pad_ref[pl.ds(r0 + CONV_HALO, CONV_ROWS), :] = src_ref[0, pl.ds(r0, CONV_ROWS), :].astype(F32)
        return 0

    lax.fori_loop(0, SEQ // CONV_ROWS, fill, 0)


def _hy_prep_body(p_ref, w_ref, z_ref, x0_ref, pad):
    _fill_padded(pad, p_ref, 3 * DG)

    def rows(c, _):
        r0 = pl.multiple_of(c * CONV_ROWS, CONV_ROWS)
        x0 = _dwconv_rows(pad, w_ref, r0, slice(0, DG), 3)
        x1 = _dwconv_rows(pad, w_ref, r0, slice(DG, 2 * DG), 3)
        v = _dwconv_rows(pad, w_ref, r0, slice(2 * DG, 3 * DG), 3)
        x0_ref[0, pl.ds(r0, CONV_ROWS), :] = x0.astype(BF16)
        z_ref[0, pl.ds(r0, CONV_ROWS), :] = (v * x1).astype(BF16)
        return 0

    lax.fori_loop(0, SEQ // CONV_ROWS, rows, 0)


def hyena_prep(p3, conv_w):
    b = p3.shape[0]
    blk = pl.BlockSpec((1, SEQ, DG), lambda i: (i, 0, 0))
    out = jax.ShapeDtypeStruct((b, SEQ, DG), BF16)
    return pl.pallas_call(
        _hy_prep_body, grid=(b,),
        in_specs=[pl.BlockSpec((1, SEQ, 3 * DG), lambda i: (i, 0, 0)), pl.BlockSpec(conv_w.shape, lambda i: (0, 0))],
        out_specs=[blk, blk], out_shape=[out, out],
        scratch_shapes=[pltpu.VMEM((SEQ + 2 * CONV_HALO, 3 * DG), F32)],
        compiler_params=_params("parallel"), name="hyena_prep",
    )(p3, conv_w)


HY_G = 2
HY_FB = 512


def _hy_conv_body(z_ref, x0_ref, cr_ref, sr_ref, cc_ref, sc_ref, kr_ref, ki_ref, kny_ref, fb_ref, o_ref, acc):
    kb = pl.program_id(1)
    krow = kb * HY_FB + lax.broadcasted_iota(I32, (HY_FB, 1), 0)
    wk = jnp.where(krow == 0, 1.0 / NFFT, 2.0 / NFFT)
    kr = kr_ref[...]
    ki = ki_ref[...]
    for g in range(HY_G):
        z = z_ref[g]
        zr = _dot(cr_ref[...], z)
        zi = _dot(sr_ref[...], z)
        yr = ((zr * kr - zi * ki) * wk).astype(BF16)
        yi = ((zr * ki + zi * kr) * wk).astype(BF16)
        part = _dot(cc_ref[...], yr) + _dot(sc_ref[...], yi)

        @pl.when(kb == 0)
        def _():
            acc[g] = part

        @pl.when(kb > 0)
        def _():
            acc[g] += part

    @pl.when(kb == pl.num_programs(1) - 1)
    def _():
        sgn = (1 - 2 * (lax.broadcasted_iota(I32, (SEQ, DG), 0) & 1)).astype(F32)
        for g in range(HY_G):
            zf = z_ref[g].astype(F32)
            zny = jnp.sum(zf * sgn, axis=0, keepdims=True)
            conv = acc[g] + (zny * kny_ref[0:1, :] * (1.0 / NFFT)) * sgn
            o_ref[g] = (x0_ref[g].astype(F32) * (conv + zf * fb_ref[...])).astype(BF16)


def hyena_conv(z3, x03, cos_bf, sin_bf, kr, ki, kny, fbias):
    b = z3.shape[0]
    seq_blk = pl.BlockSpec((HY_G, SEQ, DG), lambda i, k: (i, 0, 0))
    rows = pl.BlockSpec((HY_FB, SEQ), lambda i, k: (k, 0))
    cols = pl.BlockSpec((SEQ, HY_FB), lambda i, k: (0, k))
    kblk = pl.BlockSpec((HY_FB, DG), lambda i, k: (k, 0))
    return pl.pallas_call(
        _hy_conv_body, grid=(b // HY_G, SEQ // HY_FB),
        in_specs=[seq_blk, seq_blk, rows, rows, cols, cols, kblk, kblk,
                  pl.BlockSpec(kny.shape, lambda i, k: (0, 0)), pl.BlockSpec(fbias.shape, lambda i, k: (0, 0))],
        out_specs=seq_blk, out_shape=jax.ShapeDtypeStruct((b, SEQ, DG), BF16),
        scratch_shapes=[pltpu.VMEM((HY_G, SEQ, DG), F32)],
        compiler_params=_params("parallel", "arbitrary"), name="hyena_conv",
    )(z3, x03, cos_bf, sin_bf, cos_bf, sin_bf, kr, ki, kny, fbias)


N_MCH = SEQ // M_CHUNK
MQ = M_CHUNK


def _head_lane_vec(rows8, base):
    lane_head = lax.broadcasted_iota(I32, (1, DG), 1) // HD
    out = jnp.zeros((1, DG), F32)
    for h in range(N_HEADS):
        out = jnp.where(lane_head == h, rows8[base + h:base + h + 1, :], out)
    return out


def _mamba_body(z_ref, xbc_ref, dtc_ref, cw_ref, cb_ref, dtb_ref, a_ref, dsk_ref, nw_ref, tri_ref, bd_ref,
                o_ref, pad, xs_s, b_s, c_s, y_s, u_s, dec_s, cw_s, yo_s, st_s):
    _fill_padded(pad, xbc_ref, 2 * DG)

    def conv_rows(c, _):
        r0 = pl.multiple_of(c * CONV_ROWS, CONV_ROWS)
        for g in range(4):
            lanes = slice(g * V7X_LANES, (g + 1) * V7X_LANES)
            u = _silu(_dwconv_rows(pad, cw_ref, r0, lanes, M_CONV) + cb_ref[:, lanes])
            if g < 2:
                xs_s[pl.ds(r0, CONV_ROWS), lanes] = u
            elif g == 2:
                b_s[pl.ds(r0, CONV_ROWS), :] = u.astype(BF16)
            else:
                c_s[pl.ds(r0, CONV_ROWS), :] = u.astype(BF16)
        return 0

    lax.fori_loop(0, SEQ // CONV_ROWS, conv_rows, 0)

    li = lax.broadcasted_iota(I32, (MQ, MQ), 0)
    si = lax.broadcasted_iota(I32, (MQ, MQ), 1)
    lower = si <= li
    upper = si >= li
    bdmask = bd_ref[...]

    def chunk(c, _):
        r0 = pl.multiple_of(c * MQ, MQ)
        dt = _softplus(dtc_ref[0, c] + dtb_ref[...])
        a = dt * a_ref[...]
        cum = _dot01_rhs(a, tri_ref[...])
        tot = cum[:, MQ - 1:MQ]
        suf = tot - cum + a
        row_dir = lax.broadcasted_iota(I32, (8, MQ), 0) // N_HEADS
        seg = jnp.where(row_dir == 0, cum, suf)
        wgt = jnp.exp(tot - seg) * dt
        cols = jnp.concatenate([seg, jnp.exp(seg)], axis=0).T
        x = xs_s[pl.ds(r0, MQ), :]
        xb = x.astype(BF16)
        bm = b_s[pl.ds(r0, MQ), :]
        cm = c_s[pl.ds(r0, MQ), :]
        cmf = cm.astype(F32)
        bt = bm.astype(F32).T
        ydiag = []
        for h in range(N_HEADS):
            g = h // 2
            cb = lax.dot_general(cm[:, g * M_STATE:(g + 1) * M_STATE], bm[:, g * M_STATE:(g + 1) * M_STATE],
                                 _NT, preferred_element_type=F32)
            lf = jnp.where(lower, jnp.exp(jnp.minimum(cols[:, h:h + 1] - seg[h:h + 1, :], 0.0)), 0.0)
            lb = jnp.where(upper, jnp.exp(jnp.minimum(cols[:, 4 + h:5 + h] - seg[4 + h:5 + h, :], 0.0)), 0.0)
            m = cb * (lf * dt[h:h + 1, :] + lb * dt[4 + h:5 + h, :])
            ydiag.append(_dot(m.astype(BF16), xb[:, h * HD:(h + 1) * HD]))
        y_s[pl.ds(r0, MQ), :] = jnp.concatenate(ydiag, axis=1)
        for d in range(2):
            bwt = jnp.concatenate([bt[(h // 2) * M_STATE:(h // 2 + 1) * M_STATE, :] * wgt[4 * d + h:4 * d + h + 1, :]
                                   for h in range(N_HEADS)], axis=0)
            u_s[d, c] = (_dot(bwt.astype(BF16), xb) * bdmask).astype(BF16)
            dec_s[d, c] = jnp.broadcast_to(_head_lane_vec(jnp.exp(tot), 4 * d), (V7X_SUBLANES, DG))
            cw_s[d, c] = jnp.concatenate(
                [cmf[:, (h // 2) * M_STATE:(h // 2 + 1) * M_STATE] * cols[:, 8 + 4 * d + h:9 + 4 * d + h]
                 for h in range(N_HEADS)], axis=1).astype(BF16)
        return 0

    lax.fori_loop(0, N_MCH, chunk, 0, unroll=2)

    st_s[...] = jnp.zeros(st_s.shape, F32)

    def scan(i, _):
        for d in range(2):
            c = i if d == 0 else N_MCH - 1 - i
            st = st_s[d]
            yo_s[d, pl.ds(pl.multiple_of(c * MQ, MQ), MQ), :] = _dot(cw_s[d, c], st.astype(BF16))
            st_s[d] = st * dec_s[d, c][0:1, :] + u_s[d, c].astype(F32)
        return 0

    lax.fori_loop(0, N_MCH, scan, 0)

    def finish(c, _):
        r0 = pl.multiple_of(c * CONV_ROWS, CONV_ROWS)
        rows = pl.ds(r0, CONV_ROWS)
        y = y_s[rows, :] + yo_s[0, rows, :] + yo_s[1, rows, :] + xs_s[rows, :] * dsk_ref[...]
        y = y * _silu(z_ref[0, rows, :].astype(F32))
        o_ref[0, pl.ds(r0, CONV_ROWS), :] = (_rms(y) * nw_ref[...]).astype(BF16)
        return 0

    lax.fori_loop(0, SEQ // CONV_ROWS, finish, 0)


def mamba2(z3, xbc3, dtc4, conv_w, conv_b, dt_bias_col, a_col, dskip_lanes, norm_w, tri_incl, bdmask):
    b = z3.shape[0]
    full = lambda a: pl.BlockSpec(a.shape, lambda i: (0,) * a.ndim)
    return pl.pallas_call(
        _mamba_body, grid=(b,),
        in_specs=[pl.BlockSpec((1, SEQ, DG), lambda i: (i, 0, 0)),
                  pl.BlockSpec((1, SEQ, 2 * DG), lambda i: (i, 0, 0)),
                  pl.BlockSpec((1, N_MCH, 8, MQ), lambda i: (i, 0, 0, 0)),
                  full(conv_w), full(conv_b), full(dt_bias_col), full(a_col), full(dskip_lanes), full(norm_w),
                  full(tri_incl), full(bdmask)],
        out_specs=pl.BlockSpec((1, SEQ, DG), lambda i: (i, 0, 0)),
        out_shape=jax.ShapeDtypeStruct((b, SEQ, DG), BF16),
        scratch_shapes=[pltpu.VMEM((SEQ + 2 * CONV_HALO, 2 * DG), F32),
                        pltpu.VMEM((SEQ, DG), F32),
                        pltpu.VMEM((SEQ, 2 * M_STATE), BF16),
                        pltpu.VMEM((SEQ, 2 * M_STATE), BF16),
                        pltpu.VMEM((SEQ, DG), F32),
                        pltpu.VMEM((2, N_MCH, DG, DG), BF16),
                        pltpu.VMEM((2, N_MCH, V7X_SUBLANES, DG), F32),
                        pltpu.VMEM((2, N_MCH, MQ, DG), BF16),
                        pltpu.VMEM((2, SEQ, DG), F32),
                        pltpu.VMEM((2, DG, DG), F32)],
        compiler_params=_params("parallel"), name="mamba2",
    )(z3, xbc3, dtc4, conv_w, conv_b, dt_bias_col, a_col, dskip_lanes, norm_w, tri_incl, bdmask)


A_TQ = 128
A_ROWS = 256


def _attn_bias_body(ids_ref, rb_ref, o_ref):
    ids = ids_ref[0]
    for h in range(N_HEADS):
        acc = jnp.full(ids.shape, NEG_BIG, F32)
        for bkt in range(N_BUCKETS):
            acc = jnp.where(ids == bkt, rb_ref[bkt, h], acc)
        o_ref[h, 0] = acc


def attention_bias_table(bucket_ids, rel_bias):
    nvar, tq, w = bucket_ids.shape
    return pl.pallas_call(
        _attn_bias_body, grid=(nvar,),
        in_specs=[pl.BlockSpec((1, tq, w), lambda v: (v, 0, 0)),
                  pl.BlockSpec(memory_space=pltpu.SMEM)],
        out_specs=pl.BlockSpec((N_HEADS, 1, tq, w), lambda v: (0, v, 0, 0)),
        out_shape=jax.ShapeDtypeStruct((N_HEADS, nvar, tq, w), F32),
        compiler_params=_params("parallel"), name="attention_bias_table",
    )(bucket_ids, rel_bias)


A_SLABS = 3 * DG // V7X_LANES
A_QBLOCKS = SEQ // A_TQ


def _attn_body(at_ref, b1_ref, b4_ref, b16_ref, o_ref, qkv_s, part_o, part_l):
    def fill(c, _):
        r0 = pl.multiple_of(c * A_ROWS, A_ROWS)
        for s in range(A_SLABS):
            qkv_s[s, pl.ds(r0, A_ROWS), :] = at_ref[0, pl.ds(r0, A_ROWS), s * V7X_LANES:(s + 1) * V7X_LANES].astype(F32)
        return 0

    lax.fori_loop(0, SEQ // A_ROWS, fill, 0)
    first_head = lax.broadcasted_iota(I32, (A_TQ, V7X_LANES), 1) < HD

    def run_pattern(pat, dil, bias_ref):
        n = SEQ // dil
        nblk = n // A_TQ
        nvar = bias_ref.shape[1]
        w = bias_ref.shape[3]

        def rows(start, size):
            return pl.ds(start, size) if dil == 1 else pl.ds(start, size, stride=dil)

        def block(it, _):
            r = it // nblk
            i = it - r * nblk
            q0 = i * A_TQ
            if nvar == 1:
                k0 = 0
                var = 0
            else:
                k0 = jnp.clip(q0 - A_BAND, 0, n - w)
                var = jnp.where(i == 0, 0, jnp.where(i == nblk - 1, 2, 1))
            qrows = rows(r + dil * q0, A_TQ)
            krows = rows(r + dil * k0, w)
            for hp in range(2):
                q2 = qkv_s[hp, qrows, :]
                k2 = qkv_s[2 + hp, krows, :].astype(BF16)
                v2 = qkv_s[4 + hp, krows, :].astype(BF16)
                outs, lses = [], []
                for hh in range(2):
                    keep = first_head if hh == 0 else jnp.logical_not(first_head)
                    qm = jnp.where(keep, q2, 0.0).astype(BF16)
                    s = lax.dot_general(qm, k2, _NT, preferred_element_type=F32) * (HD ** -0.5)
                    s = s + bias_ref[2 * hp + hh, var]
                    m = jnp.max(s, axis=1, keepdims=True)
                    p = jnp.exp(s - m)
                    den = jnp.sum(p, axis=1, keepdims=True)
                    outs.append(_dot(p.astype(BF16), v2) / den)
                    lses.append(m + jnp.log(den))
                o_new = jnp.where(first_head, outs[0], outs[1])
                l_new = jnp.where(first_head, lses[0], lses[1])
                part_o[pat, hp, qrows, :] = o_new
                part_l[pat, hp, qrows, :] = l_new
            return 0

        lax.fori_loop(0, A_QBLOCKS, block, 0, unroll=2)

    for pat, (dil, bias_ref) in enumerate(zip(A_DILS, (b1_ref, b4_ref, b16_ref))):
        run_pattern(pat, dil, bias_ref)

    def finish(c, _):
        rows = pl.ds(pl.multiple_of(c * A_TQ, A_TQ), A_TQ)
        for hp in range(2):
            ls = [part_l[pat, hp, rows, :] for pat in range(len(A_DILS))]
            mx = jnp.maximum(jnp.maximum(ls[0], ls[1]), ls[2])
            ws = [jnp.exp(l - mx) for l in ls]
            num = ws[0] * part_o[0, hp, rows, :] + ws[1] * part_o[1, hp, rows, :] + ws[2] * part_o[2, hp, rows, :]
            o_ref[0, rows, hp * V7X_LANES:(hp + 1) * V7X_LANES] = (num / (ws[0] + ws[1] + ws[2])).astype(BF16)
        return 0

    lax.fori_loop(0, SEQ // A_TQ, finish, 0)


def dilated_attention(at3, bias1, bias4, bias16):
    b = at3.shape[0]
    full = lambda a: pl.BlockSpec(a.shape, lambda i: (0,) * a.ndim)
    return pl.pallas_call(
        _attn_body, grid=(b,),
        in_specs=[pl.BlockSpec((1, SEQ, 3 * DG), lambda i: (i, 0, 0)), full(bias1), full(bias4), full(bias16)],
        out_specs=pl.BlockSpec((1, SEQ, DG), lambda i: (i, 0, 0)),
        out_shape=jax.ShapeDtypeStruct((b, SEQ, DG), BF16),
        scratch_shapes=[pltpu.VMEM((A_SLABS, SEQ, V7X_LANES), F32),
                        pltpu.VMEM((len(A_DILS), 2, SEQ, V7X_LANES), F32),
                        pltpu.VMEM((len(A_DILS), 2, SEQ, V7X_LANES), F32)],
        compiler_params=_params("parallel"), name="dilated_attention",
    )(at3, bias1, bias4, bias16)


H_BLK = 256
H_CPB = H_BLK // H_CHUNK
N_HBLK = SEQ // H_BLK
N_HCH = SEQ // H_CHUNK


def _chunk_bcast(x, row):
    c = x.shape[1]
    x3 = x.reshape(H_CPB, H_CHUNK, c)
    return jnp.broadcast_to(x3[:, row:row + 1, :], (H_CPB, H_CHUNK, c)).reshape(H_BLK, c)


def _hgrn_body(p_ref, lb_ref, nw_ref, tin_ref, bd_ref, o_ref, qd_s, ut_s, oi_s, dec_s, oe_s, st_s):
    li = lax.broadcasted_iota(I32, (H_BLK, H_BLK), 0)
    si = lax.broadcasted_iota(I32, (H_BLK, H_BLK), 1)
    same = (li // H_CHUNK) == (si // H_CHUNK)
    mask_f = same & (si <= li)
    mask_b = same & (si >= li)
    bdmask = bd_ref[...]

    def block(bi, _):
        r0 = pl.multiple_of(bi * H_BLK, H_BLK)
        rows = pl.ds(r0, H_BLK)
        q = _silu(p_ref[0, rows, 0:DG].astype(F32))
        v = p_ref[0, rows, 3 * DG:4 * DG]
        oi = jnp.zeros((H_BLK, DG), F32)
        scores = [None] * N_HEADS
        for d in range(2):
            fpre = p_ref[0, rows, (1 + d) * DG:(2 + d) * DG].astype(F32)
            lb = lb_ref[d:d + 1, :]
            sg = jax.nn.sigmoid(fpre)
            g = jnp.log(lb + (1.0 - lb) * sg)
            k = (1.0 - lb) * (1.0 - sg)
            gi = _dot01(tin_ref[...], g)
            glast = _chunk_bcast(gi, H_CHUNK - 1)
            if d == 0:
                gc = gi
                gref = _chunk_bcast(gi, H_CHUNK // 2 - 1)
                msk = mask_f
            else:
                gc = glast - gi + g
                gref = _chunk_bcast(gc, H_CHUNK // 2)
                msk = mask_b
            qe = (q * jnp.exp(gc - gref)).astype(BF16)
            ke = (k * jnp.exp(gref - gc)).astype(BF16)
            for h in range(N_HEADS):
                hs = slice(h * HD, (h + 1) * HD)
                sc = jnp.where(msk, lax.dot_general(qe[:, hs], ke[:, hs], _NT, preferred_element_type=F32), 0.0)
                scores[h] = sc if d == 0 else scores[h] + sc
            qd_s[d, rows, :] = (q * jnp.exp(gc)).astype(BF16)
            kd = (k * jnp.exp(glast - gc)).astype(BF16)
            for j in range(H_CPB):
                cr = slice(j * H_CHUNK, (j + 1) * H_CHUNK)
                ut = lax.dot_general(v[cr, :], kd[cr, :], _TN, preferred_element_type=F32)
                ut_s[d, bi * H_CPB + j] = (ut * bdmask).astype(BF16)
                dec_s[d, bi * H_CPB + j] = jnp.broadcast_to(jnp.exp(glast[j * H_CHUNK:j * H_CHUNK + 1, :]),
                                                             (V7X_SUBLANES, DG))
        oi = jnp.concatenate([_dot(scores[h].astype(BF16), v[:, h * HD:(h + 1) * HD]) for h in range(N_HEADS)],
                             axis=1)
        oi_s[rows, :] = oi
        return 0

    lax.fori_loop(0, N_HBLK, block, 0)

    st_s[...] = jnp.zeros(st_s.shape, F32)

    def step(i, _):
        for d in range(2):
            c = i if d == 0 else N_HCH - 1 - i
            rows = pl.ds(pl.multiple_of(c * H_CHUNK, H_CHUNK), H_CHUNK)
            st = st_s[d]
            oe_s[d, rows, :] = lax.dot_general(qd_s[d, rows, :], st.astype(BF16), _NT,
                                               preferred_element_type=F32)
            st_s[d] = st * dec_s[d, c][0:1, :] + ut_s[d, c].astype(F32)
        return 0

    lax.fori_loop(0, N_HCH, step, 0)

    def finish(c, _):
        r0 = pl.multiple_of(c * CONV_ROWS, CONV_ROWS)
        rows = pl.ds(r0, CONV_ROWS)
        o = oi_s[rows, :] + oe_s[0, rows, :] + oe_s[1, rows, :]
        gate = _silu(p_ref[0, rows, 4 * DG:5 * DG].astype(F32))
        outs = [_rms(o[:, h * HD:(h + 1) * HD]) for h in range(N_HEADS)]
        o_ref[0, rows, :] = (jnp.concatenate(outs, axis=1) * nw_ref[...] * gate).astype(BF16)
        return 0

    lax.fori_loop(0, SEQ // CONV_ROWS, finish, 0)


def hgrn2(p3, lb2, norm_w_lanes, tri_in_chunk, bdmask):
    b = p3.shape[0]
    full = lambda a: pl.BlockSpec(a.shape, lambda i: (0,) * a.ndim)
    return pl.pallas_call(
        _hgrn_body, grid=(b,),
        in_specs=[pl.BlockSpec((1, SEQ, 5 * DG), lambda i: (i, 0, 0)), full(lb2), full(norm_w_lanes),
                  full(tri_in_chunk), full(bdmask)],
        out_specs=pl.BlockSpec((1, SEQ, DG), lambda i: (i, 0, 0)),
        out_shape=jax.ShapeDtypeStruct((b, SEQ, DG), BF16),
        scratch_shapes=[pltpu.VMEM((2, SEQ, DG), BF16),
                        pltpu.VMEM((2, N_HCH, DG, DG), BF16),
                        pltpu.VMEM((SEQ, DG), F32),
                        pltpu.VMEM((2, N_HCH, V7X_SUBLANES, DG), F32),
                        pltpu.VMEM((2, SEQ, DG), F32),
                        pltpu.VMEM((2, DG, DG), F32)],
        compiler_params=_params("parallel"), name="hgrn2",
    )(p3, lb2, norm_w_lanes, tri_in_chunk, bdmask)


@functools.lru_cache(maxsize=None)
def _tables():
    t = {}
    k = np.arange(SEQ, dtype=np.int64)
    ang = 2.0 * np.pi * ((k[:, None] * k[None, :]) % NFFT).astype(np.float64) / NFFT
    t["cos"] = np.cos(ang).astype(np.float32)
    t["sin"] = np.sin(ang).astype(np.float32)
    tt = np.linspace(0.0, 1.0, SEQ, dtype=np.float32)[:, None]
    bands = (HY_POS_DIM - 1) // 2
    ang_pos = (2.0 * math.pi * np.arange(SEQ, dtype=np.float32) / SEQ).astype(np.float32)
    f = np.linspace(1e-4, bands - 1, bands, dtype=np.float32)
    a2 = (ang_pos[:, None] * f[None, :]).astype(np.float32)
    z = np.concatenate([tt, np.cos(a2), -np.sin(a2)], axis=-1).astype(np.float32)
    zp = np.zeros((SEQ, V7X_LANES), np.float32)
    zp[:, :HY_POS_DIM] = z
    t["zpos"] = zp
    max_decay = math.log(1e-2) / 0.3
    min_decay = math.log(1e-2) / 1.5
    deltas = np.abs(np.linspace(min_decay, max_decay, DG, dtype=np.float32))
    t["decay"] = np.exp(-tt * deltas[None, :]).astype(np.float32)
    i128 = np.arange(V7X_LANES)
    t["u128"] = (i128[:, None] < i128[None, :]).astype(np.float32)
    im = np.arange(M_CHUNK)
    t["tri_incl"] = (im[:, None] <= im[None, :]).astype(np.float32)
    ib = np.arange(H_BLK)
    t["tri_in_chunk"] = ((ib[:, None] // H_CHUNK == ib[None, :] // H_CHUNK)
                         & (ib[None, :] <= ib[:, None])).astype(np.float32)
    idg = np.arange(DG)
    t["bdmask"] = (idg[:, None] // HD == idg[None, :] // HD).astype(np.float32)
    def bucket(rel):
        nb = N_BUCKETS // 2
        max_exact = nb // 2
        ret = (rel > 0).astype(np.int64) * nb
        n = np.abs(rel)
        nf = np.maximum(n, 1).astype(np.float64)
        large = max_exact + (np.log(nf / max_exact) / math.log(MAX_DISTANCE / max_exact)
                             * (nb - max_exact)).astype(np.int64)
        large = np.minimum(large, nb - 1)
        return ret + np.where(n < max_exact, n, large)

    for dil in A_DILS:
        n = SEQ // dil
        w = min(n, A_TQ + 2 * A_BAND)
        starts = [0] if n == w else [0, -A_BAND, -(w - A_TQ)]
        qi = np.arange(A_TQ)[:, None]
        kj = np.arange(w)[None, :]
        ids = []
        for s0 in starts:
            rel = kj + s0 - qi
            ids.append(np.where(np.abs(rel) <= A_BAND, bucket(rel * dil), -1))
        t[f"bucket{dil}"] = np.stack(ids).astype(np.int32)
    return t


def kernel(x, w_in, w_out, norm_mix_w, norm_ffn_w, hy_conv_w, hy_pos_w1, hy_pos_b1, hy_pos_w2, hy_pos_b2,
           hy_sin_freq, hy_pos_w3, hy_filt_bias, m_conv_w, m_conv_b, m_dt_bias, m_A_log, m_D, m_norm_w, rel_bias,
           hg_lb, hg_norm_w, router_w, moe_w_gate, moe_w_up, moe_w_down, final_norm_w):
    b = x.shape[0]
    t = b * SEQ
    tb = _tables()
    cos_f32 = jnp.asarray(tb["cos"])
    sin_f32 = jnp.asarray(tb["sin"])
    cos_bf = cos_f32.astype(BF16)
    sin_bf = sin_f32.astype(BF16)
    u128 = jnp.asarray(tb["u128"]).astype(BF16)
    tri_incl = jnp.asarray(tb["tri_incl"]).astype(BF16)
    tri_in_chunk = jnp.asarray(tb["tri_in_chunk"]).astype(BF16)
    bdmask = jnp.asarray(tb["bdmask"])
    attn_bias = [attention_bias_table(jnp.asarray(tb[f"bucket{d}"]), rel_bias.astype(F32)) for d in A_DILS]

    sm = jax.nn.softmax(hg_lb.astype(F32), axis=0)
    lower_bounds = jnp.cumsum(sm, axis=0) - sm[:1]

    xa = x.reshape(t, D_MODEL)
    for l in range(DEPTH):
        wl = w_in[l]
        w_main = jnp.concatenate([wl[:, 0:768], wl[:, 768:1024], wl[:, 1024:1536], wl[:, 1544:2312],
                                  wl[:, 2312:3592]], axis=1).astype(BF16)
        w_dt_rows = wl[:, 1536:1544].T.astype(BF16)
        hy, mz, mx, at, hg, dtc = in_projection(xa, norm_mix_w[l][None, :], w_main, w_dt_rows)

        w1p = jnp.zeros((V7X_LANES, HY_HID), F32).at[:HY_POS_DIM].set(hy_pos_w1[l])
        kr, ki, kny = hyena_filter_spectrum(
            jnp.asarray(tb["zpos"]), w1p, hy_pos_b1[l][None, :], hy_pos_w2[l], hy_pos_b2[l][None, :],
            hy_sin_freq[l][None, :], hy_pos_w3[l], jnp.asarray(tb["decay"]), cos_f32, sin_f32)
        z3, x03 = hyena_prep(hy.reshape(b, SEQ, 3 * DG), hy_conv_w[l])
        ya = hyena_conv(z3, x03, cos_bf, sin_bf, kr, ki, kny, hy_filt_bias[l][None, :]).reshape(t, DG)

        a_col = (-jnp.exp(m_A_log[l].astype(F32))).reshape(8, 1)
        yb = mamba2(mz.reshape(b, SEQ, DG), mx.reshape(b, SEQ, 2 * DG), dtc.reshape(b, N_MCH, 8, MQ),
                    m_conv_w[l], m_conv_b[l][None, :], m_dt_bias[l].reshape(8, 1), a_col,
                    jnp.repeat(m_D[l].astype(F32), HD)[None, :], m_norm_w[l][None, :], tri_incl, bdmask).reshape(t, DG)

        yc = dilated_attention(at.reshape(b, SEQ, 3 * DG), *attn_bias).reshape(t, DG)

        lbl = lower_bounds[l]
        yd = hgrn2(hg.reshape(b, SEQ, 5 * DG), lbl, jnp.tile(hg_norm_w[l], N_HEADS)[None, :],
                   tri_in_chunk, bdmask).reshape(t, DG)

        xo, xn = out_projection(xa, ya, yb, yc, yd, w_out[l].reshape(4, DG, D_MODEL).astype(BF16),
                                norm_ffn_w[l][None, :])
        xn3 = xn.reshape(b, SEQ, D_MODEL)
        rw_rows = router_w[l].T.astype(F32)
        rw_hi = rw_rows.astype(BF16)
        rw_lo = (rw_rows - rw_hi.astype(F32)).astype(BF16)
        rank, gate, seg = router(xo.reshape(b, SEQ, D_MODEL), norm_ffn_w[l][None, :], rw_hi, rw_lo, u128)
        seg_flat = seg[:, :, :MOE_SEG_STRIDE].reshape(-1)
        xe = moe_gather(seg_flat, xn3, rank)
        ye = moe_experts(xe, moe_w_gate[l].astype(BF16), moe_w_up[l].astype(BF16), moe_w_down[l].astype(BF16))
        xa = moe_scatter(seg_flat, ye, rank, gate, xo.reshape(b, SEQ, D_MODEL), final_norm_w[None, :],
                         final=(l == DEPTH - 1)).reshape(t, D_MODEL)
    return xa.reshape(b, SEQ, D_MODEL)
```

```python
import functools
import math

import numpy as np
import jax
import jax.numpy as jnp
from jax import lax
from jax.experimental import pallas as pl
from jax.experimental.pallas import tpu as pltpu

F32 = jnp.float32
BF16 = jnp.bfloat16
I32 = jnp.int32

D_MODEL = 1024
SEQ = 2048
DEPTH = 2
DG = 256
N_HEADS = 4
HD = 64
HY_POS_DIM = 33
HY_HID = 64
M_CONV = 5
M_STATE = 64
M_CHUNK = 128
H_CHUNK = 32
A_BAND = 64
A_DILS = (1, 4, 16)
N_BUCKETS = 32
MAX_DISTANCE = 1024
N_EXPERTS = 16
CAP = 2 * SEQ // N_EXPERTS
D_FF = 1024
EPS = 1e-6
NFFT = 2 * SEQ

V7X_LANES = 128
V7X_SUBLANES = 8
V7X_VMEM_LIMIT_BYTES = 56 * 1024 * 1024

NEG_BIG = -1e30

_NT = (((1,), (1,)), ((), ()))
_TN = (((0,), (0,)), ((), ()))


def _params(*sem):
    return pltpu.CompilerParams(dimension_semantics=sem, vmem_limit_bytes=V7X_VMEM_LIMIT_BYTES)


def _dot(a, b):
    return jnp.dot(a, b, preferred_element_type=F32)


def _dot_hi(a, b):
    return jnp.dot(a, b, preferred_element_type=F32, precision=lax.Precision.HIGHEST)


def _dot01(t_bf16, x):
    x1 = x.astype(BF16)
    r1 = x - x1.astype(F32)
    x2 = r1.astype(BF16)
    x3 = (r1 - x2.astype(F32)).astype(BF16)
    return _dot(t_bf16, x1) + _dot(t_bf16, x2) + _dot(t_bf16, x3)


def _dot01_rhs(x, t_bf16):
    x1 = x.astype(BF16)
    r1 = x - x1.astype(F32)
    x2 = r1.astype(BF16)
    x3 = (r1 - x2.astype(F32)).astype(BF16)
    return _dot(x1, t_bf16) + _dot(x2, t_bf16) + _dot(x3, t_bf16)


def _silu(x):
    return x * jax.nn.sigmoid(x)


def _softplus(x):
    return jnp.maximum(x, 0.0) + jnp.log(1.0 + jnp.exp(-jnp.abs(x)))


def _rms(x):
    return x * lax.rsqrt(jnp.mean(x * x, axis=-1, keepdims=True) + EPS)


TM_PROJ = 1024
_HY0, _MZ0, _MX0, _AT0, _HG0, _PEND = 0, 768, 1024, 1536, 2304, 3584


def _inproj_body(x_ref, nw_ref, w_ref, wdt_ref, hy_ref, mz_ref, mx_ref, at_ref, hg_ref, dtc_ref):
    x = x_ref[...]
    hn = (_rms(x) * nw_ref[...]).astype(BF16)
    hy_ref[...] = _dot(hn, w_ref[:, _HY0:_MZ0]).astype(BF16)
    mz_ref[...] = _dot(hn, w_ref[:, _MZ0:_MX0]).astype(BF16)
    mx_ref[...] = _dot(hn, w_ref[:, _MX0:_AT0]).astype(BF16)
    at_ref[...] = _dot(hn, w_ref[:, _AT0:_HG0]).astype(BF16)
    hg_ref[...] = _dot(hn, w_ref[:, _HG0:_PEND]).astype(BF16)
    dt_rows = lax.dot_general(wdt_ref[...], hn, _NT, preferred_element_type=F32)
    for j in range(TM_PROJ // M_CHUNK):
        dtc_ref[j] = dt_rows[:, j * M_CHUNK:(j + 1) * M_CHUNK]


def in_projection(x, norm_w, w_main, w_dt_rows):
    t = x.shape[0]
    tm = TM_PROJ
    row = lambda w: pl.BlockSpec((tm, w), lambda i: (i, 0))
    full = lambda a: pl.BlockSpec(a.shape, lambda i: (0,) * a.ndim)
    in_specs = [row(D_MODEL), full(norm_w), full(w_main), full(w_dt_rows)]
    widths = (768, 256, 512, 768, 1280)
    out_shape = [jax.ShapeDtypeStruct((t, w), BF16) for w in widths]
    out_shape.append(jax.ShapeDtypeStruct((t // M_CHUNK, 8, M_CHUNK), F32))
    out_specs = [row(w) for w in widths] + [pl.BlockSpec((tm // M_CHUNK, 8, M_CHUNK), lambda i: (i, 0, 0))]
    return pl.pallas_call(
        _inproj_body, grid=(t // tm,), in_specs=in_specs, out_specs=out_specs, out_shape=out_shape,
        compiler_params=_params("parallel"), name="in_projection",
    )(x, norm_w, w_main, w_dt_rows)


TM_OUT = 512


def _outproj_body(x_ref, ya_ref, yb_ref, yc_ref, yd_ref, w_ref, nw_ref, xo_ref, xn_ref):
    x = x_ref[...]
    acc = x + _dot(ya_ref[...], w_ref[0]) + _dot(yb_ref[...], w_ref[1])
    acc = acc + _dot(yc_ref[...], w_ref[2]) + _dot(yd_ref[...], w_ref[3])
    xo_ref[...] = acc
    xn_ref[...] = (_rms(acc) * nw_ref[...]).astype(BF16)


def out_projection(x, ya, yb, yc, yd, w_out4, norm_w):
    t = x.shape[0]
    tm = TM_OUT
    row = lambda w: pl.BlockSpec((tm, w), lambda i: (i, 0))
    full = lambda a: pl.BlockSpec(a.shape, lambda i: (0,) * a.ndim)
    in_specs = [row(D_MODEL)] + [row(DG)] * 4 + [full(w_out4), full(norm_w)]
    return pl.pallas_call(
        _outproj_body, grid=(t // tm,), in_specs=in_specs,
        out_specs=[row(D_MODEL), row(D_MODEL)],
        out_shape=[jax.ShapeDtypeStruct((t, D_MODEL), F32), jax.ShapeDtypeStruct((t, D_MODEL), BF16)],
        compiler_params=_params("parallel"), name="out_projection",
    )(x, ya, yb, yc, yd, w_out4, norm_w)


def _prefix_excl_lanes(mask_f32, u_ref):
    e = mask_f32.shape[0]
    off = jnp.zeros((e, 1), F32)
    parts, bounds = [], [off]
    for k in range(SEQ // V7X_LANES):
        tile = mask_f32[:, k * V7X_LANES:(k + 1) * V7X_LANES]
        parts.append(_dot(tile.astype(BF16), u_ref[...]) + off)
        off = off + jnp.sum(tile, axis=1, keepdims=True)
        bounds.append(off)
    return jnp.concatenate(parts, axis=1), bounds


ROUTER_SEQS = 2


def _router_body(xo_ref, nw_ref, rwh_ref, rwl_ref, u_ref, rank_ref, gate_ref, seg_ref):
    nt = lambda w, a: lax.dot_general(w, a, _NT, preferred_element_type=F32)
    affs = []
    for q in range(ROUTER_SEQS):
        xn = _rms(xo_ref[q]) * nw_ref[...]
        xh = xn.astype(BF16)
        xl = (xn - xh.astype(F32)).astype(BF16)
        logits = nt(rwh_ref[...], xh) + nt(rwh_ref[...], xl) + nt(rwl_ref[...], xh)
        ex = jnp.exp(logits - jnp.max(logits, axis=0, keepdims=True))
        affs.append(ex / jnp.sum(ex, axis=0, keepdims=True))
    aff = jnp.concatenate(affs, axis=0)
    nrow = ROUTER_SEQS * N_EXPERTS
    bits = pltpu.bitcast(aff, I32)

    def search(i, thr):
        cand = thr | jnp.left_shift(jnp.int32(1), 30 - i)
        cnt = jnp.sum((bits >= cand).astype(I32), axis=1, keepdims=True)
        return jnp.where(cnt >= CAP, cand, thr)

    thr = lax.fori_loop(0, 31, search, jnp.zeros((nrow, 1), I32))
    gt = (bits > thr).astype(F32)
    eq = (bits == thr).astype(F32)
    need = CAP - jnp.sum(gt, axis=1, keepdims=True)
    tie_rank, _ = _prefix_excl_lanes(eq, u_ref)
    sel = gt + eq * (tie_rank < need).astype(F32)
    rank, bounds = _prefix_excl_lanes(sel, u_ref)
    rank = jnp.where(sel > 0.0, rank, -1.0)
    lane = lax.broadcasted_iota(I32, (nrow, V7X_LANES), 1)
    seg = jnp.zeros((nrow, V7X_LANES), F32)
    for sgm in range(N_MOE_SEG + 1):
        seg = jnp.where(lane == sgm, bounds[sgm * (MOE_SEG // V7X_LANES)], seg)
    seg = seg.astype(I32)
    for q in range(ROUTER_SEQS):
        rows = slice(q * N_EXPERTS, (q + 1) * N_EXPERTS)
        rank_ref[q] = rank[rows]
        gate_ref[q] = aff[rows]
        seg_ref[q] = seg[rows]


def router(xo3, norm_w, rw_hi, rw_lo, u128):
    b = xo3.shape[0]
    out = jax.ShapeDtypeStruct((b, N_EXPERTS, SEQ), F32)
    full = lambda a: pl.BlockSpec(a.shape, lambda i: (0,) * a.ndim)
    return pl.pallas_call(
        _router_body, grid=(b // ROUTER_SEQS,),
        in_specs=[pl.BlockSpec((ROUTER_SEQS, SEQ, D_MODEL), lambda i: (i, 0, 0)), full(norm_w), full(rw_hi),
                  full(rw_lo), full(u128)],
        out_specs=[pl.BlockSpec((ROUTER_SEQS, N_EXPERTS, SEQ), lambda i: (i, 0, 0))] * 2
                  + [pl.BlockSpec((ROUTER_SEQS, N_EXPERTS, V7X_LANES), lambda i: (i, 0, 0))],
        out_shape=[out, out, jax.ShapeDtypeStruct((b, N_EXPERTS, V7X_LANES), I32)],
        compiler_params=_params("parallel"), name="router",
    )(xo3, norm_w, rw_hi, rw_lo, u128)


MOE_SEG = 256
N_MOE_SEG = SEQ // MOE_SEG
MOE_TILE = 64
MOE_ALIGN = 16
MOE_GROUP = 4
MOE_SEG_STRIDE = 16
MOE_FFN_SEQS = 4


def _moe_seg_plan(cs_ref, b, s):
    starts, rounds = [], jnp.int32(0)
    for ex in range(N_EXPERTS):
        base = (b * N_EXPERTS + ex) * MOE_SEG_STRIDE
        first = (cs_ref[base + s] // MOE_ALIGN) * MOE_ALIGN
        span = cs_ref[base + s + 1] - first
        starts.append(first)
        rounds = jnp.maximum(rounds, (span + MOE_TILE - 1) // MOE_TILE)
    return starts, rounds


def _moe_tile_bases(starts, r):
    own = [st + r * MOE_TILE for st in starts]
    return [pl.multiple_of(jnp.minimum(o, CAP - MOE_TILE), MOE_ALIGN) for o in own], own


def _moe_onehot_group(rank_ref, gate_ref, s, bases, own, grp):
    lanes = pl.ds(pl.multiple_of(s * MOE_SEG, MOE_SEG), MOE_SEG)
    j = lax.broadcasted_iota(I32, (MOE_TILE, MOE_SEG), 0)
    rows = []
    for ex in grp:
        slot = bases[ex] + j
        hit = (rank_ref[0, ex:ex + 1, lanes] == slot.astype(F32)) & (slot >= own[ex])
        val = 1.0 if gate_ref is None else gate_ref[0, ex:ex + 1, lanes]
        rows.append(jnp.where(hit, val, 0.0).astype(BF16))
    return jnp.concatenate(rows, axis=0)


_MOE_GROUPS = [list(range(g * MOE_GROUP, (g + 1) * MOE_GROUP)) for g in range(N_EXPERTS // MOE_GROUP)]


def _moe_gather_body(cs_ref, xn_ref, rank_ref, xe_ref):
    b = pl.program_id(0)

    def zero(ex, _):
        xe_ref[0, ex] = jnp.zeros((CAP, D_MODEL), BF16)
        return 0

    lax.fori_loop(0, N_EXPERTS, zero, 0)

    def seg_gather(s, _):
        starts, rounds = _moe_seg_plan(cs_ref, b, s)
        xn_seg = xn_ref[0, pl.ds(pl.multiple_of(s * MOE_SEG, MOE_SEG), MOE_SEG), :]

        def one_round(r, _):
            bases, own = _moe_tile_bases(starts, r)
            for grp in _MOE_GROUPS:
                got = _dot(_moe_onehot_group(rank_ref, None, s, bases, own, grp), xn_seg)
                for k, ex in enumerate(grp):
                    rows = pl.ds(bases[ex], MOE_TILE)
                    old = xe_ref[0, ex, rows, :].astype(F32)
                    xe_ref[0, ex, rows, :] = (old + got[k * MOE_TILE:(k + 1) * MOE_TILE]).astype(BF16)
            return 0

        lax.fori_loop(0, rounds, one_round, 0)
        return 0

    lax.fori_loop(0, N_MOE_SEG, seg_gather, 0)


def moe_gather(seg_counts_flat, xn3, rank3):
    b = xn3.shape[0]
    grid_spec = pltpu.PrefetchScalarGridSpec(
        num_scalar_prefetch=1, grid=(b,),
        in_specs=[pl.BlockSpec((1, SEQ, D_MODEL), lambda i, cs: (i, 0, 0)),
                  pl.BlockSpec((1, N_EXPERTS, SEQ), lambda i, cs: (i, 0, 0))],
        out_specs=pl.BlockSpec((1, N_EXPERTS, CAP, D_MODEL), lambda i, cs: (i, 0, 0, 0)))
    return pl.pallas_call(
        _moe_gather_body, grid_spec=grid_spec,
        out_shape=jax.ShapeDtypeStruct((b, N_EXPERTS, CAP, D_MODEL), BF16),
        compiler_params=_params("parallel"), name="moe_gather",
    )(seg_counts_flat, xn3, rank3)


def _moe_experts_body(xe_ref, wg_ref, wu_ref, wd_ref, ye_ref, wg_s, wu_s, wd_s):
    @pl.when(pl.program_id(1) == 0)
    def _():
        wg_s[...] = wg_ref[0, 0].astype(BF16)
        wu_s[...] = wu_ref[0, 0].astype(BF16)
        wd_s[...] = wd_ref[0, 0].astype(BF16)

    xe = xe_ref[...].reshape(MOE_FFN_SEQS * CAP, D_MODEL)
    hid = (_silu(_dot(xe, wg_s[...])) * _dot(xe, wu_s[...])).astype(BF16)
    ye_ref[...] = _dot(hid, wd_s[...]).astype(BF16).reshape(MOE_FFN_SEQS, 1, CAP, D_MODEL)


def moe_experts(xe4, w_gate, w_up, w_down, layer):
    b = xe4.shape[0]
    blk = pl.BlockSpec((MOE_FFN_SEQS, 1, CAP, D_MODEL), lambda e, g: (g, e, 0, 0))
    w_spec = lambda a: pl.BlockSpec((1, 1) + a.shape[2:], lambda e, g: (layer, e, 0, 0))
    return pl.pallas_call(
        _moe_experts_body, grid=(N_EXPERTS, b // MOE_FFN_SEQS),
        in_specs=[blk, w_spec(w_gate), w_spec(w_up), w_spec(w_down)],
        out_specs=blk, out_shape=jax.ShapeDtypeStruct(xe4.shape, BF16),
        scratch_shapes=[pltpu.VMEM((D_MODEL, D_FF), BF16), pltpu.VMEM((D_MODEL, D_FF), BF16),
                        pltpu.VMEM((D_FF, D_MODEL), BF16)],
        compiler_params=_params("parallel", "arbitrary"), name="moe_experts",
    )(xe4, w_gate, w_up, w_down)


MOE_SCATTER_SEGS = 4


def _moe_scatter_body(final, cs_ref, ye_ref, rank_ref, gate_ref, xo_ref, nw_ref, o_ref):
    b = pl.program_id(0)
    half = pl.program_id(1)

    def seg_scatter(k, _):
        s = half * MOE_SCATTER_SEGS + k
        starts, rounds = _moe_seg_plan(cs_ref, b, s)
        tok = pl.ds(pl.multiple_of(k * MOE_SEG, MOE_SEG), MOE_SEG)
        o_ref[0, tok, :] = xo_ref[0, tok, :]

        def one_round(r, _):
            bases, own = _moe_tile_bases(starts, r)
            for grp in _MOE_GROUPS:
                ye = jnp.concatenate([ye_ref[0, ex, pl.ds(bases[ex], MOE_TILE), :] for ex in grp], axis=0)
                o_ref[0, tok, :] += lax.dot_general(_moe_onehot_group(rank_ref, gate_ref, s, bases, own, grp), ye,
                                                    _TN, preferred_element_type=F32)
            return 0

        lax.fori_loop(0, rounds, one_round, 0)
        if final:
            o_ref[0, tok, :] = _rms(o_ref[0, tok, :]) * nw_ref[...]
        return 0

    lax.fori_loop(0, MOE_SCATTER_SEGS, seg_scatter, 0)


def moe_scatter(seg_counts_flat, ye4, rank3, gate3, xo3, final_norm_w, final):
    b = ye4.shape[0]
    rows = MOE_SCATTER_SEGS * MOE_SEG
    sel_spec = pl.BlockSpec((1, N_EXPERTS, SEQ), lambda i, j, cs: (i, 0, 0))
    tok_spec = pl.BlockSpec((1, rows, D_MODEL), lambda i, j, cs: (i, j, 0))
    grid_spec = pltpu.PrefetchScalarGridSpec(
        num_scalar_prefetch=1, grid=(b, SEQ // rows),
        in_specs=[pl.BlockSpec((1, N_EXPERTS, CAP, D_MODEL), lambda i, j, cs: (i, 0, 0, 0)), sel_spec, sel_spec,
                  tok_spec, pl.BlockSpec(final_norm_w.shape, lambda i, j, cs: (0, 0))],
        out_specs=tok_spec)
    return pl.pallas_call(
        functools.partial(_moe_scatter_body, final), grid_spec=grid_spec,
        out_shape=jax.ShapeDtypeStruct((b, SEQ, D_MODEL), F32),
        compiler_params=_params("parallel", "arbitrary"), name="moe_scatter",
    )(seg_counts_flat, ye4, rank3, gate3, xo3, final_norm_w)


HY_KB = 256
HY_ROWS = 256


def _hy_filter_body(z_ref, w1_ref, b1_ref, w2_ref, b2_ref, fr_ref, w3_ref, dec_ref, c_ref, s_ref,
                    kr_ref, ki_ref, kny_ref, a_s, d_s):
    @pl.when(pl.program_id(0) == 0)
    def _():
        def rows(c, kny):
            r0 = pl.multiple_of(c * HY_ROWS, HY_ROWS)
            fr = fr_ref[...]
            h = jnp.sin(fr * (_dot_hi(z_ref[pl.ds(r0, HY_ROWS), :], w1_ref[...]) + b1_ref[...]))
            h = jnp.sin(fr * (_dot_hi(h, w2_ref[...]) + b2_ref[...]))
            h = _dot_hi(h, w3_ref[...])
            dec = dec_ref[pl.ds(r0, HY_ROWS), :]
            pos = r0 + lax.broadcasted_iota(I32, (HY_ROWS, DG), 0)
            hf = h[:, :DG] * dec
            hb = jnp.where(pos == 0, 0.0, h[:, DG:] * dec)
            a = hf + hb
            a_s[pl.ds(r0, HY_ROWS), :] = a
            d_s[pl.ds(r0, HY_ROWS), :] = hf - hb
            sgn = (1 - 2 * (pos & 1)).astype(F32)
            return kny + jnp.sum(a * sgn, axis=0, keepdims=True)

        kny = lax.fori_loop(0, SEQ // HY_ROWS, rows, jnp.zeros((1, DG), F32))
        kny_ref[...] = jnp.broadcast_to(kny, kny_ref.shape)

    kr_ref[...] = _dot_hi(c_ref[...], a_s[...])
    ki_ref[...] = _dot_hi(s_ref[...], d_s[...])


def hyena_filter_spectrum(zpos, w1, b1, w2, b2, freq, w3, decay, cos_f32, sin_f32):
    full = lambda a: pl.BlockSpec(a.shape, lambda k: (0,) * a.ndim)
    kblk = pl.BlockSpec((HY_KB, SEQ), lambda k: (k, 0))
    oblk = pl.BlockSpec((HY_KB, DG), lambda k: (k, 0))
    return pl.pallas_call(
        _hy_filter_body, grid=(SEQ // HY_KB,),
        in_specs=[full(zpos), full(w1), full(b1), full(w2), full(b2), full(freq), full(w3), full(decay), kblk, kblk],
        out_specs=[oblk, oblk, pl.BlockSpec((V7X_SUBLANES, DG), lambda k: (0, 0))],
        out_shape=[jax.ShapeDtypeStruct((SEQ, DG), F32), jax.ShapeDtypeStruct((SEQ, DG), F32),
                   jax.ShapeDtypeStruct((V7X_SUBLANES, DG), F32)],
        scratch_shapes=[pltpu.VMEM((SEQ, DG), F32), pltpu.VMEM((SEQ, DG), F32)],
        compiler_params=_params("arbitrary"), name="hyena_filter",
    )(zpos, w1, b1, w2, b2, freq, w3, decay, cos_f32, sin_f32)


CONV_ROWS = 128
CONV_HALO = 8


def _dwconv_rows(pad_ref, w_ref, r0, lanes, k):
    n = CONV_ROWS + 2 * CONV_HALO
    win = pad_ref[pl.ds(r0, n), lanes]
    acc = None
    for j in range(k):
        sh = (k // 2 - j) % n
        rolled = win if sh == 0 else pltpu.roll(win, sh, 0)
        term = rolled[CONV_HALO:CONV_HALO + CONV_ROWS] * w_ref[j:j + 1, lanes]
        acc = term if acc is None else acc + term
    return acc


def _fill_padded(pad_ref, src_ref, width):
    zeros = jnp.zeros((CONV_HALO, width), F32)
    pad_ref[pl.ds(0, CONV_HALO), :] = zeros
    pad_ref[pl.ds(SEQ + CONV_HALO, CONV_HALO), :] = zeros

    def fill(c, _):
        r0 = pl.multiple_of(c * CONV_ROWS, CONV_ROWS)
        pad_ref[pl.ds(r0 + CONV_HALO, CONV_ROWS), :] = src_ref[0, pl.ds(r0, CONV_ROWS), :].astype(F32)
        return 0

    lax.fori_loop(0, SEQ // CONV_ROWS, fill, 0)


def _hy_prep_body(p_ref, w_ref, z_ref, x0_ref, pad):
    _fill_padded(pad, p_ref, 3 * DG)

    def rows(c, _):
        r0 = pl.multiple_of(c * CONV_ROWS, CONV_ROWS)
        x0 = _dwconv_rows(pad, w_ref, r0, slice(0, DG), 3)
        x1 = _dwconv_rows(pad, w_ref, r0, slice(DG, 2 * DG), 3)
        v = _dwconv_rows(pad, w_ref, r0, slice(2 * DG, 3 * DG), 3)
        x0_ref[0, pl.ds(r0, CONV_ROWS), :] = x0.astype(BF16)
        z_ref[0, pl.ds(r0, CONV_ROWS), :] = (v * x1).astype(BF16)
        return 0

    lax.fori_loop(0, SEQ // CONV_ROWS, rows, 0)


def hyena_prep(p3, conv_w):
    b = p3.shape[0]
    blk = pl.BlockSpec((1, SEQ, DG), lambda i: (i, 0, 0))
    out = jax.ShapeDtypeStruct((b, SEQ, DG), BF16)
    return pl.pallas_call(
        _hy_prep_body, grid=(b,),
        in_specs=[pl.BlockSpec((1, SEQ, 3 * DG), lambda i: (i, 0, 0)), pl.BlockSpec(conv_w.shape, lambda i: (0, 0))],
        out_specs=[blk, blk], out_shape=[out, out],
        scratch_shapes=[pltpu.VMEM((SEQ + 2 * CONV_HALO, 3 * DG), F32)],
        compiler_params=_params("parallel"), name="hyena_prep",
    )(p3, conv_w)


HY_G = 2
HY_FB = 512


def _hy_conv_body(z_ref, x0_ref, cr_ref, sr_ref, cc_ref, sc_ref, kr_ref, ki_ref, kny_ref, fb_ref, o_ref, acc):
    kb = pl.program_id(1)
    krow = kb * HY_FB + lax.broadcasted_iota(I32, (HY_FB, 1), 0)
    wk = jnp.where(krow == 0, 1.0 / NFFT, 2.0 / NFFT)
    kr = kr_ref[...]
    ki = ki_ref[...]

    @pl.when(kb == 0)
    def _():
        acc[...] = jnp.zeros(acc.shape, F32)

    for g in range(HY_G):
        z = z_ref[g]
        zr = _dot(cr_ref[...], z)
        zi = _dot(sr_ref[...], z)
        yr = ((zr * kr - zi * ki) * wk).astype(BF16)
        yi = ((zr * ki + zi * kr) * wk).astype(BF16)
        acc[g] += _dot(cc_ref[...], yr) + _dot(sc_ref[...], yi)

    @pl.when(kb == pl.num_programs(1) - 1)
    def _():
        sgn = (1 - 2 * (lax.broadcasted_iota(I32, (SEQ, DG), 0) & 1)).astype(F32)
        for g in range(HY_G):
            zf = z_ref[g].astype(F32)
            zny = jnp.sum(zf * sgn, axis=0, keepdims=True)
            conv = acc[g] + (zny * kny_ref[0:1, :] * (1.0 / NFFT)) * sgn
            o_ref[g] = (x0_ref[g].astype(F32) * (conv + zf * fb_ref[...])).astype(BF16)


def hyena_conv(z3, x03, cos_bf, sin_bf, kr, ki, kny, fbias):
    b = z3.shape[0]
    seq_blk = pl.BlockSpec((HY_G, SEQ, DG), lambda i, k: (i, 0, 0))
    rows = pl.BlockSpec((HY_FB, SEQ), lambda i, k: (k, 0))
    cols = pl.BlockSpec((SEQ, HY_FB), lambda i, k: (0, k))
    kblk = pl.BlockSpec((HY_FB, DG), lambda i, k: (k, 0))
    return pl.pallas_call(
        _hy_conv_body, grid=(b // HY_G, SEQ // HY_FB),
        in_specs=[seq_blk, seq_blk, rows, rows, cols, cols, kblk, kblk,
                  pl.BlockSpec(kny.shape, lambda i, k: (0, 0)), pl.BlockSpec(fbias.shape, lambda i, k: (0, 0))],
        out_specs=seq_blk, out_shape=jax.ShapeDtypeStruct((b, SEQ, DG), BF16),
        scratch_shapes=[pltpu.VMEM((HY_G, SEQ, DG), F32)],
        compiler_params=_params("parallel", "arbitrary"), name="hyena_conv",
    )(z3, x03, cos_bf, sin_bf, cos_bf, sin_bf, kr, ki, kny, fbias)


N_MCH = SEQ // M_CHUNK
MQ = M_CHUNK


def _head_lane_vec(rows8, base):
    lane_head = lax.broadcasted_iota(I32, (1, DG), 1) // HD
    out = jnp.zeros((1, DG), F32)
    for h in range(N_HEADS):
        out = jnp.where(lane_head == h, rows8[base + h:base + h + 1, :], out)
    return out


def _mamba_body(z_ref, xbc_ref, dtc_ref, cw_ref, cb_ref, dtb_ref, a_ref, dsk_ref, nw_ref, tri_ref, bd_ref,
                o_ref, pad, xs_s, b_s, c_s, y_s, u_s, dec_s, cw_s, yo_s, st_s):
    _fill_padded(pad, xbc_ref, 2 * DG)

    def conv_rows(c, _):
        r0 = pl.multiple_of(c * CONV_ROWS, CONV_ROWS)
        for g in range(4):
            lanes = slice(g * V7X_LANES, (g + 1) * V7X_LANES)
            u = _silu(_dwconv_rows(pad, cw_ref, r0, lanes, M_CONV) + cb_ref[:, lanes])
            if g < 2:
                xs_s[pl.ds(r0, CONV_ROWS), lanes] = u
            elif g == 2:
                b_s[pl.ds(r0, CONV_ROWS), :] = u.astype(BF16)
            else:
                c_s[pl.ds(r0, CONV_ROWS), :] = u.astype(BF16)
        return 0

    lax.fori_loop(0, SEQ // CONV_ROWS, conv_rows, 0)

    li = lax.broadcasted_iota(I32, (MQ, MQ), 0)
    si = lax.broadcasted_iota(I32, (MQ, MQ), 1)
    lower = si <= li
    upper = si >= li
    bdmask = bd_ref[...]

    def chunk(c, _):
        r0 = pl.multiple_of(c * MQ, MQ)
        dt = _softplus(dtc_ref[0, c] + dtb_ref[...])
        a = dt * a_ref[...]
        cum = _dot01_rhs(a, tri_ref[...])
        tot = cum[:, MQ - 1:MQ]
        suf = tot - cum + a
        row_dir = lax.broadcasted_iota(I32, (8, MQ), 0) // N_HEADS
        seg = jnp.where(row_dir == 0, cum, suf)
        wgt = jnp.exp(tot - seg) * dt
        cols = jnp.concatenate([seg, jnp.exp(seg)], axis=0).T
        x = xs_s[pl.ds(r0, MQ), :]
        xb = x.astype(BF16)
        bm = b_s[pl.ds(r0, MQ), :]
        cm = c_s[pl.ds(r0, MQ), :]
        cmf = cm.astype(F32)
        bt = bm.astype(F32).T
        ydiag = []
        for h in range(N_HEADS):
            g = h // 2
            cb = lax.dot_general(cm[:, g * M_STATE:(g + 1) * M_STATE], bm[:, g * M_STATE:(g + 1) * M_STATE],
                                 _NT, preferred_element_type=F32)
            lf = jnp.where(lower, jnp.exp(jnp.minimum(cols[:, h:h + 1] - seg[h:h + 1, :], 0.0)), 0.0)
            lb = jnp.where(upper, jnp.exp(jnp.minimum(cols[:, 4 + h:5 + h] - seg[4 + h:5 + h, :], 0.0)), 0.0)
            m = cb * (lf * dt[h:h + 1, :] + lb * dt[4 + h:5 + h, :])
            ydiag.append(_dot(m.astype(BF16), xb[:, h * HD:(h + 1) * HD]))
        y_s[pl.ds(r0, MQ), :] = jnp.concatenate(ydiag, axis=1)
        for d in range(2):
            bwt = jnp.concatenate([bt[(h // 2) * M_STATE:(h // 2 + 1) * M_STATE, :] * wgt[4 * d + h:4 * d + h + 1, :]
                                   for h in range(N_HEADS)], axis=0)
            u_s[d, c] = (_dot(bwt.astype(BF16), xb) * bdmask).astype(BF16)
            dec_s[d, c] = jnp.broadcast_to(_head_lane_vec(jnp.exp(tot), 4 * d), (V7X_SUBLANES, DG))
            cw_s[d, c] = jnp.concatenate(
                [cmf[:, (h // 2) * M_STATE:(h // 2 + 1) * M_STATE] * cols[:, 8 + 4 * d + h:9 + 4 * d + h]
                 for h in range(N_HEADS)], axis=1).astype(BF16)
        return 0

    lax.fori_loop(0, N_MCH, chunk, 0, unroll=2)

    st_s[...] = jnp.zeros(st_s.shape, F32)

    def scan(i, _):
        for d in range(2):
            c = i if d == 0 else N_MCH - 1 - i
            st = st_s[d]
            yo_s[d, pl.ds(pl.multiple_of(c * MQ, MQ), MQ), :] = _dot(cw_s[d, c], st.astype(BF16))
            st_s[d] = st * dec_s[d, c][0:1, :] + u_s[d, c].astype(F32)
        return 0

    lax.fori_loop(0, N_MCH, scan, 0)

    def finish(c, _):
        r0 = pl.multiple_of(c * CONV_ROWS, CONV_ROWS)
        rows = pl.ds(r0, CONV_ROWS)
        y = y_s[rows, :] + yo_s[0, rows, :] + yo_s[1, rows, :] + xs_s[rows, :] * dsk_ref[...]
        y = y * _silu(z_ref[0, rows, :].astype(F32))
        o_ref[0, pl.ds(r0, CONV_ROWS), :] = (_rms(y) * nw_ref[...]).astype(BF16)
        return 0

    lax.fori_loop(0, SEQ // CONV_ROWS, finish, 0)


def mamba2(z3, xbc3, dtc4, conv_w, conv_b, dt_bias_col, a_col, dskip_lanes, norm_w, tri_incl, bdmask):
    b = z3.shape[0]
    full = lambda a: pl.BlockSpec(a.shape, lambda i: (0,) * a.ndim)
    return pl.pallas_call(
        _mamba_body, grid=(b,),
        in_specs=[pl.BlockSpec((1, SEQ, DG), lambda i: (i, 0, 0)),
                  pl.BlockSpec((1, SEQ, 2 * DG), lambda i: (i, 0, 0)),
                  pl.BlockSpec((1, N_MCH, 8, MQ), lambda i: (i, 0, 0, 0)),
                  full(conv_w), full(conv_b), full(dt_bias_col), full(a_col), full(dskip_lanes), full(norm_w),
                  full(tri_incl), full(bdmask)],
        out_specs=pl.BlockSpec((1, SEQ, DG), lambda i: (i, 0, 0)),
        out_shape=jax.ShapeDtypeStruct((b, SEQ, DG), BF16),
        scratch_shapes=[pltpu.VMEM((SEQ + 2 * CONV_HALO, 2 * DG), F32),
                        pltpu.VMEM((SEQ, DG), F32),
                        pltpu.VMEM((SEQ, 2 * M_STATE), BF16),
                        pltpu.VMEM((SEQ, 2 * M_STATE), BF16),
                        pltpu.VMEM((SEQ, DG), F32),
                        pltpu.VMEM((2, N_MCH, DG, DG), BF16),
                        pltpu.VMEM((2, N_MCH, V7X_SUBLANES, DG), F32),
                        pltpu.VMEM((2, N_MCH, MQ, DG), BF16),
                        pltpu.VMEM((2, SEQ, DG), F32),
                        pltpu.VMEM((2, DG, DG), F32)],
        compiler_params=_params("parallel"), name="mamba2",
    )(z3, xbc3, dtc4, conv_w, conv_b, dt_bias_col, a_col, dskip_lanes, norm_w, tri_incl, bdmask)


A_TQ = 128
A_ROWS = 256


def _attn_bias_body(ids_ref, rb_ref, o_ref):
    ids = ids_ref[0]
    for h in range(N_HEADS):
        acc = jnp.full(ids.shape, NEG_BIG, F32)
        for bkt in range(N_BUCKETS):
            acc = jnp.where(ids == bkt, rb_ref[bkt, h], acc)
        o_ref[h, 0] = acc


def attention_bias_table(bucket_ids, rel_bias):
    nvar, tq, w = bucket_ids.shape
    return pl.pallas_call(
        _attn_bias_body, grid=(nvar,),
        in_specs=[pl.BlockSpec((1, tq, w), lambda v: (v, 0, 0)),
                  pl.BlockSpec(memory_space=pltpu.SMEM)],
        out_specs=pl.BlockSpec((N_HEADS, 1, tq, w), lambda v: (0, v, 0, 0)),
        out_shape=jax.ShapeDtypeStruct((N_HEADS, nvar, tq, w), F32),
        compiler_params=_params("parallel"), name="attention_bias_table",
    )(bucket_ids, rel_bias)


A_SLABS = 3 * DG // V7X_LANES
A_QBLOCKS = SEQ // A_TQ


def _attn_body(at_ref, b1_ref, b4_ref, b16_ref, o_ref, qkv_s, part_o, part_l):
    def fill(c, _):
        r0 = pl.multiple_of(c * A_ROWS, A_ROWS)
        for s in range(A_SLABS):
            qkv_s[s, pl.ds(r0, A_ROWS), :] = at_ref[0, pl.ds(r0, A_ROWS), s * V7X_LANES:(s + 1) * V7X_LANES].astype(F32)
        return 0

    lax.fori_loop(0, SEQ // A_ROWS, fill, 0)
    first_head = lax.broadcasted_iota(I32, (A_TQ, V7X_LANES), 1) < HD

    def run_pattern(pat, dil, bias_ref):
        n = SEQ // dil
        nblk = n // A_TQ
        nvar = bias_ref.shape[1]
        w = bias_ref.shape[3]

        def rows(start, size):
            return pl.ds(start, size) if dil == 1 else pl.ds(start, size, stride=dil)

        def block(it, _):
            r = it // nblk
            i = it - r * nblk
            q0 = i * A_TQ
            if nvar == 1:
                k0 = 0
                var = 0
            else:
                k0 = jnp.clip(q0 - A_BAND, 0, n - w)
                var = jnp.where(i == 0, 0, jnp.where(i == nblk - 1, 2, 1))
            qrows = rows(r + dil * q0, A_TQ)
            krows = rows(r + dil * k0, w)
            for hp in range(2):
                q2 = qkv_s[hp, qrows, :]
                k2 = qkv_s[2 + hp, krows, :].astype(BF16)
                v2 = qkv_s[4 + hp, krows, :].astype(BF16)
                outs, lses = [], []
                for hh in range(2):
                    keep = first_head if hh == 0 else jnp.logical_not(first_head)
                    qm = jnp.where(keep, q2, 0.0).astype(BF16)
                    s = lax.dot_general(qm, k2, _NT, preferred_element_type=F32) * (HD ** -0.5)
                    s = s + bias_ref[2 * hp + hh, var]
                    m = jnp.max(s, axis=1, keepdims=True)
                    p = jnp.exp(s - m)
                    den = jnp.sum(p, axis=1, keepdims=True)
                    outs.append(_dot(p.astype(BF16), v2) / den)
                    lses.append(m + jnp.log(den))
                o_new = jnp.where(first_head, outs[0], outs[1])
                l_new = jnp.where(first_head, lses[0], lses[1])
                part_o[pat, hp, qrows, :] = o_new
                part_l[pat, hp, qrows, :] = l_new
            return 0

        lax.fori_loop(0, A_QBLOCKS, block, 0, unroll=2)

    for pat, (dil, bias_ref) in enumerate(zip(A_DILS, (b1_ref, b4_ref, b16_ref))):
        run_pattern(pat, dil, bias_ref)

    def finish(c, _):
        rows = pl.ds(pl.multiple_of(c * A_TQ, A_TQ), A_TQ)
        for hp in range(2):
            ls = [part_l[pat, hp, rows, :] for pat in range(len(A_DILS))]
            mx = jnp.maximum(jnp.maximum(ls[0], ls[1]), ls[2])
            ws = [jnp.exp(l - mx) for l in ls]
            num = ws[0] * part_o[0, hp, rows, :] + ws[1] * part_o[1, hp, rows, :] + ws[2] * part_o[2, hp, rows, :]
            o_ref[0, rows, hp * V7X_LANES:(hp + 1) * V7X_LANES] = (num / (ws[0] + ws[1] + ws[2])).astype(BF16)
        return 0

    lax.fori_loop(0, SEQ // A_TQ, finish, 0)


def dilated_attention(at3, bias1, bias4, bias16):
    b = at3.shape[0]
    full = lambda a: pl.BlockSpec(a.shape, lambda i: (0,) * a.ndim)
    return pl.pallas_call(
        _attn_body, grid=(b,),
        in_specs=[pl.BlockSpec((1, SEQ, 3 * DG), lambda i: (i, 0, 0)), full(bias1), full(bias4), full(bias16)],
        out_specs=pl.BlockSpec((1, SEQ, DG), lambda i: (i, 0, 0)),
        out_shape=jax.ShapeDtypeStruct((b, SEQ, DG), BF16),
        scratch_shapes=[pltpu.VMEM((A_SLABS, SEQ, V7X_LANES), F32),
                        pltpu.VMEM((len(A_DILS), 2, SEQ, V7X_LANES), F32),
                        pltpu.VMEM((len(A_DILS), 2, SEQ, V7X_LANES), F32)],
        compiler_params=_params("parallel"), name="dilated_attention",
    )(at3, bias1, bias4, bias16)


H_BLK = 256
H_CPB = H_BLK // H_CHUNK
N_HBLK = SEQ // H_BLK
N_HCH = SEQ // H_CHUNK


def _chunk_bcast(x, row):
    c = x.shape[1]
    x3 = x.reshape(H_CPB, H_CHUNK, c)
    return jnp.broadcast_to(x3[:, row:row + 1, :], (H_CPB, H_CHUNK, c)).reshape(H_BLK, c)


def _hgrn_body(p_ref, lb_ref, nw_ref, tin_ref, bd_ref, o_ref, qd_s, ut_s, oi_s, dec_s, oe_s, st_s):
    li = lax.broadcasted_iota(I32, (H_BLK, H_BLK), 0)
    si = lax.broadcasted_iota(I32, (H_BLK, H_BLK), 1)
    same = (li // H_CHUNK) == (si // H_CHUNK)
    mask_f = same & (si <= li)
    mask_b = same & (si >= li)
    bdmask = bd_ref[...]

    def block(bi, _):
        r0 = pl.multiple_of(bi * H_BLK, H_BLK)
        rows = pl.ds(r0, H_BLK)
        q = _silu(p_ref[0, rows, 0:DG].astype(F32))
        v = p_ref[0, rows, 3 * DG:4 * DG]
        oi = jnp.zeros((H_BLK, DG), F32)
        scores = [None] * N_HEADS
        for d in range(2):
            fpre = p_ref[0, rows, (1 + d) * DG:(2 + d) * DG].astype(F32)
            lb = lb_ref[d:d + 1, :]
            sg = jax.nn.sigmoid(fpre)
            g = jnp.log(lb + (1.0 - lb) * sg)
            k = (1.0 - lb) * (1.0 - sg)
            gi = _dot01(tin_ref[...], g)
            glast = _chunk_bcast(gi, H_CHUNK - 1)
            if d == 0:
                gc = gi
                gref = _chunk_bcast(gi, H_CHUNK // 2 - 1)
                msk = mask_f
            else:
                gc = glast - gi + g
                gref = _chunk_bcast(gc, H_CHUNK // 2)
                msk = mask_b
            qe = (q * jnp.exp(gc - gref)).astype(BF16)
            ke = (k * jnp.exp(gref - gc)).astype(BF16)
            for h in range(N_HEADS):
                hs = slice(h * HD, (h + 1) * HD)
                sc = jnp.where(msk, lax.dot_general(qe[:, hs], ke[:, hs], _NT, preferred_element_type=F32), 0.0)
                scores[h] = sc if d == 0 else scores[h] + sc
            qd_s[d, rows, :] = (q * jnp.exp(gc)).astype(BF16)
            kd = (k * jnp.exp(glast - gc)).astype(BF16)
            for j in range(H_CPB):
                cr = slice(j * H_CHUNK, (j + 1) * H_CHUNK)
                ut = lax.dot_general(v[cr, :], kd[cr, :], _TN, preferred_element_type=F32)
                ut_s[d, bi * H_CPB + j] = (ut * bdmask).astype(BF16)
                dec_s[d, bi * H_CPB + j] = jnp.broadcast_to(jnp.exp(glast[j * H_CHUNK:j * H_CHUNK + 1, :]),
                                                             (V7X_SUBLANES, DG))
        oi = jnp.concatenate([_dot(scores[h].astype(BF16), v[:, h * HD:(h + 1) * HD]) for h in range(N_HEADS)],
                             axis=1)
        oi_s[rows, :] = oi
        return 0

    lax.fori_loop(0, N_HBLK, block, 0)

    st_s[...] = jnp.zeros(st_s.shape, F32)

    def step(i, _):
        for d in range(2):
            c = i if d == 0 else N_HCH - 1 - i
            rows = pl.ds(pl.multiple_of(c * H_CHUNK, H_CHUNK), H_CHUNK)
            st = st_s[d]
            oe_s[d, rows, :] = lax.dot_general(qd_s[d, rows, :], st.astype(BF16), _NT,
                                               preferred_element_type=F32)
            st_s[d] = st * dec_s[d, c][0:1, :] + ut_s[d, c].astype(F32)
        return 0

    lax.fori_loop(0, N_HCH, step, 0)

    def finish(c, _):
        r0 = pl.multiple_of(c * CONV_ROWS, CONV_ROWS)
        rows = pl.ds(r0, CONV_ROWS)
        o = oi_s[rows, :] + oe_s[0, rows, :] + oe_s[1, rows, :]
        gate = _silu(p_ref[0, rows, 4 * DG:5 * DG].astype(F32))
        outs = [_rms(o[:, h * HD:(h + 1) * HD]) for h in range(N_HEADS)]
        o_ref[0, rows, :] = (jnp.concatenate(outs, axis=1) * nw_ref[...] * gate).astype(BF16)
        return 0

    lax.fori_loop(0, SEQ // CONV_ROWS, finish, 0)


def hgrn2(p3, lb2, norm_w_lanes, tri_in_chunk, bdmask):
    b = p3.shape[0]
    full = lambda a: pl.BlockSpec(a.shape, lambda i: (0,) * a.ndim)
    return pl.pallas_call(
        _hgrn_body, grid=(b,),
        in_specs=[pl.BlockSpec((1, SEQ, 5 * DG), lambda i: (i, 0, 0)), full(lb2), full(norm_w_lanes),
                  full(tri_in_chunk), full(bdmask)],
        out_specs=pl.BlockSpec((1, SEQ, DG), lambda i: (i, 0, 0)),
        out_shape=jax.ShapeDtypeStruct((b, SEQ, DG), BF16),
        scratch_shapes=[pltpu.VMEM((2, SEQ, DG), BF16),
                        pltpu.VMEM((2, N_HCH, DG, DG), BF16),
                        pltpu.VMEM((SEQ, DG), F32),
                        pltpu.VMEM((2, N_HCH, V7X_SUBLANES, DG), F32),
                        pltpu.VMEM((2, SEQ, DG), F32),
                        pltpu.VMEM((2, DG, DG), F32)],
        compiler_params=_params("parallel"), name="hgrn2",
    )(p3, lb2, norm_w_lanes, tri_in_chunk, bdmask)


@functools.lru_cache(maxsize=None)
def _tables():
    t = {}
    k = np.arange(SEQ, dtype=np.int64)
    ang = 2.0 * np.pi * ((k[:, None] * k[None, :]) % NFFT).astype(np.float64) / NFFT
    t["cos"] = np.cos(ang).astype(np.float32)
    t["sin"] = np.sin(ang).astype(np.float32)
    tt = np.linspace(0.0, 1.0, SEQ, dtype=np.float32)[:, None]
    bands = (HY_POS_DIM - 1) // 2
    ang_pos = (2.0 * math.pi * np.arange(SEQ, dtype=np.float32) / SEQ).astype(np.float32)
    f = np.linspace(1e-4, bands - 1, bands, dtype=np.float32)
    a2 = (ang_pos[:, None] * f[None, :]).astype(np.float32)
    z = np.concatenate([tt, np.cos(a2), -np.sin(a2)], axis=-1).astype(np.float32)
    zp = np.zeros((SEQ, V7X_LANES), np.float32)
    zp[:, :HY_POS_DIM] = z
    t["zpos"] = zp
    max_decay = math.log(1e-2) / 0.3
    min_decay = math.log(1e-2) / 1.5
    deltas = np.abs(np.linspace(min_decay, max_decay, DG, dtype=np.float32))
    t["decay"] = np.exp(-tt * deltas[None, :]).astype(np.float32)
    i128 = np.arange(V7X_LANES)
    t["u128"] = (i128[:, None] < i128[None, :]).astype(np.float32)
    im = np.arange(M_CHUNK)
    t["tri_incl"] = (im[:, None] <= im[None, :]).astype(np.float32)
    ib = np.arange(H_BLK)
    t["tri_in_chunk"] = ((ib[:, None] // H_CHUNK == ib[None, :] // H_CHUNK)
                         & (ib[None, :] <= ib[:, None])).astype(np.float32)
    idg = np.arange(DG)
    t["bdmask"] = (idg[:, None] // HD == idg[None, :] // HD).astype(np.float32)
    def bucket(rel):
        nb = N_BUCKETS // 2
        max_exact = nb // 2
        ret = (rel > 0).astype(np.int64) * nb
        n = np.abs(rel)
        nf = np.maximum(n, 1).astype(np.float64)
        large = max_exact + (np.log(nf / max_exact) / math.log(MAX_DISTANCE / max_exact)
                             * (nb - max_exact)).astype(np.int64)
        large = np.minimum(large, nb - 1)
        return ret + np.where(n < max_exact, n, large)

    for dil in A_DILS:
        n = SEQ // dil
        w = min(n, A_TQ + 2 * A_BAND)
        starts = [0] if n == w else [0, -A_BAND, -(w - A_TQ)]
        qi = np.arange(A_TQ)[:, None]
        kj = np.arange(w)[None, :]
        ids = []
        for s0 in starts:
            rel = kj + s0 - qi
            ids.append(np.where(np.abs(rel) <= A_BAND, bucket(rel * dil), -1))
        t[f"bucket{dil}"] = np.stack(ids).astype(np.int32)
    return t


def kernel(x, w_in, w_out, norm_mix_w, norm_ffn_w, hy_conv_w, hy_pos_w1, hy_pos_b1, hy_pos_w2, hy_pos_b2,
           hy_sin_freq, hy_pos_w3, hy_filt_bias, m_conv_w, m_conv_b, m_dt_bias, m_A_log, m_D, m_norm_w, rel_bias,
           hg_lb, hg_norm_w, router_w, moe_w_gate, moe_w_up, moe_w_down, final_norm_w):
    b = x.shape[0]
    t = b * SEQ
    tb = _tables()
    cos_f32 = jnp.asarray(tb["cos"])
    sin_f32 = jnp.asarray(tb["sin"])
    cos_bf = cos_f32.astype(BF16)
    sin_bf = sin_f32.astype(BF16)
    u128 = jnp.asarray(tb["u128"]).astype(BF16)
    tri_incl = jnp.asarray(tb["tri_incl"]).astype(BF16)
    tri_in_chunk = jnp.asarray(tb["tri_in_chunk"]).astype(BF16)
    bdmask = jnp.asarray(tb["bdmask"])
    attn_bias = [attention_bias_table(jnp.asarray(tb[f"bucket{d}"]), rel_bias.astype(F32)) for d in A_DILS]

    sm = jax.nn.softmax(hg_lb.astype(F32), axis=0)
    lower_bounds = jnp.cumsum(sm, axis=0) - sm[:1]

    xa = x.reshape(t, D_MODEL)
    for l in range(DEPTH):
        wl = w_in[l]
        w_main = jnp.concatenate([wl[:, 0:768], wl[:, 768:1024], wl[:, 1024:1536], wl[:, 1544:2312],
                                  wl[:, 2312:3592]], axis=1).astype(BF16)
        w_dt_rows = wl[:, 1536:1544].T.astype(BF16)
        hy, mz, mx, at, hg, dtc = in_projection(xa, norm_mix_w[l][None, :], w_main, w_dt_rows)

        w1p = jnp.zeros((V7X_LANES, HY_HID), F32).at[:HY_POS_DIM].set(hy_pos_w1[l])
        kr, ki, kny = hyena_filter_spectrum(
            jnp.asarray(tb["zpos"]), w1p, hy_pos_b1[l][None, :], hy_pos_w2[l], hy_pos_b2[l][None, :],
            hy_sin_freq[l][None, :], hy_pos_w3[l], jnp.asarray(tb["decay"]), cos_f32, sin_f32)
        z3, x03 = hyena_prep(hy.reshape(b, SEQ, 3 * DG), hy_conv_w[l])
        ya = hyena_conv(z3, x03, cos_bf, sin_bf, kr, ki, kny, hy_filt_bias[l][None, :]).reshape(t, DG)

        a_col = (-jnp.exp(m_A_log[l].astype(F32))).reshape(8, 1)
        yb = mamba2(mz.reshape(b, SEQ, DG), mx.reshape(b, SEQ, 2 * DG), dtc.reshape(b, N_MCH, 8, MQ),
                    m_conv_w[l], m_conv_b[l][None, :], m_dt_bias[l].reshape(8, 1), a_col,
                    jnp.repeat(m_D[l].astype(F32), HD)[None, :], m_norm_w[l][None, :], tri_incl, bdmask).reshape(t, DG)

        yc = dilated_attention(at.reshape(b, SEQ, 3 * DG), *attn_bias).reshape(t, DG)

        lbl = lower_bounds[l]
        yd = hgrn2(hg.reshape(b, SEQ, 5 * DG), lbl, jnp.tile(hg_norm_w[l], N_HEADS)[None, :],
                   tri_in_chunk, bdmask).reshape(t, DG)

        xo, xn = out_projection(xa, ya, yb, yc, yd, w_out[l].reshape(4, DG, D_MODEL).astype(BF16),
                                norm_ffn_w[l][None, :])
        xn3 = xn.reshape(b, SEQ, D_MODEL)
        rw_rows = router_w[l].T.astype(F32)
        rw_hi = rw_rows.astype(BF16)
        rw_lo = (rw_rows - rw_hi.astype(F32)).astype(BF16)
        rank, gate, seg = router(xo.reshape(b, SEQ, D_MODEL), norm_ffn_w[l][None, :], rw_hi, rw_lo, u128)
        seg_flat = seg[:, :, :MOE_SEG_STRIDE].reshape(-1)
        xe = moe_gather(seg_flat, xn3, rank)
        ye = moe_experts(xe, moe_w_gate, moe_w_up, moe_w_down, l)
        xa = moe_scatter(seg_flat, ye, rank, gate, xo.reshape(b, SEQ, D_MODEL), final_norm_w[None, :],
                         final=(l == DEPTH - 1)).reshape(t, D_MODEL)
    return xa.reshape(b, SEQ, D_MODEL)
```

```python
import functools
import math

import numpy as np
import jax
import jax.numpy as jnp
from jax import lax
from jax.experimental import pallas as pl
from jax.experimental.pallas import tpu as pltpu

F32 = jnp.float32
BF16 = jnp.bfloat16
I32 = jnp.int32

D_MODEL = 1024
SEQ = 2048
DEPTH = 2
DG = 256
N_HEADS = 4
HD = 64
HY_POS_DIM = 33
HY_HID = 64
M_CONV = 5
M_STATE = 64
M_CHUNK = 128
H_CHUNK = 32
A_BAND = 64
A_DILS = (1, 4, 16)
N_BUCKETS = 32
MAX_DISTANCE = 1024
N_EXPERTS = 16
CAP = 2 * SEQ // N_EXPERTS
D_FF = 1024
EPS = 1e-6
NFFT = 2 * SEQ

V7X_LANES = 128
V7X_SUBLANES = 8
V7X_VMEM_LIMIT_BYTES = 56 * 1024 * 1024

NEG_BIG = -1e30

_NT = (((1,), (1,)), ((), ()))
_TN = (((0,), (0,)), ((), ()))


def _params(*sem):
    return pltpu.CompilerParams(dimension_semantics=sem, vmem_limit_bytes=V7X_VMEM_LIMIT_BYTES)


def _dot(a, b):
    return jnp.dot(a, b, preferred_element_type=F32)


def _dot_hi(a, b):
    return jnp.dot(a, b, preferred_element_type=F32, precision=lax.Precision.HIGHEST)


def _dot01(t_bf16, x):
    x1 = x.astype(BF16)
    r1 = x - x1.astype(F32)
    x2 = r1.astype(BF16)
    x3 = (r1 - x2.astype(F32)).astype(BF16)
    return _dot(t_bf16, x1) + _dot(t_bf16, x2) + _dot(t_bf16, x3)


def _dot01_rhs(x, t_bf16):
    x1 = x.astype(BF16)
    r1 = x - x1.astype(F32)
    x2 = r1.astype(BF16)
    x3 = (r1 - x2.astype(F32)).astype(BF16)
    return _dot(x1, t_bf16) + _dot(x2, t_bf16) + _dot(x3, t_bf16)


def _silu(x):
    return x * jax.nn.sigmoid(x)


def _softplus(x):
    return jnp.maximum(x, 0.0) + jnp.log(1.0 + jnp.exp(-jnp.abs(x)))


def _rms(x):
    return x * lax.rsqrt(jnp.mean(x * x, axis=-1, keepdims=True) + EPS)


TM_PROJ = 1024
_HY0, _MZ0, _MX0, _AT0, _HG0, _PEND = 0, 768, 1024, 1536, 2304, 3584


def _inproj_body(x_ref, nw_ref, w_ref, wdt_ref, hy_ref, mz_ref, mx_ref, at_ref, hg_ref, dtc_ref):
    x = x_ref[...]
    hn = (_rms(x) * nw_ref[...]).astype(BF16)
    hy_ref[...] = _dot(hn, w_ref[:, _HY0:_MZ0]).astype(BF16)
    mz_ref[...] = _dot(hn, w_ref[:, _MZ0:_MX0]).astype(BF16)
    mx_ref[...] = _dot(hn, w_ref[:, _MX0:_AT0]).astype(BF16)
    at_ref[...] = _dot(hn, w_ref[:, _AT0:_HG0]).astype(BF16)
    hg_ref[...] = _dot(hn, w_ref[:, _HG0:_PEND]).astype(BF16)
    dt_rows = lax.dot_general(wdt_ref[...], hn, _NT, preferred_element_type=F32)
    for j in range(TM_PROJ // M_CHUNK):
        dtc_ref[j] = dt_rows[:, j * M_CHUNK:(j + 1) * M_CHUNK]


def in_projection(x, norm_w, w_main, w_dt_rows):
    t = x.shape[0]
    tm = TM_PROJ
    row = lambda w: pl.BlockSpec((tm, w), lambda i: (i, 0))
    full = lambda a: pl.BlockSpec(a.shape, lambda i: (0,) * a.ndim)
    in_specs = [row(D_MODEL), full(norm_w), full(w_main), full(w_dt_rows)]
    widths = (768, 256, 512, 768, 1280)
    out_shape = [jax.ShapeDtypeStruct((t, w), BF16) for w in widths]
    out_shape.append(jax.ShapeDtypeStruct((t // M_CHUNK, 8, M_CHUNK), F32))
    out_specs = [row(w) for w in widths] + [pl.BlockSpec((tm // M_CHUNK, 8, M_CHUNK), lambda i: (i, 0, 0))]
    return pl.pallas_call(
        _inproj_body, grid=(t // tm,), in_specs=in_specs, out_specs=out_specs, out_shape=out_shape,
        compiler_params=_params("parallel"), name="in_projection",
    )(x, norm_w, w_main, w_dt_rows)


TM_OUT = 512


def _outproj_body(x_ref, ya_ref, yb_ref, yc_ref, yd_ref, w_ref, nw_ref, xo_ref, xn_ref):
    x = x_ref[...]
    acc = x + _dot(ya_ref[...], w_ref[0]) + _dot(yb_ref[...], w_ref[1])
    acc = acc + _dot(yc_ref[...], w_ref[2]) + _dot(yd_ref[...], w_ref[3])
    xo_ref[...] = acc
    xn_ref[...] = (_rms(acc) * nw_ref[...]).astype(BF16)


def out_projection(x, ya, yb, yc, yd, w_out4, norm_w):
    t = x.shape[0]
    tm = TM_OUT
    row = lambda w: pl.BlockSpec((tm, w), lambda i: (i, 0))
    full = lambda a: pl.BlockSpec(a.shape, lambda i: (0,) * a.ndim)
    in_specs = [row(D_MODEL)] + [row(DG)] * 4 + [full(w_out4), full(norm_w)]
    return pl.pallas_call(
        _outproj_body, grid=(t // tm,), in_specs=in_specs,
        out_specs=[row(D_MODEL), row(D_MODEL)],
        out_shape=[jax.ShapeDtypeStruct((t, D_MODEL), F32), jax.ShapeDtypeStruct((t, D_MODEL), BF16)],
        compiler_params=_params("parallel"), name="out_projection",
    )(x, ya, yb, yc, yd, w_out4, norm_w)


def _prefix_excl_lanes(mask_f32, u_ref):
    e = mask_f32.shape[0]
    off = jnp.zeros((e, 1), F32)
    parts, bounds = [], [off]
    for k in range(SEQ // V7X_LANES):
        tile = mask_f32[:, k * V7X_LANES:(k + 1) * V7X_LANES]
        parts.append(_dot(tile.astype(BF16), u_ref[...]) + off)
        off = off + jnp.sum(tile, axis=1, keepdims=True)
        bounds.append(off)
    return jnp.concatenate(parts, axis=1), bounds


ROUTER_SEQS = 2


def _router_body(xo_ref, nw_ref, rwh_ref, rwl_ref, u_ref, rank_ref, gate_ref, seg_ref):
    nt = lambda w, a: lax.dot_general(w, a, _NT, preferred_element_type=F32)
    affs = []
    for q in range(ROUTER_SEQS):
        xn = _rms(xo_ref[q]) * nw_ref[...]
        xh = xn.astype(BF16)
        xl = (xn - xh.astype(F32)).astype(BF16)
        logits = nt(rwh_ref[...], xh) + nt(rwh_ref[...], xl) + nt(rwl_ref[...], xh)
        ex = jnp.exp(logits - jnp.max(logits, axis=0, keepdims=True))
        affs.append(ex / jnp.sum(ex, axis=0, keepdims=True))
    aff = jnp.concatenate(affs, axis=0)
    nrow = ROUTER_SEQS * N_EXPERTS
    bits = pltpu.bitcast(aff, I32)

    def search(i, thr):
        cand = thr | jnp.left_shift(jnp.int32(1), 30 - i)
        cnt = jnp.sum((bits >= cand).astype(I32), axis=1, keepdims=True)
        return jnp.where(cnt >= CAP, cand, thr)

    thr = lax.fori_loop(0, 31, search, jnp.zeros((nrow, 1), I32))
    gt = (bits > thr).astype(F32)
    eq = (bits == thr).astype(F32)
    need = CAP - jnp.sum(gt, axis=1, keepdims=True)
    tie_rank, _ = _prefix_excl_lanes(eq, u_ref)
    sel = gt + eq * (tie_rank < need).astype(F32)
    rank, bounds = _prefix_excl_lanes(sel, u_ref)
    rank = jnp.where(sel > 0.0, rank, -1.0)
    lane = lax.broadcasted_iota(I32, (nrow, V7X_LANES), 1)
    seg = jnp.zeros((nrow, V7X_LANES), F32)
    for sgm in range(N_MOE_SEG + 1):
        seg = jnp.where(lane == sgm, bounds[sgm * (MOE_SEG // V7X_LANES)], seg)
    seg = seg.astype(I32)
    for q in range(ROUTER_SEQS):
        rows = slice(q * N_EXPERTS, (q + 1) * N_EXPERTS)
        rank_ref[q] = rank[rows]
        gate_ref[q] = aff[rows]
        seg_ref[q] = seg[rows]


def router(xo3, norm_w, rw_hi, rw_lo, u128):
    b = xo3.shape[0]
    out = jax.ShapeDtypeStruct((b, N_EXPERTS, SEQ), F32)
    full = lambda a: pl.BlockSpec(a.shape, lambda i: (0,) * a.ndim)
    return pl.pallas_call(
        _router_body, grid=(b // ROUTER_SEQS,),
        in_specs=[pl.BlockSpec((ROUTER_SEQS, SEQ, D_MODEL), lambda i: (i, 0, 0)), full(norm_w), full(rw_hi),
                  full(rw_lo), full(u128)],
        out_specs=[pl.BlockSpec((ROUTER_SEQS, N_EXPERTS, SEQ), lambda i: (i, 0, 0))] * 2
                  + [pl.BlockSpec((ROUTER_SEQS, N_EXPERTS, V7X_LANES), lambda i: (i, 0, 0))],
        out_shape=[out, out, jax.ShapeDtypeStruct((b, N_EXPERTS, V7X_LANES), I32)],
        compiler_params=_params("parallel"), name="router",
    )(xo3, norm_w, rw_hi, rw_lo, u128)


MOE_SEG = 256
N_MOE_SEG = SEQ // MOE_SEG
MOE_TILE = 64
MOE_ALIGN = 16
MOE_GROUP = 4
MOE_SEG_STRIDE = 16
MOE_FFN_SEQS = 4


def _moe_seg_plan(cs_ref, b, s):
    starts, rounds = [], jnp.int32(0)
    for ex in range(N_EXPERTS):
        base = (b * N_EXPERTS + ex) * MOE_SEG_STRIDE
        first = (cs_ref[base + s] // MOE_ALIGN) * MOE_ALIGN
        span = cs_ref[base + s + 1] - first
        starts.append(first)
        rounds = jnp.maximum(rounds, (span + MOE_TILE - 1) // MOE_TILE)
    return starts, rounds


def _moe_tile_bases(starts, r):
    own = [st + r * MOE_TILE for st in starts]
    return [pl.multiple_of(jnp.minimum(o, CAP - MOE_TILE), MOE_ALIGN) for o in own], own


def _moe_onehot_group(rank_ref, gate_ref, s, bases, own, grp):
    lanes = pl.ds(pl.multiple_of(s * MOE_SEG, MOE_SEG), MOE_SEG)
    j = lax.broadcasted_iota(I32, (MOE_TILE, MOE_SEG), 0)
    rows = []
    for ex in grp:
        slot = bases[ex] + j
        hit = (rank_ref[0, ex:ex + 1, lanes] == slot.astype(F32)) & (slot >= own[ex])
        val = 1.0 if gate_ref is None else gate_ref[0, ex:ex + 1, lanes]
        rows.append(jnp.where(hit, val, 0.0).astype(BF16))
    return jnp.concatenate(rows, axis=0)


_MOE_GROUPS = [list(range(g * MOE_GROUP, (g + 1) * MOE_GROUP)) for g in range(N_EXPERTS // MOE_GROUP)]


def _moe_gather_body(cs_ref, xn_ref, rank_ref, xe_ref):
    b = pl.program_id(0)

    def zero(ex, _):
        xe_ref[0, ex] = jnp.zeros((CAP, D_MODEL), BF16)
        return 0

    lax.fori_loop(0, N_EXPERTS, zero, 0)

    def seg_gather(s, _):
        starts, rounds = _moe_seg_plan(cs_ref, b, s)
        xn_seg = xn_ref[0, pl.ds(pl.multiple_of(s * MOE_SEG, MOE_SEG), MOE_SEG), :]

        def one_round(r, _):
            bases, own = _moe_tile_bases(starts, r)
            for grp in _MOE_GROUPS:
                got = _dot(_moe_onehot_group(rank_ref, None, s, bases, own, grp), xn_seg)
                for k, ex in enumerate(grp):
                    rows = pl.ds(bases[ex], MOE_TILE)
                    old = xe_ref[0, ex, rows, :].astype(F32)
                    xe_ref[0, ex, rows, :] = (old + got[k * MOE_TILE:(k + 1) * MOE_TILE]).astype(BF16)
            return 0

        lax.fori_loop(0, rounds, one_round, 0)
        return 0

    lax.fori_loop(0, N_MOE_SEG, seg_gather, 0)


def moe_gather(seg_counts_flat, xn3, rank3):
    b = xn3.shape[0]
    grid_spec = pltpu.PrefetchScalarGridSpec(
        num_scalar_prefetch=1, grid=(b,),
        in_specs=[pl.BlockSpec((1, SEQ, D_MODEL), lambda i, cs: (i, 0, 0)),
                  pl.BlockSpec((1, N_EXPERTS, SEQ), lambda i, cs: (i, 0, 0))],
        out_specs=pl.BlockSpec((1, N_EXPERTS, CAP, D_MODEL), lambda i, cs: (i, 0, 0, 0)))
    return pl.pallas_call(
        _moe_gather_body, grid_spec=grid_spec,
        out_shape=jax.ShapeDtypeStruct((b, N_EXPERTS, CAP, D_MODEL), BF16),
        compiler_params=_params("parallel"), name="moe_gather",
    )(seg_counts_flat, xn3, rank3)


def _moe_experts_body(xe_ref, wg_ref, wu_ref, wd_ref, ye_ref, wg_s, wu_s, wd_s):
    @pl.when(pl.program_id(1) == 0)
    def _():
        wg_s[...] = wg_ref[0, 0].astype(BF16)
        wu_s[...] = wu_ref[0, 0].astype(BF16)
        wd_s[...] = wd_ref[0, 0].astype(BF16)

    xe = xe_ref[...].reshape(MOE_FFN_SEQS * CAP, D_MODEL)
    hid = (_silu(_dot(xe, wg_s[...])) * _dot(xe, wu_s[...])).astype(BF16)
    ye_ref[...] = _dot(hid, wd_s[...]).astype(BF16).reshape(MOE_FFN_SEQS, 1, CAP, D_MODEL)


def moe_experts(xe4, w_gate, w_up, w_down, layer):
    b = xe4.shape[0]
    blk = pl.BlockSpec((MOE_FFN_SEQS, 1, CAP, D_MODEL), lambda e, g: (g, e, 0, 0))
    w_spec = lambda a: pl.BlockSpec((1, 1) + a.shape[2:], lambda e, g: (layer, e, 0, 0))
    return pl.pallas_call(
        _moe_experts_body, grid=(N_EXPERTS, b // MOE_FFN_SEQS),
        in_specs=[blk, w_spec(w_gate), w_spec(w_up), w_spec(w_down)],
        out_specs=blk, out_shape=jax.ShapeDtypeStruct(xe4.shape, BF16),
        scratch_shapes=[pltpu.VMEM((D_MODEL, D_FF), BF16), pltpu.VMEM((D_MODEL, D_FF), BF16),
                        pltpu.VMEM((D_FF, D_MODEL), BF16)],
        compiler_params=_params("parallel", "arbitrary"), name="moe_experts",
    )(xe4, w_gate, w_up, w_down)


MOE_SCATTER_SEGS = 4


def _moe_scatter_body(final, cs_ref, ye_ref, rank_ref, gate_ref, xo_ref, nw_ref, o_ref):
    b = pl.program_id(0)
    half = pl.program_id(1)

    def seg_scatter(k, _):
        s = half * MOE_SCATTER_SEGS + k
        starts, rounds = _moe_seg_plan(cs_ref, b, s)
        tok = pl.ds(pl.multiple_of(k * MOE_SEG, MOE_SEG), MOE_SEG)
        o_ref[0, tok, :] = xo_ref[0, tok, :]

        def one_round(r, _):
            bases, own = _moe_tile_bases(starts, r)
            for grp in _MOE_GROUPS:
                ye = jnp.concatenate([ye_ref[0, ex, pl.ds(bases[ex], MOE_TILE), :] for ex in grp], axis=0)
                o_ref[0, tok, :] += lax.dot_general(_moe_onehot_group(rank_ref, gate_ref, s, bases, own, grp), ye,
                                                    _TN, preferred_element_type=F32)
            return 0

        lax.fori_loop(0, rounds, one_round, 0)
        if final:
            o_ref[0, tok, :] = _rms(o_ref[0, tok, :]) * nw_ref[...]
        return 0

    lax.fori_loop(0, MOE_SCATTER_SEGS, seg_scatter, 0)


def moe_scatter(seg_counts_flat, ye4, rank3, gate3, xo3, final_norm_w, final):
    b = ye4.shape[0]
    rows = MOE_SCATTER_SEGS * MOE_SEG
    sel_spec = pl.BlockSpec((1, N_EXPERTS, SEQ), lambda i, j, cs: (i, 0, 0))
    tok_spec = pl.BlockSpec((1, rows, D_MODEL), lambda i, j, cs: (i, j, 0))
    grid_spec = pltpu.PrefetchScalarGridSpec(
        num_scalar_prefetch=1, grid=(b, SEQ // rows),
        in_specs=[pl.BlockSpec((1, N_EXPERTS, CAP, D_MODEL), lambda i, j, cs: (i, 0, 0, 0)), sel_spec, sel_spec,
                  tok_spec, pl.BlockSpec(final_norm_w.shape, lambda i, j, cs: (0, 0))],
        out_specs=tok_spec)
    return pl.pallas_call(
        functools.partial(_moe_scatter_body, final), grid_spec=grid_spec,
        out_shape=jax.ShapeDtypeStruct((b, SEQ, D_MODEL), F32),
        compiler_params=_params("parallel", "arbitrary"), name="moe_scatter",
    )(seg_counts_flat, ye4, rank3, gate3, xo3, final_norm_w)


HY_KB = 256
HY_ROWS = 256


def _hy_filter_body(z_ref, w1_ref, b1_ref, w2_ref, b2_ref, fr_ref, w3_ref, dec_ref, c_ref, s_ref,
                    kr_ref, ki_ref, kny_ref, a_s, d_s):
    @pl.when(pl.program_id(0) == 0)
    def _():
        def rows(c, kny):
            r0 = pl.multiple_of(c * HY_ROWS, HY_ROWS)
            fr = fr_ref[...]
            h = jnp.sin(fr * (_dot_hi(z_ref[pl.ds(r0, HY_ROWS), :], w1_ref[...]) + b1_ref[...]))
            h = jnp.sin(fr * (_dot_hi(h, w2_ref[...]) + b2_ref[...]))
            h = _dot_hi(h, w3_ref[...])
            dec = dec_ref[pl.ds(r0, HY_ROWS), :]
            pos = r0 + lax.broadcasted_iota(I32, (HY_ROWS, DG), 0)
            hf = h[:, :DG] * dec
            hb = jnp.where(pos == 0, 0.0, h[:, DG:] * dec)
            a = hf + hb
            a_s[pl.ds(r0, HY_ROWS), :] = a
            d_s[pl.ds(r0, HY_ROWS), :] = hf - hb
            sgn = (1 - 2 * (pos & 1)).astype(F32)
            return kny + jnp.sum(a * sgn, axis=0, keepdims=True)

        kny = lax.fori_loop(0, SEQ // HY_ROWS, rows, jnp.zeros((1, DG), F32))
        kny_ref[...] = jnp.broadcast_to(kny, kny_ref.shape)

    kr_ref[...] = _dot_hi(c_ref[...], a_s[...])
    ki_ref[...] = _dot_hi(s_ref[...], d_s[...])


def hyena_filter_spectrum(zpos, w1, b1, w2, b2, freq, w3, decay, cos_f32, sin_f32):
    full = lambda a: pl.BlockSpec(a.shape, lambda k: (0,) * a.ndim)
    kblk = pl.BlockSpec((HY_KB, SEQ), lambda k: (k, 0))
    oblk = pl.BlockSpec((HY_KB, DG), lambda k: (k, 0))
    return pl.pallas_call(
        _hy_filter_body, grid=(SEQ // HY_KB,),
        in_specs=[full(zpos), full(w1), full(b1), full(w2), full(b2), full(freq), full(w3), full(decay), kblk, kblk],
        out_specs=[oblk, oblk, pl.BlockSpec((V7X_SUBLANES, DG), lambda k: (0, 0))],
        out_shape=[jax.ShapeDtypeStruct((SEQ, DG), F32), jax.ShapeDtypeStruct((SEQ, DG), F32),
                   jax.ShapeDtypeStruct((V7X_SUBLANES, DG), F32)],
        scratch_shapes=[pltpu.VMEM((SEQ, DG), F32), pltpu.VMEM((SEQ, DG), F32)],
        compiler_params=_params("arbitrary"), name="hyena_filter",
    )(zpos, w1, b1, w2, b2, freq, w3, decay, cos_f32, sin_f32)


CONV_ROWS = 128
CONV_HALO = 8


def _dwconv_rows(pad_ref, w_ref, r0, lanes, k):
    n = CONV_ROWS + 2 * CONV_HALO
    win = pad_ref[pl.ds(r0, n), lanes]
    acc = None
    for j in range(k):
        sh = (k // 2 - j) % n
        rolled = win if sh == 0 else pltpu.roll(win, sh, 0)
        term = rolled[CONV_HALO:CONV_HALO + CONV_ROWS] * w_ref[j:j + 1, lanes]
        acc = term if acc is None else acc + term
    return acc


def _fill_padded(pad_ref, src_ref, width):
    zeros = jnp.zeros((CONV_HALO, width), F32)
    pad_ref[pl.ds(0, CONV_HALO), :] = zeros
    pad_ref[pl.ds(SEQ + CONV_HALO, CONV_HALO), :] = zeros

    def fill(c, _):
        r0 = pl.multiple_of(c * CONV_ROWS, CONV_ROWS)
        pad_ref[pl.ds(r0 + CONV_HALO, CONV_ROWS), :] = src_ref[0, pl.ds(r0, CONV_ROWS), :].astype(F32)
        return 0

    lax.fori_loop(0, SEQ // CONV_ROWS, fill, 0)


def _hy_prep_body(p_ref, w_ref, z_ref, x0_ref, pad):
    _fill_padded(pad, p_ref, 3 * DG)

    def rows(c, _):
        r0 = pl.multiple_of(c * CONV_ROWS, CONV_ROWS)
        x0 = _dwconv_rows(pad, w_ref, r0, slice(0, DG), 3)
        x1 = _dwconv_rows(pad, w_ref, r0, slice(DG, 2 * DG), 3)
        v = _dwconv_rows(pad, w_ref, r0, slice(2 * DG, 3 * DG), 3)
        x0_ref[0, pl.ds(r0, CONV_ROWS), :] = x0.astype(BF16)
        z_ref[0, pl.ds(r0, CONV_ROWS), :] = (v * x1).astype(BF16)
        return 0

    lax.fori_loop(0, SEQ // CONV_ROWS, rows, 0)


def hyena_prep(p3, conv_w):
    b = p3.shape[0]
    blk = pl.BlockSpec((1, SEQ, DG), lambda i: (i, 0, 0))
    out = jax.ShapeDtypeStruct((b, SEQ, DG), BF16)
    return pl.pallas_call(
        _hy_prep_body, grid=(b,),
        in_specs=[pl.BlockSpec((1, SEQ, 3 * DG), lambda i: (i, 0, 0)), pl.BlockSpec(conv_w.shape, lambda i: (0, 0))],
        out_specs=[blk, blk], out_shape=[out, out],
        scratch_shapes=[pltpu.VMEM((SEQ + 2 * CONV_HALO, 3 * DG), F32)],
        compiler_params=_params("parallel"), name="hyena_prep",
    )(p3, conv_w)


HY_G = 2
HY_FB = 512


def _hy_conv_body(z_ref, x0_ref, cr_ref, sr_ref, cc_ref, sc_ref, kr_ref, ki_ref, kny_ref, fb_ref, o_ref, acc):
    kb = pl.program_id(1)
    krow = kb * HY_FB + lax.broadcasted_iota(I32, (HY_FB, 1), 0)
    wk = jnp.where(krow == 0, 1.0 / NFFT, 2.0 / NFFT)
    kr = kr_ref[...]
    ki = ki_ref[...]

    @pl.when(kb == 0)
    def _():
        acc[...] = jnp.zeros(acc.shape, F32)

    for g in range(HY_G):
        z = z_ref[g]
        zr = _dot(cr_ref[...], z)
        zi = _dot(sr_ref[...], z)
        yr = ((zr * kr - zi * ki) * wk).astype(BF16)
        yi = ((zr * ki + zi * kr) * wk).astype(BF16)
        acc[g] += _dot(cc_ref[...], yr) + _dot(sc_ref[...], yi)

    @pl.when(kb == pl.num_programs(1) - 1)
    def _():
        sgn = (1 - 2 * (lax.broadcasted_iota(I32, (SEQ, DG), 0) & 1)).astype(F32)
        for g in range(HY_G):
            zf = z_ref[g].astype(F32)
            zny = jnp.sum(zf * sgn, axis=0, keepdims=True)
            conv = acc[g] + (zny * kny_ref[0:1, :] * (1.0 / NFFT)) * sgn
            o_ref[g] = (x0_ref[g].astype(F32) * (conv + zf * fb_ref[...])).astype(BF16)


def hyena_conv(z3, x03, cos_bf, sin_bf, kr, ki, kny, fbias):
    b = z3.shape[0]
    seq_blk = pl.BlockSpec((HY_G, SEQ, DG), lambda i, k: (i, 0, 0))
    rows = pl.BlockSpec((HY_FB, SEQ), lambda i, k: (k, 0))
    cols = pl.BlockSpec((SEQ, HY_FB), lambda i, k: (0, k))
    kblk = pl.BlockSpec((HY_FB, DG), lambda i, k: (k, 0))
    return pl.pallas_call(
        _hy_conv_body, grid=(b // HY_G, SEQ // HY_FB),
        in_specs=[seq_blk, seq_blk, rows, rows, cols, cols, kblk, kblk,
                  pl.BlockSpec(kny.shape, lambda i, k: (0, 0)), pl.BlockSpec(fbias.shape, lambda i, k: (0, 0))],
        out_specs=seq_blk, out_shape=jax.ShapeDtypeStruct((b, SEQ, DG), BF16),
        scratch_shapes=[pltpu.VMEM((HY_G, SEQ, DG), F32)],
        compiler_params=_params("parallel", "arbitrary"), name="hyena_conv",
    )(z3, x03, cos_bf, sin_bf, cos_bf, sin_bf, kr, ki, kny, fbias)


N_MCH = SEQ // M_CHUNK
MQ = M_CHUNK


def _head_lane_vec(rows8, base):
    lane_head = lax.broadcasted_iota(I32, (1, DG), 1) // HD
    out = jnp.zeros((1, DG), F32)
    for h in range(N_HEADS):
        out = jnp.where(lane_head == h, rows8[base + h:base + h + 1, :], out)
    return out


def _mamba_body(z_ref, xbc_ref, dtc_ref, cw_ref, cb_ref, dtb_ref, a_ref, dsk_ref, nw_ref, tri_ref, bd_ref,
                o_ref, pad, xs_s, b_s, c_s, y_s, u_s, dec_s, cw_s, yo_s, st_s):
    _fill_padded(pad, xbc_ref, 2 * DG)

    def conv_rows(c, _):
        r0 = pl.multiple_of(c * CONV_ROWS, CONV_ROWS)
        for g in range(4):
            lanes = slice(g * V7X_LANES, (g + 1) * V7X_LANES)
            u = _silu(_dwconv_rows(pad, cw_ref, r0, lanes, M_CONV) + cb_ref[:, lanes])
            if g < 2:
                xs_s[pl.ds(r0, CONV_ROWS), lanes] = u
            elif g == 2:
                b_s[pl.ds(r0, CONV_ROWS), :] = u.astype(BF16)
            else:
                c_s[pl.ds(r0, CONV_ROWS), :] = u.astype(BF16)
        return 0

    lax.fori_loop(0, SEQ // CONV_ROWS, conv_rows, 0)

    li = lax.broadcasted_iota(I32, (MQ, MQ), 0)
    si = lax.broadcasted_iota(I32, (MQ, MQ), 1)
    lower = si <= li
    upper = si >= li
    bdmask = bd_ref[...]

    def chunk(c, _):
        r0 = pl.multiple_of(c * MQ, MQ)
        dt = _softplus(dtc_ref[0, c] + dtb_ref[...])
        a = dt * a_ref[...]
        cum = _dot01_rhs(a, tri_ref[...])
        tot = cum[:, MQ - 1:MQ]
        suf = tot - cum + a
        row_dir = lax.broadcasted_iota(I32, (8, MQ), 0) // N_HEADS
        seg = jnp.where(row_dir == 0, cum, suf)
        wgt = jnp.exp(tot - seg) * dt
        cols = jnp.concatenate([seg, jnp.exp(seg)], axis=0).T
        x = xs_s[pl.ds(r0, MQ), :]
        xb = x.astype(BF16)
        bm = b_s[pl.ds(r0, MQ), :]
        cm = c_s[pl.ds(r0, MQ), :]
        cmf = cm.astype(F32)
        bt = bm.astype(F32).T
        ydiag = []
        for h in range(N_HEADS):
            g = h // 2
            cb = lax.dot_general(cm[:, g * M_STATE:(g + 1) * M_STATE], bm[:, g * M_STATE:(g + 1) * M_STATE],
                                 _NT, preferred_element_type=F32)
            lf = jnp.where(lower, jnp.exp(jnp.minimum(cols[:, h:h + 1] - seg[h:h + 1, :], 0.0)), 0.0)
            lb = jnp.where(upper, jnp.exp(jnp.minimum(cols[:, 4 + h:5 + h] - seg[4 + h:5 + h, :], 0.0)), 0.0)
            m = cb * (lf * dt[h:h + 1, :] + lb * dt[4 + h:5 + h, :])
            ydiag.append(_dot(m.astype(BF16), xb[:, h * HD:(h + 1) * HD]))
        y_s[pl.ds(r0, MQ), :] = jnp.concatenate(ydiag, axis=1)
        for d in range(2):
            bwt = jnp.concatenate([bt[(h // 2) * M_STATE:(h // 2 + 1) * M_STATE, :] * wgt[4 * d + h:4 * d + h + 1, :]
                                   for h in range(N_HEADS)], axis=0)
            u_s[d, c] = (_dot(bwt.astype(BF16), xb) * bdmask).astype(BF16)
            dec_s[d, c] = jnp.broadcast_to(_head_lane_vec(jnp.exp(tot), 4 * d), (V7X_SUBLANES, DG))
            cw_s[d, c] = jnp.concatenate(
                [cmf[:, (h // 2) * M_STATE:(h // 2 + 1) * M_STATE] * cols[:, 8 + 4 * d + h:9 + 4 * d + h]
                 for h in range(N_HEADS)], axis=1).astype(BF16)
        return 0

    lax.fori_loop(0, N_MCH, chunk, 0, unroll=2)

    st_s[...] = jnp.zeros(st_s.shape, F32)

    def scan(i, _):
        for d in range(2):
            c = i if d == 0 else N_MCH - 1 - i
            st = st_s[d]
            yo_s[d, pl.ds(pl.multiple_of(c * MQ, MQ), MQ), :] = _dot(cw_s[d, c], st.astype(BF16))
            st_s[d] = st * dec_s[d, c][0:1, :] + u_s[d, c].astype(F32)
        return 0

    lax.fori_loop(0, N_MCH, scan, 0)

    def finish(c, _):
        r0 = pl.multiple_of(c * CONV_ROWS, CONV_ROWS)
        rows = pl.ds(r0, CONV_ROWS)
        y = y_s[rows, :] + yo_s[0, rows, :] + yo_s[1, rows, :] + xs_s[rows, :] * dsk_ref[...]
        y = y * _silu(z_ref[0, rows, :].astype(F32))
        o_ref[0, pl.ds(r0, CONV_ROWS), :] = (_rms(y) * nw_ref[...]).astype(BF16)
        return 0

    lax.fori_loop(0, SEQ // CONV_ROWS, finish, 0)


def mamba2(z3, xbc3, dtc4, conv_w, conv_b, dt_bias_col, a_col, dskip_lanes, norm_w, tri_incl, bdmask):
    b = z3.shape[0]
    full = lambda a: pl.BlockSpec(a.shape, lambda i: (0,) * a.ndim)
    return pl.pallas_call(
        _mamba_body, grid=(b,),
        in_specs=[pl.BlockSpec((1, SEQ, DG), lambda i: (i, 0, 0)),
                  pl.BlockSpec((1, SEQ, 2 * DG), lambda i: (i, 0, 0)),
                  pl.BlockSpec((1, N_MCH, 8, MQ), lambda i: (i, 0, 0, 0)),
                  full(conv_w), full(conv_b), full(dt_bias_col), full(a_col), full(dskip_lanes), full(norm_w),
                  full(tri_incl), full(bdmask)],
        out_specs=pl.BlockSpec((1, SEQ, DG), lambda i: (i, 0, 0)),
        out_shape=jax.ShapeDtypeStruct((b, SEQ, DG), BF16),
        scratch_shapes=[pltpu.VMEM((SEQ + 2 * CONV_HALO, 2 * DG), F32),
                        pltpu.VMEM((SEQ, DG), F32),
                        pltpu.VMEM((SEQ, 2 * M_STATE), BF16),
                        pltpu.VMEM((SEQ, 2 * M_STATE), BF16),
                        pltpu.VMEM((SEQ, DG), F32),
                        pltpu.VMEM((2, N_MCH, DG, DG), BF16),
                        pltpu.VMEM((2, N_MCH, V7X_SUBLANES, DG), F32),
                        pltpu.VMEM((2, N_MCH, MQ, DG), BF16),
                        pltpu.VMEM((2, SEQ, DG), F32),
                        pltpu.VMEM((2, DG, DG), F32)],
        compiler_params=_params("parallel"), name="mamba2",
    )(z3, xbc3, dtc4, conv_w, conv_b, dt_bias_col, a_col, dskip_lanes, norm_w, tri_incl, bdmask)


A_TQ = 128
A_ROWS = 256
A_KW = A_TQ + 2 * A_BAND


def _attn_bias_body(ids_ref, rb_ref, o_ref):
    ids = ids_ref[0]
    for h in range(N_HEADS):
        acc = jnp.full(ids.shape, NEG_BIG, F32)
        for bkt in range(N_BUCKETS):
            acc = jnp.where(ids == bkt, rb_ref[bkt, h], acc)
        o_ref[h, 0] = acc


def attention_bias_table(bucket_ids, rel_bias):
    nvar, tq, w = bucket_ids.shape
    return pl.pallas_call(
        _attn_bias_body, grid=(nvar,),
        in_specs=[pl.BlockSpec((1, tq, w), lambda v: (v, 0, 0)),
                  pl.BlockSpec(memory_space=pltpu.SMEM)],
        out_specs=pl.BlockSpec((N_HEADS, 1, tq, w), lambda v: (0, v, 0, 0)),
        out_shape=jax.ShapeDtypeStruct((N_HEADS, nvar, tq, w), F32),
        compiler_params=_params("parallel"), name="attention_bias_table",
    )(bucket_ids, rel_bias)


A_SLABS = 3 * DG // V7X_LANES
A_QBLOCKS = SEQ // A_TQ


A_SUB4 = SEQ // 4
A_SUB16 = SEQ // 16


def _attn_body(at_ref, b1_ref, b4_ref, b16_ref, o_ref, qkv_s, x4_s, x16_s, y16_s, y4_s, part_o, part_l):
    def fill(c, _):
        r0 = pl.multiple_of(c * A_ROWS, A_ROWS)
        for s in range(A_SLABS):
            qkv_s[s, pl.ds(r0, A_ROWS), :] = at_ref[0, pl.ds(r0, A_ROWS), s * V7X_LANES:(s + 1) * V7X_LANES].astype(F32)
        return 0

    lax.fori_loop(0, SEQ // A_ROWS, fill, 0)

    def deinterleave(s, _):
        for r4 in range(4):
            for c in range(A_SUB4 // A_ROWS):
                x4_s[s, pl.ds(r4 * A_SUB4 + c * A_ROWS, A_ROWS), :] = \
                    qkv_s[s, pl.ds(r4 + 4 * c * A_ROWS, A_ROWS, stride=4), :]
        for r in range(16):
            x16_s[s, pl.ds(r * A_SUB16, A_SUB16), :] = \
                x4_s[s, pl.ds((r % 4) * A_SUB4 + r // 4, A_SUB16, stride=4), :].astype(BF16)
        return 0

    lax.fori_loop(0, A_SLABS, deinterleave, 0)
    first_head = lax.broadcasted_iota(I32, (A_TQ, V7X_LANES), 1) < HD

    def run_pattern(pat, dil, bias_ref):
        n = SEQ // dil if dil < 16 else SEQ
        nblk = n // A_TQ
        w = A_KW

        def block(it, _):
            r = it // nblk
            i = it - r * nblk
            q0 = i * A_TQ
            k0 = jnp.clip(q0 - A_BAND, 0, n - w)
            var = jnp.where(i == 0, 0, jnp.where(i == nblk - 1, 2, 1))
            for hp in range(2):
                lanes = [slice((2 * part + hp) * V7X_LANES, (2 * part + hp + 1) * V7X_LANES) for part in range(3)]
                if dil == 4:
                    qrows = pl.ds(r + dil * q0, A_TQ, stride=dil)
                    krows = pl.ds(r + dil * k0, w, stride=dil)
                    q2 = qkv_s[hp, qrows, :]
                    k2 = qkv_s[2 + hp, krows, :].astype(BF16)
                    v2 = qkv_s[4 + hp, krows, :].astype(BF16)
                else:
                    qrows = pl.ds(pl.multiple_of(q0, A_TQ), A_TQ)
                    krows = pl.ds(pl.multiple_of(k0, A_BAND), w)
                    if dil == 1:
                        q2, k2, v2 = at_ref[0, qrows, lanes[0]], at_ref[0, krows, lanes[1]], at_ref[0, krows, lanes[2]]
                    else:
                        q2, k2, v2 = x16_s[hp, qrows, :], x16_s[2 + hp, krows, :], x16_s[4 + hp, krows, :]
                outs, lses = [], []
                for hh in range(2):
                    keep = first_head if hh == 0 else jnp.logical_not(first_head)
                    qm = jnp.where(keep, q2, jnp.zeros_like(q2)).astype(BF16)
                    s = lax.dot_general(qm, k2, _NT, preferred_element_type=F32) * (HD ** -0.5)
                    s = s + bias_ref[2 * hp + hh, var]
                    m = jnp.max(s, axis=1, keepdims=True)
                    p = jnp.exp(s - m)
                    den = jnp.sum(p, axis=1, keepdims=True)
                    outs.append(_dot(p.astype(BF16), v2) / den)
                    lses.append(m + jnp.log(den))
                o_new = jnp.where(first_head, outs[0], outs[1])
                l_new = jnp.where(first_head, lses[0], lses[1])
                if dil == 16:
                    y16_s[0, hp, qrows, :] = o_new
                    y16_s[1, hp, qrows, :] = l_new
                else:
                    part_o[pat, hp, qrows, :] = o_new
                    part_l[pat, hp, qrows, :] = l_new
            return 0

        lax.fori_loop(0, A_QBLOCKS, block, 0, unroll=2)

    for pat, (dil, bias_ref) in enumerate(zip(A_DILS, (b1_ref, b4_ref, b16_ref))):
        run_pattern(pat, dil, bias_ref)

    for a, dst in enumerate((part_o, part_l)):
        for hp in range(2):
            for r in range(16):
                y4_s[a, hp, pl.ds((r % 4) * A_SUB4 + r // 4, A_SUB16, stride=4), :] = \
                    y16_s[a, hp, pl.ds(r * A_SUB16, A_SUB16), :]
            for r4 in range(4):
                for c in range(A_SUB4 // A_ROWS):
                    dst[2, hp, pl.ds(r4 + 4 * c * A_ROWS, A_ROWS, stride=4), :] = \
                        y4_s[a, hp, pl.ds(r4 * A_SUB4 + c * A_ROWS, A_ROWS), :]

    def finish(c, _):
        rows = pl.ds(pl.multiple_of(c * A_TQ, A_TQ), A_TQ)
        for hp in range(2):
            ls = [part_l[pat, hp, rows, :] for pat in range(len(A_DILS))]
            mx = jnp.maximum(jnp.maximum(ls[0], ls[1]), ls[2])
            ws = [jnp.exp(l - mx) for l in ls]
            num = ws[0] * part_o[0, hp, rows, :] + ws[1] * part_o[1, hp, rows, :] + ws[2] * part_o[2, hp, rows, :]
            o_ref[0, rows, hp * V7X_LANES:(hp + 1) * V7X_LANES] = (num / (ws[0] + ws[1] + ws[2])).astype(BF16)
        return 0

    lax.fori_loop(0, SEQ // A_TQ, finish, 0)


def dilated_attention(at3, bias1, bias4, bias16):
    b = at3.shape[0]
    full = lambda a: pl.BlockSpec(a.shape, lambda i: (0,) * a.ndim)
    return pl.pallas_call(
        _attn_body, grid=(b,),
        in_specs=[pl.BlockSpec((1, SEQ, 3 * DG), lambda i: (i, 0, 0)), full(bias1), full(bias4), full(bias16)],
        out_specs=pl.BlockSpec((1, SEQ, DG), lambda i: (i, 0, 0)),
        out_shape=jax.ShapeDtypeStruct((b, SEQ, DG), BF16),
        scratch_shapes=[pltpu.VMEM((A_SLABS, SEQ, V7X_LANES), F32),
                        pltpu.VMEM((A_SLABS, SEQ, V7X_LANES), F32),
                        pltpu.VMEM((A_SLABS, SEQ, V7X_LANES), BF16),
                        pltpu.VMEM((2, 2, SEQ, V7X_LANES), F32),
                        pltpu.VMEM((2, 2, SEQ, V7X_LANES), F32),
                        pltpu.VMEM((len(A_DILS), 2, SEQ, V7X_LANES), F32),
                        pltpu.VMEM((len(A_DILS), 2, SEQ, V7X_LANES), F32)],
        compiler_params=_params("parallel"), name="dilated_attention",
    )(at3, bias1, bias4, bias16)


H_BLK = 256
H_CPB = H_BLK // H_CHUNK
N_HBLK = SEQ // H_BLK
N_HCH = SEQ // H_CHUNK


def _chunk_bcast(x, row):
    c = x.shape[1]
    x3 = x.reshape(H_CPB, H_CHUNK, c)
    return jnp.broadcast_to(x3[:, row:row + 1, :], (H_CPB, H_CHUNK, c)).reshape(H_BLK, c)


def _hgrn_body(p_ref, lb_ref, nw_ref, tin_ref, o_ref, qm_s, ut_s, oi_s, dec_s, oe_s, st_s):
    li = lax.broadcasted_iota(I32, (H_BLK, H_BLK), 0)
    si = lax.broadcasted_iota(I32, (H_BLK, H_BLK), 1)
    same = (li // H_CHUNK) == (si // H_CHUNK)
    mask_f = same & (si <= li)
    mask_b = same & (si >= li)
    lane_head = lax.broadcasted_iota(I32, (1, DG), 1) // HD

    def block(bi, _):
        r0 = pl.multiple_of(bi * H_BLK, H_BLK)
        rows = pl.ds(r0, H_BLK)
        q = _silu(p_ref[0, rows, 0:DG].astype(F32))
        v = p_ref[0, rows, 3 * DG:4 * DG]
        scores = [None] * N_HEADS
        for d in range(2):
            fpre = p_ref[0, rows, (1 + d) * DG:(2 + d) * DG].astype(F32)
            lb = lb_ref[d:d + 1, :]
            sg = jax.nn.sigmoid(fpre)
            g = jnp.log(lb + (1.0 - lb) * sg)
            k = (1.0 - lb) * (1.0 - sg)
            gi = _dot01(tin_ref[...], g)
            glast = _chunk_bcast(gi, H_CHUNK - 1)
            if d == 0:
                gc = gi
                gref = _chunk_bcast(gi, H_CHUNK // 2 - 1)
                msk = mask_f
            else:
                gc = glast - gi + g
                gref = _chunk_bcast(gc, H_CHUNK // 2)
                msk = mask_b
            qe = (q * jnp.exp(gc - gref)).astype(BF16)
            ke = (k * jnp.exp(gref - gc)).astype(BF16)
            for h in range(N_HEADS):
                hs = slice(h * HD, (h + 1) * HD)
                sc = jnp.where(msk, lax.dot_general(qe[:, hs], ke[:, hs], _NT, preferred_element_type=F32), 0.0)
                scores[h] = sc if d == 0 else scores[h] + sc
            qd = q * jnp.exp(gc)
            kd = (k * jnp.exp(glast - gc)).astype(BF16)
            for j in range(H_CPB):
                c = bi * H_CPB + j
                cr = slice(j * H_CHUNK, (j + 1) * H_CHUNK)
                qm_s[d, c] = jnp.concatenate([jnp.where(lane_head == h, qd[cr, :], 0.0) for h in range(N_HEADS)],
                                             axis=0).astype(BF16)
                ut = lax.dot_general(v[cr, :], kd[cr, :], _TN, preferred_element_type=F32)
                packed = ut[0:HD, :]
                for h in range(1, N_HEADS):
                    packed = jnp.where(lane_head == h, ut[h * HD:(h + 1) * HD, :], packed)
                ut_s[d, c] = packed.astype(BF16)
                dec_s[d, c] = jnp.broadcast_to(jnp.exp(glast[j * H_CHUNK:j * H_CHUNK + 1, :]), (V7X_SUBLANES, DG))
        for h in range(N_HEADS):
            oi_s[h, rows, :] = _dot(scores[h].astype(BF16), v[:, h * HD:(h + 1) * HD])
        return 0

    lax.fori_loop(0, N_HBLK, block, 0)

    st_s[...] = jnp.zeros(st_s.shape, F32)

    def step(i, _):
        for d in range(2):
            c = i if d == 0 else N_HCH - 1 - i
            rows = pl.ds(pl.multiple_of(c * H_CHUNK, H_CHUNK), H_CHUNK)
            st = st_s[d]
            inter = lax.dot_general(qm_s[d, c], st.astype(BF16), _NT, preferred_element_type=F32)
            for h in range(N_HEADS):
                oe_s[d, h, rows, :] = inter[h * H_CHUNK:(h + 1) * H_CHUNK, :]
            st_s[d] = st * dec_s[d, c][0:1, :] + ut_s[d, c].astype(F32)
        return 0

    lax.fori_loop(0, N_HCH, step, 0, unroll=2)

    def finish(c, _):
        r0 = pl.multiple_of(c * CONV_ROWS, CONV_ROWS)
        rows = pl.ds(r0, CONV_ROWS)
        gate = _silu(p_ref[0, rows, 4 * DG:5 * DG].astype(F32))
        outs = [_rms(oi_s[h, rows, :] + oe_s[0, h, rows, :] + oe_s[1, h, rows, :]) for h in range(N_HEADS)]
        o_ref[0, rows, :] = (jnp.concatenate(outs, axis=1) * nw_ref[...] * gate).astype(BF16)
        return 0

    lax.fori_loop(0, SEQ // CONV_ROWS, finish, 0)


def hgrn2(p3, lb2, norm_w_lanes, tri_in_chunk):
    b = p3.shape[0]
    full = lambda a: pl.BlockSpec(a.shape, lambda i: (0,) * a.ndim)
    return pl.pallas_call(
        _hgrn_body, grid=(b,),
        in_specs=[pl.BlockSpec((1, SEQ, 5 * DG), lambda i: (i, 0, 0)), full(lb2), full(norm_w_lanes),
                  full(tri_in_chunk)],
        out_specs=pl.BlockSpec((1, SEQ, DG), lambda i: (i, 0, 0)),
        out_shape=jax.ShapeDtypeStruct((b, SEQ, DG), BF16),
        scratch_shapes=[pltpu.VMEM((2, N_HCH, N_HEADS * H_CHUNK, DG), BF16),
                        pltpu.VMEM((2, N_HCH, HD, DG), BF16),
                        pltpu.VMEM((N_HEADS, SEQ, HD), F32),
                        pltpu.VMEM((2, N_HCH, V7X_SUBLANES, DG), F32),
                        pltpu.VMEM((2, N_HEADS, SEQ, HD), F32),
                        pltpu.VMEM((2, HD, DG), F32)],
        compiler_params=_params("parallel"), name="hgrn2",
    )(p3, lb2, norm_w_lanes, tri_in_chunk)


@functools.lru_cache(maxsize=None)
def _tables():
    t = {}
    k = np.arange(SEQ, dtype=np.int64)
    ang = 2.0 * np.pi * ((k[:, None] * k[None, :]) % NFFT).astype(np.float64) / NFFT
    t["cos"] = np.cos(ang).astype(np.float32)
    t["sin"] = np.sin(ang).astype(np.float32)
    tt = np.linspace(0.0, 1.0, SEQ, dtype=np.float32)[:, None]
    bands = (HY_POS_DIM - 1) // 2
    ang_pos = (2.0 * math.pi * np.arange(SEQ, dtype=np.float32) / SEQ).astype(np.float32)
    f = np.linspace(1e-4, bands - 1, bands, dtype=np.float32)
    a2 = (ang_pos[:, None] * f[None, :]).astype(np.float32)
    z = np.concatenate([tt, np.cos(a2), -np.sin(a2)], axis=-1).astype(np.float32)
    zp = np.zeros((SEQ, V7X_LANES), np.float32)
    zp[:, :HY_POS_DIM] = z
    t["zpos"] = zp
    max_decay = math.log(1e-2) / 0.3
    min_decay = math.log(1e-2) / 1.5
    deltas = np.abs(np.linspace(min_decay, max_decay, DG, dtype=np.float32))
    t["decay"] = np.exp(-tt * deltas[None, :]).astype(np.float32)
    i128 = np.arange(V7X_LANES)
    t["u128"] = (i128[:, None] < i128[None, :]).astype(np.float32)
    im = np.arange(M_CHUNK)
    t["tri_incl"] = (im[:, None] <= im[None, :]).astype(np.float32)
    ib = np.arange(H_BLK)
    t["tri_in_chunk"] = ((ib[:, None] // H_CHUNK == ib[None, :] // H_CHUNK)
                         & (ib[None, :] <= ib[:, None])).astype(np.float32)
    idg = np.arange(DG)
    t["bdmask"] = (idg[:, None] // HD == idg[None, :] // HD).astype(np.float32)
    def bucket(rel):
        nb = N_BUCKETS // 2
        max_exact = nb // 2
        ret = (rel > 0).astype(np.int64) * nb
        n = np.abs(rel)
        nf = np.maximum(n, 1).astype(np.float64)
        large = max_exact + (np.log(nf / max_exact) / math.log(MAX_DISTANCE / max_exact)
                             * (nb - max_exact)).astype(np.int64)
        large = np.minimum(large, nb - 1)
        return ret + np.where(n < max_exact, n, large)

    for dil in A_DILS:
        n = SEQ // dil
        qi = np.arange(A_TQ)[:, None]
        kj = np.arange(A_KW)[None, :]
        ids = []
        for s0 in (0, -A_BAND, -(A_KW - A_TQ)):
            kk = kj + s0
            rel = kk - qi
            ok = np.abs(rel) <= A_BAND
            if n == A_TQ:
                ok &= (kk >= 0) & (kk < A_TQ)
            ids.append(np.where(ok, bucket(rel * dil), -1))
        t[f"bucket{dil}"] = np.stack(ids).astype(np.int32)
    return t


def kernel(x, w_in, w_out, norm_mix_w, norm_ffn_w, hy_conv_w, hy_pos_w1, hy_pos_b1, hy_pos_w2, hy_pos_b2,
           hy_sin_freq, hy_pos_w3, hy_filt_bias, m_conv_w, m_conv_b, m_dt_bias, m_A_log, m_D, m_norm_w, rel_bias,
           hg_lb, hg_norm_w, router_w, moe_w_gate, moe_w_up, moe_w_down, final_norm_w):
    b = x.shape[0]
    t = b * SEQ
    tb = _tables()
    cos_f32 = jnp.asarray(tb["cos"])
    sin_f32 = jnp.asarray(tb["sin"])
    cos_bf = cos_f32.astype(BF16)
    sin_bf = sin_f32.astype(BF16)
    u128 = jnp.asarray(tb["u128"]).astype(BF16)
    tri_incl = jnp.asarray(tb["tri_incl"]).astype(BF16)
    tri_in_chunk = jnp.asarray(tb["tri_in_chunk"]).astype(BF16)
    bdmask = jnp.asarray(tb["bdmask"])
    attn_bias = [attention_bias_table(jnp.asarray(tb[f"bucket{d}"]), rel_bias.astype(F32)) for d in A_DILS]

    sm = jax.nn.softmax(hg_lb.astype(F32), axis=0)
    lower_bounds = jnp.cumsum(sm, axis=0) - sm[:1]

    xa = x.reshape(t, D_MODEL)
    for l in range(DEPTH):
        wl = w_in[l]
        w_main = jnp.concatenate([wl[:, 0:768], wl[:, 768:1024], wl[:, 1024:1536], wl[:, 1544:2312],
                                  wl[:, 2312:3592]], axis=1).astype(BF16)
        w_dt_rows = wl[:, 1536:1544].T.astype(BF16)
        hy, mz, mx, at, hg, dtc = in_projection(xa, norm_mix_w[l][None, :], w_main, w_dt_rows)

        w1p = jnp.zeros((V7X_LANES, HY_HID), F32).at[:HY_POS_DIM].set(hy_pos_w1[l])
        kr, ki, kny = hyena_filter_spectrum(
            jnp.asarray(tb["zpos"]), w1p, hy_pos_b1[l][None, :], hy_pos_w2[l], hy_pos_b2[l][None, :],
            hy_sin_freq[l][None, :], hy_pos_w3[l], jnp.asarray(tb["decay"]), cos_f32, sin_f32)
        z3, x03 = hyena_prep(hy.reshape(b, SEQ, 3 * DG), hy_conv_w[l])
        ya = hyena_conv(z3, x03, cos_bf, sin_bf, kr, ki, kny, hy_filt_bias[l][None, :]).reshape(t, DG)

        a_col = (-jnp.exp(m_A_log[l].astype(F32))).reshape(8, 1)
        yb = mamba2(mz.reshape(b, SEQ, DG), mx.reshape(b, SEQ, 2 * DG), dtc.reshape(b, N_MCH, 8, MQ),
                    m_conv_w[l], m_conv_b[l][None, :], m_dt_bias[l].reshape(8, 1), a_col,
                    jnp.repeat(m_D[l].astype(F32), HD)[None, :], m_norm_w[l][None, :], tri_incl, bdmask).reshape(t, DG)

        yc = dilated_attention(at.reshape(b, SEQ, 3 * DG), *attn_bias).reshape(t, DG)

        lbl = lower_bounds[l]
        yd = hgrn2(hg.reshape(b, SEQ, 5 * DG), lbl, jnp.tile(hg_norm_w[l], N_HEADS)[None, :],
                   tri_in_chunk).reshape(t, DG)

        xo, xn = out_projection(xa, ya, yb, yc, yd, w_out[l].reshape(4, DG, D_MODEL).astype(BF16),
                                norm_ffn_w[l][None, :])
        xn3 = xn.reshape(b, SEQ, D_MODEL)
        rw_rows = router_w[l].T.astype(F32)
        rw_hi = rw_rows.astype(BF16)
        rw_lo = (rw_rows - rw_hi.astype(F32)).astype(BF16)
        rank, gate, seg = router(xo.reshape(b, SEQ, D_MODEL), norm_ffn_w[l][None, :], rw_hi, rw_lo, u128)
        seg_flat = seg[:, :, :MOE_SEG_STRIDE].reshape(-1)
        xe = moe_gather(seg_flat, xn3, rank)
        ye = moe_experts(xe, moe_w_gate, moe_w_up, moe_w_down, l)
        xa = moe_scatter(seg_flat, ye, rank, gate, xo.reshape(b, SEQ, D_MODEL), final_norm_w[None, :],
                         final=(l == DEPTH - 1)).reshape(t, D_MODEL)
    return xa.reshape(b, SEQ, D_MODEL)
```

```python
import functools
import math

import ml_dtypes
import numpy as np
import jax
import jax.numpy as jnp
from jax import lax
from jax.experimental import pallas as pl
from jax.experimental.pallas import tpu as pltpu

F32 = jnp.float32
BF16 = jnp.bfloat16
I32 = jnp.int32

D_MODEL = 1024
SEQ = 2048
DEPTH = 2
DG = 256
N_HEADS = 4
HD = 64
HY_POS_DIM = 33
HY_HID = 64
M_CONV = 5
M_STATE = 64
M_CHUNK = 128
H_CHUNK = 32
A_BAND = 64
A_DILS = (1, 4, 16)
N_BUCKETS = 32
MAX_DISTANCE = 1024
N_EXPERTS = 16
CAP = 2 * SEQ // N_EXPERTS
D_FF = 1024
EPS = 1e-6
NFFT = 2 * SEQ

V7X_LANES = 128
V7X_SUBLANES = 8
V7X_VMEM_LIMIT_BYTES = 56 * 1024 * 1024

NEG_BIG = -1e30

_NT = (((1,), (1,)), ((), ()))
_TN = (((0,), (0,)), ((), ()))


def _params(*sem):
    return pltpu.CompilerParams(dimension_semantics=sem, vmem_limit_bytes=V7X_VMEM_LIMIT_BYTES)


def _dot(a, b):
    return jnp.dot(a, b, preferred_element_type=F32)


def _dot_hi(a, b):
    return jnp.dot(a, b, preferred_element_type=F32, precision=lax.Precision.HIGHEST)


def _dot01(t_bf16, x):
    x1 = x.astype(BF16)
    r1 = x - x1.astype(F32)
    x2 = r1.astype(BF16)
    x3 = (r1 - x2.astype(F32)).astype(BF16)
    return _dot(t_bf16, x1) + _dot(t_bf16, x2) + _dot(t_bf16, x3)


def _dot01_rhs(x, t_bf16):
    x1 = x.astype(BF16)
    r1 = x - x1.astype(F32)
    x2 = r1.astype(BF16)
    x3 = (r1 - x2.astype(F32)).astype(BF16)
    return _dot(x1, t_bf16) + _dot(x2, t_bf16) + _dot(x3, t_bf16)


def _silu(x):
    return x * jax.nn.sigmoid(x)


def _softplus(x):
    return jnp.maximum(x, 0.0) + jnp.log(1.0 + jnp.exp(-jnp.abs(x)))


def _rms(x):
    return x * lax.rsqrt(jnp.mean(x * x, axis=-1, keepdims=True) + EPS)


TM_PROJ = 1024
_HY0, _MZ0, _MX0, _AT0, _HG0, _PEND = 0, 768, 1024, 1536, 2304, 3584


def _inproj_body(x_ref, nw_ref, w_ref, wdt_ref, hy_ref, mz_ref, mx_ref, at_ref, hg_ref, dtc_ref):
    x = x_ref[...]
    hn = (_rms(x) * nw_ref[...]).astype(BF16)
    hy_ref[...] = _dot(hn, w_ref[:, _HY0:_MZ0]).astype(BF16)
    mz_ref[...] = _dot(hn, w_ref[:, _MZ0:_MX0]).astype(BF16)
    mx_ref[...] = _dot(hn, w_ref[:, _MX0:_AT0]).astype(BF16)
    at_ref[...] = _dot(hn, w_ref[:, _AT0:_HG0]).astype(BF16)
    hg_ref[...] = _dot(hn, w_ref[:, _HG0:_PEND]).astype(BF16)
    dt_rows = lax.dot_general(wdt_ref[...], hn, _NT, preferred_element_type=F32)
    for j in range(TM_PROJ // M_CHUNK):
        dtc_ref[j] = dt_rows[:, j * M_CHUNK:(j + 1) * M_CHUNK]


def in_projection(x, norm_w, w_main, w_dt_rows):
    t = x.shape[0]
    tm = TM_PROJ
    row = lambda w: pl.BlockSpec((tm, w), lambda i: (i, 0))
    full = lambda a: pl.BlockSpec(a.shape, lambda i: (0,) * a.ndim)
    in_specs = [row(D_MODEL), full(norm_w), full(w_main), full(w_dt_rows)]
    widths = (768, 256, 512, 768, 1280)
    out_shape = [jax.ShapeDtypeStruct((t, w), BF16) for w in widths]
    out_shape.append(jax.ShapeDtypeStruct((t // M_CHUNK, 8, M_CHUNK), F32))
    out_specs = [row(w) for w in widths] + [pl.BlockSpec((tm // M_CHUNK, 8, M_CHUNK), lambda i: (i, 0, 0))]
    return pl.pallas_call(
        _inproj_body, grid=(t // tm,), in_specs=in_specs, out_specs=out_specs, out_shape=out_shape,
        compiler_params=_params("parallel"), name="in_projection",
    )(x, norm_w, w_main, w_dt_rows)


TM_OUT = 512


def _outproj_body(x_ref, ya_ref, yb_ref, yc_ref, yd_ref, w_ref, nw_ref, xo_ref, xn_ref):
    x = x_ref[...]
    acc = x + _dot(ya_ref[...], w_ref[0]) + _dot(yb_ref[...], w_ref[1])
    acc = acc + _dot(yc_ref[...], w_ref[2]) + _dot(yd_ref[...], w_ref[3])
    xo_ref[...] = acc
    xn_ref[...] = (_rms(acc) * nw_ref[...]).astype(BF16)


def out_projection(x, ya, yb, yc, yd, w_out4, norm_w):
    t = x.shape[0]
    tm = TM_OUT
    row = lambda w: pl.BlockSpec((tm, w), lambda i: (i, 0))
    full = lambda a: pl.BlockSpec(a.shape, lambda i: (0,) * a.ndim)
    in_specs = [row(D_MODEL)] + [row(DG)] * 4 + [full(w_out4), full(norm_w)]
    return pl.pallas_call(
        _outproj_body, grid=(t // tm,), in_specs=in_specs,
        out_specs=[row(D_MODEL), row(D_MODEL)],
        out_shape=[jax.ShapeDtypeStruct((t, D_MODEL), F32), jax.ShapeDtypeStruct((t, D_MODEL), BF16)],
        compiler_params=_params("parallel"), name="out_projection",
    )(x, ya, yb, yc, yd, w_out4, norm_w)


def _prefix_excl_lanes(mask_f32, u_ref):
    e = mask_f32.shape[0]
    off = jnp.zeros((e, 1), F32)
    parts, bounds = [], [off]
    for k in range(SEQ // V7X_LANES):
        tile = mask_f32[:, k * V7X_LANES:(k + 1) * V7X_LANES]
        parts.append(_dot(tile.astype(BF16), u_ref[...]) + off)
        off = off + jnp.sum(tile, axis=1, keepdims=True)
        bounds.append(off)
    return jnp.concatenate(parts, axis=1), bounds


ROUTER_SEQS = 2


def _router_body(xo_ref, nw_ref, rwh_ref, rwl_ref, u_ref, rank_ref, gate_ref, seg_ref):
    nt = lambda w, a: lax.dot_general(w, a, _NT, preferred_element_type=F32)
    affs = []
    for q in range(ROUTER_SEQS):
        xn = _rms(xo_ref[q]) * nw_ref[...]
        xh = xn.astype(BF16)
        xl = (xn - xh.astype(F32)).astype(BF16)
        logits = nt(rwh_ref[...], xh) + nt(rwh_ref[...], xl) + nt(rwl_ref[...], xh)
        ex = jnp.exp(logits - jnp.max(logits, axis=0, keepdims=True))
        affs.append(ex / jnp.sum(ex, axis=0, keepdims=True))
    aff = jnp.concatenate(affs, axis=0)
    nrow = ROUTER_SEQS * N_EXPERTS
    bits = pltpu.bitcast(aff, I32)

    def search(i, thr):
        cand = thr | jnp.left_shift(jnp.int32(1), 30 - i)
        cnt = jnp.sum((bits >= cand).astype(I32), axis=1, keepdims=True)
        return jnp.where(cnt >= CAP, cand, thr)

    thr = lax.fori_loop(0, 31, search, jnp.zeros((nrow, 1), I32))
    gt = (bits > thr).astype(F32)
    eq = (bits == thr).astype(F32)
    need = CAP - jnp.sum(gt, axis=1, keepdims=True)
    tie_rank, _ = _prefix_excl_lanes(eq, u_ref)
    sel = gt + eq * (tie_rank < need).astype(F32)
    rank, bounds = _prefix_excl_lanes(sel, u_ref)
    rank = jnp.where(sel > 0.0, rank, -1.0)
    lane = lax.broadcasted_iota(I32, (nrow, V7X_LANES), 1)
    seg = jnp.zeros((nrow, V7X_LANES), F32)
    for sgm in range(N_MOE_SEG + 1):
        seg = jnp.where(lane == sgm, bounds[sgm * (MOE_SEG // V7X_LANES)], seg)
    seg = seg.astype(I32)
    for q in range(ROUTER_SEQS):
        rows = slice(q * N_EXPERTS, (q + 1) * N_EXPERTS)
        rank_ref[q] = rank[rows]
        gate_ref[q] = aff[rows]
        seg_ref[q] = seg[rows]


def router(xo3, norm_w, rw_hi, rw_lo, u128):
    b = xo3.shape[0]
    out = jax.ShapeDtypeStruct((b, N_EXPERTS, SEQ), F32)
    full = lambda a: pl.BlockSpec(a.shape, lambda i: (0,) * a.ndim)
    return pl.pallas_call(
        _router_body, grid=(b // ROUTER_SEQS,),
        in_specs=[pl.BlockSpec((ROUTER_SEQS, SEQ, D_MODEL), lambda i: (i, 0, 0)), full(norm_w), full(rw_hi),
                  full(rw_lo), full(u128)],
        out_specs=[pl.BlockSpec((ROUTER_SEQS, N_EXPERTS, SEQ), lambda i: (i, 0, 0))] * 2
                  + [pl.BlockSpec((ROUTER_SEQS, N_EXPERTS, V7X_LANES), lambda i: (i, 0, 0))],
        out_shape=[out, out, jax.ShapeDtypeStruct((b, N_EXPERTS, V7X_LANES), I32)],
        compiler_params=_params("parallel"), name="router",
    )(xo3, norm_w, rw_hi, rw_lo, u128)


MOE_SEG = 256
N_MOE_SEG = SEQ // MOE_SEG
MOE_TILE = 64
MOE_ALIGN = 16
MOE_GROUP = 4
MOE_SEG_STRIDE = 16
MOE_FFN_SEQS = 4


def _moe_seg_plan(cs_ref, b, s):
    starts, rounds = [], jnp.int32(0)
    for ex in range(N_EXPERTS):
        base = (b * N_EXPERTS + ex) * MOE_SEG_STRIDE
        first = (cs_ref[base + s] // MOE_ALIGN) * MOE_ALIGN
        span = cs_ref[base + s + 1] - first
        starts.append(first)
        rounds = jnp.maximum(rounds, (span + MOE_TILE - 1) // MOE_TILE)
    return starts, rounds


def _moe_tile_bases(starts, r):
    own = [st + r * MOE_TILE for st in starts]
    return [pl.multiple_of(jnp.minimum(o, CAP - MOE_TILE), MOE_ALIGN) for o in own], own


def _moe_onehot_group(rank_ref, gate_ref, s, bases, own, grp):
    lanes = pl.ds(pl.multiple_of(s * MOE_SEG, MOE_SEG), MOE_SEG)
    j = lax.broadcasted_iota(I32, (MOE_TILE, MOE_SEG), 0)
    rows = []
    for ex in grp:
        slot = bases[ex] + j
        hit = (rank_ref[0, ex:ex + 1, lanes] == slot.astype(F32)) & (slot >= own[ex])
        val = 1.0 if gate_ref is None else gate_ref[0, ex:ex + 1, lanes]
        rows.append(jnp.where(hit, val, 0.0).astype(BF16))
    return jnp.concatenate(rows, axis=0)


_MOE_GROUPS = [list(range(g * MOE_GROUP, (g + 1) * MOE_GROUP)) for g in range(N_EXPERTS // MOE_GROUP)]


def _moe_gather_body(cs_ref, xn_ref, rank_ref, xe_ref):
    b = pl.program_id(0)

    def zero(ex, _):
        xe_ref[0, ex] = jnp.zeros((CAP, D_MODEL), BF16)
        return 0

    lax.fori_loop(0, N_EXPERTS, zero, 0)

    def seg_gather(s, _):
        starts, rounds = _moe_seg_plan(cs_ref, b, s)
        xn_seg = xn_ref[0, pl.ds(pl.multiple_of(s * MOE_SEG, MOE_SEG), MOE_SEG), :]

        def one_round(r, _):
            bases, own = _moe_tile_bases(starts, r)
            for grp in _MOE_GROUPS:
                got = _dot(_moe_onehot_group(rank_ref, None, s, bases, own, grp), xn_seg)
                for k, ex in enumerate(grp):
                    rows = pl.ds(bases[ex], MOE_TILE)
                    old = xe_ref[0, ex, rows, :].astype(F32)
                    xe_ref[0, ex, rows, :] = (old + got[k * MOE_TILE:(k + 1) * MOE_TILE]).astype(BF16)
            return 0

        lax.fori_loop(0, rounds, one_round, 0)
        return 0

    lax.fori_loop(0, N_MOE_SEG, seg_gather, 0)


def moe_gather(seg_counts_flat, xn3, rank3):
    b = xn3.shape[0]
    grid_spec = pltpu.PrefetchScalarGridSpec(
        num_scalar_prefetch=1, grid=(b,),
        in_specs=[pl.BlockSpec((1, SEQ, D_MODEL), lambda i, cs: (i, 0, 0)),
                  pl.BlockSpec((1, N_EXPERTS, SEQ), lambda i, cs: (i, 0, 0))],
        out_specs=pl.BlockSpec((1, N_EXPERTS, CAP, D_MODEL), lambda i, cs: (i, 0, 0, 0)))
    return pl.pallas_call(
        _moe_gather_body, grid_spec=grid_spec,
        out_shape=jax.ShapeDtypeStruct((b, N_EXPERTS, CAP, D_MODEL), BF16),
        compiler_params=_params("parallel"), name="moe_gather",
    )(seg_counts_flat, xn3, rank3)


def _moe_experts_body(xe_ref, wg_ref, wu_ref, wd_ref, ye_ref, wg_s, wu_s, wd_s):
    @pl.when(pl.program_id(1) == 0)
    def _():
        wg_s[...] = wg_ref[0, 0].astype(BF16)
        wu_s[...] = wu_ref[0, 0].astype(BF16)
        wd_s[...] = wd_ref[0, 0].astype(BF16)

    xe = xe_ref[...].reshape(MOE_FFN_SEQS * CAP, D_MODEL)
    hid = (_silu(_dot(xe, wg_s[...])) * _dot(xe, wu_s[...])).astype(BF16)
    ye_ref[...] = _dot(hid, wd_s[...]).astype(BF16).reshape(MOE_FFN_SEQS, 1, CAP, D_MODEL)


def moe_experts(xe4, w_gate, w_up, w_down, layer):
    b = xe4.shape[0]
    blk = pl.BlockSpec((MOE_FFN_SEQS, 1, CAP, D_MODEL), lambda e, g: (g, e, 0, 0))
    w_spec = lambda a: pl.BlockSpec((1, 1) + a.shape[2:], lambda e, g: (layer, e, 0, 0))
    return pl.pallas_call(
        _moe_experts_body, grid=(N_EXPERTS, b // MOE_FFN_SEQS),
        in_specs=[blk, w_spec(w_gate), w_spec(w_up), w_spec(w_down)],
        out_specs=blk, out_shape=jax.ShapeDtypeStruct(xe4.shape, BF16),
        scratch_shapes=[pltpu.VMEM((D_MODEL, D_FF), BF16), pltpu.VMEM((D_MODEL, D_FF), BF16),
                        pltpu.VMEM((D_FF, D_MODEL), BF16)],
        compiler_params=_params("parallel", "arbitrary"), name="moe_experts",
    )(xe4, w_gate, w_up, w_down)


MOE_SCATTER_SEGS = 4


def _moe_scatter_body(final, cs_ref, ye_ref, rank_ref, gate_ref, xo_ref, nw_ref, o_ref):
    b = pl.program_id(0)
    half = pl.program_id(1)

    def seg_scatter(k, _):
        s = half * MOE_SCATTER_SEGS + k
        starts, rounds = _moe_seg_plan(cs_ref, b, s)
        tok = pl.ds(pl.multiple_of(k * MOE_SEG, MOE_SEG), MOE_SEG)
        o_ref[0, tok, :] = xo_ref[0, tok, :]

        def one_round(r, _):
            bases, own = _moe_tile_bases(starts, r)
            for grp in _MOE_GROUPS:
                ye = jnp.concatenate([ye_ref[0, ex, pl.ds(bases[ex], MOE_TILE), :] for ex in grp], axis=0)
                o_ref[0, tok, :] += lax.dot_general(_moe_onehot_group(rank_ref, gate_ref, s, bases, own, grp), ye,
                                                    _TN, preferred_element_type=F32)
            return 0

        lax.fori_loop(0, rounds, one_round, 0)
        if final:
            o_ref[0, tok, :] = _rms(o_ref[0, tok, :]) * nw_ref[...]
        return 0

    lax.fori_loop(0, MOE_SCATTER_SEGS, seg_scatter, 0)


def moe_scatter(seg_counts_flat, ye4, rank3, gate3, xo3, final_norm_w, final):
    b = ye4.shape[0]
    rows = MOE_SCATTER_SEGS * MOE_SEG
    sel_spec = pl.BlockSpec((1, N_EXPERTS, SEQ), lambda i, j, cs: (i, 0, 0))
    tok_spec = pl.BlockSpec((1, rows, D_MODEL), lambda i, j, cs: (i, j, 0))
    grid_spec = pltpu.PrefetchScalarGridSpec(
        num_scalar_prefetch=1, grid=(b, SEQ // rows),
        in_specs=[pl.BlockSpec((1, N_EXPERTS, CAP, D_MODEL), lambda i, j, cs: (i, 0, 0, 0)), sel_spec, sel_spec,
                  tok_spec, pl.BlockSpec(final_norm_w.shape, lambda i, j, cs: (0, 0))],
        out_specs=tok_spec)
    return pl.pallas_call(
        functools.partial(_moe_scatter_body, final), grid_spec=grid_spec,
        out_shape=jax.ShapeDtypeStruct((b, SEQ, D_MODEL), F32),
        compiler_params=_params("parallel", "arbitrary"), name="moe_scatter",
    )(seg_counts_flat, ye4, rank3, gate3, xo3, final_norm_w)


HY_KB = 256
HY_ROWS = 256


def _hy_filter_body(z_ref, w1_ref, b1_ref, w2_ref, b2_ref, fr_ref, w3_ref, dec_ref, c_ref, s_ref,
                    kr_ref, ki_ref, kny_ref, a_s, d_s):
    @pl.when(pl.program_id(0) == 0)
    def _():
        def rows(c, kny):
            r0 = pl.multiple_of(c * HY_ROWS, HY_ROWS)
            fr = fr_ref[...]
            h = jnp.sin(fr * (_dot_hi(z_ref[pl.ds(r0, HY_ROWS), :], w1_ref[...]) + b1_ref[...]))
            h = jnp.sin(fr * (_dot_hi(h, w2_ref[...]) + b2_ref[...]))
            h = _dot_hi(h, w3_ref[...])
            dec = dec_ref[pl.ds(r0, HY_ROWS), :]
            pos = r0 + lax.broadcasted_iota(I32, (HY_ROWS, DG), 0)
            hf = h[:, :DG] * dec
            hb = jnp.where(pos == 0, 0.0, h[:, DG:] * dec)
            a = hf + hb
            a_s[pl.ds(r0, HY_ROWS), :] = a
            d_s[pl.ds(r0, HY_ROWS), :] = hf - hb
            sgn = (1 - 2 * (pos & 1)).astype(F32)
            return kny + jnp.sum(a * sgn, axis=0, keepdims=True)

        kny = lax.fori_loop(0, SEQ // HY_ROWS, rows, jnp.zeros((1, DG), F32))
        kny_ref[...] = jnp.broadcast_to(kny, kny_ref.shape)

    kr_ref[...] = _dot_hi(c_ref[...], a_s[...])
    ki_ref[...] = _dot_hi(s_ref[...], d_s[...])


def hyena_filter_spectrum(zpos, w1, b1, w2, b2, freq, w3, decay, cos_f32, sin_f32):
    full = lambda a: pl.BlockSpec(a.shape, lambda k: (0,) * a.ndim)
    kblk = pl.BlockSpec((HY_KB, SEQ), lambda k: (k, 0))
    oblk = pl.BlockSpec((HY_KB, DG), lambda k: (k, 0))
    return pl.pallas_call(
        _hy_filter_body, grid=(SEQ // HY_KB,),
        in_specs=[full(zpos), full(w1), full(b1), full(w2), full(b2), full(freq), full(w3), full(decay), kblk, kblk],
        out_specs=[oblk, oblk, pl.BlockSpec((V7X_SUBLANES, DG), lambda k: (0, 0))],
        out_shape=[jax.ShapeDtypeStruct((SEQ, DG), F32), jax.ShapeDtypeStruct((SEQ, DG), F32),
                   jax.ShapeDtypeStruct((V7X_SUBLANES, DG), F32)],
        scratch_shapes=[pltpu.VMEM((SEQ, DG), F32), pltpu.VMEM((SEQ, DG), F32)],
        compiler_params=_params("arbitrary"), name="hyena_filter",
    )(zpos, w1, b1, w2, b2, freq, w3, decay, cos_f32, sin_f32)


CONV_ROWS = 128
CONV_HALO = 8


def _dwconv_rows(pad_ref, w_ref, r0, lanes, k):
    n = CONV_ROWS + 2 * CONV_HALO
    win = pad_ref[pl.ds(r0, n), lanes]
    acc = None
    for j in range(k):
        sh = (k // 2 - j) % n
        rolled = win if sh == 0 else pltpu.roll(win, sh, 0)
        term = rolled[CONV_HALO:CONV_HALO + CONV_ROWS] * w_ref[j:j + 1, lanes]
        acc = term if acc is None else acc + term
    return acc


def _fill_padded(pad_ref, src_ref, width):
    zeros = jnp.zeros((CONV_HALO, width), F32)
    pad_ref[pl.ds(0, CONV_HALO), :] = zeros
    pad_ref[pl.ds(SEQ + CONV_HALO, CONV_HALO), :] = zeros

    def fill(c, _):
        r0 = pl.multiple_of(c * CONV_ROWS, CONV_ROWS)
        pad_ref[pl.ds(r0 + CONV_HALO, CONV_ROWS), :] = src_ref[0, pl.ds(r0, CONV_ROWS), :].astype(F32)
        return 0

    lax.fori_loop(0, SEQ // CONV_ROWS, fill, 0)


def _hy_prep_body(p_ref, w_ref, z_ref, x0_ref, pad):
    _fill_padded(pad, p_ref, 3 * DG)

    def rows(c, _):
        r0 = pl.multiple_of(c * CONV_ROWS, CONV_ROWS)
        x0 = _dwconv_rows(pad, w_ref, r0, slice(0, DG), 3)
        x1 = _dwconv_rows(pad, w_ref, r0, slice(DG, 2 * DG), 3)
        v = _dwconv_rows(pad, w_ref, r0, slice(2 * DG, 3 * DG), 3)
        x0_ref[0, pl.ds(r0, CONV_ROWS), :] = x0.astype(BF16)
        z_ref[0, pl.ds(r0, CONV_ROWS), :] = (v * x1).astype(BF16)
        return 0

    lax.fori_loop(0, SEQ // CONV_ROWS, rows, 0)


def hyena_prep(p3, conv_w):
    b = p3.shape[0]
    blk = pl.BlockSpec((1, SEQ, DG), lambda i: (i, 0, 0))
    out = jax.ShapeDtypeStruct((b, SEQ, DG), BF16)
    return pl.pallas_call(
        _hy_prep_body, grid=(b,),
        in_specs=[pl.BlockSpec((1, SEQ, 3 * DG), lambda i: (i, 0, 0)), pl.BlockSpec(conv_w.shape, lambda i: (0, 0))],
        out_specs=[blk, blk], out_shape=[out, out],
        scratch_shapes=[pltpu.VMEM((SEQ + 2 * CONV_HALO, 3 * DG), F32)],
        compiler_params=_params("parallel"), name="hyena_prep",
    )(p3, conv_w)


HY_FB = 512


def _hy_conv_body(z_ref, x0_ref, t1_ref, t2_ref, kr_ref, ki_ref, kny_ref, fb_ref, o_ref, y_s):
    z = z_ref[0]

    def spectrum(kb, _):
        rows = pl.ds(pl.multiple_of(kb * HY_FB, HY_FB), HY_FB)
        rows_s = pl.ds(pl.multiple_of(SEQ + kb * HY_FB, HY_FB), HY_FB)
        zr = _dot(t1_ref[rows, :], z)
        zi = _dot(t1_ref[rows_s, :], z)
        krow = kb * HY_FB + lax.broadcasted_iota(I32, (HY_FB, 1), 0)
        wk = jnp.where(krow == 0, 1.0 / NFFT, 2.0 / NFFT)
        kr = kr_ref[rows, :]
        ki = ki_ref[rows, :]
        y_s[rows, :] = ((zr * kr - zi * ki) * wk).astype(BF16)
        y_s[rows_s, :] = ((zr * ki + zi * kr) * wk).astype(BF16)
        return 0

    lax.fori_loop(0, SEQ // HY_FB, spectrum, 0)

    zny = jnp.sum(z.astype(F32) * (1 - 2 * (lax.broadcasted_iota(I32, (SEQ, DG), 0) & 1)).astype(F32),
                  axis=0, keepdims=True)
    nyq = zny * kny_ref[0:1, :] * (1.0 / NFFT)

    def synth(tb, _):
        rows = pl.ds(pl.multiple_of(tb * HY_FB, HY_FB), HY_FB)
        conv = _dot(t2_ref[rows, :], y_s[...])
        sgn = (1 - 2 * (lax.broadcasted_iota(I32, (HY_FB, DG), 0) & 1)).astype(F32)
        zf = z_ref[0, rows, :].astype(F32)
        o_ref[0, rows, :] = (x0_ref[0, rows, :].astype(F32) * (conv + nyq * sgn + zf * fb_ref[...])).astype(BF16)
        return 0

    lax.fori_loop(0, SEQ // HY_FB, synth, 0)


def hyena_conv(z3, x03, dft_rows, dft_cols, kr, ki, kny, fbias):
    b = z3.shape[0]
    seq_blk = pl.BlockSpec((1, SEQ, DG), lambda i: (i, 0, 0))
    once = lambda a: pl.BlockSpec(a.shape, lambda i: (0,) * a.ndim, pipeline_mode=pl.Buffered(1))
    return pl.pallas_call(
        _hy_conv_body, grid=(b,),
        in_specs=[seq_blk, seq_blk, once(dft_rows), once(dft_cols), once(kr), once(ki), once(kny), once(fbias)],
        out_specs=seq_blk, out_shape=jax.ShapeDtypeStruct((b, SEQ, DG), BF16),
        scratch_shapes=[pltpu.VMEM((2 * SEQ, DG), BF16)],
        compiler_params=_params("parallel"), name="hyena_conv",
    )(z3, x03, dft_rows, dft_cols, kr, ki, kny, fbias)


N_MCH = SEQ // M_CHUNK
MQ = M_CHUNK


def _head_lane_vec(rows8, base):
    lane_head = lax.broadcasted_iota(I32, (1, DG), 1) // HD
    out = jnp.zeros((1, DG), F32)
    for h in range(N_HEADS):
        out = jnp.where(lane_head == h, rows8[base + h:base + h + 1, :], out)
    return out


def _mamba_body(z_ref, xbc_ref, dtc_ref, cw_ref, cb_ref, dtb_ref, a_ref, dsk_ref, nw_ref, tri_ref, bd_ref,
                o_ref, pad, xs_s, b_s, c_s, y_s, u_s, dec_s, cw_s, yo_s, st_s):
    _fill_padded(pad, xbc_ref, 2 * DG)

    def conv_rows(c, _):
        r0 = pl.multiple_of(c * CONV_ROWS, CONV_ROWS)
        for g in range(4):
            lanes = slice(g * V7X_LANES, (g + 1) * V7X_LANES)
            u = _silu(_dwconv_rows(pad, cw_ref, r0, lanes, M_CONV) + cb_ref[:, lanes])
            if g < 2:
                xs_s[pl.ds(r0, CONV_ROWS), lanes] = u
            elif g == 2:
                b_s[pl.ds(r0, CONV_ROWS), :] = u.astype(BF16)
            else:
                c_s[pl.ds(r0, CONV_ROWS), :] = u.astype(BF16)
        return 0

    lax.fori_loop(0, SEQ // CONV_ROWS, conv_rows, 0)

    li = lax.broadcasted_iota(I32, (MQ, MQ), 0)
    si = lax.broadcasted_iota(I32, (MQ, MQ), 1)
    lower = si <= li
    upper = si >= li
    bdmask = bd_ref[...]

    def chunk(c, _):
        r0 = pl.multiple_of(c * MQ, MQ)
        dt = _softplus(dtc_ref[0, c] + dtb_ref[...])
        a = dt * a_ref[...]
        cum = _dot01_rhs(a, tri_ref[...])
        tot = cum[:, MQ - 1:MQ]
        suf = tot - cum + a
        row_dir = lax.broadcasted_iota(I32, (8, MQ), 0) // N_HEADS
        seg = jnp.where(row_dir == 0, cum, suf)
        wgt = jnp.exp(tot - seg) * dt
        cols = jnp.concatenate([seg, jnp.exp(seg)], axis=0).T
        x = xs_s[pl.ds(r0, MQ), :]
        xb = x.astype(BF16)
        bm = b_s[pl.ds(r0, MQ), :]
        cm = c_s[pl.ds(r0, MQ), :]
        cmf = cm.astype(F32)
        bt = bm.astype(F32).T
        ydiag = []
        for h in range(N_HEADS):
            g = h // 2
            cb = lax.dot_general(cm[:, g * M_STATE:(g + 1) * M_STATE], bm[:, g * M_STATE:(g + 1) * M_STATE],
                                 _NT, preferred_element_type=F32)
            lf = jnp.where(lower, jnp.exp(jnp.minimum(cols[:, h:h + 1] - seg[h:h + 1, :], 0.0)), 0.0)
            lb = jnp.where(upper, jnp.exp(jnp.minimum(cols[:, 4 + h:5 + h] - seg[4 + h:5 + h, :], 0.0)), 0.0)
            m = cb * (lf * dt[h:h + 1, :] + lb * dt[4 + h:5 + h, :])
            ydiag.append(_dot(m.astype(BF16), xb[:, h * HD:(h + 1) * HD]))
        y_s[pl.ds(r0, MQ), :] = jnp.concatenate(ydiag, axis=1)
        for d in range(2):
            bwt = jnp.concatenate([bt[(h // 2) * M_STATE:(h // 2 + 1) * M_STATE, :] * wgt[4 * d + h:4 * d + h + 1, :]
                                   for h in range(N_HEADS)], axis=0)
            u_s[d, c] = (_dot(bwt.astype(BF16), xb) * bdmask).astype(BF16)
            dec_s[d, c] = jnp.broadcast_to(_head_lane_vec(jnp.exp(tot), 4 * d), (V7X_SUBLANES, DG))
            cw_s[d, c] = jnp.concatenate(
                [cmf[:, (h // 2) * M_STATE:(h // 2 + 1) * M_STATE] * cols[:, 8 + 4 * d + h:9 + 4 * d + h]
                 for h in range(N_HEADS)], axis=1).astype(BF16)
        return 0

    lax.fori_loop(0, N_MCH, chunk, 0, unroll=2)

    st_s[...] = jnp.zeros(st_s.shape, F32)

    def scan(i, _):
        for d in range(2):
            c = i if d == 0 else N_MCH - 1 - i
            st = st_s[d]
            yo_s[d, pl.ds(pl.multiple_of(c * MQ, MQ), MQ), :] = _dot(cw_s[d, c], st.astype(BF16))
            st_s[d] = st * dec_s[d, c][0:1, :] + u_s[d, c].astype(F32)
        return 0

    lax.fori_loop(0, N_MCH, scan, 0)

    def finish(c, _):
        r0 = pl.multiple_of(c * CONV_ROWS, CONV_ROWS)
        rows = pl.ds(r0, CONV_ROWS)
        y = y_s[rows, :] + yo_s[0, rows, :] + yo_s[1, rows, :] + xs_s[rows, :] * dsk_ref[...]
        y = y * _silu(z_ref[0, rows, :].astype(F32))
        o_ref[0, pl.ds(r0, CONV_ROWS), :] = (_rms(y) * nw_ref[...]).astype(BF16)
        return 0

    lax.fori_loop(0, SEQ // CONV_ROWS, finish, 0)


def mamba2(z3, xbc3, dtc4, conv_w, conv_b, dt_bias_col, a_col, dskip_lanes, norm_w, tri_incl, bdmask):
    b = z3.shape[0]
    full = lambda a: pl.BlockSpec(a.shape, lambda i: (0,) * a.ndim)
    return pl.pallas_call(
        _mamba_body, grid=(b,),
        in_specs=[pl.BlockSpec((1, SEQ, DG), lambda i: (i, 0, 0)),
                  pl.BlockSpec((1, SEQ, 2 * DG), lambda i: (i, 0, 0)),
                  pl.BlockSpec((1, N_MCH, 8, MQ), lambda i: (i, 0, 0, 0)),
                  full(conv_w), full(conv_b), full(dt_bias_col), full(a_col), full(dskip_lanes), full(norm_w),
                  full(tri_incl), full(bdmask)],
        out_specs=pl.BlockSpec((1, SEQ, DG), lambda i: (i, 0, 0)),
        out_shape=jax.ShapeDtypeStruct((b, SEQ, DG), BF16),
        scratch_shapes=[pltpu.VMEM((SEQ + 2 * CONV_HALO, 2 * DG), F32),
                        pltpu.VMEM((SEQ, DG), F32),
                        pltpu.VMEM((SEQ, 2 * M_STATE), BF16),
                        pltpu.VMEM((SEQ, 2 * M_STATE), BF16),
                        pltpu.VMEM((SEQ, DG), F32),
                        pltpu.VMEM((2, N_MCH, DG, DG), BF16),
                        pltpu.VMEM((2, N_MCH, V7X_SUBLANES, DG), F32),
                        pltpu.VMEM((2, N_MCH, MQ, DG), BF16),
                        pltpu.VMEM((2, SEQ, DG), F32),
                        pltpu.VMEM((2, DG, DG), F32)],
        compiler_params=_params("parallel"), name="mamba2",
    )(z3, xbc3, dtc4, conv_w, conv_b, dt_bias_col, a_col, dskip_lanes, norm_w, tri_incl, bdmask)


A_TQ = 128
A_ROWS = 256
A_KW = A_TQ + 2 * A_BAND


def _attn_bias_body(ids_ref, rb_ref, o_ref):
    ids = ids_ref[0]
    for h in range(N_HEADS):
        acc = jnp.full(ids.shape, NEG_BIG, F32)
        for bkt in range(N_BUCKETS):
            acc = jnp.where(ids == bkt, rb_ref[bkt, h], acc)
        o_ref[h, 0] = acc


def attention_bias_table(bucket_ids, rel_bias):
    nvar, tq, w = bucket_ids.shape
    return pl.pallas_call(
        _attn_bias_body, grid=(nvar,),
        in_specs=[pl.BlockSpec((1, tq, w), lambda v: (v, 0, 0)),
                  pl.BlockSpec(memory_space=pltpu.SMEM)],
        out_specs=pl.BlockSpec((N_HEADS, 1, tq, w), lambda v: (0, v, 0, 0)),
        out_shape=jax.ShapeDtypeStruct((N_HEADS, nvar, tq, w), F32),
        compiler_params=_params("parallel"), name="attention_bias_table",
    )(bucket_ids, rel_bias)


A_SLABS = 3 * DG // V7X_LANES
A_QBLOCKS = SEQ // A_TQ


A_SUB4 = SEQ // 4
A_SUB16 = SEQ // 16


def _attn_body(at_ref, b1_ref, b4_ref, b16_ref, o_ref, qkv_s, x4_s, x16_s, y16_s, y4_s, part_o, part_l):
    def fill(c, _):
        r0 = pl.multiple_of(c * A_ROWS, A_ROWS)
        for s in range(A_SLABS):
            qkv_s[s, pl.ds(r0, A_ROWS), :] = at_ref[0, pl.ds(r0, A_ROWS), s * V7X_LANES:(s + 1) * V7X_LANES].astype(F32)
        return 0

    lax.fori_loop(0, SEQ // A_ROWS, fill, 0)

    def deinterleave(s, _):
        for r4 in range(4):
            for c in range(A_SUB4 // A_ROWS):
                x4_s[s, pl.ds(r4 * A_SUB4 + c * A_ROWS, A_ROWS), :] = \
                    qkv_s[s, pl.ds(r4 + 4 * c * A_ROWS, A_ROWS, stride=4), :]
        for r in range(16):
            x16_s[s, pl.ds(r * A_SUB16, A_SUB16), :] = \
                x4_s[s, pl.ds((r % 4) * A_SUB4 + r // 4, A_SUB16, stride=4), :].astype(BF16)
        return 0

    lax.fori_loop(0, A_SLABS, deinterleave, 0)
    first_head = lax.broadcasted_iota(I32, (A_TQ, V7X_LANES), 1) < HD

    def run_pattern(pat, dil, bias_ref):
        n = SEQ // dil if dil < 16 else SEQ
        nblk = n // A_TQ
        w = A_KW

        def block(it, _):
            r = it // nblk
            i = it - r * nblk
            q0 = i * A_TQ
            k0 = jnp.clip(q0 - A_BAND, 0, n - w)
            var = jnp.where(i == 0, 0, jnp.where(i == nblk - 1, 2, 1))
            for hp in range(2):
                lanes = [slice((2 * part + hp) * V7X_LANES, (2 * part + hp + 1) * V7X_LANES) for part in range(3)]
                if dil == 4:
                    qrows = pl.ds(r + dil * q0, A_TQ, stride=dil)
                    krows = pl.ds(r + dil * k0, w, stride=dil)
                    q2 = qkv_s[hp, qrows, :]
                    k2 = qkv_s[2 + hp, krows, :].astype(BF16)
                    v2 = qkv_s[4 + hp, krows, :].astype(BF16)
                else:
                    qrows = pl.ds(pl.multiple_of(q0, A_TQ), A_TQ)
                    krows = pl.ds(pl.multiple_of(k0, A_BAND), w)
                    if dil == 1:
                        q2, k2, v2 = at_ref[0, qrows, lanes[0]], at_ref[0, krows, lanes[1]], at_ref[0, krows, lanes[2]]
                    else:
                        q2, k2, v2 = x16_s[hp, qrows, :], x16_s[2 + hp, krows, :], x16_s[4 + hp, krows, :]
                outs, lses = [], []
                for hh in range(2):
                    keep = first_head if hh == 0 else jnp.logical_not(first_head)
                    qm = jnp.where(keep, q2, jnp.zeros_like(q2)).astype(BF16)
                    s = lax.dot_general(qm, k2, _NT, preferred_element_type=F32) * (HD ** -0.5)
                    s = s + bias_ref[2 * hp + hh, var]
                    m = jnp.max(s, axis=1, keepdims=True)
                    p = jnp.exp(s - m)
                    den = jnp.sum(p, axis=1, keepdims=True)
                    outs.append(_dot(p.astype(BF16), v2) / den)
                    lses.append(m + jnp.log(den))
                o_new = jnp.where(first_head, outs[0], outs[1])
                l_new = jnp.where(first_head, lses[0], lses[1])
                if dil == 16:
                    y16_s[0, hp, qrows, :] = o_new
                    y16_s[1, hp, qrows, :] = l_new
                else:
                    part_o[pat, hp, qrows, :] = o_new
                    part_l[pat, hp, qrows, :] = l_new
            return 0

        lax.fori_loop(0, A_QBLOCKS, block, 0, unroll=2)

    for pat, (dil, bias_ref) in enumerate(zip(A_DILS, (b1_ref, b4_ref, b16_ref))):
        run_pattern(pat, dil, bias_ref)

    for a, dst in enumerate((part_o, part_l)):
        for hp in range(2):
            for r in range(16):
                y4_s[a, hp, pl.ds((r % 4) * A_SUB4 + r // 4, A_SUB16, stride=4), :] = \
                    y16_s[a, hp, pl.ds(r * A_SUB16, A_SUB16), :]
            for r4 in range(4):
                for c in range(A_SUB4 // A_ROWS):
                    dst[2, hp, pl.ds(r4 + 4 * c * A_ROWS, A_ROWS, stride=4), :] = \
                        y4_s[a, hp, pl.ds(r4 * A_SUB4 + c * A_ROWS, A_ROWS), :]

    def finish(c, _):
        rows = pl.ds(pl.multiple_of(c * A_TQ, A_TQ), A_TQ)
        for hp in range(2):
            ls = [part_l[pat, hp, rows, :] for pat in range(len(A_DILS))]
            mx = jnp.maximum(jnp.maximum(ls[0], ls[1]), ls[2])
            ws = [jnp.exp(l - mx) for l in ls]
            num = ws[0] * part_o[0, hp, rows, :] + ws[1] * part_o[1, hp, rows, :] + ws[2] * part_o[2, hp, rows, :]
            o_ref[0, rows, hp * V7X_LANES:(hp + 1) * V7X_LANES] = (num / (ws[0] + ws[1] + ws[2])).astype(BF16)
        return 0

    lax.fori_loop(0, SEQ // A_TQ, finish, 0)


def dilated_attention(at3, bias1, bias4, bias16):
    b = at3.shape[0]
    full = lambda a: pl.BlockSpec(a.shape, lambda i: (0,) * a.ndim)
    return pl.pallas_call(
        _attn_body, grid=(b,),
        in_specs=[pl.BlockSpec((1, SEQ, 3 * DG), lambda i: (i, 0, 0)), full(bias1), full(bias4), full(bias16)],
        out_specs=pl.BlockSpec((1, SEQ, DG), lambda i: (i, 0, 0)),
        out_shape=jax.ShapeDtypeStruct((b, SEQ, DG), BF16),
        scratch_shapes=[pltpu.VMEM((A_SLABS, SEQ, V7X_LANES), F32),
                        pltpu.VMEM((A_SLABS, SEQ, V7X_LANES), F32),
                        pltpu.VMEM((A_SLABS, SEQ, V7X_LANES), BF16),
                        pltpu.VMEM((2, 2, SEQ, V7X_LANES), F32),
                        pltpu.VMEM((2, 2, SEQ, V7X_LANES), F32),
                        pltpu.VMEM((len(A_DILS), 2, SEQ, V7X_LANES), F32),
                        pltpu.VMEM((len(A_DILS), 2, SEQ, V7X_LANES), F32)],
        compiler_params=_params("parallel"), name="dilated_attention",
    )(at3, bias1, bias4, bias16)


H_BLK = 256
H_CPB = H_BLK // H_CHUNK
N_HBLK = SEQ // H_BLK
N_HCH = SEQ // H_CHUNK


def _chunk_bcast(x, row):
    c = x.shape[1]
    x3 = x.reshape(H_CPB, H_CHUNK, c)
    return jnp.broadcast_to(x3[:, row:row + 1, :], (H_CPB, H_CHUNK, c)).reshape(H_BLK, c)


def _hgrn_body(p_ref, lb_ref, nw_ref, tin_ref, o_ref, qm_s, ut_s, oi_s, dec_s, oe_s, st_s):
    li = lax.broadcasted_iota(I32, (H_BLK, H_BLK), 0)
    si = lax.broadcasted_iota(I32, (H_BLK, H_BLK), 1)
    same = (li // H_CHUNK) == (si // H_CHUNK)
    mask_f = same & (si <= li)
    mask_b = same & (si >= li)
    lane_head = lax.broadcasted_iota(I32, (1, DG), 1) // HD

    def block(bi, _):
        r0 = pl.multiple_of(bi * H_BLK, H_BLK)
        rows = pl.ds(r0, H_BLK)
        q = _silu(p_ref[0, rows, 0:DG].astype(F32))
        v = p_ref[0, rows, 3 * DG:4 * DG]
        scores = [None] * N_HEADS
        for d in range(2):
            fpre = p_ref[0, rows, (1 + d) * DG:(2 + d) * DG].astype(F32)
            lb = lb_ref[d:d + 1, :]
            sg = jax.nn.sigmoid(fpre)
            g = jnp.log(lb + (1.0 - lb) * sg)
            k = (1.0 - lb) * (1.0 - sg)
            gi = _dot01(tin_ref[...], g)
            glast = _chunk_bcast(gi, H_CHUNK - 1)
            if d == 0:
                gc = gi
                gref = _chunk_bcast(gi, H_CHUNK // 2 - 1)
                msk = mask_f
            else:
                gc = glast - gi + g
                gref = _chunk_bcast(gc, H_CHUNK // 2)
                msk = mask_b
            qe = (q * jnp.exp(gc - gref)).astype(BF16)
            ke = (k * jnp.exp(gref - gc)).astype(BF16)
            for h in range(N_HEADS):
                hs = slice(h * HD, (h + 1) * HD)
                sc = jnp.where(msk, lax.dot_general(qe[:, hs], ke[:, hs], _NT, preferred_element_type=F32), 0.0)
                scores[h] = sc if d == 0 else scores[h] + sc
            qd = q * jnp.exp(gc)
            kd = (k * jnp.exp(glast - gc)).astype(BF16)
            for j in range(H_CPB):
                c = bi * H_CPB + j
                cr = slice(j * H_CHUNK, (j + 1) * H_CHUNK)
                qm_s[d, c] = jnp.concatenate([jnp.where(lane_head == h, qd[cr, :], 0.0) for h in range(N_HEADS)],
                                             axis=0).astype(BF16)
                ut = lax.dot_general(v[cr, :], kd[cr, :], _TN, preferred_element_type=F32)
                packed = ut[0:HD, :]
                for h in range(1, N_HEADS):
                    packed = jnp.where(lane_head == h, ut[h * HD:(h + 1) * HD, :], packed)
                ut_s[d, c] = packed.astype(BF16)
                dec_s[d, c] = jnp.broadcast_to(jnp.exp(glast[j * H_CHUNK:j * H_CHUNK + 1, :]), (V7X_SUBLANES, DG))
        for h in range(N_HEADS):
            oi_s[h, rows, :] = _dot(scores[h].astype(BF16), v[:, h * HD:(h + 1) * HD])
        return 0

    lax.fori_loop(0, N_HBLK, block, 0)

    st_s[...] = jnp.zeros(st_s.shape, F32)

    def step(i, _):
        for d in range(2):
            c = i if d == 0 else N_HCH - 1 - i
            rows = pl.ds(pl.multiple_of(c * H_CHUNK, H_CHUNK), H_CHUNK)
            st = st_s[d]
            inter = lax.dot_general(qm_s[d, c], st.astype(BF16), _NT, preferred_element_type=F32)
            for h in range(N_HEADS):
                oe_s[d, h, rows, :] = inter[h * H_CHUNK:(h + 1) * H_CHUNK, :]
            st_s[d] = st * dec_s[d, c][0:1, :] + ut_s[d, c].astype(F32)
        return 0

    lax.fori_loop(0, N_HCH, step, 0, unroll=2)

    def finish(c, _):
        r0 = pl.multiple_of(c * CONV_ROWS, CONV_ROWS)
        rows = pl.ds(r0, CONV_ROWS)
        gate = _silu(p_ref[0, rows, 4 * DG:5 * DG].astype(F32))
        outs = [_rms(oi_s[h, rows, :] + oe_s[0, h, rows, :] + oe_s[1, h, rows, :]) for h in range(N_HEADS)]
        o_ref[0, rows, :] = (jnp.concatenate(outs, axis=1) * nw_ref[...] * gate).astype(BF16)
        return 0

    lax.fori_loop(0, SEQ // CONV_ROWS, finish, 0)


def hgrn2(p3, lb2, norm_w_lanes, tri_in_chunk):
    b = p3.shape[0]
    full = lambda a: pl.BlockSpec(a.shape, lambda i: (0,) * a.ndim)
    return pl.pallas_call(
        _hgrn_body, grid=(b,),
        in_specs=[pl.BlockSpec((1, SEQ, 5 * DG), lambda i: (i, 0, 0)), full(lb2), full(norm_w_lanes),
                  full(tri_in_chunk)],
        out_specs=pl.BlockSpec((1, SEQ, DG), lambda i: (i, 0, 0)),
        out_shape=jax.ShapeDtypeStruct((b, SEQ, DG), BF16),
        scratch_shapes=[pltpu.VMEM((2, N_HCH, N_HEADS * H_CHUNK, DG), BF16),
                        pltpu.VMEM((2, N_HCH, HD, DG), BF16),
                        pltpu.VMEM((N_HEADS, SEQ, HD), F32),
                        pltpu.VMEM((2, N_HCH, V7X_SUBLANES, DG), F32),
                        pltpu.VMEM((2, N_HEADS, SEQ, HD), F32),
                        pltpu.VMEM((2, HD, DG), F32)],
        compiler_params=_params("parallel"), name="hgrn2",
    )(p3, lb2, norm_w_lanes, tri_in_chunk)


@functools.lru_cache(maxsize=None)
def _tables():
    t = {}
    k = np.arange(SEQ, dtype=np.int64)
    ang = 2.0 * np.pi * ((k[:, None] * k[None, :]) % NFFT).astype(np.float64) / NFFT
    t["cos"] = np.cos(ang).astype(np.float32)
    t["sin"] = np.sin(ang).astype(np.float32)
    t["dft_rows"] = np.concatenate([t["cos"], t["sin"]], axis=0).astype(ml_dtypes.bfloat16)
    t["dft_cols"] = np.concatenate([t["cos"], t["sin"]], axis=1).astype(ml_dtypes.bfloat16)
    tt = np.linspace(0.0, 1.0, SEQ, dtype=np.float32)[:, None]
    bands = (HY_POS_DIM - 1) // 2
    ang_pos = (2.0 * math.pi * np.arange(SEQ, dtype=np.float32) / SEQ).astype(np.float32)
    f = np.linspace(1e-4, bands - 1, bands, dtype=np.float32)
    a2 = (ang_pos[:, None] * f[None, :]).astype(np.float32)
    z = np.concatenate([tt, np.cos(a2), -np.sin(a2)], axis=-1).astype(np.float32)
    zp = np.zeros((SEQ, V7X_LANES), np.float32)
    zp[:, :HY_POS_DIM] = z
    t["zpos"] = zp
    max_decay = math.log(1e-2) / 0.3
    min_decay = math.log(1e-2) / 1.5
    deltas = np.abs(np.linspace(min_decay, max_decay, DG, dtype=np.float32))
    t["decay"] = np.exp(-tt * deltas[None, :]).astype(np.float32)
    i128 = np.arange(V7X_LANES)
    t["u128"] = (i128[:, None] < i128[None, :]).astype(np.float32)
    im = np.arange(M_CHUNK)
    t["tri_incl"] = (im[:, None] <= im[None, :]).astype(np.float32)
    ib = np.arange(H_BLK)
    t["tri_in_chunk"] = ((ib[:, None] // H_CHUNK == ib[None, :] // H_CHUNK)
                         & (ib[None, :] <= ib[:, None])).astype(np.float32)
    idg = np.arange(DG)
    t["bdmask"] = (idg[:, None] // HD == idg[None, :] // HD).astype(np.float32)
    def bucket(rel):
        nb = N_BUCKETS // 2
        max_exact = nb // 2
        ret = (rel > 0).astype(np.int64) * nb
        n = np.abs(rel)
        nf = np.maximum(n, 1).astype(np.float64)
        large = max_exact + (np.log(nf / max_exact) / math.log(MAX_DISTANCE / max_exact)
                             * (nb - max_exact)).astype(np.int64)
        large = np.minimum(large, nb - 1)
        return ret + np.where(n < max_exact, n, large)

    for dil in A_DILS:
        n = SEQ // dil
        qi = np.arange(A_TQ)[:, None]
        kj = np.arange(A_KW)[None, :]
        ids = []
        for s0 in (0, -A_BAND, -(A_KW - A_TQ)):
            kk = kj + s0
            rel = kk - qi
            ok = np.abs(rel) <= A_BAND
            if n == A_TQ:
                ok &= (kk >= 0) & (kk < A_TQ)
            ids.append(np.where(ok, bucket(rel * dil), -1))
        t[f"bucket{dil}"] = np.stack(ids).astype(np.int32)
    return t


def kernel(x, w_in, w_out, norm_mix_w, norm_ffn_w, hy_conv_w, hy_pos_w1, hy_pos_b1, hy_pos_w2, hy_pos_b2,
           hy_sin_freq, hy_pos_w3, hy_filt_bias, m_conv_w, m_conv_b, m_dt_bias, m_A_log, m_D, m_norm_w, rel_bias,
           hg_lb, hg_norm_w, router_w, moe_w_gate, moe_w_up, moe_w_down, final_norm_w):
    b = x.shape[0]
    t = b * SEQ
    tb = _tables()
    cos_f32 = jnp.asarray(tb["cos"])
    sin_f32 = jnp.asarray(tb["sin"])
    dft_rows = jnp.asarray(tb["dft_rows"])
    dft_cols = jnp.asarray(tb["dft_cols"])
    u128 = jnp.asarray(tb["u128"]).astype(BF16)
    tri_incl = jnp.asarray(tb["tri_incl"]).astype(BF16)
    tri_in_chunk = jnp.asarray(tb["tri_in_chunk"]).astype(BF16)
    bdmask = jnp.asarray(tb["bdmask"])
    attn_bias = [attention_bias_table(jnp.asarray(tb[f"bucket{d}"]), rel_bias.astype(F32)) for d in A_DILS]

    sm = jax.nn.softmax(hg_lb.astype(F32), axis=0)
    lower_bounds = jnp.cumsum(sm, axis=0) - sm[:1]

    xa = x.reshape(t, D_MODEL)
    for l in range(DEPTH):
        wl = w_in[l]
        w_main = jnp.concatenate([wl[:, 0:768], wl[:, 768:1024], wl[:, 1024:1536], wl[:, 1544:2312],
                                  wl[:, 2312:3592]], axis=1).astype(BF16)
        w_dt_rows = wl[:, 1536:1544].T.astype(BF16)
        hy, mz, mx, at, hg, dtc = in_projection(xa, norm_mix_w[l][None, :], w_main, w_dt_rows)

        w1p = jnp.zeros((V7X_LANES, HY_HID), F32).at[:HY_POS_DIM].set(hy_pos_w1[l])
        kr, ki, kny = hyena_filter_spectrum(
            jnp.asarray(tb["zpos"]), w1p, hy_pos_b1[l][None, :], hy_pos_w2[l], hy_pos_b2[l][None, :],
            hy_sin_freq[l][None, :], hy_pos_w3[l], jnp.asarray(tb["decay"]), cos_f32, sin_f32)
        z3, x03 = hyena_prep(hy.reshape(b, SEQ, 3 * DG), hy_conv_w[l])
        ya = hyena_conv(z3, x03, dft_rows, dft_cols, kr, ki, kny, hy_filt_bias[l][None, :]).reshape(t, DG)

        a_col = (-jnp.exp(m_A_log[l].astype(F32))).reshape(8, 1)
        yb = mamba2(mz.reshape(b, SEQ, DG), mx.reshape(b, SEQ, 2 * DG), dtc.reshape(b, N_MCH, 8, MQ),
                    m_conv_w[l], m_conv_b[l][None, :], m_dt_bias[l].reshape(8, 1), a_col,
                    jnp.repeat(m_D[l].astype(F32), HD)[None, :], m_norm_w[l][None, :], tri_incl, bdmask).reshape(t, DG)

        yc = dilated_attention(at.reshape(b, SEQ, 3 * DG), *attn_bias).reshape(t, DG)

        lbl = lower_bounds[l]
        yd = hgrn2(hg.reshape(b, SEQ, 5 * DG), lbl, jnp.tile(hg_norm_w[l], N_HEADS)[None, :],
                   tri_in_chunk).reshape(t, DG)

        xo, xn = out_projection(xa, ya, yb, yc, yd, w_out[l].reshape(4, DG, D_MODEL).astype(BF16),
                                norm_ffn_w[l][None, :])
        xn3 = xn.reshape(b, SEQ, D_MODEL)
        rw_rows = router_w[l].T.astype(F32)
        rw_hi = rw_rows.astype(BF16)
        rw_lo = (rw_rows - rw_hi.astype(F32)).astype(BF16)
        rank, gate, seg = router(xo.reshape(b, SEQ, D_MODEL), norm_ffn_w[l][None, :], rw_hi, rw_lo, u128)
        seg_flat = seg[:, :, :MOE_SEG_STRIDE].reshape(-1)
        xe = moe_gather(seg_flat, xn3, rank)
        ye = moe_experts(xe, moe_w_gate, moe_w_up, moe_w_down, l)
        xa = moe_scatter(seg_flat, ye, rank, gate, xo.reshape(b, SEQ, D_MODEL), final_norm_w[None, :],
                         final=(l == DEPTH - 1)).reshape(t, D_MODEL)
    return xa.reshape(b, SEQ, D_MODEL)
```

```python
import functools
import math

import ml_dtypes
import numpy as np
import jax
import jax.numpy as jnp
from jax import lax
from jax.experimental import pallas as pl
from jax.experimental.pallas import tpu as pltpu

F32 = jnp.float32
BF16 = jnp.bfloat16
I32 = jnp.int32

D_MODEL = 1024
SEQ = 2048
DEPTH = 2
DG = 256
N_HEADS = 4
HD = 64
HY_POS_DIM = 33
HY_HID = 64
M_CONV = 5
M_STATE = 64
M_CHUNK = 128
H_CHUNK = 32
A_BAND = 64
A_DILS = (1, 4, 16)
N_BUCKETS = 32
MAX_DISTANCE = 1024
N_EXPERTS = 16
CAP = 2 * SEQ // N_EXPERTS
D_FF = 1024
EPS = 1e-6
NFFT = 2 * SEQ

V7X_LANES = 128
V7X_SUBLANES = 8
V7X_VMEM_LIMIT_BYTES = 56 * 1024 * 1024

NEG_BIG = -1e30

_NT = (((1,), (1,)), ((), ()))
_TN = (((0,), (0,)), ((), ()))


def _params(*sem):
    return pltpu.CompilerParams(dimension_semantics=sem, vmem_limit_bytes=V7X_VMEM_LIMIT_BYTES)


def _dot(a, b):
    return jnp.dot(a, b, preferred_element_type=F32)


def _dot_hi(a, b):
    return jnp.dot(a, b, preferred_element_type=F32, precision=lax.Precision.HIGHEST)


def _dot01_2(t_bf16, x):
    x1 = x.astype(BF16)
    x2 = (x - x1.astype(F32)).astype(BF16)
    return _dot(t_bf16, x1) + _dot(t_bf16, x2)


def _dot01_rhs(x, t_bf16):
    x1 = x.astype(BF16)
    r1 = x - x1.astype(F32)
    x2 = r1.astype(BF16)
    x3 = (r1 - x2.astype(F32)).astype(BF16)
    return _dot(x1, t_bf16) + _dot(x2, t_bf16) + _dot(x3, t_bf16)


def _silu(x):
    return x * jax.nn.sigmoid(x)


def _softplus(x):
    return jnp.maximum(x, 0.0) + jnp.log(1.0 + jnp.exp(-jnp.abs(x)))


def _rms(x):
    return x * lax.rsqrt(jnp.mean(x * x, axis=-1, keepdims=True) + EPS)


TM_PROJ = 1024
_HY0, _MZ0, _MX0, _AT0, _HG0, _PEND = 0, 768, 1024, 1536, 2304, 3584


def _inproj_body(x_ref, nw_ref, w_ref, wdt_ref, hy_ref, mz_ref, mx_ref, at_ref, hg_ref, dtc_ref):
    x = x_ref[...]
    hn = (_rms(x) * nw_ref[...]).astype(BF16)
    hy_ref[...] = _dot(hn, w_ref[:, _HY0:_MZ0]).astype(BF16)
    mz_ref[...] = _dot(hn, w_ref[:, _MZ0:_MX0]).astype(BF16)
    mx_ref[...] = _dot(hn, w_ref[:, _MX0:_AT0]).astype(BF16)
    at_ref[...] = _dot(hn, w_ref[:, _AT0:_HG0]).astype(BF16)
    hg_ref[...] = _dot(hn, w_ref[:, _HG0:_PEND]).astype(BF16)
    dt_rows = lax.dot_general(wdt_ref[...], hn, _NT, preferred_element_type=F32)
    for j in range(TM_PROJ // M_CHUNK):
        dtc_ref[j] = dt_rows[:, j * M_CHUNK:(j + 1) * M_CHUNK]


def in_projection(x, norm_w, w_main, w_dt_rows):
    t = x.shape[0]
    tm = TM_PROJ
    row = lambda w: pl.BlockSpec((tm, w), lambda i: (i, 0))
    full = lambda a: pl.BlockSpec(a.shape, lambda i: (0,) * a.ndim)
    in_specs = [row(D_MODEL), full(norm_w), full(w_main), full(w_dt_rows)]
    widths = (768, 256, 512, 768, 1280)
    out_shape = [jax.ShapeDtypeStruct((t, w), BF16) for w in widths]
    out_shape.append(jax.ShapeDtypeStruct((t // M_CHUNK, 8, M_CHUNK), F32))
    out_specs = [row(w) for w in widths] + [pl.BlockSpec((tm // M_CHUNK, 8, M_CHUNK), lambda i: (i, 0, 0))]
    return pl.pallas_call(
        _inproj_body, grid=(t // tm,), in_specs=in_specs, out_specs=out_specs, out_shape=out_shape,
        compiler_params=_params("parallel"), name="in_projection",
    )(x, norm_w, w_main, w_dt_rows)


TM_OUT = 512


def _outproj_body(x_ref, ya_ref, yb_ref, yc_ref, yd_ref, w_ref, nw_ref, xo_ref, xn_ref):
    x = x_ref[...]
    acc = x + _dot(ya_ref[...], w_ref[0]) + _dot(yb_ref[...], w_ref[1])
    acc = acc + _dot(yc_ref[...], w_ref[2]) + _dot(yd_ref[...], w_ref[3])
    xo_ref[...] = acc
    xn_ref[...] = (_rms(acc) * nw_ref[...]).astype(BF16)


def out_projection(x, ya, yb, yc, yd, w_out4, norm_w):
    t = x.shape[0]
    tm = TM_OUT
    row = lambda w: pl.BlockSpec((tm, w), lambda i: (i, 0))
    full = lambda a: pl.BlockSpec(a.shape, lambda i: (0,) * a.ndim)
    in_specs = [row(D_MODEL)] + [row(DG)] * 4 + [full(w_out4), full(norm_w)]
    return pl.pallas_call(
        _outproj_body, grid=(t // tm,), in_specs=in_specs,
        out_specs=[row(D_MODEL), row(D_MODEL)],
        out_shape=[jax.ShapeDtypeStruct((t, D_MODEL), F32), jax.ShapeDtypeStruct((t, D_MODEL), BF16)],
        compiler_params=_params("parallel"), name="out_projection",
    )(x, ya, yb, yc, yd, w_out4, norm_w)


def _prefix_excl_lanes(mask_f32, u_ref):
    e = mask_f32.shape[0]
    off = jnp.zeros((e, 1), F32)
    parts, bounds = [], [off]
    for k in range(SEQ // V7X_LANES):
        tile = mask_f32[:, k * V7X_LANES:(k + 1) * V7X_LANES]
        parts.append(_dot(tile.astype(BF16), u_ref[...]) + off)
        off = off + jnp.sum(tile, axis=1, keepdims=True)
        bounds.append(off)
    return jnp.concatenate(parts, axis=1), bounds


ROUTER_SEQS = 2


def _router_body(xo_ref, nw_ref, rwh_ref, rwl_ref, u_ref, rank_ref, gate_ref, seg_ref):
    nt = lambda w, a: lax.dot_general(w, a, _NT, preferred_element_type=F32)
    affs = []
    for q in range(ROUTER_SEQS):
        xn = _rms(xo_ref[q]) * nw_ref[...]
        xh = xn.astype(BF16)
        xl = (xn - xh.astype(F32)).astype(BF16)
        logits = nt(rwh_ref[...], xh) + nt(rwh_ref[...], xl) + nt(rwl_ref[...], xh)
        ex = jnp.exp(logits - jnp.max(logits, axis=0, keepdims=True))
        affs.append(ex / jnp.sum(ex, axis=0, keepdims=True))
    aff = jnp.concatenate(affs, axis=0)
    nrow = ROUTER_SEQS * N_EXPERTS
    bits = pltpu.bitcast(aff, I32)

    def search(i, thr):
        cand = thr | jnp.left_shift(jnp.int32(1), 30 - i)
        cnt = jnp.sum((bits >= cand).astype(I32), axis=1, keepdims=True)
        return jnp.where(cnt >= CAP, cand, thr)

    thr = lax.fori_loop(0, 31, search, jnp.zeros((nrow, 1), I32))
    gt = (bits > thr).astype(F32)
    eq = (bits == thr).astype(F32)
    need = CAP - jnp.sum(gt, axis=1, keepdims=True)
    tie_rank, _ = _prefix_excl_lanes(eq, u_ref)
    sel = gt + eq * (tie_rank < need).astype(F32)
    rank, bounds = _prefix_excl_lanes(sel, u_ref)
    rank = jnp.where(sel > 0.0, rank, -1.0)
    lane = lax.broadcasted_iota(I32, (nrow, V7X_LANES), 1)
    seg = jnp.zeros((nrow, V7X_LANES), F32)
    for sgm in range(N_MOE_SEG + 1):
        seg = jnp.where(lane == sgm, bounds[sgm * (MOE_SEG // V7X_LANES)], seg)
    seg = seg.astype(I32)
    for q in range(ROUTER_SEQS):
        rows = slice(q * N_EXPERTS, (q + 1) * N_EXPERTS)
        rank_ref[q] = rank[rows]
        gate_ref[q] = aff[rows]
        seg_ref[q] = seg[rows]


def router(xo3, norm_w, rw_hi, rw_lo, u128):
    b = xo3.shape[0]
    out = jax.ShapeDtypeStruct((b, N_EXPERTS, SEQ), F32)
    full = lambda a: pl.BlockSpec(a.shape, lambda i: (0,) * a.ndim)
    return pl.pallas_call(
        _router_body, grid=(b // ROUTER_SEQS,),
        in_specs=[pl.BlockSpec((ROUTER_SEQS, SEQ, D_MODEL), lambda i: (i, 0, 0)), full(norm_w), full(rw_hi),
                  full(rw_lo), full(u128)],
        out_specs=[pl.BlockSpec((ROUTER_SEQS, N_EXPERTS, SEQ), lambda i: (i, 0, 0))] * 2
                  + [pl.BlockSpec((ROUTER_SEQS, N_EXPERTS, V7X_LANES), lambda i: (i, 0, 0))],
        out_shape=[out, out, jax.ShapeDtypeStruct((b, N_EXPERTS, V7X_LANES), I32)],
        compiler_params=_params("parallel"), name="router",
    )(xo3, norm_w, rw_hi, rw_lo, u128)


MOE_SEG = 256
N_MOE_SEG = SEQ // MOE_SEG
MOE_TILE = 64
MOE_ALIGN = 16
MOE_GROUP = 4
MOE_SEG_STRIDE = 16
MOE_FFN_SEQS = 4


def _moe_seg_plan(cs_ref, b, s):
    starts, rounds = [], jnp.int32(0)
    for ex in range(N_EXPERTS):
        base = (b * N_EXPERTS + ex) * MOE_SEG_STRIDE
        first = (cs_ref[base + s] // MOE_ALIGN) * MOE_ALIGN
        span = cs_ref[base + s + 1] - first
        starts.append(first)
        rounds = jnp.maximum(rounds, (span + MOE_TILE - 1) // MOE_TILE)
    return starts, rounds


def _moe_tile_bases(starts, r):
    own = [st + r * MOE_TILE for st in starts]
    return [pl.multiple_of(jnp.minimum(o, CAP - MOE_TILE), MOE_ALIGN) for o in own], own


def _moe_onehot_group(rank_ref, gate_ref, s, bases, own, grp):
    lanes = pl.ds(pl.multiple_of(s * MOE_SEG, MOE_SEG), MOE_SEG)
    j = lax.broadcasted_iota(I32, (MOE_TILE, MOE_SEG), 0)
    rows = []
    for ex in grp:
        slot = bases[ex] + j
        hit = (rank_ref[0, ex:ex + 1, lanes] == slot.astype(F32)) & (slot >= own[ex])
        val = 1.0 if gate_ref is None else gate_ref[0, ex:ex + 1, lanes]
        rows.append(jnp.where(hit, val, 0.0).astype(BF16))
    return jnp.concatenate(rows, axis=0)


_MOE_GROUPS = [list(range(g * MOE_GROUP, (g + 1) * MOE_GROUP)) for g in range(N_EXPERTS // MOE_GROUP)]


def _moe_gather_body(cs_ref, xn_ref, rank_ref, xe_ref):
    b = pl.program_id(0)

    def zero(ex, _):
        xe_ref[0, ex] = jnp.zeros((CAP, D_MODEL), BF16)
        return 0

    lax.fori_loop(0, N_EXPERTS, zero, 0)

    def seg_gather(s, _):
        starts, rounds = _moe_seg_plan(cs_ref, b, s)
        xn_seg = xn_ref[0, pl.ds(pl.multiple_of(s * MOE_SEG, MOE_SEG), MOE_SEG), :]

        def one_round(r, _):
            bases, own = _moe_tile_bases(starts, r)
            for grp in _MOE_GROUPS:
                got = _dot(_moe_onehot_group(rank_ref, None, s, bases, own, grp), xn_seg)
                for k, ex in enumerate(grp):
                    rows = pl.ds(bases[ex], MOE_TILE)
                    old = xe_ref[0, ex, rows, :].astype(F32)
                    xe_ref[0, ex, rows, :] = (old + got[k * MOE_TILE:(k + 1) * MOE_TILE]).astype(BF16)
            return 0

        lax.fori_loop(0, rounds, one_round, 0)
        return 0

    lax.fori_loop(0, N_MOE_SEG, seg_gather, 0)


def moe_gather(seg_counts_flat, xn3, rank3):
    b = xn3.shape[0]
    grid_spec = pltpu.PrefetchScalarGridSpec(
        num_scalar_prefetch=1, grid=(b,),
        in_specs=[pl.BlockSpec((1, SEQ, D_MODEL), lambda i, cs: (i, 0, 0)),
                  pl.BlockSpec((1, N_EXPERTS, SEQ), lambda i, cs: (i, 0, 0))],
        out_specs=pl.BlockSpec((1, N_EXPERTS, CAP, D_MODEL), lambda i, cs: (i, 0, 0, 0)))
    return pl.pallas_call(
        _moe_gather_body, grid_spec=grid_spec,
        out_shape=jax.ShapeDtypeStruct((b, N_EXPERTS, CAP, D_MODEL), BF16),
        compiler_params=_params("parallel"), name="moe_gather",
    )(seg_counts_flat, xn3, rank3)


def _moe_experts_body(xe_ref, wg_ref, wu_ref, wd_ref, ye_ref, wg_s, wu_s, wd_s):
    @pl.when(pl.program_id(1) == 0)
    def _():
        wg_s[...] = wg_ref[0, 0].astype(BF16)
        wu_s[...] = wu_ref[0, 0].astype(BF16)
        wd_s[...] = wd_ref[0, 0].astype(BF16)

    xe = xe_ref[...].reshape(MOE_FFN_SEQS * CAP, D_MODEL)
    hid = (_silu(_dot(xe, wg_s[...])) * _dot(xe, wu_s[...])).astype(BF16)
    ye_ref[...] = _dot(hid, wd_s[...]).astype(BF16).reshape(MOE_FFN_SEQS, 1, CAP, D_MODEL)


def moe_experts(xe4, w_gate, w_up, w_down, layer):
    b = xe4.shape[0]
    blk = pl.BlockSpec((MOE_FFN_SEQS, 1, CAP, D_MODEL), lambda e, g: (g, e, 0, 0))
    w_spec = lambda a: pl.BlockSpec((1, 1) + a.shape[2:], lambda e, g: (layer, e, 0, 0))
    return pl.pallas_call(
        _moe_experts_body, grid=(N_EXPERTS, b // MOE_FFN_SEQS),
        in_specs=[blk, w_spec(w_gate), w_spec(w_up), w_spec(w_down)],
        out_specs=blk, out_shape=jax.ShapeDtypeStruct(xe4.shape, BF16),
        scratch_shapes=[pltpu.VMEM((D_MODEL, D_FF), BF16), pltpu.VMEM((D_MODEL, D_FF), BF16),
                        pltpu.VMEM((D_FF, D_MODEL), BF16)],
        compiler_params=_params("parallel", "arbitrary"), name="moe_experts",
    )(xe4, w_gate, w_up, w_down)


MOE_SCATTER_SEGS = 4


def _moe_scatter_body(final, cs_ref, ye_ref, rank_ref, gate_ref, xo_ref, nw_ref, o_ref):
    b = pl.program_id(0)
    half = pl.program_id(1)

    def seg_scatter(k, _):
        s = half * MOE_SCATTER_SEGS + k
        starts, rounds = _moe_seg_plan(cs_ref, b, s)
        tok = pl.ds(pl.multiple_of(k * MOE_SEG, MOE_SEG), MOE_SEG)
        o_ref[0, tok, :] = xo_ref[0, tok, :]

        def one_round(r, _):
            bases, own = _moe_tile_bases(starts, r)
            for grp in _MOE_GROUPS:
                ye = jnp.concatenate([ye_ref[0, ex, pl.ds(bases[ex], MOE_TILE), :] for ex in grp], axis=0)
                o_ref[0, tok, :] += lax.dot_general(_moe_onehot_group(rank_ref, gate_ref, s, bases, own, grp), ye,
                                                    _TN, preferred_element_type=F32)
            return 0

        lax.fori_loop(0, rounds, one_round, 0)
        if final:
            o_ref[0, tok, :] = _rms(o_ref[0, tok, :]) * nw_ref[...]
        return 0

    lax.fori_loop(0, MOE_SCATTER_SEGS, seg_scatter, 0)


def moe_scatter(seg_counts_flat, ye4, rank3, gate3, xo3, final_norm_w, final):
    b = ye4.shape[0]
    rows = MOE_SCATTER_SEGS * MOE_SEG
    sel_spec = pl.BlockSpec((1, N_EXPERTS, SEQ), lambda i, j, cs: (i, 0, 0))
    tok_spec = pl.BlockSpec((1, rows, D_MODEL), lambda i, j, cs: (i, j, 0))
    grid_spec = pltpu.PrefetchScalarGridSpec(
        num_scalar_prefetch=1, grid=(b, SEQ // rows),
        in_specs=[pl.BlockSpec((1, N_EXPERTS, CAP, D_MODEL), lambda i, j, cs: (i, 0, 0, 0)), sel_spec, sel_spec,
                  tok_spec, pl.BlockSpec(final_norm_w.shape, lambda i, j, cs: (0, 0))],
        out_specs=tok_spec)
    return pl.pallas_call(
        functools.partial(_moe_scatter_body, final), grid_spec=grid_spec,
        out_shape=jax.ShapeDtypeStruct((b, SEQ, D_MODEL), F32),
        compiler_params=_params("parallel", "arbitrary"), name="moe_scatter",
    )(seg_counts_flat, ye4, rank3, gate3, xo3, final_norm_w)


HY_KB = 256
HY_ROWS = 256


def _hy_filter_body(z_ref, w1_ref, b1_ref, w2_ref, b2_ref, fr_ref, w3_ref, dec_ref, c_ref, s_ref,
                    kr_ref, ki_ref, kny_ref, a_s, d_s):
    @pl.when(pl.program_id(0) == 0)
    def _():
        def rows(c, kny):
            r0 = pl.multiple_of(c * HY_ROWS, HY_ROWS)
            fr = fr_ref[...]
            h = jnp.sin(fr * (_dot_hi(z_ref[pl.ds(r0, HY_ROWS), :], w1_ref[...]) + b1_ref[...]))
            h = jnp.sin(fr * (_dot_hi(h, w2_ref[...]) + b2_ref[...]))
            h = _dot_hi(h, w3_ref[...])
            dec = dec_ref[pl.ds(r0, HY_ROWS), :]
            pos = r0 + lax.broadcasted_iota(I32, (HY_ROWS, DG), 0)
            hf = h[:, :DG] * dec
            hb = jnp.where(pos == 0, 0.0, h[:, DG:] * dec)
            a = hf + hb
            a_s[pl.ds(r0, HY_ROWS), :] = a
            d_s[pl.ds(r0, HY_ROWS), :] = hf - hb
            sgn = (1 - 2 * (pos & 1)).astype(F32)
            return kny + jnp.sum(a * sgn, axis=0, keepdims=True)

        kny = lax.fori_loop(0, SEQ // HY_ROWS, rows, jnp.zeros((1, DG), F32))
        kny_ref[...] = jnp.broadcast_to(kny, kny_ref.shape)

    kr_ref[...] = _dot_hi(c_ref[...], a_s[...])
    ki_ref[...] = _dot_hi(s_ref[...], d_s[...])


def hyena_filter_spectrum(zpos, w1, b1, w2, b2, freq, w3, decay, cos_f32, sin_f32):
    full = lambda a: pl.BlockSpec(a.shape, lambda k: (0,) * a.ndim)
    kblk = pl.BlockSpec((HY_KB, SEQ), lambda k: (k, 0))
    oblk = pl.BlockSpec((HY_KB, DG), lambda k: (k, 0))
    return pl.pallas_call(
        _hy_filter_body, grid=(SEQ // HY_KB,),
        in_specs=[full(zpos), full(w1), full(b1), full(w2), full(b2), full(freq), full(w3), full(decay), kblk, kblk],
        out_specs=[oblk, oblk, pl.BlockSpec((V7X_SUBLANES, DG), lambda k: (0, 0))],
        out_shape=[jax.ShapeDtypeStruct((SEQ, DG), F32), jax.ShapeDtypeStruct((SEQ, DG), F32),
                   jax.ShapeDtypeStruct((V7X_SUBLANES, DG), F32)],
        scratch_shapes=[pltpu.VMEM((SEQ, DG), F32), pltpu.VMEM((SEQ, DG), F32)],
        compiler_params=_params("arbitrary"), name="hyena_filter",
    )(zpos, w1, b1, w2, b2, freq, w3, decay, cos_f32, sin_f32)


CONV_ROWS = 128
CONV_HALO = 8


def _dwconv_rows(pad_ref, w_ref, r0, lanes, k):
    n = CONV_ROWS + 2 * CONV_HALO
    win = pad_ref[pl.ds(r0, n), lanes]
    acc = None
    for j in range(k):
        sh = (k // 2 - j) % n
        rolled = win if sh == 0 else pltpu.roll(win, sh, 0)
        term = rolled[CONV_HALO:CONV_HALO + CONV_ROWS] * w_ref[j:j + 1, lanes]
        acc = term if acc is None else acc + term
    return acc


def _fill_padded(pad_ref, src_ref, width):
    zeros = jnp.zeros((CONV_HALO, width), F32)
    pad_ref[pl.ds(0, CONV_HALO), :] = zeros
    pad_ref[pl.ds(SEQ + CONV_HALO, CONV_HALO), :] = zeros

    def fill(c, _):
        r0 = pl.multiple_of(c * CONV_ROWS, CONV_ROWS)
        pad_ref[pl.ds(r0 + CONV_HALO, CONV_ROWS), :] = src_ref[0, pl.ds(r0, CONV_ROWS), :].astype(F32)
        return 0

    lax.fori_loop(0, SEQ // CONV_ROWS, fill, 0)


def _hy_prep_body(p_ref, w_ref, z_ref, x0_ref, pad):
    _fill_padded(pad, p_ref, 3 * DG)

    def rows(c, _):
        r0 = pl.multiple_of(c * CONV_ROWS, CONV_ROWS)
        x0 = _dwconv_rows(pad, w_ref, r0, slice(0, DG), 3)
        x1 = _dwconv_rows(pad, w_ref, r0, slice(DG, 2 * DG), 3)
        v = _dwconv_rows(pad, w_ref, r0, slice(2 * DG, 3 * DG), 3)
        x0_ref[0, pl.ds(r0, CONV_ROWS), :] = x0.astype(BF16)
        z_ref[0, pl.ds(r0, CONV_ROWS), :] = (v * x1).astype(BF16)
        return 0

    lax.fori_loop(0, SEQ // CONV_ROWS, rows, 0)


def hyena_prep(p3, conv_w):
    b = p3.shape[0]
    blk = pl.BlockSpec((1, SEQ, DG), lambda i: (i, 0, 0))
    out = jax.ShapeDtypeStruct((b, SEQ, DG), BF16)
    return pl.pallas_call(
        _hy_prep_body, grid=(b,),
        in_specs=[pl.BlockSpec((1, SEQ, 3 * DG), lambda i: (i, 0, 0)), pl.BlockSpec(conv_w.shape, lambda i: (0, 0))],
        out_specs=[blk, blk], out_shape=[out, out],
        scratch_shapes=[pltpu.VMEM((SEQ + 2 * CONV_HALO, 3 * DG), F32)],
        compiler_params=_params("parallel"), name="hyena_prep",
    )(p3, conv_w)


HY_FB = 512


def _hy_conv_body(z_ref, x0_ref, t1_ref, t2_ref, kr_ref, ki_ref, kny_ref, fb_ref, o_ref, y_s):
    z = z_ref[0]

    def spectrum(kb, _):
        rows = pl.ds(pl.multiple_of(kb * HY_FB, HY_FB), HY_FB)
        rows_s = pl.ds(pl.multiple_of(SEQ + kb * HY_FB, HY_FB), HY_FB)
        zr = _dot(t1_ref[rows, :], z)
        zi = _dot(t1_ref[rows_s, :], z)
        krow = kb * HY_FB + lax.broadcasted_iota(I32, (HY_FB, 1), 0)
        wk = jnp.where(krow == 0, 1.0 / NFFT, 2.0 / NFFT)
        kr = kr_ref[rows, :]
        ki = ki_ref[rows, :]
        y_s[rows, :] = ((zr * kr - zi * ki) * wk).astype(BF16)
        y_s[rows_s, :] = ((zr * ki + zi * kr) * wk).astype(BF16)
        return 0

    lax.fori_loop(0, SEQ // HY_FB, spectrum, 0)

    zny = jnp.sum(z.astype(F32) * (1 - 2 * (lax.broadcasted_iota(I32, (SEQ, DG), 0) & 1)).astype(F32),
                  axis=0, keepdims=True)
    nyq = zny * kny_ref[0:1, :] * (1.0 / NFFT)

    def synth(tb, _):
        rows = pl.ds(pl.multiple_of(tb * HY_FB, HY_FB), HY_FB)
        conv = _dot(t2_ref[rows, :], y_s[...])
        sgn = (1 - 2 * (lax.broadcasted_iota(I32, (HY_FB, DG), 0) & 1)).astype(F32)
        zf = z_ref[0, rows, :].astype(F32)
        o_ref[0, rows, :] = (x0_ref[0, rows, :].astype(F32) * (conv + nyq * sgn + zf * fb_ref[...])).astype(BF16)
        return 0

    lax.fori_loop(0, SEQ // HY_FB, synth, 0)


def hyena_conv(z3, x03, dft_rows, dft_cols, kr, ki, kny, fbias):
    b = z3.shape[0]
    seq_blk = pl.BlockSpec((1, SEQ, DG), lambda i: (i, 0, 0))
    once = lambda a: pl.BlockSpec(a.shape, lambda i: (0,) * a.ndim, pipeline_mode=pl.Buffered(1))
    return pl.pallas_call(
        _hy_conv_body, grid=(b,),
        in_specs=[seq_blk, seq_blk, once(dft_rows), once(dft_cols), once(kr), once(ki), once(kny), once(fbias)],
        out_specs=seq_blk, out_shape=jax.ShapeDtypeStruct((b, SEQ, DG), BF16),
        scratch_shapes=[pltpu.VMEM((2 * SEQ, DG), BF16)],
        compiler_params=_params("parallel"), name="hyena_conv",
    )(z3, x03, dft_rows, dft_cols, kr, ki, kny, fbias)


N_MCH = SEQ // M_CHUNK
MQ = M_CHUNK


def _head_lane_vec(rows8, base):
    lane_head = lax.broadcasted_iota(I32, (1, DG), 1) // HD
    out = jnp.zeros((1, DG), F32)
    for h in range(N_HEADS):
        out = jnp.where(lane_head == h, rows8[base + h:base + h + 1, :], out)
    return out


def _mamba_body(z_ref, xbc_ref, dtc_ref, cw_ref, cb_ref, dtb_ref, a_ref, dsk_ref, nw_ref, tri_ref, bd_ref,
                o_ref, pad, xs_s, b_s, c_s, y_s, u_s, dec_s, cw_s, yo_s, st_s):
    _fill_padded(pad, xbc_ref, 2 * DG)

    def conv_rows(c, _):
        r0 = pl.multiple_of(c * CONV_ROWS, CONV_ROWS)
        for g in range(4):
            lanes = slice(g * V7X_LANES, (g + 1) * V7X_LANES)
            u = _silu(_dwconv_rows(pad, cw_ref, r0, lanes, M_CONV) + cb_ref[:, lanes])
            if g < 2:
                xs_s[pl.ds(r0, CONV_ROWS), lanes] = u
            elif g == 2:
                b_s[pl.ds(r0, CONV_ROWS), :] = u.astype(BF16)
            else:
                c_s[pl.ds(r0, CONV_ROWS), :] = u.astype(BF16)
        return 0

    lax.fori_loop(0, SEQ // CONV_ROWS, conv_rows, 0)

    li = lax.broadcasted_iota(I32, (MQ, MQ), 0)
    si = lax.broadcasted_iota(I32, (MQ, MQ), 1)
    lower = si <= li
    upper = si >= li
    bdmask = bd_ref[...]

    def chunk(c, _):
        r0 = pl.multiple_of(c * MQ, MQ)
        dt = _softplus(dtc_ref[0, c] + dtb_ref[...])
        a = dt * a_ref[...]
        cum = _dot01_rhs(a, tri_ref[...])
        tot = cum[:, MQ - 1:MQ]
        suf = tot - cum + a
        row_dir = lax.broadcasted_iota(I32, (8, MQ), 0) // N_HEADS
        seg = jnp.where(row_dir == 0, cum, suf)
        wgt = jnp.exp(tot - seg) * dt
        cols = jnp.concatenate([seg, jnp.exp(seg)], axis=0).T
        x = xs_s[pl.ds(r0, MQ), :]
        xb = x.astype(BF16)
        bm = b_s[pl.ds(r0, MQ), :]
        cm = c_s[pl.ds(r0, MQ), :]
        cmf = cm.astype(F32)
        bt = bm.astype(F32).T
        ydiag = []
        for h in range(N_HEADS):
            g = h // 2
            cb = lax.dot_general(cm[:, g * M_STATE:(g + 1) * M_STATE], bm[:, g * M_STATE:(g + 1) * M_STATE],
                                 _NT, preferred_element_type=F32)
            lf = jnp.where(lower, jnp.exp(jnp.minimum(cols[:, h:h + 1] - seg[h:h + 1, :], 0.0)), 0.0)
            lb = jnp.where(upper, jnp.exp(jnp.minimum(cols[:, 4 + h:5 + h] - seg[4 + h:5 + h, :], 0.0)), 0.0)
            m = cb * (lf * dt[h:h + 1, :] + lb * dt[4 + h:5 + h, :])
            ydiag.append(_dot(m.astype(BF16), xb[:, h * HD:(h + 1) * HD]))
        y_s[pl.ds(r0, MQ), :] = jnp.concatenate(ydiag, axis=1)
        for d in range(2):
            bwt = jnp.concatenate([bt[(h // 2) * M_STATE:(h // 2 + 1) * M_STATE, :] * wgt[4 * d + h:4 * d + h + 1, :]
                                   for h in range(N_HEADS)], axis=0)
            u_s[d, c] = (_dot(bwt.astype(BF16), xb) * bdmask).astype(BF16)
            dec_s[d, c] = jnp.broadcast_to(_head_lane_vec(jnp.exp(tot), 4 * d), (V7X_SUBLANES, DG))
            cw_s[d, c] = jnp.concatenate(
                [cmf[:, (h // 2) * M_STATE:(h // 2 + 1) * M_STATE] * cols[:, 8 + 4 * d + h:9 + 4 * d + h]
                 for h in range(N_HEADS)], axis=1).astype(BF16)
        return 0

    lax.fori_loop(0, N_MCH, chunk, 0, unroll=2)

    st_s[...] = jnp.zeros(st_s.shape, F32)

    def scan(i, _):
        for d in range(2):
            c = i if d == 0 else N_MCH - 1 - i
            st = st_s[d]
            yo_s[d, pl.ds(pl.multiple_of(c * MQ, MQ), MQ), :] = _dot(cw_s[d, c], st.astype(BF16))
            st_s[d] = st * dec_s[d, c][0:1, :] + u_s[d, c].astype(F32)
        return 0

    lax.fori_loop(0, N_MCH, scan, 0)

    def finish(c, _):
        r0 = pl.multiple_of(c * CONV_ROWS, CONV_ROWS)
        rows = pl.ds(r0, CONV_ROWS)
        y = y_s[rows, :] + yo_s[0, rows, :] + yo_s[1, rows, :] + xs_s[rows, :] * dsk_ref[...]
        y = y * _silu(z_ref[0, rows, :].astype(F32))
        o_ref[0, pl.ds(r0, CONV_ROWS), :] = (_rms(y) * nw_ref[...]).astype(BF16)
        return 0

    lax.fori_loop(0, SEQ // CONV_ROWS, finish, 0)


def mamba2(z3, xbc3, dtc4, conv_w, conv_b, dt_bias_col, a_col, dskip_lanes, norm_w, tri_incl, bdmask):
    b = z3.shape[0]
    full = lambda a: pl.BlockSpec(a.shape, lambda i: (0,) * a.ndim)
    return pl.pallas_call(
        _mamba_body, grid=(b,),
        in_specs=[pl.BlockSpec((1, SEQ, DG), lambda i: (i, 0, 0)),
                  pl.BlockSpec((1, SEQ, 2 * DG), lambda i: (i, 0, 0)),
                  pl.BlockSpec((1, N_MCH, 8, MQ), lambda i: (i, 0, 0, 0)),
                  full(conv_w), full(conv_b), full(dt_bias_col), full(a_col), full(dskip_lanes), full(norm_w),
                  full(tri_incl), full(bdmask)],
        out_specs=pl.BlockSpec((1, SEQ, DG), lambda i: (i, 0, 0)),
        out_shape=jax.ShapeDtypeStruct((b, SEQ, DG), BF16),
        scratch_shapes=[pltpu.VMEM((SEQ + 2 * CONV_HALO, 2 * DG), F32),
                        pltpu.VMEM((SEQ, DG), F32),
                        pltpu.VMEM((SEQ, 2 * M_STATE), BF16),
                        pltpu.VMEM((SEQ, 2 * M_STATE), BF16),
                        pltpu.VMEM((SEQ, DG), F32),
                        pltpu.VMEM((2, N_MCH, DG, DG), BF16),
                        pltpu.VMEM((2, N_MCH, V7X_SUBLANES, DG), F32),
                        pltpu.VMEM((2, N_MCH, MQ, DG), BF16),
                        pltpu.VMEM((2, SEQ, DG), F32),
                        pltpu.VMEM((2, DG, DG), F32)],
        compiler_params=_params("parallel"), name="mamba2",
    )(z3, xbc3, dtc4, conv_w, conv_b, dt_bias_col, a_col, dskip_lanes, norm_w, tri_incl, bdmask)


A_TQ = 128
A_ROWS = 256
A_KW = A_TQ + 2 * A_BAND


def _attn_bias_body(ids_ref, rb_ref, o_ref):
    ids = ids_ref[0]
    for h in range(N_HEADS):
        acc = jnp.full(ids.shape, NEG_BIG, F32)
        for bkt in range(N_BUCKETS):
            acc = jnp.where(ids == bkt, rb_ref[bkt, h], acc)
        o_ref[h, 0] = acc


def attention_bias_table(bucket_ids, rel_bias):
    nvar, tq, w = bucket_ids.shape
    return pl.pallas_call(
        _attn_bias_body, grid=(nvar,),
        in_specs=[pl.BlockSpec((1, tq, w), lambda v: (v, 0, 0)),
                  pl.BlockSpec(memory_space=pltpu.SMEM)],
        out_specs=pl.BlockSpec((N_HEADS, 1, tq, w), lambda v: (0, v, 0, 0)),
        out_shape=jax.ShapeDtypeStruct((N_HEADS, nvar, tq, w), F32),
        compiler_params=_params("parallel"), name="attention_bias_table",
    )(bucket_ids, rel_bias)


A_SLABS = 3 * DG // V7X_LANES
A_QBLOCKS = SEQ // A_TQ


A_SUB4 = SEQ // 4
A_SUB16 = SEQ // 16


def _attn_body(at_ref, b1_ref, b4_ref, b16_ref, o_ref, qkv_s, x4_s, x16_s, y16_s, y4_s, part_o, part_l):
    def fill(c, _):
        r0 = pl.multiple_of(c * A_ROWS, A_ROWS)
        for s in range(A_SLABS):
            qkv_s[s, pl.ds(r0, A_ROWS), :] = at_ref[0, pl.ds(r0, A_ROWS), s * V7X_LANES:(s + 1) * V7X_LANES].astype(F32)
        return 0

    lax.fori_loop(0, SEQ // A_ROWS, fill, 0)

    def deinterleave(s, _):
        for r4 in range(4):
            for c in range(A_SUB4 // A_ROWS):
                x4_s[s, pl.ds(r4 * A_SUB4 + c * A_ROWS, A_ROWS), :] = \
                    qkv_s[s, pl.ds(r4 + 4 * c * A_ROWS, A_ROWS, stride=4), :]
        for r in range(16):
            x16_s[s, pl.ds(r * A_SUB16, A_SUB16), :] = \
                x4_s[s, pl.ds((r % 4) * A_SUB4 + r // 4, A_SUB16, stride=4), :].astype(BF16)
        return 0

    lax.fori_loop(0, A_SLABS, deinterleave, 0)
    first_head = lax.broadcasted_iota(I32, (A_TQ, V7X_LANES), 1) < HD

    def run_pattern(pat, dil, bias_ref):
        n = SEQ // dil if dil < 16 else SEQ
        nblk = n // A_TQ
        w = A_KW

        def block(it, _):
            r = it // nblk
            i = it - r * nblk
            q0 = i * A_TQ
            k0 = jnp.clip(q0 - A_BAND, 0, n - w)
            var = jnp.where(i == 0, 0, jnp.where(i == nblk - 1, 2, 1))
            for hp in range(2):
                lanes = [slice((2 * part + hp) * V7X_LANES, (2 * part + hp + 1) * V7X_LANES) for part in range(3)]
                if dil == 4:
                    qrows = pl.ds(r + dil * q0, A_TQ, stride=dil)
                    krows = pl.ds(r + dil * k0, w, stride=dil)
                    q2 = qkv_s[hp, qrows, :]
                    k2 = qkv_s[2 + hp, krows, :].astype(BF16)
                    v2 = qkv_s[4 + hp, krows, :].astype(BF16)
                else:
                    qrows = pl.ds(pl.multiple_of(q0, A_TQ), A_TQ)
                    krows = pl.ds(pl.multiple_of(k0, A_BAND), w)
                    if dil == 1:
                        q2, k2, v2 = at_ref[0, qrows, lanes[0]], at_ref[0, krows, lanes[1]], at_ref[0, krows, lanes[2]]
                    else:
                        q2, k2, v2 = x16_s[hp, qrows, :], x16_s[2 + hp, krows, :], x16_s[4 + hp, krows, :]
                q2 = (q2 * (HD ** -0.5)).astype(BF16)
                outs, lses = [], []
                for hh in range(2):
                    keep = first_head if hh == 0 else jnp.logical_not(first_head)
                    qm = jnp.where(keep, q2, jnp.zeros_like(q2))
                    s = lax.dot_general(qm, k2, _NT, preferred_element_type=F32) + bias_ref[2 * hp + hh, var]
                    m = jnp.max(s, axis=1, keepdims=True)
                    p = jnp.exp(s - m)
                    den = jnp.sum(p, axis=1, keepdims=True)
                    outs.append(_dot(p.astype(BF16), v2) / den)
                    lses.append(m + jnp.log(den))
                o_new = jnp.where(first_head, outs[0], outs[1])
                l_new = jnp.where(first_head, lses[0], lses[1])
                if dil == 16:
                    y16_s[0, hp, qrows, :] = o_new
                    y16_s[1, hp, qrows, :] = l_new
                else:
                    part_o[pat, hp, qrows, :] = o_new
                    part_l[pat, hp, qrows, :] = l_new
            return 0

        lax.fori_loop(0, A_QBLOCKS, block, 0, unroll=4)

    for pat, (dil, bias_ref) in enumerate(zip(A_DILS, (b1_ref, b4_ref, b16_ref))):
        run_pattern(pat, dil, bias_ref)

    for a, dst in enumerate((part_o, part_l)):
        for hp in range(2):
            for r in range(16):
                y4_s[a, hp, pl.ds((r % 4) * A_SUB4 + r // 4, A_SUB16, stride=4), :] = \
                    y16_s[a, hp, pl.ds(r * A_SUB16, A_SUB16), :]
            for r4 in range(4):
                for c in range(A_SUB4 // A_ROWS):
                    dst[2, hp, pl.ds(r4 + 4 * c * A_ROWS, A_ROWS, stride=4), :] = \
                        y4_s[a, hp, pl.ds(r4 * A_SUB4 + c * A_ROWS, A_ROWS), :]

    def finish(c, _):
        rows = pl.ds(pl.multiple_of(c * A_TQ, A_TQ), A_TQ)
        for hp in range(2):
            ls = [part_l[pat, hp, rows, :] for pat in range(len(A_DILS))]
            mx = jnp.maximum(jnp.maximum(ls[0], ls[1]), ls[2])
            ws = [jnp.exp(l - mx) for l in ls]
            num = ws[0] * part_o[0, hp, rows, :] + ws[1] * part_o[1, hp, rows, :] + ws[2] * part_o[2, hp, rows, :]
            o_ref[0, rows, hp * V7X_LANES:(hp + 1) * V7X_LANES] = (num / (ws[0] + ws[1] + ws[2])).astype(BF16)
        return 0

    lax.fori_loop(0, SEQ // A_TQ, finish, 0)


def dilated_attention(at3, bias1, bias4, bias16):
    b = at3.shape[0]
    full = lambda a: pl.BlockSpec(a.shape, lambda i: (0,) * a.ndim)
    return pl.pallas_call(
        _attn_body, grid=(b,),
        in_specs=[pl.BlockSpec((1, SEQ, 3 * DG), lambda i: (i, 0, 0)), full(bias1), full(bias4), full(bias16)],
        out_specs=pl.BlockSpec((1, SEQ, DG), lambda i: (i, 0, 0)),
        out_shape=jax.ShapeDtypeStruct((b, SEQ, DG), BF16),
        scratch_shapes=[pltpu.VMEM((A_SLABS, SEQ, V7X_LANES), F32),
                        pltpu.VMEM((A_SLABS, SEQ, V7X_LANES), F32),
                        pltpu.VMEM((A_SLABS, SEQ, V7X_LANES), BF16),
                        pltpu.VMEM((2, 2, SEQ, V7X_LANES), F32),
                        pltpu.VMEM((2, 2, SEQ, V7X_LANES), F32),
                        pltpu.VMEM((len(A_DILS), 2, SEQ, V7X_LANES), F32),
                        pltpu.VMEM((len(A_DILS), 2, SEQ, V7X_LANES), F32)],
        compiler_params=_params("parallel"), name="dilated_attention",
    )(at3, bias1, bias4, bias16)


H_BLK = 256
H_CPB = H_BLK // H_CHUNK
N_HBLK = SEQ // H_BLK
N_HCH = SEQ // H_CHUNK


def _chunk_bcast(x, row):
    c = x.shape[1]
    x3 = x.reshape(H_CPB, H_CHUNK, c)
    return jnp.broadcast_to(x3[:, row:row + 1, :], (H_CPB, H_CHUNK, c)).reshape(H_BLK, c)


def _hgrn_body(p_ref, lb_ref, nw_ref, tin_ref, o_ref, qm_s, ut_s, oi_s, dec_s, oe_s, st_s):
    li = lax.broadcasted_iota(I32, (H_BLK, H_BLK), 0)
    si = lax.broadcasted_iota(I32, (H_BLK, H_BLK), 1)
    same = (li // H_CHUNK) == (si // H_CHUNK)
    mask_f = same & (si <= li)
    mask_b = same & (si >= li)
    lane_head = lax.broadcasted_iota(I32, (1, DG), 1) // HD

    def block(bi, _):
        r0 = pl.multiple_of(bi * H_BLK, H_BLK)
        rows = pl.ds(r0, H_BLK)
        q = _silu(p_ref[0, rows, 0:DG].astype(F32))
        v = p_ref[0, rows, 3 * DG:4 * DG]
        scores = [None] * N_HEADS
        for d in range(2):
            fpre = p_ref[0, rows, (1 + d) * DG:(2 + d) * DG].astype(F32)
            lb = lb_ref[d:d + 1, :]
            sg = jax.nn.sigmoid(fpre)
            g = jnp.log(lb + (1.0 - lb) * sg)
            k = (1.0 - lb) * (1.0 - sg)
            gi = _dot01_2(tin_ref[...], g)
            glast = _chunk_bcast(gi, H_CHUNK - 1)
            if d == 0:
                gc = gi
                gref = _chunk_bcast(gi, H_CHUNK // 2 - 1)
                msk = mask_f
            else:
                gc = glast - gi + g
                gref = _chunk_bcast(gc, H_CHUNK // 2)
                msk = mask_b
            qe = (q * jnp.exp(gc - gref)).astype(BF16)
            ke = (k * jnp.exp(gref - gc)).astype(BF16)
            for h in range(N_HEADS):
                hs = slice(h * HD, (h + 1) * HD)
                sc = jnp.where(msk, lax.dot_general(qe[:, hs], ke[:, hs], _NT, preferred_element_type=F32), 0.0)
                scores[h] = sc if d == 0 else scores[h] + sc
            qd = q * jnp.exp(gc)
            kd = (k * jnp.exp(glast - gc)).astype(BF16)
            for j in range(H_CPB):
                c = bi * H_CPB + j
                cr = slice(j * H_CHUNK, (j + 1) * H_CHUNK)
                qm_s[d, c] = jnp.concatenate([jnp.where(lane_head == h, qd[cr, :], 0.0) for h in range(N_HEADS)],
                                             axis=0).astype(BF16)
                ut = lax.dot_general(v[cr, :], kd[cr, :], _TN, preferred_element_type=F32)
                packed = ut[0:HD, :]
                for h in range(1, N_HEADS):
                    packed = jnp.where(lane_head == h, ut[h * HD:(h + 1) * HD, :], packed)
                ut_s[d, c] = packed.astype(BF16)
                dec_s[d, c] = jnp.broadcast_to(jnp.exp(glast[j * H_CHUNK:j * H_CHUNK + 1, :]), (V7X_SUBLANES, DG))
        for h in range(N_HEADS):
            oi_s[h, rows, :] = _dot(scores[h].astype(BF16), v[:, h * HD:(h + 1) * HD])
        return 0

    lax.fori_loop(0, N_HBLK, block, 0)

    st_s[...] = jnp.zeros(st_s.shape, F32)

    def step(i, _):
        for d in range(2):
            c = i if d == 0 else N_HCH - 1 - i
            rows = pl.ds(pl.multiple_of(c * H_CHUNK, H_CHUNK), H_CHUNK)
            st = st_s[d]
            inter = lax.dot_general(qm_s[d, c], st.astype(BF16), _NT, preferred_element_type=F32)
            for h in range(N_HEADS):
                oe_s[d, h, rows, :] = inter[h * H_CHUNK:(h + 1) * H_CHUNK, :]
            st_s[d] = st * dec_s[d, c][0:1, :] + ut_s[d, c].astype(F32)
        return 0

    lax.fori_loop(0, N_HCH, step, 0, unroll=2)

    def finish(c, _):
        r0 = pl.multiple_of(c * CONV_ROWS, CONV_ROWS)
        rows = pl.ds(r0, CONV_ROWS)
        gate = _silu(p_ref[0, rows, 4 * DG:5 * DG].astype(F32))
        outs = [_rms(oi_s[h, rows, :] + oe_s[0, h, rows, :] + oe_s[1, h, rows, :]) for h in range(N_HEADS)]
        o_ref[0, rows, :] = (jnp.concatenate(outs, axis=1) * nw_ref[...] * gate).astype(BF16)
        return 0

    lax.fori_loop(0, SEQ // CONV_ROWS, finish, 0)


def hgrn2(p3, lb2, norm_w_lanes, tri_in_chunk):
    b = p3.shape[0]
    full = lambda a: pl.BlockSpec(a.shape, lambda i: (0,) * a.ndim)
    return pl.pallas_call(
        _hgrn_body, grid=(b,),
        in_specs=[pl.BlockSpec((1, SEQ, 5 * DG), lambda i: (i, 0, 0)), full(lb2), full(norm_w_lanes),
                  full(tri_in_chunk)],
        out_specs=pl.BlockSpec((1, SEQ, DG), lambda i: (i, 0, 0)),
        out_shape=jax.ShapeDtypeStruct((b, SEQ, DG), BF16),
        scratch_shapes=[pltpu.VMEM((2, N_HCH, N_HEADS * H_CHUNK, DG), BF16),
                        pltpu.VMEM((2, N_HCH, HD, DG), BF16),
                        pltpu.VMEM((N_HEADS, SEQ, HD), F32),
                        pltpu.VMEM((2, N_HCH, V7X_SUBLANES, DG), F32),
                        pltpu.VMEM((2, N_HEADS, SEQ, HD), F32),
                        pltpu.VMEM((2, HD, DG), F32)],
        compiler_params=_params("parallel"), name="hgrn2",
    )(p3, lb2, norm_w_lanes, tri_in_chunk)


@functools.lru_cache(maxsize=None)
def _tables():
    t = {}
    k = np.arange(SEQ, dtype=np.int64)
    ang = 2.0 * np.pi * ((k[:, None] * k[None, :]) % NFFT).astype(np.float64) / NFFT
    t["cos"] = np.cos(ang).astype(np.float32)
    t["sin"] = np.sin(ang).astype(np.float32)
    t["dft_rows"] = np.concatenate([t["cos"], t["sin"]], axis=0).astype(ml_dtypes.bfloat16)
    t["dft_cols"] = np.concatenate([t["cos"], t["sin"]], axis=1).astype(ml_dtypes.bfloat16)
    tt = np.linspace(0.0, 1.0, SEQ, dtype=np.float32)[:, None]
    bands = (HY_POS_DIM - 1) // 2
    ang_pos = (2.0 * math.pi * np.arange(SEQ, dtype=np.float32) / SEQ).astype(np.float32)
    f = np.linspace(1e-4, bands - 1, bands, dtype=np.float32)
    a2 = (ang_pos[:, None] * f[None, :]).astype(np.float32)
    z = np.concatenate([tt, np.cos(a2), -np.sin(a2)], axis=-1).astype(np.float32)
    zp = np.zeros((SEQ, V7X_LANES), np.float32)
    zp[:, :HY_POS_DIM] = z
    t["zpos"] = zp
    max_decay = math.log(1e-2) / 0.3
    min_decay = math.log(1e-2) / 1.5
    deltas = np.abs(np.linspace(min_decay, max_decay, DG, dtype=np.float32))
    t["decay"] = np.exp(-tt * deltas[None, :]).astype(np.float32)
    i128 = np.arange(V7X_LANES)
    t["u128"] = (i128[:, None] < i128[None, :]).astype(np.float32)
    im = np.arange(M_CHUNK)
    t["tri_incl"] = (im[:, None] <= im[None, :]).astype(np.float32)
    ib = np.arange(H_BLK)
    t["tri_in_chunk"] = ((ib[:, None] // H_CHUNK == ib[None, :] // H_CHUNK)
                         & (ib[None, :] <= ib[:, None])).astype(np.float32)
    idg = np.arange(DG)
    t["bdmask"] = (idg[:, None] // HD == idg[None, :] // HD).astype(np.float32)
    def bucket(rel):
        nb = N_BUCKETS // 2
        max_exact = nb // 2
        ret = (rel > 0).astype(np.int64) * nb
        n = np.abs(rel)
        nf = np.maximum(n, 1).astype(np.float64)
        large = max_exact + (np.log(nf / max_exact) / math.log(MAX_DISTANCE / max_exact)
                             * (nb - max_exact)).astype(np.int64)
        large = np.minimum(large, nb - 1)
        return ret + np.where(n < max_exact, n, large)

    for dil in A_DILS:
        n = SEQ // dil
        qi = np.arange(A_TQ)[:, None]
        kj = np.arange(A_KW)[None, :]
        ids = []
        for s0 in (0, -A_BAND, -(A_KW - A_TQ)):
            kk = kj + s0
            rel = kk - qi
            ok = np.abs(rel) <= A_BAND
            if n == A_TQ:
                ok &= (kk >= 0) & (kk < A_TQ)
            ids.append(np.where(ok, bucket(rel * dil), -1))
        t[f"bucket{dil}"] = np.stack(ids).astype(np.int32)
    return t


def kernel(x, w_in, w_out, norm_mix_w, norm_ffn_w, hy_conv_w, hy_pos_w1, hy_pos_b1, hy_pos_w2, hy_pos_b2,
           hy_sin_freq, hy_pos_w3, hy_filt_bias, m_conv_w, m_conv_b, m_dt_bias, m_A_log, m_D, m_norm_w, rel_bias,
           hg_lb, hg_norm_w, router_w, moe_w_gate, moe_w_up, moe_w_down, final_norm_w):
    b = x.shape[0]
    t = b * SEQ
    tb = _tables()
    cos_f32 = jnp.asarray(tb["cos"])
    sin_f32 = jnp.asarray(tb["sin"])
    dft_rows = jnp.asarray(tb["dft_rows"])
    dft_cols = jnp.asarray(tb["dft_cols"])
    u128 = jnp.asarray(tb["u128"]).astype(BF16)
    tri_incl = jnp.asarray(tb["tri_incl"]).astype(BF16)
    tri_in_chunk = jnp.asarray(tb["tri_in_chunk"]).astype(BF16)
    bdmask = jnp.asarray(tb["bdmask"])
    attn_bias = [attention_bias_table(jnp.asarray(tb[f"bucket{d}"]), rel_bias.astype(F32)) for d in A_DILS]

    sm = jax.nn.softmax(hg_lb.astype(F32), axis=0)
    lower_bounds = jnp.cumsum(sm, axis=0) - sm[:1]

    xa = x.reshape(t, D_MODEL)
    for l in range(DEPTH):
        wl = w_in[l]
        w_main = jnp.concatenate([wl[:, 0:768], wl[:, 768:1024], wl[:, 1024:1536], wl[:, 1544:2312],
                                  wl[:, 2312:3592]], axis=1).astype(BF16)
        w_dt_rows = wl[:, 1536:1544].T.astype(BF16)
        hy, mz, mx, at, hg, dtc = in_projection(xa, norm_mix_w[l][None, :], w_main, w_dt_rows)

        w1p = jnp.zeros((V7X_LANES, HY_HID), F32).at[:HY_POS_DIM].set(hy_pos_w1[l])
        kr, ki, kny = hyena_filter_spectrum(
            jnp.asarray(tb["zpos"]), w1p, hy_pos_b1[l][None, :], hy_pos_w2[l], hy_pos_b2[l][None, :],
            hy_sin_freq[l][None, :], hy_pos_w3[l], jnp.asarray(tb["decay"]), cos_f32, sin_f32)
        z3, x03 = hyena_prep(hy.reshape(b, SEQ, 3 * DG), hy_conv_w[l])
        ya = hyena_conv(z3, x03, dft_rows, dft_cols, kr, ki, kny, hy_filt_bias[l][None, :]).reshape(t, DG)

        a_col = (-jnp.exp(m_A_log[l].astype(F32))).reshape(8, 1)
        yb = mamba2(mz.reshape(b, SEQ, DG), mx.reshape(b, SEQ, 2 * DG), dtc.reshape(b, N_MCH, 8, MQ),
                    m_conv_w[l], m_conv_b[l][None, :], m_dt_bias[l].reshape(8, 1), a_col,
                    jnp.repeat(m_D[l].astype(F32), HD)[None, :], m_norm_w[l][None, :], tri_incl, bdmask).reshape(t, DG)

        yc = dilated_attention(at.reshape(b, SEQ, 3 * DG), *attn_bias).reshape(t, DG)

        lbl = lower_bounds[l]
        yd = hgrn2(hg.reshape(b, SEQ, 5 * DG), lbl, jnp.tile(hg_norm_w[l], N_HEADS)[None, :],
                   tri_in_chunk).reshape(t, DG)

        xo, xn = out_projection(xa, ya, yb, yc, yd, w_out[l].reshape(4, DG, D_MODEL).astype(BF16),
                                norm_ffn_w[l][None, :])
        xn3 = xn.reshape(b, SEQ, D_MODEL)
        rw_rows = router_w[l].T.astype(F32)
        rw_hi = rw_rows.astype(BF16)
        rw_lo = (rw_rows - rw_hi.astype(F32)).astype(BF16)
        rank, gate, seg = router(xo.reshape(b, SEQ, D_MODEL), norm_ffn_w[l][None, :], rw_hi, rw_lo, u128)
        seg_flat = seg[:, :, :MOE_SEG_STRIDE].reshape(-1)
        xe = moe_gather(seg_flat, xn3, rank)
        ye = moe_experts(xe, moe_w_gate, moe_w_up, moe_w_down, l)
        xa = moe_scatter(seg_flat, ye, rank, gate, xo.reshape(b, SEQ, D_MODEL), final_norm_w[None, :],
                         final=(l == DEPTH - 1)).reshape(t, D_MODEL)
    return xa.reshape(b, SEQ, D_MODEL)
```

```python
import functools
import math

import ml_dtypes
import numpy as np
import jax
import jax.numpy as jnp
from jax import lax
from jax.experimental import pallas as pl
from jax.experimental.pallas import tpu as pltpu

F32 = jnp.float32
BF16 = jnp.bfloat16
I32 = jnp.int32

D_MODEL = 1024
SEQ = 2048
DEPTH = 2
DG = 256
N_HEADS = 4
HD = 64
HY_POS_DIM = 33
HY_HID = 64
M_CONV = 5
M_STATE = 64
M_CHUNK = 128
H_CHUNK = 32
A_BAND = 64
A_DILS = (1, 4, 16)
N_BUCKETS = 32
MAX_DISTANCE = 1024
N_EXPERTS = 16
CAP = 2 * SEQ // N_EXPERTS
D_FF = 1024
EPS = 1e-6
NFFT = 2 * SEQ

V7X_LANES = 128
V7X_SUBLANES = 8
V7X_VMEM_LIMIT_BYTES = 56 * 1024 * 1024

NEG_BIG = -1e30

_NT = (((1,), (1,)), ((), ()))
_TN = (((0,), (0,)), ((), ()))


def _params(*sem):
    return pltpu.CompilerParams(dimension_semantics=sem, vmem_limit_bytes=V7X_VMEM_LIMIT_BYTES)


def _dot(a, b):
    return jnp.dot(a, b, preferred_element_type=F32)


def _dot_hi(a, b):
    return jnp.dot(a, b, preferred_element_type=F32, precision=lax.Precision.HIGHEST)


def _dot01_2(t_bf16, x):
    x1 = x.astype(BF16)
    x2 = (x - x1.astype(F32)).astype(BF16)
    return _dot(t_bf16, x1) + _dot(t_bf16, x2)


def _dot01_rhs(x, t_bf16):
    x1 = x.astype(BF16)
    r1 = x - x1.astype(F32)
    x2 = r1.astype(BF16)
    x3 = (r1 - x2.astype(F32)).astype(BF16)
    return _dot(x1, t_bf16) + _dot(x2, t_bf16) + _dot(x3, t_bf16)


def _silu(x):
    return x * jax.nn.sigmoid(x)


def _softplus(x):
    return jnp.maximum(x, 0.0) + jnp.log(1.0 + jnp.exp(-jnp.abs(x)))


def _rms(x):
    return x * lax.rsqrt(jnp.mean(x * x, axis=-1, keepdims=True) + EPS)


TM_PROJ = 1024
_HY0, _MZ0, _MX0, _AT0, _HG0, _PEND = 0, 768, 1024, 1536, 2304, 3584


def _inproj_body(x_ref, nw_ref, w_ref, wdt_ref, hy_ref, mz_ref, mx_ref, at_ref, hg_ref, dtc_ref):
    x = x_ref[...]
    hn = (_rms(x) * nw_ref[...]).astype(BF16)
    hy_ref[...] = _dot(hn, w_ref[:, _HY0:_MZ0]).astype(BF16)
    mz_ref[...] = _dot(hn, w_ref[:, _MZ0:_MX0]).astype(BF16)
    mx_ref[...] = _dot(hn, w_ref[:, _MX0:_AT0]).astype(BF16)
    at_ref[...] = _dot(hn, w_ref[:, _AT0:_HG0]).astype(BF16)
    hg_ref[...] = _dot(hn, w_ref[:, _HG0:_PEND]).astype(BF16)
    dt_rows = lax.dot_general(wdt_ref[...], hn, _NT, preferred_element_type=F32)
    for j in range(TM_PROJ // M_CHUNK):
        dtc_ref[j] = dt_rows[:, j * M_CHUNK:(j + 1) * M_CHUNK]


def in_projection(x, norm_w, w_main, w_dt_rows):
    t = x.shape[0]
    tm = TM_PROJ
    row = lambda w: pl.BlockSpec((tm, w), lambda i: (i, 0))
    full = lambda a: pl.BlockSpec(a.shape, lambda i: (0,) * a.ndim)
    in_specs = [row(D_MODEL), full(norm_w), full(w_main), full(w_dt_rows)]
    widths = (768, 256, 512, 768, 1280)
    out_shape = [jax.ShapeDtypeStruct((t, w), BF16) for w in widths]
    out_shape.append(jax.ShapeDtypeStruct((t // M_CHUNK, 8, M_CHUNK), F32))
    out_specs = [row(w) for w in widths] + [pl.BlockSpec((tm // M_CHUNK, 8, M_CHUNK), lambda i: (i, 0, 0))]
    return pl.pallas_call(
        _inproj_body, grid=(t // tm,), in_specs=in_specs, out_specs=out_specs, out_shape=out_shape,
        compiler_params=_params("parallel"), name="in_projection",
    )(x, norm_w, w_main, w_dt_rows)


TM_OUT = 512


def _outproj_body(x_ref, ya_ref, yb_ref, yc_ref, yd_ref, w_ref, nw_ref, rwh_ref, rwl_ref, xo_ref, xn_ref, lg_ref):
    x = x_ref[...]
    acc = x + _dot(ya_ref[...], w_ref[0]) + _dot(yb_ref[...], w_ref[1])
    acc = acc + _dot(yc_ref[...], w_ref[2]) + _dot(yd_ref[...], w_ref[3])
    xo_ref[...] = acc
    xn = _rms(acc) * nw_ref[...]
    xh = xn.astype(BF16)
    xn_ref[...] = xh
    xl = (xn - xh.astype(F32)).astype(BF16)
    nt = lambda w, a: lax.dot_general(w, a, _NT, preferred_element_type=F32)
    lg_ref[...] = nt(rwh_ref[...], xh) + nt(rwh_ref[...], xl) + nt(rwl_ref[...], xh)


def out_projection(x, ya, yb, yc, yd, w_out4, norm_w, rw_hi, rw_lo):
    t = x.shape[0]
    tm = TM_OUT
    row = lambda w: pl.BlockSpec((tm, w), lambda i: (i, 0))
    full = lambda a: pl.BlockSpec(a.shape, lambda i: (0,) * a.ndim)
    in_specs = [row(D_MODEL)] + [row(DG)] * 4 + [full(w_out4), full(norm_w), full(rw_hi), full(rw_lo)]
    return pl.pallas_call(
        _outproj_body, grid=(t // tm,), in_specs=in_specs,
        out_specs=[row(D_MODEL), row(D_MODEL), pl.BlockSpec((N_EXPERTS, tm), lambda i: (0, i))],
        out_shape=[jax.ShapeDtypeStruct((t, D_MODEL), F32), jax.ShapeDtypeStruct((t, D_MODEL), BF16),
                   jax.ShapeDtypeStruct((N_EXPERTS, t), F32)],
        compiler_params=_params("parallel"), name="out_projection",
    )(x, ya, yb, yc, yd, w_out4, norm_w, rw_hi, rw_lo)


def _prefix_excl_lanes(mask_f32, u_ref):
    e = mask_f32.shape[0]
    off = jnp.zeros((e, 1), F32)
    parts, bounds = [], [off]
    for k in range(SEQ // V7X_LANES):
        tile = mask_f32[:, k * V7X_LANES:(k + 1) * V7X_LANES]
        parts.append(_dot(tile.astype(BF16), u_ref[...]) + off)
        off = off + jnp.sum(tile, axis=1, keepdims=True)
        bounds.append(off)
    return jnp.concatenate(parts, axis=1), bounds


ROUTER_SEQS = 2


def _router_body(lg_ref, u_ref, rank_ref, gate_ref, seg_ref):
    affs = []
    for q in range(ROUTER_SEQS):
        logits = lg_ref[:, q * SEQ:(q + 1) * SEQ]
        ex = jnp.exp(logits - jnp.max(logits, axis=0, keepdims=True))
        affs.append(ex / jnp.sum(ex, axis=0, keepdims=True))
    aff = jnp.concatenate(affs, axis=0)
    nrow = ROUTER_SEQS * N_EXPERTS
    bits = pltpu.bitcast(aff, I32)

    def search(i, thr):
        cand = thr | jnp.left_shift(jnp.int32(1), 30 - i)
        cnt = jnp.sum((bits >= cand).astype(I32), axis=1, keepdims=True)
        return jnp.where(cnt >= CAP, cand, thr)

    thr = lax.fori_loop(0, 31, search, jnp.zeros((nrow, 1), I32))
    gt = (bits > thr).astype(F32)
    eq = (bits == thr).astype(F32)
    need = CAP - jnp.sum(gt, axis=1, keepdims=True)
    tie_rank, _ = _prefix_excl_lanes(eq, u_ref)
    sel = gt + eq * (tie_rank < need).astype(F32)
    rank, bounds = _prefix_excl_lanes(sel, u_ref)
    rank = jnp.where(sel > 0.0, rank, -1.0)
    lane = lax.broadcasted_iota(I32, (nrow, V7X_LANES), 1)
    seg = jnp.zeros((nrow, V7X_LANES), F32)
    tiles = MOE_SEG // V7X_LANES
    for sgm in range(N_MOE_SEG):
        first = jnp.floor(bounds[sgm * tiles] * (1.0 / MOE_ALIGN)) * MOE_ALIGN
        need = jnp.floor((bounds[(sgm + 1) * tiles] - first + (MOE_TILE - 1)) * (1.0 / MOE_TILE))
        need = jnp.max(need.reshape(ROUTER_SEQS, N_EXPERTS, 1), axis=1, keepdims=True)
        need = jnp.broadcast_to(need, (ROUTER_SEQS, N_EXPERTS, 1)).reshape(nrow, 1)
        seg = jnp.where(lane == sgm, first, seg)
        seg = jnp.where(lane == N_MOE_SEG + sgm, need, seg)
    seg = seg.astype(I32)
    for q in range(ROUTER_SEQS):
        rows = slice(q * N_EXPERTS, (q + 1) * N_EXPERTS)
        rank_ref[q] = rank[rows]
        gate_ref[q] = aff[rows]
        seg_ref[q] = seg[rows]


def router(logits_et, u128):
    b = logits_et.shape[1] // SEQ
    out = jax.ShapeDtypeStruct((b, N_EXPERTS, SEQ), F32)
    return pl.pallas_call(
        _router_body, grid=(b // ROUTER_SEQS,),
        in_specs=[pl.BlockSpec((N_EXPERTS, ROUTER_SEQS * SEQ), lambda i: (0, i)),
                  pl.BlockSpec(u128.shape, lambda i: (0, 0))],
        out_specs=[pl.BlockSpec((ROUTER_SEQS, N_EXPERTS, SEQ), lambda i: (i, 0, 0))] * 2
                  + [pl.BlockSpec((ROUTER_SEQS, N_EXPERTS, V7X_LANES), lambda i: (i, 0, 0))],
        out_shape=[out, out, jax.ShapeDtypeStruct((b, N_EXPERTS, V7X_LANES), I32)],
        compiler_params=_params("parallel"), name="router",
    )(logits_et, u128)


MOE_SEG = 256
N_MOE_SEG = SEQ // MOE_SEG
MOE_TILE = 64
MOE_ALIGN = 16
MOE_GROUP = 4
MOE_SEG_STRIDE = 16
MOE_FFN_SEQS = 4


def _moe_seg_plan(cs_ref, b, s):
    base = b * N_EXPERTS * MOE_SEG_STRIDE
    starts = [cs_ref[base + ex * MOE_SEG_STRIDE + s] for ex in range(N_EXPERTS)]
    return starts, cs_ref[base + N_MOE_SEG + s]


def _moe_tile_bases(starts, r):
    own = [st + r * MOE_TILE for st in starts]
    return [pl.multiple_of(jnp.minimum(o, CAP - MOE_TILE), MOE_ALIGN) for o in own], own


def _moe_onehot_group(rank_ref, gate_ref, s, bases, own, grp):
    lanes = pl.ds(pl.multiple_of(s * MOE_SEG, MOE_SEG), MOE_SEG)
    j = lax.broadcasted_iota(I32, (MOE_TILE, MOE_SEG), 0)
    rows = []
    for ex in grp:
        slot = bases[ex] + j
        hit = (rank_ref[0, ex:ex + 1, lanes] == slot.astype(F32)) & (slot >= own[ex])
        val = 1.0 if gate_ref is None else gate_ref[0, ex:ex + 1, lanes]
        rows.append(jnp.where(hit, val, 0.0).astype(BF16))
    return jnp.concatenate(rows, axis=0)


_MOE_GROUPS = [list(range(g * MOE_GROUP, (g + 1) * MOE_GROUP)) for g in range(N_EXPERTS // MOE_GROUP)]


def _moe_gather_body(cs_ref, xn_ref, rank_ref, xe_ref):
    b = pl.program_id(0)

    def zero(ex, _):
        xe_ref[0, ex] = jnp.zeros((CAP, D_MODEL), BF16)
        return 0

    lax.fori_loop(0, N_EXPERTS, zero, 0)

    def seg_gather(s, _):
        starts, rounds = _moe_seg_plan(cs_ref, b, s)
        xn_seg = xn_ref[0, pl.ds(pl.multiple_of(s * MOE_SEG, MOE_SEG), MOE_SEG), :]

        def one_round(r, _):
            bases, own = _moe_tile_bases(starts, r)
            for grp in _MOE_GROUPS:
                got = _dot(_moe_onehot_group(rank_ref, None, s, bases, own, grp), xn_seg)
                for k, ex in enumerate(grp):
                    rows = pl.ds(bases[ex], MOE_TILE)
                    old = xe_ref[0, ex, rows, :].astype(F32)
                    xe_ref[0, ex, rows, :] = (old + got[k * MOE_TILE:(k + 1) * MOE_TILE]).astype(BF16)
            return 0

        lax.fori_loop(0, rounds, one_round, 0)
        return 0

    lax.fori_loop(0, N_MOE_SEG, seg_gather, 0)


def moe_gather(seg_counts_flat, xn3, rank3):
    b = xn3.shape[0]
    grid_spec = pltpu.PrefetchScalarGridSpec(
        num_scalar_prefetch=1, grid=(b,),
        in_specs=[pl.BlockSpec((1, SEQ, D_MODEL), lambda i, cs: (i, 0, 0)),
                  pl.BlockSpec((1, N_EXPERTS, SEQ), lambda i, cs: (i, 0, 0))],
        out_specs=pl.BlockSpec((1, N_EXPERTS, CAP, D_MODEL), lambda i, cs: (i, 0, 0, 0)))
    return pl.pallas_call(
        _moe_gather_body, grid_spec=grid_spec,
        out_shape=jax.ShapeDtypeStruct((b, N_EXPERTS, CAP, D_MODEL), BF16),
        compiler_params=_params("parallel"), name="moe_gather",
    )(seg_counts_flat, xn3, rank3)


def _moe_experts_body(xe_ref, wg_ref, wu_ref, wd_ref, ye_ref, wg_s, wu_s, wd_s):
    @pl.when(pl.program_id(1) == 0)
    def _():
        wg_s[...] = wg_ref[0, 0].astype(BF16)
        wu_s[...] = wu_ref[0, 0].astype(BF16)
        wd_s[...] = wd_ref[0, 0].astype(BF16)

    xe = xe_ref[...].reshape(MOE_FFN_SEQS * CAP, D_MODEL)
    hid = (_silu(_dot(xe, wg_s[...])) * _dot(xe, wu_s[...])).astype(BF16)
    ye_ref[...] = _dot(hid, wd_s[...]).astype(BF16).reshape(MOE_FFN_SEQS, 1, CAP, D_MODEL)


def moe_experts(xe4, w_gate, w_up, w_down, layer):
    b = xe4.shape[0]
    blk = pl.BlockSpec((MOE_FFN_SEQS, 1, CAP, D_MODEL), lambda e, g: (g, e, 0, 0))
    w_spec = lambda a: pl.BlockSpec((1, 1) + a.shape[2:], lambda e, g: (layer, e, 0, 0))
    return pl.pallas_call(
        _moe_experts_body, grid=(N_EXPERTS, b // MOE_FFN_SEQS),
        in_specs=[blk, w_spec(w_gate), w_spec(w_up), w_spec(w_down)],
        out_specs=blk, out_shape=jax.ShapeDtypeStruct(xe4.shape, BF16),
        scratch_shapes=[pltpu.VMEM((D_MODEL, D_FF), BF16), pltpu.VMEM((D_MODEL, D_FF), BF16),
                        pltpu.VMEM((D_FF, D_MODEL), BF16)],
        compiler_params=_params("parallel", "arbitrary"), name="moe_experts",
    )(xe4, w_gate, w_up, w_down)


MOE_SCATTER_SEGS = 4


def _moe_scatter_body(final, cs_ref, ye_ref, rank_ref, gate_ref, xo_ref, nw_ref, o_ref):
    b = pl.program_id(0)
    half = pl.program_id(1)

    def seg_scatter(k, _):
        s = half * MOE_SCATTER_SEGS + k
        starts, rounds = _moe_seg_plan(cs_ref, b, s)
        tok = pl.ds(pl.multiple_of(k * MOE_SEG, MOE_SEG), MOE_SEG)
        o_ref[0, tok, :] = xo_ref[0, tok, :]

        def one_round(r, _):
            bases, own = _moe_tile_bases(starts, r)
            for grp in _MOE_GROUPS:
                ye = jnp.concatenate([ye_ref[0, ex, pl.ds(bases[ex], MOE_TILE), :] for ex in grp], axis=0)
                o_ref[0, tok, :] += lax.dot_general(_moe_onehot_group(rank_ref, gate_ref, s, bases, own, grp), ye,
                                                    _TN, preferred_element_type=F32)
            return 0

        lax.fori_loop(0, rounds, one_round, 0)
        if final:
            o_ref[0, tok, :] = _rms(o_ref[0, tok, :]) * nw_ref[...]
        return 0

    lax.fori_loop(0, MOE_SCATTER_SEGS, seg_scatter, 0)


def moe_scatter(seg_counts_flat, ye4, rank3, gate3, xo3, final_norm_w, final):
    b = ye4.shape[0]
    rows = MOE_SCATTER_SEGS * MOE_SEG
    sel_spec = pl.BlockSpec((1, N_EXPERTS, SEQ), lambda i, j, cs: (i, 0, 0))
    tok_spec = pl.BlockSpec((1, rows, D_MODEL), lambda i, j, cs: (i, j, 0))
    grid_spec = pltpu.PrefetchScalarGridSpec(
        num_scalar_prefetch=1, grid=(b, SEQ // rows),
        in_specs=[pl.BlockSpec((1, N_EXPERTS, CAP, D_MODEL), lambda i, j, cs: (i, 0, 0, 0)), sel_spec, sel_spec,
                  tok_spec, pl.BlockSpec(final_norm_w.shape, lambda i, j, cs: (0, 0))],
        out_specs=tok_spec)
    return pl.pallas_call(
        functools.partial(_moe_scatter_body, final), grid_spec=grid_spec,
        out_shape=jax.ShapeDtypeStruct((b, SEQ, D_MODEL), F32),
        compiler_params=_params("parallel", "arbitrary"), name="moe_scatter",
    )(seg_counts_flat, ye4, rank3, gate3, xo3, final_norm_w)


HY_KB = 256
HY_ROWS = 256


def _hy_filter_body(z_ref, w1_ref, b1_ref, w2_ref, b2_ref, fr_ref, w3_ref, dec_ref, c_ref, s_ref,
                    kr_ref, ki_ref, kny_ref, a_s, d_s):
    @pl.when(pl.program_id(0) == 0)
    def _():
        def rows(c, kny):
            r0 = pl.multiple_of(c * HY_ROWS, HY_ROWS)
            fr = fr_ref[...]
            h = jnp.sin(fr * (_dot_hi(z_ref[pl.ds(r0, HY_ROWS), :], w1_ref[...]) + b1_ref[...]))
            h = jnp.sin(fr * (_dot_hi(h, w2_ref[...]) + b2_ref[...]))
            h = _dot_hi(h, w3_ref[...])
            dec = dec_ref[pl.ds(r0, HY_ROWS), :]
            pos = r0 + lax.broadcasted_iota(I32, (HY_ROWS, DG), 0)
            hf = h[:, :DG] * dec
            hb = jnp.where(pos == 0, 0.0, h[:, DG:] * dec)
            a = hf + hb
            a_s[pl.ds(r0, HY_ROWS), :] = a
            d_s[pl.ds(r0, HY_ROWS), :] = hf - hb
            sgn = (1 - 2 * (pos & 1)).astype(F32)
            return kny + jnp.sum(a * sgn, axis=0, keepdims=True)

        kny = lax.fori_loop(0, SEQ // HY_ROWS, rows, jnp.zeros((1, DG), F32))
        kny_ref[...] = jnp.broadcast_to(kny, kny_ref.shape)

    kr_ref[...] = _dot_hi(c_ref[...], a_s[...])
    ki_ref[...] = _dot_hi(s_ref[...], d_s[...])


def hyena_filter_spectrum(zpos, w1, b1, w2, b2, freq, w3, decay, cos_f32, sin_f32):
    full = lambda a: pl.BlockSpec(a.shape, lambda k: (0,) * a.ndim)
    kblk = pl.BlockSpec((HY_KB, SEQ), lambda k: (k, 0))
    oblk = pl.BlockSpec((HY_KB, DG), lambda k: (k, 0))
    return pl.pallas_call(
        _hy_filter_body, grid=(SEQ // HY_KB,),
        in_specs=[full(zpos), full(w1), full(b1), full(w2), full(b2), full(freq), full(w3), full(decay), kblk, kblk],
        out_specs=[oblk, oblk, pl.BlockSpec((V7X_SUBLANES, DG), lambda k: (0, 0))],
        out_shape=[jax.ShapeDtypeStruct((SEQ, DG), F32), jax.ShapeDtypeStruct((SEQ, DG), F32),
                   jax.ShapeDtypeStruct((V7X_SUBLANES, DG), F32)],
        scratch_shapes=[pltpu.VMEM((SEQ, DG), F32), pltpu.VMEM((SEQ, DG), F32)],
        compiler_params=_params("arbitrary"), name="hyena_filter",
    )(zpos, w1, b1, w2, b2, freq, w3, decay, cos_f32, sin_f32)


CONV_ROWS = 128
CONV_HALO = 8


def _dwconv_rows(pad_ref, w_ref, r0, lanes, k):
    n = CONV_ROWS + 2 * CONV_HALO
    win = pad_ref[pl.ds(r0, n), lanes]
    acc = None
    for j in range(k):
        sh = (k // 2 - j) % n
        rolled = win if sh == 0 else pltpu.roll(win, sh, 0)
        term = rolled[CONV_HALO:CONV_HALO + CONV_ROWS] * w_ref[j:j + 1, lanes]
        acc = term if acc is None else acc + term
    return acc


def _fill_padded(pad_ref, src_ref, width):
    zeros = jnp.zeros((CONV_HALO, width), F32)
    pad_ref[pl.ds(0, CONV_HALO), :] = zeros
    pad_ref[pl.ds(SEQ + CONV_HALO, CONV_HALO), :] = zeros

    def fill(c, _):
        r0 = pl.multiple_of(c * CONV_ROWS, CONV_ROWS)
        pad_ref[pl.ds(r0 + CONV_HALO, CONV_ROWS), :] = src_ref[0, pl.ds(r0, CONV_ROWS), :].astype(F32)
        return 0

    lax.fori_loop(0, SEQ // CONV_ROWS, fill, 0)


def _hy_prep_body(p_ref, w_ref, z_ref, x0_ref, pad):
    _fill_padded(pad, p_ref, 3 * DG)

    def rows(c, _):
        r0 = pl.multiple_of(c * CONV_ROWS, CONV_ROWS)
        x0 = _dwconv_rows(pad, w_ref, r0, slice(0, DG), 3)
        x1 = _dwconv_rows(pad, w_ref, r0, slice(DG, 2 * DG), 3)
        v = _dwconv_rows(pad, w_ref, r0, slice(2 * DG, 3 * DG), 3)
        x0_ref[0, pl.ds(r0, CONV_ROWS), :] = x0.astype(BF16)
        z_ref[0, pl.ds(r0, CONV_ROWS), :] = (v * x1).astype(BF16)
        return 0

    lax.fori_loop(0, SEQ // CONV_ROWS, rows, 0)


def hyena_prep(p3, conv_w):
    b = p3.shape[0]
    blk = pl.BlockSpec((1, SEQ, DG), lambda i: (i, 0, 0))
    out = jax.ShapeDtypeStruct((b, SEQ, DG), BF16)
    return pl.pallas_call(
        _hy_prep_body, grid=(b,),
        in_specs=[pl.BlockSpec((1, SEQ, 3 * DG), lambda i: (i, 0, 0)), pl.BlockSpec(conv_w.shape, lambda i: (0, 0))],
        out_specs=[blk, blk], out_shape=[out, out],
        scratch_shapes=[pltpu.VMEM((SEQ + 2 * CONV_HALO, 3 * DG), F32)],
        compiler_params=_params("parallel"), name="hyena_prep",
    )(p3, conv_w)


HY_FB = 512


def _hy_conv_body(z_ref, x0_ref, t1_ref, t2_ref, kr_ref, ki_ref, kny_ref, fb_ref, o_ref, y_s):
    z = z_ref[0]

    def spectrum(kb, _):
        rows = pl.ds(pl.multiple_of(kb * HY_FB, HY_FB), HY_FB)
        rows_s = pl.ds(pl.multiple_of(SEQ + kb * HY_FB, HY_FB), HY_FB)
        zr = _dot(t1_ref[rows, :], z)
        zi = _dot(t1_ref[rows_s, :], z)
        krow = kb * HY_FB + lax.broadcasted_iota(I32, (HY_FB, 1), 0)
        wk = jnp.where(krow == 0, 1.0 / NFFT, 2.0 / NFFT)
        kr = kr_ref[rows, :]
        ki = ki_ref[rows, :]
        y_s[rows, :] = ((zr * kr - zi * ki) * wk).astype(BF16)
        y_s[rows_s, :] = ((zr * ki + zi * kr) * wk).astype(BF16)
        return 0

    lax.fori_loop(0, SEQ // HY_FB, spectrum, 0)

    zny = jnp.sum(z.astype(F32) * (1 - 2 * (lax.broadcasted_iota(I32, (SEQ, DG), 0) & 1)).astype(F32),
                  axis=0, keepdims=True)
    nyq = zny * kny_ref[0:1, :] * (1.0 / NFFT)

    def synth(tb, _):
        rows = pl.ds(pl.multiple_of(tb * HY_FB, HY_FB), HY_FB)
        conv = _dot(t2_ref[rows, :], y_s[...])
        sgn = (1 - 2 * (lax.broadcasted_iota(I32, (HY_FB, DG), 0) & 1)).astype(F32)
        zf = z_ref[0, rows, :].astype(F32)
        o_ref[0, rows, :] = (x0_ref[0, rows, :].astype(F32) * (conv + nyq * sgn + zf * fb_ref[...])).astype(BF16)
        return 0

    lax.fori_loop(0, SEQ // HY_FB, synth, 0)


def hyena_conv(z3, x03, dft_rows, dft_cols, kr, ki, kny, fbias):
    b = z3.shape[0]
    seq_blk = pl.BlockSpec((1, SEQ, DG), lambda i: (i, 0, 0))
    once = lambda a: pl.BlockSpec(a.shape, lambda i: (0,) * a.ndim, pipeline_mode=pl.Buffered(1))
    return pl.pallas_call(
        _hy_conv_body, grid=(b,),
        in_specs=[seq_blk, seq_blk, once(dft_rows), once(dft_cols), once(kr), once(ki), once(kny), once(fbias)],
        out_specs=seq_blk, out_shape=jax.ShapeDtypeStruct((b, SEQ, DG), BF16),
        scratch_shapes=[pltpu.VMEM((2 * SEQ, DG), BF16)],
        compiler_params=_params("parallel"), name="hyena_conv",
    )(z3, x03, dft_rows, dft_cols, kr, ki, kny, fbias)


N_MCH = SEQ // M_CHUNK
MQ = M_CHUNK


def _head_lane_vec(rows8, base):
    lane_head = lax.broadcasted_iota(I32, (1, DG), 1) // HD
    out = jnp.zeros((1, DG), F32)
    for h in range(N_HEADS):
        out = jnp.where(lane_head == h, rows8[base + h:base + h + 1, :], out)
    return out


def _mamba_body(z_ref, xbc_ref, dtc_ref, cw_ref, cb_ref, dtb_ref, a_ref, dsk_ref, nw_ref, tri_ref, bd_ref,
                o_ref, pad, xs_s, b_s, c_s, y_s, u_s, dec_s, cw_s, yo_s, st_s):
    _fill_padded(pad, xbc_ref, 2 * DG)

    def conv_rows(c, _):
        r0 = pl.multiple_of(c * CONV_ROWS, CONV_ROWS)
        for g in range(4):
            lanes = slice(g * V7X_LANES, (g + 1) * V7X_LANES)
            u = _silu(_dwconv_rows(pad, cw_ref, r0, lanes, M_CONV) + cb_ref[:, lanes])
            if g < 2:
                xs_s[pl.ds(r0, CONV_ROWS), lanes] = u
            elif g == 2:
                b_s[pl.ds(r0, CONV_ROWS), :] = u.astype(BF16)
            else:
                c_s[pl.ds(r0, CONV_ROWS), :] = u.astype(BF16)
        return 0

    lax.fori_loop(0, SEQ // CONV_ROWS, conv_rows, 0)

    li = lax.broadcasted_iota(I32, (MQ, MQ), 0)
    si = lax.broadcasted_iota(I32, (MQ, MQ), 1)
    lower = si <= li
    upper = si >= li
    bdmask = bd_ref[...]

    def chunk(c, _):
        r0 = pl.multiple_of(c * MQ, MQ)
        dt = _softplus(dtc_ref[0, c] + dtb_ref[...])
        a = dt * a_ref[...]
        cum = _dot01_rhs(a, tri_ref[...])
        tot = cum[:, MQ - 1:MQ]
        suf = tot - cum + a
        row_dir = lax.broadcasted_iota(I32, (8, MQ), 0) // N_HEADS
        seg = jnp.where(row_dir == 0, cum, suf)
        wgt = jnp.exp(tot - seg) * dt
        cols = jnp.concatenate([seg, jnp.exp(seg)], axis=0).T
        x = xs_s[pl.ds(r0, MQ), :]
        xb = x.astype(BF16)
        bm = b_s[pl.ds(r0, MQ), :]
        cm = c_s[pl.ds(r0, MQ), :]
        cmf = cm.astype(F32)
        bt = bm.astype(F32).T
        ydiag = []
        for h in range(N_HEADS):
            g = h // 2
            cb = lax.dot_general(cm[:, g * M_STATE:(g + 1) * M_STATE], bm[:, g * M_STATE:(g + 1) * M_STATE],
                                 _NT, preferred_element_type=F32)
            lf = jnp.where(lower, jnp.exp(jnp.minimum(cols[:, h:h + 1] - seg[h:h + 1, :], 0.0)), 0.0)
            lb = jnp.where(upper, jnp.exp(jnp.minimum(cols[:, 4 + h:5 + h] - seg[4 + h:5 + h, :], 0.0)), 0.0)
            m = cb * (lf * dt[h:h + 1, :] + lb * dt[4 + h:5 + h, :])
            ydiag.append(_dot(m.astype(BF16), xb[:, h * HD:(h + 1) * HD]))
        y_s[pl.ds(r0, MQ), :] = jnp.concatenate(ydiag, axis=1)
        for d in range(2):
            bwt = jnp.concatenate([bt[(h // 2) * M_STATE:(h // 2 + 1) * M_STATE, :] * wgt[4 * d + h:4 * d + h + 1, :]
                                   for h in range(N_HEADS)], axis=0)
            u_s[d, c] = (_dot(bwt.astype(BF16), xb) * bdmask).astype(BF16)
            dec_s[d, c] = jnp.broadcast_to(_head_lane_vec(jnp.exp(tot), 4 * d), (V7X_SUBLANES, DG))
            cw_s[d, c] = jnp.concatenate(
                [cmf[:, (h // 2) * M_STATE:(h // 2 + 1) * M_STATE] * cols[:, 8 + 4 * d + h:9 + 4 * d + h]
                 for h in range(N_HEADS)], axis=1).astype(BF16)
        return 0

    lax.fori_loop(0, N_MCH, chunk, 0, unroll=2)

    st_s[...] = jnp.zeros(st_s.shape, F32)

    def scan(i, _):
        for d in range(2):
            c = i if d == 0 else N_MCH - 1 - i
            st = st_s[d]
            yo_s[d, pl.ds(pl.multiple_of(c * MQ, MQ), MQ), :] = _dot(cw_s[d, c], st.astype(BF16))
            st_s[d] = st * dec_s[d, c][0:1, :] + u_s[d, c].astype(F32)
        return 0

    lax.fori_loop(0, N_MCH, scan, 0)

    def finish(c, _):
        r0 = pl.multiple_of(c * CONV_ROWS, CONV_ROWS)
        rows = pl.ds(r0, CONV_ROWS)
        y = y_s[rows, :] + yo_s[0, rows, :] + yo_s[1, rows, :] + xs_s[rows, :] * dsk_ref[...]
        y = y * _silu(z_ref[0, rows, :].astype(F32))
        o_ref[0, pl.ds(r0, CONV_ROWS), :] = (_rms(y) * nw_ref[...]).astype(BF16)
        return 0

    lax.fori_loop(0, SEQ // CONV_ROWS, finish, 0)


def mamba2(z3, xbc3, dtc4, conv_w, conv_b, dt_bias_col, a_col, dskip_lanes, norm_w, tri_incl, bdmask):
    b = z3.shape[0]
    full = lambda a: pl.BlockSpec(a.shape, lambda i: (0,) * a.ndim)
    return pl.pallas_call(
        _mamba_body, grid=(b,),
        in_specs=[pl.BlockSpec((1, SEQ, DG), lambda i: (i, 0, 0)),
                  pl.BlockSpec((1, SEQ, 2 * DG), lambda i: (i, 0, 0)),
                  pl.BlockSpec((1, N_MCH, 8, MQ), lambda i: (i, 0, 0, 0)),
                  full(conv_w), full(conv_b), full(dt_bias_col), full(a_col), full(dskip_lanes), full(norm_w),
                  full(tri_incl), full(bdmask)],
        out_specs=pl.BlockSpec((1, SEQ, DG), lambda i: (i, 0, 0)),
        out_shape=jax.ShapeDtypeStruct((b, SEQ, DG), BF16),
        scratch_shapes=[pltpu.VMEM((SEQ + 2 * CONV_HALO, 2 * DG), F32),
                        pltpu.VMEM((SEQ, DG), F32),
                        pltpu.VMEM((SEQ, 2 * M_STATE), BF16),
                        pltpu.VMEM((SEQ, 2 * M_STATE), BF16),
                        pltpu.VMEM((SEQ, DG), F32),
                        pltpu.VMEM((2, N_MCH, DG, DG), BF16),
                        pltpu.VMEM((2, N_MCH, V7X_SUBLANES, DG), F32),
                        pltpu.VMEM((2, N_MCH, MQ, DG), BF16),
                        pltpu.VMEM((2, SEQ, DG), F32),
                        pltpu.VMEM((2, DG, DG), F32)],
        compiler_params=_params("parallel"), name="mamba2",
    )(z3, xbc3, dtc4, conv_w, conv_b, dt_bias_col, a_col, dskip_lanes, norm_w, tri_incl, bdmask)


A_TQ = 128
A_ROWS = 256
A_KW = A_TQ + 2 * A_BAND


def _attn_bias_body(ids_ref, rb_ref, o_ref):
    ids = ids_ref[0]
    for h in range(N_HEADS):
        acc = jnp.full(ids.shape, NEG_BIG, F32)
        for bkt in range(N_BUCKETS):
            acc = jnp.where(ids == bkt, rb_ref[bkt, h], acc)
        o_ref[h, 0] = acc


def attention_bias_table(bucket_ids, rel_bias):
    nvar, tq, w = bucket_ids.shape
    return pl.pallas_call(
        _attn_bias_body, grid=(nvar,),
        in_specs=[pl.BlockSpec((1, tq, w), lambda v: (v, 0, 0)),
                  pl.BlockSpec(memory_space=pltpu.SMEM)],
        out_specs=pl.BlockSpec((N_HEADS, 1, tq, w), lambda v: (0, v, 0, 0)),
        out_shape=jax.ShapeDtypeStruct((N_HEADS, nvar, tq, w), F32),
        compiler_params=_params("parallel"), name="attention_bias_table",
    )(bucket_ids, rel_bias)


A_SLABS = 3 * DG // V7X_LANES
A_QBLOCKS = SEQ // A_TQ


A_SUB4 = SEQ // 4
A_SUB16 = SEQ // 16


def _attn_body(at_ref, b1_ref, b4_ref, b16_ref, o_ref, qkv_s, x4_s, x16_s, y16_s, y4_s, part_o, part_l):
    def fill(c, _):
        r0 = pl.multiple_of(c * A_ROWS, A_ROWS)
        for s in range(A_SLABS):
            qkv_s[s, pl.ds(r0, A_ROWS), :] = at_ref[0, pl.ds(r0, A_ROWS), s * V7X_LANES:(s + 1) * V7X_LANES].astype(F32)
        return 0

    lax.fori_loop(0, SEQ // A_ROWS, fill, 0)

    def deinterleave(s, _):
        for r4 in range(4):
            for c in range(A_SUB4 // A_ROWS):
                x4_s[s, pl.ds(r4 * A_SUB4 + c * A_ROWS, A_ROWS), :] = \
                    qkv_s[s, pl.ds(r4 + 4 * c * A_ROWS, A_ROWS, stride=4), :]
        for r in range(16):
            x16_s[s, pl.ds(r * A_SUB16, A_SUB16), :] = \
                x4_s[s, pl.ds((r % 4) * A_SUB4 + r // 4, A_SUB16, stride=4), :].astype(BF16)
        return 0

    lax.fori_loop(0, A_SLABS, deinterleave, 0)
    first_head = lax.broadcasted_iota(I32, (A_TQ, V7X_LANES), 1) < HD

    def run_pattern(pat, dil, bias_ref):
        n = SEQ // dil if dil < 16 else SEQ
        nblk = n // A_TQ
        w = A_KW

        def block(it, _):
            r = it // nblk
            i = it - r * nblk
            q0 = i * A_TQ
            k0 = jnp.clip(q0 - A_BAND, 0, n - w)
            var = jnp.where(i == 0, 0, jnp.where(i == nblk - 1, 2, 1))
            for hp in range(2):
                lanes = [slice((2 * part + hp) * V7X_LANES, (2 * part + hp + 1) * V7X_LANES) for part in range(3)]
                if dil == 4:
                    qrows = pl.ds(r + dil * q0, A_TQ, stride=dil)
                    krows = pl.ds(r + dil * k0, w, stride=dil)
                    q2 = qkv_s[hp, qrows, :]
                    k2 = qkv_s[2 + hp, krows, :].astype(BF16)
                    v2 = qkv_s[4 + hp, krows, :].astype(BF16)
                else:
                    qrows = pl.ds(pl.multiple_of(q0, A_TQ), A_TQ)
                    krows = pl.ds(pl.multiple_of(k0, A_BAND), w)
                    if dil == 1:
                        q2, k2, v2 = at_ref[0, qrows, lanes[0]], at_ref[0, krows, lanes[1]], at_ref[0, krows, lanes[2]]
                    else:
                        q2, k2, v2 = x16_s[hp, qrows, :], x16_s[2 + hp, krows, :], x16_s[4 + hp, krows, :]
                q2 = (q2 * (HD ** -0.5)).astype(BF16)
                outs, lses = [], []
                for hh in range(2):
                    keep = first_head if hh == 0 else jnp.logical_not(first_head)
                    qm = jnp.where(keep, q2, jnp.zeros_like(q2))
                    s = lax.dot_general(qm, k2, _NT, preferred_element_type=F32) + bias_ref[2 * hp + hh, var]
                    m = jnp.max(s, axis=1, keepdims=True)
                    p = jnp.exp(s - m)
                    den = jnp.sum(p, axis=1, keepdims=True)
                    outs.append(_dot(p.astype(BF16), v2) / den)
                    lses.append(m + jnp.log(den))
                o_new = jnp.where(first_head, outs[0], outs[1])
                l_new = jnp.where(first_head, lses[0], lses[1])
                if dil == 16:
                    y16_s[0, hp, qrows, :] = o_new
                    y16_s[1, hp, qrows, :] = l_new
                else:
                    part_o[pat, hp, qrows, :] = o_new
                    part_l[pat, hp, qrows, :] = l_new
            return 0

        lax.fori_loop(0, A_QBLOCKS, block, 0, unroll=4)

    for pat, (dil, bias_ref) in enumerate(zip(A_DILS, (b1_ref, b4_ref, b16_ref))):
        run_pattern(pat, dil, bias_ref)

    for a, dst in enumerate((part_o, part_l)):
        for hp in range(2):
            for r in range(16):
                y4_s[a, hp, pl.ds((r % 4) * A_SUB4 + r // 4, A_SUB16, stride=4), :] = \
                    y16_s[a, hp, pl.ds(r * A_SUB16, A_SUB16), :]
            for r4 in range(4):
                for c in range(A_SUB4 // A_ROWS):
                    dst[2, hp, pl.ds(r4 + 4 * c * A_ROWS, A_ROWS, stride=4), :] = \
                        y4_s[a, hp, pl.ds(r4 * A_SUB4 + c * A_ROWS, A_ROWS), :]

    def finish(c, _):
        rows = pl.ds(pl.multiple_of(c * A_TQ, A_TQ), A_TQ)
        for hp in range(2):
            ls = [part_l[pat, hp, rows, :] for pat in range(len(A_DILS))]
            mx = jnp.maximum(jnp.maximum(ls[0], ls[1]), ls[2])
            ws = [jnp.exp(l - mx) for l in ls]
            num = ws[0] * part_o[0, hp, rows, :] + ws[1] * part_o[1, hp, rows, :] + ws[2] * part_o[2, hp, rows, :]
            o_ref[0, rows, hp * V7X_LANES:(hp + 1) * V7X_LANES] = (num / (ws[0] + ws[1] + ws[2])).astype(BF16)
        return 0

    lax.fori_loop(0, SEQ // A_TQ, finish, 0)


def dilated_attention(at3, bias1, bias4, bias16):
    b = at3.shape[0]
    full = lambda a: pl.BlockSpec(a.shape, lambda i: (0,) * a.ndim)
    return pl.pallas_call(
        _attn_body, grid=(b,),
        in_specs=[pl.BlockSpec((1, SEQ, 3 * DG), lambda i: (i, 0, 0)), full(bias1), full(bias4), full(bias16)],
        out_specs=pl.BlockSpec((1, SEQ, DG), lambda i: (i, 0, 0)),
        out_shape=jax.ShapeDtypeStruct((b, SEQ, DG), BF16),
        scratch_shapes=[pltpu.VMEM((A_SLABS, SEQ, V7X_LANES), F32),
                        pltpu.VMEM((A_SLABS, SEQ, V7X_LANES), F32),
                        pltpu.VMEM((A_SLABS, SEQ, V7X_LANES), BF16),
                        pltpu.VMEM((2, 2, SEQ, V7X_LANES), F32),
                        pltpu.VMEM((2, 2, SEQ, V7X_LANES), F32),
                        pltpu.VMEM((len(A_DILS), 2, SEQ, V7X_LANES), F32),
                        pltpu.VMEM((len(A_DILS), 2, SEQ, V7X_LANES), F32)],
        compiler_params=_params("parallel"), name="dilated_attention",
    )(at3, bias1, bias4, bias16)


H_BLK = 256
H_CPB = H_BLK // H_CHUNK
N_HBLK = SEQ // H_BLK
N_HCH = SEQ // H_CHUNK


def _chunk_bcast(x, row):
    c = x.shape[1]
    x3 = x.reshape(H_CPB, H_CHUNK, c)
    return jnp.broadcast_to(x3[:, row:row + 1, :], (H_CPB, H_CHUNK, c)).reshape(H_BLK, c)


def _hgrn_body(p_ref, lb_ref, nw_ref, tin_ref, o_ref, qm_s, ut_s, oi_s, dec_s, oe_s, st_s):
    li = lax.broadcasted_iota(I32, (H_BLK, H_BLK), 0)
    si = lax.broadcasted_iota(I32, (H_BLK, H_BLK), 1)
    same = (li // H_CHUNK) == (si // H_CHUNK)
    mask_f = same & (si <= li)
    mask_b = same & (si >= li)
    lane_head = lax.broadcasted_iota(I32, (1, DG), 1) // HD

    def block(bi, _):
        r0 = pl.multiple_of(bi * H_BLK, H_BLK)
        rows = pl.ds(r0, H_BLK)
        q = _silu(p_ref[0, rows, 0:DG].astype(F32))
        v = p_ref[0, rows, 3 * DG:4 * DG]
        scores = [None] * N_HEADS
        for d in range(2):
            fpre = p_ref[0, rows, (1 + d) * DG:(2 + d) * DG].astype(F32)
            lb = lb_ref[d:d + 1, :]
            sg = jax.nn.sigmoid(fpre)
            g = jnp.log(lb + (1.0 - lb) * sg)
            k = (1.0 - lb) * (1.0 - sg)
            gi = _dot01_2(tin_ref[...], g)
            glast = _chunk_bcast(gi, H_CHUNK - 1)
            if d == 0:
                gc = gi
                gref = _chunk_bcast(gi, H_CHUNK // 2 - 1)
                msk = mask_f
            else:
                gc = glast - gi + g
                gref = _chunk_bcast(gc, H_CHUNK // 2)
                msk = mask_b
            qe = (q * jnp.exp(gc - gref)).astype(BF16)
            ke = (k * jnp.exp(gref - gc)).astype(BF16)
            for h in range(N_HEADS):
                hs = slice(h * HD, (h + 1) * HD)
                sc = jnp.where(msk, lax.dot_general(qe[:, hs], ke[:, hs], _NT, preferred_element_type=F32), 0.0)
                scores[h] = sc if d == 0 else scores[h] + sc
            qd = q * jnp.exp(gc)
            kd = (k * jnp.exp(glast - gc)).astype(BF16)
            for j in range(H_CPB):
                c = bi * H_CPB + j
                cr = slice(j * H_CHUNK, (j + 1) * H_CHUNK)
                qm_s[d, c] = jnp.concatenate([jnp.where(lane_head == h, qd[cr, :], 0.0) for h in range(N_HEADS)],
                                             axis=0).astype(BF16)
                ut = lax.dot_general(v[cr, :], kd[cr, :], _TN, preferred_element_type=F32)
                packed = ut[0:HD, :]
                for h in range(1, N_HEADS):
                    packed = jnp.where(lane_head == h, ut[h * HD:(h + 1) * HD, :], packed)
                ut_s[d, c] = packed.astype(BF16)
                dec_s[d, c] = jnp.broadcast_to(jnp.exp(glast[j * H_CHUNK:j * H_CHUNK + 1, :]), (V7X_SUBLANES, DG))
        for h in range(N_HEADS):
            oi_s[h, rows, :] = _dot(scores[h].astype(BF16), v[:, h * HD:(h + 1) * HD])
        return 0

    lax.fori_loop(0, N_HBLK, block, 0)

    st_s[...] = jnp.zeros(st_s.shape, F32)

    def step(i, _):
        for d in range(2):
            c = i if d == 0 else N_HCH - 1 - i
            rows = pl.ds(pl.multiple_of(c * H_CHUNK, H_CHUNK), H_CHUNK)
            st = st_s[d]
            inter = lax.dot_general(qm_s[d, c], st.astype(BF16), _NT, preferred_element_type=F32)
            for h in range(N_HEADS):
                oe_s[d, h, rows, :] = inter[h * H_CHUNK:(h + 1) * H_CHUNK, :]
            st_s[d] = st * dec_s[d, c][0:1, :] + ut_s[d, c].astype(F32)
        return 0

    lax.fori_loop(0, N_HCH, step, 0, unroll=2)

    def finish(c, _):
        r0 = pl.multiple_of(c * CONV_ROWS, CONV_ROWS)
        rows = pl.ds(r0, CONV_ROWS)
        gate = _silu(p_ref[0, rows, 4 * DG:5 * DG].astype(F32))
        outs = [_rms(oi_s[h, rows, :] + oe_s[0, h, rows, :] + oe_s[1, h, rows, :]) for h in range(N_HEADS)]
        o_ref[0, rows, :] = (jnp.concatenate(outs, axis=1) * nw_ref[...] * gate).astype(BF16)
        return 0

    lax.fori_loop(0, SEQ // CONV_ROWS, finish, 0)


def hgrn2(p3, lb2, norm_w_lanes, tri_in_chunk):
    b = p3.shape[0]
    full = lambda a: pl.BlockSpec(a.shape, lambda i: (0,) * a.ndim)
    return pl.pallas_call(
        _hgrn_body, grid=(b,),
        in_specs=[pl.BlockSpec((1, SEQ, 5 * DG), lambda i: (i, 0, 0)), full(lb2), full(norm_w_lanes),
                  full(tri_in_chunk)],
        out_specs=pl.BlockSpec((1, SEQ, DG), lambda i: (i, 0, 0)),
        out_shape=jax.ShapeDtypeStruct((b, SEQ, DG), BF16),
        scratch_shapes=[pltpu.VMEM((2, N_HCH, N_HEADS * H_CHUNK, DG), BF16),
                        pltpu.VMEM((2, N_HCH, HD, DG), BF16),
                        pltpu.VMEM((N_HEADS, SEQ, HD), F32),
                        pltpu.VMEM((2, N_HCH, V7X_SUBLANES, DG), F32),
                        pltpu.VMEM((2, N_HEADS, SEQ, HD), F32),
                        pltpu.VMEM((2, HD, DG), F32)],
        compiler_params=_params("parallel"), name="hgrn2",
    )(p3, lb2, norm_w_lanes, tri_in_chunk)


@functools.lru_cache(maxsize=None)
def _tables():
    t = {}
    k = np.arange(SEQ, dtype=np.int64)
    ang = 2.0 * np.pi * ((k[:, None] * k[None, :]) % NFFT).astype(np.float64) / NFFT
    t["cos"] = np.cos(ang).astype(np.float32)
    t["sin"] = np.sin(ang).astype(np.float32)
    t["dft_rows"] = np.concatenate([t["cos"], t["sin"]], axis=0).astype(ml_dtypes.bfloat16)
    t["dft_cols"] = np.concatenate([t["cos"], t["sin"]], axis=1).astype(ml_dtypes.bfloat16)
    tt = np.linspace(0.0, 1.0, SEQ, dtype=np.float32)[:, None]
    bands = (HY_POS_DIM - 1) // 2
    ang_pos = (2.0 * math.pi * np.arange(SEQ, dtype=np.float32) / SEQ).astype(np.float32)
    f = np.linspace(1e-4, bands - 1, bands, dtype=np.float32)
    a2 = (ang_pos[:, None] * f[None, :]).astype(np.float32)
    z = np.concatenate([tt, np.cos(a2), -np.sin(a2)], axis=-1).astype(np.float32)
    zp = np.zeros((SEQ, V7X_LANES), np.float32)
    zp[:, :HY_POS_DIM] = z
    t["zpos"] = zp
    max_decay = math.log(1e-2) / 0.3
    min_decay = math.log(1e-2) / 1.5
    deltas = np.abs(np.linspace(min_decay, max_decay, DG, dtype=np.float32))
    t["decay"] = np.exp(-tt * deltas[None, :]).astype(np.float32)
    i128 = np.arange(V7X_LANES)
    t["u128"] = (i128[:, None] < i128[None, :]).astype(np.float32)
    im = np.arange(M_CHUNK)
    t["tri_incl"] = (im[:, None] <= im[None, :]).astype(np.float32)
    ib = np.arange(H_BLK)
    t["tri_in_chunk"] = ((ib[:, None] // H_CHUNK == ib[None, :] // H_CHUNK)
                         & (ib[None, :] <= ib[:, None])).astype(np.float32)
    idg = np.arange(DG)
    t["bdmask"] = (idg[:, None] // HD == idg[None, :] // HD).astype(np.float32)
    def bucket(rel):
        nb = N_BUCKETS // 2
        max_exact = nb // 2
        ret = (rel > 0).astype(np.int64) * nb
        n = np.abs(rel)
        nf = np.maximum(n, 1).astype(np.float64)
        large = max_exact + (np.log(nf / max_exact) / math.log(MAX_DISTANCE / max_exact)
                             * (nb - max_exact)).astype(np.int64)
        large = np.minimum(large, nb - 1)
        return ret + np.where(n < max_exact, n, large)

    for dil in A_DILS:
        n = SEQ // dil
        qi = np.arange(A_TQ)[:, None]
        kj = np.arange(A_KW)[None, :]
        ids = []
        for s0 in (0, -A_BAND, -(A_KW - A_TQ)):
            kk = kj + s0
            rel = kk - qi
            ok = np.abs(rel) <= A_BAND
            if n == A_TQ:
                ok &= (kk >= 0) & (kk < A_TQ)
            ids.append(np.where(ok, bucket(rel * dil), -1))
        t[f"bucket{dil}"] = np.stack(ids).astype(np.int32)
    return t


def kernel(x, w_in, w_out, norm_mix_w, norm_ffn_w, hy_conv_w, hy_pos_w1, hy_pos_b1, hy_pos_w2, hy_pos_b2,
           hy_sin_freq, hy_pos_w3, hy_filt_bias, m_conv_w, m_conv_b, m_dt_bias, m_A_log, m_D, m_norm_w, rel_bias,
           hg_lb, hg_norm_w, router_w, moe_w_gate, moe_w_up, moe_w_down, final_norm_w):
    b = x.shape[0]
    t = b * SEQ
    tb = _tables()
    cos_f32 = jnp.asarray(tb["cos"])
    sin_f32 = jnp.asarray(tb["sin"])
    dft_rows = jnp.asarray(tb["dft_rows"])
    dft_cols = jnp.asarray(tb["dft_cols"])
    u128 = jnp.asarray(tb["u128"]).astype(BF16)
    tri_incl = jnp.asarray(tb["tri_incl"]).astype(BF16)
    tri_in_chunk = jnp.asarray(tb["tri_in_chunk"]).astype(BF16)
    bdmask = jnp.asarray(tb["bdmask"])
    attn_bias = [attention_bias_table(jnp.asarray(tb[f"bucket{d}"]), rel_bias.astype(F32)) for d in A_DILS]

    sm = jax.nn.softmax(hg_lb.astype(F32), axis=0)
    lower_bounds = jnp.cumsum(sm, axis=0) - sm[:1]

    xa = x.reshape(t, D_MODEL)
    for l in range(DEPTH):
        wl = w_in[l]
        w_main = jnp.concatenate([wl[:, 0:768], wl[:, 768:1024], wl[:, 1024:1536], wl[:, 1544:2312],
                                  wl[:, 2312:3592]], axis=1).astype(BF16)
        w_dt_rows = wl[:, 1536:1544].T.astype(BF16)
        hy, mz, mx, at, hg, dtc = in_projection(xa, norm_mix_w[l][None, :], w_main, w_dt_rows)

        w1p = jnp.zeros((V7X_LANES, HY_HID), F32).at[:HY_POS_DIM].set(hy_pos_w1[l])
        kr, ki, kny = hyena_filter_spectrum(
            jnp.asarray(tb["zpos"]), w1p, hy_pos_b1[l][None, :], hy_pos_w2[l], hy_pos_b2[l][None, :],
            hy_sin_freq[l][None, :], hy_pos_w3[l], jnp.asarray(tb["decay"]), cos_f32, sin_f32)
        z3, x03 = hyena_prep(hy.reshape(b, SEQ, 3 * DG), hy_conv_w[l])
        ya = hyena_conv(z3, x03, dft_rows, dft_cols, kr, ki, kny, hy_filt_bias[l][None, :]).reshape(t, DG)

        a_col = (-jnp.exp(m_A_log[l].astype(F32))).reshape(8, 1)
        yb = mamba2(mz.reshape(b, SEQ, DG), mx.reshape(b, SEQ, 2 * DG), dtc.reshape(b, N_MCH, 8, MQ),
                    m_conv_w[l], m_conv_b[l][None, :], m_dt_bias[l].reshape(8, 1), a_col,
                    jnp.repeat(m_D[l].astype(F32), HD)[None, :], m_norm_w[l][None, :], tri_incl, bdmask).reshape(t, DG)

        yc = dilated_attention(at.reshape(b, SEQ, 3 * DG), *attn_bias).reshape(t, DG)

        lbl = lower_bounds[l]
        yd = hgrn2(hg.reshape(b, SEQ, 5 * DG), lbl, jnp.tile(hg_norm_w[l], N_HEADS)[None, :],
                   tri_in_chunk).reshape(t, DG)

        rw_rows = router_w[l].T.astype(F32)
        rw_hi = rw_rows.astype(BF16)
        rw_lo = (rw_rows - rw_hi.astype(F32)).astype(BF16)
        xo, xn, logits = out_projection(xa, ya, yb, yc, yd, w_out[l].reshape(4, DG, D_MODEL).astype(BF16),
                                        norm_ffn_w[l][None, :], rw_hi, rw_lo)
        xn3 = xn.reshape(b, SEQ, D_MODEL)
        rank, gate, seg = router(logits, u128)
        seg_flat = seg[:, :, :MOE_SEG_STRIDE].reshape(-1)
        xe = moe_gather(seg_flat, xn3, rank)
        ye = moe_experts(xe, moe_w_gate, moe_w_up, moe_w_down, l)
        xa = moe_scatter(seg_flat, ye, rank, gate, xo.reshape(b, SEQ, D_MODEL), final_norm_w[None, :],
                         final=(l == DEPTH - 1)).reshape(t, D_MODEL)
    return xa.reshape(b, SEQ, D_MODEL)
```

```python
import functools
import math

import ml_dtypes
import numpy as np
import jax
import jax.numpy as jnp
from jax import lax
from jax.experimental import pallas as pl
from jax.experimental.pallas import tpu as pltpu

F32 = jnp.float32
BF16 = jnp.bfloat16
I32 = jnp.int32

D_MODEL = 1024
SEQ = 2048
DEPTH = 2
DG = 256
N_HEADS = 4
HD = 64
HY_POS_DIM = 33
HY_HID = 64
M_CONV = 5
M_STATE = 64
M_CHUNK = 128
H_CHUNK = 32
A_BAND = 64
A_DILS = (1, 4, 16)
N_BUCKETS = 32
MAX_DISTANCE = 1024
N_EXPERTS = 16
CAP = 2 * SEQ // N_EXPERTS
D_FF = 1024
EPS = 1e-6
NFFT = 2 * SEQ

V7X_LANES = 128
V7X_SUBLANES = 8
V7X_VMEM_LIMIT_BYTES = 56 * 1024 * 1024

NEG_BIG = -1e30

_NT = (((1,), (1,)), ((), ()))
_TN = (((0,), (0,)), ((), ()))


def _params(*sem):
    return pltpu.CompilerParams(dimension_semantics=sem, vmem_limit_bytes=V7X_VMEM_LIMIT_BYTES)


def _dot(a, b):
    return jnp.dot(a, b, preferred_element_type=F32)


def _dot_hi(a, b):
    return jnp.dot(a, b, preferred_element_type=F32, precision=lax.Precision.HIGHEST)


def _dot01_2(t_bf16, x):
    x1 = x.astype(BF16)
    x2 = (x - x1.astype(F32)).astype(BF16)
    return _dot(t_bf16, x1) + _dot(t_bf16, x2)


def _dot01_rhs(x, t_bf16):
    x1 = x.astype(BF16)
    r1 = x - x1.astype(F32)
    x2 = r1.astype(BF16)
    x3 = (r1 - x2.astype(F32)).astype(BF16)
    return _dot(x1, t_bf16) + _dot(x2, t_bf16) + _dot(x3, t_bf16)


def _silu(x):
    return x * jax.nn.sigmoid(x)


def _softplus(x):
    return jnp.maximum(x, 0.0) + jnp.log(1.0 + jnp.exp(-jnp.abs(x)))


def _rms(x):
    return x * lax.rsqrt(jnp.mean(x * x, axis=-1, keepdims=True) + EPS)


TM_PROJ = 1024
_HY0, _MZ0, _MX0, _AT0, _HG0, _PEND = 0, 768, 1024, 1536, 2304, 3584


def _inproj_body(x_ref, nw_ref, w_ref, wdt_ref, hy_ref, mz_ref, mx_ref, at_ref, hg_ref, dtc_ref):
    x = x_ref[...]
    hn = (_rms(x) * nw_ref[...]).astype(BF16)
    hy_ref[...] = _dot(hn, w_ref[:, _HY0:_MZ0]).astype(BF16)
    mz_ref[...] = _dot(hn, w_ref[:, _MZ0:_MX0]).astype(BF16)
    mx_ref[...] = _dot(hn, w_ref[:, _MX0:_AT0]).astype(BF16)
    at_ref[...] = _dot(hn, w_ref[:, _AT0:_HG0]).astype(BF16)
    hg_ref[...] = _dot(hn, w_ref[:, _HG0:_PEND]).astype(BF16)
    dt_rows = lax.dot_general(wdt_ref[...], hn, _NT, preferred_element_type=F32)
    for j in range(TM_PROJ // M_CHUNK):
        dtc_ref[j] = dt_rows[:, j * M_CHUNK:(j + 1) * M_CHUNK]


def in_projection(x, norm_w, w_main, w_dt_rows):
    t = x.shape[0]
    tm = TM_PROJ
    row = lambda w: pl.BlockSpec((tm, w), lambda i: (i, 0))
    full = lambda a: pl.BlockSpec(a.shape, lambda i: (0,) * a.ndim)
    in_specs = [row(D_MODEL), full(norm_w), full(w_main), full(w_dt_rows)]
    widths = (768, 256, 512, 768, 1280)
    out_shape = [jax.ShapeDtypeStruct((t, w), BF16) for w in widths]
    out_shape.append(jax.ShapeDtypeStruct((t // M_CHUNK, 8, M_CHUNK), F32))
    out_specs = [row(w) for w in widths] + [pl.BlockSpec((tm // M_CHUNK, 8, M_CHUNK), lambda i: (i, 0, 0))]
    return pl.pallas_call(
        _inproj_body, grid=(t // tm,), in_specs=in_specs, out_specs=out_specs, out_shape=out_shape,
        compiler_params=_params("parallel"), name="in_projection",
    )(x, norm_w, w_main, w_dt_rows)


TM_OUT = 512


def _outproj_body(x_ref, ya_ref, yb_ref, yc_ref, yd_ref, w_ref, nw_ref, rwh_ref, rwl_ref, xo_ref, xn_ref, lg_ref):
    x = x_ref[...]
    acc = x + _dot(ya_ref[...], w_ref[0]) + _dot(yb_ref[...], w_ref[1])
    acc = acc + _dot(yc_ref[...], w_ref[2]) + _dot(yd_ref[...], w_ref[3])
    xo_ref[...] = acc
    xn = _rms(acc) * nw_ref[...]
    xh = xn.astype(BF16)
    xn_ref[...] = xh
    xl = (xn - xh.astype(F32)).astype(BF16)
    nt = lambda w, a: lax.dot_general(w, a, _NT, preferred_element_type=F32)
    lg_ref[...] = nt(rwh_ref[...], xh) + nt(rwh_ref[...], xl) + nt(rwl_ref[...], xh)


def out_projection(x, ya, yb, yc, yd, w_out4, norm_w, rw_hi, rw_lo):
    t = x.shape[0]
    tm = TM_OUT
    row = lambda w: pl.BlockSpec((tm, w), lambda i: (i, 0))
    full = lambda a: pl.BlockSpec(a.shape, lambda i: (0,) * a.ndim)
    in_specs = [row(D_MODEL)] + [row(DG)] * 4 + [full(w_out4), full(norm_w), full(rw_hi), full(rw_lo)]
    return pl.pallas_call(
        _outproj_body, grid=(t // tm,), in_specs=in_specs,
        out_specs=[row(D_MODEL), row(D_MODEL), pl.BlockSpec((N_EXPERTS, tm), lambda i: (0, i))],
        out_shape=[jax.ShapeDtypeStruct((t, D_MODEL), F32), jax.ShapeDtypeStruct((t, D_MODEL), BF16),
                   jax.ShapeDtypeStruct((N_EXPERTS, t), F32)],
        compiler_params=_params("parallel"), name="out_projection",
    )(x, ya, yb, yc, yd, w_out4, norm_w, rw_hi, rw_lo)


def _prefix_excl_lanes(mask_f32, u_ref):
    e = mask_f32.shape[0]
    off = jnp.zeros((e, 1), F32)
    parts, bounds = [], [off]
    for k in range(SEQ // V7X_LANES):
        tile = mask_f32[:, k * V7X_LANES:(k + 1) * V7X_LANES]
        parts.append(_dot(tile.astype(BF16), u_ref[...]) + off)
        off = off + jnp.sum(tile, axis=1, keepdims=True)
        bounds.append(off)
    return jnp.concatenate(parts, axis=1), bounds


ROUTER_SEQS = 2


def _router_body(lg_ref, u_ref, rank_ref, gate_ref, seg_ref):
    affs = []
    for q in range(ROUTER_SEQS):
        logits = lg_ref[:, q * SEQ:(q + 1) * SEQ]
        ex = jnp.exp(logits - jnp.max(logits, axis=0, keepdims=True))
        affs.append(ex / jnp.sum(ex, axis=0, keepdims=True))
    aff = jnp.concatenate(affs, axis=0)
    nrow = ROUTER_SEQS * N_EXPERTS
    bits = pltpu.bitcast(aff, I32)

    def search(i, thr):
        cand = thr | jnp.left_shift(jnp.int32(1), 30 - i)
        cnt = jnp.sum((bits >= cand).astype(I32), axis=1, keepdims=True)
        return jnp.where(cnt >= CAP, cand, thr)

    thr = lax.fori_loop(0, 31, search, jnp.zeros((nrow, 1), I32))
    gt = (bits > thr).astype(F32)
    eq = (bits == thr).astype(F32)
    need = CAP - jnp.sum(gt, axis=1, keepdims=True)
    tie_rank, _ = _prefix_excl_lanes(eq, u_ref)
    sel = gt + eq * (tie_rank < need).astype(F32)
    rank, bounds = _prefix_excl_lanes(sel, u_ref)
    rank = jnp.where(sel > 0.0, rank, -1.0)
    lane = lax.broadcasted_iota(I32, (nrow, V7X_LANES), 1)
    seg = jnp.zeros((nrow, V7X_LANES), F32)
    tiles = MOE_SEG // V7X_LANES
    for sgm in range(N_MOE_SEG):
        first = jnp.floor(bounds[sgm * tiles] * (1.0 / MOE_ALIGN)) * MOE_ALIGN
        need = jnp.floor((bounds[(sgm + 1) * tiles] - first + (MOE_TILE - 1)) * (1.0 / MOE_TILE))
        need = jnp.max(need.reshape(ROUTER_SEQS, N_EXPERTS, 1), axis=1, keepdims=True)
        need = jnp.broadcast_to(need, (ROUTER_SEQS, N_EXPERTS, 1)).reshape(nrow, 1)
        seg = jnp.where(lane == sgm, first, seg)
        seg = jnp.where(lane == N_MOE_SEG + sgm, need, seg)
    seg = seg.astype(I32)
    for q in range(ROUTER_SEQS):
        rows = slice(q * N_EXPERTS, (q + 1) * N_EXPERTS)
        rank_ref[q] = rank[rows]
        gate_ref[q] = aff[rows]
        seg_ref[q] = seg[rows]


def router(logits_et, u128):
    b = logits_et.shape[1] // SEQ
    out = jax.ShapeDtypeStruct((b, N_EXPERTS, SEQ), F32)
    return pl.pallas_call(
        _router_body, grid=(b // ROUTER_SEQS,),
        in_specs=[pl.BlockSpec((N_EXPERTS, ROUTER_SEQS * SEQ), lambda i: (0, i)),
                  pl.BlockSpec(u128.shape, lambda i: (0, 0))],
        out_specs=[pl.BlockSpec((ROUTER_SEQS, N_EXPERTS, SEQ), lambda i: (i, 0, 0))] * 2
                  + [pl.BlockSpec((ROUTER_SEQS, N_EXPERTS, V7X_LANES), lambda i: (i, 0, 0))],
        out_shape=[out, out, jax.ShapeDtypeStruct((b, N_EXPERTS, V7X_LANES), I32)],
        compiler_params=_params("parallel"), name="router",
    )(logits_et, u128)


MOE_SEG = 256
N_MOE_SEG = SEQ // MOE_SEG
MOE_TILE = 64
MOE_ALIGN = 16
MOE_GROUP = 4
MOE_SEG_STRIDE = 16
MOE_FFN_SEQS = 4


def _moe_seg_plan(cs_ref, b, s):
    base = b * N_EXPERTS * MOE_SEG_STRIDE
    starts = [cs_ref[base + ex * MOE_SEG_STRIDE + s] for ex in range(N_EXPERTS)]
    return starts, cs_ref[base + N_MOE_SEG + s]


def _moe_tile_bases(starts, r):
    own = [st + r * MOE_TILE for st in starts]
    return [pl.multiple_of(jnp.minimum(o, CAP - MOE_TILE), MOE_ALIGN) for o in own], own


def _moe_onehot_group(rank_ref, gate_ref, s, bases, own, grp):
    lanes = pl.ds(pl.multiple_of(s * MOE_SEG, MOE_SEG), MOE_SEG)
    j = lax.broadcasted_iota(I32, (MOE_TILE, MOE_SEG), 0)
    rows = []
    for ex in grp:
        slot = bases[ex] + j
        hit = (rank_ref[0, ex:ex + 1, lanes] == slot.astype(F32)) & (slot >= own[ex])
        val = 1.0 if gate_ref is None else gate_ref[0, ex:ex + 1, lanes]
        rows.append(jnp.where(hit, val, 0.0).astype(BF16))
    return jnp.concatenate(rows, axis=0)


_MOE_GROUPS = [list(range(g * MOE_GROUP, (g + 1) * MOE_GROUP)) for g in range(N_EXPERTS // MOE_GROUP)]


def _moe_gather_body(cs_ref, xn_ref, rank_ref, xe_ref):
    b = pl.program_id(0)

    def zero(ex, _):
        xe_ref[0, ex] = jnp.zeros((CAP, D_MODEL), BF16)
        return 0

    lax.fori_loop(0, N_EXPERTS, zero, 0)

    def seg_gather(s, _):
        starts, rounds = _moe_seg_plan(cs_ref, b, s)
        xn_seg = xn_ref[0, pl.ds(pl.multiple_of(s * MOE_SEG, MOE_SEG), MOE_SEG), :]

        def one_round(r, _):
            bases, own = _moe_tile_bases(starts, r)
            for grp in _MOE_GROUPS:
                got = _dot(_moe_onehot_group(rank_ref, None, s, bases, own, grp), xn_seg)
                for k, ex in enumerate(grp):
                    rows = pl.ds(bases[ex], MOE_TILE)
                    old = xe_ref[0, ex, rows, :].astype(F32)
                    xe_ref[0, ex, rows, :] = (old + got[k * MOE_TILE:(k + 1) * MOE_TILE]).astype(BF16)
            return 0

        lax.fori_loop(0, rounds, one_round, 0)
        return 0

    lax.fori_loop(0, N_MOE_SEG, seg_gather, 0)


def moe_gather(seg_counts_flat, xn3, rank3):
    b = xn3.shape[0]
    grid_spec = pltpu.PrefetchScalarGridSpec(
        num_scalar_prefetch=1, grid=(b,),
        in_specs=[pl.BlockSpec((1, SEQ, D_MODEL), lambda i, cs: (i, 0, 0)),
                  pl.BlockSpec((1, N_EXPERTS, SEQ), lambda i, cs: (i, 0, 0))],
        out_specs=pl.BlockSpec((1, N_EXPERTS, CAP, D_MODEL), lambda i, cs: (i, 0, 0, 0)))
    return pl.pallas_call(
        _moe_gather_body, grid_spec=grid_spec,
        out_shape=jax.ShapeDtypeStruct((b, N_EXPERTS, CAP, D_MODEL), BF16),
        compiler_params=_params("parallel"), name="moe_gather",
    )(seg_counts_flat, xn3, rank3)


def _moe_experts_body(xe_ref, wg_ref, wu_ref, wd_ref, ye_ref, wg_s, wu_s, wd_s):
    @pl.when(pl.program_id(1) == 0)
    def _():
        wg_s[...] = wg_ref[0, 0].astype(BF16)
        wu_s[...] = wu_ref[0, 0].astype(BF16)
        wd_s[...] = wd_ref[0, 0].astype(BF16)

    xe = xe_ref[...].reshape(MOE_FFN_SEQS * CAP, D_MODEL)
    hid = (_silu(_dot(xe, wg_s[...])) * _dot(xe, wu_s[...])).astype(BF16)
    ye_ref[...] = _dot(hid, wd_s[...]).astype(BF16).reshape(MOE_FFN_SEQS, 1, CAP, D_MODEL)


def moe_experts(xe4, w_gate, w_up, w_down, layer):
    b = xe4.shape[0]
    blk = pl.BlockSpec((MOE_FFN_SEQS, 1, CAP, D_MODEL), lambda e, g: (g, e, 0, 0))
    w_spec = lambda a: pl.BlockSpec((1, 1) + a.shape[2:], lambda e, g: (layer, e, 0, 0))
    return pl.pallas_call(
        _moe_experts_body, grid=(N_EXPERTS, b // MOE_FFN_SEQS),
        in_specs=[blk, w_spec(w_gate), w_spec(w_up), w_spec(w_down)],
        out_specs=blk, out_shape=jax.ShapeDtypeStruct(xe4.shape, BF16),
        scratch_shapes=[pltpu.VMEM((D_MODEL, D_FF), BF16), pltpu.VMEM((D_MODEL, D_FF), BF16),
                        pltpu.VMEM((D_FF, D_MODEL), BF16)],
        compiler_params=_params("parallel", "arbitrary"), name="moe_experts",
    )(xe4, w_gate, w_up, w_down)


MOE_SCATTER_SEGS = 4


def _moe_scatter_body(final, cs_ref, ye_ref, rank_ref, gate_ref, xo_ref, nw_ref, o_ref):
    b = pl.program_id(0)
    half = pl.program_id(1)

    def seg_scatter(k, _):
        s = half * MOE_SCATTER_SEGS + k
        starts, rounds = _moe_seg_plan(cs_ref, b, s)
        tok = pl.ds(pl.multiple_of(k * MOE_SEG, MOE_SEG), MOE_SEG)
        o_ref[0, tok, :] = xo_ref[0, tok, :]

        def one_round(r, _):
            bases, own = _moe_tile_bases(starts, r)
            for grp in _MOE_GROUPS:
                ye = jnp.concatenate([ye_ref[0, ex, pl.ds(bases[ex], MOE_TILE), :] for ex in grp], axis=0)
                o_ref[0, tok, :] += lax.dot_general(_moe_onehot_group(rank_ref, gate_ref, s, bases, own, grp), ye,
                                                    _TN, preferred_element_type=F32)
            return 0

        lax.fori_loop(0, rounds, one_round, 0)
        if final:
            o_ref[0, tok, :] = _rms(o_ref[0, tok, :]) * nw_ref[...]
        return 0

    lax.fori_loop(0, MOE_SCATTER_SEGS, seg_scatter, 0)


def moe_scatter(seg_counts_flat, ye4, rank3, gate3, xo3, final_norm_w, final):
    b = ye4.shape[0]
    rows = MOE_SCATTER_SEGS * MOE_SEG
    sel_spec = pl.BlockSpec((1, N_EXPERTS, SEQ), lambda i, j, cs: (i, 0, 0))
    tok_spec = pl.BlockSpec((1, rows, D_MODEL), lambda i, j, cs: (i, j, 0))
    grid_spec = pltpu.PrefetchScalarGridSpec(
        num_scalar_prefetch=1, grid=(b, SEQ // rows),
        in_specs=[pl.BlockSpec((1, N_EXPERTS, CAP, D_MODEL), lambda i, j, cs: (i, 0, 0, 0)), sel_spec, sel_spec,
                  tok_spec, pl.BlockSpec(final_norm_w.shape, lambda i, j, cs: (0, 0))],
        out_specs=tok_spec)
    return pl.pallas_call(
        functools.partial(_moe_scatter_body, final), grid_spec=grid_spec,
        out_shape=jax.ShapeDtypeStruct((b, SEQ, D_MODEL), F32),
        compiler_params=_params("parallel", "arbitrary"), name="moe_scatter",
    )(seg_counts_flat, ye4, rank3, gate3, xo3, final_norm_w)


HY_KB = 256
HY_ROWS = 256


def _hy_filter_body(z_ref, w1_ref, b1_ref, w2_ref, b2_ref, fr_ref, w3_ref, dec_ref, c_ref, s_ref,
                    kr_ref, ki_ref, kny_ref, a_s, d_s):
    @pl.when(pl.program_id(0) == 0)
    def _():
        def rows(c, kny):
            r0 = pl.multiple_of(c * HY_ROWS, HY_ROWS)
            fr = fr_ref[...]
            h = jnp.sin(fr * (_dot_hi(z_ref[pl.ds(r0, HY_ROWS), :], w1_ref[...]) + b1_ref[...]))
            h = jnp.sin(fr * (_dot_hi(h, w2_ref[...]) + b2_ref[...]))
            h = _dot_hi(h, w3_ref[...])
            dec = dec_ref[pl.ds(r0, HY_ROWS), :]
            pos = r0 + lax.broadcasted_iota(I32, (HY_ROWS, DG), 0)
            hf = h[:, :DG] * dec
            hb = jnp.where(pos == 0, 0.0, h[:, DG:] * dec)
            a = hf + hb
            a_s[pl.ds(r0, HY_ROWS), :] = a
            d_s[pl.ds(r0, HY_ROWS), :] = hf - hb
            sgn = (1 - 2 * (pos & 1)).astype(F32)
            return kny + jnp.sum(a * sgn, axis=0, keepdims=True)

        kny = lax.fori_loop(0, SEQ // HY_ROWS, rows, jnp.zeros((1, DG), F32))
        kny_ref[...] = jnp.broadcast_to(kny, kny_ref.shape)

    kr_ref[...] = _dot_hi(c_ref[...], a_s[...])
    ki_ref[...] = _dot_hi(s_ref[...], d_s[...])


def hyena_filter_spectrum(zpos, w1, b1, w2, b2, freq, w3, decay, cos_f32, sin_f32):
    full = lambda a: pl.BlockSpec(a.shape, lambda k: (0,) * a.ndim)
    kblk = pl.BlockSpec((HY_KB, SEQ), lambda k: (k, 0))
    oblk = pl.BlockSpec((HY_KB, DG), lambda k: (k, 0))
    return pl.pallas_call(
        _hy_filter_body, grid=(SEQ // HY_KB,),
        in_specs=[full(zpos), full(w1), full(b1), full(w2), full(b2), full(freq), full(w3), full(decay), kblk, kblk],
        out_specs=[oblk, oblk, pl.BlockSpec((V7X_SUBLANES, DG), lambda k: (0, 0))],
        out_shape=[jax.ShapeDtypeStruct((SEQ, DG), F32), jax.ShapeDtypeStruct((SEQ, DG), F32),
                   jax.ShapeDtypeStruct((V7X_SUBLANES, DG), F32)],
        scratch_shapes=[pltpu.VMEM((SEQ, DG), F32), pltpu.VMEM((SEQ, DG), F32)],
        compiler_params=_params("arbitrary"), name="hyena_filter",
    )(zpos, w1, b1, w2, b2, freq, w3, decay, cos_f32, sin_f32)


CONV_ROWS = 128
CONV_HALO = 8


def _dwconv_rows(pad_ref, w_ref, r0, lanes, k):
    n = CONV_ROWS + 2 * CONV_HALO
    win = pad_ref[pl.ds(r0, n), lanes]
    acc = None
    for j in range(k):
        sh = (k // 2 - j) % n
        rolled = win if sh == 0 else pltpu.roll(win, sh, 0)
        term = rolled[CONV_HALO:CONV_HALO + CONV_ROWS] * w_ref[j:j + 1, lanes]
        acc = term if acc is None else acc + term
    return acc


def _fill_padded(pad_ref, src_ref, width):
    zeros = jnp.zeros((CONV_HALO, width), F32)
    pad_ref[pl.ds(0, CONV_HALO), :] = zeros
    pad_ref[pl.ds(SEQ + CONV_HALO, CONV_HALO), :] = zeros

    def fill(c, _):
        r0 = pl.multiple_of(c * CONV_ROWS, CONV_ROWS)
        pad_ref[pl.ds(r0 + CONV_HALO, CONV_ROWS), :] = src_ref[0, pl.ds(r0, CONV_ROWS), :].astype(F32)
        return 0

    lax.fori_loop(0, SEQ // CONV_ROWS, fill, 0)


def _hy_prep_body(p_ref, w_ref, z_ref, x0_ref, pad):
    _fill_padded(pad, p_ref, 3 * DG)

    def rows(c, _):
        r0 = pl.multiple_of(c * CONV_ROWS, CONV_ROWS)
        x0 = _dwconv_rows(pad, w_ref, r0, slice(0, DG), 3)
        x1 = _dwconv_rows(pad, w_ref, r0, slice(DG, 2 * DG), 3)
        v = _dwconv_rows(pad, w_ref, r0, slice(2 * DG, 3 * DG), 3)
        x0_ref[0, pl.ds(r0, CONV_ROWS), :] = x0.astype(BF16)
        z_ref[0, pl.ds(r0, CONV_ROWS), :] = (v * x1).astype(BF16)
        return 0

    lax.fori_loop(0, SEQ // CONV_ROWS, rows, 0)


def hyena_prep(p3, conv_w):
    b = p3.shape[0]
    blk = pl.BlockSpec((1, SEQ, DG), lambda i: (i, 0, 0))
    out = jax.ShapeDtypeStruct((b, SEQ, DG), BF16)
    return pl.pallas_call(
        _hy_prep_body, grid=(b,),
        in_specs=[pl.BlockSpec((1, SEQ, 3 * DG), lambda i: (i, 0, 0)), pl.BlockSpec(conv_w.shape, lambda i: (0, 0))],
        out_specs=[blk, blk], out_shape=[out, out],
        scratch_shapes=[pltpu.VMEM((SEQ + 2 * CONV_HALO, 3 * DG), F32)],
        compiler_params=_params("parallel"), name="hyena_prep",
    )(p3, conv_w)


HY_FB = 512


def _hy_conv_body(z_ref, x0_ref, t1_ref, t2_ref, kr_ref, ki_ref, kny_ref, fb_ref, o_ref, y_s):
    z = z_ref[0]

    def spectrum(kb, _):
        rows = pl.ds(pl.multiple_of(kb * HY_FB, HY_FB), HY_FB)
        rows_s = pl.ds(pl.multiple_of(SEQ + kb * HY_FB, HY_FB), HY_FB)
        zr = _dot(t1_ref[rows, :], z)
        zi = _dot(t1_ref[rows_s, :], z)
        krow = kb * HY_FB + lax.broadcasted_iota(I32, (HY_FB, 1), 0)
        wk = jnp.where(krow == 0, 1.0 / NFFT, 2.0 / NFFT)
        kr = kr_ref[rows, :]
        ki = ki_ref[rows, :]
        y_s[rows, :] = ((zr * kr - zi * ki) * wk).astype(BF16)
        y_s[rows_s, :] = ((zr * ki + zi * kr) * wk).astype(BF16)
        return 0

    lax.fori_loop(0, SEQ // HY_FB, spectrum, 0)

    zny = jnp.sum(z.astype(F32) * (1 - 2 * (lax.broadcasted_iota(I32, (SEQ, DG), 0) & 1)).astype(F32),
                  axis=0, keepdims=True)
    nyq = zny * kny_ref[0:1, :] * (1.0 / NFFT)

    def synth(tb, _):
        rows = pl.ds(pl.multiple_of(tb * HY_FB, HY_FB), HY_FB)
        conv = _dot(t2_ref[rows, :], y_s[...])
        sgn = (1 - 2 * (lax.broadcasted_iota(I32, (HY_FB, DG), 0) & 1)).astype(F32)
        zf = z_ref[0, rows, :].astype(F32)
        o_ref[0, rows, :] = (x0_ref[0, rows, :].astype(F32) * (conv + nyq * sgn + zf * fb_ref[...])).astype(BF16)
        return 0

    lax.fori_loop(0, SEQ // HY_FB, synth, 0)


def hyena_conv(z3, x03, dft_rows, dft_cols, kr, ki, kny, fbias):
    b = z3.shape[0]
    seq_blk = pl.BlockSpec((1, SEQ, DG), lambda i: (i, 0, 0))
    once = lambda a: pl.BlockSpec(a.shape, lambda i: (0,) * a.ndim, pipeline_mode=pl.Buffered(1))
    return pl.pallas_call(
        _hy_conv_body, grid=(b,),
        in_specs=[seq_blk, seq_blk, once(dft_rows), once(dft_cols), once(kr), once(ki), once(kny), once(fbias)],
        out_specs=seq_blk, out_shape=jax.ShapeDtypeStruct((b, SEQ, DG), BF16),
        scratch_shapes=[pltpu.VMEM((2 * SEQ, DG), BF16)],
        compiler_params=_params("parallel"), name="hyena_conv",
    )(z3, x03, dft_rows, dft_cols, kr, ki, kny, fbias)


N_MCH = SEQ // M_CHUNK
MQ = M_CHUNK


def _head_lane_vec(rows8, base):
    lane_head = lax.broadcasted_iota(I32, (1, DG), 1) // HD
    out = jnp.zeros((1, DG), F32)
    for h in range(N_HEADS):
        out = jnp.where(lane_head == h, rows8[base + h:base + h + 1, :], out)
    return out


def _mamba_body(z_ref, xbc_ref, dtc_ref, cw_ref, cb_ref, dtb_ref, a_ref, dsk_ref, nw_ref, tri_ref, bd_ref,
                o_ref, pad, xs_s, b_s, c_s, y_s, u_s, dec_s, cw_s, yo_s, st_s):
    _fill_padded(pad, xbc_ref, 2 * DG)

    def conv_rows(c, _):
        r0 = pl.multiple_of(c * CONV_ROWS, CONV_ROWS)
        for g in range(4):
            lanes = slice(g * V7X_LANES, (g + 1) * V7X_LANES)
            u = _silu(_dwconv_rows(pad, cw_ref, r0, lanes, M_CONV) + cb_ref[:, lanes])
            if g < 2:
                xs_s[pl.ds(r0, CONV_ROWS), lanes] = u
            elif g == 2:
                b_s[pl.ds(r0, CONV_ROWS), :] = u.astype(BF16)
            else:
                c_s[pl.ds(r0, CONV_ROWS), :] = u.astype(BF16)
        return 0

    lax.fori_loop(0, SEQ // CONV_ROWS, conv_rows, 0)

    li = lax.broadcasted_iota(I32, (MQ, MQ), 0)
    si = lax.broadcasted_iota(I32, (MQ, MQ), 1)
    lower = si <= li
    upper = si >= li
    upper_half = li >= M_STATE
    first_group = si < M_STATE
    bdmask = bd_ref[...]

    def chunk(c, _):
        r0 = pl.multiple_of(c * MQ, MQ)
        dt = _softplus(dtc_ref[0, c] + dtb_ref[...])
        a = dt * a_ref[...]
        cum = _dot01_rhs(a, tri_ref[...])
        tot = cum[:, MQ - 1:MQ]
        suf = tot - cum + a
        row_dir = lax.broadcasted_iota(I32, (8, MQ), 0) // N_HEADS
        seg = jnp.where(row_dir == 0, cum, suf)
        wgt = jnp.exp(tot - seg) * dt
        carry = jnp.exp(seg)
        x = xs_s[pl.ds(r0, MQ), :]
        xb = x.astype(BF16)
        bm = b_s[pl.ds(r0, MQ), :]
        cm = c_s[pl.ds(r0, MQ), :]
        cmf = cm.astype(F32)
        cswap = pltpu.roll(cmf, M_STATE, 1)
        c_dup = [jnp.where(first_group, cmf, cswap), jnp.where(first_group, cswap, cmf)]
        bt = bm.astype(F32).T
        ydiag = []
        for h in range(N_HEADS):
            g = h // 2
            cb = lax.dot_general(cm[:, g * M_STATE:(g + 1) * M_STATE], bm[:, g * M_STATE:(g + 1) * M_STATE],
                                 _NT, preferred_element_type=F32)
            sf = jnp.broadcast_to(seg[h:h + 1, :], (MQ, MQ))
            sb = jnp.broadcast_to(seg[4 + h:5 + h, :], (MQ, MQ))
            lf = jnp.where(lower, jnp.exp(jnp.minimum(sf.T - sf, 0.0)), 0.0)
            lb = jnp.where(upper, jnp.exp(jnp.minimum(sb.T - sb, 0.0)), 0.0)
            m = cb * (lf * dt[h:h + 1, :] + lb * dt[4 + h:5 + h, :])
            ydiag.append(_dot(m.astype(BF16), xb[:, h * HD:(h + 1) * HD]))
        y_s[pl.ds(r0, MQ), :] = jnp.concatenate(ydiag, axis=1)
        for d in range(2):
            bwt = jnp.concatenate([bt[(h // 2) * M_STATE:(h // 2 + 1) * M_STATE, :] * wgt[4 * d + h:4 * d + h + 1, :]
                                   for h in range(N_HEADS)], axis=0)
            u_s[d, c] = (_dot(bwt.astype(BF16), xb) * bdmask).astype(BF16)
            dec_s[d, c] = jnp.broadcast_to(_head_lane_vec(jnp.exp(tot), 4 * d), (V7X_SUBLANES, DG))
            tiles = []
            for g in range(2):
                wrows = jnp.where(upper_half, carry[4 * d + 2 * g + 1:4 * d + 2 * g + 2, :],
                                  carry[4 * d + 2 * g:4 * d + 2 * g + 1, :])
                tiles.append(c_dup[g] * wrows.T)
            cw_s[d, c] = jnp.concatenate(tiles, axis=1).astype(BF16)
        return 0

    lax.fori_loop(0, N_MCH, chunk, 0, unroll=2)

    st_s[...] = jnp.zeros(st_s.shape, F32)

    def scan(i, _):
        for d in range(2):
            c = i if d == 0 else N_MCH - 1 - i
            st = st_s[d]
            yo_s[d, pl.ds(pl.multiple_of(c * MQ, MQ), MQ), :] = _dot(cw_s[d, c], st.astype(BF16))
            st_s[d] = st * dec_s[d, c][0:1, :] + u_s[d, c].astype(F32)
        return 0

    lax.fori_loop(0, N_MCH, scan, 0)

    def finish(c, _):
        r0 = pl.multiple_of(c * CONV_ROWS, CONV_ROWS)
        rows = pl.ds(r0, CONV_ROWS)
        y = y_s[rows, :] + yo_s[0, rows, :] + yo_s[1, rows, :] + xs_s[rows, :] * dsk_ref[...]
        y = y * _silu(z_ref[0, rows, :].astype(F32))
        o_ref[0, pl.ds(r0, CONV_ROWS), :] = (_rms(y) * nw_ref[...]).astype(BF16)
        return 0

    lax.fori_loop(0, SEQ // CONV_ROWS, finish, 0)


def mamba2(z3, xbc3, dtc4, conv_w, conv_b, dt_bias_col, a_col, dskip_lanes, norm_w, tri_incl, bdmask):
    b = z3.shape[0]
    full = lambda a: pl.BlockSpec(a.shape, lambda i: (0,) * a.ndim)
    return pl.pallas_call(
        _mamba_body, grid=(b,),
        in_specs=[pl.BlockSpec((1, SEQ, DG), lambda i: (i, 0, 0)),
                  pl.BlockSpec((1, SEQ, 2 * DG), lambda i: (i, 0, 0)),
                  pl.BlockSpec((1, N_MCH, 8, MQ), lambda i: (i, 0, 0, 0)),
                  full(conv_w), full(conv_b), full(dt_bias_col), full(a_col), full(dskip_lanes), full(norm_w),
                  full(tri_incl), full(bdmask)],
        out_specs=pl.BlockSpec((1, SEQ, DG), lambda i: (i, 0, 0)),
        out_shape=jax.ShapeDtypeStruct((b, SEQ, DG), BF16),
        scratch_shapes=[pltpu.VMEM((SEQ + 2 * CONV_HALO, 2 * DG), F32),
                        pltpu.VMEM((SEQ, DG), F32),
                        pltpu.VMEM((SEQ, 2 * M_STATE), BF16),
                        pltpu.VMEM((SEQ, 2 * M_STATE), BF16),
                        pltpu.VMEM((SEQ, DG), F32),
                        pltpu.VMEM((2, N_MCH, DG, DG), BF16),
                        pltpu.VMEM((2, N_MCH, V7X_SUBLANES, DG), F32),
                        pltpu.VMEM((2, N_MCH, MQ, DG), BF16),
                        pltpu.VMEM((2, SEQ, DG), F32),
                        pltpu.VMEM((2, DG, DG), F32)],
        compiler_params=_params("parallel"), name="mamba2",
    )(z3, xbc3, dtc4, conv_w, conv_b, dt_bias_col, a_col, dskip_lanes, norm_w, tri_incl, bdmask)


A_TQ = 128
A_ROWS = 256
A_KW = A_TQ + 2 * A_BAND


def _attn_bias_body(ids_ref, rb_ref, o_ref):
    ids = ids_ref[0]
    for h in range(N_HEADS):
        acc = jnp.full(ids.shape, NEG_BIG, F32)
        for bkt in range(N_BUCKETS):
            acc = jnp.where(ids == bkt, rb_ref[bkt, h], acc)
        o_ref[h, 0] = acc


def attention_bias_table(bucket_ids, rel_bias):
    nvar, tq, w = bucket_ids.shape
    return pl.pallas_call(
        _attn_bias_body, grid=(nvar,),
        in_specs=[pl.BlockSpec((1, tq, w), lambda v: (v, 0, 0)),
                  pl.BlockSpec(memory_space=pltpu.SMEM)],
        out_specs=pl.BlockSpec((N_HEADS, 1, tq, w), lambda v: (0, v, 0, 0)),
        out_shape=jax.ShapeDtypeStruct((N_HEADS, nvar, tq, w), F32),
        compiler_params=_params("parallel"), name="attention_bias_table",
    )(bucket_ids, rel_bias)


A_SLABS = 3 * DG // V7X_LANES
A_QBLOCKS = SEQ // A_TQ


A_SUB4 = SEQ // 4
A_SUB16 = SEQ // 16


def _attn_body(at_ref, b1_ref, b4_ref, b16_ref, o_ref, qkv_s, x4_s, x16_s, y16_s, y4_s, part_o, part_l):
    def fill(c, _):
        r0 = pl.multiple_of(c * A_ROWS, A_ROWS)
        for s in range(A_SLABS):
            qkv_s[s, pl.ds(r0, A_ROWS), :] = at_ref[0, pl.ds(r0, A_ROWS), s * V7X_LANES:(s + 1) * V7X_LANES].astype(F32)
        return 0

    lax.fori_loop(0, SEQ // A_ROWS, fill, 0)

    def deinterleave(s, _):
        for r4 in range(4):
            for c in range(A_SUB4 // A_ROWS):
                x4_s[s, pl.ds(r4 * A_SUB4 + c * A_ROWS, A_ROWS), :] = \
                    qkv_s[s, pl.ds(r4 + 4 * c * A_ROWS, A_ROWS, stride=4), :]
        for r in range(16):
            x16_s[s, pl.ds(r * A_SUB16, A_SUB16), :] = \
                x4_s[s, pl.ds((r % 4) * A_SUB4 + r // 4, A_SUB16, stride=4), :].astype(BF16)
        return 0

    lax.fori_loop(0, A_SLABS, deinterleave, 0)
    first_head = lax.broadcasted_iota(I32, (A_TQ, V7X_LANES), 1) < HD

    def run_pattern(pat, dil, bias_ref):
        n = SEQ // dil if dil < 16 else SEQ
        nblk = n // A_TQ
        w = A_KW

        def block(it, _):
            r = it // nblk
            i = it - r * nblk
            q0 = i * A_TQ
            k0 = jnp.clip(q0 - A_BAND, 0, n - w)
            var = jnp.where(i == 0, 0, jnp.where(i == nblk - 1, 2, 1))
            for hp in range(2):
                lanes = [slice((2 * part + hp) * V7X_LANES, (2 * part + hp + 1) * V7X_LANES) for part in range(3)]
                if dil == 4:
                    qrows = pl.ds(r + dil * q0, A_TQ, stride=dil)
                    krows = pl.ds(r + dil * k0, w, stride=dil)
                    q2 = qkv_s[hp, qrows, :]
                    k2 = qkv_s[2 + hp, krows, :].astype(BF16)
                    v2 = qkv_s[4 + hp, krows, :].astype(BF16)
                else:
                    qrows = pl.ds(pl.multiple_of(q0, A_TQ), A_TQ)
                    krows = pl.ds(pl.multiple_of(k0, A_BAND), w)
                    if dil == 1:
                        q2, k2, v2 = at_ref[0, qrows, lanes[0]], at_ref[0, krows, lanes[1]], at_ref[0, krows, lanes[2]]
                    else:
                        q2, k2, v2 = x16_s[hp, qrows, :], x16_s[2 + hp, krows, :], x16_s[4 + hp, krows, :]
                q2 = (q2 * (HD ** -0.5)).astype(BF16)
                outs, lses = [], []
                for hh in range(2):
                    keep = first_head if hh == 0 else jnp.logical_not(first_head)
                    qm = jnp.where(keep, q2, jnp.zeros_like(q2))
                    s = lax.dot_general(qm, k2, _NT, preferred_element_type=F32) + bias_ref[2 * hp + hh, var]
                    m = jnp.max(s, axis=1, keepdims=True)
                    p = jnp.exp(s - m)
                    den = jnp.sum(p, axis=1, keepdims=True)
                    outs.append(_dot(p.astype(BF16), v2) / den)
                    lses.append(m + jnp.log(den))
                o_new = jnp.where(first_head, outs[0], outs[1])
                l_new = jnp.where(first_head, lses[0], lses[1])
                if dil == 16:
                    y16_s[0, hp, qrows, :] = o_new
                    y16_s[1, hp, qrows, :] = l_new
                else:
                    part_o[pat, hp, qrows, :] = o_new
                    part_l[pat, hp, qrows, :] = l_new
            return 0

        lax.fori_loop(0, A_QBLOCKS, block, 0, unroll=4)

    for pat, (dil, bias_ref) in enumerate(zip(A_DILS, (b1_ref, b4_ref, b16_ref))):
        run_pattern(pat, dil, bias_ref)

    for a, dst in enumerate((part_o, part_l)):
        for hp in range(2):
            for r in range(16):
                y4_s[a, hp, pl.ds((r % 4) * A_SUB4 + r // 4, A_SUB16, stride=4), :] = \
                    y16_s[a, hp, pl.ds(r * A_SUB16, A_SUB16), :]
            for r4 in range(4):
                for c in range(A_SUB4 // A_ROWS):
                    dst[2, hp, pl.ds(r4 + 4 * c * A_ROWS, A_ROWS, stride=4), :] = \
                        y4_s[a, hp, pl.ds(r4 * A_SUB4 + c * A_ROWS, A_ROWS), :]

    def finish(c, _):
        rows = pl.ds(pl.multiple_of(c * A_TQ, A_TQ), A_TQ)
        for hp in range(2):
            ls = [part_l[pat, hp, rows, :] for pat in range(len(A_DILS))]
            mx = jnp.maximum(jnp.maximum(ls[0], ls[1]), ls[2])
            ws = [jnp.exp(l - mx) for l in ls]
            num = ws[0] * part_o[0, hp, rows, :] + ws[1] * part_o[1, hp, rows, :] + ws[2] * part_o[2, hp, rows, :]
            o_ref[0, rows, hp * V7X_LANES:(hp + 1) * V7X_LANES] = (num / (ws[0] + ws[1] + ws[2])).astype(BF16)
        return 0

    lax.fori_loop(0, SEQ // A_TQ, finish, 0)


def dilated_attention(at3, bias1, bias4, bias16):
    b = at3.shape[0]
    full = lambda a: pl.BlockSpec(a.shape, lambda i: (0,) * a.ndim)
    return pl.pallas_call(
        _attn_body, grid=(b,),
        in_specs=[pl.BlockSpec((1, SEQ, 3 * DG), lambda i: (i, 0, 0)), full(bias1), full(bias4), full(bias16)],
        out_specs=pl.BlockSpec((1, SEQ, DG), lambda i: (i, 0, 0)),
        out_shape=jax.ShapeDtypeStruct((b, SEQ, DG), BF16),
        scratch_shapes=[pltpu.VMEM((A_SLABS, SEQ, V7X_LANES), F32),
                        pltpu.VMEM((A_SLABS, SEQ, V7X_LANES), F32),
                        pltpu.VMEM((A_SLABS, SEQ, V7X_LANES), BF16),
                        pltpu.VMEM((2, 2, SEQ, V7X_LANES), F32),
                        pltpu.VMEM((2, 2, SEQ, V7X_LANES), F32),
                        pltpu.VMEM((len(A_DILS), 2, SEQ, V7X_LANES), F32),
                        pltpu.VMEM((len(A_DILS), 2, SEQ, V7X_LANES), F32)],
        compiler_params=_params("parallel"), name="dilated_attention",
    )(at3, bias1, bias4, bias16)


H_BLK = 256
H_CPB = H_BLK // H_CHUNK
N_HBLK = SEQ // H_BLK
N_HCH = SEQ // H_CHUNK


def _chunk_bcast(x, row):
    c = x.shape[1]
    x3 = x.reshape(H_CPB, H_CHUNK, c)
    return jnp.broadcast_to(x3[:, row:row + 1, :], (H_CPB, H_CHUNK, c)).reshape(H_BLK, c)


def _hgrn_body(p_ref, lb_ref, nw_ref, tin_ref, o_ref, qm_s, ut_s, oi_s, dec_s, oe_s, st_s):
    li = lax.broadcasted_iota(I32, (H_BLK, H_BLK), 0)
    si = lax.broadcasted_iota(I32, (H_BLK, H_BLK), 1)
    same = (li // H_CHUNK) == (si // H_CHUNK)
    mask_f = same & (si <= li)
    mask_b = same & (si >= li)
    lane_head = lax.broadcasted_iota(I32, (1, DG), 1) // HD

    def block(bi, _):
        r0 = pl.multiple_of(bi * H_BLK, H_BLK)
        rows = pl.ds(r0, H_BLK)
        q = _silu(p_ref[0, rows, 0:DG].astype(F32))
        v = p_ref[0, rows, 3 * DG:4 * DG]
        scores = [None] * N_HEADS
        for d in range(2):
            fpre = p_ref[0, rows, (1 + d) * DG:(2 + d) * DG].astype(F32)
            lb = lb_ref[d:d + 1, :]
            sg = jax.nn.sigmoid(fpre)
            g = jnp.log(lb + (1.0 - lb) * sg)
            k = (1.0 - lb) * (1.0 - sg)
            gi = _dot01_2(tin_ref[...], g)
            glast = _chunk_bcast(gi, H_CHUNK - 1)
            if d == 0:
                gc = gi
                gref = _chunk_bcast(gi, H_CHUNK // 2 - 1)
                msk = mask_f
            else:
                gc = glast - gi + g
                gref = _chunk_bcast(gc, H_CHUNK // 2)
                msk = mask_b
            qe = (q * jnp.exp(gc - gref)).astype(BF16)
            ke = (k * jnp.exp(gref - gc)).astype(BF16)
            for h in range(N_HEADS):
                hs = slice(h * HD, (h + 1) * HD)
                sc = jnp.where(msk, lax.dot_general(qe[:, hs], ke[:, hs], _NT, preferred_element_type=F32), 0.0)
                scores[h] = sc if d == 0 else scores[h] + sc
            qd = q * jnp.exp(gc)
            kd = (k * jnp.exp(glast - gc)).astype(BF16)
            for j in range(H_CPB):
                c = bi * H_CPB + j
                cr = slice(j * H_CHUNK, (j + 1) * H_CHUNK)
                qm_s[d, c] = jnp.concatenate([jnp.where(lane_head == h, qd[cr, :], 0.0) for h in range(N_HEADS)],
                                             axis=0).astype(BF16)
                ut = lax.dot_general(v[cr, :], kd[cr, :], _TN, preferred_element_type=F32)
                packed = ut[0:HD, :]
                for h in range(1, N_HEADS):
                    packed = jnp.where(lane_head == h, ut[h * HD:(h + 1) * HD, :], packed)
                ut_s[d, c] = packed.astype(BF16)
                dec_s[d, c] = jnp.broadcast_to(jnp.exp(glast[j * H_CHUNK:j * H_CHUNK + 1, :]), (V7X_SUBLANES, DG))
        for h in range(N_HEADS):
            oi_s[h, rows, :] = _dot(scores[h].astype(BF16), v[:, h * HD:(h + 1) * HD])
        return 0

    lax.fori_loop(0, N_HBLK, block, 0)

    st_s[...] = jnp.zeros(st_s.shape, F32)

    def step(i, _):
        for d in range(2):
            c = i if d == 0 else N_HCH - 1 - i
            rows = pl.ds(pl.multiple_of(c * H_CHUNK, H_CHUNK), H_CHUNK)
            st = st_s[d]
            inter = lax.dot_general(qm_s[d, c], st.astype(BF16), _NT, preferred_element_type=F32)
            for h in range(N_HEADS):
                oe_s[d, h, rows, :] = inter[h * H_CHUNK:(h + 1) * H_CHUNK, :]
            st_s[d] = st * dec_s[d, c][0:1, :] + ut_s[d, c].astype(F32)
        return 0

    lax.fori_loop(0, N_HCH, step, 0, unroll=2)

    def finish(c, _):
        r0 = pl.multiple_of(c * CONV_ROWS, CONV_ROWS)
        rows = pl.ds(r0, CONV_ROWS)
        gate = _silu(p_ref[0, rows, 4 * DG:5 * DG].astype(F32))
        outs = [_rms(oi_s[h, rows, :] + oe_s[0, h, rows, :] + oe_s[1, h, rows, :]) for h in range(N_HEADS)]
        o_ref[0, rows, :] = (jnp.concatenate(outs, axis=1) * nw_ref[...] * gate).astype(BF16)
        return 0

    lax.fori_loop(0, SEQ // CONV_ROWS, finish, 0)


def hgrn2(p3, lb2, norm_w_lanes, tri_in_chunk):
    b = p3.shape[0]
    full = lambda a: pl.BlockSpec(a.shape, lambda i: (0,) * a.ndim)
    return pl.pallas_call(
        _hgrn_body, grid=(b,),
        in_specs=[pl.BlockSpec((1, SEQ, 5 * DG), lambda i: (i, 0, 0)), full(lb2), full(norm_w_lanes),
                  full(tri_in_chunk)],
        out_specs=pl.BlockSpec((1, SEQ, DG), lambda i: (i, 0, 0)),
        out_shape=jax.ShapeDtypeStruct((b, SEQ, DG), BF16),
        scratch_shapes=[pltpu.VMEM((2, N_HCH, N_HEADS * H_CHUNK, DG), BF16),
                        pltpu.VMEM((2, N_HCH, HD, DG), BF16),
                        pltpu.VMEM((N_HEADS, SEQ, HD), F32),
                        pltpu.VMEM((2, N_HCH, V7X_SUBLANES, DG), F32),
                        pltpu.VMEM((2, N_HEADS, SEQ, HD), F32),
                        pltpu.VMEM((2, HD, DG), F32)],
        compiler_params=_params("parallel"), name="hgrn2",
    )(p3, lb2, norm_w_lanes, tri_in_chunk)


@functools.lru_cache(maxsize=None)
def _tables():
    t = {}
    k = np.arange(SEQ, dtype=np.int64)
    ang = 2.0 * np.pi * ((k[:, None] * k[None, :]) % NFFT).astype(np.float64) / NFFT
    t["cos"] = np.cos(ang).astype(np.float32)
    t["sin"] = np.sin(ang).astype(np.float32)
    t["dft_rows"] = np.concatenate([t["cos"], t["sin"]], axis=0).astype(ml_dtypes.bfloat16)
    t["dft_cols"] = np.concatenate([t["cos"], t["sin"]], axis=1).astype(ml_dtypes.bfloat16)
    tt = np.linspace(0.0, 1.0, SEQ, dtype=np.float32)[:, None]
    bands = (HY_POS_DIM - 1) // 2
    ang_pos = (2.0 * math.pi * np.arange(SEQ, dtype=np.float32) / SEQ).astype(np.float32)
    f = np.linspace(1e-4, bands - 1, bands, dtype=np.float32)
    a2 = (ang_pos[:, None] * f[None, :]).astype(np.float32)
    z = np.concatenate([tt, np.cos(a2), -np.sin(a2)], axis=-1).astype(np.float32)
    zp = np.zeros((SEQ, V7X_LANES), np.float32)
    zp[:, :HY_POS_DIM] = z
    t["zpos"] = zp
    max_decay = math.log(1e-2) / 0.3
    min_decay = math.log(1e-2) / 1.5
    deltas = np.abs(np.linspace(min_decay, max_decay, DG, dtype=np.float32))
    t["decay"] = np.exp(-tt * deltas[None, :]).astype(np.float32)
    i128 = np.arange(V7X_LANES)
    t["u128"] = (i128[:, None] < i128[None, :]).astype(np.float32)
    im = np.arange(M_CHUNK)
    t["tri_incl"] = (im[:, None] <= im[None, :]).astype(np.float32)
    ib = np.arange(H_BLK)
    t["tri_in_chunk"] = ((ib[:, None] // H_CHUNK == ib[None, :] // H_CHUNK)
                         & (ib[None, :] <= ib[:, None])).astype(np.float32)
    idg = np.arange(DG)
    t["bdmask"] = (idg[:, None] // HD == idg[None, :] // HD).astype(np.float32)
    def bucket(rel):
        nb = N_BUCKETS // 2
        max_exact = nb // 2
        ret = (rel > 0).astype(np.int64) * nb
        n = np.abs(rel)
        nf = np.maximum(n, 1).astype(np.float64)
        large = max_exact + (np.log(nf / max_exact) / math.log(MAX_DISTANCE / max_exact)
                             * (nb - max_exact)).astype(np.int64)
        large = np.minimum(large, nb - 1)
        return ret + np.where(n < max_exact, n, large)

    for dil in A_DILS:
        n = SEQ // dil
        qi = np.arange(A_TQ)[:, None]
        kj = np.arange(A_KW)[None, :]
        ids = []
        for s0 in (0, -A_BAND, -(A_KW - A_TQ)):
            kk = kj + s0
            rel = kk - qi
            ok = np.abs(rel) <= A_BAND
            if n == A_TQ:
                ok &= (kk >= 0) & (kk < A_TQ)
            ids.append(np.where(ok, bucket(rel * dil), -1))
        t[f"bucket{dil}"] = np.stack(ids).astype(np.int32)
    return t


def kernel(x, w_in, w_out, norm_mix_w, norm_ffn_w, hy_conv_w, hy_pos_w1, hy_pos_b1, hy_pos_w2, hy_pos_b2,
           hy_sin_freq, hy_pos_w3, hy_filt_bias, m_conv_w, m_conv_b, m_dt_bias, m_A_log, m_D, m_norm_w, rel_bias,
           hg_lb, hg_norm_w, router_w, moe_w_gate, moe_w_up, moe_w_down, final_norm_w):
    b = x.shape[0]
    t = b * SEQ
    tb = _tables()
    cos_f32 = jnp.asarray(tb["cos"])
    sin_f32 = jnp.asarray(tb["sin"])
    dft_rows = jnp.asarray(tb["dft_rows"])
    dft_cols = jnp.asarray(tb["dft_cols"])
    u128 = jnp.asarray(tb["u128"]).astype(BF16)
    tri_incl = jnp.asarray(tb["tri_incl"]).astype(BF16)
    tri_in_chunk = jnp.asarray(tb["tri_in_chunk"]).astype(BF16)
    bdmask = jnp.asarray(tb["bdmask"])
    attn_bias = [attention_bias_table(jnp.asarray(tb[f"bucket{d}"]), rel_bias.astype(F32)) for d in A_DILS]

    sm = jax.nn.softmax(hg_lb.astype(F32), axis=0)
    lower_bounds = jnp.cumsum(sm, axis=0) - sm[:1]

    xa = x.reshape(t, D_MODEL)
    for l in range(DEPTH):
        wl = w_in[l]
        w_main = jnp.concatenate([wl[:, 0:768], wl[:, 768:1024], wl[:, 1024:1536], wl[:, 1544:2312],
                                  wl[:, 2312:3592]], axis=1).astype(BF16)
        w_dt_rows = wl[:, 1536:1544].T.astype(BF16)
        hy, mz, mx, at, hg, dtc = in_projection(xa, norm_mix_w[l][None, :], w_main, w_dt_rows)

        w1p = jnp.zeros((V7X_LANES, HY_HID), F32).at[:HY_POS_DIM].set(hy_pos_w1[l])
        kr, ki, kny = hyena_filter_spectrum(
            jnp.asarray(tb["zpos"]), w1p, hy_pos_b1[l][None, :], hy_pos_w2[l], hy_pos_b2[l][None, :],
            hy_sin_freq[l][None, :], hy_pos_w3[l], jnp.asarray(tb["decay"]), cos_f32, sin_f32)
        z3, x03 = hyena_prep(hy.reshape(b, SEQ, 3 * DG), hy_conv_w[l])
        ya = hyena_conv(z3, x03, dft_rows, dft_cols, kr, ki, kny, hy_filt_bias[l][None, :]).reshape(t, DG)

        a_col = (-jnp.exp(m_A_log[l].astype(F32))).reshape(8, 1)
        yb = mamba2(mz.reshape(b, SEQ, DG), mx.reshape(b, SEQ, 2 * DG), dtc.reshape(b, N_MCH, 8, MQ),
                    m_conv_w[l], m_conv_b[l][None, :], m_dt_bias[l].reshape(8, 1), a_col,
                    jnp.repeat(m_D[l].astype(F32), HD)[None, :], m_norm_w[l][None, :], tri_incl, bdmask).reshape(t, DG)

        yc = dilated_attention(at.reshape(b, SEQ, 3 * DG), *attn_bias).reshape(t, DG)

        lbl = lower_bounds[l]
        yd = hgrn2(hg.reshape(b, SEQ, 5 * DG), lbl, jnp.tile(hg_norm_w[l], N_HEADS)[None, :],
                   tri_in_chunk).reshape(t, DG)

        rw_rows = router_w[l].T.astype(F32)
        rw_hi = rw_rows.astype(BF16)
        rw_lo = (rw_rows - rw_hi.astype(F32)).astype(BF16)
        xo, xn, logits = out_projection(xa, ya, yb, yc, yd, w_out[l].reshape(4, DG, D_MODEL).astype(BF16),
                                        norm_ffn_w[l][None, :], rw_hi, rw_lo)
        xn3 = xn.reshape(b, SEQ, D_MODEL)
        rank, gate, seg = router(logits, u128)
        seg_flat = seg[:, :, :MOE_SEG_STRIDE].reshape(-1)
        xe = moe_gather(seg_flat, xn3, rank)
        ye = moe_experts(xe, moe_w_gate, moe_w_up, moe_w_down, l)
        xa = moe_scatter(seg_flat, ye, rank, gate, xo.reshape(b, SEQ, D_MODEL), final_norm_w[None, :],
                         final=(l == DEPTH - 1)).reshape(t, D_MODEL)
    return xa.reshape(b, SEQ, D_MODEL)
```

```python
import functools
import math

import ml_dtypes
import numpy as np
import jax
import jax.numpy as jnp
from jax import lax
from jax.experimental import pallas as pl
from jax.experimental.pallas import tpu as pltpu

F32 = jnp.float32
BF16 = jnp.bfloat16
I32 = jnp.int32

D_MODEL = 1024
SEQ = 2048
DEPTH = 2
DG = 256
N_HEADS = 4
HD = 64
HY_POS_DIM = 33
HY_HID = 64
M_CONV = 5
M_STATE = 64
M_CHUNK = 128
H_CHUNK = 32
A_BAND = 64
A_DILS = (1, 4, 16)
N_BUCKETS = 32
MAX_DISTANCE = 1024
N_EXPERTS = 16
CAP = 2 * SEQ // N_EXPERTS
D_FF = 1024
EPS = 1e-6
NFFT = 2 * SEQ

V7X_LANES = 128
V7X_SUBLANES = 8
V7X_VMEM_LIMIT_BYTES = 56 * 1024 * 1024

NEG_BIG = -1e30

_NT = (((1,), (1,)), ((), ()))
_TN = (((0,), (0,)), ((), ()))


def _params(*sem):
    return pltpu.CompilerParams(dimension_semantics=sem, vmem_limit_bytes=V7X_VMEM_LIMIT_BYTES)


def _dot(a, b):
    return jnp.dot(a, b, preferred_element_type=F32)


def _dot_hi(a, b):
    return jnp.dot(a, b, preferred_element_type=F32, precision=lax.Precision.HIGHEST)


def _dot01_2(t_bf16, x):
    x1 = x.astype(BF16)
    x2 = (x - x1.astype(F32)).astype(BF16)
    return _dot(t_bf16, x1) + _dot(t_bf16, x2)


def _dot01_rhs(x, t_bf16):
    x1 = x.astype(BF16)
    r1 = x - x1.astype(F32)
    x2 = r1.astype(BF16)
    x3 = (r1 - x2.astype(F32)).astype(BF16)
    return _dot(x1, t_bf16) + _dot(x2, t_bf16) + _dot(x3, t_bf16)


def _silu(x):
    return x * jax.nn.sigmoid(x)


def _softplus(x):
    return jnp.maximum(x, 0.0) + jnp.log(1.0 + jnp.exp(-jnp.abs(x)))


def _rms(x):
    return x * lax.rsqrt(jnp.mean(x * x, axis=-1, keepdims=True) + EPS)


TM_PROJ = 1024
_HY0, _MZ0, _MX0, _AT0, _HG0, _PEND = 0, 768, 1024, 1536, 2304, 3584


def _inproj_body(x_ref, nw_ref, w_ref, wdt_ref, hy_ref, mz_ref, mx_ref, at_ref, hg_ref, dtc_ref):
    x = x_ref[...]
    hn = (_rms(x) * nw_ref[...]).astype(BF16)
    hy_ref[...] = _dot(hn, w_ref[:, _HY0:_MZ0]).astype(BF16)
    mz_ref[...] = _dot(hn, w_ref[:, _MZ0:_MX0]).astype(BF16)
    mx_ref[...] = _dot(hn, w_ref[:, _MX0:_AT0]).astype(BF16)
    at_ref[...] = _dot(hn, w_ref[:, _AT0:_HG0]).astype(BF16)
    hg_ref[...] = _dot(hn, w_ref[:, _HG0:_PEND]).astype(BF16)
    dt_rows = lax.dot_general(wdt_ref[...], hn, _NT, preferred_element_type=F32)
    for j in range(TM_PROJ // M_CHUNK):
        dtc_ref[j] = dt_rows[:, j * M_CHUNK:(j + 1) * M_CHUNK]


def in_projection(x, norm_w, w_main, w_dt_rows):
    t = x.shape[0]
    tm = TM_PROJ
    row = lambda w: pl.BlockSpec((tm, w), lambda i: (i, 0))
    full = lambda a: pl.BlockSpec(a.shape, lambda i: (0,) * a.ndim)
    in_specs = [row(D_MODEL), full(norm_w), full(w_main), full(w_dt_rows)]
    widths = (768, 256, 512, 768, 1280)
    out_shape = [jax.ShapeDtypeStruct((t, w), BF16) for w in widths]
    out_shape.append(jax.ShapeDtypeStruct((t // M_CHUNK, 8, M_CHUNK), F32))
    out_specs = [row(w) for w in widths] + [pl.BlockSpec((tm // M_CHUNK, 8, M_CHUNK), lambda i: (i, 0, 0))]
    return pl.pallas_call(
        _inproj_body, grid=(t // tm,), in_specs=in_specs, out_specs=out_specs, out_shape=out_shape,
        compiler_params=_params("parallel"), name="in_projection",
    )(x, norm_w, w_main, w_dt_rows)


TM_OUT = 1024


def _outproj_body(x_ref, ya_ref, yb_ref, yc_ref, yd_ref, w_ref, nw_ref, rwh_ref, rwl_ref, xo_ref, xn_ref, lg_ref):
    x = x_ref[...]
    acc = x + _dot(ya_ref[...], w_ref[0]) + _dot(yb_ref[...], w_ref[1])
    acc = acc + _dot(yc_ref[...], w_ref[2]) + _dot(yd_ref[...], w_ref[3])
    xo_ref[...] = acc
    xn = _rms(acc) * nw_ref[...]
    xh = xn.astype(BF16)
    xn_ref[...] = xh
    xl = (xn - xh.astype(F32)).astype(BF16)
    nt = lambda w, a: lax.dot_general(w, a, _NT, preferred_element_type=F32)
    lg_ref[...] = nt(rwh_ref[...], xh) + nt(rwh_ref[...], xl) + nt(rwl_ref[...], xh)


def out_projection(x, ya, yb, yc, yd, w_out4, norm_w, rw_hi, rw_lo):
    t = x.shape[0]
    tm = TM_OUT
    row = lambda w: pl.BlockSpec((tm, w), lambda i: (i, 0))
    full = lambda a: pl.BlockSpec(a.shape, lambda i: (0,) * a.ndim)
    in_specs = [row(D_MODEL)] + [row(DG)] * 4 + [full(w_out4), full(norm_w), full(rw_hi), full(rw_lo)]
    return pl.pallas_call(
        _outproj_body, grid=(t // tm,), in_specs=in_specs,
        out_specs=[row(D_MODEL), row(D_MODEL), pl.BlockSpec((N_EXPERTS, tm), lambda i: (0, i))],
        out_shape=[jax.ShapeDtypeStruct((t, D_MODEL), F32), jax.ShapeDtypeStruct((t, D_MODEL), BF16),
                   jax.ShapeDtypeStruct((N_EXPERTS, t), F32)],
        compiler_params=_params("parallel"), name="out_projection",
    )(x, ya, yb, yc, yd, w_out4, norm_w, rw_hi, rw_lo)


def _prefix_excl_lanes(mask_f32, u_ref):
    e = mask_f32.shape[0]
    off = jnp.zeros((e, 1), F32)
    parts, bounds = [], [off]
    for k in range(SEQ // V7X_LANES):
        tile = mask_f32[:, k * V7X_LANES:(k + 1) * V7X_LANES]
        parts.append(_dot(tile.astype(BF16), u_ref[...]) + off)
        off = off + jnp.sum(tile, axis=1, keepdims=True)
        bounds.append(off)
    return jnp.concatenate(parts, axis=1), bounds


ROUTER_SEQS = 2


def _router_body(lg_ref, u_ref, rank_ref, gate_ref, seg_ref):
    affs = []
    for q in range(ROUTER_SEQS):
        logits = lg_ref[:, q * SEQ:(q + 1) * SEQ]
        ex = jnp.exp(logits - jnp.max(logits, axis=0, keepdims=True))
        affs.append(ex / jnp.sum(ex, axis=0, keepdims=True))
    aff = jnp.concatenate(affs, axis=0)
    nrow = ROUTER_SEQS * N_EXPERTS
    bits = pltpu.bitcast(aff, I32)

    def search(i, thr):
        cand = thr | jnp.left_shift(jnp.int32(1), 30 - i)
        cnt = jnp.sum((bits >= cand).astype(I32), axis=1, keepdims=True)
        return jnp.where(cnt >= CAP, cand, thr)

    thr = lax.fori_loop(0, 31, search, jnp.zeros((nrow, 1), I32))
    gt = (bits > thr).astype(F32)
    eq = (bits == thr).astype(F32)
    need = CAP - jnp.sum(gt, axis=1, keepdims=True)
    tie_rank, _ = _prefix_excl_lanes(eq, u_ref)
    sel = gt + eq * (tie_rank < need).astype(F32)
    rank, bounds = _prefix_excl_lanes(sel, u_ref)
    rank = jnp.where(sel > 0.0, rank, -1.0)
    lane = lax.broadcasted_iota(I32, (nrow, V7X_LANES), 1)
    seg = jnp.zeros((nrow, V7X_LANES), F32)
    tiles = MOE_SEG // V7X_LANES
    for sgm in range(N_MOE_SEG):
        first = jnp.floor(bounds[sgm * tiles] * (1.0 / MOE_ALIGN)) * MOE_ALIGN
        need = jnp.floor((bounds[(sgm + 1) * tiles] - first + (MOE_TILE - 1)) * (1.0 / MOE_TILE))
        need = jnp.max(need.reshape(ROUTER_SEQS, N_EXPERTS, 1), axis=1, keepdims=True)
        need = jnp.broadcast_to(need, (ROUTER_SEQS, N_EXPERTS, 1)).reshape(nrow, 1)
        seg = jnp.where(lane == sgm, first, seg)
        seg = jnp.where(lane == N_MOE_SEG + sgm, need, seg)
    seg = seg.astype(I32)
    for q in range(ROUTER_SEQS):
        rows = slice(q * N_EXPERTS, (q + 1) * N_EXPERTS)
        rank_ref[q] = rank[rows]
        gate_ref[q] = aff[rows]
        seg_ref[q] = seg[rows]


def router(logits_et, u128):
    b = logits_et.shape[1] // SEQ
    out = jax.ShapeDtypeStruct((b, N_EXPERTS, SEQ), F32)
    return pl.pallas_call(
        _router_body, grid=(b // ROUTER_SEQS,),
        in_specs=[pl.BlockSpec((N_EXPERTS, ROUTER_SEQS * SEQ), lambda i: (0, i)),
                  pl.BlockSpec(u128.shape, lambda i: (0, 0))],
        out_specs=[pl.BlockSpec((ROUTER_SEQS, N_EXPERTS, SEQ), lambda i: (i, 0, 0))] * 2
                  + [pl.BlockSpec((ROUTER_SEQS, N_EXPERTS, V7X_LANES), lambda i: (i, 0, 0))],
        out_shape=[out, out, jax.ShapeDtypeStruct((b, N_EXPERTS, V7X_LANES), I32)],
        compiler_params=_params("parallel"), name="router",
    )(logits_et, u128)


MOE_SEG = 256
N_MOE_SEG = SEQ // MOE_SEG
MOE_TILE = 64
MOE_ALIGN = 16
MOE_GROUP = 4
MOE_SEG_STRIDE = 16
MOE_FFN_SEQS = 4


def _moe_seg_plan(cs_ref, b, s):
    base = b * N_EXPERTS * MOE_SEG_STRIDE
    starts = [cs_ref[base + ex * MOE_SEG_STRIDE + s] for ex in range(N_EXPERTS)]
    return starts, cs_ref[base + N_MOE_SEG + s]


def _moe_tile_bases(starts, r):
    own = [st + r * MOE_TILE for st in starts]
    return [pl.multiple_of(jnp.minimum(o, CAP - MOE_TILE), MOE_ALIGN) for o in own], own


def _moe_onehot_group(rank_ref, gate_ref, s, bases, own, grp):
    lanes = pl.ds(pl.multiple_of(s * MOE_SEG, MOE_SEG), MOE_SEG)
    j = lax.broadcasted_iota(I32, (MOE_TILE, MOE_SEG), 0)
    rows = []
    for ex in grp:
        slot = bases[ex] + j
        hit = (rank_ref[0, ex:ex + 1, lanes] == slot.astype(F32)) & (slot >= own[ex])
        val = 1.0 if gate_ref is None else gate_ref[0, ex:ex + 1, lanes]
        rows.append(jnp.where(hit, val, 0.0).astype(BF16))
    return jnp.concatenate(rows, axis=0)


_MOE_GROUPS = [list(range(g * MOE_GROUP, (g + 1) * MOE_GROUP)) for g in range(N_EXPERTS // MOE_GROUP)]


def _moe_gather_body(cs_ref, xn_ref, rank_ref, xe_ref):
    b = pl.program_id(0)

    def zero(ex, _):
        xe_ref[0, ex] = jnp.zeros((CAP, D_MODEL), BF16)
        return 0

    lax.fori_loop(0, N_EXPERTS, zero, 0)

    def seg_gather(s, _):
        starts, rounds = _moe_seg_plan(cs_ref, b, s)
        xn_seg = xn_ref[0, pl.ds(pl.multiple_of(s * MOE_SEG, MOE_SEG), MOE_SEG), :]

        def one_round(r, _):
            bases, own = _moe_tile_bases(starts, r)
            for grp in _MOE_GROUPS:
                got = _dot(_moe_onehot_group(rank_ref, None, s, bases, own, grp), xn_seg)
                for k, ex in enumerate(grp):
                    rows = pl.ds(bases[ex], MOE_TILE)
                    old = xe_ref[0, ex, rows, :].astype(F32)
                    xe_ref[0, ex, rows, :] = (old + got[k * MOE_TILE:(k + 1) * MOE_TILE]).astype(BF16)
            return 0

        lax.fori_loop(0, rounds, one_round, 0)
        return 0

    lax.fori_loop(0, N_MOE_SEG, seg_gather, 0)


def moe_gather(seg_counts_flat, xn3, rank3):
    b = xn3.shape[0]
    grid_spec = pltpu.PrefetchScalarGridSpec(
        num_scalar_prefetch=1, grid=(b,),
        in_specs=[pl.BlockSpec((1, SEQ, D_MODEL), lambda i, cs: (i, 0, 0)),
                  pl.BlockSpec((1, N_EXPERTS, SEQ), lambda i, cs: (i, 0, 0))],
        out_specs=pl.BlockSpec((1, N_EXPERTS, CAP, D_MODEL), lambda i, cs: (i, 0, 0, 0)))
    return pl.pallas_call(
        _moe_gather_body, grid_spec=grid_spec,
        out_shape=jax.ShapeDtypeStruct((b, N_EXPERTS, CAP, D_MODEL), BF16),
        compiler_params=_params("parallel"), name="moe_gather",
    )(seg_counts_flat, xn3, rank3)


def _moe_experts_body(xe_ref, wg_ref, wu_ref, wd_ref, ye_ref, wg_s, wu_s, wd_s):
    @pl.when(pl.program_id(1) == 0)
    def _():
        wg_s[...] = wg_ref[0, 0].astype(BF16)
        wu_s[...] = wu_ref[0, 0].astype(BF16)
        wd_s[...] = wd_ref[0, 0].astype(BF16)

    xe = xe_ref[...].reshape(MOE_FFN_SEQS * CAP, D_MODEL)
    hid = (_silu(_dot(xe, wg_s[...])) * _dot(xe, wu_s[...])).astype(BF16)
    ye_ref[...] = _dot(hid, wd_s[...]).astype(BF16).reshape(MOE_FFN_SEQS, 1, CAP, D_MODEL)


def moe_experts(xe4, w_gate, w_up, w_down, layer):
    b = xe4.shape[0]
    blk = pl.BlockSpec((MOE_FFN_SEQS, 1, CAP, D_MODEL), lambda e, g: (g, e, 0, 0))
    w_spec = lambda a: pl.BlockSpec((1, 1) + a.shape[2:], lambda e, g: (layer, e, 0, 0))
    return pl.pallas_call(
        _moe_experts_body, grid=(N_EXPERTS, b // MOE_FFN_SEQS),
        in_specs=[blk, w_spec(w_gate), w_spec(w_up), w_spec(w_down)],
        out_specs=blk, out_shape=jax.ShapeDtypeStruct(xe4.shape, BF16),
        scratch_shapes=[pltpu.VMEM((D_MODEL, D_FF), BF16), pltpu.VMEM((D_MODEL, D_FF), BF16),
                        pltpu.VMEM((D_FF, D_MODEL), BF16)],
        compiler_params=_params("parallel", "arbitrary"), name="moe_experts",
    )(xe4, w_gate, w_up, w_down)


MOE_SCATTER_SEGS = 4


def _moe_scatter_body(final, cs_ref, ye_ref, rank_ref, gate_ref, xo_ref, nw_ref, o_ref):
    b = pl.program_id(0)
    half = pl.program_id(1)

    def seg_scatter(k, _):
        s = half * MOE_SCATTER_SEGS + k
        starts, rounds = _moe_seg_plan(cs_ref, b, s)
        tok = pl.ds(pl.multiple_of(k * MOE_SEG, MOE_SEG), MOE_SEG)
        o_ref[0, tok, :] = xo_ref[0, tok, :]

        def one_round(r, _):
            bases, own = _moe_tile_bases(starts, r)
            for grp in _MOE_GROUPS:
                ye = jnp.concatenate([ye_ref[0, ex, pl.ds(bases[ex], MOE_TILE), :] for ex in grp], axis=0)
                o_ref[0, tok, :] += lax.dot_general(_moe_onehot_group(rank_ref, gate_ref, s, bases, own, grp), ye,
                                                    _TN, preferred_element_type=F32)
            return 0

        lax.fori_loop(0, rounds, one_round, 0)
        if final:
            o_ref[0, tok, :] = _rms(o_ref[0, tok, :]) * nw_ref[...]
        return 0

    lax.fori_loop(0, MOE_SCATTER_SEGS, seg_scatter, 0)


def moe_scatter(seg_counts_flat, ye4, rank3, gate3, xo3, final_norm_w, final):
    b = ye4.shape[0]
    rows = MOE_SCATTER_SEGS * MOE_SEG
    sel_spec = pl.BlockSpec((1, N_EXPERTS, SEQ), lambda i, j, cs: (i, 0, 0))
    tok_spec = pl.BlockSpec((1, rows, D_MODEL), lambda i, j, cs: (i, j, 0))
    grid_spec = pltpu.PrefetchScalarGridSpec(
        num_scalar_prefetch=1, grid=(b, SEQ // rows),
        in_specs=[pl.BlockSpec((1, N_EXPERTS, CAP, D_MODEL), lambda i, j, cs: (i, 0, 0, 0)), sel_spec, sel_spec,
                  tok_spec, pl.BlockSpec(final_norm_w.shape, lambda i, j, cs: (0, 0))],
        out_specs=tok_spec)
    return pl.pallas_call(
        functools.partial(_moe_scatter_body, final), grid_spec=grid_spec,
        out_shape=jax.ShapeDtypeStruct((b, SEQ, D_MODEL), F32),
        compiler_params=_params("parallel", "arbitrary"), name="moe_scatter",
    )(seg_counts_flat, ye4, rank3, gate3, xo3, final_norm_w)


HY_KB = 256
HY_ROWS = 256


def _hy_filter_body(z_ref, w1_ref, b1_ref, w2_ref, b2_ref, fr_ref, w3_ref, dec_ref, c_ref, s_ref,
                    kr_ref, ki_ref, kny_ref, a_s, d_s):
    @pl.when(pl.program_id(0) == 0)
    def _():
        def rows(c, kny):
            r0 = pl.multiple_of(c * HY_ROWS, HY_ROWS)
            fr = fr_ref[...]
            h = jnp.sin(fr * (_dot_hi(z_ref[pl.ds(r0, HY_ROWS), :], w1_ref[...]) + b1_ref[...]))
            h = jnp.sin(fr * (_dot_hi(h, w2_ref[...]) + b2_ref[...]))
            h = _dot_hi(h, w3_ref[...])
            dec = dec_ref[pl.ds(r0, HY_ROWS), :]
            pos = r0 + lax.broadcasted_iota(I32, (HY_ROWS, DG), 0)
            hf = h[:, :DG] * dec
            hb = jnp.where(pos == 0, 0.0, h[:, DG:] * dec)
            a = hf + hb
            a_s[pl.ds(r0, HY_ROWS), :] = a
            d_s[pl.ds(r0, HY_ROWS), :] = hf - hb
            sgn = (1 - 2 * (pos & 1)).astype(F32)
            return kny + jnp.sum(a * sgn, axis=0, keepdims=True)

        kny = lax.fori_loop(0, SEQ // HY_ROWS, rows, jnp.zeros((1, DG), F32))
        kny_ref[...] = jnp.broadcast_to(kny, kny_ref.shape)

    kr_ref[...] = _dot_hi(c_ref[...], a_s[...])
    ki_ref[...] = _dot_hi(s_ref[...], d_s[...])


def hyena_filter_spectrum(zpos, w1, b1, w2, b2, freq, w3, decay, cos_f32, sin_f32):
    full = lambda a: pl.BlockSpec(a.shape, lambda k: (0,) * a.ndim)
    kblk = pl.BlockSpec((HY_KB, SEQ), lambda k: (k, 0))
    oblk = pl.BlockSpec((HY_KB, DG), lambda k: (k, 0))
    return pl.pallas_call(
        _hy_filter_body, grid=(SEQ // HY_KB,),
        in_specs=[full(zpos), full(w1), full(b1), full(w2), full(b2), full(freq), full(w3), full(decay), kblk, kblk],
        out_specs=[oblk, oblk, pl.BlockSpec((V7X_SUBLANES, DG), lambda k: (0, 0))],
        out_shape=[jax.ShapeDtypeStruct((SEQ, DG), F32), jax.ShapeDtypeStruct((SEQ, DG), F32),
                   jax.ShapeDtypeStruct((V7X_SUBLANES, DG), F32)],
        scratch_shapes=[pltpu.VMEM((SEQ, DG), F32), pltpu.VMEM((SEQ, DG), F32)],
        compiler_params=_params("arbitrary"), name="hyena_filter",
    )(zpos, w1, b1, w2, b2, freq, w3, decay, cos_f32, sin_f32)


CONV_ROWS = 128
CONV_HALO = 8


def _dwconv_rows(pad_ref, w_ref, r0, lanes, k):
    n = CONV_ROWS + 2 * CONV_HALO
    win = pad_ref[pl.ds(r0, n), lanes]
    acc = None
    for j in range(k):
        sh = (k // 2 - j) % n
        rolled = win if sh == 0 else pltpu.roll(win, sh, 0)
        term = rolled[CONV_HALO:CONV_HALO + CONV_ROWS] * w_ref[j:j + 1, lanes]
        acc = term if acc is None else acc + term
    return acc


def _fill_padded(pad_ref, src_ref, width):
    zeros = jnp.zeros((CONV_HALO, width), F32)
    pad_ref[pl.ds(0, CONV_HALO), :] = zeros
    pad_ref[pl.ds(SEQ + CONV_HALO, CONV_HALO), :] = zeros

    def fill(c, _):
        r0 = pl.multiple_of(c * CONV_ROWS, CONV_ROWS)
        pad_ref[pl.ds(r0 + CONV_HALO, CONV_ROWS), :] = src_ref[0, pl.ds(r0, CONV_ROWS), :].astype(F32)
        return 0

    lax.fori_loop(0, SEQ // CONV_ROWS, fill, 0)


def _hy_prep_body(p_ref, w_ref, z_ref, x0_ref, pad):
    _fill_padded(pad, p_ref, 3 * DG)

    def rows(c, _):
        r0 = pl.multiple_of(c * CONV_ROWS, CONV_ROWS)
        x0 = _dwconv_rows(pad, w_ref, r0, slice(0, DG), 3)
        x1 = _dwconv_rows(pad, w_ref, r0, slice(DG, 2 * DG), 3)
        v = _dwconv_rows(pad, w_ref, r0, slice(2 * DG, 3 * DG), 3)
        x0_ref[0, pl.ds(r0, CONV_ROWS), :] = x0.astype(BF16)
        z_ref[0, pl.ds(r0, CONV_ROWS), :] = (v * x1).astype(BF16)
        return 0

    lax.fori_loop(0, SEQ // CONV_ROWS, rows, 0)


def hyena_prep(p3, conv_w):
    b = p3.shape[0]
    blk = pl.BlockSpec((1, SEQ, DG), lambda i: (i, 0, 0))
    out = jax.ShapeDtypeStruct((b, SEQ, DG), BF16)
    return pl.pallas_call(
        _hy_prep_body, grid=(b,),
        in_specs=[pl.BlockSpec((1, SEQ, 3 * DG), lambda i: (i, 0, 0)), pl.BlockSpec(conv_w.shape, lambda i: (0, 0))],
        out_specs=[blk, blk], out_shape=[out, out],
        scratch_shapes=[pltpu.VMEM((SEQ + 2 * CONV_HALO, 3 * DG), F32)],
        compiler_params=_params("parallel"), name="hyena_prep",
    )(p3, conv_w)


HY_FB = 512


def _hy_conv_body(z_ref, x0_ref, t1_ref, t2_ref, kr_ref, ki_ref, kny_ref, fb_ref, o_ref, y_s):
    z = z_ref[0]

    def spectrum(kb, _):
        rows = pl.ds(pl.multiple_of(kb * HY_FB, HY_FB), HY_FB)
        rows_s = pl.ds(pl.multiple_of(SEQ + kb * HY_FB, HY_FB), HY_FB)
        zr = _dot(t1_ref[rows, :], z)
        zi = _dot(t1_ref[rows_s, :], z)
        krow = kb * HY_FB + lax.broadcasted_iota(I32, (HY_FB, 1), 0)
        wk = jnp.where(krow == 0, 1.0 / NFFT, 2.0 / NFFT)
        kr = kr_ref[rows, :]
        ki = ki_ref[rows, :]
        y_s[rows, :] = ((zr * kr - zi * ki) * wk).astype(BF16)
        y_s[rows_s, :] = ((zr * ki + zi * kr) * wk).astype(BF16)
        return 0

    lax.fori_loop(0, SEQ // HY_FB, spectrum, 0)

    zny = jnp.sum(z.astype(F32) * (1 - 2 * (lax.broadcasted_iota(I32, (SEQ, DG), 0) & 1)).astype(F32),
                  axis=0, keepdims=True)
    nyq = zny * kny_ref[0:1, :] * (1.0 / NFFT)

    def synth(tb, _):
        rows = pl.ds(pl.multiple_of(tb * HY_FB, HY_FB), HY_FB)
        conv = _dot(t2_ref[rows, :], y_s[...])
        sgn = (1 - 2 * (lax.broadcasted_iota(I32, (HY_FB, DG), 0) & 1)).astype(F32)
        zf = z_ref[0, rows, :].astype(F32)
        o_ref[0, rows, :] = (x0_ref[0, rows, :].astype(F32) * (conv + nyq * sgn + zf * fb_ref[...])).astype(BF16)
        return 0

    lax.fori_loop(0, SEQ // HY_FB, synth, 0)


def hyena_conv(z3, x03, dft_rows, dft_cols, kr, ki, kny, fbias):
    b = z3.shape[0]
    seq_blk = pl.BlockSpec((1, SEQ, DG), lambda i: (i, 0, 0))
    once = lambda a: pl.BlockSpec(a.shape, lambda i: (0,) * a.ndim, pipeline_mode=pl.Buffered(1))
    return pl.pallas_call(
        _hy_conv_body, grid=(b,),
        in_specs=[seq_blk, seq_blk, once(dft_rows), once(dft_cols), once(kr), once(ki), once(kny), once(fbias)],
        out_specs=seq_blk, out_shape=jax.ShapeDtypeStruct((b, SEQ, DG), BF16),
        scratch_shapes=[pltpu.VMEM((2 * SEQ, DG), BF16)],
        compiler_params=_params("parallel"), name="hyena_conv",
    )(z3, x03, dft_rows, dft_cols, kr, ki, kny, fbias)


N_MCH = SEQ // M_CHUNK
MQ = M_CHUNK


def _head_lane_vec(rows8, base):
    lane_head = lax.broadcasted_iota(I32, (1, DG), 1) // HD
    out = jnp.zeros((1, DG), F32)
    for h in range(N_HEADS):
        out = jnp.where(lane_head == h, rows8[base + h:base + h + 1, :], out)
    return out


def _mamba_body(z_ref, xbc_ref, dtc_ref, cw_ref, cb_ref, dtb_ref, a_ref, dsk_ref, nw_ref, tri_ref, bd_ref,
                o_ref, pad, xs_s, b_s, c_s, y_s, u_s, dec_s, cw_s, yo_s, st_s):
    _fill_padded(pad, xbc_ref, 2 * DG)

    def conv_rows(c, _):
        r0 = pl.multiple_of(c * CONV_ROWS, CONV_ROWS)
        for g in range(4):
            lanes = slice(g * V7X_LANES, (g + 1) * V7X_LANES)
            u = _silu(_dwconv_rows(pad, cw_ref, r0, lanes, M_CONV) + cb_ref[:, lanes])
            if g < 2:
                xs_s[pl.ds(r0, CONV_ROWS), lanes] = u
            elif g == 2:
                b_s[pl.ds(r0, CONV_ROWS), :] = u.astype(BF16)
            else:
                c_s[pl.ds(r0, CONV_ROWS), :] = u.astype(BF16)
        return 0

    lax.fori_loop(0, SEQ // CONV_ROWS, conv_rows, 0)

    li = lax.broadcasted_iota(I32, (MQ, MQ), 0)
    si = lax.broadcasted_iota(I32, (MQ, MQ), 1)
    lower = si <= li
    upper = si >= li
    upper_half = li >= M_STATE
    first_group = si < M_STATE
    bdmask = bd_ref[...]

    def chunk(c, _):
        r0 = pl.multiple_of(c * MQ, MQ)
        dt = _softplus(dtc_ref[0, c] + dtb_ref[...])
        a = dt * a_ref[...]
        cum = _dot01_rhs(a, tri_ref[...])
        tot = cum[:, MQ - 1:MQ]
        suf = tot - cum + a
        row_dir = lax.broadcasted_iota(I32, (8, MQ), 0) // N_HEADS
        seg = jnp.where(row_dir == 0, cum, suf)
        wgt = jnp.exp(tot - seg) * dt
        carry = jnp.exp(seg)
        x = xs_s[pl.ds(r0, MQ), :]
        xb = x.astype(BF16)
        bm = b_s[pl.ds(r0, MQ), :]
        cm = c_s[pl.ds(r0, MQ), :]
        cmf = cm.astype(F32)
        cswap = pltpu.roll(cmf, M_STATE, 1)
        c_dup = [jnp.where(first_group, cmf, cswap), jnp.where(first_group, cswap, cmf)]
        bt = bm.astype(F32).T
        ydiag = []
        for h in range(N_HEADS):
            g = h // 2
            cb = lax.dot_general(cm[:, g * M_STATE:(g + 1) * M_STATE], bm[:, g * M_STATE:(g + 1) * M_STATE],
                                 _NT, preferred_element_type=F32)
            sf = jnp.broadcast_to(seg[h:h + 1, :], (MQ, MQ))
            sb = jnp.broadcast_to(seg[4 + h:5 + h, :], (MQ, MQ))
            lf = jnp.where(lower, jnp.exp(jnp.minimum(sf.T - sf, 0.0)), 0.0)
            lb = jnp.where(upper, jnp.exp(jnp.minimum(sb.T - sb, 0.0)), 0.0)
            m = cb * (lf * dt[h:h + 1, :] + lb * dt[4 + h:5 + h, :])
            ydiag.append(_dot(m.astype(BF16), xb[:, h * HD:(h + 1) * HD]))
        y_s[pl.ds(r0, MQ), :] = jnp.concatenate(ydiag, axis=1)
        for d in range(2):
            bwt = jnp.concatenate([bt[(h // 2) * M_STATE:(h // 2 + 1) * M_STATE, :] * wgt[4 * d + h:4 * d + h + 1, :]
                                   for h in range(N_HEADS)], axis=0)
            u_s[d, c] = (_dot(bwt.astype(BF16), xb) * bdmask).astype(BF16)
            dec_s[d, c] = jnp.broadcast_to(_head_lane_vec(jnp.exp(tot), 4 * d), (V7X_SUBLANES, DG))
            tiles = []
            for g in range(2):
                wrows = jnp.where(upper_half, carry[4 * d + 2 * g + 1:4 * d + 2 * g + 2, :],
                                  carry[4 * d + 2 * g:4 * d + 2 * g + 1, :])
                tiles.append(c_dup[g] * wrows.T)
            cw_s[d, c] = jnp.concatenate(tiles, axis=1).astype(BF16)
        return 0

    lax.fori_loop(0, N_MCH, chunk, 0, unroll=2)

    st_s[...] = jnp.zeros(st_s.shape, F32)

    def scan(i, _):
        for d in range(2):
            c = i if d == 0 else N_MCH - 1 - i
            st = st_s[d]
            yo_s[d, pl.ds(pl.multiple_of(c * MQ, MQ), MQ), :] = _dot(cw_s[d, c], st.astype(BF16))
            st_s[d] = st * dec_s[d, c][0:1, :] + u_s[d, c].astype(F32)
        return 0

    lax.fori_loop(0, N_MCH, scan, 0, unroll=4)

    def finish(c, _):
        r0 = pl.multiple_of(c * CONV_ROWS, CONV_ROWS)
        rows = pl.ds(r0, CONV_ROWS)
        y = y_s[rows, :] + yo_s[0, rows, :] + yo_s[1, rows, :] + xs_s[rows, :] * dsk_ref[...]
        y = y * _silu(z_ref[0, rows, :].astype(F32))
        o_ref[0, pl.ds(r0, CONV_ROWS), :] = (_rms(y) * nw_ref[...]).astype(BF16)
        return 0

    lax.fori_loop(0, SEQ // CONV_ROWS, finish, 0)


def mamba2(z3, xbc3, dtc4, conv_w, conv_b, dt_bias_col, a_col, dskip_lanes, norm_w, tri_incl, bdmask):
    b = z3.shape[0]
    full = lambda a: pl.BlockSpec(a.shape, lambda i: (0,) * a.ndim)
    return pl.pallas_call(
        _mamba_body, grid=(b,),
        in_specs=[pl.BlockSpec((1, SEQ, DG), lambda i: (i, 0, 0)),
                  pl.BlockSpec((1, SEQ, 2 * DG), lambda i: (i, 0, 0)),
                  pl.BlockSpec((1, N_MCH, 8, MQ), lambda i: (i, 0, 0, 0)),
                  full(conv_w), full(conv_b), full(dt_bias_col), full(a_col), full(dskip_lanes), full(norm_w),
                  full(tri_incl), full(bdmask)],
        out_specs=pl.BlockSpec((1, SEQ, DG), lambda i: (i, 0, 0)),
        out_shape=jax.ShapeDtypeStruct((b, SEQ, DG), BF16),
        scratch_shapes=[pltpu.VMEM((SEQ + 2 * CONV_HALO, 2 * DG), F32),
                        pltpu.VMEM((SEQ, DG), F32),
                        pltpu.VMEM((SEQ, 2 * M_STATE), BF16),
                        pltpu.VMEM((SEQ, 2 * M_STATE), BF16),
                        pltpu.VMEM((SEQ, DG), F32),
                        pltpu.VMEM((2, N_MCH, DG, DG), BF16),
                        pltpu.VMEM((2, N_MCH, V7X_SUBLANES, DG), F32),
                        pltpu.VMEM((2, N_MCH, MQ, DG), BF16),
                        pltpu.VMEM((2, SEQ, DG), F32),
                        pltpu.VMEM((2, DG, DG), F32)],
        compiler_params=_params("parallel"), name="mamba2",
    )(z3, xbc3, dtc4, conv_w, conv_b, dt_bias_col, a_col, dskip_lanes, norm_w, tri_incl, bdmask)


A_TQ = 128
A_ROWS = 256
A_KW = A_TQ + 2 * A_BAND


def _attn_bias_body(ids_ref, rb_ref, o_ref):
    ids = ids_ref[0]
    for h in range(N_HEADS):
        acc = jnp.full(ids.shape, NEG_BIG, F32)
        for bkt in range(N_BUCKETS):
            acc = jnp.where(ids == bkt, rb_ref[bkt, h], acc)
        o_ref[h, 0] = acc


def attention_bias_table(bucket_ids, rel_bias):
    nvar, tq, w = bucket_ids.shape
    return pl.pallas_call(
        _attn_bias_body, grid=(nvar,),
        in_specs=[pl.BlockSpec((1, tq, w), lambda v: (v, 0, 0)),
                  pl.BlockSpec(memory_space=pltpu.SMEM)],
        out_specs=pl.BlockSpec((N_HEADS, 1, tq, w), lambda v: (0, v, 0, 0)),
        out_shape=jax.ShapeDtypeStruct((N_HEADS, nvar, tq, w), F32),
        compiler_params=_params("parallel"), name="attention_bias_table",
    )(bucket_ids, rel_bias)


A_SLABS = 3 * DG // V7X_LANES
A_QBLOCKS = SEQ // A_TQ


A_SUB4 = SEQ // 4
A_SUB16 = SEQ // 16


def _attn_body(at_ref, b1_ref, b4_ref, b16_ref, o_ref, qkv_s, x4_s, x16_s, y16_s, y4_s, part_o, part_l):
    def fill(c, _):
        r0 = pl.multiple_of(c * A_ROWS, A_ROWS)
        for s in range(A_SLABS):
            qkv_s[s, pl.ds(r0, A_ROWS), :] = at_ref[0, pl.ds(r0, A_ROWS), s * V7X_LANES:(s + 1) * V7X_LANES].astype(F32)
        return 0

    lax.fori_loop(0, SEQ // A_ROWS, fill, 0)

    def deinterleave(s, _):
        for r4 in range(4):
            for c in range(A_SUB4 // A_ROWS):
                x4_s[s, pl.ds(r4 * A_SUB4 + c * A_ROWS, A_ROWS), :] = \
                    qkv_s[s, pl.ds(r4 + 4 * c * A_ROWS, A_ROWS, stride=4), :]
        for r in range(16):
            x16_s[s, pl.ds(r * A_SUB16, A_SUB16), :] = \
                x4_s[s, pl.ds((r % 4) * A_SUB4 + r // 4, A_SUB16, stride=4), :].astype(BF16)
        return 0

    lax.fori_loop(0, A_SLABS, deinterleave, 0)
    first_head = lax.broadcasted_iota(I32, (A_TQ, V7X_LANES), 1) < HD

    def run_pattern(pat, dil, bias_ref):
        n = SEQ // dil if dil < 16 else SEQ
        nblk = n // A_TQ
        w = A_KW

        def block(it, _):
            r = it // nblk
            i = it - r * nblk
            q0 = i * A_TQ
            k0 = jnp.clip(q0 - A_BAND, 0, n - w)
            var = jnp.where(i == 0, 0, jnp.where(i == nblk - 1, 2, 1))
            for hp in range(2):
                lanes = [slice((2 * part + hp) * V7X_LANES, (2 * part + hp + 1) * V7X_LANES) for part in range(3)]
                if dil == 4:
                    qrows = pl.ds(r + dil * q0, A_TQ, stride=dil)
                    krows = pl.ds(r + dil * k0, w, stride=dil)
                    q2 = qkv_s[hp, qrows, :]
                    k2 = qkv_s[2 + hp, krows, :].astype(BF16)
                    v2 = qkv_s[4 + hp, krows, :].astype(BF16)
                else:
                    qrows = pl.ds(pl.multiple_of(q0, A_TQ), A_TQ)
                    krows = pl.ds(pl.multiple_of(k0, A_BAND), w)
                    if dil == 1:
                        q2, k2, v2 = at_ref[0, qrows, lanes[0]], at_ref[0, krows, lanes[1]], at_ref[0, krows, lanes[2]]
                    else:
                        q2, k2, v2 = x16_s[hp, qrows, :], x16_s[2 + hp, krows, :], x16_s[4 + hp, krows, :]
                q2 = (q2 * (HD ** -0.5)).astype(BF16)
                outs, lses = [], []
                for hh in range(2):
                    keep = first_head if hh == 0 else jnp.logical_not(first_head)
                    qm = jnp.where(keep, q2, jnp.zeros_like(q2))
                    s = lax.dot_general(qm, k2, _NT, preferred_element_type=F32) + bias_ref[2 * hp + hh, var]
                    m = jnp.max(s, axis=1, keepdims=True)
                    p = jnp.exp(s - m)
                    den = jnp.sum(p, axis=1, keepdims=True)
                    outs.append(_dot(p.astype(BF16), v2) / den)
                    lses.append(m + jnp.log(den))
                o_new = jnp.where(first_head, outs[0], outs[1])
                l_new = jnp.where(first_head, lses[0], lses[1])
                if dil == 16:
                    y16_s[0, hp, qrows, :] = o_new
                    y16_s[1, hp, qrows, :] = l_new
                else:
                    part_o[pat, hp, qrows, :] = o_new
                    part_l[pat, hp, qrows, :] = l_new
            return 0

        lax.fori_loop(0, A_QBLOCKS, block, 0, unroll=4)

    for pat, (dil, bias_ref) in enumerate(zip(A_DILS, (b1_ref, b4_ref, b16_ref))):
        run_pattern(pat, dil, bias_ref)

    for a, dst in enumerate((part_o, part_l)):
        for hp in range(2):
            for r in range(16):
                y4_s[a, hp, pl.ds((r % 4) * A_SUB4 + r // 4, A_SUB16, stride=4), :] = \
                    y16_s[a, hp, pl.ds(r * A_SUB16, A_SUB16), :]
            for r4 in range(4):
                for c in range(A_SUB4 // A_ROWS):
                    dst[2, hp, pl.ds(r4 + 4 * c * A_ROWS, A_ROWS, stride=4), :] = \
                        y4_s[a, hp, pl.ds(r4 * A_SUB4 + c * A_ROWS, A_ROWS), :]

    def finish(c, _):
        rows = pl.ds(pl.multiple_of(c * A_TQ, A_TQ), A_TQ)
        for hp in range(2):
            ls = [part_l[pat, hp, rows, :] for pat in range(len(A_DILS))]
            mx = jnp.maximum(jnp.maximum(ls[0], ls[1]), ls[2])
            ws = [jnp.exp(l - mx) for l in ls]
            num = ws[0] * part_o[0, hp, rows, :] + ws[1] * part_o[1, hp, rows, :] + ws[2] * part_o[2, hp, rows, :]
            o_ref[0, rows, hp * V7X_LANES:(hp + 1) * V7X_LANES] = (num / (ws[0] + ws[1] + ws[2])).astype(BF16)
        return 0

    lax.fori_loop(0, SEQ // A_TQ, finish, 0)


def dilated_attention(at3, bias1, bias4, bias16):
    b = at3.shape[0]
    full = lambda a: pl.BlockSpec(a.shape, lambda i: (0,) * a.ndim)
    return pl.pallas_call(
        _attn_body, grid=(b,),
        in_specs=[pl.BlockSpec((1, SEQ, 3 * DG), lambda i: (i, 0, 0)), full(bias1), full(bias4), full(bias16)],
        out_specs=pl.BlockSpec((1, SEQ, DG), lambda i: (i, 0, 0)),
        out_shape=jax.ShapeDtypeStruct((b, SEQ, DG), BF16),
        scratch_shapes=[pltpu.VMEM((A_SLABS, SEQ, V7X_LANES), F32),
                        pltpu.VMEM((A_SLABS, SEQ, V7X_LANES), F32),
                        pltpu.VMEM((A_SLABS, SEQ, V7X_LANES), BF16),
                        pltpu.VMEM((2, 2, SEQ, V7X_LANES), F32),
                        pltpu.VMEM((2, 2, SEQ, V7X_LANES), F32),
                        pltpu.VMEM((len(A_DILS), 2, SEQ, V7X_LANES), F32),
                        pltpu.VMEM((len(A_DILS), 2, SEQ, V7X_LANES), F32)],
        compiler_params=_params("parallel"), name="dilated_attention",
    )(at3, bias1, bias4, bias16)


H_BLK = 256
H_CPB = H_BLK // H_CHUNK
N_HBLK = SEQ // H_BLK
N_HCH = SEQ // H_CHUNK


def _chunk_bcast(x, row):
    c = x.shape[1]
    x3 = x.reshape(H_CPB, H_CHUNK, c)
    return jnp.broadcast_to(x3[:, row:row + 1, :], (H_CPB, H_CHUNK, c)).reshape(H_BLK, c)


def _hgrn_body(p_ref, lb_ref, nw_ref, tin_ref, o_ref, qm_s, ut_s, oi_s, dec_s, oe_s, st_s):
    li = lax.broadcasted_iota(I32, (H_BLK, H_BLK), 0)
    si = lax.broadcasted_iota(I32, (H_BLK, H_BLK), 1)
    same = (li // H_CHUNK) == (si // H_CHUNK)
    mask_f = same & (si <= li)
    mask_b = same & (si >= li)
    lane_head = lax.broadcasted_iota(I32, (1, DG), 1) // HD

    def block(bi, _):
        r0 = pl.multiple_of(bi * H_BLK, H_BLK)
        rows = pl.ds(r0, H_BLK)
        q = _silu(p_ref[0, rows, 0:DG].astype(F32))
        v = p_ref[0, rows, 3 * DG:4 * DG]
        scores = [None] * N_HEADS
        for d in range(2):
            fpre = p_ref[0, rows, (1 + d) * DG:(2 + d) * DG].astype(F32)
            lb = lb_ref[d:d + 1, :]
            sg = jax.nn.sigmoid(fpre)
            g = jnp.log(lb + (1.0 - lb) * sg)
            k = (1.0 - lb) * (1.0 - sg)
            gi = _dot01_2(tin_ref[...], g)
            glast = _chunk_bcast(gi, H_CHUNK - 1)
            if d == 0:
                gc = gi
                gref = _chunk_bcast(gi, H_CHUNK // 2 - 1)
                msk = mask_f
            else:
                gc = glast - gi + g
                gref = _chunk_bcast(gc, H_CHUNK // 2)
                msk = mask_b
            qe = (q * jnp.exp(gc - gref)).astype(BF16)
            ke = (k * jnp.exp(gref - gc)).astype(BF16)
            for h in range(N_HEADS):
                hs = slice(h * HD, (h + 1) * HD)
                sc = jnp.where(msk, lax.dot_general(qe[:, hs], ke[:, hs], _NT, preferred_element_type=F32), 0.0)
                scores[h] = sc if d == 0 else scores[h] + sc
            qd = q * jnp.exp(gc)
            kd = (k * jnp.exp(glast - gc)).astype(BF16)
            for j in range(H_CPB):
                c = bi * H_CPB + j
                cr = slice(j * H_CHUNK, (j + 1) * H_CHUNK)
                qm_s[d, c] = jnp.concatenate([jnp.where(lane_head == h, qd[cr, :], 0.0) for h in range(N_HEADS)],
                                             axis=0).astype(BF16)
                ut = lax.dot_general(v[cr, :], kd[cr, :], _TN, preferred_element_type=F32)
                packed = ut[0:HD, :]
                for h in range(1, N_HEADS):
                    packed = jnp.where(lane_head == h, ut[h * HD:(h + 1) * HD, :], packed)
                ut_s[d, c] = packed.astype(BF16)
                dec_s[d, c] = jnp.broadcast_to(jnp.exp(glast[j * H_CHUNK:j * H_CHUNK + 1, :]), (V7X_SUBLANES, DG))
        for h in range(N_HEADS):
            oi_s[h, rows, :] = _dot(scores[h].astype(BF16), v[:, h * HD:(h + 1) * HD])
        return 0

    lax.fori_loop(0, N_HBLK, block, 0)

    st_s[...] = jnp.zeros(st_s.shape, F32)

    def step(i, _):
        for d in range(2):
            c = i if d == 0 else N_HCH - 1 - i
            rows = pl.ds(pl.multiple_of(c * H_CHUNK, H_CHUNK), H_CHUNK)
            st = st_s[d]
            inter = lax.dot_general(qm_s[d, c], st.astype(BF16), _NT, preferred_element_type=F32)
            for h in range(N_HEADS):
                oe_s[d, h, rows, :] = inter[h * H_CHUNK:(h + 1) * H_CHUNK, :]
            st_s[d] = st * dec_s[d, c][0:1, :] + ut_s[d, c].astype(F32)
        return 0

    lax.fori_loop(0, N_HCH, step, 0, unroll=8)

    def finish(c, _):
        r0 = pl.multiple_of(c * CONV_ROWS, CONV_ROWS)
        rows = pl.ds(r0, CONV_ROWS)
        gate = _silu(p_ref[0, rows, 4 * DG:5 * DG].astype(F32))
        outs = [_rms(oi_s[h, rows, :] + oe_s[0, h, rows, :] + oe_s[1, h, rows, :]) for h in range(N_HEADS)]
        o_ref[0, rows, :] = (jnp.concatenate(outs, axis=1) * nw_ref[...] * gate).astype(BF16)
        return 0

    lax.fori_loop(0, SEQ // CONV_ROWS, finish, 0)


def hgrn2(p3, lb2, norm_w_lanes, tri_in_chunk):
    b = p3.shape[0]
    full = lambda a: pl.BlockSpec(a.shape, lambda i: (0,) * a.ndim)
    return pl.pallas_call(
        _hgrn_body, grid=(b,),
        in_specs=[pl.BlockSpec((1, SEQ, 5 * DG), lambda i: (i, 0, 0)), full(lb2), full(norm_w_lanes),
                  full(tri_in_chunk)],
        out_specs=pl.BlockSpec((1, SEQ, DG), lambda i: (i, 0, 0)),
        out_shape=jax.ShapeDtypeStruct((b, SEQ, DG), BF16),
        scratch_shapes=[pltpu.VMEM((2, N_HCH, N_HEADS * H_CHUNK, DG), BF16),
                        pltpu.VMEM((2, N_HCH, HD, DG), BF16),
                        pltpu.VMEM((N_HEADS, SEQ, HD), F32),
                        pltpu.VMEM((2, N_HCH, V7X_SUBLANES, DG), F32),
                        pltpu.VMEM((2, N_HEADS, SEQ, HD), F32),
                        pltpu.VMEM((2, HD, DG), F32)],
        compiler_params=_params("parallel"), name="hgrn2",
    )(p3, lb2, norm_w_lanes, tri_in_chunk)


@functools.lru_cache(maxsize=None)
def _tables():
    t = {}
    k = np.arange(SEQ, dtype=np.int64)
    ang = 2.0 * np.pi * ((k[:, None] * k[None, :]) % NFFT).astype(np.float64) / NFFT
    t["cos"] = np.cos(ang).astype(np.float32)
    t["sin"] = np.sin(ang).astype(np.float32)
    t["dft_rows"] = np.concatenate([t["cos"], t["sin"]], axis=0).astype(ml_dtypes.bfloat16)
    t["dft_cols"] = np.concatenate([t["cos"], t["sin"]], axis=1).astype(ml_dtypes.bfloat16)
    tt = np.linspace(0.0, 1.0, SEQ, dtype=np.float32)[:, None]
    bands = (HY_POS_DIM - 1) // 2
    ang_pos = (2.0 * math.pi * np.arange(SEQ, dtype=np.float32) / SEQ).astype(np.float32)
    f = np.linspace(1e-4, bands - 1, bands, dtype=np.float32)
    a2 = (ang_pos[:, None] * f[None, :]).astype(np.float32)
    z = np.concatenate([tt, np.cos(a2), -np.sin(a2)], axis=-1).astype(np.float32)
    zp = np.zeros((SEQ, V7X_LANES), np.float32)
    zp[:, :HY_POS_DIM] = z
    t["zpos"] = zp
    max_decay = math.log(1e-2) / 0.3
    min_decay = math.log(1e-2) / 1.5
    deltas = np.abs(np.linspace(min_decay, max_decay, DG, dtype=np.float32))
    t["decay"] = np.exp(-tt * deltas[None, :]).astype(np.float32)
    i128 = np.arange(V7X_LANES)
    t["u128"] = (i128[:, None] < i128[None, :]).astype(np.float32)
    im = np.arange(M_CHUNK)
    t["tri_incl"] = (im[:, None] <= im[None, :]).astype(np.float32)
    ib = np.arange(H_BLK)
    t["tri_in_chunk"] = ((ib[:, None] // H_CHUNK == ib[None, :] // H_CHUNK)
                         & (ib[None, :] <= ib[:, None])).astype(np.float32)
    idg = np.arange(DG)
    t["bdmask"] = (idg[:, None] // HD == idg[None, :] // HD).astype(np.float32)
    def bucket(rel):
        nb = N_BUCKETS // 2
        max_exact = nb // 2
        ret = (rel > 0).astype(np.int64) * nb
        n = np.abs(rel)
        nf = np.maximum(n, 1).astype(np.float64)
        large = max_exact + (np.log(nf / max_exact) / math.log(MAX_DISTANCE / max_exact)
                             * (nb - max_exact)).astype(np.int64)
        large = np.minimum(large, nb - 1)
        return ret + np.where(n < max_exact, n, large)

    for dil in A_DILS:
        n = SEQ // dil
        qi = np.arange(A_TQ)[:, None]
        kj = np.arange(A_KW)[None, :]
        ids = []
        for s0 in (0, -A_BAND, -(A_KW - A_TQ)):
            kk = kj + s0
            rel = kk - qi
            ok = np.abs(rel) <= A_BAND
            if n == A_TQ:
                ok &= (kk >= 0) & (kk < A_TQ)
            ids.append(np.where(ok, bucket(rel * dil), -1))
        t[f"bucket{dil}"] = np.stack(ids).astype(np.int32)
    return t


def kernel(x, w_in, w_out, norm_mix_w, norm_ffn_w, hy_conv_w, hy_pos_w1, hy_pos_b1, hy_pos_w2, hy_pos_b2,
           hy_sin_freq, hy_pos_w3, hy_filt_bias, m_conv_w, m_conv_b, m_dt_bias, m_A_log, m_D, m_norm_w, rel_bias,
           hg_lb, hg_norm_w, router_w, moe_w_gate, moe_w_up, moe_w_down, final_norm_w):
    b = x.shape[0]
    t = b * SEQ
    tb = _tables()
    cos_f32 = jnp.asarray(tb["cos"])
    sin_f32 = jnp.asarray(tb["sin"])
    dft_rows = jnp.asarray(tb["dft_rows"])
    dft_cols = jnp.asarray(tb["dft_cols"])
    u128 = jnp.asarray(tb["u128"]).astype(BF16)
    tri_incl = jnp.asarray(tb["tri_incl"]).astype(BF16)
    tri_in_chunk = jnp.asarray(tb["tri_in_chunk"]).astype(BF16)
    bdmask = jnp.asarray(tb["bdmask"])
    attn_bias = [attention_bias_table(jnp.asarray(tb[f"bucket{d}"]), rel_bias.astype(F32)) for d in A_DILS]

    sm = jax.nn.softmax(hg_lb.astype(F32), axis=0)
    lower_bounds = jnp.cumsum(sm, axis=0) - sm[:1]

    xa = x.reshape(t, D_MODEL)
    for l in range(DEPTH):
        wl = w_in[l]
        w_main = jnp.concatenate([wl[:, 0:768], wl[:, 768:1024], wl[:, 1024:1536], wl[:, 1544:2312],
                                  wl[:, 2312:3592]], axis=1).astype(BF16)
        w_dt_rows = wl[:, 1536:1544].T.astype(BF16)
        hy, mz, mx, at, hg, dtc = in_projection(xa, norm_mix_w[l][None, :], w_main, w_dt_rows)

        w1p = jnp.zeros((V7X_LANES, HY_HID), F32).at[:HY_POS_DIM].set(hy_pos_w1[l])
        kr, ki, kny = hyena_filter_spectrum(
            jnp.asarray(tb["zpos"]), w1p, hy_pos_b1[l][None, :], hy_pos_w2[l], hy_pos_b2[l][None, :],
            hy_sin_freq[l][None, :], hy_pos_w3[l], jnp.asarray(tb["decay"]), cos_f32, sin_f32)
        z3, x03 = hyena_prep(hy.reshape(b, SEQ, 3 * DG), hy_conv_w[l])
        ya = hyena_conv(z3, x03, dft_rows, dft_cols, kr, ki, kny, hy_filt_bias[l][None, :]).reshape(t, DG)

        a_col = (-jnp.exp(m_A_log[l].astype(F32))).reshape(8, 1)
        yb = mamba2(mz.reshape(b, SEQ, DG), mx.reshape(b, SEQ, 2 * DG), dtc.reshape(b, N_MCH, 8, MQ),
                    m_conv_w[l], m_conv_b[l][None, :], m_dt_bias[l].reshape(8, 1), a_col,
                    jnp.repeat(m_D[l].astype(F32), HD)[None, :], m_norm_w[l][None, :], tri_incl, bdmask).reshape(t, DG)

        yc = dilated_attention(at.reshape(b, SEQ, 3 * DG), *attn_bias).reshape(t, DG)

        lbl = lower_bounds[l]
        yd = hgrn2(hg.reshape(b, SEQ, 5 * DG), lbl, jnp.tile(hg_norm_w[l], N_HEADS)[None, :],
                   tri_in_chunk).reshape(t, DG)

        rw_rows = router_w[l].T.astype(F32)
        rw_hi = rw_rows.astype(BF16)
        rw_lo = (rw_rows - rw_hi.astype(F32)).astype(BF16)
        xo, xn, logits = out_projection(xa, ya, yb, yc, yd, w_out[l].reshape(4, DG, D_MODEL).astype(BF16),
                                        norm_ffn_w[l][None, :], rw_hi, rw_lo)
        xn3 = xn.reshape(b, SEQ, D_MODEL)
        rank, gate, seg = router(logits, u128)
        seg_flat = seg[:, :, :MOE_SEG_STRIDE].reshape(-1)
        xe = moe_gather(seg_flat, xn3, rank)
        ye = moe_experts(xe, moe_w_gate, moe_w_up, moe_w_down, l)
        xa = moe_scatter(seg_flat, ye, rank, gate, xo.reshape(b, SEQ, D_MODEL), final_norm_w[None, :],
                         final=(l == DEPTH - 1)).reshape(t, D_MODEL)
    return xa.reshape(b, SEQ, D_MODEL)
```

```python
import functools
import math

import ml_dtypes
import numpy as np
import jax
import jax.numpy as jnp
from jax import lax
from jax.experimental import pallas as pl
from jax.experimental.pallas import tpu as pltpu

F32 = jnp.float32
BF16 = jnp.bfloat16
I32 = jnp.int32

D_MODEL = 1024
SEQ = 2048
DEPTH = 2
DG = 256
N_HEADS = 4
HD = 64
HY_POS_DIM = 33
HY_HID = 64
M_CONV = 5
M_STATE = 64
M_CHUNK = 128
H_CHUNK = 32
A_BAND = 64
A_DILS = (1, 4, 16)
N_BUCKETS = 32
MAX_DISTANCE = 1024
N_EXPERTS = 16
CAP = 2 * SEQ // N_EXPERTS
D_FF = 1024
EPS = 1e-6
NFFT = 2 * SEQ

V7X_LANES = 128
V7X_SUBLANES = 8
V7X_VMEM_LIMIT_BYTES = 56 * 1024 * 1024

NEG_BIG = -1e30

_NT = (((1,), (1,)), ((), ()))
_TN = (((0,), (0,)), ((), ()))


def _params(*sem):
    return pltpu.CompilerParams(dimension_semantics=sem, vmem_limit_bytes=V7X_VMEM_LIMIT_BYTES)


def _dot(a, b):
    return jnp.dot(a, b, preferred_element_type=F32)


def _dot_hi(a, b):
    return jnp.dot(a, b, preferred_element_type=F32, precision=lax.Precision.HIGHEST)


def _dot01_2(t_bf16, x):
    x1 = x.astype(BF16)
    x2 = (x - x1.astype(F32)).astype(BF16)
    return _dot(t_bf16, x1) + _dot(t_bf16, x2)


def _dot01_rhs(x, t_bf16):
    x1 = x.astype(BF16)
    r1 = x - x1.astype(F32)
    x2 = r1.astype(BF16)
    x3 = (r1 - x2.astype(F32)).astype(BF16)
    return _dot(x1, t_bf16) + _dot(x2, t_bf16) + _dot(x3, t_bf16)


def _silu(x):
    return x * jax.nn.sigmoid(x)


def _softplus(x):
    return jnp.maximum(x, 0.0) + jnp.log(1.0 + jnp.exp(-jnp.abs(x)))


def _rms(x):
    return x * lax.rsqrt(jnp.mean(x * x, axis=-1, keepdims=True) + EPS)


TM_PROJ = 1024
_HY0, _MZ0, _MX0, _AT0, _HG0, _PEND = 0, 768, 1024, 1536, 2304, 3584


def _inproj_body(x_ref, nw_ref, w_ref, wdt_ref, hy_ref, mz_ref, mx_ref, at_ref, hg_ref, dtc_ref):
    x = x_ref[...]
    hn = (_rms(x) * nw_ref[...]).astype(BF16)
    hy_ref[...] = _dot(hn, w_ref[:, _HY0:_MZ0]).astype(BF16)
    mz_ref[...] = _dot(hn, w_ref[:, _MZ0:_MX0]).astype(BF16)
    mx_ref[...] = _dot(hn, w_ref[:, _MX0:_AT0]).astype(BF16)
    at_ref[...] = _dot(hn, w_ref[:, _AT0:_HG0]).astype(BF16)
    hg_ref[...] = _dot(hn, w_ref[:, _HG0:_PEND]).astype(BF16)
    dt_rows = lax.dot_general(wdt_ref[...], hn, _NT, preferred_element_type=F32)
    for j in range(TM_PROJ // M_CHUNK):
        dtc_ref[j] = dt_rows[:, j * M_CHUNK:(j + 1) * M_CHUNK]


def in_projection(x, norm_w, w_main, w_dt_rows):
    t = x.shape[0]
    tm = TM_PROJ
    row = lambda w: pl.BlockSpec((tm, w), lambda i: (i, 0))
    full = lambda a: pl.BlockSpec(a.shape, lambda i: (0,) * a.ndim)
    in_specs = [row(D_MODEL), full(norm_w), full(w_main), full(w_dt_rows)]
    widths = (768, 256, 512, 768, 1280)
    out_shape = [jax.ShapeDtypeStruct((t, w), BF16) for w in widths]
    out_shape.append(jax.ShapeDtypeStruct((t // M_CHUNK, 8, M_CHUNK), F32))
    out_specs = [row(w) for w in widths] + [pl.BlockSpec((tm // M_CHUNK, 8, M_CHUNK), lambda i: (i, 0, 0))]
    return pl.pallas_call(
        _inproj_body, grid=(t // tm,), in_specs=in_specs, out_specs=out_specs, out_shape=out_shape,
        compiler_params=_params("parallel"), name="in_projection",
    )(x, norm_w, w_main, w_dt_rows)


TM_OUT = 1024


def _outproj_body(x_ref, ya_ref, yb_ref, yc_ref, yd_ref, w_ref, nw_ref, rwh_ref, rwl_ref, xo_ref, xn_ref, lg_ref):
    x = x_ref[...]
    acc = x + _dot(ya_ref[...], w_ref[0]) + _dot(yb_ref[...], w_ref[1])
    acc = acc + _dot(yc_ref[...], w_ref[2]) + _dot(yd_ref[...], w_ref[3])
    xo_ref[...] = acc
    xn = _rms(acc) * nw_ref[...]
    xh = xn.astype(BF16)
    xn_ref[...] = xh
    xl = (xn - xh.astype(F32)).astype(BF16)
    nt = lambda w, a: lax.dot_general(w, a, _NT, preferred_element_type=F32)
    lg_ref[...] = nt(rwh_ref[...], xh) + nt(rwh_ref[...], xl) + nt(rwl_ref[...], xh)


def out_projection(x, ya, yb, yc, yd, w_out4, norm_w, rw_hi, rw_lo):
    t = x.shape[0]
    tm = TM_OUT
    row = lambda w: pl.BlockSpec((tm, w), lambda i: (i, 0))
    full = lambda a: pl.BlockSpec(a.shape, lambda i: (0,) * a.ndim)
    in_specs = [row(D_MODEL)] + [row(DG)] * 4 + [full(w_out4), full(norm_w), full(rw_hi), full(rw_lo)]
    return pl.pallas_call(
        _outproj_body, grid=(t // tm,), in_specs=in_specs,
        out_specs=[row(D_MODEL), row(D_MODEL), pl.BlockSpec((N_EXPERTS, tm), lambda i: (0, i))],
        out_shape=[jax.ShapeDtypeStruct((t, D_MODEL), F32), jax.ShapeDtypeStruct((t, D_MODEL), BF16),
                   jax.ShapeDtypeStruct((N_EXPERTS, t), F32)],
        compiler_params=_params("parallel"), name="out_projection",
    )(x, ya, yb, yc, yd, w_out4, norm_w, rw_hi, rw_lo)


def _prefix_excl_lanes(mask_f32, u_ref):
    e = mask_f32.shape[0]
    off = jnp.zeros((e, 1), F32)
    parts, bounds = [], [off]
    for k in range(SEQ // V7X_LANES):
        tile = mask_f32[:, k * V7X_LANES:(k + 1) * V7X_LANES]
        parts.append(_dot(tile.astype(BF16), u_ref[...]) + off)
        off = off + jnp.sum(tile, axis=1, keepdims=True)
        bounds.append(off)
    return jnp.concatenate(parts, axis=1), bounds


ROUTER_SEQS = 2


def _router_body(lg_ref, u_ref, rank_ref, gate_ref, seg_ref):
    affs = []
    for q in range(ROUTER_SEQS):
        logits = lg_ref[:, q * SEQ:(q + 1) * SEQ]
        ex = jnp.exp(logits - jnp.max(logits, axis=0, keepdims=True))
        affs.append(ex / jnp.sum(ex, axis=0, keepdims=True))
    aff = jnp.concatenate(affs, axis=0)
    nrow = ROUTER_SEQS * N_EXPERTS
    bits = pltpu.bitcast(aff, I32)

    def search(i, thr):
        cand = thr | jnp.left_shift(jnp.int32(1), 30 - i)
        cnt = jnp.sum((bits >= cand).astype(I32), axis=1, keepdims=True)
        return jnp.where(cnt >= CAP, cand, thr)

    thr = lax.fori_loop(0, 31, search, jnp.zeros((nrow, 1), I32))
    gt = (bits > thr).astype(F32)
    eq = (bits == thr).astype(F32)
    need = CAP - jnp.sum(gt, axis=1, keepdims=True)
    tie_rank, _ = _prefix_excl_lanes(eq, u_ref)
    sel = gt + eq * (tie_rank < need).astype(F32)
    rank, bounds = _prefix_excl_lanes(sel, u_ref)
    rank = jnp.where(sel > 0.0, rank, -1.0)
    lane = lax.broadcasted_iota(I32, (nrow, V7X_LANES), 1)
    seg = jnp.zeros((nrow, V7X_LANES), F32)
    tiles = MOE_SEG // V7X_LANES
    for sgm in range(N_MOE_SEG):
        first = jnp.floor(bounds[sgm * tiles] * (1.0 / MOE_ALIGN)) * MOE_ALIGN
        need = jnp.floor((bounds[(sgm + 1) * tiles] - first + (MOE_TILE - 1)) * (1.0 / MOE_TILE))
        need = jnp.max(need.reshape(ROUTER_SEQS, N_EXPERTS, 1), axis=1, keepdims=True)
        need = jnp.broadcast_to(need, (ROUTER_SEQS, N_EXPERTS, 1)).reshape(nrow, 1)
        seg = jnp.where(lane == sgm, first, seg)
        seg = jnp.where(lane == N_MOE_SEG + sgm, need, seg)
    seg = seg.astype(I32)
    for q in range(ROUTER_SEQS):
        rows = slice(q * N_EXPERTS, (q + 1) * N_EXPERTS)
        rank_ref[q] = rank[rows]
        gate_ref[q] = aff[rows]
        seg_ref[q] = seg[rows]


def router(logits_et, u128):
    b = logits_et.shape[1] // SEQ
    out = jax.ShapeDtypeStruct((b, N_EXPERTS, SEQ), F32)
    return pl.pallas_call(
        _router_body, grid=(b // ROUTER_SEQS,),
        in_specs=[pl.BlockSpec((N_EXPERTS, ROUTER_SEQS * SEQ), lambda i: (0, i)),
                  pl.BlockSpec(u128.shape, lambda i: (0, 0))],
        out_specs=[pl.BlockSpec((ROUTER_SEQS, N_EXPERTS, SEQ), lambda i: (i, 0, 0))] * 2
                  + [pl.BlockSpec((ROUTER_SEQS, N_EXPERTS, V7X_LANES), lambda i: (i, 0, 0))],
        out_shape=[out, out, jax.ShapeDtypeStruct((b, N_EXPERTS, V7X_LANES), I32)],
        compiler_params=_params("parallel"), name="router",
    )(logits_et, u128)


MOE_SEG = 256
N_MOE_SEG = SEQ // MOE_SEG
MOE_TILE = 64
MOE_ALIGN = 16
MOE_GROUP = 4
MOE_SEG_STRIDE = 16
MOE_FFN_SEQS = 4


def _moe_seg_plan(cs_ref, b, s):
    base = b * N_EXPERTS * MOE_SEG_STRIDE
    starts = [cs_ref[base + ex * MOE_SEG_STRIDE + s] for ex in range(N_EXPERTS)]
    return starts, cs_ref[base + N_MOE_SEG + s]


def _moe_tile_bases(starts, r):
    own = [st + r * MOE_TILE for st in starts]
    return [pl.multiple_of(jnp.minimum(o, CAP - MOE_TILE), MOE_ALIGN) for o in own], own


def _moe_onehot_group(rank_ref, gate_ref, s, bases, own, grp):
    lanes = pl.ds(pl.multiple_of(s * MOE_SEG, MOE_SEG), MOE_SEG)
    j = lax.broadcasted_iota(I32, (MOE_TILE, MOE_SEG), 0)
    rows = []
    for ex in grp:
        slot = bases[ex] + j
        hit = (rank_ref[0, ex:ex + 1, lanes] == slot.astype(F32)) & (slot >= own[ex])
        val = 1.0 if gate_ref is None else gate_ref[0, ex:ex + 1, lanes]
        rows.append(jnp.where(hit, val, 0.0).astype(BF16))
    return jnp.concatenate(rows, axis=0)


_MOE_GROUPS = [list(range(g * MOE_GROUP, (g + 1) * MOE_GROUP)) for g in range(N_EXPERTS // MOE_GROUP)]


def _moe_gather_body(cs_ref, xn_ref, rank_ref, xe_ref):
    b = pl.program_id(0)

    def zero(ex, _):
        xe_ref[0, ex] = jnp.zeros((CAP, D_MODEL), BF16)
        return 0

    lax.fori_loop(0, N_EXPERTS, zero, 0)

    def seg_gather(s, _):
        starts, rounds = _moe_seg_plan(cs_ref, b, s)
        xn_seg = xn_ref[0, pl.ds(pl.multiple_of(s * MOE_SEG, MOE_SEG), MOE_SEG), :]

        def one_round(r, _):
            bases, own = _moe_tile_bases(starts, r)
            for grp in _MOE_GROUPS:
                got = _dot(_moe_onehot_group(rank_ref, None, s, bases, own, grp), xn_seg)
                for k, ex in enumerate(grp):
                    rows = pl.ds(bases[ex], MOE_TILE)
                    old = xe_ref[0, ex, rows, :].astype(F32)
                    xe_ref[0, ex, rows, :] = (old + got[k * MOE_TILE:(k + 1) * MOE_TILE]).astype(BF16)
            return 0

        lax.fori_loop(0, rounds, one_round, 0)
        return 0

    lax.fori_loop(0, N_MOE_SEG, seg_gather, 0)


def moe_gather(seg_counts_flat, xn3, rank3):
    b = xn3.shape[0]
    grid_spec = pltpu.PrefetchScalarGridSpec(
        num_scalar_prefetch=1, grid=(b,),
        in_specs=[pl.BlockSpec((1, SEQ, D_MODEL), lambda i, cs: (i, 0, 0)),
                  pl.BlockSpec((1, N_EXPERTS, SEQ), lambda i, cs: (i, 0, 0))],
        out_specs=pl.BlockSpec((1, N_EXPERTS, CAP, D_MODEL), lambda i, cs: (i, 0, 0, 0)))
    return pl.pallas_call(
        _moe_gather_body, grid_spec=grid_spec,
        out_shape=jax.ShapeDtypeStruct((b, N_EXPERTS, CAP, D_MODEL), BF16),
        compiler_params=_params("parallel"), name="moe_gather",
    )(seg_counts_flat, xn3, rank3)


def _moe_experts_body(xe_ref, wg_ref, wu_ref, wd_ref, ye_ref, wg_s, wu_s, wd_s):
    @pl.when(pl.program_id(1) == 0)
    def _():
        wg_s[...] = wg_ref[0, 0].astype(BF16)
        wu_s[...] = wu_ref[0, 0].astype(BF16)
        wd_s[...] = wd_ref[0, 0].astype(BF16)

    xe = xe_ref[...].reshape(MOE_FFN_SEQS * CAP, D_MODEL)
    hid = (_silu(_dot(xe, wg_s[...])) * _dot(xe, wu_s[...])).astype(BF16)
    ye_ref[...] = _dot(hid, wd_s[...]).astype(BF16).reshape(MOE_FFN_SEQS, 1, CAP, D_MODEL)


def moe_experts(xe4, w_gate, w_up, w_down, layer):
    b = xe4.shape[0]
    blk = pl.BlockSpec((MOE_FFN_SEQS, 1, CAP, D_MODEL), lambda e, g: (g, e, 0, 0))
    w_spec = lambda a: pl.BlockSpec((1, 1) + a.shape[2:], lambda e, g: (layer, e, 0, 0))
    return pl.pallas_call(
        _moe_experts_body, grid=(N_EXPERTS, b // MOE_FFN_SEQS),
        in_specs=[blk, w_spec(w_gate), w_spec(w_up), w_spec(w_down)],
        out_specs=blk, out_shape=jax.ShapeDtypeStruct(xe4.shape, BF16),
        scratch_shapes=[pltpu.VMEM((D_MODEL, D_FF), BF16), pltpu.VMEM((D_MODEL, D_FF), BF16),
                        pltpu.VMEM((D_FF, D_MODEL), BF16)],
        compiler_params=_params("parallel", "arbitrary"), name="moe_experts",
    )(xe4, w_gate, w_up, w_down)


MOE_SCATTER_SEGS = 4


def _moe_scatter_body(final, cs_ref, ye_ref, rank_ref, gate_ref, xo_ref, nw_ref, o_ref):
    b = pl.program_id(0)
    half = pl.program_id(1)

    def seg_scatter(k, _):
        s = half * MOE_SCATTER_SEGS + k
        starts, rounds = _moe_seg_plan(cs_ref, b, s)
        tok = pl.ds(pl.multiple_of(k * MOE_SEG, MOE_SEG), MOE_SEG)
        o_ref[0, tok, :] = xo_ref[0, tok, :]

        def one_round(r, _):
            bases, own = _moe_tile_bases(starts, r)
            for grp in _MOE_GROUPS:
                ye = jnp.concatenate([ye_ref[0, ex, pl.ds(bases[ex], MOE_TILE), :] for ex in grp], axis=0)
                o_ref[0, tok, :] += lax.dot_general(_moe_onehot_group(rank_ref, gate_ref, s, bases, own, grp), ye,
                                                    _TN, preferred_element_type=F32)
            return 0

        lax.fori_loop(0, rounds, one_round, 0)
        if final:
            o_ref[0, tok, :] = _rms(o_ref[0, tok, :]) * nw_ref[...]
        return 0

    lax.fori_loop(0, MOE_SCATTER_SEGS, seg_scatter, 0)


def moe_scatter(seg_counts_flat, ye4, rank3, gate3, xo3, final_norm_w, final):
    b = ye4.shape[0]
    rows = MOE_SCATTER_SEGS * MOE_SEG
    sel_spec = pl.BlockSpec((1, N_EXPERTS, SEQ), lambda i, j, cs: (i, 0, 0))
    tok_spec = pl.BlockSpec((1, rows, D_MODEL), lambda i, j, cs: (i, j, 0))
    grid_spec = pltpu.PrefetchScalarGridSpec(
        num_scalar_prefetch=1, grid=(b, SEQ // rows),
        in_specs=[pl.BlockSpec((1, N_EXPERTS, CAP, D_MODEL), lambda i, j, cs: (i, 0, 0, 0)), sel_spec, sel_spec,
                  tok_spec, pl.BlockSpec(final_norm_w.shape, lambda i, j, cs: (0, 0))],
        out_specs=tok_spec)
    return pl.pallas_call(
        functools.partial(_moe_scatter_body, final), grid_spec=grid_spec,
        out_shape=jax.ShapeDtypeStruct((b, SEQ, D_MODEL), F32),
        compiler_params=_params("parallel", "arbitrary"), name="moe_scatter",
    )(seg_counts_flat, ye4, rank3, gate3, xo3, final_norm_w)


HY_KB = 256
HY_ROWS = 256


def _hy_filter_body(z_ref, w1_ref, b1_ref, w2_ref, b2_ref, fr_ref, w3_ref, dec_ref, c_ref, s_ref,
                    kr_ref, ki_ref, kny_ref, a_s, d_s):
    @pl.when(pl.program_id(0) == 0)
    def _():
        def rows(c, kny):
            r0 = pl.multiple_of(c * HY_ROWS, HY_ROWS)
            fr = fr_ref[...]
            h = jnp.sin(fr * (_dot_hi(z_ref[pl.ds(r0, HY_ROWS), :], w1_ref[...]) + b1_ref[...]))
            h = jnp.sin(fr * (_dot_hi(h, w2_ref[...]) + b2_ref[...]))
            h = _dot_hi(h, w3_ref[...])
            dec = dec_ref[pl.ds(r0, HY_ROWS), :]
            pos = r0 + lax.broadcasted_iota(I32, (HY_ROWS, DG), 0)
            hf = h[:, :DG] * dec
            hb = jnp.where(pos == 0, 0.0, h[:, DG:] * dec)
            a = hf + hb
            a_s[pl.ds(r0, HY_ROWS), :] = a
            d_s[pl.ds(r0, HY_ROWS), :] = hf - hb
            sgn = (1 - 2 * (pos & 1)).astype(F32)
            return kny + jnp.sum(a * sgn, axis=0, keepdims=True)

        kny = lax.fori_loop(0, SEQ // HY_ROWS, rows, jnp.zeros((1, DG), F32))
        kny_ref[...] = jnp.broadcast_to(kny, kny_ref.shape)

    kr_ref[...] = _dot_hi(c_ref[...], a_s[...])
    ki_ref[...] = _dot_hi(s_ref[...], d_s[...])


def hyena_filter_spectrum(zpos, w1, b1, w2, b2, freq, w3, decay, cos_f32, sin_f32):
    full = lambda a: pl.BlockSpec(a.shape, lambda k: (0,) * a.ndim)
    kblk = pl.BlockSpec((HY_KB, SEQ), lambda k: (k, 0))
    oblk = pl.BlockSpec((HY_KB, DG), lambda k: (k, 0))
    return pl.pallas_call(
        _hy_filter_body, grid=(SEQ // HY_KB,),
        in_specs=[full(zpos), full(w1), full(b1), full(w2), full(b2), full(freq), full(w3), full(decay), kblk, kblk],
        out_specs=[oblk, oblk, pl.BlockSpec((V7X_SUBLANES, DG), lambda k: (0, 0))],
        out_shape=[jax.ShapeDtypeStruct((SEQ, DG), F32), jax.ShapeDtypeStruct((SEQ, DG), F32),
                   jax.ShapeDtypeStruct((V7X_SUBLANES, DG), F32)],
        scratch_shapes=[pltpu.VMEM((SEQ, DG), F32), pltpu.VMEM((SEQ, DG), F32)],
        compiler_params=_params("arbitrary"), name="hyena_filter",
    )(zpos, w1, b1, w2, b2, freq, w3, decay, cos_f32, sin_f32)


CONV_ROWS = 128
CONV_HALO = 8


def _dwconv_rows(pad_ref, w_ref, r0, lanes, k):
    n = CONV_ROWS + 2 * CONV_HALO
    win = pad_ref[pl.ds(r0, n), lanes]
    acc = None
    for j in range(k):
        sh = (k // 2 - j) % n
        rolled = win if sh == 0 else pltpu.roll(win, sh, 0)
        term = rolled[CONV_HALO:CONV_HALO + CONV_ROWS] * w_ref[j:j + 1, lanes]
        acc = term if acc is None else acc + term
    return acc


def _fill_padded(pad_ref, src_ref, width):
    zeros = jnp.zeros((CONV_HALO, width), F32)
    pad_ref[pl.ds(0, CONV_HALO), :] = zeros
    pad_ref[pl.ds(SEQ + CONV_HALO, CONV_HALO), :] = zeros

    def fill(c, _):
        r0 = pl.multiple_of(c * CONV_ROWS, CONV_ROWS)
        pad_ref[pl.ds(r0 + CONV_HALO, CONV_ROWS), :] = src_ref[0, pl.ds(r0, CONV_ROWS), :].astype(F32)
        return 0

    lax.fori_loop(0, SEQ // CONV_ROWS, fill, 0)


def _hy_prep_body(p_ref, w_ref, z_ref, x0_ref, pad):
    _fill_padded(pad, p_ref, 3 * DG)

    def rows(c, _):
        r0 = pl.multiple_of(c * CONV_ROWS, CONV_ROWS)
        x0 = _dwconv_rows(pad, w_ref, r0, slice(0, DG), 3)
        x1 = _dwconv_rows(pad, w_ref, r0, slice(DG, 2 * DG), 3)
        v = _dwconv_rows(pad, w_ref, r0, slice(2 * DG, 3 * DG), 3)
        x0_ref[0, pl.ds(r0, CONV_ROWS), :] = x0.astype(BF16)
        z_ref[0, pl.ds(r0, CONV_ROWS), :] = (v * x1).astype(BF16)
        return 0

    lax.fori_loop(0, SEQ // CONV_ROWS, rows, 0)


def hyena_prep(p3, conv_w):
    b = p3.shape[0]
    blk = pl.BlockSpec((1, SEQ, DG), lambda i: (i, 0, 0))
    out = jax.ShapeDtypeStruct((b, SEQ, DG), BF16)
    return pl.pallas_call(
        _hy_prep_body, grid=(b,),
        in_specs=[pl.BlockSpec((1, SEQ, 3 * DG), lambda i: (i, 0, 0)), pl.BlockSpec(conv_w.shape, lambda i: (0, 0))],
        out_specs=[blk, blk], out_shape=[out, out],
        scratch_shapes=[pltpu.VMEM((SEQ + 2 * CONV_HALO, 3 * DG), F32)],
        compiler_params=_params("parallel"), name="hyena_prep",
    )(p3, conv_w)


HY_FB = 512


def _hy_conv_body(z_ref, x0_ref, t1_ref, t2_ref, kr_ref, ki_ref, kny_ref, fb_ref, o_ref, y_s):
    z = z_ref[0]

    def spectrum(kb, _):
        rows = pl.ds(pl.multiple_of(kb * HY_FB, HY_FB), HY_FB)
        rows_s = pl.ds(pl.multiple_of(SEQ + kb * HY_FB, HY_FB), HY_FB)
        zr = _dot(t1_ref[rows, :], z)
        zi = _dot(t1_ref[rows_s, :], z)
        krow = kb * HY_FB + lax.broadcasted_iota(I32, (HY_FB, 1), 0)
        wk = jnp.where(krow == 0, 1.0 / NFFT, 2.0 / NFFT)
        kr = kr_ref[rows, :]
        ki = ki_ref[rows, :]
        y_s[rows, :] = ((zr * kr - zi * ki) * wk).astype(BF16)
        y_s[rows_s, :] = ((zr * ki + zi * kr) * wk).astype(BF16)
        return 0

    lax.fori_loop(0, SEQ // HY_FB, spectrum, 0)

    zny = jnp.sum(z.astype(F32) * (1 - 2 * (lax.broadcasted_iota(I32, (SEQ, DG), 0) & 1)).astype(F32),
                  axis=0, keepdims=True)
    nyq = zny * kny_ref[0:1, :] * (1.0 / NFFT)

    def synth(tb, _):
        rows = pl.ds(pl.multiple_of(tb * HY_FB, HY_FB), HY_FB)
        conv = _dot(t2_ref[rows, :], y_s[...])
        sgn = (1 - 2 * (lax.broadcasted_iota(I32, (HY_FB, DG), 0) & 1)).astype(F32)
        zf = z_ref[0, rows, :].astype(F32)
        o_ref[0, rows, :] = (x0_ref[0, rows, :].astype(F32) * (conv + nyq * sgn + zf * fb_ref[...])).astype(BF16)
        return 0

    lax.fori_loop(0, SEQ // HY_FB, synth, 0)


def hyena_conv(z3, x03, dft_rows, dft_cols, kr, ki, kny, fbias):
    b = z3.shape[0]
    seq_blk = pl.BlockSpec((1, SEQ, DG), lambda i: (i, 0, 0))
    once = lambda a: pl.BlockSpec(a.shape, lambda i: (0,) * a.ndim, pipeline_mode=pl.Buffered(1))
    return pl.pallas_call(
        _hy_conv_body, grid=(b,),
        in_specs=[seq_blk, seq_blk, once(dft_rows), once(dft_cols), once(kr), once(ki), once(kny), once(fbias)],
        out_specs=seq_blk, out_shape=jax.ShapeDtypeStruct((b, SEQ, DG), BF16),
        scratch_shapes=[pltpu.VMEM((2 * SEQ, DG), BF16)],
        compiler_params=_params("parallel"), name="hyena_conv",
    )(z3, x03, dft_rows, dft_cols, kr, ki, kny, fbias)


N_MCH = SEQ // M_CHUNK
MQ = M_CHUNK


def _head_lane_vec(rows8, base):
    lane_head = lax.broadcasted_iota(I32, (1, DG), 1) // HD
    out = jnp.zeros((1, DG), F32)
    for h in range(N_HEADS):
        out = jnp.where(lane_head == h, rows8[base + h:base + h + 1, :], out)
    return out


def _mamba_body(z_ref, xbc_ref, dtc_ref, cw_ref, cb_ref, dtb_ref, a_ref, dsk_ref, nw_ref, tri_ref, bd_ref,
                o_ref, pad, xs_s, b_s, c_s, y_s, u_s, dec_s, cw_s, yo_s, st_s):
    _fill_padded(pad, xbc_ref, 2 * DG)

    def conv_rows(c, _):
        r0 = pl.multiple_of(c * CONV_ROWS, CONV_ROWS)
        for g in range(4):
            lanes = slice(g * V7X_LANES, (g + 1) * V7X_LANES)
            u = _silu(_dwconv_rows(pad, cw_ref, r0, lanes, M_CONV) + cb_ref[:, lanes])
            if g < 2:
                xs_s[pl.ds(r0, CONV_ROWS), lanes] = u
            elif g == 2:
                b_s[pl.ds(r0, CONV_ROWS), :] = u.astype(BF16)
            else:
                c_s[pl.ds(r0, CONV_ROWS), :] = u.astype(BF16)
        return 0

    lax.fori_loop(0, SEQ // CONV_ROWS, conv_rows, 0)

    li = lax.broadcasted_iota(I32, (MQ, MQ), 0)
    si = lax.broadcasted_iota(I32, (MQ, MQ), 1)
    lower = si <= li
    upper = si >= li
    upper_half = li >= M_STATE
    first_group = si < M_STATE
    bdmask = bd_ref[...]

    def chunk(c, _):
        r0 = pl.multiple_of(c * MQ, MQ)
        dt = _softplus(dtc_ref[0, c] + dtb_ref[...])
        a = dt * a_ref[...]
        cum = _dot01_rhs(a, tri_ref[...])
        tot = cum[:, MQ - 1:MQ]
        suf = tot - cum + a
        row_dir = lax.broadcasted_iota(I32, (8, MQ), 0) // N_HEADS
        seg = jnp.where(row_dir == 0, cum, suf)
        wgt = jnp.exp(tot - seg) * dt
        carry = jnp.exp(seg)
        x = xs_s[pl.ds(r0, MQ), :]
        xb = x.astype(BF16)
        bm = b_s[pl.ds(r0, MQ), :]
        cm = c_s[pl.ds(r0, MQ), :]
        cmf = cm.astype(F32)
        cswap = pltpu.roll(cmf, M_STATE, 1)
        c_dup = [jnp.where(first_group, cmf, cswap), jnp.where(first_group, cswap, cmf)]
        bt = bm.astype(F32).T
        ydiag = []
        for h in range(N_HEADS):
            g = h // 2
            cb = lax.dot_general(cm[:, g * M_STATE:(g + 1) * M_STATE], bm[:, g * M_STATE:(g + 1) * M_STATE],
                                 _NT, preferred_element_type=F32)
            sf = jnp.broadcast_to(seg[h:h + 1, :], (MQ, MQ))
            sb = jnp.broadcast_to(seg[4 + h:5 + h, :], (MQ, MQ))
            lf = jnp.where(lower, jnp.exp(jnp.minimum(sf.T - sf, 0.0)), 0.0)
            lb = jnp.where(upper, jnp.exp(jnp.minimum(sb.T - sb, 0.0)), 0.0)
            m = cb * (lf * dt[h:h + 1, :] + lb * dt[4 + h:5 + h, :])
            ydiag.append(_dot(m.astype(BF16), xb[:, h * HD:(h + 1) * HD]))
        y_s[pl.ds(r0, MQ), :] = jnp.concatenate(ydiag, axis=1)
        for d in range(2):
            bwt = jnp.concatenate([bt[(h // 2) * M_STATE:(h // 2 + 1) * M_STATE, :] * wgt[4 * d + h:4 * d + h + 1, :]
                                   for h in range(N_HEADS)], axis=0)
            u_s[d, c] = (_dot(bwt.astype(BF16), xb) * bdmask).astype(BF16)
            dec_s[d, c] = jnp.broadcast_to(_head_lane_vec(jnp.exp(tot), 4 * d), (V7X_SUBLANES, DG))
            tiles = []
            for g in range(2):
                wrows = jnp.where(upper_half, carry[4 * d + 2 * g + 1:4 * d + 2 * g + 2, :],
                                  carry[4 * d + 2 * g:4 * d + 2 * g + 1, :])
                tiles.append(c_dup[g] * wrows.T)
            cw_s[d, c] = jnp.concatenate(tiles, axis=1).astype(BF16)
        return 0

    lax.fori_loop(0, N_MCH, chunk, 0, unroll=4)

    st_s[...] = jnp.zeros(st_s.shape, F32)

    def scan(i, _):
        for d in range(2):
            c = i if d == 0 else N_MCH - 1 - i
            st = st_s[d]
            yo_s[d, pl.ds(pl.multiple_of(c * MQ, MQ), MQ), :] = _dot(cw_s[d, c], st.astype(BF16))
            st_s[d] = st * dec_s[d, c][0:1, :] + u_s[d, c].astype(F32)
        return 0

    lax.fori_loop(0, N_MCH, scan, 0, unroll=4)

    def finish(c, _):
        r0 = pl.multiple_of(c * CONV_ROWS, CONV_ROWS)
        rows = pl.ds(r0, CONV_ROWS)
        y = y_s[rows, :] + yo_s[0, rows, :] + yo_s[1, rows, :] + xs_s[rows, :] * dsk_ref[...]
        y = y * _silu(z_ref[0, rows, :].astype(F32))
        o_ref[0, pl.ds(r0, CONV_ROWS), :] = (_rms(y) * nw_ref[...]).astype(BF16)
        return 0

    lax.fori_loop(0, SEQ // CONV_ROWS, finish, 0, unroll=2)


def mamba2(z3, xbc3, dtc4, conv_w, conv_b, dt_bias_col, a_col, dskip_lanes, norm_w, tri_incl, bdmask):
    b = z3.shape[0]
    full = lambda a: pl.BlockSpec(a.shape, lambda i: (0,) * a.ndim)
    return pl.pallas_call(
        _mamba_body, grid=(b,),
        in_specs=[pl.BlockSpec((1, SEQ, DG), lambda i: (i, 0, 0)),
                  pl.BlockSpec((1, SEQ, 2 * DG), lambda i: (i, 0, 0)),
                  pl.BlockSpec((1, N_MCH, 8, MQ), lambda i: (i, 0, 0, 0)),
                  full(conv_w), full(conv_b), full(dt_bias_col), full(a_col), full(dskip_lanes), full(norm_w),
                  full(tri_incl), full(bdmask)],
        out_specs=pl.BlockSpec((1, SEQ, DG), lambda i: (i, 0, 0)),
        out_shape=jax.ShapeDtypeStruct((b, SEQ, DG), BF16),
        scratch_shapes=[pltpu.VMEM((SEQ + 2 * CONV_HALO, 2 * DG), F32),
                        pltpu.VMEM((SEQ, DG), F32),
                        pltpu.VMEM((SEQ, 2 * M_STATE), BF16),
                        pltpu.VMEM((SEQ, 2 * M_STATE), BF16),
                        pltpu.VMEM((SEQ, DG), F32),
                        pltpu.VMEM((2, N_MCH, DG, DG), BF16),
                        pltpu.VMEM((2, N_MCH, V7X_SUBLANES, DG), F32),
                        pltpu.VMEM((2, N_MCH, MQ, DG), BF16),
                        pltpu.VMEM((2, SEQ, DG), F32),
                        pltpu.VMEM((2, DG, DG), F32)],
        compiler_params=_params("parallel"), name="mamba2",
    )(z3, xbc3, dtc4, conv_w, conv_b, dt_bias_col, a_col, dskip_lanes, norm_w, tri_incl, bdmask)


A_TQ = 128
A_ROWS = 256
A_KW = A_TQ + 2 * A_BAND


def _attn_bias_body(ids_ref, rb_ref, o_ref):
    ids = ids_ref[0]
    for h in range(N_HEADS):
        acc = jnp.full(ids.shape, NEG_BIG, F32)
        for bkt in range(N_BUCKETS):
            acc = jnp.where(ids == bkt, rb_ref[bkt, h], acc)
        o_ref[h, 0] = acc


def attention_bias_table(bucket_ids, rel_bias):
    nvar, tq, w = bucket_ids.shape
    return pl.pallas_call(
        _attn_bias_body, grid=(nvar,),
        in_specs=[pl.BlockSpec((1, tq, w), lambda v: (v, 0, 0)),
                  pl.BlockSpec(memory_space=pltpu.SMEM)],
        out_specs=pl.BlockSpec((N_HEADS, 1, tq, w), lambda v: (0, v, 0, 0)),
        out_shape=jax.ShapeDtypeStruct((N_HEADS, nvar, tq, w), F32),
        compiler_params=_params("parallel"), name="attention_bias_table",
    )(bucket_ids, rel_bias)


A_SLABS = 3 * DG // V7X_LANES
A_QBLOCKS = SEQ // A_TQ


A_SUB4 = SEQ // 4
A_SUB16 = SEQ // 16


def _attn_body(at_ref, b1_ref, b4_ref, b16_ref, o_ref, qkv_s, x4_s, x16_s, y16_s, y4_s, part_o, part_l):
    def fill(c, _):
        r0 = pl.multiple_of(c * A_ROWS, A_ROWS)
        for s in range(A_SLABS):
            qkv_s[s, pl.ds(r0, A_ROWS), :] = at_ref[0, pl.ds(r0, A_ROWS), s * V7X_LANES:(s + 1) * V7X_LANES].astype(F32)
        return 0

    lax.fori_loop(0, SEQ // A_ROWS, fill, 0)

    def deinterleave(s, _):
        for r4 in range(4):
            for c in range(A_SUB4 // A_ROWS):
                x4_s[s, pl.ds(r4 * A_SUB4 + c * A_ROWS, A_ROWS), :] = \
                    qkv_s[s, pl.ds(r4 + 4 * c * A_ROWS, A_ROWS, stride=4), :]
        for r in range(16):
            x16_s[s, pl.ds(r * A_SUB16, A_SUB16), :] = \
                x4_s[s, pl.ds((r % 4) * A_SUB4 + r // 4, A_SUB16, stride=4), :].astype(BF16)
        return 0

    lax.fori_loop(0, A_SLABS, deinterleave, 0)
    first_head = lax.broadcasted_iota(I32, (A_TQ, V7X_LANES), 1) < HD

    def run_pattern(pat, dil, bias_ref):
        n = SEQ // dil if dil < 16 else SEQ
        nblk = n // A_TQ
        w = A_KW

        def block(it, _):
            r = it // nblk
            i = it - r * nblk
            q0 = i * A_TQ
            k0 = jnp.clip(q0 - A_BAND, 0, n - w)
            var = jnp.where(i == 0, 0, jnp.where(i == nblk - 1, 2, 1))
            for hp in range(2):
                lanes = [slice((2 * part + hp) * V7X_LANES, (2 * part + hp + 1) * V7X_LANES) for part in range(3)]
                if dil == 4:
                    qrows = pl.ds(r + dil * q0, A_TQ, stride=dil)
                    krows = pl.ds(r + dil * k0, w, stride=dil)
                    q2 = qkv_s[hp, qrows, :]
                    k2 = qkv_s[2 + hp, krows, :].astype(BF16)
                    v2 = qkv_s[4 + hp, krows, :].astype(BF16)
                else:
                    qrows = pl.ds(pl.multiple_of(q0, A_TQ), A_TQ)
                    krows = pl.ds(pl.multiple_of(k0, A_BAND), w)
                    if dil == 1:
                        q2, k2, v2 = at_ref[0, qrows, lanes[0]], at_ref[0, krows, lanes[1]], at_ref[0, krows, lanes[2]]
                    else:
                        q2, k2, v2 = x16_s[hp, qrows, :], x16_s[2 + hp, krows, :], x16_s[4 + hp, krows, :]
                q2 = (q2 * (HD ** -0.5)).astype(BF16)
                outs, lses = [], []
                for hh in range(2):
                    keep = first_head if hh == 0 else jnp.logical_not(first_head)
                    qm = jnp.where(keep, q2, jnp.zeros_like(q2))
                    s = lax.dot_general(qm, k2, _NT, preferred_element_type=F32) + bias_ref[2 * hp + hh, var]
                    m = jnp.max(s, axis=1, keepdims=True)
                    p = jnp.exp(s - m)
                    den = jnp.sum(p, axis=1, keepdims=True)
                    outs.append(_dot(p.astype(BF16), v2) / den)
                    lses.append(m + jnp.log(den))
                o_new = jnp.where(first_head, outs[0], outs[1])
                l_new = jnp.where(first_head, lses[0], lses[1])
                if dil == 16:
                    y16_s[0, hp, qrows, :] = o_new
                    y16_s[1, hp, qrows, :] = l_new
                else:
                    part_o[pat, hp, qrows, :] = o_new
                    part_l[pat, hp, qrows, :] = l_new
            return 0

        lax.fori_loop(0, A_QBLOCKS, block, 0, unroll=4)

    for pat, (dil, bias_ref) in enumerate(zip(A_DILS, (b1_ref, b4_ref, b16_ref))):
        run_pattern(pat, dil, bias_ref)

    for a, dst in enumerate((part_o, part_l)):
        for hp in range(2):
            for r in range(16):
                y4_s[a, hp, pl.ds((r % 4) * A_SUB4 + r // 4, A_SUB16, stride=4), :] = \
                    y16_s[a, hp, pl.ds(r * A_SUB16, A_SUB16), :]
            for r4 in range(4):
                for c in range(A_SUB4 // A_ROWS):
                    dst[2, hp, pl.ds(r4 + 4 * c * A_ROWS, A_ROWS, stride=4), :] = \
                        y4_s[a, hp, pl.ds(r4 * A_SUB4 + c * A_ROWS, A_ROWS), :]

    def finish(c, _):
        rows = pl.ds(pl.multiple_of(c * A_TQ, A_TQ), A_TQ)
        for hp in range(2):
            ls = [part_l[pat, hp, rows, :] for pat in range(len(A_DILS))]
            mx = jnp.maximum(jnp.maximum(ls[0], ls[1]), ls[2])
            ws = [jnp.exp(l - mx) for l in ls]
            num = ws[0] * part_o[0, hp, rows, :] + ws[1] * part_o[1, hp, rows, :] + ws[2] * part_o[2, hp, rows, :]
            o_ref[0, rows, hp * V7X_LANES:(hp + 1) * V7X_LANES] = (num / (ws[0] + ws[1] + ws[2])).astype(BF16)
        return 0

    lax.fori_loop(0, SEQ // A_TQ, finish, 0)


def dilated_attention(at3, bias1, bias4, bias16):
    b = at3.shape[0]
    full = lambda a: pl.BlockSpec(a.shape, lambda i: (0,) * a.ndim)
    return pl.pallas_call(
        _attn_body, grid=(b,),
        in_specs=[pl.BlockSpec((1, SEQ, 3 * DG), lambda i: (i, 0, 0)), full(bias1), full(bias4), full(bias16)],
        out_specs=pl.BlockSpec((1, SEQ, DG), lambda i: (i, 0, 0)),
        out_shape=jax.ShapeDtypeStruct((b, SEQ, DG), BF16),
        scratch_shapes=[pltpu.VMEM((A_SLABS, SEQ, V7X_LANES), F32),
                        pltpu.VMEM((A_SLABS, SEQ, V7X_LANES), F32),
                        pltpu.VMEM((A_SLABS, SEQ, V7X_LANES), BF16),
                        pltpu.VMEM((2, 2, SEQ, V7X_LANES), F32),
                        pltpu.VMEM((2, 2, SEQ, V7X_LANES), F32),
                        pltpu.VMEM((len(A_DILS), 2, SEQ, V7X_LANES), F32),
                        pltpu.VMEM((len(A_DILS), 2, SEQ, V7X_LANES), F32)],
        compiler_params=_params("parallel"), name="dilated_attention",
    )(at3, bias1, bias4, bias16)


H_BLK = 256
H_CPB = H_BLK // H_CHUNK
N_HBLK = SEQ // H_BLK
N_HCH = SEQ // H_CHUNK


def _chunk_bcast(x, row):
    c = x.shape[1]
    x3 = x.reshape(H_CPB, H_CHUNK, c)
    return jnp.broadcast_to(x3[:, row:row + 1, :], (H_CPB, H_CHUNK, c)).reshape(H_BLK, c)


def _hgrn_body(p_ref, lb_ref, nw_ref, tin_ref, o_ref, qm_s, ut_s, oi_s, dec_s, oe_s, st_s):
    li = lax.broadcasted_iota(I32, (H_BLK, H_BLK), 0)
    si = lax.broadcasted_iota(I32, (H_BLK, H_BLK), 1)
    same = (li // H_CHUNK) == (si // H_CHUNK)
    mask_f = same & (si <= li)
    mask_b = same & (si >= li)
    lane_head = lax.broadcasted_iota(I32, (1, DG), 1) // HD

    def block(bi, _):
        r0 = pl.multiple_of(bi * H_BLK, H_BLK)
        rows = pl.ds(r0, H_BLK)
        q = _silu(p_ref[0, rows, 0:DG].astype(F32))
        v = p_ref[0, rows, 3 * DG:4 * DG]
        scores = [None] * N_HEADS
        for d in range(2):
            fpre = p_ref[0, rows, (1 + d) * DG:(2 + d) * DG].astype(F32)
            lb = lb_ref[d:d + 1, :]
            sg = jax.nn.sigmoid(fpre)
            g = jnp.log(lb + (1.0 - lb) * sg)
            k = (1.0 - lb) * (1.0 - sg)
            gi = _dot01_2(tin_ref[...], g)
            glast = _chunk_bcast(gi, H_CHUNK - 1)
            if d == 0:
                gc = gi
                gref = _chunk_bcast(gi, H_CHUNK // 2 - 1)
                msk = mask_f
            else:
                gc = glast - gi + g
                gref = _chunk_bcast(gc, H_CHUNK // 2)
                msk = mask_b
            qe = (q * jnp.exp(gc - gref)).astype(BF16)
            ke = (k * jnp.exp(gref - gc)).astype(BF16)
            for h in range(N_HEADS):
                hs = slice(h * HD, (h + 1) * HD)
                sc = jnp.where(msk, lax.dot_general(qe[:, hs], ke[:, hs], _NT, preferred_element_type=F32), 0.0)
                scores[h] = sc if d == 0 else scores[h] + sc
            qd = q * jnp.exp(gc)
            kd = (k * jnp.exp(glast - gc)).astype(BF16)
            for j in range(H_CPB):
                c = bi * H_CPB + j
                cr = slice(j * H_CHUNK, (j + 1) * H_CHUNK)
                qm_s[d, c] = jnp.concatenate([jnp.where(lane_head == h, qd[cr, :], 0.0) for h in range(N_HEADS)],
                                             axis=0).astype(BF16)
                ut = lax.dot_general(v[cr, :], kd[cr, :], _TN, preferred_element_type=F32)
                packed = ut[0:HD, :]
                for h in range(1, N_HEADS):
                    packed = jnp.where(lane_head == h, ut[h * HD:(h + 1) * HD, :], packed)
                ut_s[d, c] = packed.astype(BF16)
                dec_s[d, c] = jnp.broadcast_to(jnp.exp(glast[j * H_CHUNK:j * H_CHUNK + 1, :]), (V7X_SUBLANES, DG))
        for h in range(N_HEADS):
            oi_s[h, rows, :] = _dot(scores[h].astype(BF16), v[:, h * HD:(h + 1) * HD])
        return 0

    lax.fori_loop(0, N_HBLK, block, 0)

    st_s[...] = jnp.zeros(st_s.shape, F32)

    def step(i, _):
        for d in range(2):
            c = i if d == 0 else N_HCH - 1 - i
            rows = pl.ds(pl.multiple_of(c * H_CHUNK, H_CHUNK), H_CHUNK)
            st = st_s[d]
            inter = lax.dot_general(qm_s[d, c], st.astype(BF16), _NT, preferred_element_type=F32)
            for h in range(N_HEADS):
                oe_s[d, h, rows, :] = inter[h * H_CHUNK:(h + 1) * H_CHUNK, :]
            st_s[d] = st * dec_s[d, c][0:1, :] + ut_s[d, c].astype(F32)
        return 0

    lax.fori_loop(0, N_HCH, step, 0, unroll=8)

    def finish(c, _):
        r0 = pl.multiple_of(c * CONV_ROWS, CONV_ROWS)
        rows = pl.ds(r0, CONV_ROWS)
        gate = _silu(p_ref[0, rows, 4 * DG:5 * DG].astype(F32))
        outs = [_rms(oi_s[h, rows, :] + oe_s[0, h, rows, :] + oe_s[1, h, rows, :]) for h in range(N_HEADS)]
        o_ref[0, rows, :] = (jnp.concatenate(outs, axis=1) * nw_ref[...] * gate).astype(BF16)
        return 0

    lax.fori_loop(0, SEQ // CONV_ROWS, finish, 0, unroll=2)


def hgrn2(p3, lb2, norm_w_lanes, tri_in_chunk):
    b = p3.shape[0]
    full = lambda a: pl.BlockSpec(a.shape, lambda i: (0,) * a.ndim)
    return pl.pallas_call(
        _hgrn_body, grid=(b,),
        in_specs=[pl.BlockSpec((1, SEQ, 5 * DG), lambda i: (i, 0, 0)), full(lb2), full(norm_w_lanes),
                  full(tri_in_chunk)],
        out_specs=pl.BlockSpec((1, SEQ, DG), lambda i: (i, 0, 0)),
        out_shape=jax.ShapeDtypeStruct((b, SEQ, DG), BF16),
        scratch_shapes=[pltpu.VMEM((2, N_HCH, N_HEADS * H_CHUNK, DG), BF16),
                        pltpu.VMEM((2, N_HCH, HD, DG), BF16),
                        pltpu.VMEM((N_HEADS, SEQ, HD), F32),
                        pltpu.VMEM((2, N_HCH, V7X_SUBLANES, DG), F32),
                        pltpu.VMEM((2, N_HEADS, SEQ, HD), F32),
                        pltpu.VMEM((2, HD, DG), F32)],
        compiler_params=_params("parallel"), name="hgrn2",
    )(p3, lb2, norm_w_lanes, tri_in_chunk)


@functools.lru_cache(maxsize=None)
def _tables():
    t = {}
    k = np.arange(SEQ, dtype=np.int64)
    ang = 2.0 * np.pi * ((k[:, None] * k[None, :]) % NFFT).astype(np.float64) / NFFT
    t["cos"] = np.cos(ang).astype(np.float32)
    t["sin"] = np.sin(ang).astype(np.float32)
    t["dft_rows"] = np.concatenate([t["cos"], t["sin"]], axis=0).astype(ml_dtypes.bfloat16)
    t["dft_cols"] = np.concatenate([t["cos"], t["sin"]], axis=1).astype(ml_dtypes.bfloat16)
    tt = np.linspace(0.0, 1.0, SEQ, dtype=np.float32)[:, None]
    bands = (HY_POS_DIM - 1) // 2
    ang_pos = (2.0 * math.pi * np.arange(SEQ, dtype=np.float32) / SEQ).astype(np.float32)
    f = np.linspace(1e-4, bands - 1, bands, dtype=np.float32)
    a2 = (ang_pos[:, None] * f[None, :]).astype(np.float32)
    z = np.concatenate([tt, np.cos(a2), -np.sin(a2)], axis=-1).astype(np.float32)
    zp = np.zeros((SEQ, V7X_LANES), np.float32)
    zp[:, :HY_POS_DIM] = z
    t["zpos"] = zp
    max_decay = math.log(1e-2) / 0.3
    min_decay = math.log(1e-2) / 1.5
    deltas = np.abs(np.linspace(min_decay, max_decay, DG, dtype=np.float32))
    t["decay"] = np.exp(-tt * deltas[None, :]).astype(np.float32)
    i128 = np.arange(V7X_LANES)
    t["u128"] = (i128[:, None] < i128[None, :]).astype(np.float32)
    im = np.arange(M_CHUNK)
    t["tri_incl"] = (im[:, None] <= im[None, :]).astype(np.float32)
    ib = np.arange(H_BLK)
    t["tri_in_chunk"] = ((ib[:, None] // H_CHUNK == ib[None, :] // H_CHUNK)
                         & (ib[None, :] <= ib[:, None])).astype(np.float32)
    idg = np.arange(DG)
    t["bdmask"] = (idg[:, None] // HD == idg[None, :] // HD).astype(np.float32)
    def bucket(rel):
        nb = N_BUCKETS // 2
        max_exact = nb // 2
        ret = (rel > 0).astype(np.int64) * nb
        n = np.abs(rel)
        nf = np.maximum(n, 1).astype(np.float64)
        large = max_exact + (np.log(nf / max_exact) / math.log(MAX_DISTANCE / max_exact)
                             * (nb - max_exact)).astype(np.int64)
        large = np.minimum(large, nb - 1)
        return ret + np.where(n < max_exact, n, large)

    for dil in A_DILS:
        n = SEQ // dil
        qi = np.arange(A_TQ)[:, None]
        kj = np.arange(A_KW)[None, :]
        ids = []
        for s0 in (0, -A_BAND, -(A_KW - A_TQ)):
            kk = kj + s0
            rel = kk - qi
            ok = np.abs(rel) <= A_BAND
            if n == A_TQ:
                ok &= (kk >= 0) & (kk < A_TQ)
            ids.append(np.where(ok, bucket(rel * dil), -1))
        t[f"bucket{dil}"] = np.stack(ids).astype(np.int32)
    return t


def kernel(x, w_in, w_out, norm_mix_w, norm_ffn_w, hy_conv_w, hy_pos_w1, hy_pos_b1, hy_pos_w2, hy_pos_b2,
           hy_sin_freq, hy_pos_w3, hy_filt_bias, m_conv_w, m_conv_b, m_dt_bias, m_A_log, m_D, m_norm_w, rel_bias,
           hg_lb, hg_norm_w, router_w, moe_w_gate, moe_w_up, moe_w_down, final_norm_w):
    b = x.shape[0]
    assert x.shape[1:] == (SEQ, D_MODEL) and b % MOE_FFN_SEQS == 0 and b % ROUTER_SEQS == 0, x.shape
    t = b * SEQ
    tb = _tables()
    cos_f32 = jnp.asarray(tb["cos"])
    sin_f32 = jnp.asarray(tb["sin"])
    dft_rows = jnp.asarray(tb["dft_rows"])
    dft_cols = jnp.asarray(tb["dft_cols"])
    u128 = jnp.asarray(tb["u128"]).astype(BF16)
    tri_incl = jnp.asarray(tb["tri_incl"]).astype(BF16)
    tri_in_chunk = jnp.asarray(tb["tri_in_chunk"]).astype(BF16)
    bdmask = jnp.asarray(tb["bdmask"])
    attn_bias = [attention_bias_table(jnp.asarray(tb[f"bucket{d}"]), rel_bias.astype(F32)) for d in A_DILS]

    sm = jax.nn.softmax(hg_lb.astype(F32), axis=0)
    lower_bounds = jnp.cumsum(sm, axis=0) - sm[:1]

    xa = x.reshape(t, D_MODEL)
    for l in range(DEPTH):
        wl = w_in[l]
        w_main = jnp.concatenate([wl[:, 0:768], wl[:, 768:1024], wl[:, 1024:1536], wl[:, 1544:2312],
                                  wl[:, 2312:3592]], axis=1).astype(BF16)
        w_dt_rows = wl[:, 1536:1544].T.astype(BF16)
        hy, mz, mx, at, hg, dtc = in_projection(xa, norm_mix_w[l][None, :], w_main, w_dt_rows)

        w1p = jnp.zeros((V7X_LANES, HY_HID), F32).at[:HY_POS_DIM].set(hy_pos_w1[l])
        kr, ki, kny = hyena_filter_spectrum(
            jnp.asarray(tb["zpos"]), w1p, hy_pos_b1[l][None, :], hy_pos_w2[l], hy_pos_b2[l][None, :],
            hy_sin_freq[l][None, :], hy_pos_w3[l], jnp.asarray(tb["decay"]), cos_f32, sin_f32)
        z3, x03 = hyena_prep(hy.reshape(b, SEQ, 3 * DG), hy_conv_w[l])
        ya = hyena_conv(z3, x03, dft_rows, dft_cols, kr, ki, kny, hy_filt_bias[l][None, :]).reshape(t, DG)

        a_col = (-jnp.exp(m_A_log[l].astype(F32))).reshape(8, 1)
        yb = mamba2(mz.reshape(b, SEQ, DG), mx.reshape(b, SEQ, 2 * DG), dtc.reshape(b, N_MCH, 8, MQ),
                    m_conv_w[l], m_conv_b[l][None, :], m_dt_bias[l].reshape(8, 1), a_col,
                    jnp.repeat(m_D[l].astype(F32), HD)[None, :], m_norm_w[l][None, :], tri_incl, bdmask).reshape(t, DG)

        yc = dilated_attention(at.reshape(b, SEQ, 3 * DG), *attn_bias).reshape(t, DG)

        lbl = lower_bounds[l]
        yd = hgrn2(hg.reshape(b, SEQ, 5 * DG), lbl, jnp.tile(hg_norm_w[l], N_HEADS)[None, :],
                   tri_in_chunk).reshape(t, DG)

        rw_rows = router_w[l].T.astype(F32)
        rw_hi = rw_rows.astype(BF16)
        rw_lo = (rw_rows - rw_hi.astype(F32)).astype(BF16)
        xo, xn, logits = out_projection(xa, ya, yb, yc, yd, w_out[l].reshape(4, DG, D_MODEL).astype(BF16),
                                        norm_ffn_w[l][None, :], rw_hi, rw_lo)
        xn3 = xn.reshape(b, SEQ, D_MODEL)
        rank, gate, seg = router(logits, u128)
        seg_flat = seg[:, :, :MOE_SEG_STRIDE].reshape(-1)
        xe = moe_gather(seg_flat, xn3, rank)
        ye = moe_experts(xe, moe_w_gate, moe_w_up, moe_w_down, l)
        xa = moe_scatter(seg_flat, ye, rank, gate, xo.reshape(b, SEQ, D_MODEL), final_norm_w[None, :],
                         final=(l == DEPTH - 1)).reshape(t, D_MODEL)
    return xa.reshape(b, SEQ, D_MODEL)
```

```python
import functools
import math

import ml_dtypes
import numpy as np
import jax
import jax.numpy as jnp
from jax import lax
from jax.experimental import pallas as pl
from jax.experimental.pallas import tpu as pltpu

F32 = jnp.float32
BF16 = jnp.bfloat16
I32 = jnp.int32

D_MODEL = 1024
SEQ = 2048
DEPTH = 2
DG = 256
N_HEADS = 4
HD = 64
HY_POS_DIM = 33
HY_HID = 64
M_CONV = 5
M_STATE = 64
M_CHUNK = 128
H_CHUNK = 32
A_BAND = 64
A_DILS = (1, 4, 16)
N_BUCKETS = 32
MAX_DISTANCE = 1024
N_EXPERTS = 16
CAP = 2 * SEQ // N_EXPERTS
D_FF = 1024
EPS = 1e-6
NFFT = 2 * SEQ

V7X_LANES = 128
V7X_SUBLANES = 8
V7X_VMEM_LIMIT_BYTES = 56 * 1024 * 1024

NEG_BIG = -1e30

_NT = (((1,), (1,)), ((), ()))
_TN = (((0,), (0,)), ((), ()))


def _params(*sem):
    return pltpu.CompilerParams(dimension_semantics=sem, vmem_limit_bytes=V7X_VMEM_LIMIT_BYTES)


def _dot(a, b):
    return jnp.dot(a, b, preferred_element_type=F32)


def _dot_hi(a, b):
    return jnp.dot(a, b, preferred_element_type=F32, precision=lax.Precision.HIGHEST)


def _dot01_2(t_bf16, x):
    x1 = x.astype(BF16)
    x2 = (x - x1.astype(F32)).astype(BF16)
    return _dot(t_bf16, x1) + _dot(t_bf16, x2)


def _dot01_rhs(x, t_bf16):
    x1 = x.astype(BF16)
    r1 = x - x1.astype(F32)
    x2 = r1.astype(BF16)
    x3 = (r1 - x2.astype(F32)).astype(BF16)
    return _dot(x1, t_bf16) + _dot(x2, t_bf16) + _dot(x3, t_bf16)


def _silu(x):
    return x * jax.nn.sigmoid(x)


def _softplus(x):
    return jnp.maximum(x, 0.0) + jnp.log(1.0 + jnp.exp(-jnp.abs(x)))


def _rms(x):
    return x * lax.rsqrt(jnp.mean(x * x, axis=-1, keepdims=True) + EPS)


TM_PROJ = 1024
_HY0, _MZ0, _MX0, _AT0, _HG0, _PEND = 0, 768, 1024, 1536, 2304, 3584


TILES_PER_SEQ = SEQ // TM_PROJ


def _inproj_body(x_ref, xp_ref, xq_ref, nw_ref, w_ref, wdt_ref, hcw_ref, z_ref, x0_ref, mz_ref, mx_ref, at_ref, hg_ref,
                 dtc_ref):
    norm = lambda v: (_rms(v) * nw_ref[...]).astype(BF16)
    hn = norm(x_ref[...])
    w_hy = w_ref[:, _HY0:_MZ0]
    p = _dot(hn, w_hy)
    tile = pl.program_id(0) % TILES_PER_SEQ
    before = _dot(norm(xp_ref[...]), w_hy)[V7X_SUBLANES - 1:V7X_SUBLANES, :]
    after = _dot(norm(xq_ref[...]), w_hy)[0:1, :]
    before = jnp.where(tile == 0, 0.0, before)
    after = jnp.where(tile == TILES_PER_SEQ - 1, 0.0, after)
    row = lax.broadcasted_iota(I32, (TM_PROJ, 3 * DG), 0)
    prev = jnp.where(row == 0, before, pltpu.roll(p, 1, 0))
    nxt = jnp.where(row == TM_PROJ - 1, after, pltpu.roll(p, TM_PROJ - 1, 0))
    u = prev * hcw_ref[0:1, :] + p * hcw_ref[1:2, :] + nxt * hcw_ref[2:3, :]
    x0_ref[...] = u[:, 0:DG].astype(BF16)
    z_ref[...] = (u[:, 2 * DG:3 * DG] * u[:, DG:2 * DG]).astype(BF16)
    mz_ref[...] = _dot(hn, w_ref[:, _MZ0:_MX0]).astype(BF16)
    mx_ref[...] = _dot(hn, w_ref[:, _MX0:_AT0]).astype(BF16)
    at_ref[...] = _dot(hn, w_ref[:, _AT0:_HG0]).astype(BF16)
    hg_ref[...] = _dot(hn, w_ref[:, _HG0:_PEND]).astype(BF16)
    dt_rows = lax.dot_general(wdt_ref[...], hn, _NT, preferred_element_type=F32)
    for j in range(TM_PROJ // M_CHUNK):
        dtc_ref[j] = dt_rows[:, j * M_CHUNK:(j + 1) * M_CHUNK]


def in_projection(x, norm_w, w_main, w_dt_rows, hy_conv_w):
    t = x.shape[0]
    tm = TM_PROJ
    halo = tm // V7X_SUBLANES
    row = lambda w: pl.BlockSpec((tm, w), lambda i: (i, 0))
    full = lambda a: pl.BlockSpec(a.shape, lambda i: (0,) * a.ndim)
    in_specs = [row(D_MODEL),
                pl.BlockSpec((V7X_SUBLANES, D_MODEL), lambda i: (jnp.maximum(i * halo - 1, 0), 0)),
                pl.BlockSpec((V7X_SUBLANES, D_MODEL), lambda i: (jnp.minimum((i + 1) * halo, t // V7X_SUBLANES - 1), 0)),
                full(norm_w), full(w_main), full(w_dt_rows), full(hy_conv_w)]
    widths = (256, 256, 256, 512, 768, 1280)
    out_shape = [jax.ShapeDtypeStruct((t, w), BF16) for w in widths]
    out_shape.append(jax.ShapeDtypeStruct((t // M_CHUNK, 8, M_CHUNK), F32))
    out_specs = [row(w) for w in widths] + [pl.BlockSpec((tm // M_CHUNK, 8, M_CHUNK), lambda i: (i, 0, 0))]
    return pl.pallas_call(
        _inproj_body, grid=(t // tm,), in_specs=in_specs, out_specs=out_specs, out_shape=out_shape,
        compiler_params=_params("parallel"), name="in_projection",
    )(x, x, x, norm_w, w_main, w_dt_rows, hy_conv_w)


TM_OUT = 1024


def _outproj_body(x_ref, ya_ref, yb_ref, yc_ref, yd_ref, w_ref, nw_ref, rwh_ref, rwl_ref, xo_ref, xn_ref, lg_ref):
    x = x_ref[...]
    acc = x + _dot(ya_ref[...], w_ref[0]) + _dot(yb_ref[...], w_ref[1])
    acc = acc + _dot(yc_ref[...], w_ref[2]) + _dot(yd_ref[...], w_ref[3])
    xo_ref[...] = acc
    xn = _rms(acc) * nw_ref[...]
    xh = xn.astype(BF16)
    xn_ref[...] = xh
    xl = (xn - xh.astype(F32)).astype(BF16)
    nt = lambda w, a: lax.dot_general(w, a, _NT, preferred_element_type=F32)
    lg_ref[...] = nt(rwh_ref[...], xh) + nt(rwh_ref[...], xl) + nt(rwl_ref[...], xh)


def out_projection(x, ya, yb, yc, yd, w_out4, norm_w, rw_hi, rw_lo):
    t = x.shape[0]
    tm = TM_OUT
    row = lambda w: pl.BlockSpec((tm, w), lambda i: (i, 0))
    full = lambda a: pl.BlockSpec(a.shape, lambda i: (0,) * a.ndim)
    in_specs = [row(D_MODEL)] + [row(DG)] * 4 + [full(w_out4), full(norm_w), full(rw_hi), full(rw_lo)]
    return pl.pallas_call(
        _outproj_body, grid=(t // tm,), in_specs=in_specs,
        out_specs=[row(D_MODEL), row(D_MODEL), pl.BlockSpec((N_EXPERTS, tm), lambda i: (0, i))],
        out_shape=[jax.ShapeDtypeStruct((t, D_MODEL), F32), jax.ShapeDtypeStruct((t, D_MODEL), BF16),
                   jax.ShapeDtypeStruct((N_EXPERTS, t), F32)],
        compiler_params=_params("parallel"), name="out_projection",
    )(x, ya, yb, yc, yd, w_out4, norm_w, rw_hi, rw_lo)


def _prefix_excl_lanes(mask_f32, u_ref):
    e = mask_f32.shape[0]
    off = jnp.zeros((e, 1), F32)
    parts, bounds = [], [off]
    for k in range(SEQ // V7X_LANES):
        tile = mask_f32[:, k * V7X_LANES:(k + 1) * V7X_LANES]
        parts.append(_dot(tile.astype(BF16), u_ref[...]) + off)
        off = off + jnp.sum(tile, axis=1, keepdims=True)
        bounds.append(off)
    return jnp.concatenate(parts, axis=1), bounds


ROUTER_SEQS = 2


def _router_body(lg_ref, u_ref, rank_ref, gate_ref, seg_ref):
    affs = []
    for q in range(ROUTER_SEQS):
        logits = lg_ref[:, q * SEQ:(q + 1) * SEQ]
        ex = jnp.exp(logits - jnp.max(logits, axis=0, keepdims=True))
        affs.append(ex / jnp.sum(ex, axis=0, keepdims=True))
    aff = jnp.concatenate(affs, axis=0)
    nrow = ROUTER_SEQS * N_EXPERTS
    bits = pltpu.bitcast(aff, I32)

    def search(i, thr):
        cand = thr | jnp.left_shift(jnp.int32(1), 30 - i)
        cnt = jnp.sum((bits >= cand).astype(I32), axis=1, keepdims=True)
        return jnp.where(cnt >= CAP, cand, thr)

    thr = lax.fori_loop(0, 31, search, jnp.zeros((nrow, 1), I32))
    gt = (bits > thr).astype(F32)
    eq = (bits == thr).astype(F32)
    need = CAP - jnp.sum(gt, axis=1, keepdims=True)
    tie_rank, _ = _prefix_excl_lanes(eq, u_ref)
    sel = gt + eq * (tie_rank < need).astype(F32)
    rank, bounds = _prefix_excl_lanes(sel, u_ref)
    rank = jnp.where(sel > 0.0, rank, -1.0)
    lane = lax.broadcasted_iota(I32, (nrow, V7X_LANES), 1)
    seg = jnp.zeros((nrow, V7X_LANES), F32)
    tiles = MOE_SEG // V7X_LANES
    for sgm in range(N_MOE_SEG):
        first = jnp.floor(bounds[sgm * tiles] * (1.0 / MOE_ALIGN)) * MOE_ALIGN
        need = jnp.floor((bounds[(sgm + 1) * tiles] - first + (MOE_TILE - 1)) * (1.0 / MOE_TILE))
        need = jnp.max(need.reshape(ROUTER_SEQS, N_EXPERTS, 1), axis=1, keepdims=True)
        need = jnp.broadcast_to(need, (ROUTER_SEQS, N_EXPERTS, 1)).reshape(nrow, 1)
        seg = jnp.where(lane == sgm, first, seg)
        seg = jnp.where(lane == N_MOE_SEG + sgm, need, seg)
    seg = seg.astype(I32)
    for q in range(ROUTER_SEQS):
        rows = slice(q * N_EXPERTS, (q + 1) * N_EXPERTS)
        rank_ref[q] = rank[rows]
        gate_ref[q] = aff[rows]
        seg_ref[q] = seg[rows]


def router(logits_et, u128):
    b = logits_et.shape[1] // SEQ
    out = jax.ShapeDtypeStruct((b, N_EXPERTS, SEQ), F32)
    return pl.pallas_call(
        _router_body, grid=(b // ROUTER_SEQS,),
        in_specs=[pl.BlockSpec((N_EXPERTS, ROUTER_SEQS * SEQ), lambda i: (0, i)),
                  pl.BlockSpec(u128.shape, lambda i: (0, 0))],
        out_specs=[pl.BlockSpec((ROUTER_SEQS, N_EXPERTS, SEQ), lambda i: (i, 0, 0))] * 2
                  + [pl.BlockSpec((ROUTER_SEQS, N_EXPERTS, V7X_LANES), lambda i: (i, 0, 0))],
        out_shape=[out, out, jax.ShapeDtypeStruct((b, N_EXPERTS, V7X_LANES), I32)],
        compiler_params=_params("parallel"), name="router",
    )(logits_et, u128)


MOE_SEG = 256
N_MOE_SEG = SEQ // MOE_SEG
MOE_TILE = 64
MOE_ALIGN = 16
MOE_GROUP = 4
MOE_SEG_STRIDE = 16
MOE_FFN_SEQS = 4


def _moe_seg_plan(cs_ref, b, s):
    base = b * N_EXPERTS * MOE_SEG_STRIDE
    starts = [cs_ref[base + ex * MOE_SEG_STRIDE + s] for ex in range(N_EXPERTS)]
    return starts, cs_ref[base + N_MOE_SEG + s]


def _moe_tile_bases(starts, r):
    own = [st + r * MOE_TILE for st in starts]
    return [pl.multiple_of(jnp.minimum(o, CAP - MOE_TILE), MOE_ALIGN) for o in own], own


def _moe_onehot_group(rank_ref, gate_ref, s, bases, own, grp):
    lanes = pl.ds(pl.multiple_of(s * MOE_SEG, MOE_SEG), MOE_SEG)
    j = lax.broadcasted_iota(I32, (MOE_TILE, MOE_SEG), 0)
    rows = []
    for ex in grp:
        slot = bases[ex] + j
        hit = (rank_ref[0, ex:ex + 1, lanes] == slot.astype(F32)) & (slot >= own[ex])
        val = 1.0 if gate_ref is None else gate_ref[0, ex:ex + 1, lanes]
        rows.append(jnp.where(hit, val, 0.0).astype(BF16))
    return jnp.concatenate(rows, axis=0)


_MOE_GROUPS = [list(range(g * MOE_GROUP, (g + 1) * MOE_GROUP)) for g in range(N_EXPERTS // MOE_GROUP)]


def _moe_gather_body(cs_ref, xn_ref, rank_ref, xe_ref):
    b = pl.program_id(0)

    def zero(ex, _):
        xe_ref[0, ex] = jnp.zeros((CAP, D_MODEL), BF16)
        return 0

    lax.fori_loop(0, N_EXPERTS, zero, 0)

    def seg_gather(s, _):
        starts, rounds = _moe_seg_plan(cs_ref, b, s)
        xn_seg = xn_ref[0, pl.ds(pl.multiple_of(s * MOE_SEG, MOE_SEG), MOE_SEG), :]

        def one_round(r, _):
            bases, own = _moe_tile_bases(starts, r)
            for grp in _MOE_GROUPS:
                got = _dot(_moe_onehot_group(rank_ref, None, s, bases, own, grp), xn_seg)
                for k, ex in enumerate(grp):
                    rows = pl.ds(bases[ex], MOE_TILE)
                    xe_ref[0, ex, rows, :] += got[k * MOE_TILE:(k + 1) * MOE_TILE].astype(BF16)
            return 0

        lax.fori_loop(0, rounds, one_round, 0)
        return 0

    lax.fori_loop(0, N_MOE_SEG, seg_gather, 0)


def moe_gather(seg_counts_flat, xn3, rank3):
    b = xn3.shape[0]
    grid_spec = pltpu.PrefetchScalarGridSpec(
        num_scalar_prefetch=1, grid=(b,),
        in_specs=[pl.BlockSpec((1, SEQ, D_MODEL), lambda i, cs: (i, 0, 0)),
                  pl.BlockSpec((1, N_EXPERTS, SEQ), lambda i, cs: (i, 0, 0))],
        out_specs=pl.BlockSpec((1, N_EXPERTS, CAP, D_MODEL), lambda i, cs: (i, 0, 0, 0)))
    return pl.pallas_call(
        _moe_gather_body, grid_spec=grid_spec,
        out_shape=jax.ShapeDtypeStruct((b, N_EXPERTS, CAP, D_MODEL), BF16),
        compiler_params=_params("parallel"), name="moe_gather",
    )(seg_counts_flat, xn3, rank3)


def _moe_experts_body(xe_ref, wg_ref, wu_ref, wd_ref, ye_ref, wg_s, wu_s, wd_s):
    @pl.when(pl.program_id(1) == 0)
    def _():
        wg_s[...] = wg_ref[0, 0].astype(BF16)
        wu_s[...] = wu_ref[0, 0].astype(BF16)
        wd_s[...] = wd_ref[0, 0].astype(BF16)

    xe = xe_ref[...].reshape(MOE_FFN_SEQS * CAP, D_MODEL)
    hid = (_silu(_dot(xe, wg_s[...])) * _dot(xe, wu_s[...])).astype(BF16)
    ye_ref[...] = _dot(hid, wd_s[...]).astype(BF16).reshape(MOE_FFN_SEQS, 1, CAP, D_MODEL)


def moe_experts(xe4, w_gate, w_up, w_down, layer):
    b = xe4.shape[0]
    blk = pl.BlockSpec((MOE_FFN_SEQS, 1, CAP, D_MODEL), lambda e, g: (g, e, 0, 0))
    w_spec = lambda a: pl.BlockSpec((1, 1) + a.shape[2:], lambda e, g: (layer, e, 0, 0))
    return pl.pallas_call(
        _moe_experts_body, grid=(N_EXPERTS, b // MOE_FFN_SEQS),
        in_specs=[blk, w_spec(w_gate), w_spec(w_up), w_spec(w_down)],
        out_specs=blk, out_shape=jax.ShapeDtypeStruct(xe4.shape, BF16),
        scratch_shapes=[pltpu.VMEM((D_MODEL, D_FF), BF16), pltpu.VMEM((D_MODEL, D_FF), BF16),
                        pltpu.VMEM((D_FF, D_MODEL), BF16)],
        compiler_params=_params("parallel", "arbitrary"), name="moe_experts",
    )(xe4, w_gate, w_up, w_down)


MOE_SCATTER_SEGS = 4


def _moe_scatter_body(final, cs_ref, ye_ref, rank_ref, gate_ref, xo_ref, nw_ref, o_ref):
    b = pl.program_id(0)
    half = pl.program_id(1)

    def seg_scatter(k, _):
        s = half * MOE_SCATTER_SEGS + k
        starts, rounds = _moe_seg_plan(cs_ref, b, s)
        tok = pl.ds(pl.multiple_of(k * MOE_SEG, MOE_SEG), MOE_SEG)
        o_ref[0, tok, :] = xo_ref[0, tok, :]

        def one_round(r, _):
            bases, own = _moe_tile_bases(starts, r)
            for grp in _MOE_GROUPS:
                ye = jnp.concatenate([ye_ref[0, ex, pl.ds(bases[ex], MOE_TILE), :] for ex in grp], axis=0)
                o_ref[0, tok, :] += lax.dot_general(_moe_onehot_group(rank_ref, gate_ref, s, bases, own, grp), ye,
                                                    _TN, preferred_element_type=F32)
            return 0

        lax.fori_loop(0, rounds, one_round, 0)
        if final:
            o_ref[0, tok, :] = _rms(o_ref[0, tok, :]) * nw_ref[...]
        return 0

    lax.fori_loop(0, MOE_SCATTER_SEGS, seg_scatter, 0)


def moe_scatter(seg_counts_flat, ye4, rank3, gate3, xo3, final_norm_w, final):
    b = ye4.shape[0]
    rows = MOE_SCATTER_SEGS * MOE_SEG
    sel_spec = pl.BlockSpec((1, N_EXPERTS, SEQ), lambda i, j, cs: (i, 0, 0))
    tok_spec = pl.BlockSpec((1, rows, D_MODEL), lambda i, j, cs: (i, j, 0))
    grid_spec = pltpu.PrefetchScalarGridSpec(
        num_scalar_prefetch=1, grid=(b, SEQ // rows),
        in_specs=[pl.BlockSpec((1, N_EXPERTS, CAP, D_MODEL), lambda i, j, cs: (i, 0, 0, 0)), sel_spec, sel_spec,
                  tok_spec, pl.BlockSpec(final_norm_w.shape, lambda i, j, cs: (0, 0))],
        out_specs=tok_spec)
    return pl.pallas_call(
        functools.partial(_moe_scatter_body, final), grid_spec=grid_spec,
        out_shape=jax.ShapeDtypeStruct((b, SEQ, D_MODEL), F32),
        compiler_params=_params("parallel", "arbitrary"), name="moe_scatter",
    )(seg_counts_flat, ye4, rank3, gate3, xo3, final_norm_w)


HY_KB = 256
HY_ROWS = 256


def _split_bf16(x):
    hi = x.astype(BF16)
    return hi, (x - hi.astype(F32)).astype(BF16)


def _hy_filter_body(z_ref, w1_ref, b1_ref, w2_ref, b2_ref, fr_ref, w3_ref, dec_ref, ch_ref, cl_ref, sh_ref, sl_ref,
                    kr_ref, ki_ref, kny_ref, ah_s, al_s, dh_s, dl_s):
    @pl.when(pl.program_id(0) == 0)
    def _():
        def rows(c, kny):
            r0 = pl.multiple_of(c * HY_ROWS, HY_ROWS)
            fr = fr_ref[...]
            h = jnp.sin(fr * (_dot_hi(z_ref[pl.ds(r0, HY_ROWS), :], w1_ref[...]) + b1_ref[...]))
            h = jnp.sin(fr * (_dot_hi(h, w2_ref[...]) + b2_ref[...]))
            h = _dot_hi(h, w3_ref[...])
            dec = dec_ref[pl.ds(r0, HY_ROWS), :]
            pos = r0 + lax.broadcasted_iota(I32, (HY_ROWS, DG), 0)
            hf = h[:, :DG] * dec
            hb = jnp.where(pos == 0, 0.0, h[:, DG:] * dec)
            a = hf + hb
            ah_s[pl.ds(r0, HY_ROWS), :], al_s[pl.ds(r0, HY_ROWS), :] = _split_bf16(a)
            dh_s[pl.ds(r0, HY_ROWS), :], dl_s[pl.ds(r0, HY_ROWS), :] = _split_bf16(hf - hb)
            sgn = (1 - 2 * (pos & 1)).astype(F32)
            return kny + jnp.sum(a * sgn, axis=0, keepdims=True)

        kny = lax.fori_loop(0, SEQ // HY_ROWS, rows, jnp.zeros((1, DG), F32))
        kny_ref[...] = jnp.broadcast_to(kny, kny_ref.shape)

    kr_ref[...] = _dot(ch_ref[...], ah_s[...]) + _dot(ch_ref[...], al_s[...]) + _dot(cl_ref[...], ah_s[...])
    ki_ref[...] = _dot(sh_ref[...], dh_s[...]) + _dot(sh_ref[...], dl_s[...]) + _dot(sl_ref[...], dh_s[...])


def hyena_filter_spectrum(zpos, w1, b1, w2, b2, freq, w3, decay, dft_rows, dft_rows_lo):
    full = lambda a: pl.BlockSpec(a.shape, lambda k: (0,) * a.ndim)
    kblk = pl.BlockSpec((HY_KB, SEQ), lambda k: (k, 0))
    sblk = pl.BlockSpec((HY_KB, SEQ), lambda k: (SEQ // HY_KB + k, 0))
    oblk = pl.BlockSpec((HY_KB, DG), lambda k: (k, 0))
    return pl.pallas_call(
        _hy_filter_body, grid=(SEQ // HY_KB,),
        in_specs=[full(zpos), full(w1), full(b1), full(w2), full(b2), full(freq), full(w3), full(decay),
                  kblk, kblk, sblk, sblk],
        out_specs=[oblk, oblk, pl.BlockSpec((V7X_SUBLANES, DG), lambda k: (0, 0))],
        out_shape=[jax.ShapeDtypeStruct((SEQ, DG), F32), jax.ShapeDtypeStruct((SEQ, DG), F32),
                   jax.ShapeDtypeStruct((V7X_SUBLANES, DG), F32)],
        scratch_shapes=[pltpu.VMEM((SEQ, DG), BF16)] * 4,
        compiler_params=_params("arbitrary"), name="hyena_filter",
    )(zpos, w1, b1, w2, b2, freq, w3, decay, dft_rows, dft_rows_lo, dft_rows, dft_rows_lo)


CONV_ROWS = 128
CONV_HALO = 8


def _dwconv_rows(pad_ref, w_ref, r0, lanes, k):
    n = CONV_ROWS + 2 * CONV_HALO
    win = pad_ref[pl.ds(r0, n), lanes]
    acc = None
    for j in range(k):
        sh = (k // 2 - j) % n
        rolled = win if sh == 0 else pltpu.roll(win, sh, 0)
        term = rolled[CONV_HALO:CONV_HALO + CONV_ROWS] * w_ref[j:j + 1, lanes]
        acc = term if acc is None else acc + term
    return acc


def _fill_padded(pad_ref, src_ref, width):
    zeros = jnp.zeros((CONV_HALO, width), F32)
    pad_ref[pl.ds(0, CONV_HALO), :] = zeros
    pad_ref[pl.ds(SEQ + CONV_HALO, CONV_HALO), :] = zeros

    def fill(c, _):
        r0 = pl.multiple_of(c * CONV_ROWS, CONV_ROWS)
        pad_ref[pl.ds(r0 + CONV_HALO, CONV_ROWS), :] = src_ref[0, pl.ds(r0, CONV_ROWS), :].astype(F32)
        return 0

    lax.fori_loop(0, SEQ // CONV_ROWS, fill, 0)


HY_FB = 512


def _hy_conv_body(z_ref, x0_ref, t1_ref, t2_ref, kr_ref, ki_ref, kny_ref, fb_ref, o_ref, y_s):
    z = z_ref[0]

    def spectrum(kb, _):
        rows = pl.ds(pl.multiple_of(kb * HY_FB, HY_FB), HY_FB)
        rows_s = pl.ds(pl.multiple_of(SEQ + kb * HY_FB, HY_FB), HY_FB)
        zr = _dot(t1_ref[rows, :], z)
        zi = _dot(t1_ref[rows_s, :], z)
        krow = kb * HY_FB + lax.broadcasted_iota(I32, (HY_FB, 1), 0)
        wk = jnp.where(krow == 0, 1.0 / NFFT, 2.0 / NFFT)
        kr = kr_ref[rows, :]
        ki = ki_ref[rows, :]
        y_s[rows, :] = ((zr * kr - zi * ki) * wk).astype(BF16)
        y_s[rows_s, :] = ((zr * ki + zi * kr) * wk).astype(BF16)
        return 0

    lax.fori_loop(0, SEQ // HY_FB, spectrum, 0)

    zny = jnp.sum(z.astype(F32) * (1 - 2 * (lax.broadcasted_iota(I32, (SEQ, DG), 0) & 1)).astype(F32),
                  axis=0, keepdims=True)
    nyq = zny * kny_ref[0:1, :] * (1.0 / NFFT)

    def synth(tb, _):
        rows = pl.ds(pl.multiple_of(tb * HY_FB, HY_FB), HY_FB)
        conv = _dot(t2_ref[rows, :], y_s[...])
        sgn = (1 - 2 * (lax.broadcasted_iota(I32, (HY_FB, DG), 0) & 1)).astype(F32)
        zf = z_ref[0, rows, :].astype(F32)
        o_ref[0, rows, :] = (x0_ref[0, rows, :].astype(F32) * (conv + nyq * sgn + zf * fb_ref[...])).astype(BF16)
        return 0

    lax.fori_loop(0, SEQ // HY_FB, synth, 0)


def hyena_conv(z3, x03, dft_rows, dft_cols, kr, ki, kny, fbias):
    b = z3.shape[0]
    seq_blk = pl.BlockSpec((1, SEQ, DG), lambda i: (i, 0, 0))
    once = lambda a: pl.BlockSpec(a.shape, lambda i: (0,) * a.ndim, pipeline_mode=pl.Buffered(1))
    return pl.pallas_call(
        _hy_conv_body, grid=(b,),
        in_specs=[seq_blk, seq_blk, once(dft_rows), once(dft_cols), once(kr), once(ki), once(kny), once(fbias)],
        out_specs=seq_blk, out_shape=jax.ShapeDtypeStruct((b, SEQ, DG), BF16),
        scratch_shapes=[pltpu.VMEM((2 * SEQ, DG), BF16)],
        compiler_params=_params("parallel"), name="hyena_conv",
    )(z3, x03, dft_rows, dft_cols, kr, ki, kny, fbias)


N_MCH = SEQ // M_CHUNK
MQ = M_CHUNK


def _head_lane_vec(rows8, base):
    lane_head = lax.broadcasted_iota(I32, (1, DG), 1) // HD
    out = jnp.zeros((1, DG), F32)
    for h in range(N_HEADS):
        out = jnp.where(lane_head == h, rows8[base + h:base + h + 1, :], out)
    return out


def _mamba_body(z_ref, xbc_ref, dtc_ref, cw_ref, cb_ref, dtb_ref, a_ref, dsk_ref, nw_ref, tri_ref, bd_ref,
                o_ref, pad, xs_s, b_s, c_s, y_s, u_s, dec_s, cw_s, yo_s, st_s):
    _fill_padded(pad, xbc_ref, 2 * DG)

    def conv_rows(c, _):
        r0 = pl.multiple_of(c * CONV_ROWS, CONV_ROWS)
        for g in range(4):
            lanes = slice(g * V7X_LANES, (g + 1) * V7X_LANES)
            u = _silu(_dwconv_rows(pad, cw_ref, r0, lanes, M_CONV) + cb_ref[:, lanes])
            if g < 2:
                xs_s[pl.ds(r0, CONV_ROWS), lanes] = u
            elif g == 2:
                b_s[pl.ds(r0, CONV_ROWS), :] = u.astype(BF16)
            else:
                c_s[pl.ds(r0, CONV_ROWS), :] = u.astype(BF16)
        return 0

    lax.fori_loop(0, SEQ // CONV_ROWS, conv_rows, 0)

    li = lax.broadcasted_iota(I32, (MQ, MQ), 0)
    si = lax.broadcasted_iota(I32, (MQ, MQ), 1)
    lower = si <= li
    upper = si >= li
    upper_half = li >= M_STATE
    first_group = si < M_STATE
    bdmask = bd_ref[...]

    def chunk(c, _):
        r0 = pl.multiple_of(c * MQ, MQ)
        dt = _softplus(dtc_ref[0, c] + dtb_ref[...])
        a = dt * a_ref[...]
        cum = _dot01_rhs(a, tri_ref[...])
        tot = cum[:, MQ - 1:MQ]
        suf = tot - cum + a
        row_dir = lax.broadcasted_iota(I32, (8, MQ), 0) // N_HEADS
        seg = jnp.where(row_dir == 0, cum, suf)
        wgt = jnp.exp(tot - seg) * dt
        carry = jnp.exp(seg)
        x = xs_s[pl.ds(r0, MQ), :]
        xb = x.astype(BF16)
        bm = b_s[pl.ds(r0, MQ), :]
        cm = c_s[pl.ds(r0, MQ), :]
        cmf = cm.astype(F32)
        cswap = pltpu.roll(cmf, M_STATE, 1)
        c_dup = [jnp.where(first_group, cmf, cswap), jnp.where(first_group, cswap, cmf)]
        bt = bm.astype(F32).T
        ydiag = []
        for h in range(N_HEADS):
            g = h // 2
            cb = lax.dot_general(cm[:, g * M_STATE:(g + 1) * M_STATE], bm[:, g * M_STATE:(g + 1) * M_STATE],
                                 _NT, preferred_element_type=F32)
            sf = jnp.broadcast_to(seg[h:h + 1, :], (MQ, MQ))
            sb = jnp.broadcast_to(seg[4 + h:5 + h, :], (MQ, MQ))
            lf = jnp.where(lower, jnp.exp(jnp.minimum(sf.T - sf, 0.0)), 0.0)
            lb = jnp.where(upper, jnp.exp(jnp.minimum(sb.T - sb, 0.0)), 0.0)
            m = cb * (lf * dt[h:h + 1, :] + lb * dt[4 + h:5 + h, :])
            ydiag.append(_dot(m.astype(BF16), xb[:, h * HD:(h + 1) * HD]))
        y_s[pl.ds(r0, MQ), :] = jnp.concatenate(ydiag, axis=1)
        for d in range(2):
            bwt = jnp.concatenate([bt[(h // 2) * M_STATE:(h // 2 + 1) * M_STATE, :] * wgt[4 * d + h:4 * d + h + 1, :]
                                   for h in range(N_HEADS)], axis=0)
            u_s[d, c] = (_dot(bwt.astype(BF16), xb) * bdmask).astype(BF16)
            dec_s[d, c] = jnp.broadcast_to(_head_lane_vec(jnp.exp(tot), 4 * d), (V7X_SUBLANES, DG))
            tiles = []
            for g in range(2):
                wrows = jnp.where(upper_half, carry[4 * d + 2 * g + 1:4 * d + 2 * g + 2, :],
                                  carry[4 * d + 2 * g:4 * d + 2 * g + 1, :])
                tiles.append(c_dup[g] * wrows.T)
            cw_s[d, c] = jnp.concatenate(tiles, axis=1).astype(BF16)
        return 0

    lax.fori_loop(0, N_MCH, chunk, 0, unroll=4)

    st_s[...] = jnp.zeros(st_s.shape, F32)

    def scan(i, _):
        for d in range(2):
            c = i if d == 0 else N_MCH - 1 - i
            st = st_s[d]
            yo_s[d, pl.ds(pl.multiple_of(c * MQ, MQ), MQ), :] = _dot(cw_s[d, c], st.astype(BF16))
            st_s[d] = st * dec_s[d, c][0:1, :] + u_s[d, c].astype(F32)
        return 0

    lax.fori_loop(0, N_MCH, scan, 0, unroll=4)

    def finish(c, _):
        r0 = pl.multiple_of(c * CONV_ROWS, CONV_ROWS)
        rows = pl.ds(r0, CONV_ROWS)
        y = y_s[rows, :] + yo_s[0, rows, :] + yo_s[1, rows, :] + xs_s[rows, :] * dsk_ref[...]
        y = y * _silu(z_ref[0, rows, :].astype(F32))
        o_ref[0, pl.ds(r0, CONV_ROWS), :] = (_rms(y) * nw_ref[...]).astype(BF16)
        return 0

    lax.fori_loop(0, SEQ // CONV_ROWS, finish, 0, unroll=2)


def mamba2(z3, xbc3, dtc4, conv_w, conv_b, dt_bias_col, a_col, dskip_lanes, norm_w, tri_incl, bdmask):
    b = z3.shape[0]
    full = lambda a: pl.BlockSpec(a.shape, lambda i: (0,) * a.ndim)
    return pl.pallas_call(
        _mamba_body, grid=(b,),
        in_specs=[pl.BlockSpec((1, SEQ, DG), lambda i: (i, 0, 0)),
                  pl.BlockSpec((1, SEQ, 2 * DG), lambda i: (i, 0, 0)),
                  pl.BlockSpec((1, N_MCH, 8, MQ), lambda i: (i, 0, 0, 0)),
                  full(conv_w), full(conv_b), full(dt_bias_col), full(a_col), full(dskip_lanes), full(norm_w),
                  full(tri_incl), full(bdmask)],
        out_specs=pl.BlockSpec((1, SEQ, DG), lambda i: (i, 0, 0)),
        out_shape=jax.ShapeDtypeStruct((b, SEQ, DG), BF16),
        scratch_shapes=[pltpu.VMEM((SEQ + 2 * CONV_HALO, 2 * DG), F32),
                        pltpu.VMEM((SEQ, DG), F32),
                        pltpu.VMEM((SEQ, 2 * M_STATE), BF16),
                        pltpu.VMEM((SEQ, 2 * M_STATE), BF16),
                        pltpu.VMEM((SEQ, DG), F32),
                        pltpu.VMEM((2, N_MCH, DG, DG), BF16),
                        pltpu.VMEM((2, N_MCH, V7X_SUBLANES, DG), F32),
                        pltpu.VMEM((2, N_MCH, MQ, DG), BF16),
                        pltpu.VMEM((2, SEQ, DG), F32),
                        pltpu.VMEM((2, DG, DG), F32)],
        compiler_params=_params("parallel"), name="mamba2",
    )(z3, xbc3, dtc4, conv_w, conv_b, dt_bias_col, a_col, dskip_lanes, norm_w, tri_incl, bdmask)


A_TQ = 128
A_ROWS = 256
A_KW = A_TQ + 2 * A_BAND


def _attn_bias_body(ids_ref, rb_ref, o_ref):
    ids = ids_ref[0]
    for h in range(N_HEADS):
        acc = jnp.full(ids.shape, NEG_BIG, F32)
        for bkt in range(N_BUCKETS):
            acc = jnp.where(ids == bkt, rb_ref[bkt, h], acc)
        o_ref[h, 0] = acc


def attention_bias_table(bucket_ids, rel_bias):
    nvar, tq, w = bucket_ids.shape
    return pl.pallas_call(
        _attn_bias_body, grid=(nvar,),
        in_specs=[pl.BlockSpec((1, tq, w), lambda v: (v, 0, 0)),
                  pl.BlockSpec(memory_space=pltpu.SMEM)],
        out_specs=pl.BlockSpec((N_HEADS, 1, tq, w), lambda v: (0, v, 0, 0)),
        out_shape=jax.ShapeDtypeStruct((N_HEADS, nvar, tq, w), F32),
        compiler_params=_params("parallel"), name="attention_bias_table",
    )(bucket_ids, rel_bias)


A_SLABS = 3 * DG // V7X_LANES
A_QBLOCKS = SEQ // A_TQ


A_SUB4 = SEQ // 4
A_SUB16 = SEQ // 16


def _attn_body(at_ref, b1_ref, b4_ref, b16_ref, o_ref, qkv_s, x4_s, x16_s, y16_s, y4_s, part_o, part_l):
    def fill(c, _):
        r0 = pl.multiple_of(c * A_ROWS, A_ROWS)
        for s in range(A_SLABS):
            qkv_s[s, pl.ds(r0, A_ROWS), :] = at_ref[0, pl.ds(r0, A_ROWS), s * V7X_LANES:(s + 1) * V7X_LANES].astype(F32)
        return 0

    lax.fori_loop(0, SEQ // A_ROWS, fill, 0)

    def deinterleave(s, _):
        for r4 in range(4):
            for c in range(A_SUB4 // A_ROWS):
                x4_s[s, pl.ds(r4 * A_SUB4 + c * A_ROWS, A_ROWS), :] = \
                    qkv_s[s, pl.ds(r4 + 4 * c * A_ROWS, A_ROWS, stride=4), :]
        for r in range(16):
            x16_s[s, pl.ds(r * A_SUB16, A_SUB16), :] = \
                x4_s[s, pl.ds((r % 4) * A_SUB4 + r // 4, A_SUB16, stride=4), :].astype(BF16)
        return 0

    lax.fori_loop(0, A_SLABS, deinterleave, 0)
    first_head = lax.broadcasted_iota(I32, (A_TQ, V7X_LANES), 1) < HD

    def run_pattern(pat, dil, bias_ref):
        n = SEQ // dil if dil < 16 else SEQ
        nblk = n // A_TQ
        w = A_KW

        def block(it, _):
            r = it // nblk
            i = it - r * nblk
            q0 = i * A_TQ
            k0 = jnp.clip(q0 - A_BAND, 0, n - w)
            var = jnp.where(i == 0, 0, jnp.where(i == nblk - 1, 2, 1))
            for hp in range(2):
                lanes = [slice((2 * part + hp) * V7X_LANES, (2 * part + hp + 1) * V7X_LANES) for part in range(3)]
                if dil == 4:
                    qrows = pl.ds(r + dil * q0, A_TQ, stride=dil)
                    krows = pl.ds(r + dil * k0, w, stride=dil)
                    q2 = qkv_s[hp, qrows, :]
                    k2 = qkv_s[2 + hp, krows, :].astype(BF16)
                    v2 = qkv_s[4 + hp, krows, :].astype(BF16)
                else:
                    qrows = pl.ds(pl.multiple_of(q0, A_TQ), A_TQ)
                    krows = pl.ds(pl.multiple_of(k0, A_BAND), w)
                    if dil == 1:
                        q2, k2, v2 = at_ref[0, qrows, lanes[0]], at_ref[0, krows, lanes[1]], at_ref[0, krows, lanes[2]]
                    else:
                        q2, k2, v2 = x16_s[hp, qrows, :], x16_s[2 + hp, krows, :], x16_s[4 + hp, krows, :]
                q2 = (q2 * (HD ** -0.5)).astype(BF16)
                outs, lses = [], []
                for hh in range(2):
                    keep = first_head if hh == 0 else jnp.logical_not(first_head)
                    qm = jnp.where(keep, q2, jnp.zeros_like(q2))
                    s = lax.dot_general(qm, k2, _NT, preferred_element_type=F32) + bias_ref[2 * hp + hh, var]
                    m = jnp.max(s, axis=1, keepdims=True)
                    p = jnp.exp(s - m)
                    den = jnp.sum(p, axis=1, keepdims=True)
                    outs.append(_dot(p.astype(BF16), v2) / den)
                    lses.append(m + jnp.log(den))
                o_new = jnp.where(first_head, outs[0], outs[1])
                l_new = jnp.where(first_head, lses[0], lses[1])
                if dil == 16:
                    y16_s[0, hp, qrows, :] = o_new
                    y16_s[1, hp, qrows, :] = l_new
                else:
                    part_o[pat, hp, qrows, :] = o_new
                    part_l[pat, hp, qrows, :] = l_new
            return 0

        lax.fori_loop(0, A_QBLOCKS, block, 0, unroll=4)

    for pat, (dil, bias_ref) in enumerate(zip(A_DILS, (b1_ref, b4_ref, b16_ref))):
        run_pattern(pat, dil, bias_ref)

    for a, dst in enumerate((part_o, part_l)):
        for hp in range(2):
            for r in range(16):
                y4_s[a, hp, pl.ds((r % 4) * A_SUB4 + r // 4, A_SUB16, stride=4), :] = \
                    y16_s[a, hp, pl.ds(r * A_SUB16, A_SUB16), :]
            for r4 in range(4):
                for c in range(A_SUB4 // A_ROWS):
                    dst[2, hp, pl.ds(r4 + 4 * c * A_ROWS, A_ROWS, stride=4), :] = \
                        y4_s[a, hp, pl.ds(r4 * A_SUB4 + c * A_ROWS, A_ROWS), :]

    def finish(c, _):
        rows = pl.ds(pl.multiple_of(c * A_TQ, A_TQ), A_TQ)
        for hp in range(2):
            ls = [part_l[pat, hp, rows, :] for pat in range(len(A_DILS))]
            mx = jnp.maximum(jnp.maximum(ls[0], ls[1]), ls[2])
            ws = [jnp.exp(l - mx) for l in ls]
            num = ws[0] * part_o[0, hp, rows, :] + ws[1] * part_o[1, hp, rows, :] + ws[2] * part_o[2, hp, rows, :]
            o_ref[0, rows, hp * V7X_LANES:(hp + 1) * V7X_LANES] = (num / (ws[0] + ws[1] + ws[2])).astype(BF16)
        return 0

    lax.fori_loop(0, SEQ // A_TQ, finish, 0)


def dilated_attention(at3, bias1, bias4, bias16):
    b = at3.shape[0]
    full = lambda a: pl.BlockSpec(a.shape, lambda i: (0,) * a.ndim)
    return pl.pallas_call(
        _attn_body, grid=(b,),
        in_specs=[pl.BlockSpec((1, SEQ, 3 * DG), lambda i: (i, 0, 0)), full(bias1), full(bias4), full(bias16)],
        out_specs=pl.BlockSpec((1, SEQ, DG), lambda i: (i, 0, 0)),
        out_shape=jax.ShapeDtypeStruct((b, SEQ, DG), BF16),
        scratch_shapes=[pltpu.VMEM((A_SLABS, SEQ, V7X_LANES), F32),
                        pltpu.VMEM((A_SLABS, SEQ, V7X_LANES), F32),
                        pltpu.VMEM((A_SLABS, SEQ, V7X_LANES), BF16),
                        pltpu.VMEM((2, 2, SEQ, V7X_LANES), F32),
                        pltpu.VMEM((2, 2, SEQ, V7X_LANES), F32),
                        pltpu.VMEM((len(A_DILS), 2, SEQ, V7X_LANES), F32),
                        pltpu.VMEM((len(A_DILS), 2, SEQ, V7X_LANES), F32)],
        compiler_params=_params("parallel"), name="dilated_attention",
    )(at3, bias1, bias4, bias16)


H_BLK = 256
H_CPB = H_BLK // H_CHUNK
N_HBLK = SEQ // H_BLK
N_HCH = SEQ // H_CHUNK


def _chunk_bcast(x, row):
    c = x.shape[1]
    x3 = x.reshape(H_CPB, H_CHUNK, c)
    return jnp.broadcast_to(x3[:, row:row + 1, :], (H_CPB, H_CHUNK, c)).reshape(H_BLK, c)


def _hgrn_body(p_ref, lb_ref, nw_ref, tin_ref, o_ref, qm_s, ut_s, oi_s, dec_s, oe_s, st_s):
    li = lax.broadcasted_iota(I32, (H_BLK, H_BLK), 0)
    si = lax.broadcasted_iota(I32, (H_BLK, H_BLK), 1)
    same = (li // H_CHUNK) == (si // H_CHUNK)
    mask_f = same & (si <= li)
    mask_b = same & (si >= li)
    lane_head = lax.broadcasted_iota(I32, (1, DG), 1) // HD

    def block(bi, _):
        r0 = pl.multiple_of(bi * H_BLK, H_BLK)
        rows = pl.ds(r0, H_BLK)
        q = _silu(p_ref[0, rows, 0:DG].astype(F32))
        v = p_ref[0, rows, 3 * DG:4 * DG]
        scores = [None] * N_HEADS
        for d in range(2):
            fpre = p_ref[0, rows, (1 + d) * DG:(2 + d) * DG].astype(F32)
            lb = lb_ref[d:d + 1, :]
            sg = jax.nn.sigmoid(fpre)
            g = jnp.log(lb + (1.0 - lb) * sg)
            k = (1.0 - lb) * (1.0 - sg)
            gi = _dot01_2(tin_ref[...], g)
            glast = _chunk_bcast(gi, H_CHUNK - 1)
            if d == 0:
                gc = gi
                gref = _chunk_bcast(gi, H_CHUNK // 2 - 1)
                msk = mask_f
            else:
                gc = glast - gi + g
                gref = _chunk_bcast(gc, H_CHUNK // 2)
                msk = mask_b
            qe = (q * jnp.exp(gc - gref)).astype(BF16)
            ke = (k * jnp.exp(gref - gc)).astype(BF16)
            for h in range(N_HEADS):
                hs = slice(h * HD, (h + 1) * HD)
                sc = jnp.where(msk, lax.dot_general(qe[:, hs], ke[:, hs], _NT, preferred_element_type=F32), 0.0)
                scores[h] = sc if d == 0 else scores[h] + sc
            qd = q * jnp.exp(gc)
            kd = (k * jnp.exp(glast - gc)).astype(BF16)
            for j in range(H_CPB):
                c = bi * H_CPB + j
                cr = slice(j * H_CHUNK, (j + 1) * H_CHUNK)
                qm_s[d, c] = jnp.concatenate([jnp.where(lane_head == h, qd[cr, :], 0.0) for h in range(N_HEADS)],
                                             axis=0).astype(BF16)
                ut = lax.dot_general(v[cr, :], kd[cr, :], _TN, preferred_element_type=F32)
                packed = ut[0:HD, :]
                for h in range(1, N_HEADS):
                    packed = jnp.where(lane_head == h, ut[h * HD:(h + 1) * HD, :], packed)
                ut_s[d, c] = packed.astype(BF16)
                dec_s[d, c] = jnp.broadcast_to(jnp.exp(glast[j * H_CHUNK:j * H_CHUNK + 1, :]), (V7X_SUBLANES, DG))
        for h in range(N_HEADS):
            oi_s[h, rows, :] = _dot(scores[h].astype(BF16), v[:, h * HD:(h + 1) * HD])
        return 0

    lax.fori_loop(0, N_HBLK, block, 0)

    st_s[...] = jnp.zeros(st_s.shape, F32)

    def step(i, _):
        for d in range(2):
            c = i if d == 0 else N_HCH - 1 - i
            rows = pl.ds(pl.multiple_of(c * H_CHUNK, H_CHUNK), H_CHUNK)
            st = st_s[d]
            inter = lax.dot_general(qm_s[d, c], st.astype(BF16), _NT, preferred_element_type=F32)
            for h in range(N_HEADS):
                oe_s[d, h, rows, :] = inter[h * H_CHUNK:(h + 1) * H_CHUNK, :]
            st_s[d] = st * dec_s[d, c][0:1, :] + ut_s[d, c].astype(F32)
        return 0

    lax.fori_loop(0, N_HCH, step, 0, unroll=8)

    def finish(c, _):
        r0 = pl.multiple_of(c * CONV_ROWS, CONV_ROWS)
        rows = pl.ds(r0, CONV_ROWS)
        gate = _silu(p_ref[0, rows, 4 * DG:5 * DG].astype(F32))
        outs = [_rms(oi_s[h, rows, :] + oe_s[0, h, rows, :] + oe_s[1, h, rows, :]) for h in range(N_HEADS)]
        o_ref[0, rows, :] = (jnp.concatenate(outs, axis=1) * nw_ref[...] * gate).astype(BF16)
        return 0

    lax.fori_loop(0, SEQ // CONV_ROWS, finish, 0, unroll=2)


def hgrn2(p3, lb2, norm_w_lanes, tri_in_chunk):
    b = p3.shape[0]
    full = lambda a: pl.BlockSpec(a.shape, lambda i: (0,) * a.ndim)
    return pl.pallas_call(
        _hgrn_body, grid=(b,),
        in_specs=[pl.BlockSpec((1, SEQ, 5 * DG), lambda i: (i, 0, 0)), full(lb2), full(norm_w_lanes),
                  full(tri_in_chunk)],
        out_specs=pl.BlockSpec((1, SEQ, DG), lambda i: (i, 0, 0)),
        out_shape=jax.ShapeDtypeStruct((b, SEQ, DG), BF16),
        scratch_shapes=[pltpu.VMEM((2, N_HCH, N_HEADS * H_CHUNK, DG), BF16),
                        pltpu.VMEM((2, N_HCH, HD, DG), BF16),
                        pltpu.VMEM((N_HEADS, SEQ, HD), F32),
                        pltpu.VMEM((2, N_HCH, V7X_SUBLANES, DG), F32),
                        pltpu.VMEM((2, N_HEADS, SEQ, HD), F32),
                        pltpu.VMEM((2, HD, DG), F32)],
        compiler_params=_params("parallel"), name="hgrn2",
    )(p3, lb2, norm_w_lanes, tri_in_chunk)


@functools.lru_cache(maxsize=None)
def _tables():
    t = {}
    k = np.arange(SEQ, dtype=np.int64)
    ang = 2.0 * np.pi * ((k[:, None] * k[None, :]) % NFFT).astype(np.float64) / NFFT
    t["cos"] = np.cos(ang).astype(np.float32)
    t["sin"] = np.sin(ang).astype(np.float32)
    rows_f32 = np.concatenate([t["cos"], t["sin"]], axis=0)
    t["dft_rows"] = rows_f32.astype(ml_dtypes.bfloat16)
    t["dft_rows_lo"] = (rows_f32 - t["dft_rows"].astype(np.float32)).astype(ml_dtypes.bfloat16)
    t["dft_cols"] = np.concatenate([t["cos"], t["sin"]], axis=1).astype(ml_dtypes.bfloat16)
    tt = np.linspace(0.0, 1.0, SEQ, dtype=np.float32)[:, None]
    bands = (HY_POS_DIM - 1) // 2
    ang_pos = (2.0 * math.pi * np.arange(SEQ, dtype=np.float32) / SEQ).astype(np.float32)
    f = np.linspace(1e-4, bands - 1, bands, dtype=np.float32)
    a2 = (ang_pos[:, None] * f[None, :]).astype(np.float32)
    z = np.concatenate([tt, np.cos(a2), -np.sin(a2)], axis=-1).astype(np.float32)
    zp = np.zeros((SEQ, V7X_LANES), np.float32)
    zp[:, :HY_POS_DIM] = z
    t["zpos"] = zp
    max_decay = math.log(1e-2) / 0.3
    min_decay = math.log(1e-2) / 1.5
    deltas = np.abs(np.linspace(min_decay, max_decay, DG, dtype=np.float32))
    t["decay"] = np.exp(-tt * deltas[None, :]).astype(np.float32)
    i128 = np.arange(V7X_LANES)
    t["u128"] = (i128[:, None] < i128[None, :]).astype(np.float32)
    im = np.arange(M_CHUNK)
    t["tri_incl"] = (im[:, None] <= im[None, :]).astype(np.float32)
    ib = np.arange(H_BLK)
    t["tri_in_chunk"] = ((ib[:, None] // H_CHUNK == ib[None, :] // H_CHUNK)
                         & (ib[None, :] <= ib[:, None])).astype(np.float32)
    idg = np.arange(DG)
    t["bdmask"] = (idg[:, None] // HD == idg[None, :] // HD).astype(np.float32)
    def bucket(rel):
        nb = N_BUCKETS // 2
        max_exact = nb // 2
        ret = (rel > 0).astype(np.int64) * nb
        n = np.abs(rel)
        nf = np.maximum(n, 1).astype(np.float64)
        large = max_exact + (np.log(nf / max_exact) / math.log(MAX_DISTANCE / max_exact)
                             * (nb - max_exact)).astype(np.int64)
        large = np.minimum(large, nb - 1)
        return ret + np.where(n < max_exact, n, large)

    for dil in A_DILS:
        n = SEQ // dil
        qi = np.arange(A_TQ)[:, None]
        kj = np.arange(A_KW)[None, :]
        ids = []
        for s0 in (0, -A_BAND, -(A_KW - A_TQ)):
            kk = kj + s0
            rel = kk - qi
            ok = np.abs(rel) <= A_BAND
            if n == A_TQ:
                ok &= (kk >= 0) & (kk < A_TQ)
            ids.append(np.where(ok, bucket(rel * dil), -1))
        t[f"bucket{dil}"] = np.stack(ids).astype(np.int32)
    return t


def kernel(x, w_in, w_out, norm_mix_w, norm_ffn_w, hy_conv_w, hy_pos_w1, hy_pos_b1, hy_pos_w2, hy_pos_b2,
           hy_sin_freq, hy_pos_w3, hy_filt_bias, m_conv_w, m_conv_b, m_dt_bias, m_A_log, m_D, m_norm_w, rel_bias,
           hg_lb, hg_norm_w, router_w, moe_w_gate, moe_w_up, moe_w_down, final_norm_w):
    b = x.shape[0]
    assert x.shape[1:] == (SEQ, D_MODEL) and b % MOE_FFN_SEQS == 0 and b % ROUTER_SEQS == 0, x.shape
    t = b * SEQ
    tb = _tables()
    dft_rows_lo = jnp.asarray(tb["dft_rows_lo"])
    dft_rows = jnp.asarray(tb["dft_rows"])
    dft_cols = jnp.asarray(tb["dft_cols"])
    u128 = jnp.asarray(tb["u128"]).astype(BF16)
    tri_incl = jnp.asarray(tb["tri_incl"]).astype(BF16)
    tri_in_chunk = jnp.asarray(tb["tri_in_chunk"]).astype(BF16)
    bdmask = jnp.asarray(tb["bdmask"])
    attn_bias = [attention_bias_table(jnp.asarray(tb[f"bucket{d}"]), rel_bias.astype(F32)) for d in A_DILS]

    sm = jax.nn.softmax(hg_lb.astype(F32), axis=0)
    lower_bounds = jnp.cumsum(sm, axis=0) - sm[:1]

    xa = x.reshape(t, D_MODEL)
    for l in range(DEPTH):
        wl = w_in[l]
        w_main = jnp.concatenate([wl[:, 0:768], wl[:, 768:1024], wl[:, 1024:1536], wl[:, 1544:2312],
                                  wl[:, 2312:3592]], axis=1).astype(BF16)
        w_dt_rows = wl[:, 1536:1544].T.astype(BF16)
        z, x0, mz, mx, at, hg, dtc = in_projection(xa, norm_mix_w[l][None, :], w_main, w_dt_rows, hy_conv_w[l])

        w1p = jnp.zeros((V7X_LANES, HY_HID), F32).at[:HY_POS_DIM].set(hy_pos_w1[l])
        kr, ki, kny = hyena_filter_spectrum(
            jnp.asarray(tb["zpos"]), w1p, hy_pos_b1[l][None, :], hy_pos_w2[l], hy_pos_b2[l][None, :],
            hy_sin_freq[l][None, :], hy_pos_w3[l], jnp.asarray(tb["decay"]), dft_rows, dft_rows_lo)
        z3, x03 = z.reshape(b, SEQ, DG), x0.reshape(b, SEQ, DG)
        ya = hyena_conv(z3, x03, dft_rows, dft_cols, kr, ki, kny, hy_filt_bias[l][None, :]).reshape(t, DG)

        a_col = (-jnp.exp(m_A_log[l].astype(F32))).reshape(8, 1)
        yb = mamba2(mz.reshape(b, SEQ, DG), mx.reshape(b, SEQ, 2 * DG), dtc.reshape(b, N_MCH, 8, MQ),
                    m_conv_w[l], m_conv_b[l][None, :], m_dt_bias[l].reshape(8, 1), a_col,
                    jnp.repeat(m_D[l].astype(F32), HD)[None, :], m_norm_w[l][None, :], tri_incl, bdmask).reshape(t, DG)

        yc = dilated_attention(at.reshape(b, SEQ, 3 * DG), *attn_bias).reshape(t, DG)

        lbl = lower_bounds[l]
        yd = hgrn2(hg.reshape(b, SEQ, 5 * DG), lbl, jnp.tile(hg_norm_w[l], N_HEADS)[None, :],
                   tri_in_chunk).reshape(t, DG)

        rw_rows = router_w[l].T.astype(F32)
        rw_hi = rw_rows.astype(BF16)
        rw_lo = (rw_rows - rw_hi.astype(F32)).astype(BF16)
        xo, xn, logits = out_projection(xa, ya, yb, yc, yd, w_out[l].reshape(4, DG, D_MODEL).astype(BF16),
                                        norm_ffn_w[l][None, :], rw_hi, rw_lo)
        xn3 = xn.reshape(b, SEQ, D_MODEL)
        rank, gate, seg = router(logits, u128)
        seg_flat = seg[:, :, :MOE_SEG_STRIDE].reshape(-1)
        xe = moe_gather(seg_flat, xn3, rank)
        ye = moe_experts(xe, moe_w_gate, moe_w_up, moe_w_down, l)
        xa = moe_scatter(seg_flat, ye, rank, gate, xo.reshape(b, SEQ, D_MODEL), final_norm_w[None, :],
                         final=(l == DEPTH - 1)).reshape(t, D_MODEL)
    return xa.reshape(b, SEQ, D_MODEL)
```

```python
import functools
import math

import ml_dtypes
import numpy as np
import jax
import jax.numpy as jnp
from jax import lax
from jax.experimental import pallas as pl
from jax.experimental.pallas import tpu as pltpu

F32 = jnp.float32
BF16 = jnp.bfloat16
I32 = jnp.int32

D_MODEL = 1024
SEQ = 2048
DEPTH = 2
DG = 256
N_HEADS = 4
HD = 64
HY_POS_DIM = 33
HY_HID = 64
M_CONV = 5
M_STATE = 64
M_CHUNK = 128
H_CHUNK = 32
A_BAND = 64
A_DILS = (1, 4, 16)
N_BUCKETS = 32
MAX_DISTANCE = 1024
N_EXPERTS = 16
CAP = 2 * SEQ // N_EXPERTS
D_FF = 1024
EPS = 1e-6
NFFT = 2 * SEQ

V7X_LANES = 128
V7X_SUBLANES = 8
V7X_VMEM_LIMIT_BYTES = 56 * 1024 * 1024

NEG_BIG = -1e30

_NT = (((1,), (1,)), ((), ()))
_TN = (((0,), (0,)), ((), ()))


def _params(*sem):
    return pltpu.CompilerParams(dimension_semantics=sem, vmem_limit_bytes=V7X_VMEM_LIMIT_BYTES)


def _dot(a, b):
    return jnp.dot(a, b, preferred_element_type=F32)


def _dot_hi(a, b):
    return jnp.dot(a, b, preferred_element_type=F32, precision=lax.Precision.HIGHEST)


def _dot01_2(t_bf16, x):
    x1 = x.astype(BF16)
    x2 = (x - x1.astype(F32)).astype(BF16)
    return _dot(t_bf16, x1) + _dot(t_bf16, x2)


def _dot01_rhs(x, t_bf16):
    x1 = x.astype(BF16)
    r1 = x - x1.astype(F32)
    x2 = r1.astype(BF16)
    x3 = (r1 - x2.astype(F32)).astype(BF16)
    return _dot(x1, t_bf16) + _dot(x2, t_bf16) + _dot(x3, t_bf16)


def _silu(x):
    return x * jax.nn.sigmoid(x)


def _softplus(x):
    return jnp.maximum(x, 0.0) + jnp.log(1.0 + jnp.exp(-jnp.abs(x)))


def _rms(x):
    return x * lax.rsqrt(jnp.mean(x * x, axis=-1, keepdims=True) + EPS)


TM_PROJ = 1024
_HY0, _MZ0, _MX0, _AT0, _HG0, _PEND = 0, 768, 1024, 1536, 2304, 3584


TILES_PER_SEQ = SEQ // TM_PROJ
PROJ_HALO = 16


def _inproj_body(x_ref, xp_ref, xq_ref, nw_ref, w_ref, wdt_ref, hcw_ref, z_ref, x0_ref, mz_ref, mx_ref, at_ref, hg_ref,
                 dtc_ref):
    norm = lambda v: (_rms(v) * nw_ref[...]).astype(BF16)
    hn = norm(x_ref[...])
    tile = pl.program_id(0) % TILES_PER_SEQ
    zero = jnp.zeros((PROJ_HALO, D_MODEL), BF16)
    h_ext = jnp.concatenate([jnp.where(tile == 0, zero, norm(xp_ref[...])), hn,
                             jnp.where(tile == TILES_PER_SEQ - 1, zero, norm(xq_ref[...]))], axis=0)
    p = _dot(h_ext, w_ref[:, _HY0:_MZ0])
    rows_ext = TM_PROJ + 2 * PROJ_HALO
    mid = slice(PROJ_HALO, PROJ_HALO + TM_PROJ)
    u = (pltpu.roll(p, 1, 0)[mid] * hcw_ref[0:1, :] + p[mid] * hcw_ref[1:2, :]
         + pltpu.roll(p, rows_ext - 1, 0)[mid] * hcw_ref[2:3, :])
    x0_ref[...] = u[:, 0:DG].astype(BF16)
    z_ref[...] = (u[:, 2 * DG:3 * DG] * u[:, DG:2 * DG]).astype(BF16)
    mz_ref[...] = _dot(hn, w_ref[:, _MZ0:_MX0]).astype(BF16)
    mx_ref[...] = _dot(hn, w_ref[:, _MX0:_AT0]).astype(BF16)
    at_ref[...] = _dot(hn, w_ref[:, _AT0:_HG0]).astype(BF16)
    hg_ref[...] = _dot(hn, w_ref[:, _HG0:_PEND]).astype(BF16)
    dt_rows = lax.dot_general(wdt_ref[...], hn, _NT, preferred_element_type=F32)
    for j in range(TM_PROJ // M_CHUNK):
        dtc_ref[j] = dt_rows[:, j * M_CHUNK:(j + 1) * M_CHUNK]


def in_projection(x, norm_w, w_main, w_dt_rows, hy_conv_w):
    t = x.shape[0]
    tm = TM_PROJ
    halo = tm // PROJ_HALO
    row = lambda w: pl.BlockSpec((tm, w), lambda i: (i, 0))
    full = lambda a: pl.BlockSpec(a.shape, lambda i: (0,) * a.ndim)
    in_specs = [row(D_MODEL),
                pl.BlockSpec((PROJ_HALO, D_MODEL), lambda i: (jnp.maximum(i * halo - 1, 0), 0)),
                pl.BlockSpec((PROJ_HALO, D_MODEL), lambda i: (jnp.minimum((i + 1) * halo, t // PROJ_HALO - 1), 0)),
                full(norm_w), full(w_main), full(w_dt_rows), full(hy_conv_w)]
    widths = (256, 256, 256, 512, 768, 1280)
    out_shape = [jax.ShapeDtypeStruct((t, w), BF16) for w in widths]
    out_shape.append(jax.ShapeDtypeStruct((t // M_CHUNK, 8, M_CHUNK), F32))
    out_specs = [row(w) for w in widths] + [pl.BlockSpec((tm // M_CHUNK, 8, M_CHUNK), lambda i: (i, 0, 0))]
    return pl.pallas_call(
        _inproj_body, grid=(t // tm,), in_specs=in_specs, out_specs=out_specs, out_shape=out_shape,
        compiler_params=_params("parallel"), name="in_projection",
    )(x, x, x, norm_w, w_main, w_dt_rows, hy_conv_w)


TM_OUT = 1024


def _outproj_body(x_ref, ya_ref, yb_ref, yc_ref, yd_ref, w_ref, nw_ref, rwh_ref, rwl_ref, xo_ref, xn_ref, lg_ref):
    x = x_ref[...]
    acc = x + _dot(ya_ref[...], w_ref[0]) + _dot(yb_ref[...], w_ref[1])
    acc = acc + _dot(yc_ref[...], w_ref[2]) + _dot(yd_ref[...], w_ref[3])
    xo_ref[...] = acc
    xn = _rms(acc) * nw_ref[...]
    xh = xn.astype(BF16)
    xn_ref[...] = xh
    xl = (xn - xh.astype(F32)).astype(BF16)
    nt = lambda w, a: lax.dot_general(w, a, _NT, preferred_element_type=F32)
    lg_ref[...] = nt(rwh_ref[...], xh) + nt(rwh_ref[...], xl) + nt(rwl_ref[...], xh)


def out_projection(x, ya, yb, yc, yd, w_out4, norm_w, rw_hi, rw_lo):
    t = x.shape[0]
    tm = TM_OUT
    row = lambda w: pl.BlockSpec((tm, w), lambda i: (i, 0))
    full = lambda a: pl.BlockSpec(a.shape, lambda i: (0,) * a.ndim)
    in_specs = [row(D_MODEL)] + [row(DG)] * 4 + [full(w_out4), full(norm_w), full(rw_hi), full(rw_lo)]
    return pl.pallas_call(
        _outproj_body, grid=(t // tm,), in_specs=in_specs,
        out_specs=[row(D_MODEL), row(D_MODEL), pl.BlockSpec((N_EXPERTS, tm), lambda i: (0, i))],
        out_shape=[jax.ShapeDtypeStruct((t, D_MODEL), F32), jax.ShapeDtypeStruct((t, D_MODEL), BF16),
                   jax.ShapeDtypeStruct((N_EXPERTS, t), F32)],
        compiler_params=_params("parallel"), name="out_projection",
    )(x, ya, yb, yc, yd, w_out4, norm_w, rw_hi, rw_lo)


def _prefix_excl_lanes(mask_f32, u_ref):
    e = mask_f32.shape[0]
    off = jnp.zeros((e, 1), F32)
    parts, bounds = [], [off]
    for k in range(SEQ // V7X_LANES):
        tile = mask_f32[:, k * V7X_LANES:(k + 1) * V7X_LANES]
        parts.append(_dot(tile.astype(BF16), u_ref[...]) + off)
        off = off + jnp.sum(tile, axis=1, keepdims=True)
        bounds.append(off)
    return jnp.concatenate(parts, axis=1), bounds


ROUTER_SEQS = 2


def _router_body(lg_ref, u_ref, rank_ref, gate_ref, seg_ref):
    affs = []
    for q in range(ROUTER_SEQS):
        logits = lg_ref[:, q * SEQ:(q + 1) * SEQ]
        ex = jnp.exp(logits - jnp.max(logits, axis=0, keepdims=True))
        affs.append(ex / jnp.sum(ex, axis=0, keepdims=True))
    aff = jnp.concatenate(affs, axis=0)
    nrow = ROUTER_SEQS * N_EXPERTS
    bits = pltpu.bitcast(aff, I32)

    def search(i, thr):
        cand = thr | jnp.left_shift(jnp.int32(1), 30 - i)
        cnt = jnp.sum((bits >= cand).astype(I32), axis=1, keepdims=True)
        return jnp.where(cnt >= CAP, cand, thr)

    thr = lax.fori_loop(0, 31, search, jnp.zeros((nrow, 1), I32))
    gt = (bits > thr).astype(F32)
    eq = (bits == thr).astype(F32)
    need = CAP - jnp.sum(gt, axis=1, keepdims=True)
    tie_rank, _ = _prefix_excl_lanes(eq, u_ref)
    sel = gt + eq * (tie_rank < need).astype(F32)
    rank, bounds = _prefix_excl_lanes(sel, u_ref)
    rank = jnp.where(sel > 0.0, rank, -1.0)
    lane = lax.broadcasted_iota(I32, (nrow, V7X_LANES), 1)
    seg = jnp.zeros((nrow, V7X_LANES), F32)
    tiles = MOE_SEG // V7X_LANES
    for sgm in range(N_MOE_SEG):
        first = jnp.floor(bounds[sgm * tiles] * (1.0 / MOE_ALIGN)) * MOE_ALIGN
        need = jnp.floor((bounds[(sgm + 1) * tiles] - first + (MOE_TILE - 1)) * (1.0 / MOE_TILE))
        need = jnp.max(need.reshape(ROUTER_SEQS, N_EXPERTS, 1), axis=1, keepdims=True)
        need = jnp.broadcast_to(need, (ROUTER_SEQS, N_EXPERTS, 1)).reshape(nrow, 1)
        seg = jnp.where(lane == sgm, first, seg)
        seg = jnp.where(lane == N_MOE_SEG + sgm, need, seg)
    seg = seg.astype(I32)
    for q in range(ROUTER_SEQS):
        rows = slice(q * N_EXPERTS, (q + 1) * N_EXPERTS)
        rank_ref[q] = rank[rows]
        gate_ref[q] = aff[rows]
        seg_ref[q] = seg[rows]


def router(logits_et, u128):
    b = logits_et.shape[1] // SEQ
    out = jax.ShapeDtypeStruct((b, N_EXPERTS, SEQ), F32)
    return pl.pallas_call(
        _router_body, grid=(b // ROUTER_SEQS,),
        in_specs=[pl.BlockSpec((N_EXPERTS, ROUTER_SEQS * SEQ), lambda i: (0, i)),
                  pl.BlockSpec(u128.shape, lambda i: (0, 0))],
        out_specs=[pl.BlockSpec((ROUTER_SEQS, N_EXPERTS, SEQ), lambda i: (i, 0, 0))] * 2
                  + [pl.BlockSpec((ROUTER_SEQS, N_EXPERTS, V7X_LANES), lambda i: (i, 0, 0))],
        out_shape=[out, out, jax.ShapeDtypeStruct((b, N_EXPERTS, V7X_LANES), I32)],
        compiler_params=_params("parallel"), name="router",
    )(logits_et, u128)


MOE_SEG = 256
N_MOE_SEG = SEQ // MOE_SEG
MOE_TILE = 64
MOE_ALIGN = 16
MOE_GROUP = 4
MOE_SEG_STRIDE = 16
MOE_FFN_SEQS = 4


def _moe_seg_plan(cs_ref, b, s):
    base = b * N_EXPERTS * MOE_SEG_STRIDE
    starts = [cs_ref[base + ex * MOE_SEG_STRIDE + s] for ex in range(N_EXPERTS)]
    return starts, cs_ref[base + N_MOE_SEG + s]


def _moe_tile_bases(starts, r):
    own = [st + r * MOE_TILE for st in starts]
    return [pl.multiple_of(jnp.minimum(o, CAP - MOE_TILE), MOE_ALIGN) for o in own], own


def _moe_onehot_group(rank_ref, gate_ref, s, bases, own, grp):
    lanes = pl.ds(pl.multiple_of(s * MOE_SEG, MOE_SEG), MOE_SEG)
    j = lax.broadcasted_iota(I32, (MOE_TILE, MOE_SEG), 0)
    rows = []
    for ex in grp:
        slot = bases[ex] + j
        hit = (rank_ref[0, ex:ex + 1, lanes] == slot.astype(F32)) & (slot >= own[ex])
        val = 1.0 if gate_ref is None else gate_ref[0, ex:ex + 1, lanes]
        rows.append(jnp.where(hit, val, 0.0).astype(BF16))
    return jnp.concatenate(rows, axis=0)


_MOE_GROUPS = [list(range(g * MOE_GROUP, (g + 1) * MOE_GROUP)) for g in range(N_EXPERTS // MOE_GROUP)]


def _moe_gather_body(cs_ref, xn_ref, rank_ref, xe_ref):
    b = pl.program_id(0)

    def zero(ex, _):
        xe_ref[0, ex] = jnp.zeros((CAP, D_MODEL), BF16)
        return 0

    lax.fori_loop(0, N_EXPERTS, zero, 0)

    def seg_gather(s, _):
        starts, rounds = _moe_seg_plan(cs_ref, b, s)
        xn_seg = xn_ref[0, pl.ds(pl.multiple_of(s * MOE_SEG, MOE_SEG), MOE_SEG), :]

        def one_round(r, _):
            bases, own = _moe_tile_bases(starts, r)
            for grp in _MOE_GROUPS:
                got = _dot(_moe_onehot_group(rank_ref, None, s, bases, own, grp), xn_seg)
                for k, ex in enumerate(grp):
                    rows = pl.ds(bases[ex], MOE_TILE)
                    xe_ref[0, ex, rows, :] += got[k * MOE_TILE:(k + 1) * MOE_TILE].astype(BF16)
            return 0

        lax.fori_loop(0, rounds, one_round, 0)
        return 0

    lax.fori_loop(0, N_MOE_SEG, seg_gather, 0)


def moe_gather(seg_counts_flat, xn3, rank3):
    b = xn3.shape[0]
    grid_spec = pltpu.PrefetchScalarGridSpec(
        num_scalar_prefetch=1, grid=(b,),
        in_specs=[pl.BlockSpec((1, SEQ, D_MODEL), lambda i, cs: (i, 0, 0)),
                  pl.BlockSpec((1, N_EXPERTS, SEQ), lambda i, cs: (i, 0, 0))],
        out_specs=pl.BlockSpec((1, N_EXPERTS, CAP, D_MODEL), lambda i, cs: (i, 0, 0, 0)))
    return pl.pallas_call(
        _moe_gather_body, grid_spec=grid_spec,
        out_shape=jax.ShapeDtypeStruct((b, N_EXPERTS, CAP, D_MODEL), BF16),
        compiler_params=_params("parallel"), name="moe_gather",
    )(seg_counts_flat, xn3, rank3)


def _moe_experts_body(xe_ref, wg_ref, wu_ref, wd_ref, ye_ref, wg_s, wu_s, wd_s):
    @pl.when(pl.program_id(1) == 0)
    def _():
        wg_s[...] = wg_ref[0, 0].astype(BF16)
        wu_s[...] = wu_ref[0, 0].astype(BF16)
        wd_s[...] = wd_ref[0, 0].astype(BF16)

    xe = xe_ref[...].reshape(MOE_FFN_SEQS * CAP, D_MODEL)
    hid = (_silu(_dot(xe, wg_s[...])) * _dot(xe, wu_s[...])).astype(BF16)
    ye_ref[...] = _dot(hid, wd_s[...]).astype(BF16).reshape(MOE_FFN_SEQS, 1, CAP, D_MODEL)


def moe_experts(xe4, w_gate, w_up, w_down, layer):
    b = xe4.shape[0]
    blk = pl.BlockSpec((MOE_FFN_SEQS, 1, CAP, D_MODEL), lambda e, g: (g, e, 0, 0))
    w_spec = lambda a: pl.BlockSpec((1, 1) + a.shape[2:], lambda e, g: (layer, e, 0, 0))
    return pl.pallas_call(
        _moe_experts_body, grid=(N_EXPERTS, b // MOE_FFN_SEQS),
        in_specs=[blk, w_spec(w_gate), w_spec(w_up), w_spec(w_down)],
        out_specs=blk, out_shape=jax.ShapeDtypeStruct(xe4.shape, BF16),
        scratch_shapes=[pltpu.VMEM((D_MODEL, D_FF), BF16), pltpu.VMEM((D_MODEL, D_FF), BF16),
                        pltpu.VMEM((D_FF, D_MODEL), BF16)],
        compiler_params=_params("parallel", "arbitrary"), name="moe_experts",
    )(xe4, w_gate, w_up, w_down)


MOE_SCATTER_SEGS = 4


def _moe_scatter_body(final, cs_ref, ye_ref, rank_ref, gate_ref, xo_ref, nw_ref, o_ref):
    b = pl.program_id(0)
    half = pl.program_id(1)

    def seg_scatter(k, _):
        s = half * MOE_SCATTER_SEGS + k
        starts, rounds = _moe_seg_plan(cs_ref, b, s)
        tok = pl.ds(pl.multiple_of(k * MOE_SEG, MOE_SEG), MOE_SEG)
        o_ref[0, tok, :] = xo_ref[0, tok, :]

        def one_round(r, _):
            bases, own = _moe_tile_bases(starts, r)
            for grp in _MOE_GROUPS:
                ye = jnp.concatenate([ye_ref[0, ex, pl.ds(bases[ex], MOE_TILE), :] for ex in grp], axis=0)
                o_ref[0, tok, :] += lax.dot_general(_moe_onehot_group(rank_ref, gate_ref, s, bases, own, grp), ye,
                                                    _TN, preferred_element_type=F32)
            return 0

        lax.fori_loop(0, rounds, one_round, 0)
        if final:
            o_ref[0, tok, :] = _rms(o_ref[0, tok, :]) * nw_ref[...]
        return 0

    lax.fori_loop(0, MOE_SCATTER_SEGS, seg_scatter, 0)


def moe_scatter(seg_counts_flat, ye4, rank3, gate3, xo3, final_norm_w, final):
    b = ye4.shape[0]
    rows = MOE_SCATTER_SEGS * MOE_SEG
    sel_spec = pl.BlockSpec((1, N_EXPERTS, SEQ), lambda i, j, cs: (i, 0, 0))
    tok_spec = pl.BlockSpec((1, rows, D_MODEL), lambda i, j, cs: (i, j, 0))
    grid_spec = pltpu.PrefetchScalarGridSpec(
        num_scalar_prefetch=1, grid=(b, SEQ // rows),
        in_specs=[pl.BlockSpec((1, N_EXPERTS, CAP, D_MODEL), lambda i, j, cs: (i, 0, 0, 0)), sel_spec, sel_spec,
                  tok_spec, pl.BlockSpec(final_norm_w.shape, lambda i, j, cs: (0, 0))],
        out_specs=tok_spec)
    return pl.pallas_call(
        functools.partial(_moe_scatter_body, final), grid_spec=grid_spec,
        out_shape=jax.ShapeDtypeStruct((b, SEQ, D_MODEL), F32),
        compiler_params=_params("parallel", "arbitrary"), name="moe_scatter",
    )(seg_counts_flat, ye4, rank3, gate3, xo3, final_norm_w)


HY_KB = 256
HY_ROWS = 256


def _split_bf16(x):
    hi = x.astype(BF16)
    return hi, (x - hi.astype(F32)).astype(BF16)


def _hy_filter_body(z_ref, w1_ref, b1_ref, w2_ref, b2_ref, fr_ref, w3_ref, dec_ref, ch_ref, cl_ref, sh_ref, sl_ref,
                    kr_ref, ki_ref, kny_ref, ah_s, al_s, dh_s, dl_s):
    @pl.when(pl.program_id(0) == 0)
    def _():
        def rows(c, kny):
            r0 = pl.multiple_of(c * HY_ROWS, HY_ROWS)
            fr = fr_ref[...]
            h = jnp.sin(fr * (_dot_hi(z_ref[pl.ds(r0, HY_ROWS), :], w1_ref[...]) + b1_ref[...]))
            h = jnp.sin(fr * (_dot_hi(h, w2_ref[...]) + b2_ref[...]))
            h = _dot_hi(h, w3_ref[...])
            dec = dec_ref[pl.ds(r0, HY_ROWS), :]
            pos = r0 + lax.broadcasted_iota(I32, (HY_ROWS, DG), 0)
            hf = h[:, :DG] * dec
            hb = jnp.where(pos == 0, 0.0, h[:, DG:] * dec)
            a = hf + hb
            ah_s[pl.ds(r0, HY_ROWS), :], al_s[pl.ds(r0, HY_ROWS), :] = _split_bf16(a)
            dh_s[pl.ds(r0, HY_ROWS), :], dl_s[pl.ds(r0, HY_ROWS), :] = _split_bf16(hf - hb)
            sgn = (1 - 2 * (pos & 1)).astype(F32)
            return kny + jnp.sum(a * sgn, axis=0, keepdims=True)

        kny = lax.fori_loop(0, SEQ // HY_ROWS, rows, jnp.zeros((1, DG), F32))
        kny_ref[...] = jnp.broadcast_to(kny, kny_ref.shape)

    kr_ref[...] = _dot(ch_ref[...], ah_s[...]) + _dot(ch_ref[...], al_s[...]) + _dot(cl_ref[...], ah_s[...])
    ki_ref[...] = _dot(sh_ref[...], dh_s[...]) + _dot(sh_ref[...], dl_s[...]) + _dot(sl_ref[...], dh_s[...])


def hyena_filter_spectrum(zpos, w1, b1, w2, b2, freq, w3, decay, dft_rows, dft_rows_lo):
    full = lambda a: pl.BlockSpec(a.shape, lambda k: (0,) * a.ndim)
    kblk = pl.BlockSpec((HY_KB, SEQ), lambda k: (k, 0))
    sblk = pl.BlockSpec((HY_KB, SEQ), lambda k: (SEQ // HY_KB + k, 0))
    oblk = pl.BlockSpec((HY_KB, DG), lambda k: (k, 0))
    return pl.pallas_call(
        _hy_filter_body, grid=(SEQ // HY_KB,),
        in_specs=[full(zpos), full(w1), full(b1), full(w2), full(b2), full(freq), full(w3), full(decay),
                  kblk, kblk, sblk, sblk],
        out_specs=[oblk, oblk, pl.BlockSpec((V7X_SUBLANES, DG), lambda k: (0, 0))],
        out_shape=[jax.ShapeDtypeStruct((SEQ, DG), F32), jax.ShapeDtypeStruct((SEQ, DG), F32),
                   jax.ShapeDtypeStruct((V7X_SUBLANES, DG), F32)],
        scratch_shapes=[pltpu.VMEM((SEQ, DG), BF16)] * 4,
        compiler_params=_params("arbitrary"), name="hyena_filter",
    )(zpos, w1, b1, w2, b2, freq, w3, decay, dft_rows, dft_rows_lo, dft_rows, dft_rows_lo)


CONV_ROWS = 128
CONV_HALO = 8


def _dwconv_rows(pad_ref, w_ref, r0, lanes, k):
    n = CONV_ROWS + 2 * CONV_HALO
    win = pad_ref[pl.ds(r0, n), lanes]
    acc = None
    for j in range(k):
        sh = (k // 2 - j) % n
        rolled = win if sh == 0 else pltpu.roll(win, sh, 0)
        term = rolled[CONV_HALO:CONV_HALO + CONV_ROWS] * w_ref[j:j + 1, lanes]
        acc = term if acc is None else acc + term
    return acc


def _fill_padded(pad_ref, src_ref, width):
    zeros = jnp.zeros((CONV_HALO, width), F32)
    pad_ref[pl.ds(0, CONV_HALO), :] = zeros
    pad_ref[pl.ds(SEQ + CONV_HALO, CONV_HALO), :] = zeros

    def fill(c, _):
        r0 = pl.multiple_of(c * CONV_ROWS, CONV_ROWS)
        pad_ref[pl.ds(r0 + CONV_HALO, CONV_ROWS), :] = src_ref[0, pl.ds(r0, CONV_ROWS), :].astype(F32)
        return 0

    lax.fori_loop(0, SEQ // CONV_ROWS, fill, 0)


HY_FB = 512


def _hy_conv_body(z_ref, x0_ref, t1_ref, t2_ref, kr_ref, ki_ref, kny_ref, fb_ref, o_ref, y_s):
    z = z_ref[0]

    def spectrum(kb, _):
        rows = pl.ds(pl.multiple_of(kb * HY_FB, HY_FB), HY_FB)
        rows_s = pl.ds(pl.multiple_of(SEQ + kb * HY_FB, HY_FB), HY_FB)
        zr = _dot(t1_ref[rows, :], z)
        zi = _dot(t1_ref[rows_s, :], z)
        krow = kb * HY_FB + lax.broadcasted_iota(I32, (HY_FB, 1), 0)
        wk = jnp.where(krow == 0, 1.0 / NFFT, 2.0 / NFFT)
        kr = kr_ref[rows, :]
        ki = ki_ref[rows, :]
        y_s[rows, :] = ((zr * kr - zi * ki) * wk).astype(BF16)
        y_s[rows_s, :] = ((zr * ki + zi * kr) * wk).astype(BF16)
        return 0

    lax.fori_loop(0, SEQ // HY_FB, spectrum, 0)

    zny = jnp.sum(z.astype(F32) * (1 - 2 * (lax.broadcasted_iota(I32, (SEQ, DG), 0) & 1)).astype(F32),
                  axis=0, keepdims=True)
    nyq = zny * kny_ref[0:1, :] * (1.0 / NFFT)

    def synth(tb, _):
        rows = pl.ds(pl.multiple_of(tb * HY_FB, HY_FB), HY_FB)
        conv = _dot(t2_ref[rows, :], y_s[...])
        sgn = (1 - 2 * (lax.broadcasted_iota(I32, (HY_FB, DG), 0) & 1)).astype(F32)
        zf = z_ref[0, rows, :].astype(F32)
        o_ref[0, rows, :] = (x0_ref[0, rows, :].astype(F32) * (conv + nyq * sgn + zf * fb_ref[...])).astype(BF16)
        return 0

    lax.fori_loop(0, SEQ // HY_FB, synth, 0)


def hyena_conv(z3, x03, dft_rows, dft_cols, kr, ki, kny, fbias):
    b = z3.shape[0]
    seq_blk = pl.BlockSpec((1, SEQ, DG), lambda i: (i, 0, 0))
    once = lambda a: pl.BlockSpec(a.shape, lambda i: (0,) * a.ndim, pipeline_mode=pl.Buffered(1))
    return pl.pallas_call(
        _hy_conv_body, grid=(b,),
        in_specs=[seq_blk, seq_blk, once(dft_rows), once(dft_cols), once(kr), once(ki), once(kny), once(fbias)],
        out_specs=seq_blk, out_shape=jax.ShapeDtypeStruct((b, SEQ, DG), BF16),
        scratch_shapes=[pltpu.VMEM((2 * SEQ, DG), BF16)],
        compiler_params=_params("parallel"), name="hyena_conv",
    )(z3, x03, dft_rows, dft_cols, kr, ki, kny, fbias)


N_MCH = SEQ // M_CHUNK
MQ = M_CHUNK


def _head_lane_vec(rows8, base):
    lane_head = lax.broadcasted_iota(I32, (1, DG), 1) // HD
    out = jnp.zeros((1, DG), F32)
    for h in range(N_HEADS):
        out = jnp.where(lane_head == h, rows8[base + h:base + h + 1, :], out)
    return out


def _mamba_body(z_ref, xbc_ref, dtc_ref, cw_ref, cb_ref, dtb_ref, a_ref, dsk_ref, nw_ref, tri_ref, bd_ref,
                o_ref, pad, xs_s, b_s, c_s, y_s, u_s, dec_s, cw_s, yo_s, st_s):
    _fill_padded(pad, xbc_ref, 2 * DG)

    def conv_rows(c, _):
        r0 = pl.multiple_of(c * CONV_ROWS, CONV_ROWS)
        for g in range(4):
            lanes = slice(g * V7X_LANES, (g + 1) * V7X_LANES)
            u = _silu(_dwconv_rows(pad, cw_ref, r0, lanes, M_CONV) + cb_ref[:, lanes])
            if g < 2:
                xs_s[pl.ds(r0, CONV_ROWS), lanes] = u
            elif g == 2:
                b_s[pl.ds(r0, CONV_ROWS), :] = u.astype(BF16)
            else:
                c_s[pl.ds(r0, CONV_ROWS), :] = u.astype(BF16)
        return 0

    lax.fori_loop(0, SEQ // CONV_ROWS, conv_rows, 0)

    li = lax.broadcasted_iota(I32, (MQ, MQ), 0)
    si = lax.broadcasted_iota(I32, (MQ, MQ), 1)
    lower = si <= li
    upper = si >= li
    upper_half = li >= M_STATE
    first_group = si < M_STATE
    bdmask = bd_ref[...]

    def chunk(c, _):
        r0 = pl.multiple_of(c * MQ, MQ)
        dt = _softplus(dtc_ref[0, c] + dtb_ref[...])
        a = dt * a_ref[...]
        cum = _dot01_rhs(a, tri_ref[...])
        tot = cum[:, MQ - 1:MQ]
        suf = tot - cum + a
        row_dir = lax.broadcasted_iota(I32, (8, MQ), 0) // N_HEADS
        seg = jnp.where(row_dir == 0, cum, suf)
        wgt = jnp.exp(tot - seg) * dt
        carry = jnp.exp(seg)
        x = xs_s[pl.ds(r0, MQ), :]
        xb = x.astype(BF16)
        bm = b_s[pl.ds(r0, MQ), :]
        cm = c_s[pl.ds(r0, MQ), :]
        cmf = cm.astype(F32)
        cswap = pltpu.roll(cmf, M_STATE, 1)
        c_dup = [jnp.where(first_group, cmf, cswap), jnp.where(first_group, cswap, cmf)]
        bt = bm.astype(F32).T
        ydiag = []
        for h in range(N_HEADS):
            g = h // 2
            cb = lax.dot_general(cm[:, g * M_STATE:(g + 1) * M_STATE], bm[:, g * M_STATE:(g + 1) * M_STATE],
                                 _NT, preferred_element_type=F32)
            sf = jnp.broadcast_to(seg[h:h + 1, :], (MQ, MQ))
            sb = jnp.broadcast_to(seg[4 + h:5 + h, :], (MQ, MQ))
            lf = jnp.where(lower, jnp.exp(jnp.minimum(sf.T - sf, 0.0)), 0.0)
            lb = jnp.where(upper, jnp.exp(jnp.minimum(sb.T - sb, 0.0)), 0.0)
            m = cb * (lf * dt[h:h + 1, :] + lb * dt[4 + h:5 + h, :])
            ydiag.append(_dot(m.astype(BF16), xb[:, h * HD:(h + 1) * HD]))
        y_s[pl.ds(r0, MQ), :] = jnp.concatenate(ydiag, axis=1)
        for d in range(2):
            bwt = jnp.concatenate([bt[(h // 2) * M_STATE:(h // 2 + 1) * M_STATE, :] * wgt[4 * d + h:4 * d + h + 1, :]
                                   for h in range(N_HEADS)], axis=0)
            u_s[d, c] = (_dot(bwt.astype(BF16), xb) * bdmask).astype(BF16)
            dec_s[d, c] = jnp.broadcast_to(_head_lane_vec(jnp.exp(tot), 4 * d), (V7X_SUBLANES, DG))
            tiles = []
            for g in range(2):
                wrows = jnp.where(upper_half, carry[4 * d + 2 * g + 1:4 * d + 2 * g + 2, :],
                                  carry[4 * d + 2 * g:4 * d + 2 * g + 1, :])
                tiles.append(c_dup[g] * wrows.T)
            cw_s[d, c] = jnp.concatenate(tiles, axis=1).astype(BF16)
        return 0

    lax.fori_loop(0, N_MCH, chunk, 0, unroll=4)

    st_s[...] = jnp.zeros(st_s.shape, F32)

    def scan(i, _):
        for d in range(2):
            c = i if d == 0 else N_MCH - 1 - i
            st = st_s[d]
            yo_s[d, pl.ds(pl.multiple_of(c * MQ, MQ), MQ), :] = _dot(cw_s[d, c], st.astype(BF16))
            st_s[d] = st * dec_s[d, c][0:1, :] + u_s[d, c].astype(F32)
        return 0

    lax.fori_loop(0, N_MCH, scan, 0, unroll=4)

    def finish(c, _):
        r0 = pl.multiple_of(c * CONV_ROWS, CONV_ROWS)
        rows = pl.ds(r0, CONV_ROWS)
        y = y_s[rows, :] + yo_s[0, rows, :] + yo_s[1, rows, :] + xs_s[rows, :] * dsk_ref[...]
        y = y * _silu(z_ref[0, rows, :].astype(F32))
        o_ref[0, pl.ds(r0, CONV_ROWS), :] = (_rms(y) * nw_ref[...]).astype(BF16)
        return 0

    lax.fori_loop(0, SEQ // CONV_ROWS, finish, 0, unroll=2)


def mamba2(z3, xbc3, dtc4, conv_w, conv_b, dt_bias_col, a_col, dskip_lanes, norm_w, tri_incl, bdmask):
    b = z3.shape[0]
    full = lambda a: pl.BlockSpec(a.shape, lambda i: (0,) * a.ndim)
    return pl.pallas_call(
        _mamba_body, grid=(b,),
        in_specs=[pl.BlockSpec((1, SEQ, DG), lambda i: (i, 0, 0)),
                  pl.BlockSpec((1, SEQ, 2 * DG), lambda i: (i, 0, 0)),
                  pl.BlockSpec((1, N_MCH, 8, MQ), lambda i: (i, 0, 0, 0)),
                  full(conv_w), full(conv_b), full(dt_bias_col), full(a_col), full(dskip_lanes), full(norm_w),
                  full(tri_incl), full(bdmask)],
        out_specs=pl.BlockSpec((1, SEQ, DG), lambda i: (i, 0, 0)),
        out_shape=jax.ShapeDtypeStruct((b, SEQ, DG), BF16),
        scratch_shapes=[pltpu.VMEM((SEQ + 2 * CONV_HALO, 2 * DG), F32),
                        pltpu.VMEM((SEQ, DG), F32),
                        pltpu.VMEM((SEQ, 2 * M_STATE), BF16),
                        pltpu.VMEM((SEQ, 2 * M_STATE), BF16),
                        pltpu.VMEM((SEQ, DG), F32),
                        pltpu.VMEM((2, N_MCH, DG, DG), BF16),
                        pltpu.VMEM((2, N_MCH, V7X_SUBLANES, DG), F32),
                        pltpu.VMEM((2, N_MCH, MQ, DG), BF16),
                        pltpu.VMEM((2, SEQ, DG), F32),
                        pltpu.VMEM((2, DG, DG), F32)],
        compiler_params=_params("parallel"), name="mamba2",
    )(z3, xbc3, dtc4, conv_w, conv_b, dt_bias_col, a_col, dskip_lanes, norm_w, tri_incl, bdmask)


A_TQ = 128
A_ROWS = 256
A_KW = A_TQ + 2 * A_BAND


def _attn_bias_body(ids_ref, rb_ref, o_ref):
    ids = ids_ref[0]
    for h in range(N_HEADS):
        acc = jnp.full(ids.shape, NEG_BIG, F32)
        for bkt in range(N_BUCKETS):
            acc = jnp.where(ids == bkt, rb_ref[bkt, h], acc)
        o_ref[h, 0] = acc


def attention_bias_table(bucket_ids, rel_bias):
    nvar, tq, w = bucket_ids.shape
    return pl.pallas_call(
        _attn_bias_body, grid=(nvar,),
        in_specs=[pl.BlockSpec((1, tq, w), lambda v: (v, 0, 0)),
                  pl.BlockSpec(memory_space=pltpu.SMEM)],
        out_specs=pl.BlockSpec((N_HEADS, 1, tq, w), lambda v: (0, v, 0, 0)),
        out_shape=jax.ShapeDtypeStruct((N_HEADS, nvar, tq, w), F32),
        compiler_params=_params("parallel"), name="attention_bias_table",
    )(bucket_ids, rel_bias)


A_SLABS = 3 * DG // V7X_LANES
A_QBLOCKS = SEQ // A_TQ


A_SUB4 = SEQ // 4
A_SUB16 = SEQ // 16


def _attn_body(at_ref, b1_ref, b4_ref, b16_ref, o_ref, qkv_s, x4_s, x16_s, y16_s, y4_s, part_o, part_l):
    def fill(c, _):
        r0 = pl.multiple_of(c * A_ROWS, A_ROWS)
        for s in range(A_SLABS):
            qkv_s[s, pl.ds(r0, A_ROWS), :] = at_ref[0, pl.ds(r0, A_ROWS), s * V7X_LANES:(s + 1) * V7X_LANES].astype(F32)
        return 0

    lax.fori_loop(0, SEQ // A_ROWS, fill, 0)

    def deinterleave(s, _):
        for r4 in range(4):
            for c in range(A_SUB4 // A_ROWS):
                x4_s[s, pl.ds(r4 * A_SUB4 + c * A_ROWS, A_ROWS), :] = \
                    qkv_s[s, pl.ds(r4 + 4 * c * A_ROWS, A_ROWS, stride=4), :]
        for r in range(16):
            x16_s[s, pl.ds(r * A_SUB16, A_SUB16), :] = \
                x4_s[s, pl.ds((r % 4) * A_SUB4 + r // 4, A_SUB16, stride=4), :].astype(BF16)
        return 0

    lax.fori_loop(0, A_SLABS, deinterleave, 0)
    first_head = lax.broadcasted_iota(I32, (A_TQ, V7X_LANES), 1) < HD

    def run_pattern(pat, dil, bias_ref):
        n = SEQ // dil if dil < 16 else SEQ
        nblk = n // A_TQ
        w = A_KW

        def block(it, _):
            r = it // nblk
            i = it - r * nblk
            q0 = i * A_TQ
            k0 = jnp.clip(q0 - A_BAND, 0, n - w)
            var = jnp.where(i == 0, 0, jnp.where(i == nblk - 1, 2, 1))
            for hp in range(2):
                lanes = [slice((2 * part + hp) * V7X_LANES, (2 * part + hp + 1) * V7X_LANES) for part in range(3)]
                if dil == 4:
                    qrows = pl.ds(r + dil * q0, A_TQ, stride=dil)
                    krows = pl.ds(r + dil * k0, w, stride=dil)
                    q2 = qkv_s[hp, qrows, :]
                    k2 = qkv_s[2 + hp, krows, :].astype(BF16)
                    v2 = qkv_s[4 + hp, krows, :].astype(BF16)
                else:
                    qrows = pl.ds(pl.multiple_of(q0, A_TQ), A_TQ)
                    krows = pl.ds(pl.multiple_of(k0, A_BAND), w)
                    if dil == 1:
                        q2, k2, v2 = at_ref[0, qrows, lanes[0]], at_ref[0, krows, lanes[1]], at_ref[0, krows, lanes[2]]
                    else:
                        q2, k2, v2 = x16_s[hp, qrows, :], x16_s[2 + hp, krows, :], x16_s[4 + hp, krows, :]
                q2 = (q2 * (HD ** -0.5)).astype(BF16)
                outs, lses = [], []
                for hh in range(2):
                    keep = first_head if hh == 0 else jnp.logical_not(first_head)
                    qm = jnp.where(keep, q2, jnp.zeros_like(q2))
                    s = lax.dot_general(qm, k2, _NT, preferred_element_type=F32) + bias_ref[2 * hp + hh, var]
                    m = jnp.max(s, axis=1, keepdims=True)
                    p = jnp.exp(s - m)
                    den = jnp.sum(p, axis=1, keepdims=True)
                    outs.append(_dot(p.astype(BF16), v2) / den)
                    lses.append(m + jnp.log(den))
                o_new = jnp.where(first_head, outs[0], outs[1])
                l_new = jnp.where(first_head, lses[0], lses[1])
                if dil == 16:
                    y16_s[0, hp, qrows, :] = o_new
                    y16_s[1, hp, qrows, :] = l_new
                else:
                    part_o[pat, hp, qrows, :] = o_new
                    part_l[pat, hp, qrows, :] = l_new
            return 0

        lax.fori_loop(0, A_QBLOCKS, block, 0, unroll=4)

    for pat, (dil, bias_ref) in enumerate(zip(A_DILS, (b1_ref, b4_ref, b16_ref))):
        run_pattern(pat, dil, bias_ref)

    for a, dst in enumerate((part_o, part_l)):
        for hp in range(2):
            for r in range(16):
                y4_s[a, hp, pl.ds((r % 4) * A_SUB4 + r // 4, A_SUB16, stride=4), :] = \
                    y16_s[a, hp, pl.ds(r * A_SUB16, A_SUB16), :]
            for r4 in range(4):
                for c in range(A_SUB4 // A_ROWS):
                    dst[2, hp, pl.ds(r4 + 4 * c * A_ROWS, A_ROWS, stride=4), :] = \
                        y4_s[a, hp, pl.ds(r4 * A_SUB4 + c * A_ROWS, A_ROWS), :]

    def finish(c, _):
        rows = pl.ds(pl.multiple_of(c * A_TQ, A_TQ), A_TQ)
        for hp in range(2):
            ls = [part_l[pat, hp, rows, :] for pat in range(len(A_DILS))]
            mx = jnp.maximum(jnp.maximum(ls[0], ls[1]), ls[2])
            ws = [jnp.exp(l - mx) for l in ls]
            num = ws[0] * part_o[0, hp, rows, :] + ws[1] * part_o[1, hp, rows, :] + ws[2] * part_o[2, hp, rows, :]
            o_ref[0, rows, hp * V7X_LANES:(hp + 1) * V7X_LANES] = (num / (ws[0] + ws[1] + ws[2])).astype(BF16)
        return 0

    lax.fori_loop(0, SEQ // A_TQ, finish, 0)


def dilated_attention(at3, bias1, bias4, bias16):
    b = at3.shape[0]
    full = lambda a: pl.BlockSpec(a.shape, lambda i: (0,) * a.ndim)
    return pl.pallas_call(
        _attn_body, grid=(b,),
        in_specs=[pl.BlockSpec((1, SEQ, 3 * DG), lambda i: (i, 0, 0)), full(bias1), full(bias4), full(bias16)],
        out_specs=pl.BlockSpec((1, SEQ, DG), lambda i: (i, 0, 0)),
        out_shape=jax.ShapeDtypeStruct((b, SEQ, DG), BF16),
        scratch_shapes=[pltpu.VMEM((A_SLABS, SEQ, V7X_LANES), F32),
                        pltpu.VMEM((A_SLABS, SEQ, V7X_LANES), F32),
                        pltpu.VMEM((A_SLABS, SEQ, V7X_LANES), BF16),
                        pltpu.VMEM((2, 2, SEQ, V7X_LANES), F32),
                        pltpu.VMEM((2, 2, SEQ, V7X_LANES), F32),
                        pltpu.VMEM((len(A_DILS), 2, SEQ, V7X_LANES), F32),
                        pltpu.VMEM((len(A_DILS), 2, SEQ, V7X_LANES), F32)],
        compiler_params=_params("parallel"), name="dilated_attention",
    )(at3, bias1, bias4, bias16)


H_BLK = 256
H_CPB = H_BLK // H_CHUNK
N_HBLK = SEQ // H_BLK
N_HCH = SEQ // H_CHUNK


def _chunk_bcast(x, row):
    c = x.shape[1]
    x3 = x.reshape(H_CPB, H_CHUNK, c)
    return jnp.broadcast_to(x3[:, row:row + 1, :], (H_CPB, H_CHUNK, c)).reshape(H_BLK, c)


def _hgrn_body(p_ref, lb_ref, nw_ref, tin_ref, o_ref, qm_s, ut_s, oi_s, dec_s, oe_s, st_s):
    li = lax.broadcasted_iota(I32, (H_BLK, H_BLK), 0)
    si = lax.broadcasted_iota(I32, (H_BLK, H_BLK), 1)
    same = (li // H_CHUNK) == (si // H_CHUNK)
    mask_f = same & (si <= li)
    mask_b = same & (si >= li)
    lane_head = lax.broadcasted_iota(I32, (1, DG), 1) // HD

    def block(bi, _):
        r0 = pl.multiple_of(bi * H_BLK, H_BLK)
        rows = pl.ds(r0, H_BLK)
        q = _silu(p_ref[0, rows, 0:DG].astype(F32))
        v = p_ref[0, rows, 3 * DG:4 * DG]
        scores = [None] * N_HEADS
        for d in range(2):
            fpre = p_ref[0, rows, (1 + d) * DG:(2 + d) * DG].astype(F32)
            lb = lb_ref[d:d + 1, :]
            sg = jax.nn.sigmoid(fpre)
            g = jnp.log(lb + (1.0 - lb) * sg)
            k = (1.0 - lb) * (1.0 - sg)
            gi = _dot01_2(tin_ref[...], g)
            glast = _chunk_bcast(gi, H_CHUNK - 1)
            if d == 0:
                gc = gi
                gref = _chunk_bcast(gi, H_CHUNK // 2 - 1)
                msk = mask_f
            else:
                gc = glast - gi + g
                gref = _chunk_bcast(gc, H_CHUNK // 2)
                msk = mask_b
            qe = (q * jnp.exp(gc - gref)).astype(BF16)
            ke = (k * jnp.exp(gref - gc)).astype(BF16)
            for h in range(N_HEADS):
                hs = slice(h * HD, (h + 1) * HD)
                sc = jnp.where(msk, lax.dot_general(qe[:, hs], ke[:, hs], _NT, preferred_element_type=F32), 0.0)
                scores[h] = sc if d == 0 else scores[h] + sc
            qd = q * jnp.exp(gc)
            kd = (k * jnp.exp(glast - gc)).astype(BF16)
            for j in range(H_CPB):
                c = bi * H_CPB + j
                cr = slice(j * H_CHUNK, (j + 1) * H_CHUNK)
                qm_s[d, c] = jnp.concatenate([jnp.where(lane_head == h, qd[cr, :], 0.0) for h in range(N_HEADS)],
                                             axis=0).astype(BF16)
                ut = lax.dot_general(v[cr, :], kd[cr, :], _TN, preferred_element_type=F32)
                packed = ut[0:HD, :]
                for h in range(1, N_HEADS):
                    packed = jnp.where(lane_head == h, ut[h * HD:(h + 1) * HD, :], packed)
                ut_s[d, c] = packed.astype(BF16)
                dec_s[d, c] = jnp.broadcast_to(jnp.exp(glast[j * H_CHUNK:j * H_CHUNK + 1, :]), (V7X_SUBLANES, DG))
        for h in range(N_HEADS):
            oi_s[h, rows, :] = _dot(scores[h].astype(BF16), v[:, h * HD:(h + 1) * HD])
        return 0

    lax.fori_loop(0, N_HBLK, block, 0)

    st_s[...] = jnp.zeros(st_s.shape, F32)

    def step(i, _):
        for d in range(2):
            c = i if d == 0 else N_HCH - 1 - i
            rows = pl.ds(pl.multiple_of(c * H_CHUNK, H_CHUNK), H_CHUNK)
            st = st_s[d]
            inter = lax.dot_general(qm_s[d, c], st.astype(BF16), _NT, preferred_element_type=F32)
            for h in range(N_HEADS):
                oe_s[d, h, rows, :] = inter[h * H_CHUNK:(h + 1) * H_CHUNK, :]
            st_s[d] = st * dec_s[d, c][0:1, :] + ut_s[d, c].astype(F32)
        return 0

    lax.fori_loop(0, N_HCH, step, 0, unroll=8)

    def finish(c, _):
        r0 = pl.multiple_of(c * CONV_ROWS, CONV_ROWS)
        rows = pl.ds(r0, CONV_ROWS)
        gate = _silu(p_ref[0, rows, 4 * DG:5 * DG].astype(F32))
        outs = [_rms(oi_s[h, rows, :] + oe_s[0, h, rows, :] + oe_s[1, h, rows, :]) for h in range(N_HEADS)]
        o_ref[0, rows, :] = (jnp.concatenate(outs, axis=1) * nw_ref[...] * gate).astype(BF16)
        return 0

    lax.fori_loop(0, SEQ // CONV_ROWS, finish, 0, unroll=2)


def hgrn2(p3, lb2, norm_w_lanes, tri_in_chunk):
    b = p3.shape[0]
    full = lambda a: pl.BlockSpec(a.shape, lambda i: (0,) * a.ndim)
    return pl.pallas_call(
        _hgrn_body, grid=(b,),
        in_specs=[pl.BlockSpec((1, SEQ, 5 * DG), lambda i: (i, 0, 0)), full(lb2), full(norm_w_lanes),
                  full(tri_in_chunk)],
        out_specs=pl.BlockSpec((1, SEQ, DG), lambda i: (i, 0, 0)),
        out_shape=jax.ShapeDtypeStruct((b, SEQ, DG), BF16),
        scratch_shapes=[pltpu.VMEM((2, N_HCH, N_HEADS * H_CHUNK, DG), BF16),
                        pltpu.VMEM((2, N_HCH, HD, DG), BF16),
                        pltpu.VMEM((N_HEADS, SEQ, HD), F32),
                        pltpu.VMEM((2, N_HCH, V7X_SUBLANES, DG), F32),
                        pltpu.VMEM((2, N_HEADS, SEQ, HD), F32),
                        pltpu.VMEM((2, HD, DG), F32)],
        compiler_params=_params("parallel"), name="hgrn2",
    )(p3, lb2, norm_w_lanes, tri_in_chunk)


@functools.lru_cache(maxsize=None)
def _tables():
    t = {}
    k = np.arange(SEQ, dtype=np.int64)
    ang = 2.0 * np.pi * ((k[:, None] * k[None, :]) % NFFT).astype(np.float64) / NFFT
    t["cos"] = np.cos(ang).astype(np.float32)
    t["sin"] = np.sin(ang).astype(np.float32)
    rows_f32 = np.concatenate([t["cos"], t["sin"]], axis=0)
    t["dft_rows"] = rows_f32.astype(ml_dtypes.bfloat16)
    t["dft_rows_lo"] = (rows_f32 - t["dft_rows"].astype(np.float32)).astype(ml_dtypes.bfloat16)
    t["dft_cols"] = np.concatenate([t["cos"], t["sin"]], axis=1).astype(ml_dtypes.bfloat16)
    tt = np.linspace(0.0, 1.0, SEQ, dtype=np.float32)[:, None]
    bands = (HY_POS_DIM - 1) // 2
    ang_pos = (2.0 * math.pi * np.arange(SEQ, dtype=np.float32) / SEQ).astype(np.float32)
    f = np.linspace(1e-4, bands - 1, bands, dtype=np.float32)
    a2 = (ang_pos[:, None] * f[None, :]).astype(np.float32)
    z = np.concatenate([tt, np.cos(a2), -np.sin(a2)], axis=-1).astype(np.float32)
    zp = np.zeros((SEQ, V7X_LANES), np.float32)
    zp[:, :HY_POS_DIM] = z
    t["zpos"] = zp
    max_decay = math.log(1e-2) / 0.3
    min_decay = math.log(1e-2) / 1.5
    deltas = np.abs(np.linspace(min_decay, max_decay, DG, dtype=np.float32))
    t["decay"] = np.exp(-tt * deltas[None, :]).astype(np.float32)
    i128 = np.arange(V7X_LANES)
    t["u128"] = (i128[:, None] < i128[None, :]).astype(np.float32)
    im = np.arange(M_CHUNK)
    t["tri_incl"] = (im[:, None] <= im[None, :]).astype(np.float32)
    ib = np.arange(H_BLK)
    t["tri_in_chunk"] = ((ib[:, None] // H_CHUNK == ib[None, :] // H_CHUNK)
                         & (ib[None, :] <= ib[:, None])).astype(np.float32)
    idg = np.arange(DG)
    t["bdmask"] = (idg[:, None] // HD == idg[None, :] // HD).astype(np.float32)
    def bucket(rel):
        nb = N_BUCKETS // 2
        max_exact = nb // 2
        ret = (rel > 0).astype(np.int64) * nb
        n = np.abs(rel)
        nf = np.maximum(n, 1).astype(np.float64)
        large = max_exact + (np.log(nf / max_exact) / math.log(MAX_DISTANCE / max_exact)
                             * (nb - max_exact)).astype(np.int64)
        large = np.minimum(large, nb - 1)
        return ret + np.where(n < max_exact, n, large)

    for dil in A_DILS:
        n = SEQ // dil
        qi = np.arange(A_TQ)[:, None]
        kj = np.arange(A_KW)[None, :]
        ids = []
        for s0 in (0, -A_BAND, -(A_KW - A_TQ)):
            kk = kj + s0
            rel = kk - qi
            ok = np.abs(rel) <= A_BAND
            if n == A_TQ:
                ok &= (kk >= 0) & (kk < A_TQ)
            ids.append(np.where(ok, bucket(rel * dil), -1))
        t[f"bucket{dil}"] = np.stack(ids).astype(np.int32)
    return t


def kernel(x, w_in, w_out, norm_mix_w, norm_ffn_w, hy_conv_w, hy_pos_w1, hy_pos_b1, hy_pos_w2, hy_pos_b2,
           hy_sin_freq, hy_pos_w3, hy_filt_bias, m_conv_w, m_conv_b, m_dt_bias, m_A_log, m_D, m_norm_w, rel_bias,
           hg_lb, hg_norm_w, router_w, moe_w_gate, moe_w_up, moe_w_down, final_norm_w):
    b = x.shape[0]
    assert x.shape[1:] == (SEQ, D_MODEL) and b % MOE_FFN_SEQS == 0 and b % ROUTER_SEQS == 0, x.shape
    t = b * SEQ
    tb = _tables()
    dft_rows_lo = jnp.asarray(tb["dft_rows_lo"])
    dft_rows = jnp.asarray(tb["dft_rows"])
    dft_cols = jnp.asarray(tb["dft_cols"])
    u128 = jnp.asarray(tb["u128"]).astype(BF16)
    tri_incl = jnp.asarray(tb["tri_incl"]).astype(BF16)
    tri_in_chunk = jnp.asarray(tb["tri_in_chunk"]).astype(BF16)
    bdmask = jnp.asarray(tb["bdmask"])
    attn_bias = [attention_bias_table(jnp.asarray(tb[f"bucket{d}"]), rel_bias.astype(F32)) for d in A_DILS]

    sm = jax.nn.softmax(hg_lb.astype(F32), axis=0)
    lower_bounds = jnp.cumsum(sm, axis=0) - sm[:1]

    xa = x.reshape(t, D_MODEL)
    for l in range(DEPTH):
        wl = w_in[l]
        w_main = jnp.concatenate([wl[:, 0:768], wl[:, 768:1024], wl[:, 1024:1536], wl[:, 1544:2312],
                                  wl[:, 2312:3592]], axis=1).astype(BF16)
        w_dt_rows = wl[:, 1536:1544].T.astype(BF16)
        z, x0, mz, mx, at, hg, dtc = in_projection(xa, norm_mix_w[l][None, :], w_main, w_dt_rows, hy_conv_w[l])

        w1p = jnp.zeros((V7X_LANES, HY_HID), F32).at[:HY_POS_DIM].set(hy_pos_w1[l])
        kr, ki, kny = hyena_filter_spectrum(
            jnp.asarray(tb["zpos"]), w1p, hy_pos_b1[l][None, :], hy_pos_w2[l], hy_pos_b2[l][None, :],
            hy_sin_freq[l][None, :], hy_pos_w3[l], jnp.asarray(tb["decay"]), dft_rows, dft_rows_lo)
        z3, x03 = z.reshape(b, SEQ, DG), x0.reshape(b, SEQ, DG)
        ya = hyena_conv(z3, x03, dft_rows, dft_cols, kr, ki, kny, hy_filt_bias[l][None, :]).reshape(t, DG)

        a_col = (-jnp.exp(m_A_log[l].astype(F32))).reshape(8, 1)
        yb = mamba2(mz.reshape(b, SEQ, DG), mx.reshape(b, SEQ, 2 * DG), dtc.reshape(b, N_MCH, 8, MQ),
                    m_conv_w[l], m_conv_b[l][None, :], m_dt_bias[l].reshape(8, 1), a_col,
                    jnp.repeat(m_D[l].astype(F32), HD)[None, :], m_norm_w[l][None, :], tri_incl, bdmask).reshape(t, DG)

        yc = dilated_attention(at.reshape(b, SEQ, 3 * DG), *attn_bias).reshape(t, DG)

        lbl = lower_bounds[l]
        yd = hgrn2(hg.reshape(b, SEQ, 5 * DG), lbl, jnp.tile(hg_norm_w[l], N_HEADS)[None, :],
                   tri_in_chunk).reshape(t, DG)

        rw_rows = router_w[l].T.astype(F32)
        rw_hi = rw_rows.astype(BF16)
        rw_lo = (rw_rows - rw_hi.astype(F32)).astype(BF16)
        xo, xn, logits = out_projection(xa, ya, yb, yc, yd, w_out[l].reshape(4, DG, D_MODEL).astype(BF16),
                                        norm_ffn_w[l][None, :], rw_hi, rw_lo)
        xn3 = xn.reshape(b, SEQ, D_MODEL)
        rank, gate, seg = router(logits, u128)
        seg_flat = seg[:, :, :MOE_SEG_STRIDE].reshape(-1)
        xe = moe_gather(seg_flat, xn3, rank)
        ye = moe_experts(xe, moe_w_gate, moe_w_up, moe_w_down, l)
        xa = moe_scatter(seg_flat, ye, rank, gate, xo.reshape(b, SEQ, D_MODEL), final_norm_w[None, :],
                         final=(l == DEPTH - 1)).reshape(t, D_MODEL)
    return xa.reshape(b, SEQ, D_MODEL)
```

```python
import functools
import math

import ml_dtypes
import numpy as np
import jax
import jax.numpy as jnp
from jax import lax
from jax.experimental import pallas as pl
from jax.experimental.pallas import tpu as pltpu

F32 = jnp.float32
BF16 = jnp.bfloat16
I32 = jnp.int32

D_MODEL = 1024
SEQ = 2048
DEPTH = 2
DG = 256
N_HEADS = 4
HD = 64
HY_POS_DIM = 33
HY_HID = 64
M_CONV = 5
M_STATE = 64
M_CHUNK = 128
H_CHUNK = 32
A_BAND = 64
A_DILS = (1, 4, 16)
N_BUCKETS = 32
MAX_DISTANCE = 1024
N_EXPERTS = 16
CAP = 2 * SEQ // N_EXPERTS
D_FF = 1024
EPS = 1e-6
NFFT = 2 * SEQ

V7X_LANES = 128
V7X_SUBLANES = 8
V7X_VMEM_LIMIT_BYTES = 56 * 1024 * 1024

NEG_BIG = -1e30

_NT = (((1,), (1,)), ((), ()))
_TN = (((0,), (0,)), ((), ()))


def _params(*sem):
    return pltpu.CompilerParams(dimension_semantics=sem, vmem_limit_bytes=V7X_VMEM_LIMIT_BYTES)


def _dot(a, b):
    return jnp.dot(a, b, preferred_element_type=F32)


def _dot_hi(a, b):
    return jnp.dot(a, b, preferred_element_type=F32, precision=lax.Precision.HIGHEST)


def _dot01_2(t_bf16, x):
    x1 = x.astype(BF16)
    x2 = (x - x1.astype(F32)).astype(BF16)
    return _dot(t_bf16, x1) + _dot(t_bf16, x2)


def _dot01_rhs(x, t_bf16):
    x1 = x.astype(BF16)
    r1 = x - x1.astype(F32)
    x2 = r1.astype(BF16)
    x3 = (r1 - x2.astype(F32)).astype(BF16)
    return _dot(x1, t_bf16) + _dot(x2, t_bf16) + _dot(x3, t_bf16)


def _silu(x):
    return x * jax.nn.sigmoid(x)


def _softplus(x):
    return jnp.maximum(x, 0.0) + jnp.log(1.0 + jnp.exp(-jnp.abs(x)))


def _rms(x):
    return x * lax.rsqrt(jnp.mean(x * x, axis=-1, keepdims=True) + EPS)


TM_PROJ = 1024
_HY0, _MZ0, _MX0, _AT0, _HG0, _PEND = 0, 768, 1024, 1536, 2304, 3584


TILES_PER_SEQ = SEQ // TM_PROJ
PROJ_HALO = 16


def _inproj_body(x_ref, xp_ref, xq_ref, nw_ref, w_ref, wdt_ref, hcw_ref, z_ref, x0_ref, mz_ref, mx_ref, at_ref, hg_ref,
                 dtc_ref):
    norm = lambda v: (_rms(v) * nw_ref[...]).astype(BF16)
    hn = norm(x_ref[...])
    tile = pl.program_id(0) % TILES_PER_SEQ
    zero = jnp.zeros((PROJ_HALO, D_MODEL), BF16)
    h_ext = jnp.concatenate([jnp.where(tile == 0, zero, norm(xp_ref[...])), hn,
                             jnp.where(tile == TILES_PER_SEQ - 1, zero, norm(xq_ref[...]))], axis=0)
    p = _dot(h_ext, w_ref[:, _HY0:_MZ0])
    rows_ext = TM_PROJ + 2 * PROJ_HALO
    mid = slice(PROJ_HALO, PROJ_HALO + TM_PROJ)
    u = (pltpu.roll(p, 1, 0)[mid] * hcw_ref[0:1, :] + p[mid] * hcw_ref[1:2, :]
         + pltpu.roll(p, rows_ext - 1, 0)[mid] * hcw_ref[2:3, :])
    x0_ref[...] = u[:, 0:DG].astype(BF16)
    z_ref[...] = (u[:, 2 * DG:3 * DG] * u[:, DG:2 * DG]).astype(BF16)
    mz_ref[...] = _dot(hn, w_ref[:, _MZ0:_MX0]).astype(BF16)
    mx_ref[...] = _dot(hn, w_ref[:, _MX0:_AT0]).astype(BF16)
    at_ref[...] = _dot(hn, w_ref[:, _AT0:_HG0]).astype(BF16)
    hg_ref[...] = _dot(hn, w_ref[:, _HG0:_PEND]).astype(BF16)
    dt_rows = lax.dot_general(wdt_ref[...], hn, _NT, preferred_element_type=F32)
    for j in range(TM_PROJ // M_CHUNK):
        dtc_ref[j] = dt_rows[:, j * M_CHUNK:(j + 1) * M_CHUNK]


def in_projection(x, norm_w, w_main, w_dt_rows, hy_conv_w):
    t = x.shape[0]
    tm = TM_PROJ
    halo = tm // PROJ_HALO
    row = lambda w: pl.BlockSpec((tm, w), lambda i: (i, 0))
    full = lambda a: pl.BlockSpec(a.shape, lambda i: (0,) * a.ndim)
    in_specs = [row(D_MODEL),
                pl.BlockSpec((PROJ_HALO, D_MODEL), lambda i: (jnp.maximum(i * halo - 1, 0), 0)),
                pl.BlockSpec((PROJ_HALO, D_MODEL), lambda i: (jnp.minimum((i + 1) * halo, t // PROJ_HALO - 1), 0)),
                full(norm_w), full(w_main), full(w_dt_rows), full(hy_conv_w)]
    widths = (256, 256, 256, 512, 768, 1280)
    out_shape = [jax.ShapeDtypeStruct((t, w), BF16) for w in widths]
    out_shape.append(jax.ShapeDtypeStruct((t // M_CHUNK, 8, M_CHUNK), F32))
    out_specs = [row(w) for w in widths] + [pl.BlockSpec((tm // M_CHUNK, 8, M_CHUNK), lambda i: (i, 0, 0))]
    return pl.pallas_call(
        _inproj_body, grid=(t // tm,), in_specs=in_specs, out_specs=out_specs, out_shape=out_shape,
        compiler_params=_params("parallel"), name="in_projection",
    )(x, x, x, norm_w, w_main, w_dt_rows, hy_conv_w)


TM_OUT = 1024


def _outproj_body(x_ref, ya_ref, yb_ref, yc_ref, yd_ref, w_ref, nw_ref, rwh_ref, rwl_ref, xo_ref, xn_ref, lg_ref):
    x = x_ref[...]
    acc = x + _dot(ya_ref[...], w_ref[0]) + _dot(yb_ref[...], w_ref[1])
    acc = acc + _dot(yc_ref[...], w_ref[2]) + _dot(yd_ref[...], w_ref[3])
    xo_ref[...] = acc
    xn = _rms(acc) * nw_ref[...]
    xh = xn.astype(BF16)
    xn_ref[...] = xh
    xl = (xn - xh.astype(F32)).astype(BF16)
    nt = lambda w, a: lax.dot_general(w, a, _NT, preferred_element_type=F32)
    lg_ref[...] = nt(rwh_ref[...], xh) + nt(rwh_ref[...], xl) + nt(rwl_ref[...], xh)


def out_projection(x, ya, yb, yc, yd, w_out4, norm_w, rw_hi, rw_lo):
    t = x.shape[0]
    tm = TM_OUT
    row = lambda w: pl.BlockSpec((tm, w), lambda i: (i, 0))
    full = lambda a: pl.BlockSpec(a.shape, lambda i: (0,) * a.ndim)
    in_specs = [row(D_MODEL)] + [row(DG)] * 4 + [full(w_out4), full(norm_w), full(rw_hi), full(rw_lo)]
    return pl.pallas_call(
        _outproj_body, grid=(t // tm,), in_specs=in_specs,
        out_specs=[row(D_MODEL), row(D_MODEL), pl.BlockSpec((N_EXPERTS, tm), lambda i: (0, i))],
        out_shape=[jax.ShapeDtypeStruct((t, D_MODEL), F32), jax.ShapeDtypeStruct((t, D_MODEL), BF16),
                   jax.ShapeDtypeStruct((N_EXPERTS, t), F32)],
        compiler_params=_params("parallel"), name="out_projection",
    )(x, ya, yb, yc, yd, w_out4, norm_w, rw_hi, rw_lo)


def _prefix_excl_lanes(mask_f32, u_ref):
    e = mask_f32.shape[0]
    off = jnp.zeros((e, 1), F32)
    parts, bounds = [], [off]
    for k in range(SEQ // V7X_LANES):
        tile = mask_f32[:, k * V7X_LANES:(k + 1) * V7X_LANES]
        parts.append(_dot(tile.astype(BF16), u_ref[...]) + off)
        off = off + jnp.sum(tile, axis=1, keepdims=True)
        bounds.append(off)
    return jnp.concatenate(parts, axis=1), bounds


ROUTER_SEQS = 2


def _router_body(lg_ref, u_ref, rank_ref, gate_ref, seg_ref):
    affs = []
    for q in range(ROUTER_SEQS):
        logits = lg_ref[:, q * SEQ:(q + 1) * SEQ]
        ex = jnp.exp(logits - jnp.max(logits, axis=0, keepdims=True))
        affs.append(ex / jnp.sum(ex, axis=0, keepdims=True))
    aff = jnp.concatenate(affs, axis=0)
    nrow = ROUTER_SEQS * N_EXPERTS
    bits = pltpu.bitcast(aff, I32)

    def search(i, thr):
        cand = thr | jnp.left_shift(jnp.int32(1), 30 - i)
        cnt = jnp.sum((bits >= cand).astype(I32), axis=1, keepdims=True)
        return jnp.where(cnt >= CAP, cand, thr)

    thr = lax.fori_loop(0, 31, search, jnp.zeros((nrow, 1), I32))
    gt = (bits > thr).astype(F32)
    eq = (bits == thr).astype(F32)
    need = CAP - jnp.sum(gt, axis=1, keepdims=True)
    tie_rank, _ = _prefix_excl_lanes(eq, u_ref)
    sel = gt + eq * (tie_rank < need).astype(F32)
    rank, bounds = _prefix_excl_lanes(sel, u_ref)
    rank = jnp.where(sel > 0.0, rank, -1.0)
    lane = lax.broadcasted_iota(I32, (nrow, V7X_LANES), 1)
    seg = jnp.zeros((nrow, V7X_LANES), F32)
    tiles = MOE_SEG // V7X_LANES
    for sgm in range(N_MOE_SEG):
        first = jnp.floor(bounds[sgm * tiles] * (1.0 / MOE_ALIGN)) * MOE_ALIGN
        need = jnp.floor((bounds[(sgm + 1) * tiles] - first + (MOE_TILE - 1)) * (1.0 / MOE_TILE))
        need = jnp.max(need.reshape(ROUTER_SEQS, N_EXPERTS, 1), axis=1, keepdims=True)
        need = jnp.broadcast_to(need, (ROUTER_SEQS, N_EXPERTS, 1)).reshape(nrow, 1)
        seg = jnp.where(lane == sgm, first, seg)
        seg = jnp.where(lane == N_MOE_SEG + sgm, need, seg)
    seg = seg.astype(I32)
    for q in range(ROUTER_SEQS):
        rows = slice(q * N_EXPERTS, (q + 1) * N_EXPERTS)
        rank_ref[q] = rank[rows]
        gate_ref[q] = aff[rows]
        seg_ref[q] = seg[rows]


def router(logits_et, u128):
    b = logits_et.shape[1] // SEQ
    out = jax.ShapeDtypeStruct((b, N_EXPERTS, SEQ), F32)
    return pl.pallas_call(
        _router_body, grid=(b // ROUTER_SEQS,),
        in_specs=[pl.BlockSpec((N_EXPERTS, ROUTER_SEQS * SEQ), lambda i: (0, i)),
                  pl.BlockSpec(u128.shape, lambda i: (0, 0))],
        out_specs=[pl.BlockSpec((ROUTER_SEQS, N_EXPERTS, SEQ), lambda i: (i, 0, 0))] * 2
                  + [pl.BlockSpec((ROUTER_SEQS, N_EXPERTS, V7X_LANES), lambda i: (i, 0, 0))],
        out_shape=[out, out, jax.ShapeDtypeStruct((b, N_EXPERTS, V7X_LANES), I32)],
        compiler_params=_params("parallel"), name="router",
    )(logits_et, u128)


MOE_SEG = 256
N_MOE_SEG = SEQ // MOE_SEG
MOE_TILE = 64
MOE_ALIGN = 16
MOE_GROUP = 4
MOE_SEG_STRIDE = 16
MOE_FFN_SEQS = 4


def _moe_seg_plan(cs_ref, b, s):
    base = b * N_EXPERTS * MOE_SEG_STRIDE
    starts = [cs_ref[base + ex * MOE_SEG_STRIDE + s] for ex in range(N_EXPERTS)]
    return starts, cs_ref[base + N_MOE_SEG + s]


def _moe_tile_bases(starts, r):
    own = [st + r * MOE_TILE for st in starts]
    return [pl.multiple_of(jnp.minimum(o, CAP - MOE_TILE), MOE_ALIGN) for o in own], own


def _moe_onehot_group(rank_ref, gate_ref, s, bases, own, grp):
    lanes = pl.ds(pl.multiple_of(s * MOE_SEG, MOE_SEG), MOE_SEG)
    j = lax.broadcasted_iota(I32, (MOE_TILE, MOE_SEG), 0)
    rows = []
    for ex in grp:
        slot = bases[ex] + j
        hit = (rank_ref[0, ex:ex + 1, lanes] == slot.astype(F32)) & (slot >= own[ex])
        val = 1.0 if gate_ref is None else gate_ref[0, ex:ex + 1, lanes]
        rows.append(jnp.where(hit, val, 0.0).astype(BF16))
    return jnp.concatenate(rows, axis=0)


_MOE_GROUPS = [list(range(g * MOE_GROUP, (g + 1) * MOE_GROUP)) for g in range(N_EXPERTS // MOE_GROUP)]


def _moe_gather_body(cs_ref, xn_ref, rank_ref, xe_ref):
    b = pl.program_id(0)

    def zero(ex, _):
        xe_ref[0, ex] = jnp.zeros((CAP, D_MODEL), BF16)
        return 0

    lax.fori_loop(0, N_EXPERTS, zero, 0)

    def seg_gather(s, _):
        starts, rounds = _moe_seg_plan(cs_ref, b, s)
        xn_seg = xn_ref[0, pl.ds(pl.multiple_of(s * MOE_SEG, MOE_SEG), MOE_SEG), :]

        def one_round(r, _):
            bases, own = _moe_tile_bases(starts, r)
            for grp in _MOE_GROUPS:
                got = _dot(_moe_onehot_group(rank_ref, None, s, bases, own, grp), xn_seg)
                for k, ex in enumerate(grp):
                    rows = pl.ds(bases[ex], MOE_TILE)
                    xe_ref[0, ex, rows, :] += got[k * MOE_TILE:(k + 1) * MOE_TILE].astype(BF16)
            return 0

        lax.fori_loop(0, rounds, one_round, 0)
        return 0

    lax.fori_loop(0, N_MOE_SEG, seg_gather, 0)


def moe_gather(seg_counts_flat, xn3, rank3):
    b = xn3.shape[0]
    grid_spec = pltpu.PrefetchScalarGridSpec(
        num_scalar_prefetch=1, grid=(b,),
        in_specs=[pl.BlockSpec((1, SEQ, D_MODEL), lambda i, cs: (i, 0, 0)),
                  pl.BlockSpec((1, N_EXPERTS, SEQ), lambda i, cs: (i, 0, 0))],
        out_specs=pl.BlockSpec((1, N_EXPERTS, CAP, D_MODEL), lambda i, cs: (i, 0, 0, 0)))
    return pl.pallas_call(
        _moe_gather_body, grid_spec=grid_spec,
        out_shape=jax.ShapeDtypeStruct((b, N_EXPERTS, CAP, D_MODEL), BF16),
        compiler_params=_params("parallel"), name="moe_gather",
    )(seg_counts_flat, xn3, rank3)


def _moe_experts_body(xe_ref, wg_ref, wu_ref, wd_ref, ye_ref, wg_s, wu_s, wd_s):
    @pl.when(pl.program_id(1) == 0)
    def _():
        wg_s[...] = wg_ref[0, 0].astype(BF16)
        wu_s[...] = wu_ref[0, 0].astype(BF16)
        wd_s[...] = wd_ref[0, 0].astype(BF16)

    xe = xe_ref[...].reshape(MOE_FFN_SEQS * CAP, D_MODEL)
    hid = (_silu(_dot(xe, wg_s[...])) * _dot(xe, wu_s[...])).astype(BF16)
    ye_ref[...] = _dot(hid, wd_s[...]).astype(BF16).reshape(MOE_FFN_SEQS, 1, CAP, D_MODEL)


def moe_experts(xe4, w_gate, w_up, w_down, layer):
    b = xe4.shape[0]
    blk = pl.BlockSpec((MOE_FFN_SEQS, 1, CAP, D_MODEL), lambda e, g: (g, e, 0, 0))
    w_spec = lambda a: pl.BlockSpec((1, 1) + a.shape[2:], lambda e, g: (layer, e, 0, 0))
    return pl.pallas_call(
        _moe_experts_body, grid=(N_EXPERTS, b // MOE_FFN_SEQS),
        in_specs=[blk, w_spec(w_gate), w_spec(w_up), w_spec(w_down)],
        out_specs=blk, out_shape=jax.ShapeDtypeStruct(xe4.shape, BF16),
        scratch_shapes=[pltpu.VMEM((D_MODEL, D_FF), BF16), pltpu.VMEM((D_MODEL, D_FF), BF16),
                        pltpu.VMEM((D_FF, D_MODEL), BF16)],
        compiler_params=_params("parallel", "arbitrary"), name="moe_experts",
    )(xe4, w_gate, w_up, w_down)


MOE_SCATTER_SEGS = 4


def _moe_scatter_body(final, cs_ref, ye_ref, rank_ref, gate_ref, xo_ref, nw_ref, o_ref):
    b = pl.program_id(0)
    half = pl.program_id(1)

    def seg_scatter(k, _):
        s = half * MOE_SCATTER_SEGS + k
        starts, rounds = _moe_seg_plan(cs_ref, b, s)
        tok = pl.ds(pl.multiple_of(k * MOE_SEG, MOE_SEG), MOE_SEG)
        o_ref[0, tok, :] = xo_ref[0, tok, :]

        def one_round(r, _):
            bases, own = _moe_tile_bases(starts, r)
            for grp in _MOE_GROUPS:
                ye = jnp.concatenate([ye_ref[0, ex, pl.ds(bases[ex], MOE_TILE), :] for ex in grp], axis=0)
                o_ref[0, tok, :] += lax.dot_general(_moe_onehot_group(rank_ref, gate_ref, s, bases, own, grp), ye,
                                                    _TN, preferred_element_type=F32)
            return 0

        lax.fori_loop(0, rounds, one_round, 0)
        if final:
            o_ref[0, tok, :] = _rms(o_ref[0, tok, :]) * nw_ref[...]
        return 0

    lax.fori_loop(0, MOE_SCATTER_SEGS, seg_scatter, 0)


def moe_scatter(seg_counts_flat, ye4, rank3, gate3, xo3, final_norm_w, final):
    b = ye4.shape[0]
    rows = MOE_SCATTER_SEGS * MOE_SEG
    sel_spec = pl.BlockSpec((1, N_EXPERTS, SEQ), lambda i, j, cs: (i, 0, 0))
    tok_spec = pl.BlockSpec((1, rows, D_MODEL), lambda i, j, cs: (i, j, 0))
    grid_spec = pltpu.PrefetchScalarGridSpec(
        num_scalar_prefetch=1, grid=(b, SEQ // rows),
        in_specs=[pl.BlockSpec((1, N_EXPERTS, CAP, D_MODEL), lambda i, j, cs: (i, 0, 0, 0)), sel_spec, sel_spec,
                  tok_spec, pl.BlockSpec(final_norm_w.shape, lambda i, j, cs: (0, 0))],
        out_specs=tok_spec)
    return pl.pallas_call(
        functools.partial(_moe_scatter_body, final), grid_spec=grid_spec,
        out_shape=jax.ShapeDtypeStruct((b, SEQ, D_MODEL), F32),
        compiler_params=_params("parallel", "arbitrary"), name="moe_scatter",
    )(seg_counts_flat, ye4, rank3, gate3, xo3, final_norm_w)


HY_KB = 256
HY_ROWS = 256


def _split_bf16(x):
    hi = x.astype(BF16)
    return hi, (x - hi.astype(F32)).astype(BF16)


def _hy_filter_body(z_ref, w1_ref, b1_ref, w2_ref, b2_ref, fr_ref, w3_ref, dec_ref, ch_ref, cl_ref, sh_ref, sl_ref,
                    kr_ref, ki_ref, kny_ref, ah_s, al_s, dh_s, dl_s):
    @pl.when(pl.program_id(0) == 0)
    def _():
        def rows(c, kny):
            r0 = pl.multiple_of(c * HY_ROWS, HY_ROWS)
            fr = fr_ref[...]
            h = jnp.sin(fr * (_dot_hi(z_ref[pl.ds(r0, HY_ROWS), :], w1_ref[...]) + b1_ref[...]))
            h = jnp.sin(fr * (_dot_hi(h, w2_ref[...]) + b2_ref[...]))
            h = _dot_hi(h, w3_ref[...])
            dec = dec_ref[pl.ds(r0, HY_ROWS), :]
            pos = r0 + lax.broadcasted_iota(I32, (HY_ROWS, DG), 0)
            hf = h[:, :DG] * dec
            hb = jnp.where(pos == 0, 0.0, h[:, DG:] * dec)
            a = hf + hb
            ah_s[pl.ds(r0, HY_ROWS), :], al_s[pl.ds(r0, HY_ROWS), :] = _split_bf16(a)
            dh_s[pl.ds(r0, HY_ROWS), :], dl_s[pl.ds(r0, HY_ROWS), :] = _split_bf16(hf - hb)
            sgn = (1 - 2 * (pos & 1)).astype(F32)
            return kny + jnp.sum(a * sgn, axis=0, keepdims=True)

        kny = lax.fori_loop(0, SEQ // HY_ROWS, rows, jnp.zeros((1, DG), F32))
        kny_ref[...] = jnp.broadcast_to(kny, kny_ref.shape)

    kr_ref[...] = _dot(ch_ref[...], ah_s[...]) + _dot(ch_ref[...], al_s[...]) + _dot(cl_ref[...], ah_s[...])
    ki_ref[...] = _dot(sh_ref[...], dh_s[...]) + _dot(sh_ref[...], dl_s[...]) + _dot(sl_ref[...], dh_s[...])


def hyena_filter_spectrum(zpos, w1, b1, w2, b2, freq, w3, decay, dft_rows, dft_rows_lo):
    full = lambda a: pl.BlockSpec(a.shape, lambda k: (0,) * a.ndim)
    kblk = pl.BlockSpec((HY_KB, SEQ), lambda k: (k, 0))
    sblk = pl.BlockSpec((HY_KB, SEQ), lambda k: (SEQ // HY_KB + k, 0))
    oblk = pl.BlockSpec((HY_KB, DG), lambda k: (k, 0))
    return pl.pallas_call(
        _hy_filter_body, grid=(SEQ // HY_KB,),
        in_specs=[full(zpos), full(w1), full(b1), full(w2), full(b2), full(freq), full(w3), full(decay),
                  kblk, kblk, sblk, sblk],
        out_specs=[oblk, oblk, pl.BlockSpec((V7X_SUBLANES, DG), lambda k: (0, 0))],
        out_shape=[jax.ShapeDtypeStruct((SEQ, DG), F32), jax.ShapeDtypeStruct((SEQ, DG), F32),
                   jax.ShapeDtypeStruct((V7X_SUBLANES, DG), F32)],
        scratch_shapes=[pltpu.VMEM((SEQ, DG), BF16)] * 4,
        compiler_params=_params("arbitrary"), name="hyena_filter",
    )(zpos, w1, b1, w2, b2, freq, w3, decay, dft_rows, dft_rows_lo, dft_rows, dft_rows_lo)


CONV_ROWS = 128
CONV_HALO = 8


def _dwconv_rows(pad_ref, w_ref, r0, lanes, k):
    n = CONV_ROWS + 2 * CONV_HALO
    win = pad_ref[pl.ds(r0, n), lanes]
    acc = None
    for j in range(k):
        sh = (k // 2 - j) % n
        rolled = win if sh == 0 else pltpu.roll(win, sh, 0)
        term = rolled[CONV_HALO:CONV_HALO + CONV_ROWS] * w_ref[j:j + 1, lanes]
        acc = term if acc is None else acc + term
    return acc


def _fill_padded(pad_ref, src_ref, width):
    zeros = jnp.zeros((CONV_HALO, width), F32)
    pad_ref[pl.ds(0, CONV_HALO), :] = zeros
    pad_ref[pl.ds(SEQ + CONV_HALO, CONV_HALO), :] = zeros

    def fill(c, _):
        r0 = pl.multiple_of(c * CONV_ROWS, CONV_ROWS)
        pad_ref[pl.ds(r0 + CONV_HALO, CONV_ROWS), :] = src_ref[0, pl.ds(r0, CONV_ROWS), :].astype(F32)
        return 0

    lax.fori_loop(0, SEQ // CONV_ROWS, fill, 0)


HY_FB = 512


def _hy_conv_body(z_ref, x0_ref, t1_ref, t2_ref, kr_ref, ki_ref, kny_ref, fb_ref, o_ref, y_s):
    z = z_ref[0]

    def spectrum(kb, _):
        rows = pl.ds(pl.multiple_of(kb * HY_FB, HY_FB), HY_FB)
        rows_s = pl.ds(pl.multiple_of(SEQ + kb * HY_FB, HY_FB), HY_FB)
        zr = _dot(t1_ref[rows, :], z)
        zi = _dot(t1_ref[rows_s, :], z)
        krow = kb * HY_FB + lax.broadcasted_iota(I32, (HY_FB, 1), 0)
        wk = jnp.where(krow == 0, 1.0 / NFFT, 2.0 / NFFT)
        kr = kr_ref[rows, :]
        ki = ki_ref[rows, :]
        y_s[rows, :] = ((zr * kr - zi * ki) * wk).astype(BF16)
        y_s[rows_s, :] = ((zr * ki + zi * kr) * wk).astype(BF16)
        return 0

    lax.fori_loop(0, SEQ // HY_FB, spectrum, 0)

    zny = jnp.sum(z.astype(F32) * (1 - 2 * (lax.broadcasted_iota(I32, (SEQ, DG), 0) & 1)).astype(F32),
                  axis=0, keepdims=True)
    nyq = zny * kny_ref[0:1, :] * (1.0 / NFFT)

    def synth(tb, _):
        rows = pl.ds(pl.multiple_of(tb * HY_FB, HY_FB), HY_FB)
        conv = _dot(t2_ref[rows, :], y_s[...])
        sgn = (1 - 2 * (lax.broadcasted_iota(I32, (HY_FB, DG), 0) & 1)).astype(F32)
        zf = z_ref[0, rows, :].astype(F32)
        o_ref[0, rows, :] = (x0_ref[0, rows, :].astype(F32) * (conv + nyq * sgn + zf * fb_ref[...])).astype(BF16)
        return 0

    lax.fori_loop(0, SEQ // HY_FB, synth, 0)


def hyena_conv(z3, x03, dft_rows, dft_cols, kr, ki, kny, fbias):
    b = z3.shape[0]
    seq_blk = pl.BlockSpec((1, SEQ, DG), lambda i: (i, 0, 0))
    once = lambda a: pl.BlockSpec(a.shape, lambda i: (0,) * a.ndim, pipeline_mode=pl.Buffered(1))
    return pl.pallas_call(
        _hy_conv_body, grid=(b,),
        in_specs=[seq_blk, seq_blk, once(dft_rows), once(dft_cols), once(kr), once(ki), once(kny), once(fbias)],
        out_specs=seq_blk, out_shape=jax.ShapeDtypeStruct((b, SEQ, DG), BF16),
        scratch_shapes=[pltpu.VMEM((2 * SEQ, DG), BF16)],
        compiler_params=_params("parallel"), name="hyena_conv",
    )(z3, x03, dft_rows, dft_cols, kr, ki, kny, fbias)


N_MCH = SEQ // M_CHUNK
MQ = M_CHUNK


def _head_lane_vec(rows8, base):
    lane_head = lax.broadcasted_iota(I32, (1, DG), 1) // HD
    out = jnp.zeros((1, DG), F32)
    for h in range(N_HEADS):
        out = jnp.where(lane_head == h, rows8[base + h:base + h + 1, :], out)
    return out


def _mamba_body(z_ref, xbc_ref, dtc_ref, cw_ref, cb_ref, dtb_ref, a_ref, dsk_ref, nw_ref, tri_ref, bd_ref,
                o_ref, pad, xs_s, b_s, c_s, y_s, u_s, dec_s, cw_s, yo_s, st_s):
    _fill_padded(pad, xbc_ref, 2 * DG)

    def conv_rows(c, _):
        r0 = pl.multiple_of(c * CONV_ROWS, CONV_ROWS)
        for g in range(4):
            lanes = slice(g * V7X_LANES, (g + 1) * V7X_LANES)
            u = _silu(_dwconv_rows(pad, cw_ref, r0, lanes, M_CONV) + cb_ref[:, lanes])
            if g < 2:
                xs_s[pl.ds(r0, CONV_ROWS), lanes] = u
            elif g == 2:
                b_s[pl.ds(r0, CONV_ROWS), :] = u.astype(BF16)
            else:
                c_s[pl.ds(r0, CONV_ROWS), :] = u.astype(BF16)
        return 0

    lax.fori_loop(0, SEQ // CONV_ROWS, conv_rows, 0)

    li = lax.broadcasted_iota(I32, (MQ, MQ), 0)
    si = lax.broadcasted_iota(I32, (MQ, MQ), 1)
    lower = si <= li
    upper = si >= li
    upper_half = li >= M_STATE
    first_group = si < M_STATE
    bdmask = bd_ref[...]

    def chunk(c, _):
        r0 = pl.multiple_of(c * MQ, MQ)
        dt = _softplus(dtc_ref[0, c] + dtb_ref[...])
        a = dt * a_ref[...]
        cum = _dot01_rhs(a, tri_ref[...])
        tot = cum[:, MQ - 1:MQ]
        suf = tot - cum + a
        row_dir = lax.broadcasted_iota(I32, (8, MQ), 0) // N_HEADS
        seg = jnp.where(row_dir == 0, cum, suf)
        wgt = jnp.exp(tot - seg) * dt
        carry = jnp.exp(seg)
        x = xs_s[pl.ds(r0, MQ), :]
        xb = x.astype(BF16)
        bm = b_s[pl.ds(r0, MQ), :]
        cm = c_s[pl.ds(r0, MQ), :]
        cmf = cm.astype(F32)
        cswap = pltpu.roll(cmf, M_STATE, 1)
        c_dup = [jnp.where(first_group, cmf, cswap), jnp.where(first_group, cswap, cmf)]
        bt = bm.astype(F32).T
        ydiag = []
        for h in range(N_HEADS):
            g = h // 2
            cb = lax.dot_general(cm[:, g * M_STATE:(g + 1) * M_STATE], bm[:, g * M_STATE:(g + 1) * M_STATE],
                                 _NT, preferred_element_type=F32)
            sf = jnp.broadcast_to(seg[h:h + 1, :], (MQ, MQ))
            sb = jnp.broadcast_to(seg[4 + h:5 + h, :], (MQ, MQ))
            lf = jnp.where(lower, jnp.exp(jnp.minimum(sf.T - sf, 0.0)), 0.0)
            lb = jnp.where(upper, jnp.exp(jnp.minimum(sb.T - sb, 0.0)), 0.0)
            m = cb * (lf * dt[h:h + 1, :] + lb * dt[4 + h:5 + h, :])
            ydiag.append(_dot(m.astype(BF16), xb[:, h * HD:(h + 1) * HD]))
        y_s[pl.ds(r0, MQ), :] = jnp.concatenate(ydiag, axis=1)
        for d in range(2):
            bwt = jnp.concatenate([bt[(h // 2) * M_STATE:(h // 2 + 1) * M_STATE, :] * wgt[4 * d + h:4 * d + h + 1, :]
                                   for h in range(N_HEADS)], axis=0)
            u_s[d, c] = (_dot(bwt.astype(BF16), xb) * bdmask).astype(BF16)
            dec_s[d, c] = jnp.broadcast_to(_head_lane_vec(jnp.exp(tot), 4 * d), (V7X_SUBLANES, DG))
            tiles = []
            for g in range(2):
                wrows = jnp.where(upper_half, carry[4 * d + 2 * g + 1:4 * d + 2 * g + 2, :],
                                  carry[4 * d + 2 * g:4 * d + 2 * g + 1, :])
                tiles.append(c_dup[g] * wrows.T)
            cw_s[d, c] = jnp.concatenate(tiles, axis=1).astype(BF16)
        return 0

    lax.fori_loop(0, N_MCH, chunk, 0, unroll=4)

    st_s[...] = jnp.zeros(st_s.shape, F32)

    def scan(i, _):
        for d in range(2):
            c = i if d == 0 else N_MCH - 1 - i
            st = st_s[d]
            yo_s[d, pl.ds(pl.multiple_of(c * MQ, MQ), MQ), :] = _dot(cw_s[d, c], st.astype(BF16))
            st_s[d] = st * dec_s[d, c][0:1, :] + u_s[d, c].astype(F32)
        return 0

    lax.fori_loop(0, N_MCH, scan, 0, unroll=4)

    def finish(c, _):
        r0 = pl.multiple_of(c * CONV_ROWS, CONV_ROWS)
        rows = pl.ds(r0, CONV_ROWS)
        y = y_s[rows, :] + yo_s[0, rows, :] + yo_s[1, rows, :] + xs_s[rows, :] * dsk_ref[...]
        y = y * _silu(z_ref[0, rows, :].astype(F32))
        o_ref[0, pl.ds(r0, CONV_ROWS), :] = (_rms(y) * nw_ref[...]).astype(BF16)
        return 0

    lax.fori_loop(0, SEQ // CONV_ROWS, finish, 0, unroll=2)


def mamba2(z3, xbc3, dtc4, conv_w, conv_b, dt_bias_col, a_col, dskip_lanes, norm_w, tri_incl, bdmask):
    b = z3.shape[0]
    full = lambda a: pl.BlockSpec(a.shape, lambda i: (0,) * a.ndim)
    return pl.pallas_call(
        _mamba_body, grid=(b,),
        in_specs=[pl.BlockSpec((1, SEQ, DG), lambda i: (i, 0, 0)),
                  pl.BlockSpec((1, SEQ, 2 * DG), lambda i: (i, 0, 0)),
                  pl.BlockSpec((1, N_MCH, 8, MQ), lambda i: (i, 0, 0, 0)),
                  full(conv_w), full(conv_b), full(dt_bias_col), full(a_col), full(dskip_lanes), full(norm_w),
                  full(tri_incl), full(bdmask)],
        out_specs=pl.BlockSpec((1, SEQ, DG), lambda i: (i, 0, 0)),
        out_shape=jax.ShapeDtypeStruct((b, SEQ, DG), BF16),
        scratch_shapes=[pltpu.VMEM((SEQ + 2 * CONV_HALO, 2 * DG), F32),
                        pltpu.VMEM((SEQ, DG), F32),
                        pltpu.VMEM((SEQ, 2 * M_STATE), BF16),
                        pltpu.VMEM((SEQ, 2 * M_STATE), BF16),
                        pltpu.VMEM((SEQ, DG), F32),
                        pltpu.VMEM((2, N_MCH, DG, DG), BF16),
                        pltpu.VMEM((2, N_MCH, V7X_SUBLANES, DG), F32),
                        pltpu.VMEM((2, N_MCH, MQ, DG), BF16),
                        pltpu.VMEM((2, SEQ, DG), F32),
                        pltpu.VMEM((2, DG, DG), F32)],
        compiler_params=_params("parallel"), name="mamba2",
    )(z3, xbc3, dtc4, conv_w, conv_b, dt_bias_col, a_col, dskip_lanes, norm_w, tri_incl, bdmask)


A_TQ = 128
A_ROWS = 256
A_KW = A_TQ + 2 * A_BAND


def _attn_bias_body(ids_ref, rb_ref, o_ref):
    ids = ids_ref[0]
    for h in range(N_HEADS):
        acc = jnp.full(ids.shape, NEG_BIG, F32)
        for bkt in range(N_BUCKETS):
            acc = jnp.where(ids == bkt, rb_ref[bkt, h], acc)
        o_ref[h, 0] = acc


def attention_bias_table(bucket_ids, rel_bias):
    nvar, tq, w = bucket_ids.shape
    return pl.pallas_call(
        _attn_bias_body, grid=(nvar,),
        in_specs=[pl.BlockSpec((1, tq, w), lambda v: (v, 0, 0)),
                  pl.BlockSpec(memory_space=pltpu.SMEM)],
        out_specs=pl.BlockSpec((N_HEADS, 1, tq, w), lambda v: (0, v, 0, 0)),
        out_shape=jax.ShapeDtypeStruct((N_HEADS, nvar, tq, w), F32),
        compiler_params=_params("parallel"), name="attention_bias_table",
    )(bucket_ids, rel_bias)


A_SLABS = 3 * DG // V7X_LANES
A_QBLOCKS = SEQ // A_TQ


A_SUB4 = SEQ // 4
A_SUB16 = SEQ // 16


def _attn_body(at_ref, b1_ref, b4_ref, b16_ref, o_ref, qkv_s, x4_s, x16_s, y16_s, y4_s, part_o, part_l):
    def fill(c, _):
        r0 = pl.multiple_of(c * A_ROWS, A_ROWS)
        for s in range(A_SLABS):
            qkv_s[s, pl.ds(r0, A_ROWS), :] = at_ref[0, pl.ds(r0, A_ROWS), s * V7X_LANES:(s + 1) * V7X_LANES].astype(F32)
        return 0

    lax.fori_loop(0, SEQ // A_ROWS, fill, 0)

    def deinterleave(s, _):
        for r4 in range(4):
            for c in range(A_SUB4 // A_ROWS):
                x4_s[s, pl.ds(r4 * A_SUB4 + c * A_ROWS, A_ROWS), :] = \
                    qkv_s[s, pl.ds(r4 + 4 * c * A_ROWS, A_ROWS, stride=4), :]
        for r in range(16):
            x16_s[s, pl.ds(r * A_SUB16, A_SUB16), :] = \
                x4_s[s, pl.ds((r % 4) * A_SUB4 + r // 4, A_SUB16, stride=4), :].astype(BF16)
        return 0

    lax.fori_loop(0, A_SLABS, deinterleave, 0)
    first_head = lax.broadcasted_iota(I32, (A_TQ, V7X_LANES), 1) < HD

    def run_pattern(pat, dil, bias_ref):
        n = SEQ // dil if dil < 16 else SEQ
        nblk = n // A_TQ
        w = A_KW

        def block(it, _):
            r = it // nblk
            i = it - r * nblk
            q0 = i * A_TQ
            k0 = jnp.clip(q0 - A_BAND, 0, n - w)
            var = jnp.where(i == 0, 0, jnp.where(i == nblk - 1, 2, 1))
            for hp in range(2):
                lanes = [slice((2 * part + hp) * V7X_LANES, (2 * part + hp + 1) * V7X_LANES) for part in range(3)]
                if dil == 4:
                    qrows = pl.ds(r + dil * q0, A_TQ, stride=dil)
                    krows = pl.ds(r + dil * k0, w, stride=dil)
                    q2 = qkv_s[hp, qrows, :]
                    k2 = qkv_s[2 + hp, krows, :].astype(BF16)
                    v2 = qkv_s[4 + hp, krows, :].astype(BF16)
                else:
                    qrows = pl.ds(pl.multiple_of(q0, A_TQ), A_TQ)
                    krows = pl.ds(pl.multiple_of(k0, A_BAND), w)
                    if dil == 1:
                        q2, k2, v2 = at_ref[0, qrows, lanes[0]], at_ref[0, krows, lanes[1]], at_ref[0, krows, lanes[2]]
                    else:
                        q2, k2, v2 = x16_s[hp, qrows, :], x16_s[2 + hp, krows, :], x16_s[4 + hp, krows, :]
                q2 = (q2 * (HD ** -0.5)).astype(BF16)
                outs, lses = [], []
                for hh in range(2):
                    keep = first_head if hh == 0 else jnp.logical_not(first_head)
                    qm = jnp.where(keep, q2, jnp.zeros_like(q2))
                    s = lax.dot_general(qm, k2, _NT, preferred_element_type=F32) + bias_ref[2 * hp + hh, var]
                    m = jnp.max(s, axis=1, keepdims=True)
                    p = jnp.exp(s - m)
                    den = jnp.sum(p, axis=1, keepdims=True)
                    outs.append(_dot(p.astype(BF16), v2) / den)
                    lses.append(m + jnp.log(den))
                o_new = jnp.where(first_head, outs[0], outs[1])
                l_new = jnp.where(first_head, lses[0], lses[1])
                if dil == 16:
                    y16_s[0, hp, qrows, :] = o_new
                    y16_s[1, hp, qrows, :] = l_new
                else:
                    part_o[pat, hp, qrows, :] = o_new
                    part_l[pat, hp, qrows, :] = l_new
            return 0

        lax.fori_loop(0, A_QBLOCKS, block, 0, unroll=8)

    for pat, (dil, bias_ref) in enumerate(zip(A_DILS, (b1_ref, b4_ref, b16_ref))):
        run_pattern(pat, dil, bias_ref)

    for a, dst in enumerate((part_o, part_l)):
        for hp in range(2):
            for r in range(16):
                y4_s[a, hp, pl.ds((r % 4) * A_SUB4 + r // 4, A_SUB16, stride=4), :] = \
                    y16_s[a, hp, pl.ds(r * A_SUB16, A_SUB16), :]
            for r4 in range(4):
                for c in range(A_SUB4 // A_ROWS):
                    dst[2, hp, pl.ds(r4 + 4 * c * A_ROWS, A_ROWS, stride=4), :] = \
                        y4_s[a, hp, pl.ds(r4 * A_SUB4 + c * A_ROWS, A_ROWS), :]

    def finish(c, _):
        rows = pl.ds(pl.multiple_of(c * A_TQ, A_TQ), A_TQ)
        for hp in range(2):
            ls = [part_l[pat, hp, rows, :] for pat in range(len(A_DILS))]
            mx = jnp.maximum(jnp.maximum(ls[0], ls[1]), ls[2])
            ws = [jnp.exp(l - mx) for l in ls]
            num = ws[0] * part_o[0, hp, rows, :] + ws[1] * part_o[1, hp, rows, :] + ws[2] * part_o[2, hp, rows, :]
            o_ref[0, rows, hp * V7X_LANES:(hp + 1) * V7X_LANES] = (num / (ws[0] + ws[1] + ws[2])).astype(BF16)
        return 0

    lax.fori_loop(0, SEQ // A_TQ, finish, 0)


def dilated_attention(at3, bias1, bias4, bias16):
    b = at3.shape[0]
    full = lambda a: pl.BlockSpec(a.shape, lambda i: (0,) * a.ndim)
    return pl.pallas_call(
        _attn_body, grid=(b,),
        in_specs=[pl.BlockSpec((1, SEQ, 3 * DG), lambda i: (i, 0, 0)), full(bias1), full(bias4), full(bias16)],
        out_specs=pl.BlockSpec((1, SEQ, DG), lambda i: (i, 0, 0)),
        out_shape=jax.ShapeDtypeStruct((b, SEQ, DG), BF16),
        scratch_shapes=[pltpu.VMEM((A_SLABS, SEQ, V7X_LANES), F32),
                        pltpu.VMEM((A_SLABS, SEQ, V7X_LANES), F32),
                        pltpu.VMEM((A_SLABS, SEQ, V7X_LANES), BF16),
                        pltpu.VMEM((2, 2, SEQ, V7X_LANES), F32),
                        pltpu.VMEM((2, 2, SEQ, V7X_LANES), F32),
                        pltpu.VMEM((len(A_DILS), 2, SEQ, V7X_LANES), F32),
                        pltpu.VMEM((len(A_DILS), 2, SEQ, V7X_LANES), F32)],
        compiler_params=_params("parallel"), name="dilated_attention",
    )(at3, bias1, bias4, bias16)


H_BLK = 256
H_CPB = H_BLK // H_CHUNK
N_HBLK = SEQ // H_BLK
N_HCH = SEQ // H_CHUNK


def _chunk_bcast(x, row):
    c = x.shape[1]
    x3 = x.reshape(H_CPB, H_CHUNK, c)
    return jnp.broadcast_to(x3[:, row:row + 1, :], (H_CPB, H_CHUNK, c)).reshape(H_BLK, c)


def _hgrn_body(p_ref, lb_ref, nw_ref, tin_ref, o_ref, qm_s, ut_s, oi_s, dec_s, oe_s, st_s):
    li = lax.broadcasted_iota(I32, (H_BLK, H_BLK), 0)
    si = lax.broadcasted_iota(I32, (H_BLK, H_BLK), 1)
    same = (li // H_CHUNK) == (si // H_CHUNK)
    mask_f = same & (si <= li)
    mask_b = same & (si >= li)
    lane_head = lax.broadcasted_iota(I32, (1, DG), 1) // HD

    def block(bi, _):
        r0 = pl.multiple_of(bi * H_BLK, H_BLK)
        rows = pl.ds(r0, H_BLK)
        q = _silu(p_ref[0, rows, 0:DG].astype(F32))
        v = p_ref[0, rows, 3 * DG:4 * DG]
        scores = [None] * N_HEADS
        for d in range(2):
            fpre = p_ref[0, rows, (1 + d) * DG:(2 + d) * DG].astype(F32)
            lb = lb_ref[d:d + 1, :]
            sg = jax.nn.sigmoid(fpre)
            g = jnp.log(lb + (1.0 - lb) * sg)
            k = (1.0 - lb) * (1.0 - sg)
            gi = _dot01_2(tin_ref[...], g)
            glast = _chunk_bcast(gi, H_CHUNK - 1)
            if d == 0:
                gc = gi
                gref = _chunk_bcast(gi, H_CHUNK // 2 - 1)
                msk = mask_f
            else:
                gc = glast - gi + g
                gref = _chunk_bcast(gc, H_CHUNK // 2)
                msk = mask_b
            qe = (q * jnp.exp(gc - gref)).astype(BF16)
            ke = (k * jnp.exp(gref - gc)).astype(BF16)
            for h in range(N_HEADS):
                hs = slice(h * HD, (h + 1) * HD)
                sc = jnp.where(msk, lax.dot_general(qe[:, hs], ke[:, hs], _NT, preferred_element_type=F32), 0.0)
                scores[h] = sc if d == 0 else scores[h] + sc
            qd = q * jnp.exp(gc)
            kd = (k * jnp.exp(glast - gc)).astype(BF16)
            for j in range(H_CPB):
                c = bi * H_CPB + j
                cr = slice(j * H_CHUNK, (j + 1) * H_CHUNK)
                qm_s[d, c] = jnp.concatenate([jnp.where(lane_head == h, qd[cr, :], 0.0) for h in range(N_HEADS)],
                                             axis=0).astype(BF16)
                ut = lax.dot_general(v[cr, :], kd[cr, :], _TN, preferred_element_type=F32)
                packed = ut[0:HD, :]
                for h in range(1, N_HEADS):
                    packed = jnp.where(lane_head == h, ut[h * HD:(h + 1) * HD, :], packed)
                ut_s[d, c] = packed.astype(BF16)
                dec_s[d, c] = jnp.broadcast_to(jnp.exp(glast[j * H_CHUNK:j * H_CHUNK + 1, :]), (V7X_SUBLANES, DG))
        for h in range(N_HEADS):
            oi_s[h, rows, :] = _dot(scores[h].astype(BF16), v[:, h * HD:(h + 1) * HD])
        return 0

    lax.fori_loop(0, N_HBLK, block, 0)

    st_s[...] = jnp.zeros(st_s.shape, F32)

    def step(i, _):
        for d in range(2):
            c = i if d == 0 else N_HCH - 1 - i
            rows = pl.ds(pl.multiple_of(c * H_CHUNK, H_CHUNK), H_CHUNK)
            st = st_s[d]
            inter = lax.dot_general(qm_s[d, c], st.astype(BF16), _NT, preferred_element_type=F32)
            for h in range(N_HEADS):
                oe_s[d, h, rows, :] = inter[h * H_CHUNK:(h + 1) * H_CHUNK, :]
            st_s[d] = st * dec_s[d, c][0:1, :] + ut_s[d, c].astype(F32)
        return 0

    lax.fori_loop(0, N_HCH, step, 0, unroll=8)

    def finish(c, _):
        r0 = pl.multiple_of(c * CONV_ROWS, CONV_ROWS)
        rows = pl.ds(r0, CONV_ROWS)
        gate = _silu(p_ref[0, rows, 4 * DG:5 * DG].astype(F32))
        outs = [_rms(oi_s[h, rows, :] + oe_s[0, h, rows, :] + oe_s[1, h, rows, :]) for h in range(N_HEADS)]
        o_ref[0, rows, :] = (jnp.concatenate(outs, axis=1) * nw_ref[...] * gate).astype(BF16)
        return 0

    lax.fori_loop(0, SEQ // CONV_ROWS, finish, 0, unroll=2)


def hgrn2(p3, lb2, norm_w_lanes, tri_in_chunk):
    b = p3.shape[0]
    full = lambda a: pl.BlockSpec(a.shape, lambda i: (0,) * a.ndim)
    return pl.pallas_call(
        _hgrn_body, grid=(b,),
        in_specs=[pl.BlockSpec((1, SEQ, 5 * DG), lambda i: (i, 0, 0)), full(lb2), full(norm_w_lanes),
                  full(tri_in_chunk)],
        out_specs=pl.BlockSpec((1, SEQ, DG), lambda i: (i, 0, 0)),
        out_shape=jax.ShapeDtypeStruct((b, SEQ, DG), BF16),
        scratch_shapes=[pltpu.VMEM((2, N_HCH, N_HEADS * H_CHUNK, DG), BF16),
                        pltpu.VMEM((2, N_HCH, HD, DG), BF16),
                        pltpu.VMEM((N_HEADS, SEQ, HD), F32),
                        pltpu.VMEM((2, N_HCH, V7X_SUBLANES, DG), F32),
                        pltpu.VMEM((2, N_HEADS, SEQ, HD), F32),
                        pltpu.VMEM((2, HD, DG), F32)],
        compiler_params=_params("parallel"), name="hgrn2",
    )(p3, lb2, norm_w_lanes, tri_in_chunk)


@functools.lru_cache(maxsize=None)
def _tables():
    t = {}
    k = np.arange(SEQ, dtype=np.int64)
    ang = 2.0 * np.pi * ((k[:, None] * k[None, :]) % NFFT).astype(np.float64) / NFFT
    t["cos"] = np.cos(ang).astype(np.float32)
    t["sin"] = np.sin(ang).astype(np.float32)
    rows_f32 = np.concatenate([t["cos"], t["sin"]], axis=0)
    t["dft_rows"] = rows_f32.astype(ml_dtypes.bfloat16)
    t["dft_rows_lo"] = (rows_f32 - t["dft_rows"].astype(np.float32)).astype(ml_dtypes.bfloat16)
    t["dft_cols"] = np.concatenate([t["cos"], t["sin"]], axis=1).astype(ml_dtypes.bfloat16)
    tt = np.linspace(0.0, 1.0, SEQ, dtype=np.float32)[:, None]
    bands = (HY_POS_DIM - 1) // 2
    ang_pos = (2.0 * math.pi * np.arange(SEQ, dtype=np.float32) / SEQ).astype(np.float32)
    f = np.linspace(1e-4, bands - 1, bands, dtype=np.float32)
    a2 = (ang_pos[:, None] * f[None, :]).astype(np.float32)
    z = np.concatenate([tt, np.cos(a2), -np.sin(a2)], axis=-1).astype(np.float32)
    zp = np.zeros((SEQ, V7X_LANES), np.float32)
    zp[:, :HY_POS_DIM] = z
    t["zpos"] = zp
    max_decay = math.log(1e-2) / 0.3
    min_decay = math.log(1e-2) / 1.5
    deltas = np.abs(np.linspace(min_decay, max_decay, DG, dtype=np.float32))
    t["decay"] = np.exp(-tt * deltas[None, :]).astype(np.float32)
    i128 = np.arange(V7X_LANES)
    t["u128"] = (i128[:, None] < i128[None, :]).astype(np.float32)
    im = np.arange(M_CHUNK)
    t["tri_incl"] = (im[:, None] <= im[None, :]).astype(np.float32)
    ib = np.arange(H_BLK)
    t["tri_in_chunk"] = ((ib[:, None] // H_CHUNK == ib[None, :] // H_CHUNK)
                         & (ib[None, :] <= ib[:, None])).astype(np.float32)
    idg = np.arange(DG)
    t["bdmask"] = (idg[:, None] // HD == idg[None, :] // HD).astype(np.float32)
    def bucket(rel):
        nb = N_BUCKETS // 2
        max_exact = nb // 2
        ret = (rel > 0).astype(np.int64) * nb
        n = np.abs(rel)
        nf = np.maximum(n, 1).astype(np.float64)
        large = max_exact + (np.log(nf / max_exact) / math.log(MAX_DISTANCE / max_exact)
                             * (nb - max_exact)).astype(np.int64)
        large = np.minimum(large, nb - 1)
        return ret + np.where(n < max_exact, n, large)

    for dil in A_DILS:
        n = SEQ // dil
        qi = np.arange(A_TQ)[:, None]
        kj = np.arange(A_KW)[None, :]
        ids = []
        for s0 in (0, -A_BAND, -(A_KW - A_TQ)):
            kk = kj + s0
            rel = kk - qi
            ok = np.abs(rel) <= A_BAND
            if n == A_TQ:
                ok &= (kk >= 0) & (kk < A_TQ)
            ids.append(np.where(ok, bucket(rel * dil), -1))
        t[f"bucket{dil}"] = np.stack(ids).astype(np.int32)
    return t


def kernel(x, w_in, w_out, norm_mix_w, norm_ffn_w, hy_conv_w, hy_pos_w1, hy_pos_b1, hy_pos_w2, hy_pos_b2,
           hy_sin_freq, hy_pos_w3, hy_filt_bias, m_conv_w, m_conv_b, m_dt_bias, m_A_log, m_D, m_norm_w, rel_bias,
           hg_lb, hg_norm_w, router_w, moe_w_gate, moe_w_up, moe_w_down, final_norm_w):
    b = x.shape[0]
    assert x.shape[1:] == (SEQ, D_MODEL) and b % MOE_FFN_SEQS == 0 and b % ROUTER_SEQS == 0, x.shape
    t = b * SEQ
    tb = _tables()
    dft_rows_lo = jnp.asarray(tb["dft_rows_lo"])
    dft_rows = jnp.asarray(tb["dft_rows"])
    dft_cols = jnp.asarray(tb["dft_cols"])
    u128 = jnp.asarray(tb["u128"]).astype(BF16)
    tri_incl = jnp.asarray(tb["tri_incl"]).astype(BF16)
    tri_in_chunk = jnp.asarray(tb["tri_in_chunk"]).astype(BF16)
    bdmask = jnp.asarray(tb["bdmask"])
    attn_bias = [attention_bias_table(jnp.asarray(tb[f"bucket{d}"]), rel_bias.astype(F32)) for d in A_DILS]

    sm = jax.nn.softmax(hg_lb.astype(F32), axis=0)
    lower_bounds = jnp.cumsum(sm, axis=0) - sm[:1]

    xa = x.reshape(t, D_MODEL)
    for l in range(DEPTH):
        wl = w_in[l]
        w_main = jnp.concatenate([wl[:, 0:768], wl[:, 768:1024], wl[:, 1024:1536], wl[:, 1544:2312],
                                  wl[:, 2312:3592]], axis=1).astype(BF16)
        w_dt_rows = wl[:, 1536:1544].T.astype(BF16)
        z, x0, mz, mx, at, hg, dtc = in_projection(xa, norm_mix_w[l][None, :], w_main, w_dt_rows, hy_conv_w[l])

        w1p = jnp.zeros((V7X_LANES, HY_HID), F32).at[:HY_POS_DIM].set(hy_pos_w1[l])
        kr, ki, kny = hyena_filter_spectrum(
            jnp.asarray(tb["zpos"]), w1p, hy_pos_b1[l][None, :], hy_pos_w2[l], hy_pos_b2[l][None, :],
            hy_sin_freq[l][None, :], hy_pos_w3[l], jnp.asarray(tb["decay"]), dft_rows, dft_rows_lo)
        z3, x03 = z.reshape(b, SEQ, DG), x0.reshape(b, SEQ, DG)
        ya = hyena_conv(z3, x03, dft_rows, dft_cols, kr, ki, kny, hy_filt_bias[l][None, :]).reshape(t, DG)

        a_col = (-jnp.exp(m_A_log[l].astype(F32))).reshape(8, 1)
        yb = mamba2(mz.reshape(b, SEQ, DG), mx.reshape(b, SEQ, 2 * DG), dtc.reshape(b, N_MCH, 8, MQ),
                    m_conv_w[l], m_conv_b[l][None, :], m_dt_bias[l].reshape(8, 1), a_col,
                    jnp.repeat(m_D[l].astype(F32), HD)[None, :], m_norm_w[l][None, :], tri_incl, bdmask).reshape(t, DG)

        yc = dilated_attention(at.reshape(b, SEQ, 3 * DG), *attn_bias).reshape(t, DG)

        lbl = lower_bounds[l]
        yd = hgrn2(hg.reshape(b, SEQ, 5 * DG), lbl, jnp.tile(hg_norm_w[l], N_HEADS)[None, :],
                   tri_in_chunk).reshape(t, DG)

        rw_rows = router_w[l].T.astype(F32)
        rw_hi = rw_rows.astype(BF16)
        rw_lo = (rw_rows - rw_hi.astype(F32)).astype(BF16)
        xo, xn, logits = out_projection(xa, ya, yb, yc, yd, w_out[l].reshape(4, DG, D_MODEL).astype(BF16),
                                        norm_ffn_w[l][None, :], rw_hi, rw_lo)
        xn3 = xn.reshape(b, SEQ, D_MODEL)
        rank, gate, seg = router(logits, u128)
        seg_flat = seg[:, :, :MOE_SEG_STRIDE].reshape(-1)
        xe = moe_gather(seg_flat, xn3, rank)
        ye = moe_experts(xe, moe_w_gate, moe_w_up, moe_w_down, l)
        xa = moe_scatter(seg_flat, ye, rank, gate, xo.reshape(b, SEQ, D_MODEL), final_norm_w[None, :],
                         final=(l == DEPTH - 1)).reshape(t, D_MODEL)
    return xa.reshape(b, SEQ, D_MODEL)
```

```python
import functools
import math

import ml_dtypes
import numpy as np
import jax
import jax.numpy as jnp
from jax import lax
from jax.experimental import pallas as pl
from jax.experimental.pallas import tpu as pltpu

F32 = jnp.float32
BF16 = jnp.bfloat16
I32 = jnp.int32

D_MODEL = 1024
SEQ = 2048
DEPTH = 2
DG = 256
N_HEADS = 4
HD = 64
HY_POS_DIM = 33
HY_HID = 64
M_CONV = 5
M_STATE = 64
M_CHUNK = 128
H_CHUNK = 32
A_BAND = 64
A_DILS = (1, 4, 16)
N_BUCKETS = 32
MAX_DISTANCE = 1024
N_EXPERTS = 16
CAP = 2 * SEQ // N_EXPERTS
D_FF = 1024
EPS = 1e-6
NFFT = 2 * SEQ

V7X_LANES = 128
V7X_SUBLANES = 8
V7X_VMEM_LIMIT_BYTES = 56 * 1024 * 1024

NEG_BIG = -1e30

_NT = (((1,), (1,)), ((), ()))
_TN = (((0,), (0,)), ((), ()))


def _params(*sem):
    return pltpu.CompilerParams(dimension_semantics=sem, vmem_limit_bytes=V7X_VMEM_LIMIT_BYTES)


def _dot(a, b):
    return jnp.dot(a, b, preferred_element_type=F32)


def _dot_hi(a, b):
    return jnp.dot(a, b, preferred_element_type=F32, precision=lax.Precision.HIGHEST)


def _dot01_2(t_bf16, x):
    x1 = x.astype(BF16)
    x2 = (x - x1.astype(F32)).astype(BF16)
    return _dot(t_bf16, x1) + _dot(t_bf16, x2)


def _dot01_rhs(x, t_bf16):
    x1 = x.astype(BF16)
    r1 = x - x1.astype(F32)
    x2 = r1.astype(BF16)
    x3 = (r1 - x2.astype(F32)).astype(BF16)
    return _dot(x1, t_bf16) + _dot(x2, t_bf16) + _dot(x3, t_bf16)


def _silu(x):
    return x * jax.nn.sigmoid(x)


def _softplus(x):
    return jnp.maximum(x, 0.0) + jnp.log(1.0 + jnp.exp(-jnp.abs(x)))


def _rms(x):
    return x * lax.rsqrt(jnp.mean(x * x, axis=-1, keepdims=True) + EPS)


TM_PROJ = 1024
_HY0, _MZ0, _MX0, _AT0, _HG0, _PEND = 0, 768, 1024, 1536, 2304, 3584


TILES_PER_SEQ = SEQ // TM_PROJ
PROJ_HALO = 16


def _inproj_body(x_ref, xp_ref, xq_ref, nw_ref, w_ref, wdt_ref, hcw_ref, z_ref, x0_ref, mz_ref, mx_ref, at_ref, hg_ref,
                 dtc_ref):
    norm = lambda v: (_rms(v) * nw_ref[...]).astype(BF16)
    hn = norm(x_ref[...])
    tile = pl.program_id(0) % TILES_PER_SEQ
    zero = jnp.zeros((PROJ_HALO, D_MODEL), BF16)
    h_ext = jnp.concatenate([jnp.where(tile == 0, zero, norm(xp_ref[...])), hn,
                             jnp.where(tile == TILES_PER_SEQ - 1, zero, norm(xq_ref[...]))], axis=0)
    p = _dot(h_ext, w_ref[:, _HY0:_MZ0])
    rows_ext = TM_PROJ + 2 * PROJ_HALO
    mid = slice(PROJ_HALO, PROJ_HALO + TM_PROJ)
    u = (pltpu.roll(p, 1, 0)[mid] * hcw_ref[0:1, :] + p[mid] * hcw_ref[1:2, :]
         + pltpu.roll(p, rows_ext - 1, 0)[mid] * hcw_ref[2:3, :])
    x0_ref[...] = u[:, 0:DG].astype(BF16)
    z_ref[...] = (u[:, 2 * DG:3 * DG] * u[:, DG:2 * DG]).astype(BF16)
    mz_ref[...] = _dot(hn, w_ref[:, _MZ0:_MX0]).astype(BF16)
    mx_ref[...] = _dot(hn, w_ref[:, _MX0:_AT0]).astype(BF16)
    at_ref[...] = _dot(hn, w_ref[:, _AT0:_HG0]).astype(BF16)
    hg_ref[...] = _dot(hn, w_ref[:, _HG0:_PEND]).astype(BF16)
    dt_rows = lax.dot_general(wdt_ref[...], hn, _NT, preferred_element_type=F32)
    for j in range(TM_PROJ // M_CHUNK):
        dtc_ref[j] = dt_rows[:, j * M_CHUNK:(j + 1) * M_CHUNK]


def in_projection(x, norm_w, w_main, w_dt_rows, hy_conv_w):
    t = x.shape[0]
    tm = TM_PROJ
    halo = tm // PROJ_HALO
    row = lambda w: pl.BlockSpec((tm, w), lambda i: (i, 0))
    full = lambda a: pl.BlockSpec(a.shape, lambda i: (0,) * a.ndim)
    in_specs = [row(D_MODEL),
                pl.BlockSpec((PROJ_HALO, D_MODEL), lambda i: (jnp.maximum(i * halo - 1, 0), 0)),
                pl.BlockSpec((PROJ_HALO, D_MODEL), lambda i: (jnp.minimum((i + 1) * halo, t // PROJ_HALO - 1), 0)),
                full(norm_w), full(w_main), full(w_dt_rows), full(hy_conv_w)]
    widths = (256, 256, 256, 512, 768, 1280)
    out_shape = [jax.ShapeDtypeStruct((t, w), BF16) for w in widths]
    out_shape.append(jax.ShapeDtypeStruct((t // M_CHUNK, 8, M_CHUNK), F32))
    out_specs = [row(w) for w in widths] + [pl.BlockSpec((tm // M_CHUNK, 8, M_CHUNK), lambda i: (i, 0, 0))]
    return pl.pallas_call(
        _inproj_body, grid=(t // tm,), in_specs=in_specs, out_specs=out_specs, out_shape=out_shape,
        compiler_params=_params("parallel"), name="in_projection",
    )(x, x, x, norm_w, w_main, w_dt_rows, hy_conv_w)


TM_OUT = 1024


def _outproj_body(x_ref, ya_ref, yb_ref, yc_ref, yd_ref, w_ref, nw_ref, rwh_ref, rwl_ref, xo_ref, xn_ref, lg_ref):
    x = x_ref[...]
    acc = x + _dot(ya_ref[...], w_ref[0]) + _dot(yb_ref[...], w_ref[1])
    acc = acc + _dot(yc_ref[...], w_ref[2]) + _dot(yd_ref[...], w_ref[3])
    xo_ref[...] = acc
    xn = _rms(acc) * nw_ref[...]
    xh = xn.astype(BF16)
    xn_ref[...] = xh
    xl = (xn - xh.astype(F32)).astype(BF16)
    nt = lambda w, a: lax.dot_general(w, a, _NT, preferred_element_type=F32)
    lg_ref[...] = nt(rwh_ref[...], xh) + nt(rwh_ref[...], xl) + nt(rwl_ref[...], xh)


def out_projection(x, ya, yb, yc, yd, w_out4, norm_w, rw_hi, rw_lo):
    t = x.shape[0]
    tm = TM_OUT
    row = lambda w: pl.BlockSpec((tm, w), lambda i: (i, 0))
    full = lambda a: pl.BlockSpec(a.shape, lambda i: (0,) * a.ndim)
    in_specs = [row(D_MODEL)] + [row(DG)] * 4 + [full(w_out4), full(norm_w), full(rw_hi), full(rw_lo)]
    return pl.pallas_call(
        _outproj_body, grid=(t // tm,), in_specs=in_specs,
        out_specs=[row(D_MODEL), row(D_MODEL), pl.BlockSpec((N_EXPERTS, tm), lambda i: (0, i))],
        out_shape=[jax.ShapeDtypeStruct((t, D_MODEL), F32), jax.ShapeDtypeStruct((t, D_MODEL), BF16),
                   jax.ShapeDtypeStruct((N_EXPERTS, t), F32)],
        compiler_params=_params("parallel"), name="out_projection",
    )(x, ya, yb, yc, yd, w_out4, norm_w, rw_hi, rw_lo)


def _prefix_excl_lanes(mask_f32, u_ref):
    e = mask_f32.shape[0]
    off = jnp.zeros((e, 1), F32)
    parts, bounds = [], [off]
    for k in range(SEQ // V7X_LANES):
        tile = mask_f32[:, k * V7X_LANES:(k + 1) * V7X_LANES]
        parts.append(_dot(tile.astype(BF16), u_ref[...]) + off)
        off = off + jnp.sum(tile, axis=1, keepdims=True)
        bounds.append(off)
    return jnp.concatenate(parts, axis=1), bounds


ROUTER_SEQS = 2


def _router_body(lg_ref, u_ref, rank_ref, gate_ref, seg_ref):
    affs = []
    for q in range(ROUTER_SEQS):
        logits = lg_ref[:, q * SEQ:(q + 1) * SEQ]
        ex = jnp.exp(logits - jnp.max(logits, axis=0, keepdims=True))
        affs.append(ex / jnp.sum(ex, axis=0, keepdims=True))
    aff = jnp.concatenate(affs, axis=0)
    nrow = ROUTER_SEQS * N_EXPERTS
    bits = pltpu.bitcast(aff, I32)

    def search(i, thr):
        cand = thr | jnp.left_shift(jnp.int32(1), 30 - i)
        cnt = jnp.sum((bits >= cand).astype(I32), axis=1, keepdims=True)
        return jnp.where(cnt >= CAP, cand, thr)

    thr = lax.fori_loop(0, 31, search, jnp.zeros((nrow, 1), I32))
    gt = (bits > thr).astype(F32)
    eq = (bits == thr).astype(F32)
    need = CAP - jnp.sum(gt, axis=1, keepdims=True)
    tie_rank, _ = _prefix_excl_lanes(eq, u_ref)
    sel = gt + eq * (tie_rank < need).astype(F32)
    rank, bounds = _prefix_excl_lanes(sel, u_ref)
    rank = jnp.where(sel > 0.0, rank, -1.0)
    lane = lax.broadcasted_iota(I32, (nrow, V7X_LANES), 1)
    seg = jnp.zeros((nrow, V7X_LANES), F32)
    tiles = MOE_SEG // V7X_LANES
    for sgm in range(N_MOE_SEG):
        first = jnp.floor(bounds[sgm * tiles] * (1.0 / MOE_ALIGN)) * MOE_ALIGN
        need = jnp.floor((bounds[(sgm + 1) * tiles] - first + (MOE_TILE - 1)) * (1.0 / MOE_TILE))
        need = jnp.max(need.reshape(ROUTER_SEQS, N_EXPERTS, 1), axis=1, keepdims=True)
        need = jnp.broadcast_to(need, (ROUTER_SEQS, N_EXPERTS, 1)).reshape(nrow, 1)
        seg = jnp.where(lane == sgm, first, seg)
        seg = jnp.where(lane == N_MOE_SEG + sgm, need, seg)
    seg = seg.astype(I32)
    for q in range(ROUTER_SEQS):
        rows = slice(q * N_EXPERTS, (q + 1) * N_EXPERTS)
        rank_ref[q] = rank[rows]
        gate_ref[q] = aff[rows]
        seg_ref[q] = seg[rows]


def router(logits_et, u128):
    b = logits_et.shape[1] // SEQ
    out = jax.ShapeDtypeStruct((b, N_EXPERTS, SEQ), F32)
    return pl.pallas_call(
        _router_body, grid=(b // ROUTER_SEQS,),
        in_specs=[pl.BlockSpec((N_EXPERTS, ROUTER_SEQS * SEQ), lambda i: (0, i)),
                  pl.BlockSpec(u128.shape, lambda i: (0, 0))],
        out_specs=[pl.BlockSpec((ROUTER_SEQS, N_EXPERTS, SEQ), lambda i: (i, 0, 0))] * 2
                  + [pl.BlockSpec((ROUTER_SEQS, N_EXPERTS, V7X_LANES), lambda i: (i, 0, 0))],
        out_shape=[out, out, jax.ShapeDtypeStruct((b, N_EXPERTS, V7X_LANES), I32)],
        compiler_params=_params("parallel"), name="router",
    )(logits_et, u128)


MOE_SEG = 256
N_MOE_SEG = SEQ // MOE_SEG
MOE_TILE = 64
MOE_ALIGN = 16
MOE_GROUP = 4
MOE_SEG_STRIDE = 16
MOE_FFN_SEQS = 4


def _moe_seg_plan(cs_ref, b, s):
    base = b * N_EXPERTS * MOE_SEG_STRIDE
    starts = [cs_ref[base + ex * MOE_SEG_STRIDE + s] for ex in range(N_EXPERTS)]
    return starts, cs_ref[base + N_MOE_SEG + s]


def _moe_tile_bases(starts, r):
    own = [st + r * MOE_TILE for st in starts]
    return [pl.multiple_of(jnp.minimum(o, CAP - MOE_TILE), MOE_ALIGN) for o in own], own


def _moe_onehot_group(rank_ref, gate_ref, s, bases, own, grp):
    lanes = pl.ds(pl.multiple_of(s * MOE_SEG, MOE_SEG), MOE_SEG)
    j = lax.broadcasted_iota(I32, (MOE_TILE, MOE_SEG), 0)
    rows = []
    for ex in grp:
        slot = bases[ex] + j
        hit = (rank_ref[0, ex:ex + 1, lanes] == slot.astype(F32)) & (slot >= own[ex])
        val = 1.0 if gate_ref is None else gate_ref[0, ex:ex + 1, lanes]
        rows.append(jnp.where(hit, val, 0.0).astype(BF16))
    return jnp.concatenate(rows, axis=0)


_MOE_GROUPS = [list(range(g * MOE_GROUP, (g + 1) * MOE_GROUP)) for g in range(N_EXPERTS // MOE_GROUP)]


def _moe_gather_body(cs_ref, xn_ref, rank_ref, xe_ref):
    b = pl.program_id(0)

    def zero(ex, _):
        xe_ref[0, ex] = jnp.zeros((CAP, D_MODEL), BF16)
        return 0

    lax.fori_loop(0, N_EXPERTS, zero, 0)

    def seg_gather(s, _):
        starts, rounds = _moe_seg_plan(cs_ref, b, s)
        xn_seg = xn_ref[0, pl.ds(pl.multiple_of(s * MOE_SEG, MOE_SEG), MOE_SEG), :]

        def one_round(r, _):
            bases, own = _moe_tile_bases(starts, r)
            for grp in _MOE_GROUPS:
                got = _dot(_moe_onehot_group(rank_ref, None, s, bases, own, grp), xn_seg)
                for k, ex in enumerate(grp):
                    rows = pl.ds(bases[ex], MOE_TILE)
                    xe_ref[0, ex, rows, :] += got[k * MOE_TILE:(k + 1) * MOE_TILE].astype(BF16)
            return 0

        lax.fori_loop(0, rounds, one_round, 0)
        return 0

    lax.fori_loop(0, N_MOE_SEG, seg_gather, 0)


def moe_gather(seg_counts_flat, xn3, rank3):
    b = xn3.shape[0]
    grid_spec = pltpu.PrefetchScalarGridSpec(
        num_scalar_prefetch=1, grid=(b,),
        in_specs=[pl.BlockSpec((1, SEQ, D_MODEL), lambda i, cs: (i, 0, 0)),
                  pl.BlockSpec((1, N_EXPERTS, SEQ), lambda i, cs: (i, 0, 0))],
        out_specs=pl.BlockSpec((1, N_EXPERTS, CAP, D_MODEL), lambda i, cs: (i, 0, 0, 0)))
    return pl.pallas_call(
        _moe_gather_body, grid_spec=grid_spec,
        out_shape=jax.ShapeDtypeStruct((b, N_EXPERTS, CAP, D_MODEL), BF16),
        compiler_params=_params("parallel"), name="moe_gather",
    )(seg_counts_flat, xn3, rank3)


def _moe_experts_body(xe_ref, wg_ref, wu_ref, wd_ref, ye_ref, wg_s, wu_s, wd_s):
    @pl.when(pl.program_id(1) == 0)
    def _():
        wg_s[...] = wg_ref[0, 0].astype(BF16)
        wu_s[...] = wu_ref[0, 0].astype(BF16)
        wd_s[...] = wd_ref[0, 0].astype(BF16)

    xe = xe_ref[...].reshape(MOE_FFN_SEQS * CAP, D_MODEL)
    hid = (_silu(_dot(xe, wg_s[...])) * _dot(xe, wu_s[...])).astype(BF16)
    ye_ref[...] = _dot(hid, wd_s[...]).astype(BF16).reshape(MOE_FFN_SEQS, 1, CAP, D_MODEL)


def moe_experts(xe4, w_gate, w_up, w_down, layer):
    b = xe4.shape[0]
    blk = pl.BlockSpec((MOE_FFN_SEQS, 1, CAP, D_MODEL), lambda e, g: (g, e, 0, 0))
    w_spec = lambda a: pl.BlockSpec((1, 1) + a.shape[2:], lambda e, g: (layer, e, 0, 0))
    return pl.pallas_call(
        _moe_experts_body, grid=(N_EXPERTS, b // MOE_FFN_SEQS),
        in_specs=[blk, w_spec(w_gate), w_spec(w_up), w_spec(w_down)],
        out_specs=blk, out_shape=jax.ShapeDtypeStruct(xe4.shape, BF16),
        scratch_shapes=[pltpu.VMEM((D_MODEL, D_FF), BF16), pltpu.VMEM((D_MODEL, D_FF), BF16),
                        pltpu.VMEM((D_FF, D_MODEL), BF16)],
        compiler_params=_params("parallel", "arbitrary"), name="moe_experts",
    )(xe4, w_gate, w_up, w_down)


MOE_SCATTER_SEGS = 4


def _moe_scatter_body(final, cs_ref, ye_ref, rank_ref, gate_ref, xo_ref, nw_ref, o_ref):
    b = pl.program_id(0)
    half = pl.program_id(1)

    def seg_scatter(k, _):
        s = half * MOE_SCATTER_SEGS + k
        starts, rounds = _moe_seg_plan(cs_ref, b, s)
        tok = pl.ds(pl.multiple_of(k * MOE_SEG, MOE_SEG), MOE_SEG)
        o_ref[0, tok, :] = xo_ref[0, tok, :]

        def one_round(r, _):
            bases, own = _moe_tile_bases(starts, r)
            for grp in _MOE_GROUPS:
                ye = jnp.concatenate([ye_ref[0, ex, pl.ds(bases[ex], MOE_TILE), :] for ex in grp], axis=0)
                o_ref[0, tok, :] += lax.dot_general(_moe_onehot_group(rank_ref, gate_ref, s, bases, own, grp), ye,
                                                    _TN, preferred_element_type=F32)
            return 0

        lax.fori_loop(0, rounds, one_round, 0)
        if final:
            o_ref[0, tok, :] = _rms(o_ref[0, tok, :]) * nw_ref[...]
        return 0

    lax.fori_loop(0, MOE_SCATTER_SEGS, seg_scatter, 0)


def moe_scatter(seg_counts_flat, ye4, rank3, gate3, xo3, final_norm_w, final):
    b = ye4.shape[0]
    rows = MOE_SCATTER_SEGS * MOE_SEG
    sel_spec = pl.BlockSpec((1, N_EXPERTS, SEQ), lambda i, j, cs: (i, 0, 0))
    tok_spec = pl.BlockSpec((1, rows, D_MODEL), lambda i, j, cs: (i, j, 0))
    grid_spec = pltpu.PrefetchScalarGridSpec(
        num_scalar_prefetch=1, grid=(b, SEQ // rows),
        in_specs=[pl.BlockSpec((1, N_EXPERTS, CAP, D_MODEL), lambda i, j, cs: (i, 0, 0, 0)), sel_spec, sel_spec,
                  tok_spec, pl.BlockSpec(final_norm_w.shape, lambda i, j, cs: (0, 0))],
        out_specs=tok_spec)
    return pl.pallas_call(
        functools.partial(_moe_scatter_body, final), grid_spec=grid_spec,
        out_shape=jax.ShapeDtypeStruct((b, SEQ, D_MODEL), F32),
        compiler_params=_params("parallel", "arbitrary"), name="moe_scatter",
    )(seg_counts_flat, ye4, rank3, gate3, xo3, final_norm_w)


HY_KB = 256
HY_ROWS = 256


def _split_bf16(x):
    hi = x.astype(BF16)
    return hi, (x - hi.astype(F32)).astype(BF16)


def _hy_filter_body(z_ref, w1_ref, b1_ref, w2_ref, b2_ref, fr_ref, w3_ref, dec_ref, ch_ref, cl_ref, sh_ref, sl_ref,
                    kr_ref, ki_ref, kny_ref, ah_s, al_s, dh_s, dl_s):
    @pl.when(pl.program_id(0) == 0)
    def _():
        def rows(c, kny):
            r0 = pl.multiple_of(c * HY_ROWS, HY_ROWS)
            fr = fr_ref[...]
            h = jnp.sin(fr * (_dot_hi(z_ref[pl.ds(r0, HY_ROWS), :], w1_ref[...]) + b1_ref[...]))
            h = jnp.sin(fr * (_dot_hi(h, w2_ref[...]) + b2_ref[...]))
            h = _dot_hi(h, w3_ref[...])
            dec = dec_ref[pl.ds(r0, HY_ROWS), :]
            pos = r0 + lax.broadcasted_iota(I32, (HY_ROWS, DG), 0)
            hf = h[:, :DG] * dec
            hb = jnp.where(pos == 0, 0.0, h[:, DG:] * dec)
            a = hf + hb
            ah_s[pl.ds(r0, HY_ROWS), :], al_s[pl.ds(r0, HY_ROWS), :] = _split_bf16(a)
            dh_s[pl.ds(r0, HY_ROWS), :], dl_s[pl.ds(r0, HY_ROWS), :] = _split_bf16(hf - hb)
            sgn = (1 - 2 * (pos & 1)).astype(F32)
            return kny + jnp.sum(a * sgn, axis=0, keepdims=True)

        kny = lax.fori_loop(0, SEQ // HY_ROWS, rows, jnp.zeros((1, DG), F32))
        kny_ref[...] = jnp.broadcast_to(kny, kny_ref.shape)

    kr_ref[...] = _dot(ch_ref[...], ah_s[...]) + _dot(ch_ref[...], al_s[...]) + _dot(cl_ref[...], ah_s[...])
    ki_ref[...] = _dot(sh_ref[...], dh_s[...]) + _dot(sh_ref[...], dl_s[...]) + _dot(sl_ref[...], dh_s[...])


def hyena_filter_spectrum(zpos, w1, b1, w2, b2, freq, w3, decay, dft_rows, dft_rows_lo):
    full = lambda a: pl.BlockSpec(a.shape, lambda k: (0,) * a.ndim)
    kblk = pl.BlockSpec((HY_KB, SEQ), lambda k: (k, 0))
    sblk = pl.BlockSpec((HY_KB, SEQ), lambda k: (SEQ // HY_KB + k, 0))
    oblk = pl.BlockSpec((HY_KB, DG), lambda k: (k, 0))
    return pl.pallas_call(
        _hy_filter_body, grid=(SEQ // HY_KB,),
        in_specs=[full(zpos), full(w1), full(b1), full(w2), full(b2), full(freq), full(w3), full(decay),
                  kblk, kblk, sblk, sblk],
        out_specs=[oblk, oblk, pl.BlockSpec((V7X_SUBLANES, DG), lambda k: (0, 0))],
        out_shape=[jax.ShapeDtypeStruct((SEQ, DG), F32), jax.ShapeDtypeStruct((SEQ, DG), F32),
                   jax.ShapeDtypeStruct((V7X_SUBLANES, DG), F32)],
        scratch_shapes=[pltpu.VMEM((SEQ, DG), BF16)] * 4,
        compiler_params=_params("arbitrary"), name="hyena_filter",
    )(zpos, w1, b1, w2, b2, freq, w3, decay, dft_rows, dft_rows_lo, dft_rows, dft_rows_lo)


CONV_ROWS = 128
CONV_HALO = 8


def _dwconv_rows(pad_ref, w_ref, r0, lanes, k):
    n = CONV_ROWS + 2 * CONV_HALO
    win = pad_ref[pl.ds(r0, n), lanes]
    acc = None
    for j in range(k):
        sh = (k // 2 - j) % n
        rolled = win if sh == 0 else pltpu.roll(win, sh, 0)
        term = rolled[CONV_HALO:CONV_HALO + CONV_ROWS] * w_ref[j:j + 1, lanes]
        acc = term if acc is None else acc + term
    return acc


def _fill_padded(pad_ref, src_ref, width):
    zeros = jnp.zeros((CONV_HALO, width), F32)
    pad_ref[pl.ds(0, CONV_HALO), :] = zeros
    pad_ref[pl.ds(SEQ + CONV_HALO, CONV_HALO), :] = zeros

    def fill(c, _):
        r0 = pl.multiple_of(c * CONV_ROWS, CONV_ROWS)
        pad_ref[pl.ds(r0 + CONV_HALO, CONV_ROWS), :] = src_ref[0, pl.ds(r0, CONV_ROWS), :].astype(F32)
        return 0

    lax.fori_loop(0, SEQ // CONV_ROWS, fill, 0)


HY_FB = 512


def _hy_conv_body(z_ref, x0_ref, t1_ref, t2_ref, kr_ref, ki_ref, kny_ref, fb_ref, o_ref, y_s):
    z = z_ref[0]

    def spectrum(kb, _):
        rows = pl.ds(pl.multiple_of(kb * HY_FB, HY_FB), HY_FB)
        rows_s = pl.ds(pl.multiple_of(SEQ + kb * HY_FB, HY_FB), HY_FB)
        zr = _dot(t1_ref[rows, :], z)
        zi = _dot(t1_ref[rows_s, :], z)
        krow = kb * HY_FB + lax.broadcasted_iota(I32, (HY_FB, 1), 0)
        wk = jnp.where(krow == 0, 1.0 / NFFT, 2.0 / NFFT)
        kr = kr_ref[rows, :]
        ki = ki_ref[rows, :]
        y_s[rows, :] = ((zr * kr - zi * ki) * wk).astype(BF16)
        y_s[rows_s, :] = ((zr * ki + zi * kr) * wk).astype(BF16)
        return 0

    lax.fori_loop(0, SEQ // HY_FB, spectrum, 0)

    zny = jnp.sum(z.astype(F32) * (1 - 2 * (lax.broadcasted_iota(I32, (SEQ, DG), 0) & 1)).astype(F32),
                  axis=0, keepdims=True)
    nyq = zny * kny_ref[0:1, :] * (1.0 / NFFT)

    def synth(tb, _):
        rows = pl.ds(pl.multiple_of(tb * HY_FB, HY_FB), HY_FB)
        conv = _dot(t2_ref[rows, :], y_s[...])
        sgn = (1 - 2 * (lax.broadcasted_iota(I32, (HY_FB, DG), 0) & 1)).astype(F32)
        zf = z_ref[0, rows, :].astype(F32)
        o_ref[0, rows, :] = (x0_ref[0, rows, :].astype(F32) * (conv + nyq * sgn + zf * fb_ref[...])).astype(BF16)
        return 0

    lax.fori_loop(0, SEQ // HY_FB, synth, 0)


def hyena_conv(z3, x03, dft_rows, dft_cols, kr, ki, kny, fbias):
    b = z3.shape[0]
    seq_blk = pl.BlockSpec((1, SEQ, DG), lambda i: (i, 0, 0))
    once = lambda a: pl.BlockSpec(a.shape, lambda i: (0,) * a.ndim, pipeline_mode=pl.Buffered(1))
    return pl.pallas_call(
        _hy_conv_body, grid=(b,),
        in_specs=[seq_blk, seq_blk, once(dft_rows), once(dft_cols), once(kr), once(ki), once(kny), once(fbias)],
        out_specs=seq_blk, out_shape=jax.ShapeDtypeStruct((b, SEQ, DG), BF16),
        scratch_shapes=[pltpu.VMEM((2 * SEQ, DG), BF16)],
        compiler_params=_params("parallel"), name="hyena_conv",
    )(z3, x03, dft_rows, dft_cols, kr, ki, kny, fbias)


N_MCH = SEQ // M_CHUNK
MQ = M_CHUNK


def _head_lane_vec(rows8, base):
    lane_head = lax.broadcasted_iota(I32, (1, DG), 1) // HD
    out = jnp.zeros((1, DG), F32)
    for h in range(N_HEADS):
        out = jnp.where(lane_head == h, rows8[base + h:base + h + 1, :], out)
    return out


def _mamba_body(z_ref, xbc_ref, dtc_ref, cw_ref, cb_ref, dtb_ref, a_ref, dsk_ref, nw_ref, tri_ref, bd_ref,
                o_ref, pad, xs_s, b_s, c_s, y_s, u_s, dec_s, cw_s, yo_s, st_s):
    _fill_padded(pad, xbc_ref, 2 * DG)

    def conv_rows(c, _):
        r0 = pl.multiple_of(c * CONV_ROWS, CONV_ROWS)
        for g in range(4):
            lanes = slice(g * V7X_LANES, (g + 1) * V7X_LANES)
            u = _silu(_dwconv_rows(pad, cw_ref, r0, lanes, M_CONV) + cb_ref[:, lanes])
            if g < 2:
                xs_s[pl.ds(r0, CONV_ROWS), lanes] = u
            elif g == 2:
                b_s[pl.ds(r0, CONV_ROWS), :] = u.astype(BF16)
            else:
                c_s[pl.ds(r0, CONV_ROWS), :] = u.astype(BF16)
        return 0

    lax.fori_loop(0, SEQ // CONV_ROWS, conv_rows, 0)

    li = lax.broadcasted_iota(I32, (MQ, MQ), 0)
    si = lax.broadcasted_iota(I32, (MQ, MQ), 1)
    lower = si <= li
    upper = si >= li
    upper_half = li >= M_STATE
    first_group = si < M_STATE
    bdmask = bd_ref[...]

    def chunk(c, _):
        r0 = pl.multiple_of(c * MQ, MQ)
        dt = _softplus(dtc_ref[0, c] + dtb_ref[...])
        a = dt * a_ref[...]
        cum = _dot01_rhs(a, tri_ref[...])
        tot = cum[:, MQ - 1:MQ]
        suf = tot - cum + a
        row_dir = lax.broadcasted_iota(I32, (8, MQ), 0) // N_HEADS
        seg = jnp.where(row_dir == 0, cum, suf)
        wgt = jnp.exp(tot - seg) * dt
        carry = jnp.exp(seg)
        x = xs_s[pl.ds(r0, MQ), :]
        xb = x.astype(BF16)
        bm = b_s[pl.ds(r0, MQ), :]
        cm = c_s[pl.ds(r0, MQ), :]
        cmf = cm.astype(F32)
        cswap = pltpu.roll(cmf, M_STATE, 1)
        c_dup = [jnp.where(first_group, cmf, cswap), jnp.where(first_group, cswap, cmf)]
        bt = bm.astype(F32).T
        ydiag = []
        for h in range(N_HEADS):
            g = h // 2
            cb = lax.dot_general(cm[:, g * M_STATE:(g + 1) * M_STATE], bm[:, g * M_STATE:(g + 1) * M_STATE],
                                 _NT, preferred_element_type=F32)
            sf = jnp.broadcast_to(seg[h:h + 1, :], (MQ, MQ))
            sb = jnp.broadcast_to(seg[4 + h:5 + h, :], (MQ, MQ))
            lf = jnp.where(lower, jnp.exp(jnp.minimum(sf.T - sf, 0.0)), 0.0)
            lb = jnp.where(upper, jnp.exp(jnp.minimum(sb.T - sb, 0.0)), 0.0)
            m = cb * (lf * dt[h:h + 1, :] + lb * dt[4 + h:5 + h, :])
            ydiag.append(_dot(m.astype(BF16), xb[:, h * HD:(h + 1) * HD]))
        y_s[pl.ds(r0, MQ), :] = jnp.concatenate(ydiag, axis=1)
        for d in range(2):
            bwt = jnp.concatenate([bt[(h // 2) * M_STATE:(h // 2 + 1) * M_STATE, :] * wgt[4 * d + h:4 * d + h + 1, :]
                                   for h in range(N_HEADS)], axis=0)
            u_s[d, c] = (_dot(bwt.astype(BF16), xb) * bdmask).astype(BF16)
            dec_s[d, c] = jnp.broadcast_to(_head_lane_vec(jnp.exp(tot), 4 * d), (V7X_SUBLANES, DG))
            tiles = []
            for g in range(2):
                wrows = jnp.where(upper_half, carry[4 * d + 2 * g + 1:4 * d + 2 * g + 2, :],
                                  carry[4 * d + 2 * g:4 * d + 2 * g + 1, :])
                tiles.append(c_dup[g] * wrows.T)
            cw_s[d, c] = jnp.concatenate(tiles, axis=1).astype(BF16)
        return 0

    lax.fori_loop(0, N_MCH, chunk, 0, unroll=4)

    st_s[...] = jnp.zeros(st_s.shape, F32)

    def scan(i, _):
        for d in range(2):
            c = i if d == 0 else N_MCH - 1 - i
            st = st_s[d]
            yo_s[d, pl.ds(pl.multiple_of(c * MQ, MQ), MQ), :] = _dot(cw_s[d, c], st.astype(BF16))
            st_s[d] = st * dec_s[d, c][0:1, :] + u_s[d, c].astype(F32)
        return 0

    lax.fori_loop(0, N_MCH, scan, 0, unroll=8)

    def finish(c, _):
        r0 = pl.multiple_of(c * CONV_ROWS, CONV_ROWS)
        rows = pl.ds(r0, CONV_ROWS)
        y = y_s[rows, :] + yo_s[0, rows, :] + yo_s[1, rows, :] + xs_s[rows, :] * dsk_ref[...]
        y = y * _silu(z_ref[0, rows, :].astype(F32))
        o_ref[0, pl.ds(r0, CONV_ROWS), :] = (_rms(y) * nw_ref[...]).astype(BF16)
        return 0

    lax.fori_loop(0, SEQ // CONV_ROWS, finish, 0, unroll=2)


def mamba2(z3, xbc3, dtc4, conv_w, conv_b, dt_bias_col, a_col, dskip_lanes, norm_w, tri_incl, bdmask):
    b = z3.shape[0]
    full = lambda a: pl.BlockSpec(a.shape, lambda i: (0,) * a.ndim)
    return pl.pallas_call(
        _mamba_body, grid=(b,),
        in_specs=[pl.BlockSpec((1, SEQ, DG), lambda i: (i, 0, 0)),
                  pl.BlockSpec((1, SEQ, 2 * DG), lambda i: (i, 0, 0)),
                  pl.BlockSpec((1, N_MCH, 8, MQ), lambda i: (i, 0, 0, 0)),
                  full(conv_w), full(conv_b), full(dt_bias_col), full(a_col), full(dskip_lanes), full(norm_w),
                  full(tri_incl), full(bdmask)],
        out_specs=pl.BlockSpec((1, SEQ, DG), lambda i: (i, 0, 0)),
        out_shape=jax.ShapeDtypeStruct((b, SEQ, DG), BF16),
        scratch_shapes=[pltpu.VMEM((SEQ + 2 * CONV_HALO, 2 * DG), F32),
                        pltpu.VMEM((SEQ, DG), F32),
                        pltpu.VMEM((SEQ, 2 * M_STATE), BF16),
                        pltpu.VMEM((SEQ, 2 * M_STATE), BF16),
                        pltpu.VMEM((SEQ, DG), F32),
                        pltpu.VMEM((2, N_MCH, DG, DG), BF16),
                        pltpu.VMEM((2, N_MCH, V7X_SUBLANES, DG), F32),
                        pltpu.VMEM((2, N_MCH, MQ, DG), BF16),
                        pltpu.VMEM((2, SEQ, DG), F32),
                        pltpu.VMEM((2, DG, DG), F32)],
        compiler_params=_params("parallel"), name="mamba2",
    )(z3, xbc3, dtc4, conv_w, conv_b, dt_bias_col, a_col, dskip_lanes, norm_w, tri_incl, bdmask)


A_TQ = 128
A_ROWS = 256
A_KW = A_TQ + 2 * A_BAND


def _attn_bias_body(ids_ref, rb_ref, o_ref):
    ids = ids_ref[0]
    for h in range(N_HEADS):
        acc = jnp.full(ids.shape, NEG_BIG, F32)
        for bkt in range(N_BUCKETS):
            acc = jnp.where(ids == bkt, rb_ref[bkt, h], acc)
        o_ref[h, 0] = acc


def attention_bias_table(bucket_ids, rel_bias):
    nvar, tq, w = bucket_ids.shape
    return pl.pallas_call(
        _attn_bias_body, grid=(nvar,),
        in_specs=[pl.BlockSpec((1, tq, w), lambda v: (v, 0, 0)),
                  pl.BlockSpec(memory_space=pltpu.SMEM)],
        out_specs=pl.BlockSpec((N_HEADS, 1, tq, w), lambda v: (0, v, 0, 0)),
        out_shape=jax.ShapeDtypeStruct((N_HEADS, nvar, tq, w), F32),
        compiler_params=_params("parallel"), name="attention_bias_table",
    )(bucket_ids, rel_bias)


A_SLABS = 3 * DG // V7X_LANES
A_QBLOCKS = SEQ // A_TQ


A_SUB4 = SEQ // 4
A_SUB16 = SEQ // 16


def _attn_body(at_ref, b1_ref, b4_ref, b16_ref, o_ref, qkv_s, x4_s, x16_s, y16_s, y4_s, part_o, part_l):
    def fill(c, _):
        r0 = pl.multiple_of(c * A_ROWS, A_ROWS)
        for s in range(A_SLABS):
            qkv_s[s, pl.ds(r0, A_ROWS), :] = at_ref[0, pl.ds(r0, A_ROWS), s * V7X_LANES:(s + 1) * V7X_LANES].astype(F32)
        return 0

    lax.fori_loop(0, SEQ // A_ROWS, fill, 0)

    def deinterleave(s, _):
        for r4 in range(4):
            for c in range(A_SUB4 // A_ROWS):
                x4_s[s, pl.ds(r4 * A_SUB4 + c * A_ROWS, A_ROWS), :] = \
                    qkv_s[s, pl.ds(r4 + 4 * c * A_ROWS, A_ROWS, stride=4), :]
        for r in range(16):
            x16_s[s, pl.ds(r * A_SUB16, A_SUB16), :] = \
                x4_s[s, pl.ds((r % 4) * A_SUB4 + r // 4, A_SUB16, stride=4), :].astype(BF16)
        return 0

    lax.fori_loop(0, A_SLABS, deinterleave, 0)
    first_head = lax.broadcasted_iota(I32, (A_TQ, V7X_LANES), 1) < HD

    def run_pattern(pat, dil, bias_ref):
        n = SEQ // dil if dil < 16 else SEQ
        nblk = n // A_TQ
        w = A_KW

        def block(it, _):
            r = it // nblk
            i = it - r * nblk
            q0 = i * A_TQ
            k0 = jnp.clip(q0 - A_BAND, 0, n - w)
            var = jnp.where(i == 0, 0, jnp.where(i == nblk - 1, 2, 1))
            for hp in range(2):
                lanes = [slice((2 * part + hp) * V7X_LANES, (2 * part + hp + 1) * V7X_LANES) for part in range(3)]
                if dil == 4:
                    qrows = pl.ds(r + dil * q0, A_TQ, stride=dil)
                    krows = pl.ds(r + dil * k0, w, stride=dil)
                    q2 = qkv_s[hp, qrows, :]
                    k2 = qkv_s[2 + hp, krows, :].astype(BF16)
                    v2 = qkv_s[4 + hp, krows, :].astype(BF16)
                else:
                    qrows = pl.ds(pl.multiple_of(q0, A_TQ), A_TQ)
                    krows = pl.ds(pl.multiple_of(k0, A_BAND), w)
                    if dil == 1:
                        q2, k2, v2 = at_ref[0, qrows, lanes[0]], at_ref[0, krows, lanes[1]], at_ref[0, krows, lanes[2]]
                    else:
                        q2, k2, v2 = x16_s[hp, qrows, :], x16_s[2 + hp, krows, :], x16_s[4 + hp, krows, :]
                q2 = (q2 * (HD ** -0.5)).astype(BF16)
                outs, lses = [], []
                for hh in range(2):
                    keep = first_head if hh == 0 else jnp.logical_not(first_head)
                    qm = jnp.where(keep, q2, jnp.zeros_like(q2))
                    s = lax.dot_general(qm, k2, _NT, preferred_element_type=F32) + bias_ref[2 * hp + hh, var]
                    m = jnp.max(s, axis=1, keepdims=True)
                    p = jnp.exp(s - m)
                    den = jnp.sum(p, axis=1, keepdims=True)
                    outs.append(_dot(p.astype(BF16), v2) / den)
                    lses.append(m + jnp.log(den))
                o_new = jnp.where(first_head, outs[0], outs[1])
                l_new = jnp.where(first_head, lses[0], lses[1])
                if dil == 16:
                    y16_s[0, hp, qrows, :] = o_new
                    y16_s[1, hp, qrows, :] = l_new
                else:
                    part_o[pat, hp, qrows, :] = o_new
                    part_l[pat, hp, qrows, :] = l_new
            return 0

        lax.fori_loop(0, A_QBLOCKS, block, 0, unroll=8)

    for pat, (dil, bias_ref) in enumerate(zip(A_DILS, (b1_ref, b4_ref, b16_ref))):
        run_pattern(pat, dil, bias_ref)

    for a, dst in enumerate((part_o, part_l)):
        for hp in range(2):
            for r in range(16):
                y4_s[a, hp, pl.ds((r % 4) * A_SUB4 + r // 4, A_SUB16, stride=4), :] = \
                    y16_s[a, hp, pl.ds(r * A_SUB16, A_SUB16), :]
            for r4 in range(4):
                for c in range(A_SUB4 // A_ROWS):
                    dst[2, hp, pl.ds(r4 + 4 * c * A_ROWS, A_ROWS, stride=4), :] = \
                        y4_s[a, hp, pl.ds(r4 * A_SUB4 + c * A_ROWS, A_ROWS), :]

    def finish(c, _):
        rows = pl.ds(pl.multiple_of(c * A_TQ, A_TQ), A_TQ)
        for hp in range(2):
            ls = [part_l[pat, hp, rows, :] for pat in range(len(A_DILS))]
            mx = jnp.maximum(jnp.maximum(ls[0], ls[1]), ls[2])
            ws = [jnp.exp(l - mx) for l in ls]
            num = ws[0] * part_o[0, hp, rows, :] + ws[1] * part_o[1, hp, rows, :] + ws[2] * part_o[2, hp, rows, :]
            o_ref[0, rows, hp * V7X_LANES:(hp + 1) * V7X_LANES] = (num / (ws[0] + ws[1] + ws[2])).astype(BF16)
        return 0

    lax.fori_loop(0, SEQ // A_TQ, finish, 0, unroll=2)


def dilated_attention(at3, bias1, bias4, bias16):
    b = at3.shape[0]
    full = lambda a: pl.BlockSpec(a.shape, lambda i: (0,) * a.ndim)
    return pl.pallas_call(
        _attn_body, grid=(b,),
        in_specs=[pl.BlockSpec((1, SEQ, 3 * DG), lambda i: (i, 0, 0)), full(bias1), full(bias4), full(bias16)],
        out_specs=pl.BlockSpec((1, SEQ, DG), lambda i: (i, 0, 0)),
        out_shape=jax.ShapeDtypeStruct((b, SEQ, DG), BF16),
        scratch_shapes=[pltpu.VMEM((A_SLABS, SEQ, V7X_LANES), F32),
                        pltpu.VMEM((A_SLABS, SEQ, V7X_LANES), F32),
                        pltpu.VMEM((A_SLABS, SEQ, V7X_LANES), BF16),
                        pltpu.VMEM((2, 2, SEQ, V7X_LANES), F32),
                        pltpu.VMEM((2, 2, SEQ, V7X_LANES), F32),
                        pltpu.VMEM((len(A_DILS), 2, SEQ, V7X_LANES), F32),
                        pltpu.VMEM((len(A_DILS), 2, SEQ, V7X_LANES), F32)],
        compiler_params=_params("parallel"), name="dilated_attention",
    )(at3, bias1, bias4, bias16)


H_BLK = 256
H_CPB = H_BLK // H_CHUNK
N_HBLK = SEQ // H_BLK
N_HCH = SEQ // H_CHUNK


def _chunk_bcast(x, row):
    c = x.shape[1]
    x3 = x.reshape(H_CPB, H_CHUNK, c)
    return jnp.broadcast_to(x3[:, row:row + 1, :], (H_CPB, H_CHUNK, c)).reshape(H_BLK, c)


def _hgrn_body(p_ref, lb_ref, nw_ref, tin_ref, o_ref, qm_s, ut_s, oi_s, dec_s, oe_s, st_s):
    li = lax.broadcasted_iota(I32, (H_BLK, H_BLK), 0)
    si = lax.broadcasted_iota(I32, (H_BLK, H_BLK), 1)
    same = (li // H_CHUNK) == (si // H_CHUNK)
    mask_f = same & (si <= li)
    mask_b = same & (si >= li)
    lane_head = lax.broadcasted_iota(I32, (1, DG), 1) // HD

    def block(bi, _):
        r0 = pl.multiple_of(bi * H_BLK, H_BLK)
        rows = pl.ds(r0, H_BLK)
        q = _silu(p_ref[0, rows, 0:DG].astype(F32))
        v = p_ref[0, rows, 3 * DG:4 * DG]
        scores = [None] * N_HEADS
        for d in range(2):
            fpre = p_ref[0, rows, (1 + d) * DG:(2 + d) * DG].astype(F32)
            lb = lb_ref[d:d + 1, :]
            sg = jax.nn.sigmoid(fpre)
            g = jnp.log(lb + (1.0 - lb) * sg)
            k = (1.0 - lb) * (1.0 - sg)
            gi = _dot01_2(tin_ref[...], g)
            glast = _chunk_bcast(gi, H_CHUNK - 1)
            if d == 0:
                gc = gi
                gref = _chunk_bcast(gi, H_CHUNK // 2 - 1)
                msk = mask_f
            else:
                gc = glast - gi + g
                gref = _chunk_bcast(gc, H_CHUNK // 2)
                msk = mask_b
            qe = (q * jnp.exp(gc - gref)).astype(BF16)
            ke = (k * jnp.exp(gref - gc)).astype(BF16)
            for h in range(N_HEADS):
                hs = slice(h * HD, (h + 1) * HD)
                sc = jnp.where(msk, lax.dot_general(qe[:, hs], ke[:, hs], _NT, preferred_element_type=F32), 0.0)
                scores[h] = sc if d == 0 else scores[h] + sc
            qd = q * jnp.exp(gc)
            kd = (k * jnp.exp(glast - gc)).astype(BF16)
            for j in range(H_CPB):
                c = bi * H_CPB + j
                cr = slice(j * H_CHUNK, (j + 1) * H_CHUNK)
                qm_s[d, c] = jnp.concatenate([jnp.where(lane_head == h, qd[cr, :], 0.0) for h in range(N_HEADS)],
                                             axis=0).astype(BF16)
                ut = lax.dot_general(v[cr, :], kd[cr, :], _TN, preferred_element_type=F32)
                packed = ut[0:HD, :]
                for h in range(1, N_HEADS):
                    packed = jnp.where(lane_head == h, ut[h * HD:(h + 1) * HD, :], packed)
                ut_s[d, c] = packed.astype(BF16)
                dec_s[d, c] = jnp.broadcast_to(jnp.exp(glast[j * H_CHUNK:j * H_CHUNK + 1, :]), (V7X_SUBLANES, DG))
        for h in range(N_HEADS):
            oi_s[h, rows, :] = _dot(scores[h].astype(BF16), v[:, h * HD:(h + 1) * HD])
        return 0

    lax.fori_loop(0, N_HBLK, block, 0)

    st_s[...] = jnp.zeros(st_s.shape, F32)

    def step(i, _):
        for d in range(2):
            c = i if d == 0 else N_HCH - 1 - i
            rows = pl.ds(pl.multiple_of(c * H_CHUNK, H_CHUNK), H_CHUNK)
            st = st_s[d]
            inter = lax.dot_general(qm_s[d, c], st.astype(BF16), _NT, preferred_element_type=F32)
            for h in range(N_HEADS):
                oe_s[d, h, rows, :] = inter[h * H_CHUNK:(h + 1) * H_CHUNK, :]
            st_s[d] = st * dec_s[d, c][0:1, :] + ut_s[d, c].astype(F32)
        return 0

    lax.fori_loop(0, N_HCH, step, 0, unroll=16)

    def finish(c, _):
        r0 = pl.multiple_of(c * CONV_ROWS, CONV_ROWS)
        rows = pl.ds(r0, CONV_ROWS)
        gate = _silu(p_ref[0, rows, 4 * DG:5 * DG].astype(F32))
        outs = [_rms(oi_s[h, rows, :] + oe_s[0, h, rows, :] + oe_s[1, h, rows, :]) for h in range(N_HEADS)]
        o_ref[0, rows, :] = (jnp.concatenate(outs, axis=1) * nw_ref[...] * gate).astype(BF16)
        return 0

    lax.fori_loop(0, SEQ // CONV_ROWS, finish, 0, unroll=2)


def hgrn2(p3, lb2, norm_w_lanes, tri_in_chunk):
    b = p3.shape[0]
    full = lambda a: pl.BlockSpec(a.shape, lambda i: (0,) * a.ndim)
    return pl.pallas_call(
        _hgrn_body, grid=(b,),
        in_specs=[pl.BlockSpec((1, SEQ, 5 * DG), lambda i: (i, 0, 0)), full(lb2), full(norm_w_lanes),
                  full(tri_in_chunk)],
        out_specs=pl.BlockSpec((1, SEQ, DG), lambda i: (i, 0, 0)),
        out_shape=jax.ShapeDtypeStruct((b, SEQ, DG), BF16),
        scratch_shapes=[pltpu.VMEM((2, N_HCH, N_HEADS * H_CHUNK, DG), BF16),
                        pltpu.VMEM((2, N_HCH, HD, DG), BF16),
                        pltpu.VMEM((N_HEADS, SEQ, HD), F32),
                        pltpu.VMEM((2, N_HCH, V7X_SUBLANES, DG), F32),
                        pltpu.VMEM((2, N_HEADS, SEQ, HD), F32),
                        pltpu.VMEM((2, HD, DG), F32)],
        compiler_params=_params("parallel"), name="hgrn2",
    )(p3, lb2, norm_w_lanes, tri_in_chunk)


@functools.lru_cache(maxsize=None)
def _tables():
    t = {}
    k = np.arange(SEQ, dtype=np.int64)
    ang = 2.0 * np.pi * ((k[:, None] * k[None, :]) % NFFT).astype(np.float64) / NFFT
    t["cos"] = np.cos(ang).astype(np.float32)
    t["sin"] = np.sin(ang).astype(np.float32)
    rows_f32 = np.concatenate([t["cos"], t["sin"]], axis=0)
    t["dft_rows"] = rows_f32.astype(ml_dtypes.bfloat16)
    t["dft_rows_lo"] = (rows_f32 - t["dft_rows"].astype(np.float32)).astype(ml_dtypes.bfloat16)
    t["dft_cols"] = np.concatenate([t["cos"], t["sin"]], axis=1).astype(ml_dtypes.bfloat16)
    tt = np.linspace(0.0, 1.0, SEQ, dtype=np.float32)[:, None]
    bands = (HY_POS_DIM - 1) // 2
    ang_pos = (2.0 * math.pi * np.arange(SEQ, dtype=np.float32) / SEQ).astype(np.float32)
    f = np.linspace(1e-4, bands - 1, bands, dtype=np.float32)
    a2 = (ang_pos[:, None] * f[None, :]).astype(np.float32)
    z = np.concatenate([tt, np.cos(a2), -np.sin(a2)], axis=-1).astype(np.float32)
    zp = np.zeros((SEQ, V7X_LANES), np.float32)
    zp[:, :HY_POS_DIM] = z
    t["zpos"] = zp
    max_decay = math.log(1e-2) / 0.3
    min_decay = math.log(1e-2) / 1.5
    deltas = np.abs(np.linspace(min_decay, max_decay, DG, dtype=np.float32))
    t["decay"] = np.exp(-tt * deltas[None, :]).astype(np.float32)
    i128 = np.arange(V7X_LANES)
    t["u128"] = (i128[:, None] < i128[None, :]).astype(np.float32)
    im = np.arange(M_CHUNK)
    t["tri_incl"] = (im[:, None] <= im[None, :]).astype(np.float32)
    ib = np.arange(H_BLK)
    t["tri_in_chunk"] = ((ib[:, None] // H_CHUNK == ib[None, :] // H_CHUNK)
                         & (ib[None, :] <= ib[:, None])).astype(np.float32)
    idg = np.arange(DG)
    t["bdmask"] = (idg[:, None] // HD == idg[None, :] // HD).astype(np.float32)
    def bucket(rel):
        nb = N_BUCKETS // 2
        max_exact = nb // 2
        ret = (rel > 0).astype(np.int64) * nb
        n = np.abs(rel)
        nf = np.maximum(n, 1).astype(np.float64)
        large = max_exact + (np.log(nf / max_exact) / math.log(MAX_DISTANCE / max_exact)
                             * (nb - max_exact)).astype(np.int64)
        large = np.minimum(large, nb - 1)
        return ret + np.where(n < max_exact, n, large)

    for dil in A_DILS:
        n = SEQ // dil
        qi = np.arange(A_TQ)[:, None]
        kj = np.arange(A_KW)[None, :]
        ids = []
        for s0 in (0, -A_BAND, -(A_KW - A_TQ)):
            kk = kj + s0
            rel = kk - qi
            ok = np.abs(rel) <= A_BAND
            if n == A_TQ:
                ok &= (kk >= 0) & (kk < A_TQ)
            ids.append(np.where(ok, bucket(rel * dil), -1))
        t[f"bucket{dil}"] = np.stack(ids).astype(np.int32)
    return t


def kernel(x, w_in, w_out, norm_mix_w, norm_ffn_w, hy_conv_w, hy_pos_w1, hy_pos_b1, hy_pos_w2, hy_pos_b2,
           hy_sin_freq, hy_pos_w3, hy_filt_bias, m_conv_w, m_conv_b, m_dt_bias, m_A_log, m_D, m_norm_w, rel_bias,
           hg_lb, hg_norm_w, router_w, moe_w_gate, moe_w_up, moe_w_down, final_norm_w):
    b = x.shape[0]
    assert x.shape[1:] == (SEQ, D_MODEL) and b % MOE_FFN_SEQS == 0 and b % ROUTER_SEQS == 0, x.shape
    t = b * SEQ
    tb = _tables()
    dft_rows_lo = jnp.asarray(tb["dft_rows_lo"])
    dft_rows = jnp.asarray(tb["dft_rows"])
    dft_cols = jnp.asarray(tb["dft_cols"])
    u128 = jnp.asarray(tb["u128"]).astype(BF16)
    tri_incl = jnp.asarray(tb["tri_incl"]).astype(BF16)
    tri_in_chunk = jnp.asarray(tb["tri_in_chunk"]).astype(BF16)
    bdmask = jnp.asarray(tb["bdmask"])
    attn_bias = [attention_bias_table(jnp.asarray(tb[f"bucket{d}"]), rel_bias.astype(F32)) for d in A_DILS]

    sm = jax.nn.softmax(hg_lb.astype(F32), axis=0)
    lower_bounds = jnp.cumsum(sm, axis=0) - sm[:1]

    xa = x.reshape(t, D_MODEL)
    for l in range(DEPTH):
        wl = w_in[l]
        w_main = jnp.concatenate([wl[:, 0:768], wl[:, 768:1024], wl[:, 1024:1536], wl[:, 1544:2312],
                                  wl[:, 2312:3592]], axis=1).astype(BF16)
        w_dt_rows = wl[:, 1536:1544].T.astype(BF16)
        z, x0, mz, mx, at, hg, dtc = in_projection(xa, norm_mix_w[l][None, :], w_main, w_dt_rows, hy_conv_w[l])

        w1p = jnp.zeros((V7X_LANES, HY_HID), F32).at[:HY_POS_DIM].set(hy_pos_w1[l])
        kr, ki, kny = hyena_filter_spectrum(
            jnp.asarray(tb["zpos"]), w1p, hy_pos_b1[l][None, :], hy_pos_w2[l], hy_pos_b2[l][None, :],
            hy_sin_freq[l][None, :], hy_pos_w3[l], jnp.asarray(tb["decay"]), dft_rows, dft_rows_lo)
        z3, x03 = z.reshape(b, SEQ, DG), x0.reshape(b, SEQ, DG)
        ya = hyena_conv(z3, x03, dft_rows, dft_cols, kr, ki, kny, hy_filt_bias[l][None, :]).reshape(t, DG)

        a_col = (-jnp.exp(m_A_log[l].astype(F32))).reshape(8, 1)
        yb = mamba2(mz.reshape(b, SEQ, DG), mx.reshape(b, SEQ, 2 * DG), dtc.reshape(b, N_MCH, 8, MQ),
                    m_conv_w[l], m_conv_b[l][None, :], m_dt_bias[l].reshape(8, 1), a_col,
                    jnp.repeat(m_D[l].astype(F32), HD)[None, :], m_norm_w[l][None, :], tri_incl, bdmask).reshape(t, DG)

        yc = dilated_attention(at.reshape(b, SEQ, 3 * DG), *attn_bias).reshape(t, DG)

        lbl = lower_bounds[l]
        yd = hgrn2(hg.reshape(b, SEQ, 5 * DG), lbl, jnp.tile(hg_norm_w[l], N_HEADS)[None, :],
                   tri_in_chunk).reshape(t, DG)

        rw_rows = router_w[l].T.astype(F32)
        rw_hi = rw_rows.astype(BF16)
        rw_lo = (rw_rows - rw_hi.astype(F32)).astype(BF16)
        xo, xn, logits = out_projection(xa, ya, yb, yc, yd, w_out[l].reshape(4, DG, D_MODEL).astype(BF16),
                                        norm_ffn_w[l][None, :], rw_hi, rw_lo)
        xn3 = xn.reshape(b, SEQ, D_MODEL)
        rank, gate, seg = router(logits, u128)
        seg_flat = seg[:, :, :MOE_SEG_STRIDE].reshape(-1)
        xe = moe_gather(seg_flat, xn3, rank)
        ye = moe_experts(xe, moe_w_gate, moe_w_up, moe_w_down, l)
        xa = moe_scatter(seg_flat, ye, rank, gate, xo.reshape(b, SEQ, D_MODEL), final_norm_w[None, :],
                         final=(l == DEPTH - 1)).reshape(t, D_MODEL)
    return xa.reshape(b, SEQ, D_MODEL)
```
